```python
import math
import jax, jax.numpy as jnp
from jax import lax
import numpy as np

D_MODEL = 1024
BATCH = 16
SEQ = 2048
DEPTH = 1

CHUNK = 64
MIX_WIDTH = D_MODEL
SB_WIDTH = MIX_WIDTH // 2
SB_HEAD_DIM = 64
SB_HEADS = SB_WIDTH // SB_HEAD_DIM
SSM_WIDTH = MIX_WIDTH - SB_WIDTH
SSM_GROUP = 16
SSM_GROUPS = SSM_WIDTH // SSM_GROUP
SSM_STATE = 64
D_FF = 4 * D_MODEL
QBLOCK = 128
EPS = 1e-6
DT_MIN = 1e-3
DT_MAX = 1e-1

kernel_name = "hybrid_stickbreaking_s5_block"


def rmsnorm(x, g):
    xf = x.astype(jnp.float32)
    y = xf * lax.rsqrt(jnp.mean(xf * xf, axis=-1, keepdims=True) + EPS)
    return y * g.astype(jnp.float32)


def stick_breaking_attention(q, k, v):
    L = q.shape[2]
    scale = 1.0 / math.sqrt(q.shape[-1])
    outs = []
    for i in range(L // QBLOCK):
        q0 = i * QBLOCK
        kend = q0 + QBLOCK
        qb = q[:, :, q0:kend]
        kb = k[:, :, :kend]
        vb = v[:, :, :kend]
        z = jnp.einsum('bhqd,bhkd->bhqk', qb, kb) * scale
        t_idx = q0 + jnp.arange(QBLOCK)[:, None]
        s_idx = jnp.arange(kend)[None, :]
        mask = s_idx < t_idx
        log_one_minus = jnp.where(mask, -jax.nn.softplus(z), 0.0)
        tail = lax.cumsum(log_one_minus, axis=3, reverse=True) - log_one_minus
        log_a = jax.nn.log_sigmoid(z) + tail
        a = jnp.where(mask, jnp.exp(log_a), 0.0)
        outs.append(jnp.einsum('bhqk,bhkd->bhqd', a, vb))
    return jnp.concatenate(outs, axis=2)


def s5_glu(u, lam_re, lam_im, log_dt, b_re, b_im, c_re, c_im, d_skip, w_glu, b_glu):
    Bsz, L, _ = u.shape
    ug = u.reshape(Bsz, L, SSM_GROUPS, SSM_GROUP)
    lam = lax.complex(lam_re.astype(jnp.float32), lam_im.astype(jnp.float32))
    dt = jnp.exp(log_dt.astype(jnp.float32))[:, None]
    lam_bar = jnp.exp(lam * dt)
    b_mat = lax.complex(b_re.astype(jnp.float32), b_im.astype(jnp.float32))
    c_mat = lax.complex(c_re.astype(jnp.float32), c_im.astype(jnp.float32))
    b_bar = ((lam_bar - 1.0) / lam)[:, :, None] * b_mat
    bu = jnp.einsum('blgh,gph->blgp', ug.astype(jnp.complex64), b_bar)
    lam_seq = jnp.broadcast_to(lam_bar[None, None], (1, L, SSM_GROUPS, SSM_STATE))

    def combine(e_i, e_j):
        a_i, s_i = e_i
        a_j, s_j = e_j
        return a_j * a_i, a_j * s_i + s_j

    _, states = lax.associative_scan(combine, (lam_seq, bu), axis=1)
    y = jnp.einsum('blgp,ghp->blgh', states, c_mat).real + d_skip.astype(jnp.float32)[None, None] * ug
    y = jax.nn.gelu(y.reshape(Bsz, L, SSM_WIDTH))
    gate = jax.nn.sigmoid(y @ w_glu.astype(jnp.float32) + b_glu.astype(jnp.float32))
    return y * gate


def _fwd_setup_inputs(seed: int = 0) -> dict:
    key = jax.random.key(seed)
    ks = jax.random.split(key, 24)
    f32 = jnp.float32
    G, P, H = SSM_GROUPS, SSM_STATE, SSM_GROUP
    x = jax.random.normal(ks[0], (BATCH, SEQ, D_MODEL), f32)
    norm1_g = 1.0 + 0.02 * jax.random.normal(ks[1], (D_MODEL,), f32)
    w_in = jax.random.normal(ks[2], (D_MODEL, 3 * SB_WIDTH + SSM_WIDTH), f32) * D_MODEL ** -0.5
    q_norm_g = 1.0 + 0.02 * jax.random.normal(ks[3], (SB_HEAD_DIM,), f32)
    k_norm_g = 1.0 + 0.02 * jax.random.normal(ks[4], (SB_HEAD_DIM,), f32)
    ssm_lambda_re = -0.5 + 0.01 * jax.random.normal(ks[5], (G, P), f32)
    ssm_lambda_im = math.pi * jnp.broadcast_to(jnp.arange(P, dtype=f32)[None], (G, P)) \
        + 0.01 * jax.random.normal(ks[6], (G, P), f32)
    ssm_log_dt = jax.random.uniform(ks[7], (G,), f32, math.log(DT_MIN), math.log(DT_MAX))
    ssm_b_re = jax.random.normal(ks[8], (G, P, H), f32) * (2.0 * H) ** -0.5
    ssm_b_im = jax.random.normal(ks[9], (G, P, H), f32) * (2.0 * H) ** -0.5
    ssm_c_re = jax.random.normal(ks[10], (G, H, P), f32) * (2.0 * P) ** -0.5
    ssm_c_im = jax.random.normal(ks[11], (G, H, P), f32) * (2.0 * P) ** -0.5
    ssm_d = jax.random.normal(ks[12], (G, H), f32)
    w_glu = jax.random.normal(ks[13], (SSM_WIDTH, SSM_WIDTH), f32) * SSM_WIDTH ** -0.5
    b_glu = 0.01 * jax.random.normal(ks[14], (SSM_WIDTH,), f32)
    attn_out_g = 1.0 + 0.02 * jax.random.normal(ks[15], (SB_WIDTH,), f32)
    ssm_out_g = 1.0 + 0.02 * jax.random.normal(ks[16], (SSM_WIDTH,), f32)
    w_out = jax.random.normal(ks[17], (MIX_WIDTH, D_MODEL), f32) * MIX_WIDTH ** -0.5
    norm2_g = 1.0 + 0.02 * jax.random.normal(ks[18], (D_MODEL,), f32)
    w_mlp_in = jax.random.normal(ks[19], (D_MODEL, D_FF), f32) * D_MODEL ** -0.5
    w_mlp_out = jax.random.normal(ks[20], (D_FF, D_MODEL), f32) * D_FF ** -0.5
    return {"x": x, "norm1_g": norm1_g, "w_in": w_in, "q_norm_g": q_norm_g, "k_norm_g": k_norm_g,
            "ssm_lambda_re": ssm_lambda_re, "ssm_lambda_im": ssm_lambda_im, "ssm_log_dt": ssm_log_dt,
            "ssm_b_re": ssm_b_re, "ssm_b_im": ssm_b_im, "ssm_c_re": ssm_c_re, "ssm_c_im": ssm_c_im,
            "ssm_d": ssm_d, "w_glu": w_glu, "b_glu": b_glu, "attn_out_g": attn_out_g,
            "ssm_out_g": ssm_out_g, "w_out": w_out, "norm2_g": norm2_g,
            "w_mlp_in": w_mlp_in, "w_mlp_out": w_mlp_out}


def _fwd_reference(x, norm1_g, w_in, q_norm_g, k_norm_g, ssm_lambda_re, ssm_lambda_im, ssm_log_dt,
              ssm_b_re, ssm_b_im, ssm_c_re, ssm_c_im, ssm_d, w_glu, b_glu, attn_out_g,
              ssm_out_g, w_out, norm2_g, w_mlp_in, w_mlp_out):
    Bsz, L, _ = x.shape
    h = x.astype(jnp.float32)
    for _layer in range(DEPTH):
        xn = rmsnorm(h, norm1_g)
        proj = xn @ w_in.astype(jnp.float32)
        q, k, v, u = jnp.split(proj, [SB_WIDTH, 2 * SB_WIDTH, 3 * SB_WIDTH], axis=-1)

        def heads(t):
            return t.reshape(Bsz, L, SB_HEADS, SB_HEAD_DIM).transpose(0, 2, 1, 3)

        qh = rmsnorm(heads(q), q_norm_g)
        kh = rmsnorm(heads(k), k_norm_g)
        vh = heads(v)
        sb = stick_breaking_attention(qh, kh, vh)
        sb = sb.transpose(0, 2, 1, 3).reshape(Bsz, L, SB_WIDTH)

        ssm = s5_glu(u, ssm_lambda_re, ssm_lambda_im, ssm_log_dt, ssm_b_re, ssm_b_im,
                     ssm_c_re, ssm_c_im, ssm_d, w_glu, b_glu)

        mixed = jnp.concatenate([rmsnorm(sb, attn_out_g), rmsnorm(ssm, ssm_out_g)], axis=-1)
        h = h + mixed @ w_out.astype(jnp.float32)

        hn = rmsnorm(h, norm2_g)
        a = jnp.square(jax.nn.relu(hn @ w_mlp_in.astype(jnp.float32)))
        h = h + a @ w_mlp_out.astype(jnp.float32)
    return h.astype(x.dtype)


import jax as _jax
import jax.numpy as _jnp

TWIN_FORMAT = 'train_step'
FWD_PARAMS = ['x', 'norm1_g', 'w_in', 'q_norm_g', 'k_norm_g', 'ssm_lambda_re', 'ssm_lambda_im', 'ssm_log_dt', 'ssm_b_re', 'ssm_b_im', 'ssm_c_re', 'ssm_c_im', 'ssm_d', 'w_glu', 'b_glu', 'attn_out_g', 'ssm_out_g', 'w_out', 'norm2_g', 'w_mlp_in', 'w_mlp_out']
TWIN_WEIGHTS = ['norm1_g', 'w_in', 'q_norm_g', 'k_norm_g', 'ssm_lambda_re', 'ssm_lambda_im', 'ssm_log_dt', 'ssm_b_re', 'ssm_b_im', 'ssm_c_re', 'ssm_c_im', 'ssm_d', 'w_glu', 'b_glu', 'attn_out_g', 'ssm_out_g', 'w_out', 'norm2_g', 'w_mlp_in', 'w_mlp_out']
TWIN_DIFF_INPUT = 'x'
TWIN_INPUTS = ['x', 'norm1_g', 'w_in', 'q_norm_g', 'k_norm_g', 'ssm_lambda_re', 'ssm_lambda_im', 'ssm_log_dt', 'ssm_b_re', 'ssm_b_im', 'ssm_c_re', 'ssm_c_im', 'ssm_d', 'w_glu', 'b_glu', 'attn_out_g', 'ssm_out_g', 'w_out', 'norm2_g', 'w_mlp_in', 'w_mlp_out', 'loss_target', 'm_norm1_g', 'm_w_in', 'm_q_norm_g', 'm_k_norm_g', 'm_ssm_lambda_re', 'm_ssm_lambda_im', 'm_ssm_log_dt', 'm_ssm_b_re', 'm_ssm_b_im', 'm_ssm_c_re', 'm_ssm_c_im', 'm_ssm_d', 'm_w_glu', 'm_b_glu', 'm_attn_out_g', 'm_ssm_out_g', 'm_w_out', 'm_norm2_g', 'm_w_mlp_in', 'm_w_mlp_out', 'v_norm1_g', 'v_w_in', 'v_q_norm_g', 'v_k_norm_g', 'v_ssm_lambda_re', 'v_ssm_lambda_im', 'v_ssm_log_dt', 'v_ssm_b_re', 'v_ssm_b_im', 'v_ssm_c_re', 'v_ssm_c_im', 'v_ssm_d', 'v_w_glu', 'v_b_glu', 'v_attn_out_g', 'v_ssm_out_g', 'v_w_out', 'v_norm2_g', 'v_w_mlp_in', 'v_w_mlp_out']
TWIN_OUTPUTS = ['loss', 'grad_x', 'grad_norm1_g', 'grad_w_in', 'grad_q_norm_g', 'grad_k_norm_g', 'grad_ssm_lambda_re', 'grad_ssm_lambda_im', 'grad_ssm_log_dt', 'grad_ssm_b_re', 'grad_ssm_b_im', 'grad_ssm_c_re', 'grad_ssm_c_im', 'grad_ssm_d', 'grad_w_glu', 'grad_b_glu', 'grad_attn_out_g', 'grad_ssm_out_g', 'grad_w_out', 'grad_norm2_g', 'grad_w_mlp_in', 'grad_w_mlp_out', 'delta_norm1_g', 'delta_w_in', 'delta_q_norm_g', 'delta_k_norm_g', 'delta_ssm_lambda_re', 'delta_ssm_lambda_im', 'delta_ssm_log_dt', 'delta_ssm_b_re', 'delta_ssm_b_im', 'delta_ssm_c_re', 'delta_ssm_c_im', 'delta_ssm_d', 'delta_w_glu', 'delta_b_glu', 'delta_attn_out_g', 'delta_ssm_out_g', 'delta_w_out', 'delta_norm2_g', 'delta_w_mlp_in', 'delta_w_mlp_out', 'new_m_norm1_g', 'new_m_w_in', 'new_m_q_norm_g', 'new_m_k_norm_g', 'new_m_ssm_lambda_re', 'new_m_ssm_lambda_im', 'new_m_ssm_log_dt', 'new_m_ssm_b_re', 'new_m_ssm_b_im', 'new_m_ssm_c_re', 'new_m_ssm_c_im', 'new_m_ssm_d', 'new_m_w_glu', 'new_m_b_glu', 'new_m_attn_out_g', 'new_m_ssm_out_g', 'new_m_w_out', 'new_m_norm2_g', 'new_m_w_mlp_in', 'new_m_w_mlp_out', 'new_v_norm1_g', 'new_v_w_in', 'new_v_q_norm_g', 'new_v_k_norm_g', 'new_v_ssm_lambda_re', 'new_v_ssm_lambda_im', 'new_v_ssm_log_dt', 'new_v_ssm_b_re', 'new_v_ssm_b_im', 'new_v_ssm_c_re', 'new_v_ssm_c_im', 'new_v_ssm_d', 'new_v_w_glu', 'new_v_b_glu', 'new_v_attn_out_g', 'new_v_ssm_out_g', 'new_v_w_out', 'new_v_norm2_g', 'new_v_w_mlp_in', 'new_v_w_mlp_out']
TWIN_LEAF_KINDS = {'loss': 'loss', 'grad_x': 'grad_x', 'grad_norm1_g': 'grad_w', 'grad_w_in': 'grad_w', 'grad_q_norm_g': 'grad_w', 'grad_k_norm_g': 'grad_w', 'grad_ssm_lambda_re': 'grad_w', 'grad_ssm_lambda_im': 'grad_w', 'grad_ssm_log_dt': 'grad_w', 'grad_ssm_b_re': 'grad_w', 'grad_ssm_b_im': 'grad_w', 'grad_ssm_c_re': 'grad_w', 'grad_ssm_c_im': 'grad_w', 'grad_ssm_d': 'grad_w', 'grad_w_glu': 'grad_w', 'grad_b_glu': 'grad_w', 'grad_attn_out_g': 'grad_w', 'grad_ssm_out_g': 'grad_w', 'grad_w_out': 'grad_w', 'grad_norm2_g': 'grad_w', 'grad_w_mlp_in': 'grad_w', 'grad_w_mlp_out': 'grad_w', 'delta_norm1_g': 'delta_w', 'delta_w_in': 'delta_w', 'delta_q_norm_g': 'delta_w', 'delta_k_norm_g': 'delta_w', 'delta_ssm_lambda_re': 'delta_w', 'delta_ssm_lambda_im': 'delta_w', 'delta_ssm_log_dt': 'delta_w', 'delta_ssm_b_re': 'delta_w', 'delta_ssm_b_im': 'delta_w', 'delta_ssm_c_re': 'delta_w', 'delta_ssm_c_im': 'delta_w', 'delta_ssm_d': 'delta_w', 'delta_w_glu': 'delta_w', 'delta_b_glu': 'delta_w', 'delta_attn_out_g': 'delta_w', 'delta_ssm_out_g': 'delta_w', 'delta_w_out': 'delta_w', 'delta_norm2_g': 'delta_w', 'delta_w_mlp_in': 'delta_w', 'delta_w_mlp_out': 'delta_w', 'new_m_norm1_g': 'new_m', 'new_m_w_in': 'new_m', 'new_m_q_norm_g': 'new_m', 'new_m_k_norm_g': 'new_m', 'new_m_ssm_lambda_re': 'new_m', 'new_m_ssm_lambda_im': 'new_m', 'new_m_ssm_log_dt': 'new_m', 'new_m_ssm_b_re': 'new_m', 'new_m_ssm_b_im': 'new_m', 'new_m_ssm_c_re': 'new_m', 'new_m_ssm_c_im': 'new_m', 'new_m_ssm_d': 'new_m', 'new_m_w_glu': 'new_m', 'new_m_b_glu': 'new_m', 'new_m_attn_out_g': 'new_m', 'new_m_ssm_out_g': 'new_m', 'new_m_w_out': 'new_m', 'new_m_norm2_g': 'new_m', 'new_m_w_mlp_in': 'new_m', 'new_m_w_mlp_out': 'new_m', 'new_v_norm1_g': 'new_v', 'new_v_w_in': 'new_v', 'new_v_q_norm_g': 'new_v', 'new_v_k_norm_g': 'new_v', 'new_v_ssm_lambda_re': 'new_v', 'new_v_ssm_lambda_im': 'new_v', 'new_v_ssm_log_dt': 'new_v', 'new_v_ssm_b_re': 'new_v', 'new_v_ssm_b_im': 'new_v', 'new_v_ssm_c_re': 'new_v', 'new_v_ssm_c_im': 'new_v', 'new_v_ssm_d': 'new_v', 'new_v_w_glu': 'new_v', 'new_v_b_glu': 'new_v', 'new_v_attn_out_g': 'new_v', 'new_v_ssm_out_g': 'new_v', 'new_v_w_out': 'new_v', 'new_v_norm2_g': 'new_v', 'new_v_w_mlp_in': 'new_v', 'new_v_w_mlp_out': 'new_v'}


def _forward(args):
    return _fwd_reference(*[args[k] for k in FWD_PARAMS])


def _output_shape():
    out = _jax.eval_shape(lambda: _forward(_fwd_setup_inputs(0)))
    return out.shape, out.dtype

N_MICROBATCH = 1
ADAM_LR = 0.001
ADAM_B1 = 0.9
ADAM_B2 = 0.999
ADAM_EPS = 1e-08
ADAM_WD = 0.01
ADAM_STEP = 10
PER_EXAMPLE_BATCH_AXIS = {'x': 0, 'loss_target': 0}
SHARED_INPUTS = []
_WEIGHT_DTYPES = {'norm1_g': _jnp.float32, 'w_in': _jnp.float32, 'q_norm_g': _jnp.float32, 'k_norm_g': _jnp.float32, 'ssm_lambda_re': _jnp.float32, 'ssm_lambda_im': _jnp.float32, 'ssm_log_dt': _jnp.float32, 'ssm_b_re': _jnp.float32, 'ssm_b_im': _jnp.float32, 'ssm_c_re': _jnp.float32, 'ssm_c_im': _jnp.float32, 'ssm_d': _jnp.float32, 'w_glu': _jnp.float32, 'b_glu': _jnp.float32, 'attn_out_g': _jnp.float32, 'ssm_out_g': _jnp.float32, 'w_out': _jnp.float32, 'norm2_g': _jnp.float32, 'w_mlp_in': _jnp.float32, 'w_mlp_out': _jnp.float32}
MOMENT_SCALE = {'norm1_g': 1.238432e+00, 'w_in': 7.181023e-01, 'q_norm_g': 1.096936e+00, 'k_norm_g': 1.110030e+00, 'ssm_lambda_re': 5.581475e-02, 'ssm_lambda_im': 3.736492e-02, 'ssm_log_dt': 2.972197e+01, 'ssm_b_re': 2.934372e-02, 'ssm_b_im': 3.184865e-02, 'ssm_c_re': 5.700525e-02, 'ssm_c_im': 5.751771e-02, 'ssm_d': 1.581672e+01, 'w_glu': 1.848743e+00, 'b_glu': 6.058898e+00, 'attn_out_g': 3.152113e+01, 'ssm_out_g': 6.201863e+01, 'w_out': 9.750097e+00, 'norm2_g': 9.642284e+01, 'w_mlp_in': 3.780511e+00, 'w_mlp_out': 1.081172e+01}


def _to_microbatches(a, axis):
    t = _jnp.moveaxis(a, axis, 0)
    t = t.reshape((N_MICROBATCH, t.shape[0] // N_MICROBATCH) + t.shape[1:])
    return _jnp.moveaxis(t, 1, axis + 1)


def setup_inputs(seed: int = 0) -> dict:
    inp = _fwd_setup_inputs(seed)
    key = _jax.random.fold_in(_jax.random.key(seed), 7919)
    shape, _ = _output_shape()
    out = dict(inp)
    out["loss_target"] = _jax.random.normal(_jax.random.fold_in(key, 0), shape, _jnp.float32)
    for i, name in enumerate(TWIN_WEIGHTS):
        w = inp[name].astype(_jnp.float32)
        if MOMENT_SCALE is None:
            s = _jnp.sqrt(_jnp.mean(_jnp.square(w)) + 1e-30)
        else:
            s = MOMENT_SCALE[name]
        km, kv = _jax.random.split(_jax.random.fold_in(key, i + 1))
        out[name] = w
        out["m_" + name] = s * _jax.random.normal(km, w.shape, _jnp.float32)
        out["v_" + name] = (s * s) * _jax.random.uniform(kv, w.shape, _jnp.float32, 0.5, 1.5)
    if N_MICROBATCH > 1:
        for name, axis in PER_EXAMPLE_BATCH_AXIS.items():
            out[name] = _to_microbatches(out[name], axis)
    return {'x': out['x'], 'norm1_g': out['norm1_g'], 'w_in': out['w_in'], 'q_norm_g': out['q_norm_g'], 'k_norm_g': out['k_norm_g'], 'ssm_lambda_re': out['ssm_lambda_re'], 'ssm_lambda_im': out['ssm_lambda_im'], 'ssm_log_dt': out['ssm_log_dt'], 'ssm_b_re': out['ssm_b_re'], 'ssm_b_im': out['ssm_b_im'], 'ssm_c_re': out['ssm_c_re'], 'ssm_c_im': out['ssm_c_im'], 'ssm_d': out['ssm_d'], 'w_glu': out['w_glu'], 'b_glu': out['b_glu'], 'attn_out_g': out['attn_out_g'], 'ssm_out_g': out['ssm_out_g'], 'w_out': out['w_out'], 'norm2_g': out['norm2_g'], 'w_mlp_in': out['w_mlp_in'], 'w_mlp_out': out['w_mlp_out'], 'loss_target': out['loss_target'], 'm_norm1_g': out['m_norm1_g'], 'm_w_in': out['m_w_in'], 'm_q_norm_g': out['m_q_norm_g'], 'm_k_norm_g': out['m_k_norm_g'], 'm_ssm_lambda_re': out['m_ssm_lambda_re'], 'm_ssm_lambda_im': out['m_ssm_lambda_im'], 'm_ssm_log_dt': out['m_ssm_log_dt'], 'm_ssm_b_re': out['m_ssm_b_re'], 'm_ssm_b_im': out['m_ssm_b_im'], 'm_ssm_c_re': out['m_ssm_c_re'], 'm_ssm_c_im': out['m_ssm_c_im'], 'm_ssm_d': out['m_ssm_d'], 'm_w_glu': out['m_w_glu'], 'm_b_glu': out['m_b_glu'], 'm_attn_out_g': out['m_attn_out_g'], 'm_ssm_out_g': out['m_ssm_out_g'], 'm_w_out': out['m_w_out'], 'm_norm2_g': out['m_norm2_g'], 'm_w_mlp_in': out['m_w_mlp_in'], 'm_w_mlp_out': out['m_w_mlp_out'], 'v_norm1_g': out['v_norm1_g'], 'v_w_in': out['v_w_in'], 'v_q_norm_g': out['v_q_norm_g'], 'v_k_norm_g': out['v_k_norm_g'], 'v_ssm_lambda_re': out['v_ssm_lambda_re'], 'v_ssm_lambda_im': out['v_ssm_lambda_im'], 'v_ssm_log_dt': out['v_ssm_log_dt'], 'v_ssm_b_re': out['v_ssm_b_re'], 'v_ssm_b_im': out['v_ssm_b_im'], 'v_ssm_c_re': out['v_ssm_c_re'], 'v_ssm_c_im': out['v_ssm_c_im'], 'v_ssm_d': out['v_ssm_d'], 'v_w_glu': out['v_w_glu'], 'v_b_glu': out['v_b_glu'], 'v_attn_out_g': out['v_attn_out_g'], 'v_ssm_out_g': out['v_ssm_out_g'], 'v_w_out': out['v_w_out'], 'v_norm2_g': out['v_norm2_g'], 'v_w_mlp_in': out['v_w_mlp_in'], 'v_w_mlp_out': out['v_w_mlp_out']}


def _loss(weights, diff, rest, loss_target):
    with _jax.named_scope("forward"):
        args = {**rest, TWIN_DIFF_INPUT: diff, **{k: w.astype(_WEIGHT_DTYPES[k]) for k, w in weights.items()}}
        y = _forward(args)
    with _jax.named_scope("loss_head"):
        err = _jnp.square(y.astype(_jnp.float32) - loss_target)
        return 0.5 * _jnp.sum(_jnp.mean(err, axis=-1)) if err.ndim else 0.5 * err


def _adamw(w, g, m, v):
    m = ADAM_B1 * m + (1.0 - ADAM_B1) * g
    v = ADAM_B2 * v + (1.0 - ADAM_B2) * _jnp.square(g)
    m_hat = m / (1.0 - ADAM_B1 ** ADAM_STEP)
    v_hat = v / (1.0 - ADAM_B2 ** ADAM_STEP)
    delta = -ADAM_LR * (m_hat / (_jnp.sqrt(v_hat) + ADAM_EPS) + ADAM_WD * w)
    return delta, m, v


def reference(x, norm1_g, w_in, q_norm_g, k_norm_g, ssm_lambda_re, ssm_lambda_im, ssm_log_dt, ssm_b_re, ssm_b_im, ssm_c_re, ssm_c_im, ssm_d, w_glu, b_glu, attn_out_g, ssm_out_g, w_out, norm2_g, w_mlp_in, w_mlp_out, loss_target, m_norm1_g, m_w_in, m_q_norm_g, m_k_norm_g, m_ssm_lambda_re, m_ssm_lambda_im, m_ssm_log_dt, m_ssm_b_re, m_ssm_b_im, m_ssm_c_re, m_ssm_c_im, m_ssm_d, m_w_glu, m_b_glu, m_attn_out_g, m_ssm_out_g, m_w_out, m_norm2_g, m_w_mlp_in, m_w_mlp_out, v_norm1_g, v_w_in, v_q_norm_g, v_k_norm_g, v_ssm_lambda_re, v_ssm_lambda_im, v_ssm_log_dt, v_ssm_b_re, v_ssm_b_im, v_ssm_c_re, v_ssm_c_im, v_ssm_d, v_w_glu, v_b_glu, v_attn_out_g, v_ssm_out_g, v_w_out, v_norm2_g, v_w_mlp_in, v_w_mlp_out):
    given = dict(x=x, norm1_g=norm1_g, w_in=w_in, q_norm_g=q_norm_g, k_norm_g=k_norm_g, ssm_lambda_re=ssm_lambda_re, ssm_lambda_im=ssm_lambda_im, ssm_log_dt=ssm_log_dt, ssm_b_re=ssm_b_re, ssm_b_im=ssm_b_im, ssm_c_re=ssm_c_re, ssm_c_im=ssm_c_im, ssm_d=ssm_d, w_glu=w_glu, b_glu=b_glu, attn_out_g=attn_out_g, ssm_out_g=ssm_out_g, w_out=w_out, norm2_g=norm2_g, w_mlp_in=w_mlp_in, w_mlp_out=w_mlp_out, loss_target=loss_target, m_norm1_g=m_norm1_g, m_w_in=m_w_in, m_q_norm_g=m_q_norm_g, m_k_norm_g=m_k_norm_g, m_ssm_lambda_re=m_ssm_lambda_re, m_ssm_lambda_im=m_ssm_lambda_im, m_ssm_log_dt=m_ssm_log_dt, m_ssm_b_re=m_ssm_b_re, m_ssm_b_im=m_ssm_b_im, m_ssm_c_re=m_ssm_c_re, m_ssm_c_im=m_ssm_c_im, m_ssm_d=m_ssm_d, m_w_glu=m_w_glu, m_b_glu=m_b_glu, m_attn_out_g=m_attn_out_g, m_ssm_out_g=m_ssm_out_g, m_w_out=m_w_out, m_norm2_g=m_norm2_g, m_w_mlp_in=m_w_mlp_in, m_w_mlp_out=m_w_mlp_out, v_norm1_g=v_norm1_g, v_w_in=v_w_in, v_q_norm_g=v_q_norm_g, v_k_norm_g=v_k_norm_g, v_ssm_lambda_re=v_ssm_lambda_re, v_ssm_lambda_im=v_ssm_lambda_im, v_ssm_log_dt=v_ssm_log_dt, v_ssm_b_re=v_ssm_b_re, v_ssm_b_im=v_ssm_b_im, v_ssm_c_re=v_ssm_c_re, v_ssm_c_im=v_ssm_c_im, v_ssm_d=v_ssm_d, v_w_glu=v_w_glu, v_b_glu=v_b_glu, v_attn_out_g=v_attn_out_g, v_ssm_out_g=v_ssm_out_g, v_w_out=v_w_out, v_norm2_g=v_norm2_g, v_w_mlp_in=v_w_mlp_in, v_w_mlp_out=v_w_mlp_out)
    weights = {n: given[n] for n in TWIN_WEIGHTS}
    shared = {n: given[n] for n in SHARED_INPUTS}
    per_example = {n: given[n] for n in ['x']}
    grad_fn = _jax.value_and_grad(_loss, argnums=(0, 1))

    def one_microbatch(ex, loss_target):
        ex = dict(ex)
        diff = ex.pop(TWIN_DIFF_INPUT)
        return grad_fn(weights, diff, {**shared, **ex}, loss_target)

    if N_MICROBATCH == 1:
        loss, (grad_w, grad_x) = one_microbatch(per_example, given["loss_target"])
    else:
        def body(carry, xs):
            loss_sum, grad_sum = carry
            l_k, (gw_k, gx_k) = one_microbatch(xs[0], xs[1])
            with _jax.named_scope("update"):
                return (loss_sum + l_k, _jax.tree.map(_jnp.add, grad_sum, gw_k)), gx_k

        init = (_jnp.zeros((), _jnp.float32), _jax.tree.map(_jnp.zeros_like, weights))
        (loss, grad_w), grad_x = _jax.lax.scan(body, init, (per_example, given["loss_target"]))
    with _jax.named_scope("update"):
        delta_w, new_m, new_v = {}, {}, {}
        for n in TWIN_WEIGHTS:
            delta_w[n], new_m[n], new_v[n] = _adamw(weights[n], grad_w[n], given["m_" + n], given["v_" + n])
    return (loss, grad_x, *[grad_w[n] for n in TWIN_WEIGHTS], *[delta_w[n] for n in TWIN_WEIGHTS],
            *[new_m[n] for n in TWIN_WEIGHTS], *[new_v[n] for n in TWIN_WEIGHTS])
```

```python
import functools
import math

import jax
import jax.numpy as jnp
from jax import lax
from jax.experimental import pallas as pl
from jax.experimental.pallas import tpu as pltpu

F32 = jnp.float32
BF16 = jnp.bfloat16
HIGHEST = lax.Precision.HIGHEST
MESH = pl.DeviceIdType.MESH

RMS_EPS = 1e-6
HEAD_DIM = 64
SSM_GROUP = 16
SSM_CHUNK = 16
LANES = 128
N_CHIPS = 4
N_DEV = 8
VMEM_LIMIT = 48 * 1024 * 1024

ADAM_LR = 0.001
ADAM_B1 = 0.9
ADAM_B2 = 0.999
ADAM_EPS = 1e-08
ADAM_WD = 0.01
ADAM_STEP = 10


def _tile(n, pref):
    t = min(n, pref)
    while n % t:
        t //= 2
    return t


def _params(*sem):
    return pltpu.CompilerParams(dimension_semantics=sem, vmem_limit_bytes=VMEM_LIMIT)


_DIMS = {"nn": (((1,), (0,)), ((), ())), "nt": (((1,), (1,)), ((), ())), "tn": (((0,), (0,)), ((), ()))}


def _mm(name, a, b, mode, *, tm=1024, tn=1024, tk=512, a_fn=None, extras=(), epilogue=None, out_dtypes=(F32,)):
    if mode == "nn":
        (m, k), n = a.shape, b.shape[1]
    elif mode == "nt":
        (m, k), n = a.shape, b.shape[0]
    else:
        (k, m), n = a.shape, b.shape[1]
    tm, tn, tk = _tile(m, tm), _tile(n, tn), _tile(k, tk)
    nk = k // tk
    ne, nout = len(extras), len(out_dtypes)
    dims = _DIMS[mode]

    def body(a_ref, b_ref, *rest):
        ex, outs = rest[:ne], rest[ne:ne + nout]
        at = a_ref[...]
        if a_fn is not None:
            at = a_fn(at)
        p = lax.dot_general(at.astype(BF16), b_ref[...].astype(BF16), dims, preferred_element_type=F32)

        def finish(r):
            if epilogue is not None:
                r = epilogue(r, *[e[...] for e in ex])
            if not isinstance(r, (tuple, list)):
                r = (r,)
            for o, v in zip(outs, r):
                o[...] = v.astype(o.dtype)

        if nk == 1:
            finish(p)
        else:
            acc = rest[ne + nout]
            kk = pl.program_id(2)

            @pl.when(kk == 0)
            def _():
                acc[...] = p

            @pl.when(kk > 0)
            def _():
                acc[...] += p

            @pl.when(kk == nk - 1)
            def _():
                finish(acc[...])

    if mode == "tn":
        a_spec = pl.BlockSpec((tk, tm), lambda i, j, kk: (kk, i))
    else:
        a_spec = pl.BlockSpec((tm, tk), lambda i, j, kk: (i, kk))
    if mode == "nt":
        b_spec = pl.BlockSpec((tn, tk), lambda i, j, kk: (j, kk))
    else:
        b_spec = pl.BlockSpec((tk, tn), lambda i, j, kk: (kk, j))
    ex_specs = []
    for _, kind in extras:
        if kind == "tile":
            ex_specs.append(pl.BlockSpec((tm, tn), lambda i, j, kk: (i, j)))
        else:
            ex_specs.append(pl.BlockSpec((1, tn), lambda i, j, kk: (0, j)))
    return pl.pallas_call(
        body, name=name, grid=(m // tm, n // tn, nk),
        in_specs=[a_spec, b_spec] + ex_specs,
        out_specs=[pl.BlockSpec((tm, tn), lambda i, j, kk: (i, j)) for _ in out_dtypes],
        out_shape=[jax.ShapeDtypeStruct((m, n), dt) for dt in out_dtypes],
        scratch_shapes=[pltpu.VMEM((tm, tn), F32)] if nk > 1 else [],
        compiler_params=_params("parallel", "parallel", "arbitrary"),
    )(a, b, *[e for e, _ in extras])


def _rowwise(name, fn, rows, consts, row_outs, acc_outs=(), tm=256):
    norm = [r if isinstance(r, tuple) else (r, r.shape[1], 0) for r in rows]
    t = norm[0][0].shape[0]
    tm = _tile(t, tm)
    nr, nc, no = len(norm), len(consts), len(row_outs)

    def body(*refs):
        outs = fn(*[r[...] for r in refs[:nr + nc]])
        if not isinstance(outs, (tuple, list)):
            outs = (outs,)
        o_refs, a_refs = refs[nr + nc:nr + nc + no], refs[nr + nc + no:]
        for r, v in zip(o_refs, outs[:no]):
            r[...] = v.astype(r.dtype)
        if a_refs:
            i = pl.program_id(0)

            @pl.when(i == 0)
            def _():
                for r, v in zip(a_refs, outs[no:]):
                    r[...] = v

            @pl.when(i > 0)
            def _():
                for r, v in zip(a_refs, outs[no:]):
                    r[...] += v

    in_specs = [pl.BlockSpec((tm, w), functools.partial(lambda i, cb: (i, cb), cb=cb)) for _, w, cb in norm]
    in_specs += [pl.BlockSpec(c.shape, functools.partial(lambda i, nd: (0,) * nd, nd=c.ndim)) for c in consts]
    out_specs = [pl.BlockSpec((tm, w), lambda i: (i, 0)) for w, _ in row_outs]
    out_specs += [pl.BlockSpec(s, functools.partial(lambda i, nd: (0,) * nd, nd=len(s))) for s in acc_outs]
    out_shape = [jax.ShapeDtypeStruct((t, w), dt) for w, dt in row_outs]
    out_shape += [jax.ShapeDtypeStruct(s, F32) for s in acc_outs]
    return pl.pallas_call(
        body, name=name, grid=(t // tm,), in_specs=in_specs, out_specs=out_specs, out_shape=out_shape,
        compiler_params=_params("arbitrary"),
    )(*[r[0] for r in norm], *consts)


def _rms(x, g):
    return x * lax.rsqrt(jnp.mean(x * x, axis=-1, keepdims=True) + RMS_EPS) * g


def _head_rms(x, g, ones_blocks):
    ss = jnp.dot(x * x, ones_blocks, precision=HIGHEST, preferred_element_type=F32)
    return x * lax.rsqrt(ss * (1.0 / HEAD_DIM) + RMS_EPS) * g


def _gelu(x):
    return x * (0.5 * (1.0 + jnp.tanh(math.sqrt(2.0 / math.pi) * (x + 0.044715 * (x * x * x)))))


def _relu_sq(x):
    r = jnp.maximum(x, 0.0)
    return r * r


def _mixed(sb, y_ssm, gate_pre, g_attn, g_ssm):
    ssm = _gelu(y_ssm) * jax.nn.sigmoid(gate_pre)
    return jnp.concatenate([_rms(sb, g_attn), _rms(ssm, g_ssm)], axis=-1)


def _softplus(z):
    return jnp.maximum(z, 0.0) + jnp.log(1.0 + jnp.exp(-jnp.abs(z)))


def _suffix_sums(x, tri):
    hi = x.astype(BF16)
    lo = (x - hi.astype(F32)).astype(BF16)
    return (jnp.dot(hi, tri, preferred_element_type=F32) + jnp.dot(lo, tri, preferred_element_type=F32))


def _dot_nt(a, b, **kw):
    return lax.dot_general(a, b, _DIMS["nt"], preferred_element_type=F32, **kw)


def _dot_tn(a, b, **kw):
    return lax.dot_general(a, b, _DIMS["tn"], preferred_element_type=F32, **kw)


def _attn_tile_fwd(q, k, c, scale, tri, mask):
    z = _dot_nt(q, k) * scale
    sp = _softplus(z)
    if mask is not None:
        sp = jnp.where(mask, sp, 0.0)
    r = _suffix_sums(sp, tri)
    a = jnp.exp(z - r - c)
    if mask is not None:
        a = jnp.where(mask, a, 0.0)
    return z, a, c + r[:, 0:1]


def _attn_fwd(qn, kn, v, *, batch, seq, bq):
    width = qn.shape[1]
    bq = _tile(seq, bq)
    nq = seq // bq
    scale = 1.0 / math.sqrt(HEAD_DIM)
    heads_per_block = LANES // HEAD_DIM

    def body(q_ref, k_ref, v_ref, o_ref, c_ref):
        row = lax.broadcasted_iota(jnp.int32, (bq, bq), 0)
        col = lax.broadcasted_iota(jnp.int32, (bq, bq), 1)
        tri = (row >= col).astype(BF16)
        mask = col < row
        for hh in range(heads_per_block):
            lanes = slice(hh * HEAD_DIM, (hh + 1) * HEAD_DIM)

            def q_block(qi, carry):
                r0 = pl.multiple_of(qi * bq, bq)
                q = q_ref[pl.ds(r0, bq), lanes]

                def tile(k0, c, msk):
                    k = k_ref[pl.ds(k0, bq), lanes]
                    _, a, c = _attn_tile_fwd(q, k, c, scale, tri, msk)
                    return jnp.dot(a.astype(BF16), v_ref[pl.ds(k0, bq), lanes], preferred_element_type=F32), c

                o, c = tile(r0, jnp.zeros((bq, 1), F32), mask)

                def k_block(it, oc):
                    k0 = pl.multiple_of((qi - 1 - it) * bq, bq)
                    o2, c2 = tile(k0, oc[1], None)
                    return oc[0] + o2, c2

                o, c = lax.fori_loop(0, qi, k_block, (o, c))
                o_ref[pl.ds(r0, bq), lanes] = o
                c_ref[pl.ds(r0, bq), lanes] = jnp.broadcast_to(c, (bq, HEAD_DIM))
                return carry

            lax.fori_loop(0, nq, q_block, 0)

    spec = pl.BlockSpec((seq, LANES), lambda b, h: (b, h))
    shape = jax.ShapeDtypeStruct((batch * seq, width), F32)
    return pl.pallas_call(
        body, name="attn_fwd", grid=(batch, width // LANES), in_specs=[spec, spec, spec], out_specs=[spec, spec],
        out_shape=[shape, shape], compiler_params=_params("parallel", "parallel"),
    )(qn, kn, v)


def _attn_bwd(qn, kn, v, c_tot, do, *, batch, seq, bq):
    width = qn.shape[1]
    bq = _tile(seq, bq)
    nq = seq // bq
    scale = 1.0 / math.sqrt(HEAD_DIM)
    heads_per_block = LANES // HEAD_DIM

    def body(q_ref, k_ref, v_ref, c_ref, do_ref, dq_ref, dk_ref, dv_ref):
        row = lax.broadcasted_iota(jnp.int32, (bq, bq), 0)
        col = lax.broadcasted_iota(jnp.int32, (bq, bq), 1)
        tri = (row >= col).astype(BF16)
        tri_t = (row <= col).astype(BF16)
        mask = col < row
        dk_ref[...] = jnp.zeros_like(dk_ref)
        dv_ref[...] = jnp.zeros_like(dv_ref)
        for hh in range(heads_per_block):
            lanes = slice(hh * HEAD_DIM, (hh + 1) * HEAD_DIM)

            def q_block(qi, carry):
                r0 = pl.multiple_of(qi * bq, bq)
                q = q_ref[pl.ds(r0, bq), lanes]
                d_out = do_ref[pl.ds(r0, bq), lanes].astype(BF16)
                c_all = c_ref[pl.ds(r0, bq), lanes][:, 0:1]

                def tile(k0, c_left, g_left, dq, msk):
                    k = k_ref[pl.ds(k0, bq), lanes]
                    z = _dot_nt(q, k) * scale
                    sp = _softplus(z)
                    if msk is not None:
                        sp = jnp.where(msk, sp, 0.0)
                    r = _suffix_sums(sp, tri)
                    c_left = c_left + r[:, 0:1]
                    a = jnp.exp(z - r - (0.0 if msk is not None else c_all - c_left))
                    if msk is not None:
                        a = jnp.where(msk, a, 0.0)
                    g = a * _dot_nt(d_out, v_ref[pl.ds(k0, bq), lanes])
                    pg = _suffix_sums(g, tri_t)
                    dz = g - jax.nn.sigmoid(z) * (g_left + pg)
                    if msk is not None:
                        dz = jnp.where(msk, dz, 0.0)
                    dz = (dz * scale).astype(BF16)
                    dk_ref[pl.ds(k0, bq), lanes] += _dot_tn(dz, q)
                    dv_ref[pl.ds(k0, bq), lanes] += _dot_tn(a.astype(BF16), d_out)
                    return c_left, g_left + pg[:, bq - 1:bq], dq + jnp.dot(dz, k, preferred_element_type=F32)

                def k_block(it, st):
                    return tile(pl.multiple_of(it * bq, bq), st[0], st[1], st[2], None)

                zero = jnp.zeros((bq, 1), F32)
                state = lax.fori_loop(0, qi, k_block, (zero, zero, jnp.zeros((bq, HEAD_DIM), F32)))
                state = tile(r0, state[0], state[1], state[2], mask)
                dq_ref[pl.ds(r0, bq), lanes] = state[2]
                return carry

            lax.fori_loop(0, nq, q_block, 0)

    spec = pl.BlockSpec((seq, LANES), lambda b, h: (b, h))
    shape = jax.ShapeDtypeStruct((batch * seq, width), F32)
    return pl.pallas_call(
        body, name="attn_bwd", grid=(batch, width // LANES), in_specs=[spec] * 5, out_specs=[spec] * 3,
        out_shape=[shape] * 3, compiler_params=_params("parallel", "parallel"),
    )(qn, kn, v, c_tot, do)


def _s5_operators(lam_re, lam_im, log_dt, b_re, b_im, c_re, c_im, d_skip):
    groups, n_state, n_ch = b_re.shape
    cs = SSM_CHUNK
    dt = jnp.exp(log_dt)[:, None]
    steps = jnp.arange(cs + 1, dtype=F32)[None, :, None]
    mag = jnp.exp(steps * (lam_re * dt)[:, None, :])
    ang = steps * (lam_im * dt)[:, None, :]
    pw_re, pw_im = mag * jnp.cos(ang), mag * jnp.sin(ang)
    num_re, num_im = pw_re[:, 1] - 1.0, pw_im[:, 1]
    den = lam_re * lam_re + lam_im * lam_im
    cf_re = (num_re * lam_re + num_im * lam_im) / den
    cf_im = (num_im * lam_re - num_re * lam_im) / den
    bb_re = cf_re[:, :, None] * b_re - cf_im[:, :, None] * b_im
    bb_im = cf_re[:, :, None] * b_im + cf_im[:, :, None] * b_re
    cl_re = c_re[:, None] * pw_re[:, :, None, :] - c_im[:, None] * pw_im[:, :, None, :]
    cl_im = c_re[:, None] * pw_im[:, :, None, :] + c_im[:, None] * pw_re[:, :, None, :]
    kern = (jnp.einsum("gkop,gpi->gkoi", cl_re[:, :cs], bb_re, precision=HIGHEST)
            - jnp.einsum("gkop,gpi->gkoi", cl_im[:, :cs], bb_im, precision=HIGHEST))
    s_idx = jnp.arange(cs)[:, None]
    t_idx = jnp.arange(cs)[None, :]
    lag = jnp.clip(t_idx - s_idx, 0, cs - 1)
    toep = jnp.where((t_idx >= s_idx)[None, :, :, None, None], kern[:, lag], 0.0)
    toep = toep + (jnp.eye(cs, dtype=F32)[None, :, :, None, None]
                   * (jnp.eye(n_ch, dtype=F32)[None] * d_skip[:, :, None])[:, None, None])
    t_mat = toep.transpose(0, 1, 4, 2, 3).reshape(groups, cs * n_ch, cs * n_ch)
    rp_re, rp_im = pw_re[:, cs - 1::-1][:, :cs], pw_im[:, cs - 1::-1][:, :cs]
    bm_re = rp_re[:, :, None, :] * bb_re.transpose(0, 2, 1)[:, None] - rp_im[:, :, None, :] * bb_im.transpose(0, 2, 1)[:, None]
    bm_im = rp_re[:, :, None, :] * bb_im.transpose(0, 2, 1)[:, None] + rp_im[:, :, None, :] * bb_re.transpose(0, 2, 1)[:, None]
    b_mat = jnp.concatenate([bm_re, bm_im], axis=-1).reshape(groups, cs * n_ch, 2 * n_state)
    c_mat = jnp.concatenate([cl_re[:, 1:], -cl_im[:, 1:]], axis=-1)
    c_mat = c_mat.transpose(0, 3, 1, 2).reshape(groups, 2 * n_state, cs * n_ch)
    la = jnp.concatenate([pw_re[:, cs], pw_re[:, cs]], axis=-1)[:, None, :]
    lb = jnp.concatenate([-pw_im[:, cs], pw_im[:, cs]], axis=-1)[:, None, :]
    return t_mat, b_mat, c_mat, la, lb


def _to_groups(u, batch):
    t, w = u.shape
    g = w // SSM_GROUP
    nch = t // batch // SSM_CHUNK
    return (u.reshape(batch, nch, SSM_CHUNK, g, SSM_GROUP).transpose(3, 0, 1, 2, 4)
            .reshape(g, batch * nch, SSM_CHUNK * SSM_GROUP))


def _from_groups(ug, batch):
    g, n, _ = ug.shape
    nch = n // batch
    return (ug.reshape(g, batch, nch, SSM_CHUNK, SSM_GROUP).transpose(1, 2, 3, 0, 4)
            .reshape(batch * nch * SSM_CHUNK, g * SSM_GROUP))


def _s5_fwd(ug, t_mat, b_mat, c_mat, la, lb, *, batch, gb=8):
    groups, n, ch = ug.shape
    p2 = b_mat.shape[2]
    gb = _tile(groups, gb)
    nch = n // batch

    def body(u_ref, t_ref, b_ref, c_ref, la_ref, lb_ref, y_ref, x_ref, s_ref):
        for g in range(gb):
            s_ref[g] = jnp.dot(u_ref[g], b_ref[g], precision=HIGHEST, preferred_element_type=F32)

        def step(r, xs):
            new = []
            for g in range(gb):
                for b in range(batch):
                    x = xs[g * batch + b]
                    row = b * nch + r
                    x_ref[g, pl.ds(row, 1), :] = x
                    new.append(la_ref[g] * x + lb_ref[g] * pltpu.roll(x, p2 // 2, 1) + s_ref[g, pl.ds(row, 1), :])
            return tuple(new)

        lax.fori_loop(0, nch, step, tuple(jnp.zeros((1, p2), F32) for _ in range(gb * batch)))
        for g in range(gb):
            y_ref[g] = (jnp.dot(u_ref[g], t_ref[g], precision=HIGHEST, preferred_element_type=F32)
                        + jnp.dot(x_ref[g], c_ref[g], precision=HIGHEST, preferred_element_type=F32))

    def spec(a, b):
        return pl.BlockSpec((gb, a, b), lambda i: (i, 0, 0))

    return pl.pallas_call(
        body, name="s5_fwd", grid=(groups // gb,),
        in_specs=[spec(n, ch), spec(ch, ch), spec(ch, p2), spec(p2, ch), spec(1, p2), spec(1, p2)],
        out_specs=[spec(n, ch), spec(n, p2)],
        out_shape=[jax.ShapeDtypeStruct((groups, n, ch), F32), jax.ShapeDtypeStruct((groups, n, p2), F32)],
        scratch_shapes=[pltpu.VMEM((gb, n, p2), F32)], compiler_params=_params("parallel"),
    )(ug, t_mat, b_mat, c_mat, la, lb)


def _s5_bwd(ug, dyg, xin, t_mat, b_mat, c_mat, la, lb, *, batch, gb=8):
    groups, n, ch = ug.shape
    p2 = b_mat.shape[2]
    gb = _tile(groups, gb)
    nch = n // batch

    def body(u_ref, dy_ref, x_ref, t_ref, b_ref, c_ref, la_ref, lb_ref,
             du_ref, dt_ref, db_ref, dc_ref, dla_ref, dlb_ref, dx_ref, ds_ref):
        for g in range(gb):
            dx_ref[g] = _dot_nt(dy_ref[g], c_ref[g], precision=HIGHEST)
        last = []
        for g in range(gb):
            for b in range(batch):
                row = b * nch + nch - 1
                ds_ref[g, pl.ds(row, 1), :] = jnp.zeros((1, p2), F32)
                last.append(dx_ref[g, pl.ds(row, 1), :])

        def step(it, dxs):
            r = nch - 2 - it
            new = []
            for g in range(gb):
                for b in range(batch):
                    dxn = dxs[g * batch + b]
                    row = b * nch + r
                    ds_ref[g, pl.ds(row, 1), :] = dxn
                    new.append(dx_ref[g, pl.ds(row, 1), :] + la_ref[g] * dxn - lb_ref[g] * pltpu.roll(dxn, p2 // 2, 1))
            return tuple(new)

        lax.fori_loop(0, nch - 1, step, tuple(last))
        for g in range(gb):
            u, dy, ds, x = u_ref[g], dy_ref[g], ds_ref[g], x_ref[g]
            du_ref[g] = _dot_nt(dy, t_ref[g], precision=HIGHEST) + _dot_nt(ds, b_ref[g], precision=HIGHEST)
            dt_ref[g] = _dot_tn(u, dy, precision=HIGHEST)
            db_ref[g] = _dot_tn(u, ds, precision=HIGHEST)
            dc_ref[g] = _dot_tn(x, dy, precision=HIGHEST)
            dla_ref[g] = jnp.sum(ds * x, axis=0, keepdims=True)
            dlb_ref[g] = jnp.sum(ds * pltpu.roll(x, p2 // 2, 1), axis=0, keepdims=True)

    def spec(a, b):
        return pl.BlockSpec((gb, a, b), lambda i: (i, 0, 0))

    def shape(a, b):
        return jax.ShapeDtypeStruct((groups, a, b), F32)

    return pl.pallas_call(
        body, name="s5_bwd", grid=(groups // gb,),
        in_specs=[spec(n, ch), spec(n, ch), spec(n, p2), spec(ch, ch), spec(ch, p2), spec(p2, ch), spec(1, p2), spec(1, p2)],
        out_specs=[spec(n, ch), spec(ch, ch), spec(ch, p2), spec(p2, ch), spec(1, p2), spec(1, p2)],
        out_shape=[shape(n, ch), shape(ch, ch), shape(ch, p2), shape(p2, ch), shape(1, p2), shape(1, p2)],
        scratch_shapes=[pltpu.VMEM((gb, n, p2), F32), pltpu.VMEM((gb, n, p2), F32)],
        compiler_params=_params("parallel"),
    )(ug, dyg, xin, t_mat, b_mat, c_mat, la, lb)


def _block(ref, axis, j, size):
    return ref.at[pl.ds(j * size, size), :] if axis == 0 else ref.at[:, pl.ds(j * size, size)]


def _gather_weights(blocks):
    n = len(blocks)
    hbm = pl.BlockSpec(memory_space=pl.ANY)

    def body(*refs):
        ins, outs = refs[:n], refs[n:2 * n]
        send_sems, recv_sems, local_sems = refs[2 * n:]
        c = lax.axis_index("c")
        my_chip = 2 * lax.axis_index("x") + lax.axis_index("y")
        for j in range(N_CHIPS):
            @pl.when(my_chip == j)
            def _(j=j):
                local, sends = [], []
                for w, (blk, axis) in enumerate(blocks):
                    size = blk.shape[axis]
                    mine = pltpu.make_async_copy(ins[w], _block(outs[w], axis, j, size), local_sems.at[w])
                    mine.start()
                    local.append(mine)
                for w, (blk, axis) in enumerate(blocks):
                    size = blk.shape[axis]
                    for peer in range(N_CHIPS):
                        if peer == j:
                            continue
                        send = pltpu.make_async_remote_copy(
                            src_ref=ins[w], dst_ref=_block(outs[w], axis, j, size),
                            send_sem=send_sems.at[w * N_CHIPS + peer], recv_sem=recv_sems.at[w * N_CHIPS + j],
                            device_id=(peer // 2, peer % 2, c), device_id_type=MESH)
                        send.start()
                        sends.append(send)
                for w, (blk, axis) in enumerate(blocks):
                    size = blk.shape[axis]
                    for peer in range(N_CHIPS):
                        if peer == j:
                            continue
                        pltpu.make_async_remote_copy(
                            src_ref=ins[w], dst_ref=_block(outs[w], axis, peer, size),
                            send_sem=send_sems.at[w * N_CHIPS + peer], recv_sem=recv_sems.at[w * N_CHIPS + peer],
                            device_id=(peer // 2, peer % 2, c), device_id_type=MESH).wait_recv()
                for cp in sends:
                    cp.wait_send()
                for cp in local:
                    cp.wait()

    out_shape = []
    for blk, axis in blocks:
        full = list(blk.shape)
        full[axis] *= N_CHIPS
        out_shape.append(jax.ShapeDtypeStruct(tuple(full), blk.dtype))
    return pl.pallas_call(
        body, name="gather_weights", in_specs=[hbm] * n, out_specs=[hbm] * n, out_shape=out_shape,
        scratch_shapes=[pltpu.SemaphoreType.DMA((n * N_CHIPS,)), pltpu.SemaphoreType.DMA((n * N_CHIPS,)),
                        pltpu.SemaphoreType.DMA((n,))],
    )(*[b for b, _ in blocks])


def _scatter_grads(grads):
    n = len(grads)
    hbm = pl.BlockSpec(memory_space=pl.ANY)

    def body(*refs):
        ins, outs = refs[:n], refs[n:2 * n]
        send_sems, recv_sems, local_sems = refs[2 * n:]
        c = lax.axis_index("c")
        my_chip = 2 * lax.axis_index("x") + lax.axis_index("y")
        for j in range(N_CHIPS):
            @pl.when(my_chip == j)
            def _(j=j):
                local, sends = [], []
                for w, (g, axis) in enumerate(grads):
                    size = g.shape[axis] // N_CHIPS
                    cp = pltpu.make_async_copy(_block(ins[w], axis, j, size), outs[w].at[j], local_sems.at[w])
                    cp.start()
                    local.append(cp)
                for w, (g, axis) in enumerate(grads):
                    size = g.shape[axis] // N_CHIPS
                    for peer in range(N_CHIPS):
                        if peer == j:
                            continue
                        cp = pltpu.make_async_remote_copy(
                            src_ref=_block(ins[w], axis, peer, size), dst_ref=outs[w].at[j],
                            send_sem=send_sems.at[w * N_CHIPS + peer], recv_sem=recv_sems.at[w * N_CHIPS + j],
                            device_id=(peer // 2, peer % 2, c), device_id_type=MESH)
                        cp.start()
                        sends.append(cp)
                for w, (g, axis) in enumerate(grads):
                    size = g.shape[axis] // N_CHIPS
                    for peer in range(N_CHIPS):
                        if peer == j:
                            continue
                        pltpu.make_async_remote_copy(
                            src_ref=_block(ins[w], axis, peer, size), dst_ref=outs[w].at[peer],
                            send_sem=send_sems.at[w * N_CHIPS + peer], recv_sem=recv_sems.at[w * N_CHIPS + peer],
                            device_id=(peer // 2, peer % 2, c), device_id_type=MESH).wait_recv()
                for cp in sends:
                    cp.wait_send()
                for cp in local:
                    cp.wait()

    out_shape = []
    for g, axis in grads:
        blk = list(g.shape)
        blk[axis] //= N_CHIPS
        out_shape.append(jax.ShapeDtypeStruct((N_CHIPS, *blk), g.dtype))
    return pl.pallas_call(
        body, name="scatter_grads", in_specs=[hbm] * n, out_specs=[hbm] * n, out_shape=out_shape,
        scratch_shapes=[pltpu.SemaphoreType.DMA((n * N_CHIPS,)), pltpu.SemaphoreType.DMA((n * N_CHIPS,)),
                        pltpu.SemaphoreType.DMA((n,))],
    )(*[g for g, _ in grads])


def _sum_slots(name, slots, tm=256):
    _, r, c = slots.shape
    tm = _tile(r, tm)

    def body(s0, s1, s2, s3, o_ref):
        o_ref[...] = ((s0[...] + s1[...]) + s2[...]) + s3[...]

    specs = [pl.BlockSpec((None, tm, c), functools.partial(lambda i, s: (s, i, 0), s=s)) for s in range(N_CHIPS)]
    return pl.pallas_call(
        body, name=name, grid=(r // tm,), in_specs=specs, out_specs=pl.BlockSpec((tm, c), lambda i: (i, 0)),
        out_shape=jax.ShapeDtypeStruct((r, c), F32), compiler_params=_params("parallel"),
    )(slots, slots, slots, slots)


def _swap_with_sibling(arrays):
    n = len(arrays)
    hbm = pl.BlockSpec(memory_space=pl.ANY)

    def body(*refs):
        ins, outs = refs[:n], refs[n:2 * n]
        send_sems, recv_sems = refs[2 * n:]
        sibling = (lax.axis_index("x"), lax.axis_index("y"), 1 - lax.axis_index("c"))
        copies = [pltpu.make_async_remote_copy(src_ref=ins[w], dst_ref=outs[w], send_sem=send_sems.at[w],
                                               recv_sem=recv_sems.at[w], device_id=sibling, device_id_type=MESH)
                  for w in range(n)]
        for cp in copies:
            cp.start()
        for cp in copies:
            cp.wait()

    return pl.pallas_call(
        body, name="swap_with_sibling", in_specs=[hbm] * n, out_specs=[hbm] * n,
        out_shape=[jax.ShapeDtypeStruct(a.shape, a.dtype) for a in arrays],
        scratch_shapes=[pltpu.SemaphoreType.DMA((n,)), pltpu.SemaphoreType.DMA((n,))],
    )(*arrays)


def _all_reduce_small(packed):
    r, lanes = packed.shape
    vmem = pl.BlockSpec(memory_space=pltpu.VMEM)

    def body(in_ref, out_ref, buf, send_sems, recv_sems):
        x, y, c = lax.axis_index("x"), lax.axis_index("y"), lax.axis_index("c")
        me = 4 * x + 2 * y + c
        buf[me] = in_ref[...]
        sends = []
        for d in range(N_DEV):
            @pl.when(me != d)
            def _(d=d):
                pltpu.make_async_remote_copy(
                    src_ref=in_ref, dst_ref=buf.at[me], send_sem=send_sems.at[d], recv_sem=recv_sems.at[me],
                    device_id=(d // 4, (d // 2) % 2, d % 2), device_id_type=MESH).start()
        for d in range(N_DEV):
            @pl.when(me != d)
            def _(d=d):
                cp = pltpu.make_async_remote_copy(
                    src_ref=in_ref, dst_ref=buf.at[d], send_sem=send_sems.at[d], recv_sem=recv_sems.at[d],
                    device_id=(d // 4, (d // 2) % 2, d % 2), device_id_type=MESH)
                cp.wait_recv()
                cp.wait_send()
        del sends
        acc = buf[0]
        for d in range(1, N_DEV):
            acc = acc + buf[d]
        out_ref[...] = acc

    return pl.pallas_call(
        body, name="all_reduce_small", in_specs=[vmem], out_specs=vmem,
        out_shape=jax.ShapeDtypeStruct((r, lanes), F32),
        scratch_shapes=[pltpu.VMEM((N_DEV, r, lanes), F32), pltpu.SemaphoreType.DMA((N_DEV,)),
                        pltpu.SemaphoreType.DMA((N_DEV,))],
        compiler_params=pltpu.CompilerParams(vmem_limit_bytes=VMEM_LIMIT),
    )(packed)


def _adamw(g, w, m, v):
    m = ADAM_B1 * m + (1.0 - ADAM_B1) * g
    v = ADAM_B2 * v + (1.0 - ADAM_B2) * jnp.square(g)
    m_hat = m / (1.0 - ADAM_B1 ** ADAM_STEP)
    v_hat = v / (1.0 - ADAM_B2 ** ADAM_STEP)
    delta = -ADAM_LR * (m_hat / (jnp.sqrt(v_hat) + ADAM_EPS) + ADAM_WD * w)
    return delta, m, v


def kernel(x, norm1_g, w_in, q_norm_g, k_norm_g, ssm_lambda_re, ssm_lambda_im, ssm_log_dt, ssm_b_re, ssm_b_im, ssm_c_re, ssm_c_im, ssm_d, w_glu, b_glu, attn_out_g, ssm_out_g, w_out, norm2_g, w_mlp_in, w_mlp_out, loss_target, m_norm1_g, m_w_in, m_q_norm_g, m_k_norm_g, m_ssm_lambda_re, m_ssm_lambda_im, m_ssm_log_dt, m_ssm_b_re, m_ssm_b_im, m_ssm_c_re, m_ssm_c_im, m_ssm_d, m_w_glu, m_b_glu, m_attn_out_g, m_ssm_out_g, m_w_out, m_norm2_g, m_w_mlp_in, m_w_mlp_out, v_norm1_g, v_w_in, v_q_norm_g, v_k_norm_g, v_ssm_lambda_re, v_ssm_lambda_im, v_ssm_log_dt, v_ssm_b_re, v_ssm_b_im, v_ssm_c_re, v_ssm_c_im, v_ssm_d, v_w_glu, v_b_glu, v_attn_out_g, v_ssm_out_g, v_w_out, v_norm2_g, v_w_mlp_in, v_w_mlp_out):
    batch, seq, d_model = x.shape
    tokens = batch * seq
    sb_width = w_in.shape[1]
    n_features = d_model

    big = [("w_in", w_in, m_w_in, v_w_in, 1), ("w_glu", w_glu, m_w_glu, v_w_glu, 0),
           ("w_out", w_out, m_w_out, v_w_out, 0), ("w_mlp_in", w_mlp_in, m_w_mlp_in, v_w_mlp_in, 1),
           ("w_mlp_out", w_mlp_out, m_w_mlp_out, v_w_mlp_out, 0)]
    small = [("norm1_g", norm1_g, m_norm1_g, v_norm1_g), ("q_norm_g", q_norm_g, m_q_norm_g, v_q_norm_g),
             ("k_norm_g", k_norm_g, m_k_norm_g, v_k_norm_g),
             ("ssm_lambda_re", ssm_lambda_re, m_ssm_lambda_re, v_ssm_lambda_re),
             ("ssm_lambda_im", ssm_lambda_im, m_ssm_lambda_im, v_ssm_lambda_im),
             ("ssm_log_dt", ssm_log_dt, m_ssm_log_dt, v_ssm_log_dt),
             ("ssm_b_re", ssm_b_re, m_ssm_b_re, v_ssm_b_re), ("ssm_b_im", ssm_b_im, m_ssm_b_im, v_ssm_b_im),
             ("ssm_c_re", ssm_c_re, m_ssm_c_re, v_ssm_c_re), ("ssm_c_im", ssm_c_im, m_ssm_c_im, v_ssm_c_im),
             ("ssm_d", ssm_d, m_ssm_d, v_ssm_d), ("b_glu", b_glu, m_b_glu, v_b_glu),
             ("attn_out_g", attn_out_g, m_attn_out_g, v_attn_out_g), ("ssm_out_g", ssm_out_g, m_ssm_out_g, v_ssm_out_g),
             ("norm2_g", norm2_g, m_norm2_g, v_norm2_g)]

    wf_in, wf_glu, wf_out, wf_mlp_in, wf_mlp_out = _gather_weights([(w.astype(BF16), axis) for _, w, _, _, axis in big])

    x2 = x.reshape(tokens, d_model)
    tgt2 = loss_target.reshape(tokens, d_model)
    g1, g2 = norm1_g[None, :], norm2_g[None, :]
    g_attn, g_ssm, bias_glu = attn_out_g[None, :], ssm_out_g[None, :], b_glu[None, :]
    heads = sb_width // HEAD_DIM
    gq, gk = jnp.tile(q_norm_g, heads)[None, :], jnp.tile(k_norm_g, heads)[None, :]
    lane_head = jnp.arange(sb_width) // HEAD_DIM
    ones_blocks = (lane_head[:, None] == lane_head[None, :]).astype(F32)

    (xn,) = _rowwise("norm1", _rms, [x2], [g1], [(d_model, BF16)])
    (proj,) = _mm("proj_in", xn, wf_in, "nn")

    def qkv_fn(q, k, v, gq_, gk_, ones):
        return _head_rms(q, gq_, ones), _head_rms(k, gk_, ones), v

    qn, kn, vb = _rowwise("qk_norm", qkv_fn, [(proj, sb_width, 0), (proj, sb_width, 1), (proj, sb_width, 2)],
                          [gq, gk, ones_blocks], [(sb_width, BF16)] * 3)
    sb, c_tot = _attn_fwd(qn, kn, vb, batch=batch, seq=seq, bq=256)

    s5_params = (ssm_lambda_re, ssm_lambda_im, ssm_log_dt, ssm_b_re, ssm_b_im, ssm_c_re, ssm_c_im, ssm_d)
    (t_mat, b_mat, c_mat, la, lb), s5_vjp = jax.vjp(_s5_operators, *s5_params)
    ug = _to_groups(proj[:, 3 * sb_width:], batch)
    yg, xin = _s5_fwd(ug, t_mat, b_mat, c_mat, la, lb, batch=batch)
    y_ssm = _from_groups(yg, batch)

    (y_act,) = _rowwise("gelu", _gelu, [y_ssm], [], [(sb_width, BF16)])
    (gate_pre,) = _mm("glu_gate", y_act, wf_glu, "nn", extras=[(bias_glu, "row")], epilogue=lambda acc, b: acc + b)
    (mixed,) = _rowwise("mix_norm", _mixed, [sb, y_ssm, gate_pre], [g_attn, g_ssm], [(2 * sb_width, BF16)])
    (h1,) = _mm("proj_out", mixed, wf_out, "nn", extras=[(x2, "tile")], epilogue=lambda acc, r: acc + r)
    (hn,) = _rowwise("norm2", _rms, [h1], [g2], [(d_model, BF16)])
    (pre,) = _mm("mlp_in", hn, wf_mlp_in, "nn", tk=1024)
    inv_n = 1.0 / n_features
    (dy,) = _mm("mlp_out_loss", pre, wf_mlp_out, "nn", a_fn=_relu_sq, extras=[(h1, "tile"), (tgt2, "tile")],
                epilogue=lambda acc, r, t: ((acc + r) - t) * inv_n)

    def loss_fn(d):
        return d, jnp.sum(d * d, keepdims=True) * (0.5 * n_features)

    _, loss_part = _rowwise("loss", loss_fn, [dy], [], [(d_model, F32)], [(1, 1)])

    (dw_mlp_out,) = _mm("dw_mlp_out", pre, dy, "tn", a_fn=_relu_sq)
    (dpre,) = _mm("d_mlp_act", dy, wf_mlp_out, "nt", extras=[(pre, "tile")],
                  epilogue=lambda acc, p: acc * (2.0 * jnp.maximum(p, 0.0)), out_dtypes=(BF16,))
    (dw_mlp_in,) = _mm("dw_mlp_in", hn, dpre, "tn", tk=1024)
    (dhn,) = _mm("d_norm2_in", dpre, wf_mlp_in, "nt", tk=1024)

    def norm_bwd(res, hx, dn, g):
        _, vjp = jax.vjp(_rms, hx, g)
        dh, dg = vjp(dn)
        return res + dh, dg

    dh1, dg_norm2 = _rowwise("norm2_bwd", norm_bwd, [dy, h1, dhn], [g2], [(d_model, F32)], [(1, d_model)])
    (dmixed,) = _mm("d_mixed", dh1, wf_out, "nt")
    (dw_out,) = _mm("dw_out", mixed, dh1, "tn")

    def mixed_bwd(dm, sb_, ys, gp, ga, gs):
        _, vjp = jax.vjp(lambda a, act, b, c, d: jnp.concatenate(
            [_rms(a, c), _rms(act * jax.nn.sigmoid(b), d)], axis=-1), sb_, _gelu(ys), gp, ga, gs)
        dsb_, dact, dgp_, dga, dgs = vjp(dm)
        return dsb_, dgp_, dact, dga, dgs, jnp.sum(dgp_, axis=0, keepdims=True)

    dsb, dgate_pre, dact_part, dg_attn, dg_ssm, db_glu = _rowwise(
        "mix_norm_bwd", mixed_bwd, [dmixed, sb, y_ssm, gate_pre], [g_attn, g_ssm],
        [(sb_width, F32), (sb_width, BF16), (sb_width, F32)], [(1, sb_width)] * 3)

    def gelu_bwd(acc, part, ys):
        _, vjp = jax.vjp(_gelu, ys)
        return vjp(acc + part)[0]

    (dy_ssm,) = _mm("d_glu_in", dgate_pre, wf_glu, "nt", extras=[(dact_part, "tile"), (y_ssm, "tile")], epilogue=gelu_bwd)
    (dw_glu,) = _mm("dw_glu", y_act, dgate_pre, "tn")

    dug, dt_mat, db_mat, dc_mat, dla, dlb = _s5_bwd(ug, _to_groups(dy_ssm, batch), xin, t_mat, b_mat, c_mat, la, lb,
                                                    batch=batch)
    du = _from_groups(dug, batch)
    ds5 = s5_vjp((dt_mat, db_mat, dc_mat, dla, dlb))

    dqn, dkn, dv = _attn_bwd(qn, kn, vb, c_tot, dsb, batch=batch, seq=seq, bq=256)

    def qk_bwd(q, k, dq_, dk_, gq_, gk_, ones):
        _, vjp_q = jax.vjp(lambda a, g: _head_rms(a, g, ones), q, gq_)
        _, vjp_k = jax.vjp(lambda a, g: _head_rms(a, g, ones), k, gk_)
        dq, dgq = vjp_q(dq_)
        dk, dgk = vjp_k(dk_)
        return dq, dk, dgq, dgk

    dq, dk, dgq, dgk = _rowwise("qk_norm_bwd", qk_bwd, [(proj, sb_width, 0), (proj, sb_width, 1), dqn, dkn],
                                [gq, gk, ones_blocks], [(sb_width, BF16)] * 2, [(1, sb_width)] * 2)
    dproj = jnp.concatenate([dq, dk, dv.astype(BF16), du.astype(BF16)], axis=1)
    (dw_in,) = _mm("dw_in", xn, dproj, "tn")
    (dxn,) = _mm("d_norm1_in", dproj, wf_in, "nt")
    dx, dg_norm1 = _rowwise("norm1_bwd", norm_bwd, [dh1, x2, dxn], [g1], [(d_model, F32)], [(1, d_model)])

    big_grads = [dw_in, dw_glu, dw_out, dw_mlp_in, dw_mlp_out]
    slots = _scatter_grads([(g, axis) for g, (_, _, _, _, axis) in zip(big_grads, big)])
    mine = [_sum_slots("sum_" + name, s) for s, (name, *_rest) in zip(slots, big)]
    theirs = _swap_with_sibling(mine)

    small_grads = [dg_norm1[0], dgq.reshape(heads, HEAD_DIM).sum(0), dgk.reshape(heads, HEAD_DIM).sum(0), *ds5,
                   db_glu[0], dg_attn[0], dg_ssm[0], dg_norm2[0]]
    order = ["norm1_g", "q_norm_g", "k_norm_g", "ssm_lambda_re", "ssm_lambda_im", "ssm_log_dt", "ssm_b_re", "ssm_b_im",
             "ssm_c_re", "ssm_c_im", "ssm_d", "b_glu", "attn_out_g", "ssm_out_g", "norm2_g"]
    assert order == [name for name, *_ in small]

    def pack(parts, extra=None):
        flat = [p.reshape(-1) for p in parts] + ([extra.reshape(-1)] if extra is not None else [])
        flat = jnp.concatenate(flat)
        rows = -(-flat.shape[0] // (LANES * LANES)) * LANES
        return jnp.pad(flat, (0, rows * LANES - flat.shape[0])).reshape(rows, LANES)

    n_small = sum(w.size for _, w, _, _ in small)
    reduced = _all_reduce_small(pack(small_grads, loss_part))
    loss = reduced.reshape(-1)[n_small]

    def adam_big(sa, sb_, w, m, v):
        g = sa + sb_
        delta, m, v = _adamw(g, w, m, v)
        return g, delta, m, v

    def adam_small(g, w, m, v):
        delta, m, v = _adamw(g, w, m, v)
        return delta, m, v

    big_out = {}
    for (name, w, m, v, _), sa, sb_ in zip(big, mine, theirs):
        big_out[name] = _rowwise("adamw_" + name, adam_big, [sa, sb_, w, m, v], [], [(w.shape[1], F32)] * 4)
    ones_pad = jnp.ones((1,), F32)
    small_upd = _rowwise("adamw_small", adam_small,
                         [reduced, pack([w for _, w, _, _ in small], ones_pad * 0), pack([m for _, _, m, _ in small], ones_pad * 0),
                          pack([v for _, _, _, v in small], ones_pad)], [], [(LANES, F32)] * 3)

    def unpack(packed):
        flat, out, off = packed.reshape(-1), {}, 0
        for name, w, _, _ in small:
            out[name] = flat[off:off + w.size].reshape(w.shape)
            off += w.size
        return out

    small_out = [unpack(reduced)] + [unpack(p) for p in small_upd]
    names = ["norm1_g", "w_in", "q_norm_g", "k_norm_g", "ssm_lambda_re", "ssm_lambda_im", "ssm_log_dt", "ssm_b_re",
             "ssm_b_im", "ssm_c_re", "ssm_c_im", "ssm_d", "w_glu", "b_glu", "attn_out_g", "ssm_out_g", "w_out",
             "norm2_g", "w_mlp_in", "w_mlp_out"]
    outs = [loss, dx.reshape(batch, seq, d_model)]
    for kind in range(4):
        for name in names:
            outs.append(big_out[name][kind] if name in big_out else small_out[kind][name])
    return tuple(outs)
```

```python
import functools
import math

import jax
import jax.numpy as jnp
from jax import lax
from jax.experimental import pallas as pl
from jax.experimental.pallas import tpu as pltpu

F32 = jnp.float32
BF16 = jnp.bfloat16
HIGHEST = lax.Precision.HIGHEST
MESH = pl.DeviceIdType.MESH

RMS_EPS = 1e-6
HEAD_DIM = 64
SSM_GROUP = 16
SSM_CHUNK = 16
LANES = 128
N_CHIPS = 4
N_DEV = 8
VMEM_LIMIT = 48 * 1024 * 1024

ADAM_LR = 0.001
ADAM_B1 = 0.9
ADAM_B2 = 0.999
ADAM_EPS = 1e-08
ADAM_WD = 0.01
ADAM_STEP = 10


def _tile(n, pref):
    t = min(n, pref)
    while n % t:
        t //= 2
    return t


def _params(*sem):
    return pltpu.CompilerParams(dimension_semantics=sem, vmem_limit_bytes=VMEM_LIMIT)


_DIMS = {"nn": (((1,), (0,)), ((), ())), "nt": (((1,), (1,)), ((), ())), "tn": (((0,), (0,)), ((), ()))}


def _mm(name, a, b, mode, *, tm=1024, tn=1024, tk=512, a_fn=None, extras=(), epilogue=None, out_dtypes=(F32,)):
    if mode == "nn":
        (m, k), n = a.shape, b.shape[1]
    elif mode == "nt":
        (m, k), n = a.shape, b.shape[0]
    else:
        (k, m), n = a.shape, b.shape[1]
    tm, tn, tk = _tile(m, tm), _tile(n, tn), _tile(k, tk)
    nk = k // tk
    ne, nout = len(extras), len(out_dtypes)
    dims = _DIMS[mode]

    def body(a_ref, b_ref, *rest):
        ex, outs = rest[:ne], rest[ne:ne + nout]
        at = a_ref[...]
        if a_fn is not None:
            at = a_fn(at)
        p = lax.dot_general(at.astype(BF16), b_ref[...].astype(BF16), dims, preferred_element_type=F32)

        def finish(r):
            if epilogue is not None:
                r = epilogue(r, *[e[...] for e in ex])
            if not isinstance(r, (tuple, list)):
                r = (r,)
            for o, v in zip(outs, r):
                o[...] = v.astype(o.dtype)

        if nk == 1:
            finish(p)
        else:
            acc = rest[ne + nout]
            kk = pl.program_id(2)

            @pl.when(kk == 0)
            def _():
                acc[...] = p

            @pl.when(kk > 0)
            def _():
                acc[...] += p

            @pl.when(kk == nk - 1)
            def _():
                finish(acc[...])

    if mode == "tn":
        a_spec = pl.BlockSpec((tk, tm), lambda i, j, kk: (kk, i))
    else:
        a_spec = pl.BlockSpec((tm, tk), lambda i, j, kk: (i, kk))
    if mode == "nt":
        b_spec = pl.BlockSpec((tn, tk), lambda i, j, kk: (j, kk))
    else:
        b_spec = pl.BlockSpec((tk, tn), lambda i, j, kk: (kk, j))
    ex_specs = []
    for _, kind in extras:
        if kind == "tile":
            ex_specs.append(pl.BlockSpec((tm, tn), lambda i, j, kk: (i, j)))
        else:
            ex_specs.append(pl.BlockSpec((1, tn), lambda i, j, kk: (0, j)))
    return pl.pallas_call(
        body, name=name, grid=(m // tm, n // tn, nk),
        in_specs=[a_spec, b_spec] + ex_specs,
        out_specs=[pl.BlockSpec((tm, tn), lambda i, j, kk: (i, j)) for _ in out_dtypes],
        out_shape=[jax.ShapeDtypeStruct((m, n), dt) for dt in out_dtypes],
        scratch_shapes=[pltpu.VMEM((tm, tn), F32)] if nk > 1 else [],
        compiler_params=_params("parallel", "parallel", "arbitrary"),
    )(a, b, *[e for e, _ in extras])


def _rowwise(name, fn, rows, consts, row_outs, acc_outs=(), tm=256):
    norm = [r if isinstance(r, tuple) else (r, r.shape[1], 0) for r in rows]
    t = norm[0][0].shape[0]
    tm = _tile(t, tm)
    nr, nc, no = len(norm), len(consts), len(row_outs)

    def body(*refs):
        outs = fn(*[r[...] for r in refs[:nr + nc]])
        if not isinstance(outs, (tuple, list)):
            outs = (outs,)
        o_refs, a_refs = refs[nr + nc:nr + nc + no], refs[nr + nc + no:]
        for r, v in zip(o_refs, outs[:no]):
            r[...] = v.astype(r.dtype)
        if a_refs:
            i = pl.program_id(0)

            @pl.when(i == 0)
            def _():
                for r, v in zip(a_refs, outs[no:]):
                    r[...] = v

            @pl.when(i > 0)
            def _():
                for r, v in zip(a_refs, outs[no:]):
                    r[...] += v

    in_specs = [pl.BlockSpec((tm, w), functools.partial(lambda i, cb: (i, cb), cb=cb)) for _, w, cb in norm]
    in_specs += [pl.BlockSpec(c.shape, functools.partial(lambda i, nd: (0,) * nd, nd=c.ndim)) for c in consts]
    out_specs = [pl.BlockSpec((tm, w), lambda i: (i, 0)) for w, _ in row_outs]
    out_specs += [pl.BlockSpec(s, functools.partial(lambda i, nd: (0,) * nd, nd=len(s))) for s in acc_outs]
    out_shape = [jax.ShapeDtypeStruct((t, w), dt) for w, dt in row_outs]
    out_shape += [jax.ShapeDtypeStruct(s, F32) for s in acc_outs]
    return pl.pallas_call(
        body, name=name, grid=(t // tm,), in_specs=in_specs, out_specs=out_specs, out_shape=out_shape,
        compiler_params=_params("arbitrary"),
    )(*[r[0] for r in norm], *consts)


def _rms(x, g):
    return x * lax.rsqrt(jnp.mean(x * x, axis=-1, keepdims=True) + RMS_EPS) * g


def _head_rms(x, g, ones_blocks):
    ss = jnp.dot(x * x, ones_blocks, precision=HIGHEST, preferred_element_type=F32)
    return x * lax.rsqrt(ss * (1.0 / HEAD_DIM) + RMS_EPS) * g


def _gelu(x):
    return x * (0.5 * (1.0 + jnp.tanh(math.sqrt(2.0 / math.pi) * (x + 0.044715 * (x * x * x)))))


def _relu_sq(x):
    r = jnp.maximum(x, 0.0)
    return r * r


def _mixed(sb, y_ssm, gate_pre, g_attn, g_ssm):
    ssm = _gelu(y_ssm) * jax.nn.sigmoid(gate_pre)
    return jnp.concatenate([_rms(sb, g_attn), _rms(ssm, g_ssm)], axis=-1)


def _softplus(z):
    return jnp.maximum(z, 0.0) + jnp.log(1.0 + jnp.exp(-jnp.abs(z)))


def _suffix_sums(x, tri):
    hi = x.astype(BF16)
    lo = (x - hi.astype(F32)).astype(BF16)
    return (jnp.dot(hi, tri, preferred_element_type=F32) + jnp.dot(lo, tri, preferred_element_type=F32))


def _dot_nt(a, b, **kw):
    return lax.dot_general(a, b, _DIMS["nt"], preferred_element_type=F32, **kw)


def _dot_tn(a, b, **kw):
    return lax.dot_general(a, b, _DIMS["tn"], preferred_element_type=F32, **kw)


HEAD_LANES = tuple(slice(h * HEAD_DIM, (h + 1) * HEAD_DIM) for h in range(LANES // HEAD_DIM))


def _attn_fwd(qs, kn, v, *, batch, seq, bq):
    width = qs.shape[1]
    bq = _tile(seq, bq)
    nq = seq // bq

    def body(q_ref, k_ref, v_ref, o_ref, c_ref):
        row = lax.broadcasted_iota(jnp.int32, (bq, bq), 0)
        col = lax.broadcasted_iota(jnp.int32, (bq, bq), 1)
        tri = (row >= col).astype(BF16)
        mask = col < row

        def q_block(qi, carry):
            r0 = pl.multiple_of(qi * bq, bq)
            qh = [q_ref[pl.ds(r0, bq), ln] for ln in HEAD_LANES]

            def tile(k0, state, msk):
                new = []
                for h, ln in enumerate(HEAD_LANES):
                    o, c = state[2 * h], state[2 * h + 1]
                    z = _dot_nt(qh[h], k_ref[pl.ds(k0, bq), ln])
                    sp = _softplus(z)
                    if msk is not None:
                        sp = jnp.where(msk, sp, 0.0)
                    r = _suffix_sums(sp, tri)
                    a = jnp.exp(z - r - c)
                    if msk is not None:
                        a = jnp.where(msk, a, 0.0)
                    new += [o + jnp.dot(a.astype(BF16), v_ref[pl.ds(k0, bq), ln], preferred_element_type=F32),
                            c + r[:, 0:1]]
                return tuple(new)

            init = (jnp.zeros((bq, HEAD_DIM), F32), jnp.zeros((bq, 1), F32)) * len(HEAD_LANES)
            state = tile(r0, init, mask)
            state = lax.fori_loop(0, qi, lambda it, st: tile(pl.multiple_of((qi - 1 - it) * bq, bq), st, None), state)
            for h, ln in enumerate(HEAD_LANES):
                o_ref[pl.ds(r0, bq), ln] = state[2 * h]
                c_ref[pl.ds(r0, bq), ln] = jnp.broadcast_to(state[2 * h + 1], (bq, HEAD_DIM))
            return carry

        lax.fori_loop(0, nq, q_block, 0)

    spec = pl.BlockSpec((seq, LANES), lambda b, h: (b, h))
    shape = jax.ShapeDtypeStruct((batch * seq, width), F32)
    return pl.pallas_call(
        body, name="attn_fwd", grid=(batch, width // LANES), in_specs=[spec, spec, spec], out_specs=[spec, spec],
        out_shape=[shape, shape], compiler_params=_params("parallel", "parallel"),
    )(qs, kn, v)


def _attn_bwd(qs, kn, v, c_tot, do, *, batch, seq, bq):
    width = qs.shape[1]
    bq = _tile(seq, bq)
    nq = seq // bq

    def body(q_ref, k_ref, v_ref, c_ref, do_ref, dq_ref, dk_ref, dv_ref):
        row = lax.broadcasted_iota(jnp.int32, (bq, bq), 0)
        col = lax.broadcasted_iota(jnp.int32, (bq, bq), 1)
        tri = (row >= col).astype(BF16)
        tri_t = (row <= col).astype(BF16)
        mask = col < row
        dk_ref[...] = jnp.zeros_like(dk_ref)
        dv_ref[...] = jnp.zeros_like(dv_ref)

        def q_block(qi, carry):
            r0 = pl.multiple_of(qi * bq, bq)
            qh = [q_ref[pl.ds(r0, bq), ln] for ln in HEAD_LANES]
            d_out = [do_ref[pl.ds(r0, bq), ln].astype(BF16) for ln in HEAD_LANES]
            c_all = [c_ref[pl.ds(r0, bq), ln][:, 0:1] for ln in HEAD_LANES]

            def tile(k0, state, msk):
                new = []
                for h, ln in enumerate(HEAD_LANES):
                    c_left, g_left, dq = state[3 * h:3 * h + 3]
                    k = k_ref[pl.ds(k0, bq), ln]
                    z = _dot_nt(qh[h], k)
                    e = jnp.exp(-jnp.abs(z))
                    sp = jnp.maximum(z, 0.0) + jnp.log(1.0 + e)
                    if msk is not None:
                        sp = jnp.where(msk, sp, 0.0)
                    r = _suffix_sums(sp, tri)
                    c_left = c_left + r[:, 0:1]
                    a = jnp.exp(z - r - (0.0 if msk is not None else c_all[h] - c_left))
                    if msk is not None:
                        a = jnp.where(msk, a, 0.0)
                    g = a * _dot_nt(d_out[h], v_ref[pl.ds(k0, bq), ln])
                    pg = _suffix_sums(g, tri_t)
                    sig = jnp.where(z >= 0.0, 1.0, e) / (1.0 + e)
                    dz = g - sig * (g_left + pg)
                    if msk is not None:
                        dz = jnp.where(msk, dz, 0.0)
                    dz = dz.astype(BF16)
                    dk_ref[pl.ds(k0, bq), ln] += _dot_tn(dz, qh[h])
                    dv_ref[pl.ds(k0, bq), ln] += _dot_tn(a.astype(BF16), d_out[h])
                    new += [c_left, g_left + pg[:, bq - 1:bq], dq + jnp.dot(dz, k, preferred_element_type=F32)]
                return tuple(new)

            zero = jnp.zeros((bq, 1), F32)
            init = (zero, zero, jnp.zeros((bq, HEAD_DIM), F32)) * len(HEAD_LANES)
            state = lax.fori_loop(0, qi, lambda it, st: tile(pl.multiple_of(it * bq, bq), st, None), init)
            state = tile(r0, state, mask)
            for h, ln in enumerate(HEAD_LANES):
                dq_ref[pl.ds(r0, bq), ln] = state[3 * h + 2]
            return carry

        lax.fori_loop(0, nq, q_block, 0)

    spec = pl.BlockSpec((seq, LANES), lambda b, h: (b, h))
    shape = jax.ShapeDtypeStruct((batch * seq, width), F32)
    return pl.pallas_call(
        body, name="attn_bwd", grid=(batch, width // LANES), in_specs=[spec] * 5, out_specs=[spec] * 3,
        out_shape=[shape] * 3, compiler_params=_params("parallel", "parallel"),
    )(qs, kn, v, c_tot, do)


def _s5_operators(lam_re, lam_im, log_dt, b_re, b_im, c_re, c_im, d_skip):
    groups, n_state, n_ch = b_re.shape
    cs = SSM_CHUNK
    dt = jnp.exp(log_dt)[:, None]
    steps = jnp.arange(cs + 1, dtype=F32)[None, :, None]
    mag = jnp.exp(steps * (lam_re * dt)[:, None, :])
    ang = steps * (lam_im * dt)[:, None, :]
    pw_re, pw_im = mag * jnp.cos(ang), mag * jnp.sin(ang)
    num_re, num_im = pw_re[:, 1] - 1.0, pw_im[:, 1]
    den = lam_re * lam_re + lam_im * lam_im
    cf_re = (num_re * lam_re + num_im * lam_im) / den
    cf_im = (num_im * lam_re - num_re * lam_im) / den
    bb_re = cf_re[:, :, None] * b_re - cf_im[:, :, None] * b_im
    bb_im = cf_re[:, :, None] * b_im + cf_im[:, :, None] * b_re
    cl_re = c_re[:, None] * pw_re[:, :, None, :] - c_im[:, None] * pw_im[:, :, None, :]
    cl_im = c_re[:, None] * pw_im[:, :, None, :] + c_im[:, None] * pw_re[:, :, None, :]
    kern = (jnp.einsum("gkop,gpi->gkoi", cl_re[:, :cs], bb_re, precision=HIGHEST)
            - jnp.einsum("gkop,gpi->gkoi", cl_im[:, :cs], bb_im, precision=HIGHEST))
    s_idx = jnp.arange(cs)[:, None]
    t_idx = jnp.arange(cs)[None, :]
    lag = jnp.clip(t_idx - s_idx, 0, cs - 1)
    toep = jnp.where((t_idx >= s_idx)[None, :, :, None, None], kern[:, lag], 0.0)
    toep = toep + (jnp.eye(cs, dtype=F32)[None, :, :, None, None]
                   * (jnp.eye(n_ch, dtype=F32)[None] * d_skip[:, :, None])[:, None, None])
    t_mat = toep.transpose(0, 1, 4, 2, 3).reshape(groups, cs * n_ch, cs * n_ch)
    rp_re, rp_im = pw_re[:, cs - 1::-1][:, :cs], pw_im[:, cs - 1::-1][:, :cs]
    bm_re = rp_re[:, :, None, :] * bb_re.transpose(0, 2, 1)[:, None] - rp_im[:, :, None, :] * bb_im.transpose(0, 2, 1)[:, None]
    bm_im = rp_re[:, :, None, :] * bb_im.transpose(0, 2, 1)[:, None] + rp_im[:, :, None, :] * bb_re.transpose(0, 2, 1)[:, None]
    b_mat = jnp.concatenate([bm_re, bm_im], axis=-1).reshape(groups, cs * n_ch, 2 * n_state)
    c_mat = jnp.concatenate([cl_re[:, 1:], -cl_im[:, 1:]], axis=-1)
    c_mat = c_mat.transpose(0, 3, 1, 2).reshape(groups, 2 * n_state, cs * n_ch)
    la = jnp.concatenate([pw_re[:, cs], pw_re[:, cs]], axis=-1)[:, None, :]
    lb = jnp.concatenate([-pw_im[:, cs], pw_im[:, cs]], axis=-1)[:, None, :]
    return t_mat, b_mat, c_mat, la, lb


def _to_groups(u, batch):
    t, w = u.shape
    g = w // SSM_GROUP
    nch = t // batch // SSM_CHUNK
    return (u.reshape(batch, nch, SSM_CHUNK, g, SSM_GROUP).transpose(3, 0, 1, 2, 4)
            .reshape(g, batch * nch, SSM_CHUNK * SSM_GROUP))


def _from_groups(ug, batch):
    g, n, _ = ug.shape
    nch = n // batch
    return (ug.reshape(g, batch, nch, SSM_CHUNK, SSM_GROUP).transpose(1, 2, 3, 0, 4)
            .reshape(batch * nch * SSM_CHUNK, g * SSM_GROUP))


def _s5_fwd(ug, t_mat, b_mat, c_mat, la, lb, *, batch, gb=8):
    groups, n, ch = ug.shape
    p2 = b_mat.shape[2]
    gb = _tile(groups, gb)
    nch = n // batch

    def body(u_ref, t_ref, b_ref, c_ref, la_ref, lb_ref, y_ref, x_ref, s_ref):
        for g in range(gb):
            s_ref[g] = jnp.dot(u_ref[g], b_ref[g], precision=HIGHEST, preferred_element_type=F32)

        def step(r, xs):
            new = []
            for g in range(gb):
                for b in range(batch):
                    x = xs[g * batch + b]
                    row = b * nch + r
                    x_ref[g, pl.ds(row, 1), :] = x
                    new.append(la_ref[g] * x + lb_ref[g] * pltpu.roll(x, p2 // 2, 1) + s_ref[g, pl.ds(row, 1), :])
            return tuple(new)

        lax.fori_loop(0, nch, step, tuple(jnp.zeros((1, p2), F32) for _ in range(gb * batch)))
        for g in range(gb):
            y_ref[g] = (jnp.dot(u_ref[g], t_ref[g], precision=HIGHEST, preferred_element_type=F32)
                        + jnp.dot(x_ref[g], c_ref[g], precision=HIGHEST, preferred_element_type=F32))

    def spec(a, b):
        return pl.BlockSpec((gb, a, b), lambda i: (i, 0, 0))

    return pl.pallas_call(
        body, name="s5_fwd", grid=(groups // gb,),
        in_specs=[spec(n, ch), spec(ch, ch), spec(ch, p2), spec(p2, ch), spec(1, p2), spec(1, p2)],
        out_specs=[spec(n, ch), spec(n, p2)],
        out_shape=[jax.ShapeDtypeStruct((groups, n, ch), F32), jax.ShapeDtypeStruct((groups, n, p2), F32)],
        scratch_shapes=[pltpu.VMEM((gb, n, p2), F32)], compiler_params=_params("parallel"),
    )(ug, t_mat, b_mat, c_mat, la, lb)


def _s5_bwd(ug, dyg, xin, t_mat, b_mat, c_mat, la, lb, *, batch, gb=8):
    groups, n, ch = ug.shape
    p2 = b_mat.shape[2]
    gb = _tile(groups, gb)
    nch = n // batch

    def body(u_ref, dy_ref, x_ref, t_ref, b_ref, c_ref, la_ref, lb_ref,
             du_ref, dt_ref, db_ref, dc_ref, dla_ref, dlb_ref, dx_ref, ds_ref):
        for g in range(gb):
            dx_ref[g] = _dot_nt(dy_ref[g], c_ref[g], precision=HIGHEST)
        last = []
        for g in range(gb):
            for b in range(batch):
                row = b * nch + nch - 1
                ds_ref[g, pl.ds(row, 1), :] = jnp.zeros((1, p2), F32)
                last.append(dx_ref[g, pl.ds(row, 1), :])

        def step(it, dxs):
            r = nch - 2 - it
            new = []
            for g in range(gb):
                for b in range(batch):
                    dxn = dxs[g * batch + b]
                    row = b * nch + r
                    ds_ref[g, pl.ds(row, 1), :] = dxn
                    new.append(dx_ref[g, pl.ds(row, 1), :] + la_ref[g] * dxn - lb_ref[g] * pltpu.roll(dxn, p2 // 2, 1))
            return tuple(new)

        lax.fori_loop(0, nch - 1, step, tuple(last))
        for g in range(gb):
            u, dy, ds, x = u_ref[g], dy_ref[g], ds_ref[g], x_ref[g]
            du_ref[g] = _dot_nt(dy, t_ref[g], precision=HIGHEST) + _dot_nt(ds, b_ref[g], precision=HIGHEST)
            dt_ref[g] = _dot_tn(u, dy, precision=HIGHEST)
            db_ref[g] = _dot_tn(u, ds, precision=HIGHEST)
            dc_ref[g] = _dot_tn(x, dy, precision=HIGHEST)
            dla_ref[g] = jnp.sum(ds * x, axis=0, keepdims=True)
            dlb_ref[g] = jnp.sum(ds * pltpu.roll(x, p2 // 2, 1), axis=0, keepdims=True)

    def spec(a, b):
        return pl.BlockSpec((gb, a, b), lambda i: (i, 0, 0))

    def shape(a, b):
        return jax.ShapeDtypeStruct((groups, a, b), F32)

    return pl.pallas_call(
        body, name="s5_bwd", grid=(groups // gb,),
        in_specs=[spec(n, ch), spec(n, ch), spec(n, p2), spec(ch, ch), spec(ch, p2), spec(p2, ch), spec(1, p2), spec(1, p2)],
        out_specs=[spec(n, ch), spec(ch, ch), spec(ch, p2), spec(p2, ch), spec(1, p2), spec(1, p2)],
        out_shape=[shape(n, ch), shape(ch, ch), shape(ch, p2), shape(p2, ch), shape(1, p2), shape(1, p2)],
        scratch_shapes=[pltpu.VMEM((gb, n, p2), F32), pltpu.VMEM((gb, n, p2), F32)],
        compiler_params=_params("parallel"),
    )(ug, dyg, xin, t_mat, b_mat, c_mat, la, lb)


def _block(ref, axis, j, size):
    start = j * size if isinstance(j, int) else pl.multiple_of(j * size, size)
    return ref.at[pl.ds(start, size), :] if axis == 0 else ref.at[:, pl.ds(start, size)]


def _gather_weights(blocks):
    n = len(blocks)
    hbm = pl.BlockSpec(memory_space=pl.ANY)

    def body(*refs):
        ins, outs = refs[:n], refs[n:2 * n]
        send_sems, recv_sems, local_sems = refs[2 * n:]
        c = lax.axis_index("c")
        my_chip = 2 * lax.axis_index("x") + lax.axis_index("y")
        for j in range(N_CHIPS):
            @pl.when(my_chip == j)
            def _(j=j):
                local, sends = [], []
                for w, (blk, axis) in enumerate(blocks):
                    size = blk.shape[axis]
                    mine = pltpu.make_async_copy(ins[w], _block(outs[w], axis, j, size), local_sems.at[w])
                    mine.start()
                    local.append(mine)
                for w, (blk, axis) in enumerate(blocks):
                    size = blk.shape[axis]
                    for peer in range(N_CHIPS):
                        if peer == j:
                            continue
                        send = pltpu.make_async_remote_copy(
                            src_ref=ins[w], dst_ref=_block(outs[w], axis, j, size),
                            send_sem=send_sems.at[w * N_CHIPS + peer], recv_sem=recv_sems.at[w * N_CHIPS + j],
                            device_id=(peer // 2, peer % 2, c), device_id_type=MESH)
                        send.start()
                        sends.append(send)
                for w, (blk, axis) in enumerate(blocks):
                    size = blk.shape[axis]
                    for peer in range(N_CHIPS):
                        if peer == j:
                            continue
                        pltpu.make_async_remote_copy(
                            src_ref=ins[w], dst_ref=_block(outs[w], axis, peer, size),
                            send_sem=send_sems.at[w * N_CHIPS + peer], recv_sem=recv_sems.at[w * N_CHIPS + peer],
                            device_id=(peer // 2, peer % 2, c), device_id_type=MESH).wait_recv()
                for cp in sends:
                    cp.wait_send()
                for cp in local:
                    cp.wait()

    out_shape = []
    for blk, axis in blocks:
        full = list(blk.shape)
        full[axis] *= N_CHIPS
        out_shape.append(jax.ShapeDtypeStruct(tuple(full), blk.dtype))
    return pl.pallas_call(
        body, name="gather_weights", in_specs=[hbm] * n, out_specs=[hbm] * n, out_shape=out_shape,
        scratch_shapes=[pltpu.SemaphoreType.DMA((n * N_CHIPS,)), pltpu.SemaphoreType.DMA((n * N_CHIPS,)),
                        pltpu.SemaphoreType.DMA((n,))],
    )(*[b for b, _ in blocks])


def _chip_exchange_copies(mode, axes, srcs, lands, send_sems, recv_sems, local_sems):
    x, y, c = lax.axis_index("x"), lax.axis_index("y"), lax.axis_index("c")
    me = 2 * x + y
    local, sends, arrivals = [], [], []
    for w, axis in enumerate(axes):
        if mode == "gather":
            size = srcs[w].shape[axis]
            local.append(pltpu.make_async_copy(srcs[w], _block(lands[w], axis, me, size), local_sems.at[w]))
        else:
            size = srcs[w].shape[axis] // N_CHIPS
            local.append(pltpu.make_async_copy(_block(srcs[w], axis, me, size), lands[w].at[me], local_sems.at[w]))
        for k in range(1, N_CHIPS):
            px = 1 - x if k & 2 else x
            py = 1 - y if k & 1 else y
            peer = 2 * px + py
            if mode == "gather":
                src, dst, arrive = srcs[w], _block(lands[w], axis, me, size), _block(lands[w], axis, peer, size)
            else:
                src, dst, arrive = _block(srcs[w], axis, peer, size), lands[w].at[me], lands[w].at[peer]
            sem = w * (N_CHIPS - 1) + k - 1
            for target, out in ((dst, sends), (arrive, arrivals)):
                out.append(pltpu.make_async_remote_copy(
                    src_ref=src, dst_ref=target, send_sem=send_sems.at[sem], recv_sem=recv_sems.at[sem],
                    device_id=(px, py, c), device_id_type=MESH))
    return local, sends, arrivals


def _chip_exchange_start(name, mode, items, after=None):
    n = len(items)
    n_after = 0 if after is None else 1
    axes = [axis for _, axis in items]
    hbm = pl.BlockSpec(memory_space=pltpu.HBM)
    sem = pl.BlockSpec(memory_space=pltpu.SEMAPHORE)
    lands = []
    for a, axis in items:
        shape = list(a.shape)
        if mode == "gather":
            shape[axis] *= N_CHIPS
        else:
            shape[axis] //= N_CHIPS
            shape = [N_CHIPS] + shape
        lands.append(pltpu.with_memory_space_constraint(lax.empty(tuple(shape), a.dtype), pltpu.HBM))

    def body(*refs):
        srcs, land_refs = refs[:n], refs[n:2 * n]
        send_sems, recv_sems, local_sems = refs[2 * n + n_after:2 * n + n_after + 3]
        token = refs[-1]
        local, sends, _ = _chip_exchange_copies(mode, axes, srcs, land_refs, send_sems, recv_sems, local_sems)
        for cp in local + sends:
            cp.start()
        token[...] = jnp.zeros_like(token)

    n_sem = n * (N_CHIPS - 1)
    outs = pl.pallas_call(
        body, name=name,
        out_shape=(pltpu.SemaphoreType.DMA((n_sem,)), pltpu.SemaphoreType.DMA((n_sem,)), pltpu.SemaphoreType.DMA((n,)),
                   *[pltpu.HBM(a.shape, a.dtype) for a, _ in items], *[pltpu.HBM(l.shape, l.dtype) for l in lands],
                   jax.ShapeDtypeStruct((8, LANES), F32)),
        in_specs=[hbm] * (2 * n) + [pl.BlockSpec(memory_space=pl.ANY)] * n_after,
        out_specs=(sem, sem, sem, *[hbm] * (2 * n), pl.BlockSpec(memory_space=pltpu.VMEM)),
        input_output_aliases={i: 3 + i for i in range(2 * n)},
        compiler_params=pltpu.CompilerParams(has_side_effects=pltpu.SideEffectType.DATAFLOW_SIDE_EFFECTING),
    )(*[pltpu.with_memory_space_constraint(a, pltpu.HBM) for a, _ in items], *lands, *([after] if n_after else []))
    return (mode, axes, outs[:3], outs[3:3 + n], outs[3 + n:3 + 2 * n]), outs[-1][0:1, 0:1]


def _chip_exchange_wait(name, handle, after):
    mode, axes, sems, srcs, lands = handle
    n = len(axes)
    hbm = pl.BlockSpec(memory_space=pltpu.HBM)
    sem = pl.BlockSpec(memory_space=pltpu.SEMAPHORE)

    def body(*refs):
        src_refs, land_refs = refs[:n], refs[n:2 * n]
        send_sems, recv_sems, local_sems = refs[2 * n:2 * n + 3]
        local, sends, arrivals = _chip_exchange_copies(mode, axes, src_refs, land_refs, send_sems, recv_sems, local_sems)
        for cp in sends:
            cp.wait_send()
        for cp in arrivals:
            cp.wait_recv()
        for cp in local:
            cp.wait()

    outs = pl.pallas_call(
        body, name=name,
        out_shape=(*[pltpu.HBM(a.shape, a.dtype) for a in srcs], *[pltpu.HBM(l.shape, l.dtype) for l in lands]),
        in_specs=[hbm] * (2 * n) + [sem] * 3 + [pl.BlockSpec(memory_space=pl.ANY)], out_specs=[hbm] * (2 * n),
        input_output_aliases={i: i for i in range(2 * n)},
        compiler_params=pltpu.CompilerParams(has_side_effects=pltpu.SideEffectType.DATAFLOW_SIDE_EFFECTING),
    )(*srcs, *lands, *sems, after)
    return outs[n:]


def _sum_slots(name, slots, tm=256):
    _, r, c = slots.shape
    tm = _tile(r, tm)

    def body(s0, s1, s2, s3, o_ref):
        o_ref[...] = ((s0[...] + s1[...]) + s2[...]) + s3[...]

    specs = [pl.BlockSpec((None, tm, c), functools.partial(lambda i, s: (s, i, 0), s=s)) for s in range(N_CHIPS)]
    return pl.pallas_call(
        body, name=name, grid=(r // tm,), in_specs=specs, out_specs=pl.BlockSpec((tm, c), lambda i: (i, 0)),
        out_shape=jax.ShapeDtypeStruct((r, c), F32), compiler_params=_params("parallel"),
    )(slots, slots, slots, slots)


def _swap_with_sibling(arrays):
    n = len(arrays)
    hbm = pl.BlockSpec(memory_space=pl.ANY)

    def body(*refs):
        ins, outs = refs[:n], refs[n:2 * n]
        send_sems, recv_sems = refs[2 * n:]
        sibling = (lax.axis_index("x"), lax.axis_index("y"), 1 - lax.axis_index("c"))
        copies = [pltpu.make_async_remote_copy(src_ref=ins[w], dst_ref=outs[w], send_sem=send_sems.at[w],
                                               recv_sem=recv_sems.at[w], device_id=sibling, device_id_type=MESH)
                  for w in range(n)]
        for cp in copies:
            cp.start()
        for cp in copies:
            cp.wait()

    return pl.pallas_call(
        body, name="swap_with_sibling", in_specs=[hbm] * n, out_specs=[hbm] * n,
        out_shape=[jax.ShapeDtypeStruct(a.shape, a.dtype) for a in arrays],
        scratch_shapes=[pltpu.SemaphoreType.DMA((n,)), pltpu.SemaphoreType.DMA((n,))],
    )(*arrays)


def _all_reduce_small(packed):
    r, lanes = packed.shape
    vmem = pl.BlockSpec(memory_space=pltpu.VMEM)

    def body(in_ref, out_ref, buf, send_sems, recv_sems):
        x, y, c = lax.axis_index("x"), lax.axis_index("y"), lax.axis_index("c")
        me = 4 * x + 2 * y + c
        buf[me] = in_ref[...]
        sends = []
        for d in range(N_DEV):
            @pl.when(me != d)
            def _(d=d):
                pltpu.make_async_remote_copy(
                    src_ref=in_ref, dst_ref=buf.at[me], send_sem=send_sems.at[d], recv_sem=recv_sems.at[me],
                    device_id=(d // 4, (d // 2) % 2, d % 2), device_id_type=MESH).start()
        for d in range(N_DEV):
            @pl.when(me != d)
            def _(d=d):
                cp = pltpu.make_async_remote_copy(
                    src_ref=in_ref, dst_ref=buf.at[d], send_sem=send_sems.at[d], recv_sem=recv_sems.at[d],
                    device_id=(d // 4, (d // 2) % 2, d % 2), device_id_type=MESH)
                cp.wait_recv()
                cp.wait_send()
        del sends
        acc = buf[0]
        for d in range(1, N_DEV):
            acc = acc + buf[d]
        out_ref[...] = acc

    return pl.pallas_call(
        body, name="all_reduce_small", in_specs=[vmem], out_specs=vmem,
        out_shape=jax.ShapeDtypeStruct((r, lanes), F32),
        scratch_shapes=[pltpu.VMEM((N_DEV, r, lanes), F32), pltpu.SemaphoreType.DMA((N_DEV,)),
                        pltpu.SemaphoreType.DMA((N_DEV,))],
        compiler_params=pltpu.CompilerParams(vmem_limit_bytes=VMEM_LIMIT),
    )(packed)


def _adamw(g, w, m, v):
    m = ADAM_B1 * m + (1.0 - ADAM_B1) * g
    v = ADAM_B2 * v + (1.0 - ADAM_B2) * jnp.square(g)
    m_hat = m / (1.0 - ADAM_B1 ** ADAM_STEP)
    v_hat = v / (1.0 - ADAM_B2 ** ADAM_STEP)
    delta = -ADAM_LR * (m_hat / (jnp.sqrt(v_hat) + ADAM_EPS) + ADAM_WD * w)
    return delta, m, v


def kernel(x, norm1_g, w_in, q_norm_g, k_norm_g, ssm_lambda_re, ssm_lambda_im, ssm_log_dt, ssm_b_re, ssm_b_im, ssm_c_re, ssm_c_im, ssm_d, w_glu, b_glu, attn_out_g, ssm_out_g, w_out, norm2_g, w_mlp_in, w_mlp_out, loss_target, m_norm1_g, m_w_in, m_q_norm_g, m_k_norm_g, m_ssm_lambda_re, m_ssm_lambda_im, m_ssm_log_dt, m_ssm_b_re, m_ssm_b_im, m_ssm_c_re, m_ssm_c_im, m_ssm_d, m_w_glu, m_b_glu, m_attn_out_g, m_ssm_out_g, m_w_out, m_norm2_g, m_w_mlp_in, m_w_mlp_out, v_norm1_g, v_w_in, v_q_norm_g, v_k_norm_g, v_ssm_lambda_re, v_ssm_lambda_im, v_ssm_log_dt, v_ssm_b_re, v_ssm_b_im, v_ssm_c_re, v_ssm_c_im, v_ssm_d, v_w_glu, v_b_glu, v_attn_out_g, v_ssm_out_g, v_w_out, v_norm2_g, v_w_mlp_in, v_w_mlp_out):
    batch, seq, d_model = x.shape
    tokens = batch * seq
    sb_width = w_in.shape[1]
    n_features = d_model

    big = [("w_in", w_in, m_w_in, v_w_in, 1), ("w_glu", w_glu, m_w_glu, v_w_glu, 0),
           ("w_out", w_out, m_w_out, v_w_out, 0), ("w_mlp_in", w_mlp_in, m_w_mlp_in, v_w_mlp_in, 1),
           ("w_mlp_out", w_mlp_out, m_w_mlp_out, v_w_mlp_out, 0)]
    small = [("norm1_g", norm1_g, m_norm1_g, v_norm1_g), ("q_norm_g", q_norm_g, m_q_norm_g, v_q_norm_g),
             ("k_norm_g", k_norm_g, m_k_norm_g, v_k_norm_g),
             ("ssm_lambda_re", ssm_lambda_re, m_ssm_lambda_re, v_ssm_lambda_re),
             ("ssm_lambda_im", ssm_lambda_im, m_ssm_lambda_im, v_ssm_lambda_im),
             ("ssm_log_dt", ssm_log_dt, m_ssm_log_dt, v_ssm_log_dt),
             ("ssm_b_re", ssm_b_re, m_ssm_b_re, v_ssm_b_re), ("ssm_b_im", ssm_b_im, m_ssm_b_im, v_ssm_b_im),
             ("ssm_c_re", ssm_c_re, m_ssm_c_re, v_ssm_c_re), ("ssm_c_im", ssm_c_im, m_ssm_c_im, v_ssm_c_im),
             ("ssm_d", ssm_d, m_ssm_d, v_ssm_d), ("b_glu", b_glu, m_b_glu, v_b_glu),
             ("attn_out_g", attn_out_g, m_attn_out_g, v_attn_out_g), ("ssm_out_g", ssm_out_g, m_ssm_out_g, v_ssm_out_g),
             ("norm2_g", norm2_g, m_norm2_g, v_norm2_g)]

    gather_in, tok_in = _chip_exchange_start("gather_w_in_start", "gather", [(w_in.astype(BF16), 1)])
    gather_rest, tok_rest = _chip_exchange_start(
        "gather_rest_start", "gather", [(w.astype(BF16), axis) for _, w, _, _, axis in big[1:]], after=tok_in)

    x2 = x.reshape(tokens, d_model)
    tgt2 = loss_target.reshape(tokens, d_model)
    g1, g2 = norm1_g[None, :], norm2_g[None, :]
    g_attn, g_ssm, bias_glu = attn_out_g[None, :], ssm_out_g[None, :], b_glu[None, :]
    heads = sb_width // HEAD_DIM
    qk_scale = 1.0 / math.sqrt(HEAD_DIM)
    gq, gk = (jnp.tile(q_norm_g, heads) * qk_scale)[None, :], jnp.tile(k_norm_g, heads)[None, :]
    lane_head = jnp.arange(sb_width) // HEAD_DIM
    ones_blocks = (lane_head[:, None] == lane_head[None, :]).astype(F32)

    (xn,) = _rowwise("norm1", _rms, [x2], [g1 + tok_rest], [(d_model, BF16)])
    (wf_in,) = _chip_exchange_wait("gather_w_in_wait", gather_in, xn)
    (proj,) = _mm("proj_in", xn, wf_in, "nn")

    def qkv_fn(q, k, v, gq_, gk_, ones):
        return _head_rms(q, gq_, ones), _head_rms(k, gk_, ones), v

    qn, kn, vb = _rowwise("qk_norm", qkv_fn, [(proj, sb_width, 0), (proj, sb_width, 1), (proj, sb_width, 2)],
                          [gq, gk, ones_blocks], [(sb_width, BF16)] * 3)
    sb, c_tot = _attn_fwd(qn, kn, vb, batch=batch, seq=seq, bq=256)

    s5_params = (ssm_lambda_re, ssm_lambda_im, ssm_log_dt, ssm_b_re, ssm_b_im, ssm_c_re, ssm_c_im, ssm_d)
    (t_mat, b_mat, c_mat, la, lb), s5_vjp = jax.vjp(_s5_operators, *s5_params)
    ug = _to_groups(proj[:, 3 * sb_width:], batch)
    yg, xin = _s5_fwd(ug, t_mat, b_mat, c_mat, la, lb, batch=batch)
    y_ssm = _from_groups(yg, batch)

    (y_act,) = _rowwise("gelu", _gelu, [y_ssm], [], [(sb_width, BF16)])
    wf_glu, wf_out, wf_mlp_in, wf_mlp_out = _chip_exchange_wait("gather_rest_wait", gather_rest, y_act)
    (gate_pre,) = _mm("glu_gate", y_act, wf_glu, "nn", extras=[(bias_glu, "row")], epilogue=lambda acc, b: acc + b)
    (mixed,) = _rowwise("mix_norm", _mixed, [sb, y_ssm, gate_pre], [g_attn, g_ssm], [(2 * sb_width, BF16)])
    (h1,) = _mm("proj_out", mixed, wf_out, "nn", extras=[(x2, "tile")], epilogue=lambda acc, r: acc + r)
    (hn,) = _rowwise("norm2", _rms, [h1], [g2], [(d_model, BF16)])
    (pre,) = _mm("mlp_in", hn, wf_mlp_in, "nn", tk=1024)
    inv_n = 1.0 / n_features
    (dy,) = _mm("mlp_out_loss", pre, wf_mlp_out, "nn", a_fn=_relu_sq, extras=[(h1, "tile"), (tgt2, "tile")],
                epilogue=lambda acc, r, t: ((acc + r) - t) * inv_n)

    def loss_fn(d):
        return d, jnp.sum(d * d, keepdims=True) * (0.5 * n_features)

    _, loss_part = _rowwise("loss", loss_fn, [dy], [], [(d_model, F32)], [(1, 1)])

    (dw_mlp_out,) = _mm("dw_mlp_out", pre, dy, "tn", a_fn=_relu_sq)
    (dpre,) = _mm("d_mlp_act", dy, wf_mlp_out, "nt", extras=[(pre, "tile")],
                  epilogue=lambda acc, p: acc * (2.0 * jnp.maximum(p, 0.0)), out_dtypes=(BF16,))
    (dw_mlp_in,) = _mm("dw_mlp_in", hn, dpre, "tn", tk=1024)
    scatter_mlp, tok_mlp = _chip_exchange_start("scatter_mlp_start", "scatter", [(dw_mlp_in, 1), (dw_mlp_out, 0)])
    (dhn,) = _mm("d_norm2_in", dpre, wf_mlp_in, "nt", tk=1024)

    def norm_bwd(res, hx, dn, g):
        _, vjp = jax.vjp(_rms, hx, g)
        dh, dg = vjp(dn)
        return res + dh, dg

    dh1, dg_norm2 = _rowwise("norm2_bwd", norm_bwd, [dy, h1, dhn], [g2 + tok_mlp], [(d_model, F32)], [(1, d_model)])
    (dmixed,) = _mm("d_mixed", dh1, wf_out, "nt")
    (dw_out,) = _mm("dw_out", mixed, dh1, "tn")

    def mixed_bwd(dm, sb_, ys, gp, ga, gs):
        _, vjp = jax.vjp(lambda a, act, b, c, d: jnp.concatenate(
            [_rms(a, c), _rms(act * jax.nn.sigmoid(b), d)], axis=-1), sb_, _gelu(ys), gp, ga, gs)
        dsb_, dact, dgp_, dga, dgs = vjp(dm)
        return dsb_, dgp_, dact, dga, dgs, jnp.sum(dgp_, axis=0, keepdims=True)

    dsb, dgate_pre, dact_part, dg_attn, dg_ssm, db_glu = _rowwise(
        "mix_norm_bwd", mixed_bwd, [dmixed, sb, y_ssm, gate_pre], [g_attn, g_ssm],
        [(sb_width, F32), (sb_width, BF16), (sb_width, F32)], [(1, sb_width)] * 3)

    def gelu_bwd(acc, part, ys):
        _, vjp = jax.vjp(_gelu, ys)
        return vjp(acc + part)[0]

    (dy_ssm,) = _mm("d_glu_in", dgate_pre, wf_glu, "nt", extras=[(dact_part, "tile"), (y_ssm, "tile")], epilogue=gelu_bwd)
    (dw_glu,) = _mm("dw_glu", y_act, dgate_pre, "tn")
    scatter_mix, tok_mix = _chip_exchange_start("scatter_mix_start", "scatter", [(dw_glu, 0), (dw_out, 0)])

    dug, dt_mat, db_mat, dc_mat, dla, dlb = _s5_bwd(ug, _to_groups(dy_ssm, batch), xin, t_mat, b_mat, c_mat, la,
                                                    lb + tok_mix, batch=batch)
    du = _from_groups(dug, batch)
    ds5 = s5_vjp((dt_mat, db_mat, dc_mat, dla, dlb))

    dqn, dkn, dv = _attn_bwd(qn, kn, vb, c_tot, dsb, batch=batch, seq=seq, bq=256)

    def qk_bwd(q, k, dq_, dk_, gq_, gk_, ones):
        _, vjp_q = jax.vjp(lambda a, g: _head_rms(a, g, ones), q, gq_)
        _, vjp_k = jax.vjp(lambda a, g: _head_rms(a, g, ones), k, gk_)
        dq, dgq = vjp_q(dq_)
        dk, dgk = vjp_k(dk_)
        return dq, dk, dgq, dgk

    dq, dk, dgq, dgk = _rowwise("qk_norm_bwd", qk_bwd, [(proj, sb_width, 0), (proj, sb_width, 1), dqn, dkn],
                                [gq, gk, ones_blocks], [(sb_width, BF16)] * 2, [(1, sb_width)] * 2)
    dproj = jnp.concatenate([dq, dk, dv.astype(BF16), du.astype(BF16)], axis=1)
    (dw_in,) = _mm("dw_in", xn, dproj, "tn")
    scatter_in, tok_w_in = _chip_exchange_start("scatter_in_start", "scatter", [(dw_in, 1)])
    (dxn,) = _mm("d_norm1_in", dproj, wf_in, "nt")
    dx, dg_norm1 = _rowwise("norm1_bwd", norm_bwd, [dh1, x2, dxn], [g1 + tok_w_in], [(d_model, F32)], [(1, d_model)])

    slots_mlp_in, slots_mlp_out = _chip_exchange_wait("scatter_mlp_wait", scatter_mlp, dg_norm1)
    slots_glu, slots_out = _chip_exchange_wait("scatter_mix_wait", scatter_mix, dg_norm1)
    (slots_in,) = _chip_exchange_wait("scatter_in_wait", scatter_in, dg_norm1)
    slots = [slots_in, slots_glu, slots_out, slots_mlp_in, slots_mlp_out]
    mine = [_sum_slots("sum_" + name, s) for s, (name, *_rest) in zip(slots, big)]
    theirs = _swap_with_sibling(mine)

    small_grads = [dg_norm1[0], dgq.reshape(heads, HEAD_DIM).sum(0) * qk_scale, dgk.reshape(heads, HEAD_DIM).sum(0), *ds5,
                   db_glu[0], dg_attn[0], dg_ssm[0], dg_norm2[0]]
    order = ["norm1_g", "q_norm_g", "k_norm_g", "ssm_lambda_re", "ssm_lambda_im", "ssm_log_dt", "ssm_b_re", "ssm_b_im",
             "ssm_c_re", "ssm_c_im", "ssm_d", "b_glu", "attn_out_g", "ssm_out_g", "norm2_g"]
    assert order == [name for name, *_ in small]

    def pack(parts, extra=None):
        flat = [p.reshape(-1) for p in parts] + ([extra.reshape(-1)] if extra is not None else [])
        flat = jnp.concatenate(flat)
        rows = -(-flat.shape[0] // (LANES * LANES)) * LANES
        return jnp.pad(flat, (0, rows * LANES - flat.shape[0])).reshape(rows, LANES)

    n_small = sum(w.size for _, w, _, _ in small)
    reduced = _all_reduce_small(pack(small_grads, loss_part))
    loss = reduced.reshape(-1)[n_small]

    def adam_big(sa, sb_, w, m, v):
        g = sa + sb_
        delta, m, v = _adamw(g, w, m, v)
        return g, delta, m, v

    def adam_small(g, w, m, v):
        delta, m, v = _adamw(g, w, m, v)
        return delta, m, v

    big_out = {}
    for (name, w, m, v, _), sa, sb_ in zip(big, mine, theirs):
        big_out[name] = _rowwise("adamw_" + name, adam_big, [sa, sb_, w, m, v], [], [(w.shape[1], F32)] * 4)
    ones_pad = jnp.ones((1,), F32)
    small_upd = _rowwise("adamw_small", adam_small,
                         [reduced, pack([w for _, w, _, _ in small], ones_pad * 0), pack([m for _, _, m, _ in small], ones_pad * 0),
                          pack([v for _, _, _, v in small], ones_pad)], [], [(LANES, F32)] * 3)

    def unpack(packed):
        flat, out, off = packed.reshape(-1), {}, 0
        for name, w, _, _ in small:
            out[name] = flat[off:off + w.size].reshape(w.shape)
            off += w.size
        return out

    small_out = [unpack(reduced)] + [unpack(p) for p in small_upd]
    names = ["norm1_g", "w_in", "q_norm_g", "k_norm_g", "ssm_lambda_re", "ssm_lambda_im", "ssm_log_dt", "ssm_b_re",
             "ssm_b_im", "ssm_c_re", "ssm_c_im", "ssm_d", "w_glu", "b_glu", "attn_out_g", "ssm_out_g", "w_out",
             "norm2_g", "w_mlp_in", "w_mlp_out"]
    outs = [loss, dx.reshape(batch, seq, d_model)]
    for kind in range(4):
        for name in names:
            outs.append(big_out[name][kind] if name in big_out else small_out[kind][name])
    return tuple(outs)
```

```python
import functools
import math

import jax
import jax.numpy as jnp
from jax import lax
from jax.experimental import pallas as pl
from jax.experimental.pallas import tpu as pltpu

F32 = jnp.float32
BF16 = jnp.bfloat16
HIGHEST = lax.Precision.HIGHEST
MESH = pl.DeviceIdType.MESH

RMS_EPS = 1e-6
HEAD_DIM = 64
SSM_GROUP = 16
SSM_CHUNK = 16
LANES = 128
N_CHIPS = 4
N_DEV = 8
VMEM_LIMIT = 48 * 1024 * 1024

ADAM_LR = 0.001
ADAM_B1 = 0.9
ADAM_B2 = 0.999
ADAM_EPS = 1e-08
ADAM_WD = 0.01
ADAM_STEP = 10


def _tile(n, pref):
    t = min(n, pref)
    while n % t:
        t //= 2
    return t


def _params(*sem):
    return pltpu.CompilerParams(dimension_semantics=sem, vmem_limit_bytes=VMEM_LIMIT)


_DIMS = {"nn": (((1,), (0,)), ((), ())), "nt": (((1,), (1,)), ((), ())), "tn": (((0,), (0,)), ((), ()))}


def _mm(name, a, b, mode, *, tm=1024, tn=1024, tk=512, a_fn=None, extras=(), epilogue=None, out_dtypes=(F32,)):
    if mode == "nn":
        (m, k), n = a.shape, b.shape[1]
    elif mode == "nt":
        (m, k), n = a.shape, b.shape[0]
    else:
        (k, m), n = a.shape, b.shape[1]
    tm, tn, tk = _tile(m, tm), _tile(n, tn), _tile(k, tk)
    nk = k // tk
    ne, nout = len(extras), len(out_dtypes)
    dims = _DIMS[mode]

    def body(a_ref, b_ref, *rest):
        ex, outs = rest[:ne], rest[ne:ne + nout]
        at = a_ref[...]
        if a_fn is not None:
            at = a_fn(at)
        p = lax.dot_general(at.astype(BF16), b_ref[...].astype(BF16), dims, preferred_element_type=F32)

        def finish(r):
            if epilogue is not None:
                r = epilogue(r, *[e[...] for e in ex])
            if not isinstance(r, (tuple, list)):
                r = (r,)
            for o, v in zip(outs, r):
                o[...] = v.astype(o.dtype)

        if nk == 1:
            finish(p)
        else:
            acc = rest[ne + nout]
            kk = pl.program_id(2)

            @pl.when(kk == 0)
            def _():
                acc[...] = p

            @pl.when(kk > 0)
            def _():
                acc[...] += p

            @pl.when(kk == nk - 1)
            def _():
                finish(acc[...])

    if mode == "tn":
        a_spec = pl.BlockSpec((tk, tm), lambda i, j, kk: (kk, i))
    else:
        a_spec = pl.BlockSpec((tm, tk), lambda i, j, kk: (i, kk))
    if mode == "nt":
        b_spec = pl.BlockSpec((tn, tk), lambda i, j, kk: (j, kk))
    else:
        b_spec = pl.BlockSpec((tk, tn), lambda i, j, kk: (kk, j))
    ex_specs = []
    for _, kind in extras:
        if kind == "tile":
            ex_specs.append(pl.BlockSpec((tm, tn), lambda i, j, kk: (i, j)))
        else:
            ex_specs.append(pl.BlockSpec((1, tn), lambda i, j, kk: (0, j)))
    return pl.pallas_call(
        body, name=name, grid=(m // tm, n // tn, nk),
        in_specs=[a_spec, b_spec] + ex_specs,
        out_specs=[pl.BlockSpec((tm, tn), lambda i, j, kk: (i, j)) for _ in out_dtypes],
        out_shape=[jax.ShapeDtypeStruct((m, n), dt) for dt in out_dtypes],
        scratch_shapes=[pltpu.VMEM((tm, tn), F32)] if nk > 1 else [],
        compiler_params=_params("parallel", "parallel", "arbitrary"),
    )(a, b, *[e for e, _ in extras])


def _rowwise(name, fn, rows, consts, row_outs, acc_outs=(), tm=256):
    norm = [r if isinstance(r, tuple) else (r, r.shape[1], 0) for r in rows]
    t = norm[0][0].shape[0]
    tm = _tile(t, tm)
    nr, nc, no = len(norm), len(consts), len(row_outs)

    def body(*refs):
        outs = fn(*[r[...] for r in refs[:nr + nc]])
        if not isinstance(outs, (tuple, list)):
            outs = (outs,)
        o_refs, a_refs = refs[nr + nc:nr + nc + no], refs[nr + nc + no:]
        for r, v in zip(o_refs, outs[:no]):
            r[...] = v.astype(r.dtype)
        if a_refs:
            i = pl.program_id(0)

            @pl.when(i == 0)
            def _():
                for r, v in zip(a_refs, outs[no:]):
                    r[...] = v

            @pl.when(i > 0)
            def _():
                for r, v in zip(a_refs, outs[no:]):
                    r[...] += v

    in_specs = [pl.BlockSpec((tm, w), functools.partial(lambda i, cb: (i, cb), cb=cb)) for _, w, cb in norm]
    in_specs += [pl.BlockSpec(c.shape, functools.partial(lambda i, nd: (0,) * nd, nd=c.ndim)) for c in consts]
    out_specs = [pl.BlockSpec((tm, w), lambda i: (i, 0)) for w, _ in row_outs]
    out_specs += [pl.BlockSpec(s, functools.partial(lambda i, nd: (0,) * nd, nd=len(s))) for s in acc_outs]
    out_shape = [jax.ShapeDtypeStruct((t, w), dt) for w, dt in row_outs]
    out_shape += [jax.ShapeDtypeStruct(s, F32) for s in acc_outs]
    return pl.pallas_call(
        body, name=name, grid=(t // tm,), in_specs=in_specs, out_specs=out_specs, out_shape=out_shape,
        compiler_params=_params("arbitrary"),
    )(*[r[0] for r in norm], *consts)


def _rms(x, g):
    return x * lax.rsqrt(jnp.mean(x * x, axis=-1, keepdims=True) + RMS_EPS) * g


def _head_rms(x, g, ones_blocks):
    ss = jnp.dot(x * x, ones_blocks, precision=HIGHEST, preferred_element_type=F32)
    return x * lax.rsqrt(ss * (1.0 / HEAD_DIM) + RMS_EPS) * g


def _gelu(x):
    return x * (0.5 * (1.0 + jnp.tanh(math.sqrt(2.0 / math.pi) * (x + 0.044715 * (x * x * x)))))


def _relu_sq(x):
    r = jnp.maximum(x, 0.0)
    return r * r


def _mixed(sb, y_ssm, gate_pre, g_attn, g_ssm):
    ssm = _gelu(y_ssm) * jax.nn.sigmoid(gate_pre)
    return jnp.concatenate([_rms(sb, g_attn), _rms(ssm, g_ssm)], axis=-1)


def _softplus(z):
    return jnp.maximum(z, 0.0) + jnp.log(1.0 + jnp.exp(-jnp.abs(z)))


def _suffix_sums(x, tri):
    hi = x.astype(BF16)
    lo = (x - hi.astype(F32)).astype(BF16)
    return (jnp.dot(hi, tri, preferred_element_type=F32) + jnp.dot(lo, tri, preferred_element_type=F32))


def _dot_nt(a, b, **kw):
    return lax.dot_general(a, b, _DIMS["nt"], preferred_element_type=F32, **kw)


def _dot_tn(a, b, **kw):
    return lax.dot_general(a, b, _DIMS["tn"], preferred_element_type=F32, **kw)


ATTN_BQ, ATTN_BK = 1024, 256
HEAD_LANES = tuple(slice(h * HEAD_DIM, (h + 1) * HEAD_DIM) for h in range(LANES // HEAD_DIM))


def _attn_fwd(qs, kn, v, *, batch, seq, bq, bk):
    width = qs.shape[1]
    bq = _tile(seq, bq)
    bk = _tile(bq, bk)
    nq, kpq = seq // bq, bq // bk

    def body(q_ref, k_ref, v_ref, o_ref, c_ref):
        row = lax.broadcasted_iota(jnp.int32, (bq, bk), 0)
        col = lax.broadcasted_iota(jnp.int32, (bq, bk), 1)
        tri = (lax.broadcasted_iota(jnp.int32, (bk, bk), 0) >= lax.broadcasted_iota(jnp.int32, (bk, bk), 1)).astype(BF16)

        def q_block(qi, carry):
            r0 = pl.multiple_of(qi * bq, bq)
            qh = [q_ref[pl.ds(r0, bq), ln] for ln in HEAD_LANES]

            def tile(k0, state, top=0):
                diag = top is not None
                top = top or 0
                msk = (col < row)[:bq - top] if diag else None
                new = []
                for h, ln in enumerate(HEAD_LANES):
                    o, c = state[2 * h], state[2 * h + 1]
                    z = _dot_nt(qh[h][top:], k_ref[pl.ds(k0, bk), ln])
                    sp = _softplus(z)
                    if diag:
                        sp = jnp.where(msk, sp, 0.0)
                    r = _suffix_sums(sp, tri)
                    a = jnp.exp(z - r - c[top:])
                    if diag:
                        a = jnp.where(msk, a, 0.0)
                    o_new = o[top:] + jnp.dot(a.astype(BF16), v_ref[pl.ds(k0, bk), ln], preferred_element_type=F32)
                    c_new = c[top:] + r[:, 0:1]
                    if top:
                        o_new, c_new = jnp.concatenate([o[:top], o_new]), jnp.concatenate([c[:top], c_new])
                    new += [o_new, c_new]
                return tuple(new)

            state = (jnp.zeros((bq, HEAD_DIM), F32), jnp.zeros((bq, 1), F32)) * len(HEAD_LANES)
            for d in reversed(range(kpq)):
                state = tile(pl.multiple_of(r0 + d * bk, bk), state, top=d * bk)
            state = lax.fori_loop(0, qi * kpq, lambda it, st: tile(pl.multiple_of(r0 - (it + 1) * bk, bk), st, None),
                                  state)
            for h, ln in enumerate(HEAD_LANES):
                o_ref[pl.ds(r0, bq), ln] = state[2 * h]
                c_ref[pl.ds(r0, bq), ln] = jnp.broadcast_to(state[2 * h + 1], (bq, HEAD_DIM))
            return carry

        lax.fori_loop(0, nq, q_block, 0)

    spec = pl.BlockSpec((seq, LANES), lambda b, h: (b, h))
    shape = jax.ShapeDtypeStruct((batch * seq, width), F32)
    return pl.pallas_call(
        body, name="attn_fwd", grid=(batch, width // LANES), in_specs=[spec, spec, spec], out_specs=[spec, spec],
        out_shape=[shape, shape], compiler_params=_params("parallel", "parallel"),
    )(qs, kn, v)


def _attn_bwd(qs, kn, v, c_tot, do, *, batch, seq, bq, bk):
    width = qs.shape[1]
    bq = _tile(seq, bq)
    bk = _tile(bq, bk)
    nq, kpq = seq // bq, bq // bk

    def body(q_ref, k_ref, v_ref, c_ref, do_ref, dq_ref, dk_ref, dv_ref):
        row = lax.broadcasted_iota(jnp.int32, (bq, bk), 0)
        col = lax.broadcasted_iota(jnp.int32, (bq, bk), 1)
        sq_row = lax.broadcasted_iota(jnp.int32, (bk, bk), 0)
        sq_col = lax.broadcasted_iota(jnp.int32, (bk, bk), 1)
        tri = (sq_row >= sq_col).astype(BF16)
        tri_t = (sq_row <= sq_col).astype(BF16)
        dk_ref[...] = jnp.zeros_like(dk_ref)
        dv_ref[...] = jnp.zeros_like(dv_ref)

        def q_block(qi, carry):
            r0 = pl.multiple_of(qi * bq, bq)
            qh = [q_ref[pl.ds(r0, bq), ln] for ln in HEAD_LANES]
            d_out = [do_ref[pl.ds(r0, bq), ln].astype(BF16) for ln in HEAD_LANES]
            c_all = [c_ref[pl.ds(r0, bq), ln][:, 0:1] for ln in HEAD_LANES]

            def tile(k0, state, top=0):
                diag = top is not None
                top = top or 0
                last = diag and top == bq - bk
                msk = (col < row)[:bq - top] if diag else None
                new = []
                for h, ln in enumerate(HEAD_LANES):
                    c_left, g_left, dq = state[3 * h:3 * h + 3]
                    q, d_o = qh[h][top:], d_out[h][top:]
                    k = k_ref[pl.ds(k0, bk), ln]
                    z = _dot_nt(q, k)
                    e = jnp.exp(-jnp.abs(z))
                    sp = jnp.maximum(z, 0.0) + jnp.log(1.0 + e)
                    if diag:
                        sp = jnp.where(msk, sp, 0.0)
                    r = _suffix_sums(sp, tri)
                    c_new = c_left[top:] + r[:, 0:1]
                    a = jnp.exp(z - r - (0.0 if last else c_all[h][top:] - c_new))
                    if diag:
                        a = jnp.where(msk, a, 0.0)
                    g = a * _dot_nt(d_o, v_ref[pl.ds(k0, bk), ln])
                    pg = _suffix_sums(g, tri_t)
                    sig = jnp.where(z >= 0.0, 1.0, e) / (1.0 + e)
                    dz = g - sig * (g_left[top:] + pg)
                    if diag:
                        dz = jnp.where(msk, dz, 0.0)
                    dz = dz.astype(BF16)
                    dk_ref[pl.ds(k0, bk), ln] += _dot_tn(dz, q)
                    dv_ref[pl.ds(k0, bk), ln] += _dot_tn(a.astype(BF16), d_o)
                    g_new = g_left[top:] + pg[:, bk - 1:bk]
                    dq_new = dq[top:] + jnp.dot(dz, k, preferred_element_type=F32)
                    if top:
                        c_new = jnp.concatenate([c_left[:top], c_new])
                        g_new = jnp.concatenate([g_left[:top], g_new])
                        dq_new = jnp.concatenate([dq[:top], dq_new])
                    new += [c_new, g_new, dq_new]
                return tuple(new)

            zero = jnp.zeros((bq, 1), F32)
            init = (zero, zero, jnp.zeros((bq, HEAD_DIM), F32)) * len(HEAD_LANES)
            state = lax.fori_loop(0, qi * kpq, lambda it, st: tile(pl.multiple_of(it * bk, bk), st, None), init)
            for d in range(kpq):
                state = tile(pl.multiple_of(r0 + d * bk, bk), state, top=d * bk)
            for h, ln in enumerate(HEAD_LANES):
                dq_ref[pl.ds(r0, bq), ln] = state[3 * h + 2]
            return carry

        lax.fori_loop(0, nq, q_block, 0)

    spec = pl.BlockSpec((seq, LANES), lambda b, h: (b, h))
    shape = jax.ShapeDtypeStruct((batch * seq, width), F32)
    return pl.pallas_call(
        body, name="attn_bwd", grid=(batch, width // LANES), in_specs=[spec] * 5, out_specs=[spec] * 3,
        out_shape=[shape] * 3, compiler_params=_params("parallel", "parallel"),
    )(qs, kn, v, c_tot, do)


def _s5_operators(lam_re, lam_im, log_dt, b_re, b_im, c_re, c_im, d_skip):
    groups, n_state, n_ch = b_re.shape
    cs = SSM_CHUNK
    dt = jnp.exp(log_dt)[:, None]
    steps = jnp.arange(cs + 1, dtype=F32)[None, :, None]
    mag = jnp.exp(steps * (lam_re * dt)[:, None, :])
    ang = steps * (lam_im * dt)[:, None, :]
    pw_re, pw_im = mag * jnp.cos(ang), mag * jnp.sin(ang)
    num_re, num_im = pw_re[:, 1] - 1.0, pw_im[:, 1]
    den = lam_re * lam_re + lam_im * lam_im
    cf_re = (num_re * lam_re + num_im * lam_im) / den
    cf_im = (num_im * lam_re - num_re * lam_im) / den
    bb_re = cf_re[:, :, None] * b_re - cf_im[:, :, None] * b_im
    bb_im = cf_re[:, :, None] * b_im + cf_im[:, :, None] * b_re
    width = cs * n_ch
    ct_re, ct_im = c_re.transpose(0, 2, 1), c_im.transpose(0, 2, 1)

    def c_times_powers(first):
        pr = pw_re[:, first:first + cs].transpose(0, 2, 1)[:, :, :, None]
        pi = pw_im[:, first:first + cs].transpose(0, 2, 1)[:, :, :, None]
        re = pr * ct_re[:, :, None, :] - pi * ct_im[:, :, None, :]
        im = pr * ct_im[:, :, None, :] + pi * ct_re[:, :, None, :]
        return re.reshape(groups, n_state, width), im.reshape(groups, n_state, width)

    w_re, w_im = c_times_powers(0)
    kt_row = (jnp.einsum("gpi,gpw->giw", bb_re, w_re, precision=HIGHEST)
              - jnp.einsum("gpi,gpw->giw", bb_im, w_im, precision=HIGHEST))
    kt_row = kt_row + jnp.pad(jnp.eye(n_ch, dtype=F32)[None] * d_skip[:, None, :], ((0, 0), (0, 0), (0, width - n_ch)))
    padded = jnp.pad(kt_row, ((0, 0), (0, 0), (0, width + n_ch)))
    skew = jnp.tile(padded, (1, 1, cs))[:, :, :cs * 2 * width].reshape(groups, n_ch, cs, 2 * width)[..., :width]
    t_mat = skew.transpose(0, 2, 1, 3).reshape(groups, width, width)
    rp_re, rp_im = pw_re[:, cs - 1::-1][:, :cs], pw_im[:, cs - 1::-1][:, :cs]
    bm_re = rp_re[:, :, None, :] * bb_re.transpose(0, 2, 1)[:, None] - rp_im[:, :, None, :] * bb_im.transpose(0, 2, 1)[:, None]
    bm_im = rp_re[:, :, None, :] * bb_im.transpose(0, 2, 1)[:, None] + rp_im[:, :, None, :] * bb_re.transpose(0, 2, 1)[:, None]
    b_mat = jnp.concatenate([bm_re, bm_im], axis=-1).reshape(groups, width, 2 * n_state)
    w1_re, w1_im = c_times_powers(1)
    c_mat = jnp.concatenate([w1_re, -w1_im], axis=1)
    la = jnp.concatenate([pw_re[:, cs], pw_re[:, cs]], axis=-1)[:, None, :]
    lb = jnp.concatenate([-pw_im[:, cs], pw_im[:, cs]], axis=-1)[:, None, :]
    return t_mat, b_mat, c_mat, la, lb


GROUPS_PER_BLOCK = LANES // SSM_GROUP


def _tokens_to_groups(name, u, col_block, width):
    t = u.shape[0]
    n = t // SSM_CHUNK
    ch = SSM_CHUNK * SSM_GROUP
    blocks = width // LANES

    def body(u_ref, o_ref):
        for s in range(SSM_CHUNK):
            rows = u_ref[pl.ds(s, n, stride=SSM_CHUNK), :]
            for g in range(GROUPS_PER_BLOCK):
                o_ref[g, :, s * SSM_GROUP:(s + 1) * SSM_GROUP] = rows[:, g * SSM_GROUP:(g + 1) * SSM_GROUP]

    return pl.pallas_call(
        body, name=name, grid=(blocks,),
        in_specs=[pl.BlockSpec((t, LANES), lambda j: (0, col_block * blocks + j))],
        out_specs=pl.BlockSpec((GROUPS_PER_BLOCK, n, ch), lambda j: (j, 0, 0)),
        out_shape=jax.ShapeDtypeStruct((width // SSM_GROUP, n, ch), F32), compiler_params=_params("parallel"),
    )(u)


def _groups_to_tokens(name, ug):
    groups, n, ch = ug.shape

    def body(g_ref, o_ref, rows_ref):
        for s in range(SSM_CHUNK):
            for g in range(GROUPS_PER_BLOCK):
                rows_ref[s % 2, :, g * SSM_GROUP:(g + 1) * SSM_GROUP] = g_ref[g, :, s * SSM_GROUP:(s + 1) * SSM_GROUP]
            o_ref[pl.ds(s, n, stride=SSM_CHUNK), :] = rows_ref[s % 2]

    return pl.pallas_call(
        body, name=name, grid=(groups // GROUPS_PER_BLOCK,),
        in_specs=[pl.BlockSpec((GROUPS_PER_BLOCK, n, ch), lambda j: (j, 0, 0))],
        out_specs=pl.BlockSpec((n * SSM_CHUNK, LANES), lambda j: (0, j)),
        out_shape=jax.ShapeDtypeStruct((n * SSM_CHUNK, groups * SSM_GROUP), F32),
        scratch_shapes=[pltpu.VMEM((2, n, LANES), F32)], compiler_params=_params("parallel"),
    )(ug)


def _s5_fwd(ug, t_mat, b_mat, c_mat, la, lb, *, batch, gb=8):
    groups, n, ch = ug.shape
    p2 = b_mat.shape[2]
    gb = _tile(groups, gb)
    nch = n // batch

    def body(u_ref, t_ref, b_ref, c_ref, la_ref, lb_ref, y_ref, x_ref, s_ref):
        for g in range(gb):
            s_ref[g] = jnp.dot(u_ref[g], b_ref[g], precision=HIGHEST, preferred_element_type=F32)

        def step(r, xs):
            new = []
            for g in range(gb):
                for b in range(batch):
                    x = xs[g * batch + b]
                    row = b * nch + r
                    x_ref[g, pl.ds(row, 1), :] = x
                    new.append(la_ref[g] * x + lb_ref[g] * pltpu.roll(x, p2 // 2, 1) + s_ref[g, pl.ds(row, 1), :])
            return tuple(new)

        lax.fori_loop(0, nch, step, tuple(jnp.zeros((1, p2), F32) for _ in range(gb * batch)))
        for g in range(gb):
            y_ref[g] = (jnp.dot(u_ref[g], t_ref[g], precision=HIGHEST, preferred_element_type=F32)
                        + jnp.dot(x_ref[g], c_ref[g], precision=HIGHEST, preferred_element_type=F32))

    def spec(a, b):
        return pl.BlockSpec((gb, a, b), lambda i: (i, 0, 0))

    return pl.pallas_call(
        body, name="s5_fwd", grid=(groups // gb,),
        in_specs=[spec(n, ch), spec(ch, ch), spec(ch, p2), spec(p2, ch), spec(1, p2), spec(1, p2)],
        out_specs=[spec(n, ch), spec(n, p2)],
        out_shape=[jax.ShapeDtypeStruct((groups, n, ch), F32), jax.ShapeDtypeStruct((groups, n, p2), F32)],
        scratch_shapes=[pltpu.VMEM((gb, n, p2), F32)], compiler_params=_params("parallel"),
    )(ug, t_mat, b_mat, c_mat, la, lb)


def _s5_bwd(ug, dyg, xin, t_mat, b_mat, c_mat, la, lb, *, batch, gb=8):
    groups, n, ch = ug.shape
    p2 = b_mat.shape[2]
    gb = _tile(groups, gb)
    nch = n // batch

    def body(u_ref, dy_ref, x_ref, t_ref, b_ref, c_ref, la_ref, lb_ref,
             du_ref, dt_ref, db_ref, dc_ref, dla_ref, dlb_ref, dx_ref, ds_ref):
        for g in range(gb):
            dx_ref[g] = _dot_nt(dy_ref[g], c_ref[g], precision=HIGHEST)
        last = []
        for g in range(gb):
            for b in range(batch):
                row = b * nch + nch - 1
                ds_ref[g, pl.ds(row, 1), :] = jnp.zeros((1, p2), F32)
                last.append(dx_ref[g, pl.ds(row, 1), :])

        def step(it, dxs):
            r = nch - 2 - it
            new = []
            for g in range(gb):
                for b in range(batch):
                    dxn = dxs[g * batch + b]
                    row = b * nch + r
                    ds_ref[g, pl.ds(row, 1), :] = dxn
                    new.append(dx_ref[g, pl.ds(row, 1), :] + la_ref[g] * dxn - lb_ref[g] * pltpu.roll(dxn, p2 // 2, 1))
            return tuple(new)

        lax.fori_loop(0, nch - 1, step, tuple(last))
        for g in range(gb):
            u, dy, ds, x = u_ref[g], dy_ref[g], ds_ref[g], x_ref[g]
            du_ref[g] = _dot_nt(dy, t_ref[g], precision=HIGHEST) + _dot_nt(ds, b_ref[g], precision=HIGHEST)
            dt_ref[g] = _dot_tn(u, dy, precision=HIGHEST)
            db_ref[g] = _dot_tn(u, ds, precision=HIGHEST)
            dc_ref[g] = _dot_tn(x, dy, precision=HIGHEST)
            dla_ref[g] = jnp.sum(ds * x, axis=0, keepdims=True)
            dlb_ref[g] = jnp.sum(ds * pltpu.roll(x, p2 // 2, 1), axis=0, keepdims=True)

    def spec(a, b):
        return pl.BlockSpec((gb, a, b), lambda i: (i, 0, 0))

    def shape(a, b):
        return jax.ShapeDtypeStruct((groups, a, b), F32)

    return pl.pallas_call(
        body, name="s5_bwd", grid=(groups // gb,),
        in_specs=[spec(n, ch), spec(n, ch), spec(n, p2), spec(ch, ch), spec(ch, p2), spec(p2, ch), spec(1, p2), spec(1, p2)],
        out_specs=[spec(n, ch), spec(ch, ch), spec(ch, p2), spec(p2, ch), spec(1, p2), spec(1, p2)],
        out_shape=[shape(n, ch), shape(ch, ch), shape(ch, p2), shape(p2, ch), shape(1, p2), shape(1, p2)],
        scratch_shapes=[pltpu.VMEM((gb, n, p2), F32), pltpu.VMEM((gb, n, p2), F32)],
        compiler_params=_params("parallel"),
    )(ug, dyg, xin, t_mat, b_mat, c_mat, la, lb)


def _block(ref, axis, j, size):
    start = j * size if isinstance(j, int) else pl.multiple_of(j * size, size)
    return ref.at[pl.ds(start, size), :] if axis == 0 else ref.at[:, pl.ds(start, size)]


def _gather_weights(blocks):
    n = len(blocks)
    hbm = pl.BlockSpec(memory_space=pl.ANY)

    def body(*refs):
        ins, outs = refs[:n], refs[n:2 * n]
        send_sems, recv_sems, local_sems = refs[2 * n:]
        c = lax.axis_index("c")
        my_chip = 2 * lax.axis_index("x") + lax.axis_index("y")
        for j in range(N_CHIPS):
            @pl.when(my_chip == j)
            def _(j=j):
                local, sends = [], []
                for w, (blk, axis) in enumerate(blocks):
                    size = blk.shape[axis]
                    mine = pltpu.make_async_copy(ins[w], _block(outs[w], axis, j, size), local_sems.at[w])
                    mine.start()
                    local.append(mine)
                for w, (blk, axis) in enumerate(blocks):
                    size = blk.shape[axis]
                    for peer in range(N_CHIPS):
                        if peer == j:
                            continue
                        send = pltpu.make_async_remote_copy(
                            src_ref=ins[w], dst_ref=_block(outs[w], axis, j, size),
                            send_sem=send_sems.at[w * N_CHIPS + peer], recv_sem=recv_sems.at[w * N_CHIPS + j],
                            device_id=(peer // 2, peer % 2, c), device_id_type=MESH)
                        send.start()
                        sends.append(send)
                for w, (blk, axis) in enumerate(blocks):
                    size = blk.shape[axis]
                    for peer in range(N_CHIPS):
                        if peer == j:
                            continue
                        pltpu.make_async_remote_copy(
                            src_ref=ins[w], dst_ref=_block(outs[w], axis, peer, size),
                            send_sem=send_sems.at[w * N_CHIPS + peer], recv_sem=recv_sems.at[w * N_CHIPS + peer],
                            device_id=(peer // 2, peer % 2, c), device_id_type=MESH).wait_recv()
                for cp in sends:
                    cp.wait_send()
                for cp in local:
                    cp.wait()

    out_shape = []
    for blk, axis in blocks:
        full = list(blk.shape)
        full[axis] *= N_CHIPS
        out_shape.append(jax.ShapeDtypeStruct(tuple(full), blk.dtype))
    return pl.pallas_call(
        body, name="gather_weights", in_specs=[hbm] * n, out_specs=[hbm] * n, out_shape=out_shape,
        scratch_shapes=[pltpu.SemaphoreType.DMA((n * N_CHIPS,)), pltpu.SemaphoreType.DMA((n * N_CHIPS,)),
                        pltpu.SemaphoreType.DMA((n,))],
    )(*[b for b, _ in blocks])


def _chip_exchange_copies(mode, axes, srcs, lands, send_sems, recv_sems, local_sems):
    x, y, c = lax.axis_index("x"), lax.axis_index("y"), lax.axis_index("c")
    me = 2 * x + y
    local, sends, arrivals = [], [], []
    for w, axis in enumerate(axes):
        if mode == "gather":
            size = srcs[w].shape[axis]
            local.append(pltpu.make_async_copy(srcs[w], _block(lands[w], axis, me, size), local_sems.at[w]))
        else:
            size = srcs[w].shape[axis] // N_CHIPS
            local.append(pltpu.make_async_copy(_block(srcs[w], axis, me, size), lands[w].at[me], local_sems.at[w]))
        for k in range(1, N_CHIPS):
            px = 1 - x if k & 2 else x
            py = 1 - y if k & 1 else y
            peer = 2 * px + py
            if mode == "gather":
                src, dst, arrive = srcs[w], _block(lands[w], axis, me, size), _block(lands[w], axis, peer, size)
            else:
                src, dst, arrive = _block(srcs[w], axis, peer, size), lands[w].at[me], lands[w].at[peer]
            sem = w * (N_CHIPS - 1) + k - 1
            for target, out in ((dst, sends), (arrive, arrivals)):
                out.append(pltpu.make_async_remote_copy(
                    src_ref=src, dst_ref=target, send_sem=send_sems.at[sem], recv_sem=recv_sems.at[sem],
                    device_id=(px, py, c), device_id_type=MESH))
    return local, sends, arrivals


def _chip_exchange_start(name, mode, items, after=None):
    n = len(items)
    n_after = 0 if after is None else 1
    axes = [axis for _, axis in items]
    hbm = pl.BlockSpec(memory_space=pltpu.HBM)
    sem = pl.BlockSpec(memory_space=pltpu.SEMAPHORE)
    lands = []
    for a, axis in items:
        shape = list(a.shape)
        if mode == "gather":
            shape[axis] *= N_CHIPS
        else:
            shape[axis] //= N_CHIPS
            shape = [N_CHIPS] + shape
        lands.append(pltpu.with_memory_space_constraint(lax.empty(tuple(shape), a.dtype), pltpu.HBM))

    def body(*refs):
        srcs, land_refs = refs[:n], refs[n:2 * n]
        send_sems, recv_sems, local_sems = refs[2 * n + n_after:2 * n + n_after + 3]
        token = refs[-1]
        local, sends, _ = _chip_exchange_copies(mode, axes, srcs, land_refs, send_sems, recv_sems, local_sems)
        for cp in local + sends:
            cp.start()
        token[...] = jnp.zeros_like(token)

    n_sem = n * (N_CHIPS - 1)
    outs = pl.pallas_call(
        body, name=name,
        out_shape=(pltpu.SemaphoreType.DMA((n_sem,)), pltpu.SemaphoreType.DMA((n_sem,)), pltpu.SemaphoreType.DMA((n,)),
                   *[pltpu.HBM(a.shape, a.dtype) for a, _ in items], *[pltpu.HBM(l.shape, l.dtype) for l in lands],
                   jax.ShapeDtypeStruct((8, LANES), F32)),
        in_specs=[hbm] * (2 * n) + [pl.BlockSpec(memory_space=pl.ANY)] * n_after,
        out_specs=(sem, sem, sem, *[hbm] * (2 * n), pl.BlockSpec(memory_space=pltpu.VMEM)),
        input_output_aliases={i: 3 + i for i in range(2 * n)},
        compiler_params=pltpu.CompilerParams(has_side_effects=pltpu.SideEffectType.DATAFLOW_SIDE_EFFECTING),
    )(*[pltpu.with_memory_space_constraint(a, pltpu.HBM) for a, _ in items], *lands, *([after] if n_after else []))
    return (mode, axes, outs[:3], outs[3:3 + n], outs[3 + n:3 + 2 * n]), outs[-1][0:1, 0:1]


def _chip_exchange_wait(name, handle, after):
    mode, axes, sems, srcs, lands = handle
    n = len(axes)
    hbm = pl.BlockSpec(memory_space=pltpu.HBM)
    sem = pl.BlockSpec(memory_space=pltpu.SEMAPHORE)

    def body(*refs):
        src_refs, land_refs = refs[:n], refs[n:2 * n]
        send_sems, recv_sems, local_sems = refs[2 * n:2 * n + 3]
        local, sends, arrivals = _chip_exchange_copies(mode, axes, src_refs, land_refs, send_sems, recv_sems, local_sems)
        for cp in sends:
            cp.wait_send()
        for cp in arrivals:
            cp.wait_recv()
        for cp in local:
            cp.wait()

    outs = pl.pallas_call(
        body, name=name,
        out_shape=(*[pltpu.HBM(a.shape, a.dtype) for a in srcs], *[pltpu.HBM(l.shape, l.dtype) for l in lands]),
        in_specs=[hbm] * (2 * n) + [sem] * 3 + [pl.BlockSpec(memory_space=pl.ANY)], out_specs=[hbm] * (2 * n),
        input_output_aliases={i: i for i in range(2 * n)},
        compiler_params=pltpu.CompilerParams(has_side_effects=pltpu.SideEffectType.DATAFLOW_SIDE_EFFECTING),
    )(*srcs, *lands, *sems, after)
    return outs[n:]


def _sum_slots(name, slots, tm=256):
    _, r, c = slots.shape
    tm = _tile(r, tm)

    def body(s0, s1, s2, s3, o_ref):
        o_ref[...] = ((s0[...] + s1[...]) + s2[...]) + s3[...]

    specs = [pl.BlockSpec((None, tm, c), functools.partial(lambda i, s: (s, i, 0), s=s)) for s in range(N_CHIPS)]
    return pl.pallas_call(
        body, name=name, grid=(r // tm,), in_specs=specs, out_specs=pl.BlockSpec((tm, c), lambda i: (i, 0)),
        out_shape=jax.ShapeDtypeStruct((r, c), F32), compiler_params=_params("parallel"),
    )(slots, slots, slots, slots)


def _swap_with_sibling(arrays):
    n = len(arrays)
    hbm = pl.BlockSpec(memory_space=pl.ANY)

    def body(*refs):
        ins, outs = refs[:n], refs[n:2 * n]
        send_sems, recv_sems = refs[2 * n:]
        sibling = (lax.axis_index("x"), lax.axis_index("y"), 1 - lax.axis_index("c"))
        copies = [pltpu.make_async_remote_copy(src_ref=ins[w], dst_ref=outs[w], send_sem=send_sems.at[w],
                                               recv_sem=recv_sems.at[w], device_id=sibling, device_id_type=MESH)
                  for w in range(n)]
        for cp in copies:
            cp.start()
        for cp in copies:
            cp.wait()

    return pl.pallas_call(
        body, name="swap_with_sibling", in_specs=[hbm] * n, out_specs=[hbm] * n,
        out_shape=[jax.ShapeDtypeStruct(a.shape, a.dtype) for a in arrays],
        scratch_shapes=[pltpu.SemaphoreType.DMA((n,)), pltpu.SemaphoreType.DMA((n,))],
    )(*arrays)


def _all_reduce_small(packed):
    r, lanes = packed.shape
    vmem = pl.BlockSpec(memory_space=pltpu.VMEM)

    def body(in_ref, out_ref, buf, send_sems, recv_sems):
        x, y, c = lax.axis_index("x"), lax.axis_index("y"), lax.axis_index("c")
        me = 4 * x + 2 * y + c
        buf[me] = in_ref[...]
        sends = []
        for d in range(N_DEV):
            @pl.when(me != d)
            def _(d=d):
                pltpu.make_async_remote_copy(
                    src_ref=in_ref, dst_ref=buf.at[me], send_sem=send_sems.at[d], recv_sem=recv_sems.at[me],
                    device_id=(d // 4, (d // 2) % 2, d % 2), device_id_type=MESH).start()
        for d in range(N_DEV):
            @pl.when(me != d)
            def _(d=d):
                cp = pltpu.make_async_remote_copy(
                    src_ref=in_ref, dst_ref=buf.at[d], send_sem=send_sems.at[d], recv_sem=recv_sems.at[d],
                    device_id=(d // 4, (d // 2) % 2, d % 2), device_id_type=MESH)
                cp.wait_recv()
                cp.wait_send()
        del sends
        acc = buf[0]
        for d in range(1, N_DEV):
            acc = acc + buf[d]
        out_ref[...] = acc

    return pl.pallas_call(
        body, name="all_reduce_small", in_specs=[vmem], out_specs=vmem,
        out_shape=jax.ShapeDtypeStruct((r, lanes), F32),
        scratch_shapes=[pltpu.VMEM((N_DEV, r, lanes), F32), pltpu.SemaphoreType.DMA((N_DEV,)),
                        pltpu.SemaphoreType.DMA((N_DEV,))],
        compiler_params=pltpu.CompilerParams(vmem_limit_bytes=VMEM_LIMIT),
    )(packed)


def _adamw(g, w, m, v):
    m = ADAM_B1 * m + (1.0 - ADAM_B1) * g
    v = ADAM_B2 * v + (1.0 - ADAM_B2) * jnp.square(g)
    m_hat = m / (1.0 - ADAM_B1 ** ADAM_STEP)
    v_hat = v / (1.0 - ADAM_B2 ** ADAM_STEP)
    delta = -ADAM_LR * (m_hat / (jnp.sqrt(v_hat) + ADAM_EPS) + ADAM_WD * w)
    return delta, m, v


def _adamw_small(grads, ws, ms, vs):
    n = len(ws)
    vmem = pl.BlockSpec(memory_space=pltpu.VMEM)

    def body(*refs):
        for i in range(n):
            g, w, m, v = (refs[k * n + i][...] for k in range(4))
            for k, val in enumerate(_adamw(g, w, m, v)):
                refs[(4 + k) * n + i][...] = val

    outs = pl.pallas_call(
        body, name="adamw_small", in_specs=[vmem] * (4 * n), out_specs=[vmem] * (3 * n),
        out_shape=[jax.ShapeDtypeStruct(w.shape, F32) for _ in range(3) for w in ws],
        compiler_params=pltpu.CompilerParams(vmem_limit_bytes=VMEM_LIMIT),
    )(*grads, *ws, *ms, *vs)
    return outs[:n], outs[n:2 * n], outs[2 * n:]


def kernel(x, norm1_g, w_in, q_norm_g, k_norm_g, ssm_lambda_re, ssm_lambda_im, ssm_log_dt, ssm_b_re, ssm_b_im, ssm_c_re, ssm_c_im, ssm_d, w_glu, b_glu, attn_out_g, ssm_out_g, w_out, norm2_g, w_mlp_in, w_mlp_out, loss_target, m_norm1_g, m_w_in, m_q_norm_g, m_k_norm_g, m_ssm_lambda_re, m_ssm_lambda_im, m_ssm_log_dt, m_ssm_b_re, m_ssm_b_im, m_ssm_c_re, m_ssm_c_im, m_ssm_d, m_w_glu, m_b_glu, m_attn_out_g, m_ssm_out_g, m_w_out, m_norm2_g, m_w_mlp_in, m_w_mlp_out, v_norm1_g, v_w_in, v_q_norm_g, v_k_norm_g, v_ssm_lambda_re, v_ssm_lambda_im, v_ssm_log_dt, v_ssm_b_re, v_ssm_b_im, v_ssm_c_re, v_ssm_c_im, v_ssm_d, v_w_glu, v_b_glu, v_attn_out_g, v_ssm_out_g, v_w_out, v_norm2_g, v_w_mlp_in, v_w_mlp_out):
    batch, seq, d_model = x.shape
    tokens = batch * seq
    sb_width = w_in.shape[1]
    n_features = d_model

    big = [("w_in", w_in, m_w_in, v_w_in, 1), ("w_glu", w_glu, m_w_glu, v_w_glu, 0),
           ("w_out", w_out, m_w_out, v_w_out, 0), ("w_mlp_in", w_mlp_in, m_w_mlp_in, v_w_mlp_in, 1),
           ("w_mlp_out", w_mlp_out, m_w_mlp_out, v_w_mlp_out, 0)]
    small = [("norm1_g", norm1_g, m_norm1_g, v_norm1_g), ("q_norm_g", q_norm_g, m_q_norm_g, v_q_norm_g),
             ("k_norm_g", k_norm_g, m_k_norm_g, v_k_norm_g),
             ("ssm_lambda_re", ssm_lambda_re, m_ssm_lambda_re, v_ssm_lambda_re),
             ("ssm_lambda_im", ssm_lambda_im, m_ssm_lambda_im, v_ssm_lambda_im),
             ("ssm_log_dt", ssm_log_dt, m_ssm_log_dt, v_ssm_log_dt),
             ("ssm_b_re", ssm_b_re, m_ssm_b_re, v_ssm_b_re), ("ssm_b_im", ssm_b_im, m_ssm_b_im, v_ssm_b_im),
             ("ssm_c_re", ssm_c_re, m_ssm_c_re, v_ssm_c_re), ("ssm_c_im", ssm_c_im, m_ssm_c_im, v_ssm_c_im),
             ("ssm_d", ssm_d, m_ssm_d, v_ssm_d), ("b_glu", b_glu, m_b_glu, v_b_glu),
             ("attn_out_g", attn_out_g, m_attn_out_g, v_attn_out_g), ("ssm_out_g", ssm_out_g, m_ssm_out_g, v_ssm_out_g),
             ("norm2_g", norm2_g, m_norm2_g, v_norm2_g)]

    gather_in, tok_in = _chip_exchange_start("gather_w_in_start", "gather", [(w_in.astype(BF16), 1)])
    gather_rest, tok_rest = _chip_exchange_start(
        "gather_rest_start", "gather", [(w.astype(BF16), axis) for _, w, _, _, axis in big[1:]], after=tok_in)

    x2 = x.reshape(tokens, d_model)
    tgt2 = loss_target.reshape(tokens, d_model)
    g1, g2 = norm1_g[None, :], norm2_g[None, :]
    g_attn, g_ssm, bias_glu = attn_out_g[None, :], ssm_out_g[None, :], b_glu[None, :]
    heads = sb_width // HEAD_DIM
    qk_scale = 1.0 / math.sqrt(HEAD_DIM)
    gq, gk = (jnp.tile(q_norm_g, heads) * qk_scale)[None, :], jnp.tile(k_norm_g, heads)[None, :]
    lane_head = jnp.arange(sb_width) // HEAD_DIM
    ones_blocks = (lane_head[:, None] == lane_head[None, :]).astype(F32)

    (xn,) = _rowwise("norm1", _rms, [x2], [g1 + tok_rest], [(d_model, BF16)])
    (wf_in,) = _chip_exchange_wait("gather_w_in_wait", gather_in, xn)
    (proj,) = _mm("proj_in", xn, wf_in, "nn")

    def qkv_fn(q, k, v, gq_, gk_, ones):
        return _head_rms(q, gq_, ones), _head_rms(k, gk_, ones), v

    qn, kn, vb = _rowwise("qk_norm", qkv_fn, [(proj, sb_width, 0), (proj, sb_width, 1), (proj, sb_width, 2)],
                          [gq, gk, ones_blocks], [(sb_width, BF16)] * 3)
    sb, c_tot = _attn_fwd(qn, kn, vb, batch=batch, seq=seq, bq=ATTN_BQ, bk=ATTN_BK)

    s5_params = (ssm_lambda_re, ssm_lambda_im, ssm_log_dt, ssm_b_re, ssm_b_im, ssm_c_re, ssm_c_im, ssm_d)
    (t_mat, b_mat, c_mat, la, lb), s5_vjp = jax.vjp(_s5_operators, *s5_params)
    ug = _tokens_to_groups("u_to_groups", proj, 3, sb_width)
    yg, xin = _s5_fwd(ug, t_mat, b_mat, c_mat, la, lb, batch=batch)
    y_ssm = _groups_to_tokens("y_to_tokens", yg)

    (y_act,) = _rowwise("gelu", _gelu, [y_ssm], [], [(sb_width, BF16)])
    wf_glu, wf_out, wf_mlp_in, wf_mlp_out = _chip_exchange_wait("gather_rest_wait", gather_rest, y_act)
    (gate_pre,) = _mm("glu_gate", y_act, wf_glu, "nn", extras=[(bias_glu, "row")], epilogue=lambda acc, b: acc + b)
    (mixed,) = _rowwise("mix_norm", _mixed, [sb, y_ssm, gate_pre], [g_attn, g_ssm], [(2 * sb_width, BF16)])
    (h1,) = _mm("proj_out", mixed, wf_out, "nn", extras=[(x2, "tile")], epilogue=lambda acc, r: acc + r)
    (hn,) = _rowwise("norm2", _rms, [h1], [g2], [(d_model, BF16)])
    (pre,) = _mm("mlp_in", hn, wf_mlp_in, "nn", tk=1024)
    inv_n = 1.0 / n_features
    (dy,) = _mm("mlp_out_loss", pre, wf_mlp_out, "nn", a_fn=_relu_sq, extras=[(h1, "tile"), (tgt2, "tile")],
                epilogue=lambda acc, r, t: ((acc + r) - t) * inv_n)

    def loss_fn(d):
        return d, jnp.sum(d * d, keepdims=True) * (0.5 * n_features)

    _, loss_part = _rowwise("loss", loss_fn, [dy], [], [(d_model, F32)], [(1, 1)])

    (dw_mlp_out,) = _mm("dw_mlp_out", pre, dy, "tn", a_fn=_relu_sq)
    (dpre,) = _mm("d_mlp_act", dy, wf_mlp_out, "nt", extras=[(pre, "tile")],
                  epilogue=lambda acc, p: acc * (2.0 * jnp.maximum(p, 0.0)), out_dtypes=(BF16,))
    (dw_mlp_in,) = _mm("dw_mlp_in", hn, dpre, "tn", tk=1024)
    scatter_mlp, tok_mlp = _chip_exchange_start("scatter_mlp_start", "scatter", [(dw_mlp_in, 1), (dw_mlp_out, 0)])
    (dhn,) = _mm("d_norm2_in", dpre, wf_mlp_in, "nt", tk=1024)

    def norm_bwd(res, hx, dn, g):
        _, vjp = jax.vjp(_rms, hx, g)
        dh, dg = vjp(dn)
        return res + dh, dg

    dh1, dg_norm2 = _rowwise("norm2_bwd", norm_bwd, [dy, h1, dhn], [g2 + tok_mlp], [(d_model, F32)], [(1, d_model)])
    (dmixed,) = _mm("d_mixed", dh1, wf_out, "nt")
    (dw_out,) = _mm("dw_out", mixed, dh1, "tn")

    def mixed_bwd(dm, sb_, ys, gp, ga, gs):
        _, vjp = jax.vjp(lambda a, act, b, c, d: jnp.concatenate(
            [_rms(a, c), _rms(act * jax.nn.sigmoid(b), d)], axis=-1), sb_, _gelu(ys), gp, ga, gs)
        dsb_, dact, dgp_, dga, dgs = vjp(dm)
        return dsb_, dgp_, dact, dga, dgs, jnp.sum(dgp_, axis=0, keepdims=True)

    dsb, dgate_pre, dact_part, dg_attn, dg_ssm, db_glu = _rowwise(
        "mix_norm_bwd", mixed_bwd, [dmixed, sb, y_ssm, gate_pre], [g_attn, g_ssm],
        [(sb_width, F32), (sb_width, BF16), (sb_width, F32)], [(1, sb_width)] * 3)

    def gelu_bwd(acc, part, ys):
        _, vjp = jax.vjp(_gelu, ys)
        return vjp(acc + part)[0]

    (dy_ssm,) = _mm("d_glu_in", dgate_pre, wf_glu, "nt", extras=[(dact_part, "tile"), (y_ssm, "tile")], epilogue=gelu_bwd)
    (dw_glu,) = _mm("dw_glu", y_act, dgate_pre, "tn")
    scatter_mix, tok_mix = _chip_exchange_start("scatter_mix_start", "scatter", [(dw_glu, 0), (dw_out, 0)])

    dug, dt_mat, db_mat, dc_mat, dla, dlb = _s5_bwd(ug, _tokens_to_groups("dy_to_groups", dy_ssm, 0, sb_width), xin, t_mat, b_mat, c_mat, la,
                                                    lb + tok_mix, batch=batch)
    du = _groups_to_tokens("du_to_tokens", dug)
    ds5 = s5_vjp((dt_mat, db_mat, dc_mat, dla, dlb))

    dqn, dkn, dv = _attn_bwd(qn, kn, vb, c_tot, dsb, batch=batch, seq=seq, bq=ATTN_BQ, bk=ATTN_BK)

    def qk_bwd(q, k, dq_, dk_, gq_, gk_, ones):
        _, vjp_q = jax.vjp(lambda a, g: _head_rms(a, g, ones), q, gq_)
        _, vjp_k = jax.vjp(lambda a, g: _head_rms(a, g, ones), k, gk_)
        dq, dgq = vjp_q(dq_)
        dk, dgk = vjp_k(dk_)
        return dq, dk, dgq, dgk

    dq, dk, dgq, dgk = _rowwise("qk_norm_bwd", qk_bwd, [(proj, sb_width, 0), (proj, sb_width, 1), dqn, dkn],
                                [gq, gk, ones_blocks], [(sb_width, BF16)] * 2, [(1, sb_width)] * 2)
    dproj = jnp.concatenate([dq, dk, dv.astype(BF16), du.astype(BF16)], axis=1)
    (dw_in,) = _mm("dw_in", xn, dproj, "tn")
    scatter_in, tok_w_in = _chip_exchange_start("scatter_in_start", "scatter", [(dw_in, 1)])
    (dxn,) = _mm("d_norm1_in", dproj, wf_in, "nt")
    dx, dg_norm1 = _rowwise("norm1_bwd", norm_bwd, [dh1, x2, dxn], [g1 + tok_w_in], [(d_model, F32)], [(1, d_model)])

    slots_mlp_in, slots_mlp_out = _chip_exchange_wait("scatter_mlp_wait", scatter_mlp, dg_norm1)
    slots_glu, slots_out = _chip_exchange_wait("scatter_mix_wait", scatter_mix, dg_norm1)
    (slots_in,) = _chip_exchange_wait("scatter_in_wait", scatter_in, dg_norm1)
    slots = [slots_in, slots_glu, slots_out, slots_mlp_in, slots_mlp_out]
    mine = [_sum_slots("sum_" + name, s) for s, (name, *_rest) in zip(slots, big)]
    theirs = _swap_with_sibling(mine)

    small_grads = [dg_norm1[0], dgq.reshape(heads, HEAD_DIM).sum(0) * qk_scale, dgk.reshape(heads, HEAD_DIM).sum(0), *ds5,
                   db_glu[0], dg_attn[0], dg_ssm[0], dg_norm2[0]]
    order = ["norm1_g", "q_norm_g", "k_norm_g", "ssm_lambda_re", "ssm_lambda_im", "ssm_log_dt", "ssm_b_re", "ssm_b_im",
             "ssm_c_re", "ssm_c_im", "ssm_d", "b_glu", "attn_out_g", "ssm_out_g", "norm2_g"]
    assert order == [name for name, *_ in small]

    def pack(parts, extra=None):
        flat = [p.reshape(-1) for p in parts] + ([extra.reshape(-1)] if extra is not None else [])
        flat = jnp.concatenate(flat)
        rows = -(-flat.shape[0] // (LANES * LANES)) * LANES
        return jnp.pad(flat, (0, rows * LANES - flat.shape[0])).reshape(rows, LANES)

    n_small = sum(w.size for _, w, _, _ in small)
    reduced = _all_reduce_small(pack(small_grads, loss_part))
    loss = reduced.reshape(-1)[n_small]

    def adam_big(sa, sb_, w, m, v):
        g = sa + sb_
        delta, m, v = _adamw(g, w, m, v)
        return g, delta, m, v

    big_out = {}
    for (name, w, m, v, _), sa, sb_ in zip(big, mine, theirs):
        big_out[name] = _rowwise("adamw_" + name, adam_big, [sa, sb_, w, m, v], [], [(w.shape[1], F32)] * 4)

    flat, small_g, off = reduced.reshape(-1), {}, 0
    for name, w, _, _ in small:
        small_g[name] = flat[off:off + w.size].reshape(w.shape)
        off += w.size
    small_upd = _adamw_small([small_g[name] for name, *_ in small], [w for _, w, _, _ in small],
                             [m for _, _, m, _ in small], [v for _, _, _, v in small])
    small_out = [small_g] + [{name: small_upd[kind][i] for i, (name, *_) in enumerate(small)} for kind in range(3)]
    names = ["norm1_g", "w_in", "q_norm_g", "k_norm_g", "ssm_lambda_re", "ssm_lambda_im", "ssm_log_dt", "ssm_b_re",
             "ssm_b_im", "ssm_c_re", "ssm_c_im", "ssm_d", "w_glu", "b_glu", "attn_out_g", "ssm_out_g", "w_out",
             "norm2_g", "w_mlp_in", "w_mlp_out"]
    outs = [loss, dx.reshape(batch, seq, d_model)]
    for kind in range(4):
        for name in names:
            outs.append(big_out[name][kind] if name in big_out else small_out[kind][name])
    return tuple(outs)
```

```python
import functools
import math

import jax
import jax.numpy as jnp
from jax import lax
from jax.experimental import pallas as pl
from jax.experimental.pallas import tpu as pltpu

F32 = jnp.float32
BF16 = jnp.bfloat16
F32_DOT = lax.Precision.HIGH
MESH = pl.DeviceIdType.MESH

RMS_EPS = 1e-6
HEAD_DIM = 64
SSM_GROUP = 16
SSM_CHUNK = 16
LANES = 128
N_CHIPS = 4
N_DEV = 8
VMEM_LIMIT = 48 * 1024 * 1024

ADAM_LR = 0.001
ADAM_B1 = 0.9
ADAM_B2 = 0.999
ADAM_EPS = 1e-08
ADAM_WD = 0.01
ADAM_STEP = 10


def _tile(n, pref):
    t = min(n, pref)
    while n % t:
        t //= 2
    return t


def _params(*sem):
    return pltpu.CompilerParams(dimension_semantics=sem, vmem_limit_bytes=VMEM_LIMIT)


_DIMS = {"nn": (((1,), (0,)), ((), ())), "nt": (((1,), (1,)), ((), ())), "tn": (((0,), (0,)), ((), ()))}


def _mm(name, a, b, mode, *, tm=1024, tn=1024, tk=512, a_fn=None, extras=(), epilogue=None, out_dtypes=(F32,),
        tile_sums=0):
    if mode == "nn":
        (m, k), n = a.shape, b.shape[1]
    elif mode == "nt":
        (m, k), n = a.shape, b.shape[0]
    else:
        (k, m), n = a.shape, b.shape[1]
    tm, tn, tk = _tile(m, tm), _tile(n, tn), _tile(k, tk)
    nk = k // tk
    ne, nout = len(extras), len(out_dtypes)
    dims = _DIMS[mode]

    def body(a_ref, b_ref, *rest):
        ex, outs, sums = rest[:ne], rest[ne:ne + nout], rest[ne + nout:ne + nout + tile_sums]
        at = a_ref[...]
        if a_fn is not None:
            at = a_fn(at)
        p = lax.dot_general(at.astype(BF16), b_ref[...].astype(BF16), dims, preferred_element_type=F32)

        def finish(r):
            if epilogue is not None:
                r = epilogue(r, *[e[...] for e in ex])
            if not isinstance(r, (tuple, list)):
                r = (r,)
            for o, v in zip(outs, r[:nout]):
                o[...] = v.astype(o.dtype)
            first = ((lax.broadcasted_iota(jnp.int32, (8, LANES), 0) == 0)
                     & (lax.broadcasted_iota(jnp.int32, (8, LANES), 1) == 0))
            for o, v in zip(sums, r[nout:]):
                o[...] = jnp.where(first, v, 0.0)

        if nk == 1:
            finish(p)
        else:
            acc = rest[ne + nout + tile_sums]
            kk = pl.program_id(2)

            @pl.when(kk == 0)
            def _():
                acc[...] = p

            @pl.when(kk > 0)
            def _():
                acc[...] += p

            @pl.when(kk == nk - 1)
            def _():
                finish(acc[...])

    if mode == "tn":
        a_spec = pl.BlockSpec((tk, tm), lambda i, j, kk: (kk, i))
    else:
        a_spec = pl.BlockSpec((tm, tk), lambda i, j, kk: (i, kk))
    if mode == "nt":
        b_spec = pl.BlockSpec((tn, tk), lambda i, j, kk: (j, kk))
    else:
        b_spec = pl.BlockSpec((tk, tn), lambda i, j, kk: (kk, j))
    ex_specs = []
    for _, kind in extras:
        if kind == "tile":
            ex_specs.append(pl.BlockSpec((tm, tn), lambda i, j, kk: (i, j)))
        else:
            ex_specs.append(pl.BlockSpec((1, tn), lambda i, j, kk: (0, j)))
    return pl.pallas_call(
        body, name=name, grid=(m // tm, n // tn, nk),
        in_specs=[a_spec, b_spec] + ex_specs,
        out_specs=([pl.BlockSpec((tm, tn), lambda i, j, kk: (i, j)) for _ in out_dtypes]
                   + [pl.BlockSpec((8, LANES), lambda i, j, kk: (i, j))] * tile_sums),
        out_shape=([jax.ShapeDtypeStruct((m, n), dt) for dt in out_dtypes]
                   + [jax.ShapeDtypeStruct((m // tm * 8, n // tn * LANES), F32)] * tile_sums),
        scratch_shapes=[pltpu.VMEM((tm, tn), F32)] if nk > 1 else [],
        compiler_params=_params("parallel", "parallel", "arbitrary"),
    )(a, b, *[e for e, _ in extras])


def _rowwise(name, fn, rows, consts, row_outs, acc_outs=(), tm=256):
    norm = [r if isinstance(r, tuple) else (r, r.shape[1], 0) for r in rows]
    t = norm[0][0].shape[0]
    tm = _tile(t, tm)
    nr, nc, no = len(norm), len(consts), len(row_outs)

    def body(*refs):
        outs = fn(*[r[...] for r in refs[:nr + nc]])
        if not isinstance(outs, (tuple, list)):
            outs = (outs,)
        o_refs, a_refs = refs[nr + nc:nr + nc + no], refs[nr + nc + no:]
        for r, v in zip(o_refs, outs[:no]):
            r[...] = v.astype(r.dtype)
        if a_refs:
            i = pl.program_id(0)

            @pl.when(i == 0)
            def _():
                for r, v in zip(a_refs, outs[no:]):
                    r[...] = v

            @pl.when(i > 0)
            def _():
                for r, v in zip(a_refs, outs[no:]):
                    r[...] += v

    in_specs = [pl.BlockSpec((tm, w), functools.partial(lambda i, cb: (i, cb), cb=cb)) for _, w, cb in norm]
    in_specs += [pl.BlockSpec(c.shape, functools.partial(lambda i, nd: (0,) * nd, nd=c.ndim)) for c in consts]
    out_specs = [pl.BlockSpec((tm, w), lambda i: (i, 0)) for w, _ in row_outs]
    out_specs += [pl.BlockSpec(s, functools.partial(lambda i, nd: (0,) * nd, nd=len(s))) for s in acc_outs]
    out_shape = [jax.ShapeDtypeStruct((t, w), dt) for w, dt in row_outs]
    out_shape += [jax.ShapeDtypeStruct(s, F32) for s in acc_outs]
    return pl.pallas_call(
        body, name=name, grid=(t // tm,), in_specs=in_specs, out_specs=out_specs, out_shape=out_shape,
        compiler_params=_params("arbitrary"),
    )(*[r[0] for r in norm], *consts)


def _rms(x, g):
    return x * lax.rsqrt(jnp.mean(x * x, axis=-1, keepdims=True) + RMS_EPS) * g


def _head_rms(x, g, ones_blocks):
    ss = jnp.dot(x * x, ones_blocks, precision=F32_DOT, preferred_element_type=F32)
    return x * lax.rsqrt(ss * (1.0 / HEAD_DIM) + RMS_EPS) * g


def _gelu(x):
    return x * (0.5 * (1.0 + jnp.tanh(math.sqrt(2.0 / math.pi) * (x + 0.044715 * (x * x * x)))))


def _mixed(sb, y_ssm, gate_pre, g_attn, g_ssm):
    ssm = _gelu(y_ssm) * jax.nn.sigmoid(gate_pre)
    return jnp.concatenate([_rms(sb, g_attn), _rms(ssm, g_ssm)], axis=-1)


def _softplus(z):
    return jnp.maximum(z, 0.0) + jnp.log(1.0 + jnp.exp(-jnp.abs(z)))


def _suffix_sums(x, tri):
    hi = x.astype(BF16)
    lo = (x - hi.astype(F32)).astype(BF16)
    return (jnp.dot(hi, tri, preferred_element_type=F32) + jnp.dot(lo, tri, preferred_element_type=F32))


def _dot_nt(a, b, **kw):
    return lax.dot_general(a, b, _DIMS["nt"], preferred_element_type=F32, **kw)


def _dot_tn(a, b, **kw):
    return lax.dot_general(a, b, _DIMS["tn"], preferred_element_type=F32, **kw)


ATTN_BQ, ATTN_BK = 1024, 256
HEAD_LANES = tuple(slice(h * HEAD_DIM, (h + 1) * HEAD_DIM) for h in range(LANES // HEAD_DIM))


def _attn_fwd(qs, kn, v, *, batch, seq, bq, bk):
    width = qs.shape[1]
    bq = _tile(seq, bq)
    bk = _tile(bq, bk)
    nq, kpq = seq // bq, bq // bk

    def body(q_ref, k_ref, v_ref, o_ref, c_ref):
        row = lax.broadcasted_iota(jnp.int32, (bq, bk), 0)
        col = lax.broadcasted_iota(jnp.int32, (bq, bk), 1)
        tri = (lax.broadcasted_iota(jnp.int32, (bk, bk), 0) >= lax.broadcasted_iota(jnp.int32, (bk, bk), 1)).astype(BF16)

        def q_block(qi, carry):
            r0 = pl.multiple_of(qi * bq, bq)
            qh = [q_ref[pl.ds(r0, bq), ln] for ln in HEAD_LANES]

            def tile(k0, state, top=0):
                diag = top is not None
                top = top or 0
                msk = (col < row)[:bq - top] if diag else None
                new = []
                for h, ln in enumerate(HEAD_LANES):
                    o, c = state[2 * h], state[2 * h + 1]
                    z = _dot_nt(qh[h][top:], k_ref[pl.ds(k0, bk), ln])
                    sp = _softplus(z)
                    if diag:
                        sp = jnp.where(msk, sp, 0.0)
                    r = _suffix_sums(sp, tri)
                    a = jnp.exp(z - r - c[top:])
                    if diag:
                        a = jnp.where(msk, a, 0.0)
                    o_new = o[top:] + jnp.dot(a.astype(BF16), v_ref[pl.ds(k0, bk), ln], preferred_element_type=F32)
                    c_new = c[top:] + r[:, 0:1]
                    if top:
                        o_new, c_new = jnp.concatenate([o[:top], o_new]), jnp.concatenate([c[:top], c_new])
                    new += [o_new, c_new]
                return tuple(new)

            state = (jnp.zeros((bq, HEAD_DIM), F32), jnp.zeros((bq, 1), F32)) * len(HEAD_LANES)
            for d in reversed(range(kpq)):
                state = tile(pl.multiple_of(r0 + d * bk, bk), state, top=d * bk)
            state = lax.fori_loop(0, qi * kpq, lambda it, st: tile(pl.multiple_of(r0 - (it + 1) * bk, bk), st, None),
                                  state)
            for h, ln in enumerate(HEAD_LANES):
                o_ref[pl.ds(r0, bq), ln] = state[2 * h]
                c_ref[pl.ds(r0, bq), ln] = jnp.broadcast_to(state[2 * h + 1], (bq, HEAD_DIM))
            return carry

        lax.fori_loop(0, nq, q_block, 0)

    spec = pl.BlockSpec((seq, LANES), lambda b, h: (b, h))
    shape = jax.ShapeDtypeStruct((batch * seq, width), F32)
    return pl.pallas_call(
        body, name="attn_fwd", grid=(batch, width // LANES), in_specs=[spec, spec, spec], out_specs=[spec, spec],
        out_shape=[shape, shape], compiler_params=_params("parallel", "parallel"),
    )(qs, kn, v)


def _attn_bwd(qs, kn, v, c_tot, do, *, batch, seq, bq, bk):
    width = qs.shape[1]
    bq = _tile(seq, bq)
    bk = _tile(bq, bk)
    nq, kpq = seq // bq, bq // bk

    def body(q_ref, k_ref, v_ref, c_ref, do_ref, dq_ref, dk_ref, dv_ref):
        row = lax.broadcasted_iota(jnp.int32, (bq, bk), 0)
        col = lax.broadcasted_iota(jnp.int32, (bq, bk), 1)
        sq_row = lax.broadcasted_iota(jnp.int32, (bk, bk), 0)
        sq_col = lax.broadcasted_iota(jnp.int32, (bk, bk), 1)
        tri = (sq_row >= sq_col).astype(BF16)
        tri_t = (sq_row <= sq_col).astype(BF16)
        dk_ref[...] = jnp.zeros_like(dk_ref)
        dv_ref[...] = jnp.zeros_like(dv_ref)

        def q_block(qi, carry):
            r0 = pl.multiple_of(qi * bq, bq)
            qh = [q_ref[pl.ds(r0, bq), ln] for ln in HEAD_LANES]
            d_out = [do_ref[pl.ds(r0, bq), ln].astype(BF16) for ln in HEAD_LANES]
            c_all = [c_ref[pl.ds(r0, bq), ln][:, 0:1] for ln in HEAD_LANES]

            def tile(k0, state, top=0):
                diag = top is not None
                top = top or 0
                last = diag and top == bq - bk
                msk = (col < row)[:bq - top] if diag else None
                new = []
                for h, ln in enumerate(HEAD_LANES):
                    c_left, g_left, dq = state[3 * h:3 * h + 3]
                    q, d_o = qh[h][top:], d_out[h][top:]
                    k = k_ref[pl.ds(k0, bk), ln]
                    z = _dot_nt(q, k)
                    e = jnp.exp(-jnp.abs(z))
                    sp = jnp.maximum(z, 0.0) + jnp.log(1.0 + e)
                    if diag:
                        sp = jnp.where(msk, sp, 0.0)
                    r = _suffix_sums(sp, tri)
                    c_new = c_left[top:] + r[:, 0:1]
                    a = jnp.exp(z - r - (0.0 if last else c_all[h][top:] - c_new))
                    if diag:
                        a = jnp.where(msk, a, 0.0)
                    g = a * _dot_nt(d_o, v_ref[pl.ds(k0, bk), ln])
                    pg = _suffix_sums(g, tri_t)
                    sig = jnp.where(z >= 0.0, 1.0, e) / (1.0 + e)
                    dz = g - sig * (g_left[top:] + pg)
                    if diag:
                        dz = jnp.where(msk, dz, 0.0)
                    dz = dz.astype(BF16)
                    dk_ref[pl.ds(k0, bk), ln] += _dot_tn(dz, q)
                    dv_ref[pl.ds(k0, bk), ln] += _dot_tn(a.astype(BF16), d_o)
                    g_new = g_left[top:] + pg[:, bk - 1:bk]
                    dq_new = dq[top:] + jnp.dot(dz, k, preferred_element_type=F32)
                    if top:
                        c_new = jnp.concatenate([c_left[:top], c_new])
                        g_new = jnp.concatenate([g_left[:top], g_new])
                        dq_new = jnp.concatenate([dq[:top], dq_new])
                    new += [c_new, g_new, dq_new]
                return tuple(new)

            zero = jnp.zeros((bq, 1), F32)
            init = (zero, zero, jnp.zeros((bq, HEAD_DIM), F32)) * len(HEAD_LANES)
            state = lax.fori_loop(0, qi * kpq, lambda it, st: tile(pl.multiple_of(it * bk, bk), st, None), init)
            for d in range(kpq):
                state = tile(pl.multiple_of(r0 + d * bk, bk), state, top=d * bk)
            for h, ln in enumerate(HEAD_LANES):
                dq_ref[pl.ds(r0, bq), ln] = state[3 * h + 2]
            return carry

        lax.fori_loop(0, nq, q_block, 0)

    spec = pl.BlockSpec((seq, LANES), lambda b, h: (b, h))
    shape = jax.ShapeDtypeStruct((batch * seq, width), F32)
    return pl.pallas_call(
        body, name="attn_bwd", grid=(batch, width // LANES), in_specs=[spec] * 5, out_specs=[spec] * 3,
        out_shape=[shape] * 3, compiler_params=_params("parallel", "parallel"),
    )(qs, kn, v, c_tot, do)


def _s5_operators(lam_re, lam_im, log_dt, b_re, b_im, c_re, c_im, d_skip):
    groups, n_state, n_ch = b_re.shape
    cs = SSM_CHUNK
    dt = jnp.exp(log_dt)[:, None]
    steps = jnp.arange(cs + 1, dtype=F32)[None, :, None]
    mag = jnp.exp(steps * (lam_re * dt)[:, None, :])
    ang = steps * (lam_im * dt)[:, None, :]
    pw_re, pw_im = mag * jnp.cos(ang), mag * jnp.sin(ang)
    num_re, num_im = pw_re[:, 1] - 1.0, pw_im[:, 1]
    den = lam_re * lam_re + lam_im * lam_im
    cf_re = (num_re * lam_re + num_im * lam_im) / den
    cf_im = (num_im * lam_re - num_re * lam_im) / den
    bb_re = cf_re[:, :, None] * b_re - cf_im[:, :, None] * b_im
    bb_im = cf_re[:, :, None] * b_im + cf_im[:, :, None] * b_re
    width = cs * n_ch
    ct_re, ct_im = c_re.transpose(0, 2, 1), c_im.transpose(0, 2, 1)

    def c_times_powers(first):
        pr = pw_re[:, first:first + cs].transpose(0, 2, 1)[:, :, :, None]
        pi = pw_im[:, first:first + cs].transpose(0, 2, 1)[:, :, :, None]
        re = pr * ct_re[:, :, None, :] - pi * ct_im[:, :, None, :]
        im = pr * ct_im[:, :, None, :] + pi * ct_re[:, :, None, :]
        return re.reshape(groups, n_state, width), im.reshape(groups, n_state, width)

    w_re, w_im = c_times_powers(0)
    kt_row = (jnp.einsum("gpi,gpw->giw", bb_re, w_re, precision=F32_DOT)
              - jnp.einsum("gpi,gpw->giw", bb_im, w_im, precision=F32_DOT))
    kt_row = kt_row + jnp.pad(jnp.eye(n_ch, dtype=F32)[None] * d_skip[:, None, :], ((0, 0), (0, 0), (0, width - n_ch)))
    rp_re, rp_im = pw_re[:, cs - 1::-1][:, :cs], pw_im[:, cs - 1::-1][:, :cs]
    bm_re = rp_re[:, :, None, :] * bb_re.transpose(0, 2, 1)[:, None] - rp_im[:, :, None, :] * bb_im.transpose(0, 2, 1)[:, None]
    bm_im = rp_re[:, :, None, :] * bb_im.transpose(0, 2, 1)[:, None] + rp_im[:, :, None, :] * bb_re.transpose(0, 2, 1)[:, None]
    b_mat = jnp.concatenate([bm_re, bm_im], axis=-1).reshape(groups, width, 2 * n_state)
    w1_re, w1_im = c_times_powers(1)
    c_mat = jnp.concatenate([w1_re, -w1_im], axis=1)
    la = jnp.concatenate([pw_re[:, cs], pw_re[:, cs]], axis=-1)[:, None, :]
    lb = jnp.concatenate([-pw_im[:, cs], pw_im[:, cs]], axis=-1)[:, None, :]
    return kt_row, b_mat, c_mat, la, lb


GROUPS_PER_BLOCK = LANES // SSM_GROUP


def _tokens_to_groups(name, u, col_block, width):
    t = u.shape[0]
    n = t // SSM_CHUNK
    ch = SSM_CHUNK * SSM_GROUP
    blocks = width // LANES

    def body(u_ref, o_ref):
        for s in range(SSM_CHUNK):
            rows = u_ref[pl.ds(s, n, stride=SSM_CHUNK), :]
            for g in range(GROUPS_PER_BLOCK):
                o_ref[g, :, s * SSM_GROUP:(s + 1) * SSM_GROUP] = rows[:, g * SSM_GROUP:(g + 1) * SSM_GROUP]

    return pl.pallas_call(
        body, name=name, grid=(blocks,),
        in_specs=[pl.BlockSpec((t, LANES), lambda j: (0, col_block * blocks + j))],
        out_specs=pl.BlockSpec((GROUPS_PER_BLOCK, n, ch), lambda j: (j, 0, 0)),
        out_shape=jax.ShapeDtypeStruct((width // SSM_GROUP, n, ch), F32), compiler_params=_params("parallel"),
    )(u)


def _groups_to_tokens(name, ug):
    groups, n, ch = ug.shape

    def body(g_ref, o_ref, rows_ref):
        for s in range(SSM_CHUNK):
            for g in range(GROUPS_PER_BLOCK):
                rows_ref[s % 2, :, g * SSM_GROUP:(g + 1) * SSM_GROUP] = g_ref[g, :, s * SSM_GROUP:(s + 1) * SSM_GROUP]
            o_ref[pl.ds(s, n, stride=SSM_CHUNK), :] = rows_ref[s % 2]

    return pl.pallas_call(
        body, name=name, grid=(groups // GROUPS_PER_BLOCK,),
        in_specs=[pl.BlockSpec((GROUPS_PER_BLOCK, n, ch), lambda j: (j, 0, 0))],
        out_specs=pl.BlockSpec((n * SSM_CHUNK, LANES), lambda j: (0, j)),
        out_shape=jax.ShapeDtypeStruct((n * SSM_CHUNK, groups * SSM_GROUP), F32),
        scratch_shapes=[pltpu.VMEM((2, n, LANES), F32)], compiler_params=_params("parallel"),
    )(ug)


SCAN_ROWS = 8


def _toeplitz_to(tm_ref, g, kt_row):
    width = kt_row.shape[1]
    tm_ref[g] = jnp.zeros((width, width), F32)
    for s in range(SSM_CHUNK):
        tm_ref[g, s * SSM_GROUP:(s + 1) * SSM_GROUP, s * SSM_GROUP:] = kt_row[:, :width - s * SSM_GROUP]


def _lam_powers(la, lb, reverse):
    if reverse:
        lb = -lb

    def mul(p, q):
        return p[0] * q[0] - p[1] * q[1], p[0] * q[1] + p[1] * q[0]

    p1 = (la, lb)
    p2 = mul(p1, p1)
    p3 = mul(p2, p1)
    p4 = mul(p2, p2)
    rows = [p1, p2, p3, p4, mul(p4, p1), mul(p4, p2), mul(p4, p3), mul(p4, p4)]
    if reverse:
        rows = rows[::-1]
    idx = lax.broadcasted_iota(jnp.int32, (SCAN_ROWS, la.shape[1]), 0)
    tab_a = sum(jnp.where(idx == j, r[0], 0.0) for j, r in enumerate(rows))
    tab_b = sum(jnp.where(idx == j, r[1], 0.0) for j, r in enumerate(rows))
    return (p1, p2, p4), (tab_a, tab_b), idx


def _scan_block(e, carry, steps, table, idx, half, reverse):
    n = SCAN_ROWS
    for d, (pa, pb) in zip((1, 2, 4), steps):
        sh = pltpu.roll(e, n - d if reverse else d, 0)
        sh = jnp.where(idx < n - d if reverse else idx >= d, sh, 0.0)
        e = e + pa * sh + pb * pltpu.roll(sh, half, 1)
    tab_a, tab_b = table
    e = e + tab_a * carry + tab_b * pltpu.roll(carry, half, 1)
    shifted = jnp.where(idx == (n - 1 if reverse else 0), carry, pltpu.roll(e, n - 1 if reverse else 1, 0))
    edge = e[0:1] if reverse else e[n - 1:n]
    return shifted, jnp.broadcast_to(edge, e.shape)


def _s5_fwd(ug, kt_row, b_mat, c_mat, la, lb, *, batch, gb=8):
    groups, n, ch = ug.shape
    p2 = b_mat.shape[2]
    gb = _tile(groups, gb)
    nch = n // batch
    nblk = nch // SCAN_ROWS

    def body(u_ref, k_ref, b_ref, c_ref, la_ref, lb_ref, y_ref, x_ref, s_ref, tm_ref):
        for g in range(gb):
            _toeplitz_to(tm_ref, g, k_ref[g])
            s_ref[g] = jnp.dot(u_ref[g], b_ref[g], precision=F32_DOT, preferred_element_type=F32)
        powers = [_lam_powers(la_ref[g], lb_ref[g], False) for g in range(gb)]

        def step(blk, carries):
            new = []
            for g in range(gb):
                steps, table, idx = powers[g]
                for b in range(batch):
                    rows = pl.ds(pl.multiple_of(b * nch + blk * SCAN_ROWS, SCAN_ROWS), SCAN_ROWS)
                    x_in, carry = _scan_block(s_ref[g, rows, :], carries[g * batch + b], steps, table, idx, p2 // 2, False)
                    x_ref[g, rows, :] = x_in
                    new.append(carry)
            return tuple(new)

        lax.fori_loop(0, nblk, step, tuple(jnp.zeros((SCAN_ROWS, p2), F32) for _ in range(gb * batch)))
        for g in range(gb):
            y_ref[g] = (jnp.dot(u_ref[g], tm_ref[g], precision=F32_DOT, preferred_element_type=F32)
                        + jnp.dot(x_ref[g], c_ref[g], precision=F32_DOT, preferred_element_type=F32))

    def spec(a, b):
        return pl.BlockSpec((gb, a, b), lambda i: (i, 0, 0))

    return pl.pallas_call(
        body, name="s5_fwd", grid=(groups // gb,),
        in_specs=[spec(n, ch), spec(SSM_GROUP, ch), spec(ch, p2), spec(p2, ch), spec(1, p2), spec(1, p2)],
        out_specs=[spec(n, ch), spec(n, p2)],
        out_shape=[jax.ShapeDtypeStruct((groups, n, ch), F32), jax.ShapeDtypeStruct((groups, n, p2), F32)],
        scratch_shapes=[pltpu.VMEM((gb, n, p2), F32), pltpu.VMEM((gb, ch, ch), F32)],
        compiler_params=_params("parallel"),
    )(ug, kt_row, b_mat, c_mat, la, lb)


def _s5_bwd(ug, dyg, xin, kt_row, b_mat, c_mat, la, lb, *, batch, gb=8):
    groups, n, ch = ug.shape
    p2 = b_mat.shape[2]
    gb = _tile(groups, gb)
    nch = n // batch
    nblk = nch // SCAN_ROWS

    def body(u_ref, dy_ref, x_ref, k_ref, b_ref, c_ref, la_ref, lb_ref,
             du_ref, dk_ref, db_ref, dc_ref, dla_ref, dlb_ref, dx_ref, ds_ref, tm_ref):
        for g in range(gb):
            _toeplitz_to(tm_ref, g, k_ref[g])
            dx_ref[g] = _dot_nt(dy_ref[g], c_ref[g], precision=F32_DOT)
        powers = [_lam_powers(la_ref[g], lb_ref[g], True) for g in range(gb)]

        def step(it, carries):
            new = []
            for g in range(gb):
                steps, table, idx = powers[g]
                for b in range(batch):
                    rows = pl.ds(pl.multiple_of(b * nch + (nblk - 1 - it) * SCAN_ROWS, SCAN_ROWS), SCAN_ROWS)
                    d_s, carry = _scan_block(dx_ref[g, rows, :], carries[g * batch + b], steps, table, idx, p2 // 2, True)
                    ds_ref[g, rows, :] = d_s
                    new.append(carry)
            return tuple(new)

        lax.fori_loop(0, nblk, step, tuple(jnp.zeros((SCAN_ROWS, p2), F32) for _ in range(gb * batch)))
        for g in range(gb):
            u, dy, ds, x = u_ref[g], dy_ref[g], ds_ref[g], x_ref[g]
            du_ref[g] = _dot_nt(dy, tm_ref[g], precision=F32_DOT) + _dot_nt(ds, b_ref[g], precision=F32_DOT)
            tm_ref[g] = _dot_tn(u, dy, precision=F32_DOT)
            dk_ref[g] = tm_ref[g, 0:SSM_GROUP, :]
            for s in range(1, SSM_CHUNK):
                dk_ref[g, :, :ch - s * SSM_GROUP] += tm_ref[g, s * SSM_GROUP:(s + 1) * SSM_GROUP, s * SSM_GROUP:]
            db_ref[g] = _dot_tn(u, ds, precision=F32_DOT)
            dc_ref[g] = _dot_tn(x, dy, precision=F32_DOT)
            dla_ref[g] = jnp.sum(ds * x, axis=0, keepdims=True)
            dlb_ref[g] = jnp.sum(ds * pltpu.roll(x, p2 // 2, 1), axis=0, keepdims=True)

    def spec(a, b):
        return pl.BlockSpec((gb, a, b), lambda i: (i, 0, 0))

    def shape(a, b):
        return jax.ShapeDtypeStruct((groups, a, b), F32)

    return pl.pallas_call(
        body, name="s5_bwd", grid=(groups // gb,),
        in_specs=[spec(n, ch), spec(n, ch), spec(n, p2), spec(SSM_GROUP, ch), spec(ch, p2), spec(p2, ch), spec(1, p2),
                  spec(1, p2)],
        out_specs=[spec(n, ch), spec(SSM_GROUP, ch), spec(ch, p2), spec(p2, ch), spec(1, p2), spec(1, p2)],
        out_shape=[shape(n, ch), shape(SSM_GROUP, ch), shape(ch, p2), shape(p2, ch), shape(1, p2), shape(1, p2)],
        scratch_shapes=[pltpu.VMEM((gb, n, p2), F32), pltpu.VMEM((gb, n, p2), F32), pltpu.VMEM((gb, ch, ch), F32)],
        compiler_params=_params("parallel"),
    )(ug, dyg, xin, kt_row, b_mat, c_mat, la, lb)


def _block(ref, axis, j, size):
    start = j * size if isinstance(j, int) else pl.multiple_of(j * size, size)
    return ref.at[pl.ds(start, size), :] if axis == 0 else ref.at[:, pl.ds(start, size)]


def _chip_exchange_copies(mode, axes, srcs, lands, send_sems, recv_sems, local_sems):
    x, y, c = lax.axis_index("x"), lax.axis_index("y"), lax.axis_index("c")
    everyone = mode == "all"
    me = 4 * x + 2 * y + c if everyone else 2 * x + y
    n_peers = _exchange_peers(mode)
    local, sends, arrivals = [], [], []
    for w, axis in enumerate(axes):
        if mode == "gather":
            size = srcs[w].shape[axis]
            local.append(pltpu.make_async_copy(srcs[w], _block(lands[w], axis, me, size), local_sems.at[w]))
        elif mode == "scatter":
            size = srcs[w].shape[axis] // N_CHIPS
            local.append(pltpu.make_async_copy(_block(srcs[w], axis, me, size), lands[w].at[me], local_sems.at[w]))
        else:
            local.append(pltpu.make_async_copy(srcs[w], lands[w].at[me], local_sems.at[w]))
        for k in range(1, n_peers + 1):
            bits = k if everyone else 2 * k
            px = 1 - x if bits & 4 else x
            py = 1 - y if bits & 2 else y
            pc = 1 - c if bits & 1 else c
            peer = 4 * px + 2 * py + pc if everyone else 2 * px + py
            if mode == "gather":
                src, dst, arrive = srcs[w], _block(lands[w], axis, me, size), _block(lands[w], axis, peer, size)
            elif mode == "scatter":
                src, dst, arrive = _block(srcs[w], axis, peer, size), lands[w].at[me], lands[w].at[peer]
            else:
                src, dst, arrive = srcs[w], lands[w].at[me], lands[w].at[peer]
            sem = w * n_peers + k - 1
            for target, out in ((dst, sends), (arrive, arrivals)):
                out.append(pltpu.make_async_remote_copy(
                    src_ref=src, dst_ref=target, send_sem=send_sems.at[sem], recv_sem=recv_sems.at[sem],
                    device_id=(px, py, pc), device_id_type=MESH))
    return local, sends, arrivals


def _exchange_peers(mode):
    return N_DEV - 1 if mode == "all" else N_CHIPS - 1


def _chip_exchange_start(name, mode, items, after=None):
    n = len(items)
    n_after = 0 if after is None else 1
    axes = [axis for _, axis in items]
    hbm = pl.BlockSpec(memory_space=pltpu.HBM)
    sem = pl.BlockSpec(memory_space=pltpu.SEMAPHORE)
    lands = []
    for a, axis in items:
        shape = list(a.shape)
        if mode == "gather":
            shape[axis] *= N_CHIPS
        elif mode == "scatter":
            shape[axis] //= N_CHIPS
            shape = [N_CHIPS] + shape
        else:
            shape = [N_DEV] + shape
        lands.append(pltpu.with_memory_space_constraint(lax.empty(tuple(shape), a.dtype), pltpu.HBM))

    def body(*refs):
        srcs, land_refs = refs[:n], refs[n:2 * n]
        send_sems, recv_sems, local_sems = refs[2 * n + n_after:2 * n + n_after + 3]
        token = refs[-1]
        local, sends, _ = _chip_exchange_copies(mode, axes, srcs, land_refs, send_sems, recv_sems, local_sems)
        for cp in local + sends:
            cp.start()
        token[...] = jnp.zeros_like(token)

    n_sem = n * _exchange_peers(mode)
    outs = pl.pallas_call(
        body, name=name,
        out_shape=(pltpu.SemaphoreType.DMA((n_sem,)), pltpu.SemaphoreType.DMA((n_sem,)), pltpu.SemaphoreType.DMA((n,)),
                   *[pltpu.HBM(a.shape, a.dtype) for a, _ in items], *[pltpu.HBM(l.shape, l.dtype) for l in lands],
                   jax.ShapeDtypeStruct((8, LANES), F32)),
        in_specs=[hbm] * (2 * n) + [pl.BlockSpec(memory_space=pl.ANY)] * n_after,
        out_specs=(sem, sem, sem, *[hbm] * (2 * n), pl.BlockSpec(memory_space=pltpu.VMEM)),
        input_output_aliases={i: 3 + i for i in range(2 * n)},
        compiler_params=pltpu.CompilerParams(has_side_effects=pltpu.SideEffectType.DATAFLOW_SIDE_EFFECTING),
    )(*[pltpu.with_memory_space_constraint(a, pltpu.HBM) for a, _ in items], *lands, *([after] if n_after else []))
    return (mode, axes, outs[:3], outs[3:3 + n], outs[3 + n:3 + 2 * n]), outs[-1][0:1, 0:1]


def _chip_exchange_wait(name, handle, after):
    mode, axes, sems, srcs, lands = handle
    n = len(axes)
    hbm = pl.BlockSpec(memory_space=pltpu.HBM)
    sem = pl.BlockSpec(memory_space=pltpu.SEMAPHORE)

    def body(*refs):
        src_refs, land_refs = refs[:n], refs[n:2 * n]
        send_sems, recv_sems, local_sems = refs[2 * n:2 * n + 3]
        local, sends, arrivals = _chip_exchange_copies(mode, axes, src_refs, land_refs, send_sems, recv_sems, local_sems)
        for cp in sends:
            cp.wait_send()
        for cp in arrivals:
            cp.wait_recv()
        for cp in local:
            cp.wait()

    outs = pl.pallas_call(
        body, name=name,
        out_shape=(*[pltpu.HBM(a.shape, a.dtype) for a in srcs], *[pltpu.HBM(l.shape, l.dtype) for l in lands]),
        in_specs=[hbm] * (2 * n) + [sem] * 3 + [pl.BlockSpec(memory_space=pl.ANY)], out_specs=[hbm] * (2 * n),
        input_output_aliases={i: i for i in range(2 * n)},
        compiler_params=pltpu.CompilerParams(has_side_effects=pltpu.SideEffectType.DATAFLOW_SIDE_EFFECTING),
    )(*srcs, *lands, *sems, after)
    return outs[n:]


def _sum_slots(name, slots, tm=256):
    n_slots, r, c = slots.shape
    tm = _tile(r, tm)

    def body(*refs):
        acc = refs[0][...]
        for s_ref in refs[1:n_slots]:
            acc = acc + s_ref[...]
        refs[n_slots][...] = acc

    specs = [pl.BlockSpec((None, tm, c), functools.partial(lambda i, s: (s, i, 0), s=s)) for s in range(n_slots)]
    return pl.pallas_call(
        body, name=name, grid=(r // tm,), in_specs=specs, out_specs=pl.BlockSpec((tm, c), lambda i: (i, 0)),
        out_shape=jax.ShapeDtypeStruct((r, c), F32), compiler_params=_params("parallel"),
    )(*[slots] * n_slots)


def _swap_with_sibling(name, arrays):
    n = len(arrays)
    hbm = pl.BlockSpec(memory_space=pl.ANY)

    def body(*refs):
        ins, outs = refs[:n], refs[n:2 * n]
        send_sems, recv_sems = refs[2 * n:]
        sibling = (lax.axis_index("x"), lax.axis_index("y"), 1 - lax.axis_index("c"))
        copies = [pltpu.make_async_remote_copy(src_ref=ins[w], dst_ref=outs[w], send_sem=send_sems.at[w],
                                               recv_sem=recv_sems.at[w], device_id=sibling, device_id_type=MESH)
                  for w in range(n)]
        for cp in copies:
            cp.start()
        for cp in copies:
            cp.wait()

    return pl.pallas_call(
        body, name=name, in_specs=[hbm] * n, out_specs=[hbm] * n,
        out_shape=[jax.ShapeDtypeStruct(a.shape, a.dtype) for a in arrays],
        scratch_shapes=[pltpu.SemaphoreType.DMA((n,)), pltpu.SemaphoreType.DMA((n,))],
    )(*arrays)


def _adamw(g, w, m, v):
    m = ADAM_B1 * m + (1.0 - ADAM_B1) * g
    v = ADAM_B2 * v + (1.0 - ADAM_B2) * jnp.square(g)
    m_hat = m / (1.0 - ADAM_B1 ** ADAM_STEP)
    v_hat = v / (1.0 - ADAM_B2 ** ADAM_STEP)
    delta = -ADAM_LR * (m_hat / (jnp.sqrt(v_hat) + ADAM_EPS) + ADAM_WD * w)
    return delta, m, v


def _adamw_small(grads, ws, ms, vs):
    n = len(ws)
    vmem = pl.BlockSpec(memory_space=pltpu.VMEM)

    def body(*refs):
        for i in range(n):
            g, w, m, v = (refs[k * n + i][...] for k in range(4))
            for k, val in enumerate(_adamw(g, w, m, v)):
                refs[(4 + k) * n + i][...] = val

    outs = pl.pallas_call(
        body, name="adamw_small", in_specs=[vmem] * (4 * n), out_specs=[vmem] * (3 * n),
        out_shape=[jax.ShapeDtypeStruct(w.shape, F32) for _ in range(3) for w in ws],
        compiler_params=pltpu.CompilerParams(vmem_limit_bytes=VMEM_LIMIT),
    )(*grads, *ws, *ms, *vs)
    return outs[:n], outs[n:2 * n], outs[2 * n:]


def kernel(x, norm1_g, w_in, q_norm_g, k_norm_g, ssm_lambda_re, ssm_lambda_im, ssm_log_dt, ssm_b_re, ssm_b_im, ssm_c_re, ssm_c_im, ssm_d, w_glu, b_glu, attn_out_g, ssm_out_g, w_out, norm2_g, w_mlp_in, w_mlp_out, loss_target, m_norm1_g, m_w_in, m_q_norm_g, m_k_norm_g, m_ssm_lambda_re, m_ssm_lambda_im, m_ssm_log_dt, m_ssm_b_re, m_ssm_b_im, m_ssm_c_re, m_ssm_c_im, m_ssm_d, m_w_glu, m_b_glu, m_attn_out_g, m_ssm_out_g, m_w_out, m_norm2_g, m_w_mlp_in, m_w_mlp_out, v_norm1_g, v_w_in, v_q_norm_g, v_k_norm_g, v_ssm_lambda_re, v_ssm_lambda_im, v_ssm_log_dt, v_ssm_b_re, v_ssm_b_im, v_ssm_c_re, v_ssm_c_im, v_ssm_d, v_w_glu, v_b_glu, v_attn_out_g, v_ssm_out_g, v_w_out, v_norm2_g, v_w_mlp_in, v_w_mlp_out):
    batch, seq, d_model = x.shape
    tokens = batch * seq
    sb_width = w_in.shape[1]
    n_features = d_model

    big = [("w_in", w_in, m_w_in, v_w_in, 1), ("w_glu", w_glu, m_w_glu, v_w_glu, 0),
           ("w_out", w_out, m_w_out, v_w_out, 0), ("w_mlp_in", w_mlp_in, m_w_mlp_in, v_w_mlp_in, 1),
           ("w_mlp_out", w_mlp_out, m_w_mlp_out, v_w_mlp_out, 0)]
    small = [("norm1_g", norm1_g, m_norm1_g, v_norm1_g), ("q_norm_g", q_norm_g, m_q_norm_g, v_q_norm_g),
             ("k_norm_g", k_norm_g, m_k_norm_g, v_k_norm_g),
             ("ssm_lambda_re", ssm_lambda_re, m_ssm_lambda_re, v_ssm_lambda_re),
             ("ssm_lambda_im", ssm_lambda_im, m_ssm_lambda_im, v_ssm_lambda_im),
             ("ssm_log_dt", ssm_log_dt, m_ssm_log_dt, v_ssm_log_dt),
             ("ssm_b_re", ssm_b_re, m_ssm_b_re, v_ssm_b_re), ("ssm_b_im", ssm_b_im, m_ssm_b_im, v_ssm_b_im),
             ("ssm_c_re", ssm_c_re, m_ssm_c_re, v_ssm_c_re), ("ssm_c_im", ssm_c_im, m_ssm_c_im, v_ssm_c_im),
             ("ssm_d", ssm_d, m_ssm_d, v_ssm_d), ("b_glu", b_glu, m_b_glu, v_b_glu),
             ("attn_out_g", attn_out_g, m_attn_out_g, v_attn_out_g), ("ssm_out_g", ssm_out_g, m_ssm_out_g, v_ssm_out_g),
             ("norm2_g", norm2_g, m_norm2_g, v_norm2_g)]

    gather_in, tok_in = _chip_exchange_start("gather_w_in_start", "gather", [(w_in.astype(BF16), 1)])
    gather_rest, tok_rest = _chip_exchange_start(
        "gather_rest_start", "gather", [(w.astype(BF16), axis) for _, w, _, _, axis in big[1:]], after=tok_in)

    x2 = x.reshape(tokens, d_model)
    tgt2 = loss_target.reshape(tokens, d_model)
    g1, g2 = norm1_g[None, :], norm2_g[None, :]
    g_attn, g_ssm, bias_glu = attn_out_g[None, :], ssm_out_g[None, :], b_glu[None, :]
    heads = sb_width // HEAD_DIM
    qk_scale = 1.0 / math.sqrt(HEAD_DIM)
    gq, gk = (jnp.tile(q_norm_g, heads) * qk_scale)[None, :], jnp.tile(k_norm_g, heads)[None, :]
    lane_head = jnp.arange(sb_width) // HEAD_DIM
    ones_blocks = (lane_head[:, None] == lane_head[None, :]).astype(F32)

    (xn,) = _rowwise("norm1", _rms, [x2], [g1 + tok_rest], [(d_model, BF16)])
    (wf_in,) = _chip_exchange_wait("gather_w_in_wait", gather_in, xn)
    (proj,) = _mm("proj_in", xn, wf_in, "nn")

    def qkv_fn(q, k, v, gq_, gk_, ones):
        return _head_rms(q, gq_, ones), _head_rms(k, gk_, ones), v

    qn, kn, vb = _rowwise("qk_norm", qkv_fn, [(proj, sb_width, 0), (proj, sb_width, 1), (proj, sb_width, 2)],
                          [gq, gk, ones_blocks], [(sb_width, BF16)] * 3)
    sb, c_tot = _attn_fwd(qn, kn, vb, batch=batch, seq=seq, bq=ATTN_BQ, bk=ATTN_BK)

    s5_params = (ssm_lambda_re, ssm_lambda_im, ssm_log_dt, ssm_b_re, ssm_b_im, ssm_c_re, ssm_c_im, ssm_d)
    (kt_row, b_mat, c_mat, la, lb), s5_vjp = jax.vjp(_s5_operators, *s5_params)
    ug = _tokens_to_groups("u_to_groups", proj, 3, sb_width)
    yg, xin = _s5_fwd(ug, kt_row, b_mat, c_mat, la, lb, batch=batch)
    y_ssm = _groups_to_tokens("y_to_tokens", yg)

    (y_act,) = _rowwise("gelu", _gelu, [y_ssm], [], [(sb_width, BF16)])
    wf_glu, wf_out, wf_mlp_in, wf_mlp_out = _chip_exchange_wait("gather_rest_wait", gather_rest, y_act)
    (gate_pre,) = _mm("glu_gate", y_act, wf_glu, "nn", extras=[(bias_glu, "row")], epilogue=lambda acc, b: acc + b)
    (mixed,) = _rowwise("mix_norm", _mixed, [sb, y_ssm, gate_pre], [g_attn, g_ssm], [(2 * sb_width, BF16)])
    (h1,) = _mm("proj_out", mixed, wf_out, "nn", extras=[(x2, "tile")], epilogue=lambda acc, r: acc + r)
    (hn,) = _rowwise("norm2", _rms, [h1], [g2], [(d_model, BF16)])
    def mlp_act(acc):
        r = jnp.maximum(acc, 0.0)
        return r * r, r

    act, act_root = _mm("mlp_in", hn, wf_mlp_in, "nn", tk=1024, epilogue=mlp_act, out_dtypes=(BF16, BF16))
    inv_n = 1.0 / n_features

    def loss_head(acc, r, t):
        d = ((acc + r) - t) * inv_n
        return d, d, jnp.sum(d * d, keepdims=True) * (0.5 * n_features)

    dy, dy_b, loss_tiles = _mm("mlp_out_loss", act, wf_mlp_out, "nn", tn=512, tk=1024, extras=[(h1, "tile"), (tgt2, "tile")],
                               epilogue=loss_head, out_dtypes=(F32, BF16), tile_sums=1)
    loss_part = jnp.sum(loss_tiles)

    (dw_mlp_out,) = _mm("dw_mlp_out", act, dy_b, "tn", tk=1024)
    (dpre,) = _mm("d_mlp_act", dy_b, wf_mlp_out, "nt", tk=1024, extras=[(act_root, "tile")],
                  epilogue=lambda acc, r: acc * (2.0 * r.astype(F32)), out_dtypes=(BF16,))
    (dw_mlp_in,) = _mm("dw_mlp_in", hn, dpre, "tn", tk=1024)
    scatter_mlp, tok_mlp = _chip_exchange_start("scatter_mlp_start", "scatter", [(dw_mlp_in, 1), (dw_mlp_out, 0)])
    (dhn,) = _mm("d_norm2_in", dpre, wf_mlp_in, "nt", tk=1024)

    def norm_bwd(res, hx, dn, g):
        _, vjp = jax.vjp(_rms, hx, g)
        dh, dg = vjp(dn)
        return res + dh, dg

    dh1, dg_norm2 = _rowwise("norm2_bwd", norm_bwd, [dy, h1, dhn], [g2 + tok_mlp], [(d_model, F32)], [(1, d_model)])
    (dmixed,) = _mm("d_mixed", dh1, wf_out, "nt")
    (dw_out,) = _mm("dw_out", mixed, dh1, "tn")

    def mixed_bwd(dm, sb_, ys, gp, ga, gs):
        _, vjp = jax.vjp(lambda a, act, b, c, d: jnp.concatenate(
            [_rms(a, c), _rms(act * jax.nn.sigmoid(b), d)], axis=-1), sb_, _gelu(ys), gp, ga, gs)
        dsb_, dact, dgp_, dga, dgs = vjp(dm)
        return dsb_, dgp_, dact, dga, dgs, jnp.sum(dgp_, axis=0, keepdims=True)

    dsb, dgate_pre, dact_part, dg_attn, dg_ssm, db_glu = _rowwise(
        "mix_norm_bwd", mixed_bwd, [dmixed, sb, y_ssm, gate_pre], [g_attn, g_ssm],
        [(sb_width, F32), (sb_width, BF16), (sb_width, F32)], [(1, sb_width)] * 3)

    def gelu_bwd(acc, part, ys):
        _, vjp = jax.vjp(_gelu, ys)
        return vjp(acc + part)[0]

    (dy_ssm,) = _mm("d_glu_in", dgate_pre, wf_glu, "nt", extras=[(dact_part, "tile"), (y_ssm, "tile")], epilogue=gelu_bwd)
    (dw_glu,) = _mm("dw_glu", y_act, dgate_pre, "tn")
    scatter_mix, tok_mix = _chip_exchange_start("scatter_mix_start", "scatter", [(dw_glu, 0), (dw_out, 0)])

    dug, dkt_row, db_mat, dc_mat, dla, dlb = _s5_bwd(ug, _tokens_to_groups("dy_to_groups", dy_ssm, 0, sb_width), xin,
                                                     kt_row, b_mat, c_mat, la, lb + tok_mix, batch=batch)
    du = _groups_to_tokens("du_to_tokens", dug)
    ds5 = s5_vjp((dkt_row, db_mat, dc_mat, dla, dlb))

    dqn, dkn, dv = _attn_bwd(qn, kn, vb, c_tot, dsb, batch=batch, seq=seq, bq=ATTN_BQ, bk=ATTN_BK)

    def qk_bwd(q, k, dq_, dk_, gq_, gk_, ones):
        _, vjp_q = jax.vjp(lambda a, g: _head_rms(a, g, ones), q, gq_)
        _, vjp_k = jax.vjp(lambda a, g: _head_rms(a, g, ones), k, gk_)
        dq, dgq = vjp_q(dq_)
        dk, dgk = vjp_k(dk_)
        return dq, dk, dgq, dgk

    dq, dk, dgq, dgk = _rowwise("qk_norm_bwd", qk_bwd, [(proj, sb_width, 0), (proj, sb_width, 1), dqn, dkn],
                                [gq, gk, ones_blocks], [(sb_width, BF16)] * 2, [(1, sb_width)] * 2)
    dproj = jnp.concatenate([dq, dk, dv.astype(BF16), du.astype(BF16)], axis=1)
    (dw_in,) = _mm("dw_in", xn, dproj, "tn")
    scatter_in, tok_w_in = _chip_exchange_start("scatter_in_start", "scatter", [(dw_in, 1)])
    (dxn,) = _mm("d_norm1_in", dproj, wf_in, "nt")
    dx, dg_norm1 = _rowwise("norm1_bwd", norm_bwd, [dh1, x2, dxn], [g1 + tok_w_in], [(d_model, F32)], [(1, d_model)])

    small_grads = [dg_norm1[0], dgq.reshape(heads, HEAD_DIM).sum(0) * qk_scale, dgk.reshape(heads, HEAD_DIM).sum(0), *ds5,
                   db_glu[0], dg_attn[0], dg_ssm[0], dg_norm2[0]]
    order = ["norm1_g", "q_norm_g", "k_norm_g", "ssm_lambda_re", "ssm_lambda_im", "ssm_log_dt", "ssm_b_re", "ssm_b_im",
             "ssm_c_re", "ssm_c_im", "ssm_d", "b_glu", "attn_out_g", "ssm_out_g", "norm2_g"]
    assert order == [name for name, *_ in small]

    def pack(parts, extra=None):
        flat = [p.reshape(-1) for p in parts] + ([extra.reshape(-1)] if extra is not None else [])
        flat = jnp.concatenate(flat)
        rows = -(-flat.shape[0] // (LANES * LANES)) * LANES
        return jnp.pad(flat, (0, rows * LANES - flat.shape[0])).reshape(rows, LANES)

    n_small = sum(w.size for _, w, _, _ in small)
    small_exchange, _ = _chip_exchange_start("small_grads_start", "all", [(pack(small_grads, loss_part), 0)])

    def adam_big(sa, sb_, w, m, v):
        g = sa + sb_
        delta, m, v = _adamw(g, w, m, v)
        return g, delta, m, v

    def reduce_and_update(tag, params, slots):
        mine = [_sum_slots("sum_" + name, s) for s, (name, *_rest) in zip(slots, params)]
        theirs = _swap_with_sibling("swap_" + tag, mine)
        return {name: _rowwise("adamw_" + name, adam_big, [sa, sb_, w, m, v], [], [(w.shape[1], F32)] * 4)
                for (name, w, m, v, _), sa, sb_ in zip(params, mine, theirs)}

    started = small_exchange[3][0]
    slots_mlp_in, slots_mlp_out = _chip_exchange_wait("scatter_mlp_wait", scatter_mlp, started)
    slots_glu, slots_out = _chip_exchange_wait("scatter_mix_wait", scatter_mix, started)
    big_out = reduce_and_update("rest", big[1:], [slots_glu, slots_out, slots_mlp_in, slots_mlp_out])

    (small_slots,) = _chip_exchange_wait("small_grads_wait", small_exchange, big_out["w_mlp_out"][3])
    reduced = _sum_slots("sum_small", small_slots)
    loss = reduced.reshape(-1)[n_small]
    flat, small_g, off = reduced.reshape(-1), {}, 0
    for name, w, _, _ in small:
        small_g[name] = flat[off:off + w.size].reshape(w.shape)
        off += w.size
    small_upd = _adamw_small([small_g[name] for name, *_ in small], [w for _, w, _, _ in small],
                             [m for _, _, m, _ in small], [v for _, _, _, v in small])
    small_out = [small_g] + [{name: small_upd[kind][i] for i, (name, *_) in enumerate(small)} for kind in range(3)]

    (slots_in,) = _chip_exchange_wait("scatter_in_wait", scatter_in, reduced)
    big_out.update(reduce_and_update("w_in", big[:1], [slots_in]))
    names = ["norm1_g", "w_in", "q_norm_g", "k_norm_g", "ssm_lambda_re", "ssm_lambda_im", "ssm_log_dt", "ssm_b_re",
             "ssm_b_im", "ssm_c_re", "ssm_c_im", "ssm_d", "w_glu", "b_glu", "attn_out_g", "ssm_out_g", "w_out",
             "norm2_g", "w_mlp_in", "w_mlp_out"]
    outs = [loss, dx.reshape(batch, seq, d_model)]
    for kind in range(4):
        for name in names:
            outs.append(big_out[name][kind] if name in big_out else small_out[kind][name])
    return tuple(outs)
```

```python
import functools
import math

import jax
import jax.numpy as jnp
from jax import lax
from jax.experimental import pallas as pl
from jax.experimental.pallas import tpu as pltpu

F32 = jnp.float32
BF16 = jnp.bfloat16
F32_DOT = lax.Precision.HIGH
MESH = pl.DeviceIdType.MESH

RMS_EPS = 1e-6
HEAD_DIM = 64
SSM_GROUP = 16
SSM_CHUNK = 16
LANES = 128
N_CHIPS = 4
N_DEV = 8
VMEM_LIMIT = 48 * 1024 * 1024

ADAM_LR = 0.001
ADAM_B1 = 0.9
ADAM_B2 = 0.999
ADAM_EPS = 1e-08
ADAM_WD = 0.01
ADAM_STEP = 10


def _tile(n, pref):
    t = min(n, pref)
    while n % t:
        t //= 2
    return t


def _params(*sem):
    return pltpu.CompilerParams(dimension_semantics=sem, vmem_limit_bytes=VMEM_LIMIT)


_DIMS = {"nn": (((1,), (0,)), ((), ())), "nt": (((1,), (1,)), ((), ())), "tn": (((0,), (0,)), ((), ()))}


MM_VMEM_BUDGET = 40 * 1024 * 1024


def _mm_tiles(m, n, k, a_bytes, b_bytes, tile_bytes):
    best = None
    for tk in [t for t in (k, k // 2, k // 4, k // 8) if t >= 256 or t == k]:
        for tm in [t for t in (1024, 512, 256, 128) if t <= m and m % t == 0]:
            for tn in [t for t in (1024, 512, 256, 128) if t <= n and n % t == 0]:
                need = 2 * (tm * tk * a_bytes + tk * tn * b_bytes) + 2 * tm * tn * tile_bytes + (tm * tn * 4 if tk < k else 0)
                if need > MM_VMEM_BUDGET:
                    continue
                traffic = m * k * a_bytes * (n // tn) + k * n * b_bytes * (m // tm)
                key = (tk < k, traffic, -tm * tn)
                if best is None or key < best[0]:
                    best = (key, (tm, tn, tk))
    return best[1]


def _mm(name, a, b, mode, *, a_fn=None, extras=(), epilogue=None, out_dtypes=(F32,), tile_sums=0):
    if mode == "nn":
        (m, k), n = a.shape, b.shape[1]
    elif mode == "nt":
        (m, k), n = a.shape, b.shape[0]
    else:
        (k, m), n = a.shape, b.shape[1]
    tile_bytes = sum(e.dtype.itemsize for e, kind in extras if kind == "tile") + sum(jnp.dtype(d).itemsize for d in out_dtypes)
    tm, tn, tk = _mm_tiles(m, n, k, a.dtype.itemsize, b.dtype.itemsize, tile_bytes)
    nk = k // tk
    ne, nout = len(extras), len(out_dtypes)
    dims = _DIMS[mode]

    def body(a_ref, b_ref, *rest):
        ex, outs, sums = rest[:ne], rest[ne:ne + nout], rest[ne + nout:ne + nout + tile_sums]
        at = a_ref[...]
        if a_fn is not None:
            at = a_fn(at)
        p = lax.dot_general(at.astype(BF16), b_ref[...].astype(BF16), dims, preferred_element_type=F32)

        def finish(r):
            if epilogue is not None:
                r = epilogue(r, *[e[...] for e in ex])
            if not isinstance(r, (tuple, list)):
                r = (r,)
            for o, v in zip(outs, r[:nout]):
                o[...] = v.astype(o.dtype)
            first = ((lax.broadcasted_iota(jnp.int32, (8, LANES), 0) == 0)
                     & (lax.broadcasted_iota(jnp.int32, (8, LANES), 1) == 0))
            for o, v in zip(sums, r[nout:]):
                o[...] = jnp.where(first, v, 0.0)

        if nk == 1:
            finish(p)
        else:
            acc = rest[ne + nout + tile_sums]
            kk = pl.program_id(2)

            @pl.when(kk == 0)
            def _():
                acc[...] = p

            @pl.when(kk > 0)
            def _():
                acc[...] += p

            @pl.when(kk == nk - 1)
            def _():
                finish(acc[...])

    if mode == "tn":
        a_spec = pl.BlockSpec((tk, tm), lambda i, j, kk: (kk, i))
    else:
        a_spec = pl.BlockSpec((tm, tk), lambda i, j, kk: (i, kk))
    if mode == "nt":
        b_spec = pl.BlockSpec((tn, tk), lambda i, j, kk: (j, kk))
    else:
        b_spec = pl.BlockSpec((tk, tn), lambda i, j, kk: (kk, j))
    ex_specs = []
    for _, kind in extras:
        if kind == "tile":
            ex_specs.append(pl.BlockSpec((tm, tn), lambda i, j, kk: (i, j)))
        else:
            ex_specs.append(pl.BlockSpec((1, tn), lambda i, j, kk: (0, j)))
    return pl.pallas_call(
        body, name=name, grid=(m // tm, n // tn, nk),
        in_specs=[a_spec, b_spec] + ex_specs,
        out_specs=([pl.BlockSpec((tm, tn), lambda i, j, kk: (i, j)) for _ in out_dtypes]
                   + [pl.BlockSpec((8, LANES), lambda i, j, kk: (i, j))] * tile_sums),
        out_shape=([jax.ShapeDtypeStruct((m, n), dt) for dt in out_dtypes]
                   + [jax.ShapeDtypeStruct((m // tm * 8, n // tn * LANES), F32)] * tile_sums),
        scratch_shapes=[pltpu.VMEM((tm, tn), F32)] if nk > 1 else [],
        compiler_params=_params("parallel", "parallel", "arbitrary"),
    )(a, b, *[e for e, _ in extras])


def _rowwise(name, fn, rows, consts, row_outs, acc_outs=(), tm=256):
    norm = [r if isinstance(r, tuple) else (r, r.shape[1], 0) for r in rows]
    t = norm[0][0].shape[0]
    tm = _tile(t, tm)
    nr, nc, no = len(norm), len(consts), len(row_outs)

    def body(*refs):
        outs = fn(*[r[...] for r in refs[:nr + nc]])
        if not isinstance(outs, (tuple, list)):
            outs = (outs,)
        o_refs, a_refs = refs[nr + nc:nr + nc + no], refs[nr + nc + no:]
        for r, v in zip(o_refs, outs[:no]):
            r[...] = v.astype(r.dtype)
        if a_refs:
            i = pl.program_id(0)

            @pl.when(i == 0)
            def _():
                for r, v in zip(a_refs, outs[no:]):
                    r[...] = v

            @pl.when(i > 0)
            def _():
                for r, v in zip(a_refs, outs[no:]):
                    r[...] += v

    in_specs = [pl.BlockSpec((tm, w), functools.partial(lambda i, cb: (i, cb), cb=cb)) for _, w, cb in norm]
    in_specs += [pl.BlockSpec(c.shape, functools.partial(lambda i, nd: (0,) * nd, nd=c.ndim)) for c in consts]
    out_specs = [pl.BlockSpec((tm, w), lambda i: (i, 0)) for w, _ in row_outs]
    out_specs += [pl.BlockSpec(s, functools.partial(lambda i, nd: (0,) * nd, nd=len(s))) for s in acc_outs]
    out_shape = [jax.ShapeDtypeStruct((t, w), dt) for w, dt in row_outs]
    out_shape += [jax.ShapeDtypeStruct(s, F32) for s in acc_outs]
    return pl.pallas_call(
        body, name=name, grid=(t // tm,), in_specs=in_specs, out_specs=out_specs, out_shape=out_shape,
        compiler_params=_params("arbitrary"),
    )(*[r[0] for r in norm], *consts)


def _rms(x, g):
    return x * lax.rsqrt(jnp.mean(x * x, axis=-1, keepdims=True) + RMS_EPS) * g


def _head_rms(x, g, ones_blocks):
    ss = jnp.dot(x * x, ones_blocks, precision=F32_DOT, preferred_element_type=F32)
    return x * lax.rsqrt(ss * (1.0 / HEAD_DIM) + RMS_EPS) * g


def _gelu(x):
    return x * (0.5 * (1.0 + jnp.tanh(math.sqrt(2.0 / math.pi) * (x + 0.044715 * (x * x * x)))))


def _mixed(sb, y_ssm, gate_pre, g_attn, g_ssm):
    ssm = _gelu(y_ssm) * jax.nn.sigmoid(gate_pre)
    return jnp.concatenate([_rms(sb, g_attn), _rms(ssm, g_ssm)], axis=-1)


def _softplus(z):
    return jnp.maximum(z, 0.0) + jnp.log(1.0 + jnp.exp(-jnp.abs(z)))


def _suffix_sums(x, tri):
    hi = x.astype(BF16)
    lo = (x - hi.astype(F32)).astype(BF16)
    return (jnp.dot(hi, tri, preferred_element_type=F32) + jnp.dot(lo, tri, preferred_element_type=F32))


def _dot_nt(a, b, **kw):
    return lax.dot_general(a, b, _DIMS["nt"], preferred_element_type=F32, **kw)


def _dot_tn(a, b, **kw):
    return lax.dot_general(a, b, _DIMS["tn"], preferred_element_type=F32, **kw)


ATTN_BQ, ATTN_BK = 2048, 256
HEAD_LANES = tuple(slice(h * HEAD_DIM, (h + 1) * HEAD_DIM) for h in range(LANES // HEAD_DIM))


def _attn_fwd(qs, kn, v, *, batch, seq, bq, bk):
    width = qs.shape[1]
    bq = _tile(seq, bq)
    bk = _tile(bq, bk)
    nq, kpq = seq // bq, bq // bk

    def body(q_ref, k_ref, v_ref, o_ref, c_ref):
        row = lax.broadcasted_iota(jnp.int32, (bq, bk), 0)
        col = lax.broadcasted_iota(jnp.int32, (bq, bk), 1)
        tri = (lax.broadcasted_iota(jnp.int32, (bk, bk), 0) >= lax.broadcasted_iota(jnp.int32, (bk, bk), 1)).astype(BF16)

        def q_block(qi, carry):
            r0 = pl.multiple_of(qi * bq, bq)
            qh = [q_ref[pl.ds(r0, bq), ln] for ln in HEAD_LANES]

            def tile(k0, state, top=0):
                diag = top is not None
                top = top or 0
                msk = (col < row)[:bq - top] if diag else None
                new = []
                for h, ln in enumerate(HEAD_LANES):
                    o, c = state[2 * h], state[2 * h + 1]
                    z = _dot_nt(qh[h][top:], k_ref[pl.ds(k0, bk), ln])
                    sp = _softplus(z)
                    if diag:
                        sp = jnp.where(msk, sp, 0.0)
                    r = _suffix_sums(sp, tri)
                    a = jnp.exp(z - r - c[top:])
                    if diag:
                        a = jnp.where(msk, a, 0.0)
                    o_new = o[top:] + jnp.dot(a.astype(BF16), v_ref[pl.ds(k0, bk), ln], preferred_element_type=F32)
                    c_new = c[top:] + r[:, 0:1]
                    if top:
                        o_new, c_new = jnp.concatenate([o[:top], o_new]), jnp.concatenate([c[:top], c_new])
                    new += [o_new, c_new]
                return tuple(new)

            state = (jnp.zeros((bq, HEAD_DIM), F32), jnp.zeros((bq, 1), F32)) * len(HEAD_LANES)
            for d in reversed(range(kpq)):
                state = tile(pl.multiple_of(r0 + d * bk, bk), state, top=d * bk)
            state = lax.fori_loop(0, qi * kpq, lambda it, st: tile(pl.multiple_of(r0 - (it + 1) * bk, bk), st, None),
                                  state)
            for h, ln in enumerate(HEAD_LANES):
                o_ref[pl.ds(r0, bq), ln] = state[2 * h]
                c_ref[pl.ds(r0, bq), ln] = jnp.broadcast_to(state[2 * h + 1], (bq, HEAD_DIM))
            return carry

        lax.fori_loop(0, nq, q_block, 0)

    spec = pl.BlockSpec((seq, LANES), lambda b, h: (b, h))
    shape = jax.ShapeDtypeStruct((batch * seq, width), F32)
    return pl.pallas_call(
        body, name="attn_fwd", grid=(batch, width // LANES), in_specs=[spec, spec, spec], out_specs=[spec, spec],
        out_shape=[shape, shape], compiler_params=_params("parallel", "parallel"),
    )(qs, kn, v)


def _attn_bwd(qs, kn, v, c_tot, do, *, batch, seq, bq, bk):
    width = qs.shape[1]
    bq = _tile(seq, bq)
    bk = _tile(bq, bk)
    nq, kpq = seq // bq, bq // bk

    def body(q_ref, k_ref, v_ref, c_ref, do_ref, dq_ref, dk_ref, dv_ref):
        row = lax.broadcasted_iota(jnp.int32, (bq, bk), 0)
        col = lax.broadcasted_iota(jnp.int32, (bq, bk), 1)
        sq_row = lax.broadcasted_iota(jnp.int32, (bk, bk), 0)
        sq_col = lax.broadcasted_iota(jnp.int32, (bk, bk), 1)
        tri = (sq_row >= sq_col).astype(BF16)
        tri_t = (sq_row <= sq_col).astype(BF16)
        dk_ref[...] = jnp.zeros_like(dk_ref)
        dv_ref[...] = jnp.zeros_like(dv_ref)

        def q_block(qi, carry):
            r0 = pl.multiple_of(qi * bq, bq)
            qh = [q_ref[pl.ds(r0, bq), ln] for ln in HEAD_LANES]
            d_out = [do_ref[pl.ds(r0, bq), ln].astype(BF16) for ln in HEAD_LANES]
            c_all = [c_ref[pl.ds(r0, bq), ln][:, 0:1] for ln in HEAD_LANES]

            def tile(k0, state, top=0):
                diag = top is not None
                top = top or 0
                last = diag and top == bq - bk
                msk = (col < row)[:bq - top] if diag else None
                new = []
                for h, ln in enumerate(HEAD_LANES):
                    c_left, g_left, dq = state[3 * h:3 * h + 3]
                    q, d_o = qh[h][top:], d_out[h][top:]
                    k = k_ref[pl.ds(k0, bk), ln]
                    z = _dot_nt(q, k)
                    e = jnp.exp(-jnp.abs(z))
                    sp = jnp.maximum(z, 0.0) + jnp.log(1.0 + e)
                    if diag:
                        sp = jnp.where(msk, sp, 0.0)
                    r = _suffix_sums(sp, tri)
                    c_new = c_left[top:] + r[:, 0:1]
                    a = jnp.exp(z - r - (0.0 if last else c_all[h][top:] - c_new))
                    if diag:
                        a = jnp.where(msk, a, 0.0)
                    g = a * _dot_nt(d_o, v_ref[pl.ds(k0, bk), ln])
                    pg = _suffix_sums(g, tri_t)
                    sig = jnp.where(z >= 0.0, 1.0, e) / (1.0 + e)
                    dz = g - sig * (g_left[top:] + pg)
                    if diag:
                        dz = jnp.where(msk, dz, 0.0)
                    dz = dz.astype(BF16)
                    dk_ref[pl.ds(k0, bk), ln] += _dot_tn(dz, q)
                    dv_ref[pl.ds(k0, bk), ln] += _dot_tn(a.astype(BF16), d_o)
                    g_new = g_left[top:] + pg[:, bk - 1:bk]
                    dq_new = dq[top:] + jnp.dot(dz, k, preferred_element_type=F32)
                    if top:
                        c_new = jnp.concatenate([c_left[:top], c_new])
                        g_new = jnp.concatenate([g_left[:top], g_new])
                        dq_new = jnp.concatenate([dq[:top], dq_new])
                    new += [c_new, g_new, dq_new]
                return tuple(new)

            zero = jnp.zeros((bq, 1), F32)
            init = (zero, zero, jnp.zeros((bq, HEAD_DIM), F32)) * len(HEAD_LANES)
            state = lax.fori_loop(0, qi * kpq, lambda it, st: tile(pl.multiple_of(it * bk, bk), st, None), init)
            for d in range(kpq):
                state = tile(pl.multiple_of(r0 + d * bk, bk), state, top=d * bk)
            for h, ln in enumerate(HEAD_LANES):
                dq_ref[pl.ds(r0, bq), ln] = state[3 * h + 2]
            return carry

        lax.fori_loop(0, nq, q_block, 0)

    spec = pl.BlockSpec((seq, LANES), lambda b, h: (b, h))
    shape = jax.ShapeDtypeStruct((batch * seq, width), F32)
    return pl.pallas_call(
        body, name="attn_bwd", grid=(batch, width // LANES), in_specs=[spec] * 5, out_specs=[spec] * 3,
        out_shape=[shape] * 3, compiler_params=_params("parallel", "parallel"),
    )(qs, kn, v, c_tot, do)


def _s5_operators(lam_re, lam_im, log_dt, b_re, b_im, c_re, c_im, d_skip):
    groups, n_state, n_ch = b_re.shape
    cs = SSM_CHUNK
    dt = jnp.exp(log_dt)[:, None]
    steps = jnp.arange(cs + 1, dtype=F32)[None, :, None]
    mag = jnp.exp(steps * (lam_re * dt)[:, None, :])
    ang = steps * (lam_im * dt)[:, None, :]
    pw_re, pw_im = mag * jnp.cos(ang), mag * jnp.sin(ang)
    num_re, num_im = pw_re[:, 1] - 1.0, pw_im[:, 1]
    den = lam_re * lam_re + lam_im * lam_im
    cf_re = (num_re * lam_re + num_im * lam_im) / den
    cf_im = (num_im * lam_re - num_re * lam_im) / den
    bb_re = cf_re[:, :, None] * b_re - cf_im[:, :, None] * b_im
    bb_im = cf_re[:, :, None] * b_im + cf_im[:, :, None] * b_re
    width = cs * n_ch
    ct_re, ct_im = c_re.transpose(0, 2, 1), c_im.transpose(0, 2, 1)

    def c_times_powers(first):
        pr = pw_re[:, first:first + cs].transpose(0, 2, 1)[:, :, :, None]
        pi = pw_im[:, first:first + cs].transpose(0, 2, 1)[:, :, :, None]
        re = pr * ct_re[:, :, None, :] - pi * ct_im[:, :, None, :]
        im = pr * ct_im[:, :, None, :] + pi * ct_re[:, :, None, :]
        return re.reshape(groups, n_state, width), im.reshape(groups, n_state, width)

    w_re, w_im = c_times_powers(0)
    kt_row = (jnp.einsum("gpi,gpw->giw", bb_re, w_re, precision=F32_DOT)
              - jnp.einsum("gpi,gpw->giw", bb_im, w_im, precision=F32_DOT))
    kt_row = kt_row + jnp.pad(jnp.eye(n_ch, dtype=F32)[None] * d_skip[:, None, :], ((0, 0), (0, 0), (0, width - n_ch)))
    rp_re, rp_im = pw_re[:, cs - 1::-1][:, :cs], pw_im[:, cs - 1::-1][:, :cs]
    bm_re = rp_re[:, :, None, :] * bb_re.transpose(0, 2, 1)[:, None] - rp_im[:, :, None, :] * bb_im.transpose(0, 2, 1)[:, None]
    bm_im = rp_re[:, :, None, :] * bb_im.transpose(0, 2, 1)[:, None] + rp_im[:, :, None, :] * bb_re.transpose(0, 2, 1)[:, None]
    b_mat = jnp.concatenate([bm_re, bm_im], axis=-1).reshape(groups, width, 2 * n_state)
    w1_re, w1_im = c_times_powers(1)
    c_mat = jnp.concatenate([w1_re, -w1_im], axis=1)
    la = jnp.concatenate([pw_re[:, cs], pw_re[:, cs]], axis=-1)[:, None, :]
    lb = jnp.concatenate([-pw_im[:, cs], pw_im[:, cs]], axis=-1)[:, None, :]
    return kt_row, b_mat, c_mat, la, lb


GROUPS_PER_BLOCK = LANES // SSM_GROUP


def _tokens_to_groups(name, u, col_block, width):
    t = u.shape[0]
    n = t // SSM_CHUNK
    ch = SSM_CHUNK * SSM_GROUP
    blocks = width // LANES

    def body(u_ref, o_ref):
        for s in range(SSM_CHUNK):
            rows = u_ref[pl.ds(s, n, stride=SSM_CHUNK), :]
            for g in range(GROUPS_PER_BLOCK):
                o_ref[g, :, s * SSM_GROUP:(s + 1) * SSM_GROUP] = rows[:, g * SSM_GROUP:(g + 1) * SSM_GROUP]

    return pl.pallas_call(
        body, name=name, grid=(blocks,),
        in_specs=[pl.BlockSpec((t, LANES), lambda j: (0, col_block * blocks + j))],
        out_specs=pl.BlockSpec((GROUPS_PER_BLOCK, n, ch), lambda j: (j, 0, 0)),
        out_shape=jax.ShapeDtypeStruct((width // SSM_GROUP, n, ch), F32), compiler_params=_params("parallel"),
    )(u)


def _groups_to_tokens(name, ug):
    groups, n, ch = ug.shape

    def body(g_ref, o_ref, rows_ref):
        for s in range(SSM_CHUNK):
            for g in range(GROUPS_PER_BLOCK):
                rows_ref[s % 2, :, g * SSM_GROUP:(g + 1) * SSM_GROUP] = g_ref[g, :, s * SSM_GROUP:(s + 1) * SSM_GROUP]
            o_ref[pl.ds(s, n, stride=SSM_CHUNK), :] = rows_ref[s % 2]

    return pl.pallas_call(
        body, name=name, grid=(groups // GROUPS_PER_BLOCK,),
        in_specs=[pl.BlockSpec((GROUPS_PER_BLOCK, n, ch), lambda j: (j, 0, 0))],
        out_specs=pl.BlockSpec((n * SSM_CHUNK, LANES), lambda j: (0, j)),
        out_shape=jax.ShapeDtypeStruct((n * SSM_CHUNK, groups * SSM_GROUP), F32),
        scratch_shapes=[pltpu.VMEM((2, n, LANES), F32)], compiler_params=_params("parallel"),
    )(ug)


SCAN_ROWS = 8


def _toeplitz_to(tm_ref, g, kt_row):
    width = kt_row.shape[1]
    tm_ref[g] = jnp.zeros((width, width), F32)
    for s in range(SSM_CHUNK):
        tm_ref[g, s * SSM_GROUP:(s + 1) * SSM_GROUP, s * SSM_GROUP:] = kt_row[:, :width - s * SSM_GROUP]


def _lam_powers(la, lb, reverse):
    if reverse:
        lb = -lb

    def mul(p, q):
        return p[0] * q[0] - p[1] * q[1], p[0] * q[1] + p[1] * q[0]

    p1 = (la, lb)
    p2 = mul(p1, p1)
    p3 = mul(p2, p1)
    p4 = mul(p2, p2)
    rows = [p1, p2, p3, p4, mul(p4, p1), mul(p4, p2), mul(p4, p3), mul(p4, p4)]
    if reverse:
        rows = rows[::-1]
    idx = lax.broadcasted_iota(jnp.int32, (SCAN_ROWS, la.shape[1]), 0)
    tab_a = sum(jnp.where(idx == j, r[0], 0.0) for j, r in enumerate(rows))
    tab_b = sum(jnp.where(idx == j, r[1], 0.0) for j, r in enumerate(rows))
    return (p1, p2, p4), (tab_a, tab_b), idx


def _scan_block(e, carry, steps, table, idx, half, reverse):
    n = SCAN_ROWS
    for d, (pa, pb) in zip((1, 2, 4), steps):
        sh = pltpu.roll(e, n - d if reverse else d, 0)
        sh = jnp.where(idx < n - d if reverse else idx >= d, sh, 0.0)
        e = e + pa * sh + pb * pltpu.roll(sh, half, 1)
    tab_a, tab_b = table
    e = e + tab_a * carry + tab_b * pltpu.roll(carry, half, 1)
    shifted = jnp.where(idx == (n - 1 if reverse else 0), carry, pltpu.roll(e, n - 1 if reverse else 1, 0))
    edge = e[0:1] if reverse else e[n - 1:n]
    return shifted, jnp.broadcast_to(edge, e.shape)


def _s5_fwd(ug, kt_row, b_mat, c_mat, la, lb, *, batch, gb=8):
    groups, n, ch = ug.shape
    p2 = b_mat.shape[2]
    gb = _tile(groups, gb)
    nch = n // batch
    nblk = nch // SCAN_ROWS

    def body(u_ref, k_ref, b_ref, c_ref, la_ref, lb_ref, y_ref, x_ref, s_ref, tm_ref):
        for g in range(gb):
            _toeplitz_to(tm_ref, g, k_ref[g])
            s_ref[g] = jnp.dot(u_ref[g], b_ref[g], precision=F32_DOT, preferred_element_type=F32)
        powers = [_lam_powers(la_ref[g], lb_ref[g], False) for g in range(gb)]

        def step(blk, carries):
            new = []
            for g in range(gb):
                steps, table, idx = powers[g]
                for b in range(batch):
                    rows = pl.ds(pl.multiple_of(b * nch + blk * SCAN_ROWS, SCAN_ROWS), SCAN_ROWS)
                    x_in, carry = _scan_block(s_ref[g, rows, :], carries[g * batch + b], steps, table, idx, p2 // 2, False)
                    x_ref[g, rows, :] = x_in
                    new.append(carry)
            return tuple(new)

        lax.fori_loop(0, nblk, step, tuple(jnp.zeros((SCAN_ROWS, p2), F32) for _ in range(gb * batch)))
        for g in range(gb):
            y_ref[g] = (jnp.dot(u_ref[g], tm_ref[g], precision=F32_DOT, preferred_element_type=F32)
                        + jnp.dot(x_ref[g], c_ref[g], precision=F32_DOT, preferred_element_type=F32))

    def spec(a, b):
        return pl.BlockSpec((gb, a, b), lambda i: (i, 0, 0))

    return pl.pallas_call(
        body, name="s5_fwd", grid=(groups // gb,),
        in_specs=[spec(n, ch), spec(SSM_GROUP, ch), spec(ch, p2), spec(p2, ch), spec(1, p2), spec(1, p2)],
        out_specs=[spec(n, ch), spec(n, p2)],
        out_shape=[jax.ShapeDtypeStruct((groups, n, ch), F32), jax.ShapeDtypeStruct((groups, n, p2), F32)],
        scratch_shapes=[pltpu.VMEM((gb, n, p2), F32), pltpu.VMEM((gb, ch, ch), F32)],
        compiler_params=_params("parallel"),
    )(ug, kt_row, b_mat, c_mat, la, lb)


def _s5_bwd(ug, dyg, xin, kt_row, b_mat, c_mat, la, lb, *, batch, gb=8):
    groups, n, ch = ug.shape
    p2 = b_mat.shape[2]
    gb = _tile(groups, gb)
    nch = n // batch
    nblk = nch // SCAN_ROWS

    def body(u_ref, dy_ref, x_ref, k_ref, b_ref, c_ref, la_ref, lb_ref,
             du_ref, dk_ref, db_ref, dc_ref, dla_ref, dlb_ref, dx_ref, ds_ref, tm_ref):
        for g in range(gb):
            _toeplitz_to(tm_ref, g, k_ref[g])
            dx_ref[g] = _dot_nt(dy_ref[g], c_ref[g], precision=F32_DOT)
        powers = [_lam_powers(la_ref[g], lb_ref[g], True) for g in range(gb)]

        def step(it, carries):
            new = []
            for g in range(gb):
                steps, table, idx = powers[g]
                for b in range(batch):
                    rows = pl.ds(pl.multiple_of(b * nch + (nblk - 1 - it) * SCAN_ROWS, SCAN_ROWS), SCAN_ROWS)
                    d_s, carry = _scan_block(dx_ref[g, rows, :], carries[g * batch + b], steps, table, idx, p2 // 2, True)
                    ds_ref[g, rows, :] = d_s
                    new.append(carry)
            return tuple(new)

        lax.fori_loop(0, nblk, step, tuple(jnp.zeros((SCAN_ROWS, p2), F32) for _ in range(gb * batch)))
        for g in range(gb):
            u, dy, ds, x = u_ref[g], dy_ref[g], ds_ref[g], x_ref[g]
            du_ref[g] = _dot_nt(dy, tm_ref[g], precision=F32_DOT) + _dot_nt(ds, b_ref[g], precision=F32_DOT)
            tm_ref[g] = _dot_tn(u, dy, precision=F32_DOT)
            dk_ref[g] = tm_ref[g, 0:SSM_GROUP, :]
            for s in range(1, SSM_CHUNK):
                dk_ref[g, :, :ch - s * SSM_GROUP] += tm_ref[g, s * SSM_GROUP:(s + 1) * SSM_GROUP, s * SSM_GROUP:]
            db_ref[g] = _dot_tn(u, ds, precision=F32_DOT)
            dc_ref[g] = _dot_tn(x, dy, precision=F32_DOT)
            dla_ref[g] = jnp.sum(ds * x, axis=0, keepdims=True)
            dlb_ref[g] = jnp.sum(ds * pltpu.roll(x, p2 // 2, 1), axis=0, keepdims=True)

    def spec(a, b):
        return pl.BlockSpec((gb, a, b), lambda i: (i, 0, 0))

    def shape(a, b):
        return jax.ShapeDtypeStruct((groups, a, b), F32)

    return pl.pallas_call(
        body, name="s5_bwd", grid=(groups // gb,),
        in_specs=[spec(n, ch), spec(n, ch), spec(n, p2), spec(SSM_GROUP, ch), spec(ch, p2), spec(p2, ch), spec(1, p2),
                  spec(1, p2)],
        out_specs=[spec(n, ch), spec(SSM_GROUP, ch), spec(ch, p2), spec(p2, ch), spec(1, p2), spec(1, p2)],
        out_shape=[shape(n, ch), shape(SSM_GROUP, ch), shape(ch, p2), shape(p2, ch), shape(1, p2), shape(1, p2)],
        scratch_shapes=[pltpu.VMEM((gb, n, p2), F32), pltpu.VMEM((gb, n, p2), F32), pltpu.VMEM((gb, ch, ch), F32)],
        compiler_params=_params("parallel"),
    )(ug, dyg, xin, kt_row, b_mat, c_mat, la, lb)


def _block(ref, axis, j, size):
    start = j * size if isinstance(j, int) else pl.multiple_of(j * size, size)
    return ref.at[pl.ds(start, size), :] if axis == 0 else ref.at[:, pl.ds(start, size)]


def _chip_exchange_copies(mode, axes, srcs, lands, send_sems, recv_sems, local_sems):
    x, y, c = lax.axis_index("x"), lax.axis_index("y"), lax.axis_index("c")
    everyone = mode == "all"
    me = 4 * x + 2 * y + c if everyone else 2 * x + y
    n_peers = _exchange_peers(mode)
    local, sends, arrivals = [], [], []
    for w, axis in enumerate(axes):
        if mode == "gather":
            size = srcs[w].shape[axis]
            local.append(pltpu.make_async_copy(srcs[w], _block(lands[w], axis, me, size), local_sems.at[w]))
        elif mode == "scatter":
            size = srcs[w].shape[axis] // N_CHIPS
            local.append(pltpu.make_async_copy(_block(srcs[w], axis, me, size), lands[w].at[me], local_sems.at[w]))
        else:
            local.append(pltpu.make_async_copy(srcs[w], lands[w].at[me], local_sems.at[w]))
        for k in range(1, n_peers + 1):
            bits = k if everyone else 2 * k
            px = 1 - x if bits & 4 else x
            py = 1 - y if bits & 2 else y
            pc = 1 - c if bits & 1 else c
            peer = 4 * px + 2 * py + pc if everyone else 2 * px + py
            if mode == "gather":
                src, dst, arrive = srcs[w], _block(lands[w], axis, me, size), _block(lands[w], axis, peer, size)
            elif mode == "scatter":
                src, dst, arrive = _block(srcs[w], axis, peer, size), lands[w].at[me], lands[w].at[peer]
            else:
                src, dst, arrive = srcs[w], lands[w].at[me], lands[w].at[peer]
            sem = w * n_peers + k - 1
            for target, out in ((dst, sends), (arrive, arrivals)):
                out.append(pltpu.make_async_remote_copy(
                    src_ref=src, dst_ref=target, send_sem=send_sems.at[sem], recv_sem=recv_sems.at[sem],
                    device_id=(px, py, pc), device_id_type=MESH))
    return local, sends, arrivals


def _exchange_peers(mode):
    return N_DEV - 1 if mode == "all" else N_CHIPS - 1


def _chip_exchange_start(name, mode, items, after=None):
    n = len(items)
    n_after = 0 if after is None else 1
    axes = [axis for _, axis in items]
    hbm = pl.BlockSpec(memory_space=pltpu.HBM)
    sem = pl.BlockSpec(memory_space=pltpu.SEMAPHORE)
    lands = []
    for a, axis in items:
        shape = list(a.shape)
        if mode == "gather":
            shape[axis] *= N_CHIPS
        elif mode == "scatter":
            shape[axis] //= N_CHIPS
            shape = [N_CHIPS] + shape
        else:
            shape = [N_DEV] + shape
        lands.append(pltpu.with_memory_space_constraint(lax.empty(tuple(shape), a.dtype), pltpu.HBM))

    def body(*refs):
        srcs, land_refs = refs[:n], refs[n:2 * n]
        send_sems, recv_sems, local_sems = refs[2 * n + n_after:2 * n + n_after + 3]
        token = refs[-1]
        local, sends, _ = _chip_exchange_copies(mode, axes, srcs, land_refs, send_sems, recv_sems, local_sems)
        for cp in local + sends:
            cp.start()
        token[...] = jnp.zeros_like(token)

    n_sem = n * _exchange_peers(mode)
    outs = pl.pallas_call(
        body, name=name,
        out_shape=(pltpu.SemaphoreType.DMA((n_sem,)), pltpu.SemaphoreType.DMA((n_sem,)), pltpu.SemaphoreType.DMA((n,)),
                   *[pltpu.HBM(a.shape, a.dtype) for a, _ in items], *[pltpu.HBM(l.shape, l.dtype) for l in lands],
                   jax.ShapeDtypeStruct((8, LANES), F32)),
        in_specs=[hbm] * (2 * n) + [pl.BlockSpec(memory_space=pl.ANY)] * n_after,
        out_specs=(sem, sem, sem, *[hbm] * (2 * n), pl.BlockSpec(memory_space=pltpu.VMEM)),
        input_output_aliases={i: 3 + i for i in range(2 * n)},
        compiler_params=pltpu.CompilerParams(has_side_effects=pltpu.SideEffectType.DATAFLOW_SIDE_EFFECTING),
    )(*[pltpu.with_memory_space_constraint(a, pltpu.HBM) for a, _ in items], *lands, *([after] if n_after else []))
    return (mode, axes, outs[:3], outs[3:3 + n], outs[3 + n:3 + 2 * n]), outs[-1][0:1, 0:1]


def _chip_exchange_wait(name, handle, after):
    mode, axes, sems, srcs, lands = handle
    n = len(axes)
    after = list(after) if isinstance(after, (tuple, list)) else [after]
    hbm = pl.BlockSpec(memory_space=pltpu.HBM)
    sem = pl.BlockSpec(memory_space=pltpu.SEMAPHORE)

    def body(*refs):
        src_refs, land_refs = refs[:n], refs[n:2 * n]
        send_sems, recv_sems, local_sems = refs[2 * n:2 * n + 3]
        local, sends, arrivals = _chip_exchange_copies(mode, axes, src_refs, land_refs, send_sems, recv_sems, local_sems)
        for cp in sends:
            cp.wait_send()
        for cp in arrivals:
            cp.wait_recv()
        for cp in local:
            cp.wait()

    outs = pl.pallas_call(
        body, name=name,
        out_shape=(*[pltpu.HBM(a.shape, a.dtype) for a in srcs], *[pltpu.HBM(l.shape, l.dtype) for l in lands]),
        in_specs=[hbm] * (2 * n) + [sem] * 3 + [pl.BlockSpec(memory_space=pl.ANY)] * len(after), out_specs=[hbm] * (2 * n),
        input_output_aliases={i: i for i in range(2 * n)},
        compiler_params=pltpu.CompilerParams(has_side_effects=pltpu.SideEffectType.DATAFLOW_SIDE_EFFECTING),
    )(*srcs, *lands, *sems, *after)
    return outs[n:]


def _sum_slots(name, slots, tm=256):
    n_slots, r, c = slots.shape
    tm = _tile(r, tm)

    def body(*refs):
        acc = refs[0][...]
        for s_ref in refs[1:n_slots]:
            acc = acc + s_ref[...]
        refs[n_slots][...] = acc

    specs = [pl.BlockSpec((None, tm, c), functools.partial(lambda i, s: (s, i, 0), s=s)) for s in range(n_slots)]
    return pl.pallas_call(
        body, name=name, grid=(r // tm,), in_specs=specs, out_specs=pl.BlockSpec((tm, c), lambda i: (i, 0)),
        out_shape=jax.ShapeDtypeStruct((r, c), F32), compiler_params=_params("parallel"),
    )(*[slots] * n_slots)


def _swap_with_sibling(name, arrays):
    n = len(arrays)
    hbm = pl.BlockSpec(memory_space=pl.ANY)

    def body(*refs):
        ins, outs = refs[:n], refs[n:2 * n]
        send_sems, recv_sems = refs[2 * n:]
        sibling = (lax.axis_index("x"), lax.axis_index("y"), 1 - lax.axis_index("c"))
        copies = [pltpu.make_async_remote_copy(src_ref=ins[w], dst_ref=outs[w], send_sem=send_sems.at[w],
                                               recv_sem=recv_sems.at[w], device_id=sibling, device_id_type=MESH)
                  for w in range(n)]
        for cp in copies:
            cp.start()
        for cp in copies:
            cp.wait()

    return pl.pallas_call(
        body, name=name, in_specs=[hbm] * n, out_specs=[hbm] * n,
        out_shape=[jax.ShapeDtypeStruct(a.shape, a.dtype) for a in arrays],
        scratch_shapes=[pltpu.SemaphoreType.DMA((n,)), pltpu.SemaphoreType.DMA((n,))],
    )(*arrays)


def _adamw(g, w, m, v):
    m = ADAM_B1 * m + (1.0 - ADAM_B1) * g
    v = ADAM_B2 * v + (1.0 - ADAM_B2) * jnp.square(g)
    m_hat = m / (1.0 - ADAM_B1 ** ADAM_STEP)
    v_hat = v / (1.0 - ADAM_B2 ** ADAM_STEP)
    delta = -ADAM_LR * (m_hat / (jnp.sqrt(v_hat) + ADAM_EPS) + ADAM_WD * w)
    return delta, m, v


def _adamw_small(grads, ws, ms, vs):
    n = len(ws)
    vmem = pl.BlockSpec(memory_space=pltpu.VMEM)

    def body(*refs):
        for i in range(n):
            g, w, m, v = (refs[k * n + i][...] for k in range(4))
            for k, val in enumerate(_adamw(g, w, m, v)):
                refs[(4 + k) * n + i][...] = val

    outs = pl.pallas_call(
        body, name="adamw_small", in_specs=[vmem] * (4 * n), out_specs=[vmem] * (3 * n),
        out_shape=[jax.ShapeDtypeStruct(w.shape, F32) for _ in range(3) for w in ws],
        compiler_params=pltpu.CompilerParams(vmem_limit_bytes=VMEM_LIMIT),
    )(*grads, *ws, *ms, *vs)
    return outs[:n], outs[n:2 * n], outs[2 * n:]


def kernel(x, norm1_g, w_in, q_norm_g, k_norm_g, ssm_lambda_re, ssm_lambda_im, ssm_log_dt, ssm_b_re, ssm_b_im, ssm_c_re, ssm_c_im, ssm_d, w_glu, b_glu, attn_out_g, ssm_out_g, w_out, norm2_g, w_mlp_in, w_mlp_out, loss_target, m_norm1_g, m_w_in, m_q_norm_g, m_k_norm_g, m_ssm_lambda_re, m_ssm_lambda_im, m_ssm_log_dt, m_ssm_b_re, m_ssm_b_im, m_ssm_c_re, m_ssm_c_im, m_ssm_d, m_w_glu, m_b_glu, m_attn_out_g, m_ssm_out_g, m_w_out, m_norm2_g, m_w_mlp_in, m_w_mlp_out, v_norm1_g, v_w_in, v_q_norm_g, v_k_norm_g, v_ssm_lambda_re, v_ssm_lambda_im, v_ssm_log_dt, v_ssm_b_re, v_ssm_b_im, v_ssm_c_re, v_ssm_c_im, v_ssm_d, v_w_glu, v_b_glu, v_attn_out_g, v_ssm_out_g, v_w_out, v_norm2_g, v_w_mlp_in, v_w_mlp_out):
    batch, seq, d_model = x.shape
    tokens = batch * seq
    sb_width = w_in.shape[1]
    n_features = d_model

    big = [("w_in", w_in, m_w_in, v_w_in, 1), ("w_glu", w_glu, m_w_glu, v_w_glu, 0),
           ("w_out", w_out, m_w_out, v_w_out, 0), ("w_mlp_in", w_mlp_in, m_w_mlp_in, v_w_mlp_in, 1),
           ("w_mlp_out", w_mlp_out, m_w_mlp_out, v_w_mlp_out, 0)]
    small = [("norm1_g", norm1_g, m_norm1_g, v_norm1_g), ("q_norm_g", q_norm_g, m_q_norm_g, v_q_norm_g),
             ("k_norm_g", k_norm_g, m_k_norm_g, v_k_norm_g),
             ("ssm_lambda_re", ssm_lambda_re, m_ssm_lambda_re, v_ssm_lambda_re),
             ("ssm_lambda_im", ssm_lambda_im, m_ssm_lambda_im, v_ssm_lambda_im),
             ("ssm_log_dt", ssm_log_dt, m_ssm_log_dt, v_ssm_log_dt),
             ("ssm_b_re", ssm_b_re, m_ssm_b_re, v_ssm_b_re), ("ssm_b_im", ssm_b_im, m_ssm_b_im, v_ssm_b_im),
             ("ssm_c_re", ssm_c_re, m_ssm_c_re, v_ssm_c_re), ("ssm_c_im", ssm_c_im, m_ssm_c_im, v_ssm_c_im),
             ("ssm_d", ssm_d, m_ssm_d, v_ssm_d), ("b_glu", b_glu, m_b_glu, v_b_glu),
             ("attn_out_g", attn_out_g, m_attn_out_g, v_attn_out_g), ("ssm_out_g", ssm_out_g, m_ssm_out_g, v_ssm_out_g),
             ("norm2_g", norm2_g, m_norm2_g, v_norm2_g)]

    gather_in, tok_in = _chip_exchange_start("gather_w_in_start", "gather", [(w_in.astype(BF16), 1)])
    gather_rest, tok_rest = _chip_exchange_start(
        "gather_rest_start", "gather", [(w.astype(BF16), axis) for _, w, _, _, axis in big[1:]], after=tok_in)

    x2 = x.reshape(tokens, d_model)
    tgt2 = loss_target.reshape(tokens, d_model)
    g1, g2 = norm1_g[None, :], norm2_g[None, :]
    g_attn, g_ssm, bias_glu = attn_out_g[None, :], ssm_out_g[None, :], b_glu[None, :]
    heads = sb_width // HEAD_DIM
    qk_scale = 1.0 / math.sqrt(HEAD_DIM)
    gq, gk = (jnp.tile(q_norm_g, heads) * qk_scale)[None, :], jnp.tile(k_norm_g, heads)[None, :]
    lane_head = jnp.arange(sb_width) // HEAD_DIM
    ones_blocks = (lane_head[:, None] == lane_head[None, :]).astype(F32)

    (xn,) = _rowwise("norm1", _rms, [x2], [g1 + tok_rest], [(d_model, BF16)])
    s5_params = (ssm_lambda_re, ssm_lambda_im, ssm_log_dt, ssm_b_re, ssm_b_im, ssm_c_re, ssm_c_im, ssm_d)
    (kt_row, b_mat, c_mat, la, lb), s5_vjp = jax.vjp(_s5_operators, *s5_params)
    (wf_in,) = _chip_exchange_wait("gather_w_in_wait", gather_in, [xn, b_mat, c_mat])
    (proj,) = _mm("proj_in", xn, wf_in, "nn")

    def qkv_fn(q, k, v, gq_, gk_, ones):
        return _head_rms(q, gq_, ones), _head_rms(k, gk_, ones), v

    qn, kn, vb = _rowwise("qk_norm", qkv_fn, [(proj, sb_width, 0), (proj, sb_width, 1), (proj, sb_width, 2)],
                          [gq, gk, ones_blocks], [(sb_width, BF16)] * 3)
    sb, c_tot = _attn_fwd(qn, kn, vb, batch=batch, seq=seq, bq=ATTN_BQ, bk=ATTN_BK)
    ug = _tokens_to_groups("u_to_groups", proj, 3, sb_width)
    yg, xin = _s5_fwd(ug, kt_row, b_mat, c_mat, la, lb, batch=batch)
    y_ssm = _groups_to_tokens("y_to_tokens", yg)

    (y_act,) = _rowwise("gelu", _gelu, [y_ssm], [], [(sb_width, BF16)])
    wf_glu, wf_out, wf_mlp_in, wf_mlp_out = _chip_exchange_wait("gather_rest_wait", gather_rest, y_act)
    (gate_pre,) = _mm("glu_gate", y_act, wf_glu, "nn", extras=[(bias_glu, "row")], epilogue=lambda acc, b: acc + b)
    (mixed,) = _rowwise("mix_norm", _mixed, [sb, y_ssm, gate_pre], [g_attn, g_ssm], [(2 * sb_width, BF16)])
    (h1,) = _mm("proj_out", mixed, wf_out, "nn", extras=[(x2, "tile")], epilogue=lambda acc, r: acc + r)
    (hn,) = _rowwise("norm2", _rms, [h1], [g2], [(d_model, BF16)])
    def mlp_act(acc):
        r = jnp.maximum(acc, 0.0)
        return r * r, r

    act, act_root = _mm("mlp_in", hn, wf_mlp_in, "nn", epilogue=mlp_act, out_dtypes=(BF16, BF16))
    inv_n = 1.0 / n_features

    def loss_head(acc, r, t):
        d = ((acc + r) - t) * inv_n
        return d, d, jnp.sum(d * d, keepdims=True) * (0.5 * n_features)

    dy, dy_b, loss_tiles = _mm("mlp_out_loss", act, wf_mlp_out, "nn", extras=[(h1, "tile"), (tgt2, "tile")],
                               epilogue=loss_head, out_dtypes=(F32, BF16), tile_sums=1)
    loss_part = jnp.sum(loss_tiles)

    (dw_mlp_out,) = _mm("dw_mlp_out", act, dy_b, "tn")
    (dpre,) = _mm("d_mlp_act", dy_b, wf_mlp_out, "nt", extras=[(act_root, "tile")],
                  epilogue=lambda acc, r: acc * (2.0 * r.astype(F32)), out_dtypes=(BF16,))
    (dw_mlp_in,) = _mm("dw_mlp_in", hn, dpre, "tn")
    scatter_mlp, tok_mlp = _chip_exchange_start("scatter_mlp_start", "scatter", [(dw_mlp_in, 1), (dw_mlp_out, 0)])
    (dhn,) = _mm("d_norm2_in", dpre, wf_mlp_in, "nt")

    def norm_bwd(res, hx, dn, g):
        _, vjp = jax.vjp(_rms, hx, g)
        dh, dg = vjp(dn)
        return res + dh, dg

    dh1, dg_norm2 = _rowwise("norm2_bwd", norm_bwd, [dy, h1, dhn], [g2 + tok_mlp], [(d_model, F32)], [(1, d_model)])
    (dmixed,) = _mm("d_mixed", dh1, wf_out, "nt")
    (dw_out,) = _mm("dw_out", mixed, dh1, "tn")

    def mixed_bwd(dm, sb_, ys, gp, ga, gs):
        _, vjp = jax.vjp(lambda a, act, b, c, d: jnp.concatenate(
            [_rms(a, c), _rms(act * jax.nn.sigmoid(b), d)], axis=-1), sb_, _gelu(ys), gp, ga, gs)
        dsb_, dact, dgp_, dga, dgs = vjp(dm)
        return dsb_, dgp_, dact, dga, dgs, jnp.sum(dgp_, axis=0, keepdims=True)

    dsb, dgate_pre, dact_part, dg_attn, dg_ssm, db_glu = _rowwise(
        "mix_norm_bwd", mixed_bwd, [dmixed, sb, y_ssm, gate_pre], [g_attn, g_ssm],
        [(sb_width, F32), (sb_width, BF16), (sb_width, F32)], [(1, sb_width)] * 3)

    def gelu_bwd(acc, part, ys):
        _, vjp = jax.vjp(_gelu, ys)
        return vjp(acc + part)[0]

    (dy_ssm,) = _mm("d_glu_in", dgate_pre, wf_glu, "nt", extras=[(dact_part, "tile"), (y_ssm, "tile")], epilogue=gelu_bwd)
    (dw_glu,) = _mm("dw_glu", y_act, dgate_pre, "tn")
    scatter_mix, tok_mix = _chip_exchange_start("scatter_mix_start", "scatter", [(dw_glu, 0), (dw_out, 0)])

    dug, dkt_row, db_mat, dc_mat, dla, dlb = _s5_bwd(ug, _tokens_to_groups("dy_to_groups", dy_ssm, 0, sb_width), xin,
                                                     kt_row, b_mat, c_mat, la, lb + tok_mix, batch=batch)
    du = _groups_to_tokens("du_to_tokens", dug)
    ds5 = s5_vjp((dkt_row, db_mat, dc_mat, dla, dlb))

    dqn, dkn, dv = _attn_bwd(qn, kn, vb, c_tot, dsb, batch=batch, seq=seq, bq=ATTN_BQ, bk=ATTN_BK)

    def qk_bwd(q, k, dq_, dk_, gq_, gk_, ones):
        _, vjp_q = jax.vjp(lambda a, g: _head_rms(a, g, ones), q, gq_)
        _, vjp_k = jax.vjp(lambda a, g: _head_rms(a, g, ones), k, gk_)
        dq, dgq = vjp_q(dq_)
        dk, dgk = vjp_k(dk_)
        return dq, dk, dgq, dgk

    dq, dk, dgq, dgk = _rowwise("qk_norm_bwd", qk_bwd, [(proj, sb_width, 0), (proj, sb_width, 1), dqn, dkn],
                                [gq, gk, ones_blocks], [(sb_width, BF16)] * 2, [(1, sb_width)] * 2)
    dproj = jnp.concatenate([dq, dk, dv.astype(BF16), du.astype(BF16)], axis=1)
    (dw_in,) = _mm("dw_in", xn, dproj, "tn")
    scatter_in, tok_w_in = _chip_exchange_start("scatter_in_start", "scatter", [(dw_in, 1)])
    (dxn,) = _mm("d_norm1_in", dproj, wf_in, "nt")
    dx, dg_norm1 = _rowwise("norm1_bwd", norm_bwd, [dh1, x2, dxn], [g1 + tok_w_in], [(d_model, F32)], [(1, d_model)])

    small_grads = [dg_norm1[0], dgq.reshape(heads, HEAD_DIM).sum(0) * qk_scale, dgk.reshape(heads, HEAD_DIM).sum(0), *ds5,
                   db_glu[0], dg_attn[0], dg_ssm[0], dg_norm2[0]]
    order = ["norm1_g", "q_norm_g", "k_norm_g", "ssm_lambda_re", "ssm_lambda_im", "ssm_log_dt", "ssm_b_re", "ssm_b_im",
             "ssm_c_re", "ssm_c_im", "ssm_d", "b_glu", "attn_out_g", "ssm_out_g", "norm2_g"]
    assert order == [name for name, *_ in small]

    def pack(parts, extra=None):
        flat = [p.reshape(-1) for p in parts] + ([extra.reshape(-1)] if extra is not None else [])
        flat = jnp.concatenate(flat)
        rows = -(-flat.shape[0] // (LANES * LANES)) * LANES
        return jnp.pad(flat, (0, rows * LANES - flat.shape[0])).reshape(rows, LANES)

    n_small = sum(w.size for _, w, _, _ in small)
    small_exchange, _ = _chip_exchange_start("small_grads_start", "all", [(pack(small_grads, loss_part), 0)])

    def adam_big(sa, sb_, w, m, v):
        g = sa + sb_
        delta, m, v = _adamw(g, w, m, v)
        return g, delta, m, v

    def reduce_and_update(tag, params, slots):
        mine = [_sum_slots("sum_" + name, s) for s, (name, *_rest) in zip(slots, params)]
        theirs = _swap_with_sibling("swap_" + tag, mine)
        return {name: _rowwise("adamw_" + name, adam_big, [sa, sb_, w, m, v], [], [(w.shape[1], F32)] * 4)
                for (name, w, m, v, _), sa, sb_ in zip(params, mine, theirs)}

    started = small_exchange[3][0]
    slots_mlp_in, slots_mlp_out = _chip_exchange_wait("scatter_mlp_wait", scatter_mlp, started)
    slots_glu, slots_out = _chip_exchange_wait("scatter_mix_wait", scatter_mix, started)
    big_out = reduce_and_update("rest", big[1:], [slots_glu, slots_out, slots_mlp_in, slots_mlp_out])

    (small_slots,) = _chip_exchange_wait("small_grads_wait", small_exchange, big_out["w_mlp_out"][3])
    reduced = _sum_slots("sum_small", small_slots)
    loss = reduced.reshape(-1)[n_small]
    flat, small_g, off = reduced.reshape(-1), {}, 0
    for name, w, _, _ in small:
        small_g[name] = flat[off:off + w.size].reshape(w.shape)
        off += w.size
    small_upd = _adamw_small([small_g[name] for name, *_ in small], [w for _, w, _, _ in small],
                             [m for _, _, m, _ in small], [v for _, _, _, v in small])
    small_out = [small_g] + [{name: small_upd[kind][i] for i, (name, *_) in enumerate(small)} for kind in range(3)]

    (slots_in,) = _chip_exchange_wait("scatter_in_wait", scatter_in, reduced)
    big_out.update(reduce_and_update("w_in", big[:1], [slots_in]))
    names = ["norm1_g", "w_in", "q_norm_g", "k_norm_g", "ssm_lambda_re", "ssm_lambda_im", "ssm_log_dt", "ssm_b_re",
             "ssm_b_im", "ssm_c_re", "ssm_c_im", "ssm_d", "w_glu", "b_glu", "attn_out_g", "ssm_out_g", "w_out",
             "norm2_g", "w_mlp_in", "w_mlp_out"]
    outs = [loss, dx.reshape(batch, seq, d_model)]
    for kind in range(4):
        for name in names:
            outs.append(big_out[name][kind] if name in big_out else small_out[kind][name])
    return tuple(outs)
```

```python
import functools
import math

import jax
import jax.numpy as jnp
from jax import lax
from jax.experimental import pallas as pl
from jax.experimental.pallas import tpu as pltpu

F32 = jnp.float32
BF16 = jnp.bfloat16
F32_DOT = lax.Precision.HIGH
MESH = pl.DeviceIdType.MESH

RMS_EPS = 1e-6
HEAD_DIM = 64
SSM_GROUP = 16
SSM_CHUNK = 16
LANES = 128
N_CHIPS = 4
N_DEV = 8
VMEM_LIMIT = 48 * 1024 * 1024

ADAM_LR = 0.001
ADAM_B1 = 0.9
ADAM_B2 = 0.999
ADAM_EPS = 1e-08
ADAM_WD = 0.01
ADAM_STEP = 10


def _tile(n, pref):
    t = min(n, pref)
    while n % t:
        t //= 2
    return t


def _params(*sem):
    return pltpu.CompilerParams(dimension_semantics=sem, vmem_limit_bytes=VMEM_LIMIT)


_DIMS = {"nn": (((1,), (0,)), ((), ())), "nt": (((1,), (1,)), ((), ())), "tn": (((0,), (0,)), ((), ()))}


MM_VMEM_BUDGET = 40 * 1024 * 1024


def _mm_tiles(m, n, k, a_bytes, b_bytes, tile_bytes):
    best = None
    for tk in [t for t in (k, k // 2, k // 4, k // 8) if t >= 256 or t == k]:
        for tm in [t for t in (1024, 512, 256, 128) if t <= m and m % t == 0]:
            for tn in [t for t in (1024, 512, 256, 128) if t <= n and n % t == 0]:
                need = 2 * (tm * tk * a_bytes + tk * tn * b_bytes) + 2 * tm * tn * tile_bytes + (tm * tn * 4 if tk < k else 0)
                if need > MM_VMEM_BUDGET:
                    continue
                traffic = m * k * a_bytes * (n // tn) + k * n * b_bytes * (m // tm)
                key = (tk < k, traffic, -tm * tn)
                if best is None or key < best[0]:
                    best = (key, (tm, tn, tk))
    return best[1]


def _mm(name, a, b, mode, *, a_fn=None, extras=(), epilogue=None, out_dtypes=(F32,), tile_sums=0):
    if mode == "nn":
        (m, k), n = a.shape, b.shape[1]
    elif mode == "nt":
        (m, k), n = a.shape, b.shape[0]
    else:
        (k, m), n = a.shape, b.shape[1]
    tile_bytes = sum(e.dtype.itemsize for e, kind in extras if kind == "tile") + sum(jnp.dtype(d).itemsize for d in out_dtypes)
    tm, tn, tk = _mm_tiles(m, n, k, a.dtype.itemsize, b.dtype.itemsize, tile_bytes)
    nk = k // tk
    ne, nout = len(extras), len(out_dtypes)
    dims = _DIMS[mode]

    def body(a_ref, b_ref, *rest):
        ex, outs, sums = rest[:ne], rest[ne:ne + nout], rest[ne + nout:ne + nout + tile_sums]
        at = a_ref[...]
        if a_fn is not None:
            at = a_fn(at)
        p = lax.dot_general(at.astype(BF16), b_ref[...].astype(BF16), dims, preferred_element_type=F32)

        def finish(r):
            if epilogue is not None:
                r = epilogue(r, *[e[...] for e in ex])
            if not isinstance(r, (tuple, list)):
                r = (r,)
            for o, v in zip(outs, r[:nout]):
                o[...] = v.astype(o.dtype)
            first = ((lax.broadcasted_iota(jnp.int32, (8, LANES), 0) == 0)
                     & (lax.broadcasted_iota(jnp.int32, (8, LANES), 1) == 0))
            for o, v in zip(sums, r[nout:]):
                o[...] = jnp.where(first, v, 0.0)

        if nk == 1:
            finish(p)
        else:
            acc = rest[ne + nout + tile_sums]
            kk = pl.program_id(2)

            @pl.when(kk == 0)
            def _():
                acc[...] = p

            @pl.when(kk > 0)
            def _():
                acc[...] += p

            @pl.when(kk == nk - 1)
            def _():
                finish(acc[...])

    if mode == "tn":
        a_spec = pl.BlockSpec((tk, tm), lambda i, j, kk: (kk, i))
    else:
        a_spec = pl.BlockSpec((tm, tk), lambda i, j, kk: (i, kk))
    if mode == "nt":
        b_spec = pl.BlockSpec((tn, tk), lambda i, j, kk: (j, kk))
    else:
        b_spec = pl.BlockSpec((tk, tn), lambda i, j, kk: (kk, j))
    ex_specs = []
    for _, kind in extras:
        if kind == "tile":
            ex_specs.append(pl.BlockSpec((tm, tn), lambda i, j, kk: (i, j)))
        else:
            ex_specs.append(pl.BlockSpec((1, tn), lambda i, j, kk: (0, j)))
    return pl.pallas_call(
        body, name=name, grid=(m // tm, n // tn, nk),
        in_specs=[a_spec, b_spec] + ex_specs,
        out_specs=([pl.BlockSpec((tm, tn), lambda i, j, kk: (i, j)) for _ in out_dtypes]
                   + [pl.BlockSpec((8, LANES), lambda i, j, kk: (i, j))] * tile_sums),
        out_shape=([jax.ShapeDtypeStruct((m, n), dt) for dt in out_dtypes]
                   + [jax.ShapeDtypeStruct((m // tm * 8, n // tn * LANES), F32)] * tile_sums),
        scratch_shapes=[pltpu.VMEM((tm, tn), F32)] if nk > 1 else [],
        compiler_params=_params("parallel", "parallel", "arbitrary"),
    )(a, b, *[e for e, _ in extras])


def _rowwise(name, fn, rows, consts, row_outs, acc_outs=(), tm=256):
    norm = [r if isinstance(r, tuple) else (r, r.shape[1], 0) for r in rows]
    t = norm[0][0].shape[0]
    tm = _tile(t, tm)
    nr, nc, no = len(norm), len(consts), len(row_outs)

    def body(*refs):
        outs = fn(*[r[...] for r in refs[:nr + nc]])
        if not isinstance(outs, (tuple, list)):
            outs = (outs,)
        o_refs, a_refs = refs[nr + nc:nr + nc + no], refs[nr + nc + no:]
        for r, v in zip(o_refs, outs[:no]):
            r[...] = v.astype(r.dtype)
        if a_refs:
            i = pl.program_id(0)

            @pl.when(i == 0)
            def _():
                for r, v in zip(a_refs, outs[no:]):
                    r[...] = v

            @pl.when(i > 0)
            def _():
                for r, v in zip(a_refs, outs[no:]):
                    r[...] += v

    in_specs = [pl.BlockSpec((tm, w), functools.partial(lambda i, cb: (i, cb), cb=cb)) for _, w, cb in norm]
    in_specs += [pl.BlockSpec(c.shape, functools.partial(lambda i, nd: (0,) * nd, nd=c.ndim)) for c in consts]
    out_specs = [pl.BlockSpec((tm, w), lambda i: (i, 0)) for w, _ in row_outs]
    out_specs += [pl.BlockSpec(s, functools.partial(lambda i, nd: (0,) * nd, nd=len(s))) for s in acc_outs]
    out_shape = [jax.ShapeDtypeStruct((t, w), dt) for w, dt in row_outs]
    out_shape += [jax.ShapeDtypeStruct(s, F32) for s in acc_outs]
    return pl.pallas_call(
        body, name=name, grid=(t // tm,), in_specs=in_specs, out_specs=out_specs, out_shape=out_shape,
        compiler_params=_params("arbitrary"),
    )(*[r[0] for r in norm], *consts)


def _rms(x, g):
    return x * lax.rsqrt(jnp.mean(x * x, axis=-1, keepdims=True) + RMS_EPS) * g


@jax.custom_vjp
def _head_sums(x, ones_blocks):
    parts = [jnp.dot(x[:, j:j + LANES], ones_blocks, precision=F32_DOT, preferred_element_type=F32)
             for j in range(0, x.shape[1], LANES)]
    return jnp.concatenate(parts, axis=1)


_head_sums.defvjp(lambda x, ones_blocks: (_head_sums(x, ones_blocks), ones_blocks),
                  lambda ones_blocks, ct: (_head_sums(ct, ones_blocks), None))


def _head_rms(x, g, ones_blocks):
    return x * lax.rsqrt(_head_sums(x * x, ones_blocks) * (1.0 / HEAD_DIM) + RMS_EPS) * g


def _gelu(x):
    return x * (0.5 * (1.0 + jnp.tanh(math.sqrt(2.0 / math.pi) * (x + 0.044715 * (x * x * x)))))


def _mixed(sb, y_ssm, gate_pre, g_attn, g_ssm):
    ssm = _gelu(y_ssm) * jax.nn.sigmoid(gate_pre)
    return jnp.concatenate([_rms(sb, g_attn), _rms(ssm, g_ssm)], axis=-1)


def _softplus(z):
    return jnp.maximum(z, 0.0) + jnp.log(1.0 + jnp.exp(-jnp.abs(z)))


def _running_sums(x, tri):
    return jnp.dot(x.astype(BF16), tri, preferred_element_type=F32)


def _dot_nt(a, b, **kw):
    return lax.dot_general(a, b, _DIMS["nt"], preferred_element_type=F32, **kw)


def _dot_tn(a, b, **kw):
    return lax.dot_general(a, b, _DIMS["tn"], preferred_element_type=F32, **kw)


ATTN_BQ, ATTN_BK = 2048, 256
HEAD_LANES = tuple(slice(h * HEAD_DIM, (h + 1) * HEAD_DIM) for h in range(LANES // HEAD_DIM))


def _attn_fwd(qs, kn, v, *, batch, seq, bq, bk):
    width = qs.shape[1]
    bq = _tile(seq, bq)
    bk = _tile(bq, bk)
    nq, kpq = seq // bq, bq // bk

    def body(q_ref, k_ref, v_ref, o_ref, c_ref):
        row = lax.broadcasted_iota(jnp.int32, (bq, bk), 0)
        col = lax.broadcasted_iota(jnp.int32, (bq, bk), 1)
        tri = (lax.broadcasted_iota(jnp.int32, (bk, bk), 0) >= lax.broadcasted_iota(jnp.int32, (bk, bk), 1)).astype(BF16)

        def q_block(qi, carry):
            r0 = pl.multiple_of(qi * bq, bq)
            qh = [q_ref[pl.ds(r0, bq), ln] for ln in HEAD_LANES]

            def tile(k0, state, top=0):
                diag = top is not None
                top = top or 0
                msk = (col < row)[:bq - top] if diag else None
                new = []
                for h, ln in enumerate(HEAD_LANES):
                    o, c = state[2 * h], state[2 * h + 1]
                    z = _dot_nt(qh[h][top:], k_ref[pl.ds(k0, bk), ln])
                    sp = _softplus(z)
                    if diag:
                        sp = jnp.where(msk, sp, 0.0)
                    r = _running_sums(sp, tri)
                    a = jnp.exp(z - r - c[top:])
                    if diag:
                        a = jnp.where(msk, a, 0.0)
                    o_new = o[top:] + jnp.dot(a.astype(BF16), v_ref[pl.ds(k0, bk), ln], preferred_element_type=F32)
                    c_new = c[top:] + r[:, 0:1]
                    if top:
                        o_new, c_new = jnp.concatenate([o[:top], o_new]), jnp.concatenate([c[:top], c_new])
                    new += [o_new, c_new]
                return tuple(new)

            state = (jnp.zeros((bq, HEAD_DIM), F32), jnp.zeros((bq, 1), F32)) * len(HEAD_LANES)
            for d in reversed(range(kpq)):
                state = tile(pl.multiple_of(r0 + d * bk, bk), state, top=d * bk)
            state = lax.fori_loop(0, qi * kpq, lambda it, st: tile(pl.multiple_of(r0 - (it + 1) * bk, bk), st, None),
                                  state)
            for h, ln in enumerate(HEAD_LANES):
                o_ref[pl.ds(r0, bq), ln] = state[2 * h]
                c_ref[pl.ds(r0, bq), ln] = jnp.broadcast_to(state[2 * h + 1], (bq, HEAD_DIM))
            return carry

        lax.fori_loop(0, nq, q_block, 0)

    spec = pl.BlockSpec((seq, LANES), lambda b, h: (b, h))
    shape = jax.ShapeDtypeStruct((batch * seq, width), F32)
    return pl.pallas_call(
        body, name="attn_fwd", grid=(batch, width // LANES), in_specs=[spec, spec, spec], out_specs=[spec, spec],
        out_shape=[shape, shape], compiler_params=_params("parallel", "parallel"),
    )(qs, kn, v)


def _attn_bwd(qs, kn, v, c_tot, do, *, batch, seq, bq, bk):
    width = qs.shape[1]
    bq = _tile(seq, bq)
    bk = _tile(bq, bk)
    nq, kpq = seq // bq, bq // bk

    def body(q_ref, k_ref, v_ref, c_ref, do_ref, dq_ref, dk_ref, dv_ref):
        row = lax.broadcasted_iota(jnp.int32, (bq, bk), 0)
        col = lax.broadcasted_iota(jnp.int32, (bq, bk), 1)
        sq_row = lax.broadcasted_iota(jnp.int32, (bk, bk), 0)
        sq_col = lax.broadcasted_iota(jnp.int32, (bk, bk), 1)
        tri = (sq_row >= sq_col).astype(BF16)
        tri_t = (sq_row <= sq_col).astype(BF16)
        dk_ref[...] = jnp.zeros_like(dk_ref)
        dv_ref[...] = jnp.zeros_like(dv_ref)

        def q_block(qi, carry):
            r0 = pl.multiple_of(qi * bq, bq)
            qh = [q_ref[pl.ds(r0, bq), ln] for ln in HEAD_LANES]
            d_out = [do_ref[pl.ds(r0, bq), ln].astype(BF16) for ln in HEAD_LANES]
            c_all = [c_ref[pl.ds(r0, bq), ln][:, 0:1] for ln in HEAD_LANES]

            def tile(k0, state, top=0):
                diag = top is not None
                top = top or 0
                last = diag and top == bq - bk
                msk = (col < row)[:bq - top] if diag else None
                new = []
                for h, ln in enumerate(HEAD_LANES):
                    c_left, g_left, dq = state[3 * h:3 * h + 3]
                    q, d_o = qh[h][top:], d_out[h][top:]
                    k = k_ref[pl.ds(k0, bk), ln]
                    z = _dot_nt(q, k)
                    e = jnp.exp(-jnp.abs(z))
                    sp = jnp.maximum(z, 0.0) + jnp.log(1.0 + e)
                    sig = jnp.exp(z - sp)
                    if diag:
                        sp = jnp.where(msk, sp, 0.0)
                    r = _running_sums(sp, tri)
                    c_new = c_left[top:] + r[:, 0:1]
                    a = jnp.exp(z - r - (0.0 if last else c_all[h][top:] - c_new))
                    if diag:
                        a = jnp.where(msk, a, 0.0)
                    g = a * _dot_nt(d_o, v_ref[pl.ds(k0, bk), ln])
                    pg = _running_sums(g, tri_t)
                    dz = g - sig * (g_left[top:] + pg)
                    if diag:
                        dz = jnp.where(msk, dz, 0.0)
                    dz = dz.astype(BF16)
                    dk_ref[pl.ds(k0, bk), ln] += _dot_tn(dz, q)
                    dv_ref[pl.ds(k0, bk), ln] += _dot_tn(a.astype(BF16), d_o)
                    g_new = g_left[top:] + pg[:, bk - 1:bk]
                    dq_new = dq[top:] + jnp.dot(dz, k, preferred_element_type=F32)
                    if top:
                        c_new = jnp.concatenate([c_left[:top], c_new])
                        g_new = jnp.concatenate([g_left[:top], g_new])
                        dq_new = jnp.concatenate([dq[:top], dq_new])
                    new += [c_new, g_new, dq_new]
                return tuple(new)

            zero = jnp.zeros((bq, 1), F32)
            init = (zero, zero, jnp.zeros((bq, HEAD_DIM), F32)) * len(HEAD_LANES)
            state = lax.fori_loop(0, qi * kpq, lambda it, st: tile(pl.multiple_of(it * bk, bk), st, None), init)
            for d in range(kpq):
                state = tile(pl.multiple_of(r0 + d * bk, bk), state, top=d * bk)
            for h, ln in enumerate(HEAD_LANES):
                dq_ref[pl.ds(r0, bq), ln] = state[3 * h + 2]
            return carry

        lax.fori_loop(0, nq, q_block, 0)

    spec = pl.BlockSpec((seq, LANES), lambda b, h: (b, h))
    shape = jax.ShapeDtypeStruct((batch * seq, width), F32)
    return pl.pallas_call(
        body, name="attn_bwd", grid=(batch, width // LANES), in_specs=[spec] * 5, out_specs=[spec] * 3,
        out_shape=[shape] * 3, compiler_params=_params("parallel", "parallel"),
    )(qs, kn, v, c_tot, do)


def _s5_operators(lam_re, lam_im, log_dt, b_re, b_im, c_re, c_im, d_skip):
    groups, n_state, n_ch = b_re.shape
    cs = SSM_CHUNK
    dt = jnp.exp(log_dt)[:, None]
    steps = jnp.arange(cs + 1, dtype=F32)[None, :, None]
    mag = jnp.exp(steps * (lam_re * dt)[:, None, :])
    ang = steps * (lam_im * dt)[:, None, :]
    pw_re, pw_im = mag * jnp.cos(ang), mag * jnp.sin(ang)
    num_re, num_im = pw_re[:, 1] - 1.0, pw_im[:, 1]
    den = lam_re * lam_re + lam_im * lam_im
    cf_re = (num_re * lam_re + num_im * lam_im) / den
    cf_im = (num_im * lam_re - num_re * lam_im) / den
    bb_re = cf_re[:, :, None] * b_re - cf_im[:, :, None] * b_im
    bb_im = cf_re[:, :, None] * b_im + cf_im[:, :, None] * b_re
    width = cs * n_ch
    ct_re, ct_im = c_re.transpose(0, 2, 1), c_im.transpose(0, 2, 1)

    def c_times_powers(first):
        pr = pw_re[:, first:first + cs].transpose(0, 2, 1)[:, :, :, None]
        pi = pw_im[:, first:first + cs].transpose(0, 2, 1)[:, :, :, None]
        re = pr * ct_re[:, :, None, :] - pi * ct_im[:, :, None, :]
        im = pr * ct_im[:, :, None, :] + pi * ct_re[:, :, None, :]
        return re.reshape(groups, n_state, width), im.reshape(groups, n_state, width)

    w_re, w_im = c_times_powers(0)
    kt_row = (jnp.einsum("gpi,gpw->giw", bb_re, w_re, precision=F32_DOT)
              - jnp.einsum("gpi,gpw->giw", bb_im, w_im, precision=F32_DOT))
    kt_row = kt_row + jnp.pad(jnp.eye(n_ch, dtype=F32)[None] * d_skip[:, None, :], ((0, 0), (0, 0), (0, width - n_ch)))
    rp_re, rp_im = pw_re[:, cs - 1::-1][:, :cs], pw_im[:, cs - 1::-1][:, :cs]
    bm_re = rp_re[:, :, None, :] * bb_re.transpose(0, 2, 1)[:, None] - rp_im[:, :, None, :] * bb_im.transpose(0, 2, 1)[:, None]
    bm_im = rp_re[:, :, None, :] * bb_im.transpose(0, 2, 1)[:, None] + rp_im[:, :, None, :] * bb_re.transpose(0, 2, 1)[:, None]
    b_mat = jnp.concatenate([bm_re, bm_im], axis=-1).reshape(groups, width, 2 * n_state)
    w1_re, w1_im = c_times_powers(1)
    c_mat = jnp.concatenate([w1_re, -w1_im], axis=1)
    la = jnp.concatenate([pw_re[:, cs], pw_re[:, cs]], axis=-1)[:, None, :]
    lb = jnp.concatenate([-pw_im[:, cs], pw_im[:, cs]], axis=-1)[:, None, :]
    return kt_row, b_mat, c_mat, la, lb


GROUPS_PER_BLOCK = LANES // SSM_GROUP


def _tokens_to_groups(name, u, col_block, width):
    t = u.shape[0]
    n = t // SSM_CHUNK
    ch = SSM_CHUNK * SSM_GROUP
    blocks = width // LANES

    def body(u_ref, o_ref):
        for s in range(SSM_CHUNK):
            rows = u_ref[pl.ds(s, n, stride=SSM_CHUNK), :]
            for g in range(GROUPS_PER_BLOCK):
                o_ref[g, :, s * SSM_GROUP:(s + 1) * SSM_GROUP] = rows[:, g * SSM_GROUP:(g + 1) * SSM_GROUP]

    return pl.pallas_call(
        body, name=name, grid=(blocks,),
        in_specs=[pl.BlockSpec((t, LANES), lambda j: (0, col_block * blocks + j))],
        out_specs=pl.BlockSpec((GROUPS_PER_BLOCK, n, ch), lambda j: (j, 0, 0)),
        out_shape=jax.ShapeDtypeStruct((width // SSM_GROUP, n, ch), F32), compiler_params=_params("parallel"),
    )(u)


def _groups_to_tokens(name, ug):
    groups, n, ch = ug.shape

    def body(g_ref, o_ref, rows_ref):
        for s in range(SSM_CHUNK):
            for g in range(GROUPS_PER_BLOCK):
                rows_ref[s % 2, :, g * SSM_GROUP:(g + 1) * SSM_GROUP] = g_ref[g, :, s * SSM_GROUP:(s + 1) * SSM_GROUP]
            o_ref[pl.ds(s, n, stride=SSM_CHUNK), :] = rows_ref[s % 2]

    return pl.pallas_call(
        body, name=name, grid=(groups // GROUPS_PER_BLOCK,),
        in_specs=[pl.BlockSpec((GROUPS_PER_BLOCK, n, ch), lambda j: (j, 0, 0))],
        out_specs=pl.BlockSpec((n * SSM_CHUNK, LANES), lambda j: (0, j)),
        out_shape=jax.ShapeDtypeStruct((n * SSM_CHUNK, groups * SSM_GROUP), F32),
        scratch_shapes=[pltpu.VMEM((2, n, LANES), F32)], compiler_params=_params("parallel"),
    )(ug)


SCAN_ROWS = 8


def _toeplitz_to(tm_ref, g, kt_row):
    width = kt_row.shape[1]
    tm_ref[g] = jnp.zeros((width, width), F32)
    for s in range(SSM_CHUNK):
        tm_ref[g, s * SSM_GROUP:(s + 1) * SSM_GROUP, s * SSM_GROUP:] = kt_row[:, :width - s * SSM_GROUP]


def _lam_powers(la, lb, reverse):
    if reverse:
        lb = -lb

    def mul(p, q):
        return p[0] * q[0] - p[1] * q[1], p[0] * q[1] + p[1] * q[0]

    p1 = (la, lb)
    p2 = mul(p1, p1)
    p3 = mul(p2, p1)
    p4 = mul(p2, p2)
    rows = [p1, p2, p3, p4, mul(p4, p1), mul(p4, p2), mul(p4, p3), mul(p4, p4)]
    if reverse:
        rows = rows[::-1]
    idx = lax.broadcasted_iota(jnp.int32, (SCAN_ROWS, la.shape[1]), 0)
    tab_a = sum(jnp.where(idx == j, r[0], 0.0) for j, r in enumerate(rows))
    tab_b = sum(jnp.where(idx == j, r[1], 0.0) for j, r in enumerate(rows))
    return (p1, p2, p4), (tab_a, tab_b), idx


def _scan_block(e, carry, steps, table, idx, half, reverse):
    n = SCAN_ROWS
    for d, (pa, pb) in zip((1, 2, 4), steps):
        sh = pltpu.roll(e, n - d if reverse else d, 0)
        sh = jnp.where(idx < n - d if reverse else idx >= d, sh, 0.0)
        e = e + pa * sh + pb * pltpu.roll(sh, half, 1)
    tab_a, tab_b = table
    e = e + tab_a * carry + tab_b * pltpu.roll(carry, half, 1)
    shifted = jnp.where(idx == (n - 1 if reverse else 0), carry, pltpu.roll(e, n - 1 if reverse else 1, 0))
    edge = e[0:1] if reverse else e[n - 1:n]
    return shifted, jnp.broadcast_to(edge, e.shape)


def _s5_fwd(ug, kt_row, b_mat, c_mat, la, lb, *, batch, gb=8):
    groups, n, ch = ug.shape
    p2 = b_mat.shape[2]
    gb = _tile(groups, gb)
    nch = n // batch
    nblk = nch // SCAN_ROWS

    def body(u_ref, k_ref, b_ref, c_ref, la_ref, lb_ref, y_ref, x_ref, s_ref, tm_ref):
        for g in range(gb):
            _toeplitz_to(tm_ref, g, k_ref[g])
            s_ref[g] = jnp.dot(u_ref[g], b_ref[g], precision=F32_DOT, preferred_element_type=F32)
        powers = [_lam_powers(la_ref[g], lb_ref[g], False) for g in range(gb)]

        def step(blk, carries):
            new = []
            for g in range(gb):
                steps, table, idx = powers[g]
                for b in range(batch):
                    rows = pl.ds(pl.multiple_of(b * nch + blk * SCAN_ROWS, SCAN_ROWS), SCAN_ROWS)
                    x_in, carry = _scan_block(s_ref[g, rows, :], carries[g * batch + b], steps, table, idx, p2 // 2, False)
                    x_ref[g, rows, :] = x_in
                    new.append(carry)
            return tuple(new)

        lax.fori_loop(0, nblk, step, tuple(jnp.zeros((SCAN_ROWS, p2), F32) for _ in range(gb * batch)))
        for g in range(gb):
            y_ref[g] = (jnp.dot(u_ref[g], tm_ref[g], precision=F32_DOT, preferred_element_type=F32)
                        + jnp.dot(x_ref[g], c_ref[g], precision=F32_DOT, preferred_element_type=F32))

    def spec(a, b):
        return pl.BlockSpec((gb, a, b), lambda i: (i, 0, 0))

    return pl.pallas_call(
        body, name="s5_fwd", grid=(groups // gb,),
        in_specs=[spec(n, ch), spec(SSM_GROUP, ch), spec(ch, p2), spec(p2, ch), spec(1, p2), spec(1, p2)],
        out_specs=[spec(n, ch), spec(n, p2)],
        out_shape=[jax.ShapeDtypeStruct((groups, n, ch), F32), jax.ShapeDtypeStruct((groups, n, p2), F32)],
        scratch_shapes=[pltpu.VMEM((gb, n, p2), F32), pltpu.VMEM((gb, ch, ch), F32)],
        compiler_params=_params("parallel"),
    )(ug, kt_row, b_mat, c_mat, la, lb)


def _s5_bwd(ug, dyg, xin, kt_row, b_mat, c_mat, la, lb, *, batch, gb=8):
    groups, n, ch = ug.shape
    p2 = b_mat.shape[2]
    gb = _tile(groups, gb)
    nch = n // batch
    nblk = nch // SCAN_ROWS

    def body(u_ref, dy_ref, x_ref, k_ref, b_ref, c_ref, la_ref, lb_ref,
             du_ref, dk_ref, db_ref, dc_ref, dla_ref, dlb_ref, dx_ref, ds_ref, tm_ref):
        for g in range(gb):
            _toeplitz_to(tm_ref, g, k_ref[g])
            dx_ref[g] = _dot_nt(dy_ref[g], c_ref[g], precision=F32_DOT)
        powers = [_lam_powers(la_ref[g], lb_ref[g], True) for g in range(gb)]

        def step(it, carries):
            new = []
            for g in range(gb):
                steps, table, idx = powers[g]
                for b in range(batch):
                    rows = pl.ds(pl.multiple_of(b * nch + (nblk - 1 - it) * SCAN_ROWS, SCAN_ROWS), SCAN_ROWS)
                    d_s, carry = _scan_block(dx_ref[g, rows, :], carries[g * batch + b], steps, table, idx, p2 // 2, True)
                    ds_ref[g, rows, :] = d_s
                    new.append(carry)
            return tuple(new)

        lax.fori_loop(0, nblk, step, tuple(jnp.zeros((SCAN_ROWS, p2), F32) for _ in range(gb * batch)))
        for g in range(gb):
            u, dy, ds, x = u_ref[g], dy_ref[g], ds_ref[g], x_ref[g]
            du_ref[g] = _dot_nt(dy, tm_ref[g], precision=F32_DOT) + _dot_nt(ds, b_ref[g], precision=F32_DOT)
            tm_ref[g] = _dot_tn(u, dy, precision=F32_DOT)
            dk_ref[g] = tm_ref[g, 0:SSM_GROUP, :]
            for s in range(1, SSM_CHUNK):
                dk_ref[g, :, :ch - s * SSM_GROUP] += tm_ref[g, s * SSM_GROUP:(s + 1) * SSM_GROUP, s * SSM_GROUP:]
            db_ref[g] = _dot_tn(u, ds, precision=F32_DOT)
            dc_ref[g] = _dot_tn(x, dy, precision=F32_DOT)
            dla_ref[g] = jnp.sum(ds * x, axis=0, keepdims=True)
            dlb_ref[g] = jnp.sum(ds * pltpu.roll(x, p2 // 2, 1), axis=0, keepdims=True)

    def spec(a, b):
        return pl.BlockSpec((gb, a, b), lambda i: (i, 0, 0))

    def shape(a, b):
        return jax.ShapeDtypeStruct((groups, a, b), F32)

    return pl.pallas_call(
        body, name="s5_bwd", grid=(groups // gb,),
        in_specs=[spec(n, ch), spec(n, ch), spec(n, p2), spec(SSM_GROUP, ch), spec(ch, p2), spec(p2, ch), spec(1, p2),
                  spec(1, p2)],
        out_specs=[spec(n, ch), spec(SSM_GROUP, ch), spec(ch, p2), spec(p2, ch), spec(1, p2), spec(1, p2)],
        out_shape=[shape(n, ch), shape(SSM_GROUP, ch), shape(ch, p2), shape(p2, ch), shape(1, p2), shape(1, p2)],
        scratch_shapes=[pltpu.VMEM((gb, n, p2), F32), pltpu.VMEM((gb, n, p2), F32), pltpu.VMEM((gb, ch, ch), F32)],
        compiler_params=_params("parallel"),
    )(ug, dyg, xin, kt_row, b_mat, c_mat, la, lb)


def _block(ref, axis, j, size):
    start = j * size if isinstance(j, int) else pl.multiple_of(j * size, size)
    return ref.at[pl.ds(start, size), :] if axis == 0 else ref.at[:, pl.ds(start, size)]


def _chip_exchange_copies(mode, axes, srcs, lands, send_sems, recv_sems, local_sems):
    x, y, c = lax.axis_index("x"), lax.axis_index("y"), lax.axis_index("c")
    everyone = mode == "all"
    me = 4 * x + 2 * y + c if everyone else 2 * x + y
    n_peers = _exchange_peers(mode)
    local, sends, arrivals = [], [], []
    for w, axis in enumerate(axes):
        if mode == "gather":
            size = srcs[w].shape[axis]
            local.append(pltpu.make_async_copy(srcs[w], _block(lands[w], axis, me, size), local_sems.at[w]))
        elif mode == "scatter":
            size = srcs[w].shape[axis] // N_CHIPS
            local.append(pltpu.make_async_copy(_block(srcs[w], axis, me, size), lands[w].at[me], local_sems.at[w]))
        else:
            local.append(pltpu.make_async_copy(srcs[w], lands[w].at[me], local_sems.at[w]))
        for k in range(1, n_peers + 1):
            bits = k if everyone else 2 * k
            px = 1 - x if bits & 4 else x
            py = 1 - y if bits & 2 else y
            pc = 1 - c if bits & 1 else c
            peer = 4 * px + 2 * py + pc if everyone else 2 * px + py
            if mode == "gather":
                src, dst, arrive = srcs[w], _block(lands[w], axis, me, size), _block(lands[w], axis, peer, size)
            elif mode == "scatter":
                src, dst, arrive = _block(srcs[w], axis, peer, size), lands[w].at[me], lands[w].at[peer]
            else:
                src, dst, arrive = srcs[w], lands[w].at[me], lands[w].at[peer]
            sem = w * n_peers + k - 1
            for target, out in ((dst, sends), (arrive, arrivals)):
                out.append(pltpu.make_async_remote_copy(
                    src_ref=src, dst_ref=target, send_sem=send_sems.at[sem], recv_sem=recv_sems.at[sem],
                    device_id=(px, py, pc), device_id_type=MESH))
    return local, sends, arrivals


def _exchange_peers(mode):
    return N_DEV - 1 if mode == "all" else N_CHIPS - 1


def _chip_exchange_start(name, mode, items, after=None):
    n = len(items)
    n_after = 0 if after is None else 1
    axes = [axis for _, axis in items]
    hbm = pl.BlockSpec(memory_space=pltpu.HBM)
    sem = pl.BlockSpec(memory_space=pltpu.SEMAPHORE)
    lands = []
    for a, axis in items:
        shape = list(a.shape)
        if mode == "gather":
            shape[axis] *= N_CHIPS
        elif mode == "scatter":
            shape[axis] //= N_CHIPS
            shape = [N_CHIPS] + shape
        else:
            shape = [N_DEV] + shape
        lands.append(pltpu.with_memory_space_constraint(lax.empty(tuple(shape), a.dtype), pltpu.HBM))

    def body(*refs):
        srcs, land_refs = refs[:n], refs[n:2 * n]
        send_sems, recv_sems, local_sems = refs[2 * n + n_after:2 * n + n_after + 3]
        token = refs[-1]
        local, sends, _ = _chip_exchange_copies(mode, axes, srcs, land_refs, send_sems, recv_sems, local_sems)
        for cp in local + sends:
            cp.start()
        token[...] = jnp.zeros_like(token)

    n_sem = n * _exchange_peers(mode)
    outs = pl.pallas_call(
        body, name=name,
        out_shape=(pltpu.SemaphoreType.DMA((n_sem,)), pltpu.SemaphoreType.DMA((n_sem,)), pltpu.SemaphoreType.DMA((n,)),
                   *[pltpu.HBM(a.shape, a.dtype) for a, _ in items], *[pltpu.HBM(l.shape, l.dtype) for l in lands],
                   jax.ShapeDtypeStruct((8, LANES), F32)),
        in_specs=[hbm] * (2 * n) + [pl.BlockSpec(memory_space=pl.ANY)] * n_after,
        out_specs=(sem, sem, sem, *[hbm] * (2 * n), pl.BlockSpec(memory_space=pltpu.VMEM)),
        input_output_aliases={i: 3 + i for i in range(2 * n)},
        compiler_params=pltpu.CompilerParams(has_side_effects=pltpu.SideEffectType.DATAFLOW_SIDE_EFFECTING),
    )(*[pltpu.with_memory_space_constraint(a, pltpu.HBM) for a, _ in items], *lands, *([after] if n_after else []))
    return (mode, axes, outs[:3], outs[3:3 + n], outs[3 + n:3 + 2 * n]), outs[-1][0:1, 0:1]


def _chip_exchange_wait(name, handle, after):
    mode, axes, sems, srcs, lands = handle
    n = len(axes)
    after = list(after) if isinstance(after, (tuple, list)) else [after]
    hbm = pl.BlockSpec(memory_space=pltpu.HBM)
    sem = pl.BlockSpec(memory_space=pltpu.SEMAPHORE)

    def body(*refs):
        src_refs, land_refs = refs[:n], refs[n:2 * n]
        send_sems, recv_sems, local_sems = refs[2 * n:2 * n + 3]
        local, sends, arrivals = _chip_exchange_copies(mode, axes, src_refs, land_refs, send_sems, recv_sems, local_sems)
        for cp in sends:
            cp.wait_send()
        for cp in arrivals:
            cp.wait_recv()
        for cp in local:
            cp.wait()

    outs = pl.pallas_call(
        body, name=name,
        out_shape=(*[pltpu.HBM(a.shape, a.dtype) for a in srcs], *[pltpu.HBM(l.shape, l.dtype) for l in lands]),
        in_specs=[hbm] * (2 * n) + [sem] * 3 + [pl.BlockSpec(memory_space=pl.ANY)] * len(after), out_specs=[hbm] * (2 * n),
        input_output_aliases={i: i for i in range(2 * n)},
        compiler_params=pltpu.CompilerParams(has_side_effects=pltpu.SideEffectType.DATAFLOW_SIDE_EFFECTING),
    )(*srcs, *lands, *sems, *after)
    return outs[n:]


def _sum_slots(name, slots, tm=256):
    n_slots, r, c = slots.shape
    tm = _tile(r, tm)

    def body(*refs):
        acc = refs[0][...]
        for s_ref in refs[1:n_slots]:
            acc = acc + s_ref[...]
        refs[n_slots][...] = acc

    specs = [pl.BlockSpec((None, tm, c), functools.partial(lambda i, s: (s, i, 0), s=s)) for s in range(n_slots)]
    return pl.pallas_call(
        body, name=name, grid=(r // tm,), in_specs=specs, out_specs=pl.BlockSpec((tm, c), lambda i: (i, 0)),
        out_shape=jax.ShapeDtypeStruct((r, c), F32), compiler_params=_params("parallel"),
    )(*[slots] * n_slots)


def _swap_with_sibling(name, arrays):
    n = len(arrays)
    hbm = pl.BlockSpec(memory_space=pl.ANY)

    def body(*refs):
        ins, outs = refs[:n], refs[n:2 * n]
        send_sems, recv_sems = refs[2 * n:]
        sibling = (lax.axis_index("x"), lax.axis_index("y"), 1 - lax.axis_index("c"))
        copies = [pltpu.make_async_remote_copy(src_ref=ins[w], dst_ref=outs[w], send_sem=send_sems.at[w],
                                               recv_sem=recv_sems.at[w], device_id=sibling, device_id_type=MESH)
                  for w in range(n)]
        for cp in copies:
            cp.start()
        for cp in copies:
            cp.wait()

    return pl.pallas_call(
        body, name=name, in_specs=[hbm] * n, out_specs=[hbm] * n,
        out_shape=[jax.ShapeDtypeStruct(a.shape, a.dtype) for a in arrays],
        scratch_shapes=[pltpu.SemaphoreType.DMA((n,)), pltpu.SemaphoreType.DMA((n,))],
    )(*arrays)


def _adamw(g, w, m, v):
    m = ADAM_B1 * m + (1.0 - ADAM_B1) * g
    v = ADAM_B2 * v + (1.0 - ADAM_B2) * jnp.square(g)
    m_hat = m / (1.0 - ADAM_B1 ** ADAM_STEP)
    v_hat = v / (1.0 - ADAM_B2 ** ADAM_STEP)
    delta = -ADAM_LR * (m_hat / (jnp.sqrt(v_hat) + ADAM_EPS) + ADAM_WD * w)
    return delta, m, v


def _adamw_small(grads, ws, ms, vs):
    n = len(ws)
    vmem = pl.BlockSpec(memory_space=pltpu.VMEM)

    def body(*refs):
        for i in range(n):
            g, w, m, v = (refs[k * n + i][...] for k in range(4))
            for k, val in enumerate(_adamw(g, w, m, v)):
                refs[(4 + k) * n + i][...] = val

    outs = pl.pallas_call(
        body, name="adamw_small", in_specs=[vmem] * (4 * n), out_specs=[vmem] * (3 * n),
        out_shape=[jax.ShapeDtypeStruct(w.shape, F32) for _ in range(3) for w in ws],
        compiler_params=pltpu.CompilerParams(vmem_limit_bytes=VMEM_LIMIT),
    )(*grads, *ws, *ms, *vs)
    return outs[:n], outs[n:2 * n], outs[2 * n:]


def kernel(x, norm1_g, w_in, q_norm_g, k_norm_g, ssm_lambda_re, ssm_lambda_im, ssm_log_dt, ssm_b_re, ssm_b_im, ssm_c_re, ssm_c_im, ssm_d, w_glu, b_glu, attn_out_g, ssm_out_g, w_out, norm2_g, w_mlp_in, w_mlp_out, loss_target, m_norm1_g, m_w_in, m_q_norm_g, m_k_norm_g, m_ssm_lambda_re, m_ssm_lambda_im, m_ssm_log_dt, m_ssm_b_re, m_ssm_b_im, m_ssm_c_re, m_ssm_c_im, m_ssm_d, m_w_glu, m_b_glu, m_attn_out_g, m_ssm_out_g, m_w_out, m_norm2_g, m_w_mlp_in, m_w_mlp_out, v_norm1_g, v_w_in, v_q_norm_g, v_k_norm_g, v_ssm_lambda_re, v_ssm_lambda_im, v_ssm_log_dt, v_ssm_b_re, v_ssm_b_im, v_ssm_c_re, v_ssm_c_im, v_ssm_d, v_w_glu, v_b_glu, v_attn_out_g, v_ssm_out_g, v_w_out, v_norm2_g, v_w_mlp_in, v_w_mlp_out):
    batch, seq, d_model = x.shape
    tokens = batch * seq
    sb_width = w_in.shape[1]
    n_features = d_model

    big = [("w_in", w_in, m_w_in, v_w_in, 1), ("w_glu", w_glu, m_w_glu, v_w_glu, 0),
           ("w_out", w_out, m_w_out, v_w_out, 0), ("w_mlp_in", w_mlp_in, m_w_mlp_in, v_w_mlp_in, 1),
           ("w_mlp_out", w_mlp_out, m_w_mlp_out, v_w_mlp_out, 0)]
    small = [("norm1_g", norm1_g, m_norm1_g, v_norm1_g), ("q_norm_g", q_norm_g, m_q_norm_g, v_q_norm_g),
             ("k_norm_g", k_norm_g, m_k_norm_g, v_k_norm_g),
             ("ssm_lambda_re", ssm_lambda_re, m_ssm_lambda_re, v_ssm_lambda_re),
             ("ssm_lambda_im", ssm_lambda_im, m_ssm_lambda_im, v_ssm_lambda_im),
             ("ssm_log_dt", ssm_log_dt, m_ssm_log_dt, v_ssm_log_dt),
             ("ssm_b_re", ssm_b_re, m_ssm_b_re, v_ssm_b_re), ("ssm_b_im", ssm_b_im, m_ssm_b_im, v_ssm_b_im),
             ("ssm_c_re", ssm_c_re, m_ssm_c_re, v_ssm_c_re), ("ssm_c_im", ssm_c_im, m_ssm_c_im, v_ssm_c_im),
             ("ssm_d", ssm_d, m_ssm_d, v_ssm_d), ("b_glu", b_glu, m_b_glu, v_b_glu),
             ("attn_out_g", attn_out_g, m_attn_out_g, v_attn_out_g), ("ssm_out_g", ssm_out_g, m_ssm_out_g, v_ssm_out_g),
             ("norm2_g", norm2_g, m_norm2_g, v_norm2_g)]

    gather_in, tok_in = _chip_exchange_start("gather_w_in_start", "gather", [(w_in.astype(BF16), 1)])
    gather_rest, tok_rest = _chip_exchange_start(
        "gather_rest_start", "gather", [(w.astype(BF16), axis) for _, w, _, _, axis in big[1:]], after=tok_in)

    x2 = x.reshape(tokens, d_model)
    tgt2 = loss_target.reshape(tokens, d_model)
    g1, g2 = norm1_g[None, :], norm2_g[None, :]
    g_attn, g_ssm, bias_glu = attn_out_g[None, :], ssm_out_g[None, :], b_glu[None, :]
    heads = sb_width // HEAD_DIM
    qk_scale = 1.0 / math.sqrt(HEAD_DIM)
    gq, gk = (jnp.tile(q_norm_g, heads) * qk_scale)[None, :], jnp.tile(k_norm_g, heads)[None, :]
    lane_head = jnp.arange(LANES) // HEAD_DIM
    ones_blocks = (lane_head[:, None] == lane_head[None, :]).astype(F32)

    (xn,) = _rowwise("norm1", _rms, [x2], [g1 + tok_rest], [(d_model, BF16)])
    s5_params = (ssm_lambda_re, ssm_lambda_im, ssm_log_dt, ssm_b_re, ssm_b_im, ssm_c_re, ssm_c_im, ssm_d)
    (kt_row, b_mat, c_mat, la, lb), s5_vjp = jax.vjp(_s5_operators, *s5_params)
    (wf_in,) = _chip_exchange_wait("gather_w_in_wait", gather_in, [xn, b_mat, c_mat])
    (proj,) = _mm("proj_in", xn, wf_in, "nn")

    def qkv_fn(q, k, v, gq_, gk_, ones):
        return _head_rms(q, gq_, ones), _head_rms(k, gk_, ones), v

    qn, kn, vb = _rowwise("qk_norm", qkv_fn, [(proj, sb_width, 0), (proj, sb_width, 1), (proj, sb_width, 2)],
                          [gq, gk, ones_blocks], [(sb_width, BF16)] * 3)
    sb, c_tot = _attn_fwd(qn, kn, vb, batch=batch, seq=seq, bq=ATTN_BQ, bk=ATTN_BK)
    ug = _tokens_to_groups("u_to_groups", proj, 3, sb_width)
    yg, xin = _s5_fwd(ug, kt_row, b_mat, c_mat, la, lb, batch=batch)
    y_ssm = _groups_to_tokens("y_to_tokens", yg)

    (y_act,) = _rowwise("gelu", _gelu, [y_ssm], [], [(sb_width, BF16)])
    wf_glu, wf_out, wf_mlp_in, wf_mlp_out = _chip_exchange_wait("gather_rest_wait", gather_rest, y_act)
    (gate_pre,) = _mm("glu_gate", y_act, wf_glu, "nn", extras=[(bias_glu, "row")], epilogue=lambda acc, b: acc + b)
    (mixed,) = _rowwise("mix_norm", _mixed, [sb, y_ssm, gate_pre], [g_attn, g_ssm], [(2 * sb_width, BF16)])
    (h1,) = _mm("proj_out", mixed, wf_out, "nn", extras=[(x2, "tile")], epilogue=lambda acc, r: acc + r)
    (hn,) = _rowwise("norm2", _rms, [h1], [g2], [(d_model, BF16)])
    def mlp_act(acc):
        r = jnp.maximum(acc, 0.0)
        return r * r, r

    act, act_root = _mm("mlp_in", hn, wf_mlp_in, "nn", epilogue=mlp_act, out_dtypes=(BF16, BF16))
    inv_n = 1.0 / n_features

    def loss_head(acc, r, t):
        d = ((acc + r) - t) * inv_n
        return d, d, jnp.sum(d * d, keepdims=True) * (0.5 * n_features)

    dy, dy_b, loss_tiles = _mm("mlp_out_loss", act, wf_mlp_out, "nn", extras=[(h1, "tile"), (tgt2, "tile")],
                               epilogue=loss_head, out_dtypes=(F32, BF16), tile_sums=1)
    loss_part = jnp.sum(loss_tiles)

    (dw_mlp_out,) = _mm("dw_mlp_out", act, dy_b, "tn")
    (dpre,) = _mm("d_mlp_act", dy_b, wf_mlp_out, "nt", extras=[(act_root, "tile")],
                  epilogue=lambda acc, r: acc * (2.0 * r.astype(F32)), out_dtypes=(BF16,))
    (dw_mlp_in,) = _mm("dw_mlp_in", hn, dpre, "tn")
    scatter_mlp, tok_mlp = _chip_exchange_start("scatter_mlp_start", "scatter", [(dw_mlp_in, 1), (dw_mlp_out, 0)])
    (dhn,) = _mm("d_norm2_in", dpre, wf_mlp_in, "nt")

    def norm_bwd(res, hx, dn, g):
        _, vjp = jax.vjp(_rms, hx, g)
        dh, dg = vjp(dn)
        return res + dh, dg

    dh1, dg_norm2 = _rowwise("norm2_bwd", norm_bwd, [dy, h1, dhn], [g2 + tok_mlp], [(d_model, F32)], [(1, d_model)])
    (dmixed,) = _mm("d_mixed", dh1, wf_out, "nt")
    (dw_out,) = _mm("dw_out", mixed, dh1, "tn")

    def mixed_bwd(dm, sb_, ys, gp, ga, gs):
        _, vjp = jax.vjp(lambda a, act, b, c, d: jnp.concatenate(
            [_rms(a, c), _rms(act * jax.nn.sigmoid(b), d)], axis=-1), sb_, _gelu(ys), gp, ga, gs)
        dsb_, dact, dgp_, dga, dgs = vjp(dm)
        return dsb_, dgp_, dact, dga, dgs, jnp.sum(dgp_, axis=0, keepdims=True)

    dsb, dgate_pre, dact_part, dg_attn, dg_ssm, db_glu = _rowwise(
        "mix_norm_bwd", mixed_bwd, [dmixed, sb, y_ssm, gate_pre], [g_attn, g_ssm],
        [(sb_width, F32), (sb_width, BF16), (sb_width, F32)], [(1, sb_width)] * 3)

    def gelu_bwd(acc, part, ys):
        _, vjp = jax.vjp(_gelu, ys)
        return vjp(acc + part)[0]

    (dy_ssm,) = _mm("d_glu_in", dgate_pre, wf_glu, "nt", extras=[(dact_part, "tile"), (y_ssm, "tile")], epilogue=gelu_bwd)
    (dw_glu,) = _mm("dw_glu", y_act, dgate_pre, "tn")
    scatter_mix, tok_mix = _chip_exchange_start("scatter_mix_start", "scatter", [(dw_glu, 0), (dw_out, 0)])

    dug, dkt_row, db_mat, dc_mat, dla, dlb = _s5_bwd(ug, _tokens_to_groups("dy_to_groups", dy_ssm, 0, sb_width), xin,
                                                     kt_row, b_mat, c_mat, la, lb + tok_mix, batch=batch)
    du = _groups_to_tokens("du_to_tokens", dug)
    ds5 = s5_vjp((dkt_row, db_mat, dc_mat, dla, dlb))

    dqn, dkn, dv = _attn_bwd(qn, kn, vb, c_tot, dsb, batch=batch, seq=seq, bq=ATTN_BQ, bk=ATTN_BK)

    def qk_bwd(q, k, dq_, dk_, gq_, gk_, ones):
        _, vjp_q = jax.vjp(lambda a, g: _head_rms(a, g, ones), q, gq_)
        _, vjp_k = jax.vjp(lambda a, g: _head_rms(a, g, ones), k, gk_)
        dq, dgq = vjp_q(dq_)
        dk, dgk = vjp_k(dk_)
        return dq, dk, dgq, dgk

    dq, dk, dgq, dgk = _rowwise("qk_norm_bwd", qk_bwd, [(proj, sb_width, 0), (proj, sb_width, 1), dqn, dkn],
                                [gq, gk, ones_blocks], [(sb_width, BF16)] * 2, [(1, sb_width)] * 2)
    dproj = jnp.concatenate([dq, dk, dv.astype(BF16), du.astype(BF16)], axis=1)
    (dw_in,) = _mm("dw_in", xn, dproj, "tn")
    scatter_in, tok_w_in = _chip_exchange_start("scatter_in_start", "scatter", [(dw_in, 1)])
    (dxn,) = _mm("d_norm1_in", dproj, wf_in, "nt")
    dx, dg_norm1 = _rowwise("norm1_bwd", norm_bwd, [dh1, x2, dxn], [g1 + tok_w_in], [(d_model, F32)], [(1, d_model)])

    small_grads = [dg_norm1[0], dgq.reshape(heads, HEAD_DIM).sum(0) * qk_scale, dgk.reshape(heads, HEAD_DIM).sum(0), *ds5,
                   db_glu[0], dg_attn[0], dg_ssm[0], dg_norm2[0]]
    order = ["norm1_g", "q_norm_g", "k_norm_g", "ssm_lambda_re", "ssm_lambda_im", "ssm_log_dt", "ssm_b_re", "ssm_b_im",
             "ssm_c_re", "ssm_c_im", "ssm_d", "b_glu", "attn_out_g", "ssm_out_g", "norm2_g"]
    assert order == [name for name, *_ in small]

    def pack(parts, extra=None):
        flat = [p.reshape(-1) for p in parts] + ([extra.reshape(-1)] if extra is not None else [])
        flat = jnp.concatenate(flat)
        rows = -(-flat.shape[0] // (LANES * LANES)) * LANES
        return jnp.pad(flat, (0, rows * LANES - flat.shape[0])).reshape(rows, LANES)

    n_small = sum(w.size for _, w, _, _ in small)
    small_exchange, _ = _chip_exchange_start("small_grads_start", "all", [(pack(small_grads, loss_part), 0)])

    def adam_big(sa, sb_, w, m, v):
        g = sa + sb_
        delta, m, v = _adamw(g, w, m, v)
        return g, delta, m, v

    def reduce_and_update(tag, params, slots):
        mine = [_sum_slots("sum_" + name, s) for s, (name, *_rest) in zip(slots, params)]
        theirs = _swap_with_sibling("swap_" + tag, mine)
        return {name: _rowwise("adamw_" + name, adam_big, [sa, sb_, w, m, v], [], [(w.shape[1], F32)] * 4)
                for (name, w, m, v, _), sa, sb_ in zip(params, mine, theirs)}

    started = small_exchange[3][0]
    slots_mlp_in, slots_mlp_out = _chip_exchange_wait("scatter_mlp_wait", scatter_mlp, started)
    slots_glu, slots_out = _chip_exchange_wait("scatter_mix_wait", scatter_mix, started)
    big_out = reduce_and_update("rest", big[1:], [slots_glu, slots_out, slots_mlp_in, slots_mlp_out])

    (small_slots,) = _chip_exchange_wait("small_grads_wait", small_exchange, big_out["w_mlp_out"][3])
    reduced = _sum_slots("sum_small", small_slots)
    loss = reduced.reshape(-1)[n_small]
    flat, small_g, off = reduced.reshape(-1), {}, 0
    for name, w, _, _ in small:
        small_g[name] = flat[off:off + w.size].reshape(w.shape)
        off += w.size
    small_upd = _adamw_small([small_g[name] for name, *_ in small], [w for _, w, _, _ in small],
                             [m for _, _, m, _ in small], [v for _, _, _, v in small])
    small_out = [small_g] + [{name: small_upd[kind][i] for i, (name, *_) in enumerate(small)} for kind in range(3)]

    (slots_in,) = _chip_exchange_wait("scatter_in_wait", scatter_in, reduced)
    big_out.update(reduce_and_update("w_in", big[:1], [slots_in]))
    names = ["norm1_g", "w_in", "q_norm_g", "k_norm_g", "ssm_lambda_re", "ssm_lambda_im", "ssm_log_dt", "ssm_b_re",
             "ssm_b_im", "ssm_c_re", "ssm_c_im", "ssm_d", "w_glu", "b_glu", "attn_out_g", "ssm_out_g", "w_out",
             "norm2_g", "w_mlp_in", "w_mlp_out"]
    outs = [loss, dx.reshape(batch, seq, d_model)]
    for kind in range(4):
        for name in names:
            outs.append(big_out[name][kind] if name in big_out else small_out[kind][name])
    return tuple(outs)
```

```python
import functools
import math

import jax
import jax.numpy as jnp
from jax import lax
from jax.experimental import pallas as pl
from jax.experimental.pallas import tpu as pltpu

F32 = jnp.float32
BF16 = jnp.bfloat16
F32_DOT = lax.Precision.HIGH
MESH = pl.DeviceIdType.MESH

RMS_EPS = 1e-6
HEAD_DIM = 64
SSM_GROUP = 16
SSM_CHUNK = 16
LANES = 128
N_CHIPS = 4
N_DEV = 8
VMEM_LIMIT = 48 * 1024 * 1024

ADAM_LR = 0.001
ADAM_B1 = 0.9
ADAM_B2 = 0.999
ADAM_EPS = 1e-08
ADAM_WD = 0.01
ADAM_STEP = 10


def _tile(n, pref):
    t = min(n, pref)
    while n % t:
        t //= 2
    return t


def _params(*sem):
    return pltpu.CompilerParams(dimension_semantics=sem, vmem_limit_bytes=VMEM_LIMIT)


_DIMS = {"nn": (((1,), (0,)), ((), ())), "nt": (((1,), (1,)), ((), ())), "tn": (((0,), (0,)), ((), ()))}


MM_VMEM_BUDGET = 40 * 1024 * 1024


def _mm_tiles(m, n, k, a_bytes, b_bytes, tile_bytes, full_rows=False):
    best = None
    for tk in [t for t in (k, k // 2, k // 4, k // 8) if t >= 256 or t == k]:
        for tm in [t for t in (1024, 512, 256, 128) if t <= m and m % t == 0]:
            for tn in [n] if full_rows else [t for t in (1024, 512, 256, 128) if t <= n and n % t == 0]:
                need = 2 * (tm * tk * a_bytes + tk * tn * b_bytes) + 2 * tm * tn * tile_bytes + (tm * tn * 4 if tk < k else 0)
                if need > MM_VMEM_BUDGET:
                    continue
                traffic = m * k * a_bytes * (n // tn) + k * n * b_bytes * (m // tm)
                key = (tk < k, traffic, -tm * tn)
                if best is None or key < best[0]:
                    best = (key, (tm, tn, tk))
    return best[1]


def _mm(name, a, b, mode, *, a_fn=None, extras=(), epilogue=None, out_dtypes=(F32,), tile_sums=(), full_rows=False):
    if mode == "nn":
        (m, k), n = a.shape, b.shape[1]
    elif mode == "nt":
        (m, k), n = a.shape, b.shape[0]
    else:
        (k, m), n = a.shape, b.shape[1]
    tile_bytes = sum(e.dtype.itemsize for e, kind in extras if kind == "tile") + sum(jnp.dtype(d).itemsize for d in out_dtypes)
    tm, tn, tk = _mm_tiles(m, n, k, a.dtype.itemsize, b.dtype.itemsize, tile_bytes, full_rows)
    nk = k // tk
    ne, nout = len(extras), len(out_dtypes)
    dims = _DIMS[mode]

    def body(a_ref, b_ref, *rest):
        ex, outs, sums = rest[:ne], rest[ne:ne + nout], rest[ne + nout:ne + nout + len(tile_sums)]
        at = a_ref[...]
        if a_fn is not None:
            at = a_fn(at)
        p = lax.dot_general(at.astype(BF16), b_ref[...].astype(BF16), dims, preferred_element_type=F32)

        def finish(r):
            if epilogue is not None:
                r = epilogue(r, *[e[...] for e in ex])
            if not isinstance(r, (tuple, list)):
                r = (r,)
            for o, v in zip(outs, r[:nout]):
                o[...] = v.astype(o.dtype)
            for o, v, kind in zip(sums, r[nout:], tile_sums):
                first = lax.broadcasted_iota(jnp.int32, o.shape, 0) == 0
                if kind == "scalar":
                    first &= lax.broadcasted_iota(jnp.int32, o.shape, 1) == 0
                o[...] = jnp.where(first, v, 0.0)

        if nk == 1:
            finish(p)
        else:
            acc = rest[ne + nout + len(tile_sums)]
            kk = pl.program_id(2)

            @pl.when(kk == 0)
            def _():
                acc[...] = p

            @pl.when(kk > 0)
            def _():
                acc[...] += p

            @pl.when(kk == nk - 1)
            def _():
                finish(acc[...])

    if mode == "tn":
        a_spec = pl.BlockSpec((tk, tm), lambda i, j, kk: (kk, i))
    else:
        a_spec = pl.BlockSpec((tm, tk), lambda i, j, kk: (i, kk))
    if mode == "nt":
        b_spec = pl.BlockSpec((tn, tk), lambda i, j, kk: (j, kk))
    else:
        b_spec = pl.BlockSpec((tk, tn), lambda i, j, kk: (kk, j))
    ex_specs = []
    for _, kind in extras:
        if kind == "tile":
            ex_specs.append(pl.BlockSpec((tm, tn), lambda i, j, kk: (i, j)))
        else:
            ex_specs.append(pl.BlockSpec((1, tn), lambda i, j, kk: (0, j)))
    return pl.pallas_call(
        body, name=name, grid=(m // tm, n // tn, nk),
        in_specs=[a_spec, b_spec] + ex_specs,
        out_specs=([pl.BlockSpec((tm, tn), lambda i, j, kk: (i, j)) for _ in out_dtypes]
                   + [pl.BlockSpec((8, LANES if kind == "scalar" else tn), lambda i, j, kk: (i, j)) for kind in tile_sums]),
        out_shape=([jax.ShapeDtypeStruct((m, n), dt) for dt in out_dtypes]
                   + [jax.ShapeDtypeStruct((m // tm * 8, n // tn * LANES if kind == "scalar" else n), F32)
                      for kind in tile_sums]),
        scratch_shapes=[pltpu.VMEM((tm, tn), F32)] if nk > 1 else [],
        compiler_params=_params("parallel", "parallel", "arbitrary"),
    )(a, b, *[e for e, _ in extras])


def _rowwise(name, fn, rows, consts, row_outs, acc_outs=(), tm=256):
    norm = [r if isinstance(r, tuple) else (r, r.shape[1], 0) for r in rows]
    t = norm[0][0].shape[0]
    tm = _tile(t, tm)
    nr, nc, no = len(norm), len(consts), len(row_outs)

    def body(*refs):
        outs = fn(*[r[...] for r in refs[:nr + nc]])
        if not isinstance(outs, (tuple, list)):
            outs = (outs,)
        o_refs, a_refs = refs[nr + nc:nr + nc + no], refs[nr + nc + no:]
        for r, v in zip(o_refs, outs[:no]):
            r[...] = v.astype(r.dtype)
        if a_refs:
            i = pl.program_id(0)

            @pl.when(i == 0)
            def _():
                for r, v in zip(a_refs, outs[no:]):
                    r[...] = v

            @pl.when(i > 0)
            def _():
                for r, v in zip(a_refs, outs[no:]):
                    r[...] += v

    in_specs = [pl.BlockSpec((tm, w), functools.partial(lambda i, cb: (i, cb), cb=cb)) for _, w, cb in norm]
    in_specs += [pl.BlockSpec(c.shape, functools.partial(lambda i, nd: (0,) * nd, nd=c.ndim)) for c in consts]
    out_specs = [pl.BlockSpec((tm, w), lambda i: (i, 0)) for w, _ in row_outs]
    out_specs += [pl.BlockSpec(s, functools.partial(lambda i, nd: (0,) * nd, nd=len(s))) for s in acc_outs]
    out_shape = [jax.ShapeDtypeStruct((t, w), dt) for w, dt in row_outs]
    out_shape += [jax.ShapeDtypeStruct(s, F32) for s in acc_outs]
    return pl.pallas_call(
        body, name=name, grid=(t // tm,), in_specs=in_specs, out_specs=out_specs, out_shape=out_shape,
        compiler_params=_params("arbitrary"),
    )(*[r[0] for r in norm], *consts)


def _rms(x, g):
    return x * lax.rsqrt(jnp.mean(x * x, axis=-1, keepdims=True) + RMS_EPS) * g


@jax.custom_vjp
def _head_sums(x, ones_blocks):
    parts = [jnp.dot(x[:, j:j + LANES], ones_blocks, precision=F32_DOT, preferred_element_type=F32)
             for j in range(0, x.shape[1], LANES)]
    return jnp.concatenate(parts, axis=1)


_head_sums.defvjp(lambda x, ones_blocks: (_head_sums(x, ones_blocks), ones_blocks),
                  lambda ones_blocks, ct: (_head_sums(ct, ones_blocks), None))


def _head_rms(x, g, ones_blocks):
    return x * lax.rsqrt(_head_sums(x * x, ones_blocks) * (1.0 / HEAD_DIM) + RMS_EPS) * g


def _gelu(x):
    return x * (0.5 * (1.0 + jnp.tanh(math.sqrt(2.0 / math.pi) * (x + 0.044715 * (x * x * x)))))


def _mixed(sb, y_ssm, gate_pre, g_attn, g_ssm):
    ssm = _gelu(y_ssm) * jax.nn.sigmoid(gate_pre)
    return jnp.concatenate([_rms(sb, g_attn), _rms(ssm, g_ssm)], axis=-1)


def _softplus(z):
    return jnp.maximum(z, 0.0) + jnp.log(1.0 + jnp.exp(-jnp.abs(z)))


def _running_sums(x, tri):
    return jnp.dot(x.astype(BF16), tri, preferred_element_type=F32)


def _dot_nt(a, b, **kw):
    return lax.dot_general(a, b, _DIMS["nt"], preferred_element_type=F32, **kw)


def _dot_tn(a, b, **kw):
    return lax.dot_general(a, b, _DIMS["tn"], preferred_element_type=F32, **kw)


ATTN_BQ, ATTN_BK = 2048, 256
HEAD_LANES = tuple(slice(h * HEAD_DIM, (h + 1) * HEAD_DIM) for h in range(LANES // HEAD_DIM))


def _attn_fwd(qs, kn, v, *, batch, seq, bq, bk):
    width = qs.shape[1]
    bq = _tile(seq, bq)
    bk = _tile(bq, bk)
    nq, kpq = seq // bq, bq // bk

    def body(q_ref, k_ref, v_ref, o_ref, c_ref):
        row = lax.broadcasted_iota(jnp.int32, (bq, bk), 0)
        col = lax.broadcasted_iota(jnp.int32, (bq, bk), 1)
        tri = (lax.broadcasted_iota(jnp.int32, (bk, bk), 0) >= lax.broadcasted_iota(jnp.int32, (bk, bk), 1)).astype(BF16)

        def q_block(qi, carry):
            r0 = pl.multiple_of(qi * bq, bq)
            qh = [q_ref[pl.ds(r0, bq), ln] for ln in HEAD_LANES]

            def tile(k0, state, top=0):
                diag = top is not None
                top = top or 0
                msk = (col < row)[:bq - top] if diag else None
                new = []
                for h, ln in enumerate(HEAD_LANES):
                    o, c = state[2 * h], state[2 * h + 1]
                    z = _dot_nt(qh[h][top:], k_ref[pl.ds(k0, bk), ln])
                    sp = _softplus(z)
                    if diag:
                        sp = jnp.where(msk, sp, 0.0)
                    r = _running_sums(sp, tri)
                    a = jnp.exp(z - r - c[top:])
                    if diag:
                        a = jnp.where(msk, a, 0.0)
                    o_new = o[top:] + jnp.dot(a.astype(BF16), v_ref[pl.ds(k0, bk), ln], preferred_element_type=F32)
                    c_new = c[top:] + r[:, 0:1]
                    if top:
                        o_new, c_new = jnp.concatenate([o[:top], o_new]), jnp.concatenate([c[:top], c_new])
                    new += [o_new, c_new]
                return tuple(new)

            state = (jnp.zeros((bq, HEAD_DIM), F32), jnp.zeros((bq, 1), F32)) * len(HEAD_LANES)
            for d in reversed(range(kpq)):
                state = tile(pl.multiple_of(r0 + d * bk, bk), state, top=d * bk)
            state = lax.fori_loop(0, qi * kpq, lambda it, st: tile(pl.multiple_of(r0 - (it + 1) * bk, bk), st, None),
                                  state)
            for h, ln in enumerate(HEAD_LANES):
                o_ref[pl.ds(r0, bq), ln] = state[2 * h]
                c_ref[pl.ds(r0, bq), ln] = jnp.broadcast_to(state[2 * h + 1], (bq, HEAD_DIM))
            return carry

        lax.fori_loop(0, nq, q_block, 0)

    spec = pl.BlockSpec((seq, LANES), lambda b, h: (b, h))
    shape = jax.ShapeDtypeStruct((batch * seq, width), F32)
    return pl.pallas_call(
        body, name="attn_fwd", grid=(batch, width // LANES), in_specs=[spec, spec, spec], out_specs=[spec, spec],
        out_shape=[shape, shape], compiler_params=_params("parallel", "parallel"),
    )(qs, kn, v)


def _attn_bwd(qs, kn, v, c_tot, do, *, batch, seq, bq, bk):
    width = qs.shape[1]
    bq = _tile(seq, bq)
    bk = _tile(bq, bk)
    nq, kpq = seq // bq, bq // bk

    def body(q_ref, k_ref, v_ref, c_ref, do_ref, dq_ref, dk_ref, dv_ref):
        row = lax.broadcasted_iota(jnp.int32, (bq, bk), 0)
        col = lax.broadcasted_iota(jnp.int32, (bq, bk), 1)
        sq_row = lax.broadcasted_iota(jnp.int32, (bk, bk), 0)
        sq_col = lax.broadcasted_iota(jnp.int32, (bk, bk), 1)
        tri = (sq_row >= sq_col).astype(BF16)
        tri_t = (sq_row <= sq_col).astype(BF16)
        dk_ref[...] = jnp.zeros_like(dk_ref)
        dv_ref[...] = jnp.zeros_like(dv_ref)

        def q_block(qi, carry):
            r0 = pl.multiple_of(qi * bq, bq)
            qh = [q_ref[pl.ds(r0, bq), ln] for ln in HEAD_LANES]
            d_out = [do_ref[pl.ds(r0, bq), ln].astype(BF16) for ln in HEAD_LANES]
            c_all = [c_ref[pl.ds(r0, bq), ln][:, 0:1] for ln in HEAD_LANES]

            def tile(k0, state, top=0):
                diag = top is not None
                top = top or 0
                last = diag and top == bq - bk
                msk = (col < row)[:bq - top] if diag else None
                new = []
                for h, ln in enumerate(HEAD_LANES):
                    c_left, g_left, dq = state[3 * h:3 * h + 3]
                    q, d_o = qh[h][top:], d_out[h][top:]
                    k = k_ref[pl.ds(k0, bk), ln]
                    z = _dot_nt(q, k)
                    e = jnp.exp(-jnp.abs(z))
                    sp = jnp.maximum(z, 0.0) + jnp.log(1.0 + e)
                    sig = jnp.exp(z - sp)
                    if diag:
                        sp = jnp.where(msk, sp, 0.0)
                    r = _running_sums(sp, tri)
                    c_new = c_left[top:] + r[:, 0:1]
                    a = jnp.exp(z - r - (0.0 if last else c_all[h][top:] - c_new))
                    if diag:
                        a = jnp.where(msk, a, 0.0)
                    g = a * _dot_nt(d_o, v_ref[pl.ds(k0, bk), ln])
                    pg = _running_sums(g, tri_t)
                    dz = g - sig * (g_left[top:] + pg)
                    if diag:
                        dz = jnp.where(msk, dz, 0.0)
                    dz = dz.astype(BF16)
                    dk_ref[pl.ds(k0, bk), ln] += _dot_tn(dz, q)
                    dv_ref[pl.ds(k0, bk), ln] += _dot_tn(a.astype(BF16), d_o)
                    g_new = g_left[top:] + pg[:, bk - 1:bk]
                    dq_new = dq[top:] + jnp.dot(dz, k, preferred_element_type=F32)
                    if top:
                        c_new = jnp.concatenate([c_left[:top], c_new])
                        g_new = jnp.concatenate([g_left[:top], g_new])
                        dq_new = jnp.concatenate([dq[:top], dq_new])
                    new += [c_new, g_new, dq_new]
                return tuple(new)

            zero = jnp.zeros((bq, 1), F32)
            init = (zero, zero, jnp.zeros((bq, HEAD_DIM), F32)) * len(HEAD_LANES)
            state = lax.fori_loop(0, qi * kpq, lambda it, st: tile(pl.multiple_of(it * bk, bk), st, None), init)
            for d in range(kpq):
                state = tile(pl.multiple_of(r0 + d * bk, bk), state, top=d * bk)
            for h, ln in enumerate(HEAD_LANES):
                dq_ref[pl.ds(r0, bq), ln] = state[3 * h + 2]
            return carry

        lax.fori_loop(0, nq, q_block, 0)

    spec = pl.BlockSpec((seq, LANES), lambda b, h: (b, h))
    shape = jax.ShapeDtypeStruct((batch * seq, width), F32)
    return pl.pallas_call(
        body, name="attn_bwd", grid=(batch, width // LANES), in_specs=[spec] * 5, out_specs=[spec] * 3,
        out_shape=[shape] * 3, compiler_params=_params("parallel", "parallel"),
    )(qs, kn, v, c_tot, do)


def _s5_operators(lam_re, lam_im, log_dt, b_re, b_im, c_re, c_im, d_skip):
    groups, n_state, n_ch = b_re.shape
    cs = SSM_CHUNK
    dt = jnp.exp(log_dt)[:, None]
    steps = jnp.arange(cs + 1, dtype=F32)[None, :, None]
    mag = jnp.exp(steps * (lam_re * dt)[:, None, :])
    ang = steps * (lam_im * dt)[:, None, :]
    pw_re, pw_im = mag * jnp.cos(ang), mag * jnp.sin(ang)
    num_re, num_im = pw_re[:, 1] - 1.0, pw_im[:, 1]
    den = lam_re * lam_re + lam_im * lam_im
    cf_re = (num_re * lam_re + num_im * lam_im) / den
    cf_im = (num_im * lam_re - num_re * lam_im) / den
    bb_re = cf_re[:, :, None] * b_re - cf_im[:, :, None] * b_im
    bb_im = cf_re[:, :, None] * b_im + cf_im[:, :, None] * b_re
    width = cs * n_ch
    ct_re, ct_im = c_re.transpose(0, 2, 1), c_im.transpose(0, 2, 1)

    def c_times_powers(first):
        pr = pw_re[:, first:first + cs].transpose(0, 2, 1)[:, :, :, None]
        pi = pw_im[:, first:first + cs].transpose(0, 2, 1)[:, :, :, None]
        re = pr * ct_re[:, :, None, :] - pi * ct_im[:, :, None, :]
        im = pr * ct_im[:, :, None, :] + pi * ct_re[:, :, None, :]
        return re.reshape(groups, n_state, width), im.reshape(groups, n_state, width)

    w_re, w_im = c_times_powers(0)
    kt_row = (jnp.einsum("gpi,gpw->giw", bb_re, w_re, precision=F32_DOT)
              - jnp.einsum("gpi,gpw->giw", bb_im, w_im, precision=F32_DOT))
    kt_row = kt_row + jnp.pad(jnp.eye(n_ch, dtype=F32)[None] * d_skip[:, None, :], ((0, 0), (0, 0), (0, width - n_ch)))
    rp_re, rp_im = pw_re[:, cs - 1::-1][:, :cs], pw_im[:, cs - 1::-1][:, :cs]
    bm_re = rp_re[:, :, None, :] * bb_re.transpose(0, 2, 1)[:, None] - rp_im[:, :, None, :] * bb_im.transpose(0, 2, 1)[:, None]
    bm_im = rp_re[:, :, None, :] * bb_im.transpose(0, 2, 1)[:, None] + rp_im[:, :, None, :] * bb_re.transpose(0, 2, 1)[:, None]
    b_mat = jnp.concatenate([bm_re, bm_im], axis=-1).reshape(groups, width, 2 * n_state)
    w1_re, w1_im = c_times_powers(1)
    c_mat = jnp.concatenate([w1_re, -w1_im], axis=1)
    la = jnp.concatenate([pw_re[:, cs], pw_re[:, cs]], axis=-1)[:, None, :]
    lb = jnp.concatenate([-pw_im[:, cs], pw_im[:, cs]], axis=-1)[:, None, :]
    return kt_row, b_mat, c_mat, la, lb


GROUPS_PER_BLOCK = LANES // SSM_GROUP


def _tokens_to_groups(name, u, col_block, width):
    t = u.shape[0]
    n = t // SSM_CHUNK
    ch = SSM_CHUNK * SSM_GROUP
    blocks = width // LANES

    def body(u_ref, o_ref):
        for s in range(SSM_CHUNK):
            rows = u_ref[pl.ds(s, n, stride=SSM_CHUNK), :]
            for g in range(GROUPS_PER_BLOCK):
                o_ref[g, :, s * SSM_GROUP:(s + 1) * SSM_GROUP] = rows[:, g * SSM_GROUP:(g + 1) * SSM_GROUP]

    return pl.pallas_call(
        body, name=name, grid=(blocks,),
        in_specs=[pl.BlockSpec((t, LANES), lambda j: (0, col_block * blocks + j))],
        out_specs=pl.BlockSpec((GROUPS_PER_BLOCK, n, ch), lambda j: (j, 0, 0)),
        out_shape=jax.ShapeDtypeStruct((width // SSM_GROUP, n, ch), F32), compiler_params=_params("parallel"),
    )(u)


def _groups_to_tokens(name, ug):
    groups, n, ch = ug.shape

    def body(g_ref, o_ref, rows_ref):
        for s in range(SSM_CHUNK):
            for g in range(GROUPS_PER_BLOCK):
                rows_ref[s % 2, :, g * SSM_GROUP:(g + 1) * SSM_GROUP] = g_ref[g, :, s * SSM_GROUP:(s + 1) * SSM_GROUP]
            o_ref[pl.ds(s, n, stride=SSM_CHUNK), :] = rows_ref[s % 2]

    return pl.pallas_call(
        body, name=name, grid=(groups // GROUPS_PER_BLOCK,),
        in_specs=[pl.BlockSpec((GROUPS_PER_BLOCK, n, ch), lambda j: (j, 0, 0))],
        out_specs=pl.BlockSpec((n * SSM_CHUNK, LANES), lambda j: (0, j)),
        out_shape=jax.ShapeDtypeStruct((n * SSM_CHUNK, groups * SSM_GROUP), F32),
        scratch_shapes=[pltpu.VMEM((2, n, LANES), F32)], compiler_params=_params("parallel"),
    )(ug)


SCAN_ROWS = 8


def _toeplitz_to(tm_ref, g, kt_row):
    width = kt_row.shape[1]
    tm_ref[g] = jnp.zeros((width, width), F32)
    for s in range(SSM_CHUNK):
        tm_ref[g, s * SSM_GROUP:(s + 1) * SSM_GROUP, s * SSM_GROUP:] = kt_row[:, :width - s * SSM_GROUP]


def _lam_powers(la, lb, reverse):
    if reverse:
        lb = -lb

    def mul(p, q):
        return p[0] * q[0] - p[1] * q[1], p[0] * q[1] + p[1] * q[0]

    p1 = (la, lb)
    p2 = mul(p1, p1)
    p3 = mul(p2, p1)
    p4 = mul(p2, p2)
    rows = [p1, p2, p3, p4, mul(p4, p1), mul(p4, p2), mul(p4, p3), mul(p4, p4)]
    if reverse:
        rows = rows[::-1]
    idx = lax.broadcasted_iota(jnp.int32, (SCAN_ROWS, la.shape[1]), 0)
    tab_a = sum(jnp.where(idx == j, r[0], 0.0) for j, r in enumerate(rows))
    tab_b = sum(jnp.where(idx == j, r[1], 0.0) for j, r in enumerate(rows))
    return (p1, p2, p4), (tab_a, tab_b), idx


def _scan_block(e, carry, steps, table, idx, half, reverse):
    n = SCAN_ROWS
    for d, (pa, pb) in zip((1, 2, 4), steps):
        sh = pltpu.roll(e, n - d if reverse else d, 0)
        sh = jnp.where(idx < n - d if reverse else idx >= d, sh, 0.0)
        e = e + pa * sh + pb * pltpu.roll(sh, half, 1)
    tab_a, tab_b = table
    e = e + tab_a * carry + tab_b * pltpu.roll(carry, half, 1)
    shifted = jnp.where(idx == (n - 1 if reverse else 0), carry, pltpu.roll(e, n - 1 if reverse else 1, 0))
    edge = e[0:1] if reverse else e[n - 1:n]
    return shifted, jnp.broadcast_to(edge, e.shape)


def _s5_fwd(ug, kt_row, b_mat, c_mat, la, lb, *, batch, gb=8):
    groups, n, ch = ug.shape
    p2 = b_mat.shape[2]
    gb = _tile(groups, gb)
    nch = n // batch
    nblk = nch // SCAN_ROWS

    def body(u_ref, k_ref, b_ref, c_ref, la_ref, lb_ref, y_ref, x_ref, s_ref, tm_ref):
        for g in range(gb):
            _toeplitz_to(tm_ref, g, k_ref[g])
            s_ref[g] = jnp.dot(u_ref[g], b_ref[g], precision=F32_DOT, preferred_element_type=F32)
        powers = [_lam_powers(la_ref[g], lb_ref[g], False) for g in range(gb)]

        def step(blk, carries):
            new = []
            for g in range(gb):
                steps, table, idx = powers[g]
                for b in range(batch):
                    rows = pl.ds(pl.multiple_of(b * nch + blk * SCAN_ROWS, SCAN_ROWS), SCAN_ROWS)
                    x_in, carry = _scan_block(s_ref[g, rows, :], carries[g * batch + b], steps, table, idx, p2 // 2, False)
                    x_ref[g, rows, :] = x_in
                    new.append(carry)
            return tuple(new)

        lax.fori_loop(0, nblk, step, tuple(jnp.zeros((SCAN_ROWS, p2), F32) for _ in range(gb * batch)))
        for g in range(gb):
            y_ref[g] = (jnp.dot(u_ref[g], tm_ref[g], precision=F32_DOT, preferred_element_type=F32)
                        + jnp.dot(x_ref[g], c_ref[g], precision=F32_DOT, preferred_element_type=F32))

    def spec(a, b):
        return pl.BlockSpec((gb, a, b), lambda i: (i, 0, 0))

    return pl.pallas_call(
        body, name="s5_fwd", grid=(groups // gb,),
        in_specs=[spec(n, ch), spec(SSM_GROUP, ch), spec(ch, p2), spec(p2, ch), spec(1, p2), spec(1, p2)],
        out_specs=[spec(n, ch), spec(n, p2)],
        out_shape=[jax.ShapeDtypeStruct((groups, n, ch), F32), jax.ShapeDtypeStruct((groups, n, p2), F32)],
        scratch_shapes=[pltpu.VMEM((gb, n, p2), F32), pltpu.VMEM((gb, ch, ch), F32)],
        compiler_params=_params("parallel"),
    )(ug, kt_row, b_mat, c_mat, la, lb)


def _s5_bwd(ug, dyg, xin, kt_row, b_mat, c_mat, la, lb, *, batch, gb=8):
    groups, n, ch = ug.shape
    p2 = b_mat.shape[2]
    gb = _tile(groups, gb)
    nch = n // batch
    nblk = nch // SCAN_ROWS

    def body(u_ref, dy_ref, x_ref, k_ref, b_ref, c_ref, la_ref, lb_ref,
             du_ref, dk_ref, db_ref, dc_ref, dla_ref, dlb_ref, dx_ref, ds_ref, tm_ref):
        for g in range(gb):
            _toeplitz_to(tm_ref, g, k_ref[g])
            dx_ref[g] = _dot_nt(dy_ref[g], c_ref[g], precision=F32_DOT)
        powers = [_lam_powers(la_ref[g], lb_ref[g], True) for g in range(gb)]

        def step(it, carries):
            new = []
            for g in range(gb):
                steps, table, idx = powers[g]
                for b in range(batch):
                    rows = pl.ds(pl.multiple_of(b * nch + (nblk - 1 - it) * SCAN_ROWS, SCAN_ROWS), SCAN_ROWS)
                    d_s, carry = _scan_block(dx_ref[g, rows, :], carries[g * batch + b], steps, table, idx, p2 // 2, True)
                    ds_ref[g, rows, :] = d_s
                    new.append(carry)
            return tuple(new)

        lax.fori_loop(0, nblk, step, tuple(jnp.zeros((SCAN_ROWS, p2), F32) for _ in range(gb * batch)))
        for g in range(gb):
            u, dy, ds, x = u_ref[g], dy_ref[g], ds_ref[g], x_ref[g]
            du_ref[g] = _dot_nt(dy, tm_ref[g], precision=F32_DOT) + _dot_nt(ds, b_ref[g], precision=F32_DOT)
            tm_ref[g] = _dot_tn(u, dy, precision=F32_DOT)
            dk_ref[g] = tm_ref[g, 0:SSM_GROUP, :]
            for s in range(1, SSM_CHUNK):
                dk_ref[g, :, :ch - s * SSM_GROUP] += tm_ref[g, s * SSM_GROUP:(s + 1) * SSM_GROUP, s * SSM_GROUP:]
            db_ref[g] = _dot_tn(u, ds, precision=F32_DOT)
            dc_ref[g] = _dot_tn(x, dy, precision=F32_DOT)
            dla_ref[g] = jnp.sum(ds * x, axis=0, keepdims=True)
            dlb_ref[g] = jnp.sum(ds * pltpu.roll(x, p2 // 2, 1), axis=0, keepdims=True)

    def spec(a, b):
        return pl.BlockSpec((gb, a, b), lambda i: (i, 0, 0))

    def shape(a, b):
        return jax.ShapeDtypeStruct((groups, a, b), F32)

    return pl.pallas_call(
        body, name="s5_bwd", grid=(groups // gb,),
        in_specs=[spec(n, ch), spec(n, ch), spec(n, p2), spec(SSM_GROUP, ch), spec(ch, p2), spec(p2, ch), spec(1, p2),
                  spec(1, p2)],
        out_specs=[spec(n, ch), spec(SSM_GROUP, ch), spec(ch, p2), spec(p2, ch), spec(1, p2), spec(1, p2)],
        out_shape=[shape(n, ch), shape(SSM_GROUP, ch), shape(ch, p2), shape(p2, ch), shape(1, p2), shape(1, p2)],
        scratch_shapes=[pltpu.VMEM((gb, n, p2), F32), pltpu.VMEM((gb, n, p2), F32), pltpu.VMEM((gb, ch, ch), F32)],
        compiler_params=_params("parallel"),
    )(ug, dyg, xin, kt_row, b_mat, c_mat, la, lb)


def _block(ref, axis, j, size):
    start = j * size if isinstance(j, int) else pl.multiple_of(j * size, size)
    return ref.at[pl.ds(start, size), :] if axis == 0 else ref.at[:, pl.ds(start, size)]


def _chip_exchange_copies(mode, axes, srcs, lands, send_sems, recv_sems, local_sems):
    x, y, c = lax.axis_index("x"), lax.axis_index("y"), lax.axis_index("c")
    everyone = mode == "all"
    me = 4 * x + 2 * y + c if everyone else 2 * x + y
    n_peers = _exchange_peers(mode)
    local, sends, arrivals = [], [], []
    for w, axis in enumerate(axes):
        if mode == "gather":
            size = srcs[w].shape[axis]
            local.append(pltpu.make_async_copy(srcs[w], _block(lands[w], axis, me, size), local_sems.at[w]))
        elif mode == "scatter":
            size = srcs[w].shape[axis] // N_CHIPS
            local.append(pltpu.make_async_copy(_block(srcs[w], axis, me, size), lands[w].at[me], local_sems.at[w]))
        else:
            local.append(pltpu.make_async_copy(srcs[w], lands[w].at[me], local_sems.at[w]))
        for k in range(1, n_peers + 1):
            bits = k if everyone else 2 * k
            px = 1 - x if bits & 4 else x
            py = 1 - y if bits & 2 else y
            pc = 1 - c if bits & 1 else c
            peer = 4 * px + 2 * py + pc if everyone else 2 * px + py
            if mode == "gather":
                src, dst, arrive = srcs[w], _block(lands[w], axis, me, size), _block(lands[w], axis, peer, size)
            elif mode == "scatter":
                src, dst, arrive = _block(srcs[w], axis, peer, size), lands[w].at[me], lands[w].at[peer]
            else:
                src, dst, arrive = srcs[w], lands[w].at[me], lands[w].at[peer]
            sem = w * n_peers + k - 1
            for target, out in ((dst, sends), (arrive, arrivals)):
                out.append(pltpu.make_async_remote_copy(
                    src_ref=src, dst_ref=target, send_sem=send_sems.at[sem], recv_sem=recv_sems.at[sem],
                    device_id=(px, py, pc), device_id_type=MESH))
    return local, sends, arrivals


def _exchange_peers(mode):
    return N_DEV - 1 if mode == "all" else N_CHIPS - 1


def _chip_exchange_start(name, mode, items, after=None):
    n = len(items)
    n_after = 0 if after is None else 1
    axes = [axis for _, axis in items]
    hbm = pl.BlockSpec(memory_space=pltpu.HBM)
    sem = pl.BlockSpec(memory_space=pltpu.SEMAPHORE)
    lands = []
    for a, axis in items:
        shape = list(a.shape)
        if mode == "gather":
            shape[axis] *= N_CHIPS
        elif mode == "scatter":
            shape[axis] //= N_CHIPS
            shape = [N_CHIPS] + shape
        else:
            shape = [N_DEV] + shape
        lands.append(pltpu.with_memory_space_constraint(lax.empty(tuple(shape), a.dtype), pltpu.HBM))

    def body(*refs):
        srcs, land_refs = refs[:n], refs[n:2 * n]
        send_sems, recv_sems, local_sems = refs[2 * n + n_after:2 * n + n_after + 3]
        token = refs[-1]
        local, sends, _ = _chip_exchange_copies(mode, axes, srcs, land_refs, send_sems, recv_sems, local_sems)
        for cp in local + sends:
            cp.start()
        token[...] = jnp.zeros_like(token)

    n_sem = n * _exchange_peers(mode)
    outs = pl.pallas_call(
        body, name=name,
        out_shape=(pltpu.SemaphoreType.DMA((n_sem,)), pltpu.SemaphoreType.DMA((n_sem,)), pltpu.SemaphoreType.DMA((n,)),
                   *[pltpu.HBM(a.shape, a.dtype) for a, _ in items], *[pltpu.HBM(l.shape, l.dtype) for l in lands],
                   jax.ShapeDtypeStruct((8, LANES), F32)),
        in_specs=[hbm] * (2 * n) + [pl.BlockSpec(memory_space=pl.ANY)] * n_after,
        out_specs=(sem, sem, sem, *[hbm] * (2 * n), pl.BlockSpec(memory_space=pltpu.VMEM)),
        input_output_aliases={i: 3 + i for i in range(2 * n)},
        compiler_params=pltpu.CompilerParams(has_side_effects=pltpu.SideEffectType.DATAFLOW_SIDE_EFFECTING),
    )(*[pltpu.with_memory_space_constraint(a, pltpu.HBM) for a, _ in items], *lands, *([after] if n_after else []))
    return (mode, axes, outs[:3], outs[3:3 + n], outs[3 + n:3 + 2 * n]), outs[-1][0:1, 0:1]


def _chip_exchange_wait(name, handle, after):
    mode, axes, sems, srcs, lands = handle
    n = len(axes)
    after = list(after) if isinstance(after, (tuple, list)) else [after]
    hbm = pl.BlockSpec(memory_space=pltpu.HBM)
    sem = pl.BlockSpec(memory_space=pltpu.SEMAPHORE)

    def body(*refs):
        src_refs, land_refs = refs[:n], refs[n:2 * n]
        send_sems, recv_sems, local_sems = refs[2 * n:2 * n + 3]
        local, sends, arrivals = _chip_exchange_copies(mode, axes, src_refs, land_refs, send_sems, recv_sems, local_sems)
        for cp in sends:
            cp.wait_send()
        for cp in arrivals:
            cp.wait_recv()
        for cp in local:
            cp.wait()

    outs = pl.pallas_call(
        body, name=name,
        out_shape=(*[pltpu.HBM(a.shape, a.dtype) for a in srcs], *[pltpu.HBM(l.shape, l.dtype) for l in lands]),
        in_specs=[hbm] * (2 * n) + [sem] * 3 + [pl.BlockSpec(memory_space=pl.ANY)] * len(after), out_specs=[hbm] * (2 * n),
        input_output_aliases={i: i for i in range(2 * n)},
        compiler_params=pltpu.CompilerParams(has_side_effects=pltpu.SideEffectType.DATAFLOW_SIDE_EFFECTING),
    )(*srcs, *lands, *sems, *after)
    return outs[n:]


def _sum_slots(name, slots, tm=256):
    n_slots, r, c = slots.shape
    tm = _tile(r, tm)

    def body(*refs):
        acc = refs[0][...]
        for s_ref in refs[1:n_slots]:
            acc = acc + s_ref[...]
        refs[n_slots][...] = acc

    specs = [pl.BlockSpec((None, tm, c), functools.partial(lambda i, s: (s, i, 0), s=s)) for s in range(n_slots)]
    return pl.pallas_call(
        body, name=name, grid=(r // tm,), in_specs=specs, out_specs=pl.BlockSpec((tm, c), lambda i: (i, 0)),
        out_shape=jax.ShapeDtypeStruct((r, c), F32), compiler_params=_params("parallel"),
    )(*[slots] * n_slots)


def _swap_with_sibling(name, arrays):
    n = len(arrays)
    hbm = pl.BlockSpec(memory_space=pl.ANY)

    def body(*refs):
        ins, outs = refs[:n], refs[n:2 * n]
        send_sems, recv_sems = refs[2 * n:]
        sibling = (lax.axis_index("x"), lax.axis_index("y"), 1 - lax.axis_index("c"))
        copies = [pltpu.make_async_remote_copy(src_ref=ins[w], dst_ref=outs[w], send_sem=send_sems.at[w],
                                               recv_sem=recv_sems.at[w], device_id=sibling, device_id_type=MESH)
                  for w in range(n)]
        for cp in copies:
            cp.start()
        for cp in copies:
            cp.wait()

    return pl.pallas_call(
        body, name=name, in_specs=[hbm] * n, out_specs=[hbm] * n,
        out_shape=[jax.ShapeDtypeStruct(a.shape, a.dtype) for a in arrays],
        scratch_shapes=[pltpu.SemaphoreType.DMA((n,)), pltpu.SemaphoreType.DMA((n,))],
    )(*arrays)


def _adamw(g, w, m, v):
    m = ADAM_B1 * m + (1.0 - ADAM_B1) * g
    v = ADAM_B2 * v + (1.0 - ADAM_B2) * jnp.square(g)
    m_hat = m / (1.0 - ADAM_B1 ** ADAM_STEP)
    v_hat = v / (1.0 - ADAM_B2 ** ADAM_STEP)
    delta = -ADAM_LR * (m_hat / (jnp.sqrt(v_hat) + ADAM_EPS) + ADAM_WD * w)
    return delta, m, v


def _adamw_small(grads, ws, ms, vs):
    n = len(ws)
    vmem = pl.BlockSpec(memory_space=pltpu.VMEM)

    def body(*refs):
        for i in range(n):
            g, w, m, v = (refs[k * n + i][...] for k in range(4))
            for k, val in enumerate(_adamw(g, w, m, v)):
                refs[(4 + k) * n + i][...] = val

    outs = pl.pallas_call(
        body, name="adamw_small", in_specs=[vmem] * (4 * n), out_specs=[vmem] * (3 * n),
        out_shape=[jax.ShapeDtypeStruct(w.shape, F32) for _ in range(3) for w in ws],
        compiler_params=pltpu.CompilerParams(vmem_limit_bytes=VMEM_LIMIT),
    )(*grads, *ws, *ms, *vs)
    return outs[:n], outs[n:2 * n], outs[2 * n:]


def kernel(x, norm1_g, w_in, q_norm_g, k_norm_g, ssm_lambda_re, ssm_lambda_im, ssm_log_dt, ssm_b_re, ssm_b_im, ssm_c_re, ssm_c_im, ssm_d, w_glu, b_glu, attn_out_g, ssm_out_g, w_out, norm2_g, w_mlp_in, w_mlp_out, loss_target, m_norm1_g, m_w_in, m_q_norm_g, m_k_norm_g, m_ssm_lambda_re, m_ssm_lambda_im, m_ssm_log_dt, m_ssm_b_re, m_ssm_b_im, m_ssm_c_re, m_ssm_c_im, m_ssm_d, m_w_glu, m_b_glu, m_attn_out_g, m_ssm_out_g, m_w_out, m_norm2_g, m_w_mlp_in, m_w_mlp_out, v_norm1_g, v_w_in, v_q_norm_g, v_k_norm_g, v_ssm_lambda_re, v_ssm_lambda_im, v_ssm_log_dt, v_ssm_b_re, v_ssm_b_im, v_ssm_c_re, v_ssm_c_im, v_ssm_d, v_w_glu, v_b_glu, v_attn_out_g, v_ssm_out_g, v_w_out, v_norm2_g, v_w_mlp_in, v_w_mlp_out):
    batch, seq, d_model = x.shape
    tokens = batch * seq
    sb_width = w_in.shape[1]
    n_features = d_model

    big = [("w_in", w_in, m_w_in, v_w_in, 1), ("w_glu", w_glu, m_w_glu, v_w_glu, 0),
           ("w_out", w_out, m_w_out, v_w_out, 0), ("w_mlp_in", w_mlp_in, m_w_mlp_in, v_w_mlp_in, 1),
           ("w_mlp_out", w_mlp_out, m_w_mlp_out, v_w_mlp_out, 0)]
    small = [("norm1_g", norm1_g, m_norm1_g, v_norm1_g), ("q_norm_g", q_norm_g, m_q_norm_g, v_q_norm_g),
             ("k_norm_g", k_norm_g, m_k_norm_g, v_k_norm_g),
             ("ssm_lambda_re", ssm_lambda_re, m_ssm_lambda_re, v_ssm_lambda_re),
             ("ssm_lambda_im", ssm_lambda_im, m_ssm_lambda_im, v_ssm_lambda_im),
             ("ssm_log_dt", ssm_log_dt, m_ssm_log_dt, v_ssm_log_dt),
             ("ssm_b_re", ssm_b_re, m_ssm_b_re, v_ssm_b_re), ("ssm_b_im", ssm_b_im, m_ssm_b_im, v_ssm_b_im),
             ("ssm_c_re", ssm_c_re, m_ssm_c_re, v_ssm_c_re), ("ssm_c_im", ssm_c_im, m_ssm_c_im, v_ssm_c_im),
             ("ssm_d", ssm_d, m_ssm_d, v_ssm_d), ("b_glu", b_glu, m_b_glu, v_b_glu),
             ("attn_out_g", attn_out_g, m_attn_out_g, v_attn_out_g), ("ssm_out_g", ssm_out_g, m_ssm_out_g, v_ssm_out_g),
             ("norm2_g", norm2_g, m_norm2_g, v_norm2_g)]

    gather_in, tok_in = _chip_exchange_start("gather_w_in_start", "gather", [(w_in.astype(BF16), 1)])
    gather_rest, tok_rest = _chip_exchange_start(
        "gather_rest_start", "gather", [(w.astype(BF16), axis) for _, w, _, _, axis in big[1:]], after=tok_in)

    x2 = x.reshape(tokens, d_model)
    tgt2 = loss_target.reshape(tokens, d_model)
    g1, g2 = norm1_g[None, :], norm2_g[None, :]
    g_attn, g_ssm, bias_glu = attn_out_g[None, :], ssm_out_g[None, :], b_glu[None, :]
    heads = sb_width // HEAD_DIM
    qk_scale = 1.0 / math.sqrt(HEAD_DIM)
    gq, gk = (jnp.tile(q_norm_g, heads) * qk_scale)[None, :], jnp.tile(k_norm_g, heads)[None, :]
    lane_head = jnp.arange(LANES) // HEAD_DIM
    ones_blocks = (lane_head[:, None] == lane_head[None, :]).astype(F32)

    (xn,) = _rowwise("norm1", _rms, [x2], [g1 + tok_rest], [(d_model, BF16)])
    s5_params = (ssm_lambda_re, ssm_lambda_im, ssm_log_dt, ssm_b_re, ssm_b_im, ssm_c_re, ssm_c_im, ssm_d)
    (kt_row, b_mat, c_mat, la, lb), s5_vjp = jax.vjp(_s5_operators, *s5_params)
    (wf_in,) = _chip_exchange_wait("gather_w_in_wait", gather_in, [xn, b_mat, c_mat])
    (proj,) = _mm("proj_in", xn, wf_in, "nn")

    def qkv_fn(q, k, v, gq_, gk_, ones):
        return _head_rms(q, gq_, ones), _head_rms(k, gk_, ones), v

    qn, kn, vb = _rowwise("qk_norm", qkv_fn, [(proj, sb_width, 0), (proj, sb_width, 1), (proj, sb_width, 2)],
                          [gq, gk, ones_blocks], [(sb_width, BF16)] * 3)
    sb, c_tot = _attn_fwd(qn, kn, vb, batch=batch, seq=seq, bq=ATTN_BQ, bk=ATTN_BK)
    ug = _tokens_to_groups("u_to_groups", proj, 3, sb_width)
    yg, xin = _s5_fwd(ug, kt_row, b_mat, c_mat, la, lb, batch=batch)
    y_ssm = _groups_to_tokens("y_to_tokens", yg)

    wf_glu, wf_out, wf_mlp_in, wf_mlp_out = _chip_exchange_wait("gather_rest_wait", gather_rest, y_ssm)
    (gate_pre,) = _mm("glu_gate", y_ssm, wf_glu, "nn", a_fn=_gelu, extras=[(bias_glu, "row")],
                      epilogue=lambda acc, b: acc + b)
    (mixed,) = _rowwise("mix_norm", _mixed, [sb, y_ssm, gate_pre], [g_attn, g_ssm], [(2 * sb_width, BF16)])
    def out_head(acc, r, g):
        h = acc + r
        return h, _rms(h, g)

    h1, hn = _mm("proj_out", mixed, wf_out, "nn", extras=[(x2, "tile"), (g2, "row")], epilogue=out_head,
                 out_dtypes=(F32, BF16), full_rows=True)
    def mlp_act(acc):
        r = jnp.maximum(acc, 0.0)
        return r * r, r

    act, act_root = _mm("mlp_in", hn, wf_mlp_in, "nn", epilogue=mlp_act, out_dtypes=(BF16, BF16))
    inv_n = 1.0 / n_features

    def loss_head(acc, r, t):
        d = ((acc + r) - t) * inv_n
        return d, d, jnp.sum(d * d, keepdims=True) * (0.5 * n_features)

    dy, dy_b, loss_tiles = _mm("mlp_out_loss", act, wf_mlp_out, "nn", extras=[(h1, "tile"), (tgt2, "tile")],
                               epilogue=loss_head, out_dtypes=(F32, BF16), tile_sums=("scalar",))
    loss_part = jnp.sum(loss_tiles)

    (dw_mlp_out,) = _mm("dw_mlp_out", act, dy_b, "tn")
    (dpre,) = _mm("d_mlp_act", dy_b, wf_mlp_out, "nt", extras=[(act_root, "tile")],
                  epilogue=lambda acc, r: acc * (2.0 * r.astype(F32)), out_dtypes=(BF16,))
    (dw_mlp_in,) = _mm("dw_mlp_in", hn, dpre, "tn")
    scatter_mlp, tok_mlp = _chip_exchange_start("scatter_mlp_start", "scatter", [(dw_mlp_in, 1), (dw_mlp_out, 0)])
    def norm_bwd(dn, res, hx, g):
        _, vjp = jax.vjp(_rms, hx, g)
        dh, dg = vjp(dn)
        return res + dh, dg

    dh1, dg_tiles = _mm("d_norm2_in", dpre, wf_mlp_in, "nt", extras=[(dy, "tile"), (h1, "tile"), (g2 + tok_mlp, "row")],
                        epilogue=norm_bwd, tile_sums=("row",), full_rows=True)
    dg_norm2 = jnp.sum(dg_tiles, axis=0, keepdims=True)
    (dmixed,) = _mm("d_mixed", dh1, wf_out, "nt")
    (dw_out,) = _mm("dw_out", mixed, dh1, "tn")

    def mixed_bwd(dm, sb_, ys, gp, ga, gs):
        _, vjp = jax.vjp(lambda a, act, b, c, d: jnp.concatenate(
            [_rms(a, c), _rms(act * jax.nn.sigmoid(b), d)], axis=-1), sb_, _gelu(ys), gp, ga, gs)
        dsb_, dact, dgp_, dga, dgs = vjp(dm)
        return dsb_, dgp_, dact, dga, dgs, jnp.sum(dgp_, axis=0, keepdims=True)

    dsb, dgate_pre, dact_part, dg_attn, dg_ssm, db_glu = _rowwise(
        "mix_norm_bwd", mixed_bwd, [dmixed, sb, y_ssm, gate_pre], [g_attn, g_ssm],
        [(sb_width, F32), (sb_width, BF16), (sb_width, F32)], [(1, sb_width)] * 3)

    def gelu_bwd(acc, part, ys):
        _, vjp = jax.vjp(_gelu, ys)
        return vjp(acc + part)[0]

    (dy_ssm,) = _mm("d_glu_in", dgate_pre, wf_glu, "nt", extras=[(dact_part, "tile"), (y_ssm, "tile")], epilogue=gelu_bwd)
    (dw_glu,) = _mm("dw_glu", y_ssm, dgate_pre, "tn", a_fn=_gelu)
    scatter_mix, tok_mix = _chip_exchange_start("scatter_mix_start", "scatter", [(dw_glu, 0), (dw_out, 0)])

    dug, dkt_row, db_mat, dc_mat, dla, dlb = _s5_bwd(ug, _tokens_to_groups("dy_to_groups", dy_ssm, 0, sb_width), xin,
                                                     kt_row, b_mat, c_mat, la, lb + tok_mix, batch=batch)
    du = _groups_to_tokens("du_to_tokens", dug)
    ds5 = s5_vjp((dkt_row, db_mat, dc_mat, dla, dlb))

    dqn, dkn, dv = _attn_bwd(qn, kn, vb, c_tot, dsb, batch=batch, seq=seq, bq=ATTN_BQ, bk=ATTN_BK)

    def qk_bwd(q, k, dq_, dk_, gq_, gk_, ones):
        _, vjp_q = jax.vjp(lambda a, g: _head_rms(a, g, ones), q, gq_)
        _, vjp_k = jax.vjp(lambda a, g: _head_rms(a, g, ones), k, gk_)
        dq, dgq = vjp_q(dq_)
        dk, dgk = vjp_k(dk_)
        return dq, dk, dgq, dgk

    dq, dk, dgq, dgk = _rowwise("qk_norm_bwd", qk_bwd, [(proj, sb_width, 0), (proj, sb_width, 1), dqn, dkn],
                                [gq, gk, ones_blocks], [(sb_width, BF16)] * 2, [(1, sb_width)] * 2)
    dproj = jnp.concatenate([dq, dk, dv.astype(BF16), du.astype(BF16)], axis=1)
    (dw_in,) = _mm("dw_in", xn, dproj, "tn")
    scatter_in, tok_w_in = _chip_exchange_start("scatter_in_start", "scatter", [(dw_in, 1)])
    dx, dg_tiles = _mm("d_norm1_in", dproj, wf_in, "nt", extras=[(dh1, "tile"), (x2, "tile"), (g1 + tok_w_in, "row")],
                       epilogue=norm_bwd, tile_sums=("row",), full_rows=True)
    dg_norm1 = jnp.sum(dg_tiles, axis=0, keepdims=True)

    small_grads = [dg_norm1[0], dgq.reshape(heads, HEAD_DIM).sum(0) * qk_scale, dgk.reshape(heads, HEAD_DIM).sum(0), *ds5,
                   db_glu[0], dg_attn[0], dg_ssm[0], dg_norm2[0]]
    order = ["norm1_g", "q_norm_g", "k_norm_g", "ssm_lambda_re", "ssm_lambda_im", "ssm_log_dt", "ssm_b_re", "ssm_b_im",
             "ssm_c_re", "ssm_c_im", "ssm_d", "b_glu", "attn_out_g", "ssm_out_g", "norm2_g"]
    assert order == [name for name, *_ in small]

    def pack(parts, extra=None):
        flat = [p.reshape(-1) for p in parts] + ([extra.reshape(-1)] if extra is not None else [])
        flat = jnp.concatenate(flat)
        rows = -(-flat.shape[0] // (LANES * LANES)) * LANES
        return jnp.pad(flat, (0, rows * LANES - flat.shape[0])).reshape(rows, LANES)

    n_small = sum(w.size for _, w, _, _ in small)
    small_exchange, _ = _chip_exchange_start("small_grads_start", "all", [(pack(small_grads, loss_part), 0)])

    def adam_big(sa, sb_, w, m, v):
        g = sa + sb_
        delta, m, v = _adamw(g, w, m, v)
        return g, delta, m, v

    def reduce_and_update(tag, params, slots):
        mine = [_sum_slots("sum_" + name, s) for s, (name, *_rest) in zip(slots, params)]
        theirs = _swap_with_sibling("swap_" + tag, mine)
        return {name: _rowwise("adamw_" + name, adam_big, [sa, sb_, w, m, v], [], [(w.shape[1], F32)] * 4)
                for (name, w, m, v, _), sa, sb_ in zip(params, mine, theirs)}

    started = small_exchange[3][0]
    slots_mlp_in, slots_mlp_out = _chip_exchange_wait("scatter_mlp_wait", scatter_mlp, started)
    slots_glu, slots_out = _chip_exchange_wait("scatter_mix_wait", scatter_mix, started)
    big_out = reduce_and_update("rest", big[1:], [slots_glu, slots_out, slots_mlp_in, slots_mlp_out])

    (small_slots,) = _chip_exchange_wait("small_grads_wait", small_exchange, big_out["w_mlp_out"][3])
    reduced = _sum_slots("sum_small", small_slots)
    loss = reduced.reshape(-1)[n_small]
    flat, small_g, off = reduced.reshape(-1), {}, 0
    for name, w, _, _ in small:
        small_g[name] = flat[off:off + w.size].reshape(w.shape)
        off += w.size
    small_upd = _adamw_small([small_g[name] for name, *_ in small], [w for _, w, _, _ in small],
                             [m for _, _, m, _ in small], [v for _, _, _, v in small])
    small_out = [small_g] + [{name: small_upd[kind][i] for i, (name, *_) in enumerate(small)} for kind in range(3)]

    (slots_in,) = _chip_exchange_wait("scatter_in_wait", scatter_in, reduced)
    big_out.update(reduce_and_update("w_in", big[:1], [slots_in]))
    names = ["norm1_g", "w_in", "q_norm_g", "k_norm_g", "ssm_lambda_re", "ssm_lambda_im", "ssm_log_dt", "ssm_b_re",
             "ssm_b_im", "ssm_c_re", "ssm_c_im", "ssm_d", "w_glu", "b_glu", "attn_out_g", "ssm_out_g", "w_out",
             "norm2_g", "w_mlp_in", "w_mlp_out"]
    outs = [loss, dx.reshape(batch, seq, d_model)]
    for kind in range(4):
        for name in names:
            outs.append(big_out[name][kind] if name in big_out else small_out[kind][name])
    return tuple(outs)
```

```python
import functools
import math

import jax
import jax.numpy as jnp
from jax import lax
from jax.experimental import pallas as pl
from jax.experimental.pallas import tpu as pltpu

F32 = jnp.float32
BF16 = jnp.bfloat16
F32_DOT = lax.Precision.HIGH
MESH = pl.DeviceIdType.MESH

RMS_EPS = 1e-6
HEAD_DIM = 64
SSM_GROUP = 16
SSM_CHUNK = 16
LANES = 128
N_CHIPS = 4
N_DEV = 8
VMEM_LIMIT = 48 * 1024 * 1024

ADAM_LR = 0.001
ADAM_B1 = 0.9
ADAM_B2 = 0.999
ADAM_EPS = 1e-08
ADAM_WD = 0.01
ADAM_STEP = 10


def _tile(n, pref):
    t = min(n, pref)
    while n % t:
        t //= 2
    return t


def _params(*sem):
    return pltpu.CompilerParams(dimension_semantics=sem, vmem_limit_bytes=VMEM_LIMIT)


_DIMS = {"nn": (((1,), (0,)), ((), ())), "nt": (((1,), (1,)), ((), ())), "tn": (((0,), (0,)), ((), ()))}


MM_VMEM_BUDGET = 40 * 1024 * 1024


def _mm_tiles(m, n, k, a_bytes, b_bytes, tile_bytes, full_rows=False):
    best = None
    for tk in [t for t in (k, k // 2, k // 4, k // 8) if t >= 256 or t == k]:
        for tm in [t for t in (1024, 512, 256, 128) if t <= m and m % t == 0]:
            for tn in [n] if full_rows else [t for t in (1024, 512, 256, 128) if t <= n and n % t == 0]:
                need = 2 * (tm * tk * a_bytes + tk * tn * b_bytes) + 2 * tm * tn * tile_bytes + (tm * tn * 4 if tk < k else 0)
                if need > MM_VMEM_BUDGET:
                    continue
                traffic = m * k * a_bytes * (n // tn) + k * n * b_bytes * (m // tm)
                key = (tk < k, traffic, -tm * tn)
                if best is None or key < best[0]:
                    best = (key, (tm, tn, tk))
    return best[1]


def _mm(name, a, b, mode, *, a_fn=None, extras=(), epilogue=None, out_dtypes=(F32,), tile_sums=(), full_rows=False):
    if mode == "nn":
        (m, k), n = a.shape, b.shape[1]
    elif mode == "nt":
        (m, k), n = a.shape, b.shape[0]
    else:
        (k, m), n = a.shape, b.shape[1]
    tile_bytes = sum(e.dtype.itemsize for e, kind in extras if kind == "tile") + sum(jnp.dtype(d).itemsize for d in out_dtypes)
    tm, tn, tk = _mm_tiles(m, n, k, a.dtype.itemsize, b.dtype.itemsize, tile_bytes, full_rows)
    nk = k // tk
    ne, nout = len(extras), len(out_dtypes)
    dims = _DIMS[mode]

    def body(a_ref, b_ref, *rest):
        ex, outs, sums = rest[:ne], rest[ne:ne + nout], rest[ne + nout:ne + nout + len(tile_sums)]
        at = a_ref[...]
        if a_fn is not None:
            at = a_fn(at)
        p = lax.dot_general(at.astype(BF16), b_ref[...].astype(BF16), dims, preferred_element_type=F32)

        def finish(r):
            if epilogue is not None:
                r = epilogue(r, *[e[...] for e in ex])
            if not isinstance(r, (tuple, list)):
                r = (r,)
            for o, v in zip(outs, r[:nout]):
                o[...] = v.astype(o.dtype)
            for o, v, kind in zip(sums, r[nout:], tile_sums):
                first = lax.broadcasted_iota(jnp.int32, o.shape, 0) == 0
                if kind == "scalar":
                    first &= lax.broadcasted_iota(jnp.int32, o.shape, 1) == 0
                o[...] = jnp.where(first, v, 0.0)

        if nk == 1:
            finish(p)
        else:
            acc = rest[ne + nout + len(tile_sums)]
            kk = pl.program_id(2)

            @pl.when(kk == 0)
            def _():
                acc[...] = p

            @pl.when(kk > 0)
            def _():
                acc[...] += p

            @pl.when(kk == nk - 1)
            def _():
                finish(acc[...])

    if mode == "tn":
        a_spec = pl.BlockSpec((tk, tm), lambda i, j, kk: (kk, i))
    else:
        a_spec = pl.BlockSpec((tm, tk), lambda i, j, kk: (i, kk))
    if mode == "nt":
        b_spec = pl.BlockSpec((tn, tk), lambda i, j, kk: (j, kk))
    else:
        b_spec = pl.BlockSpec((tk, tn), lambda i, j, kk: (kk, j))
    ex_specs = []
    for _, kind in extras:
        if kind == "tile":
            ex_specs.append(pl.BlockSpec((tm, tn), lambda i, j, kk: (i, j)))
        else:
            ex_specs.append(pl.BlockSpec((1, tn), lambda i, j, kk: (0, j)))
    return pl.pallas_call(
        body, name=name, grid=(m // tm, n // tn, nk),
        in_specs=[a_spec, b_spec] + ex_specs,
        out_specs=([pl.BlockSpec((tm, tn), lambda i, j, kk: (i, j)) for _ in out_dtypes]
                   + [pl.BlockSpec((8, LANES if kind == "scalar" else tn), lambda i, j, kk: (i, j)) for kind in tile_sums]),
        out_shape=([jax.ShapeDtypeStruct((m, n), dt) for dt in out_dtypes]
                   + [jax.ShapeDtypeStruct((m // tm * 8, n // tn * LANES if kind == "scalar" else n), F32)
                      for kind in tile_sums]),
        scratch_shapes=[pltpu.VMEM((tm, tn), F32)] if nk > 1 else [],
        compiler_params=_params("parallel", "parallel", "arbitrary"),
    )(a, b, *[e for e, _ in extras])


def _rowwise(name, fn, rows, consts, row_outs, acc_outs=(), tm=256):
    norm = [r if isinstance(r, tuple) else (r, r.shape[1], 0) for r in rows]
    t = norm[0][0].shape[0]
    tm = _tile(t, tm)
    nr, nc, no = len(norm), len(consts), len(row_outs)

    def body(*refs):
        outs = fn(*[r[...] for r in refs[:nr + nc]])
        if not isinstance(outs, (tuple, list)):
            outs = (outs,)
        o_refs, a_refs = refs[nr + nc:nr + nc + no], refs[nr + nc + no:]
        for r, v in zip(o_refs, outs[:no]):
            r[...] = v.astype(r.dtype)
        if a_refs:
            i = pl.program_id(0)

            @pl.when(i == 0)
            def _():
                for r, v in zip(a_refs, outs[no:]):
                    r[...] = v

            @pl.when(i > 0)
            def _():
                for r, v in zip(a_refs, outs[no:]):
                    r[...] += v

    in_specs = [pl.BlockSpec((tm, w), functools.partial(lambda i, cb: (i, cb), cb=cb)) for _, w, cb in norm]
    in_specs += [pl.BlockSpec(c.shape, functools.partial(lambda i, nd: (0,) * nd, nd=c.ndim)) for c in consts]
    out_specs = [pl.BlockSpec((tm, w), lambda i: (i, 0)) for w, _ in row_outs]
    out_specs += [pl.BlockSpec(s, functools.partial(lambda i, nd: (0,) * nd, nd=len(s))) for s in acc_outs]
    out_shape = [jax.ShapeDtypeStruct((t, w), dt) for w, dt in row_outs]
    out_shape += [jax.ShapeDtypeStruct(s, F32) for s in acc_outs]
    return pl.pallas_call(
        body, name=name, grid=(t // tm,), in_specs=in_specs, out_specs=out_specs, out_shape=out_shape,
        compiler_params=_params("arbitrary"),
    )(*[r[0] for r in norm], *consts)


def _rms(x, g):
    return x * lax.rsqrt(jnp.mean(x * x, axis=-1, keepdims=True) + RMS_EPS) * g


@jax.custom_vjp
def _head_sums(x, ones_blocks):
    parts = [jnp.dot(x[:, j:j + LANES], ones_blocks, precision=F32_DOT, preferred_element_type=F32)
             for j in range(0, x.shape[1], LANES)]
    return jnp.concatenate(parts, axis=1)


_head_sums.defvjp(lambda x, ones_blocks: (_head_sums(x, ones_blocks), ones_blocks),
                  lambda ones_blocks, ct: (_head_sums(ct, ones_blocks), None))


def _head_rms(x, g, ones_blocks):
    return x * lax.rsqrt(_head_sums(x * x, ones_blocks) * (1.0 / HEAD_DIM) + RMS_EPS) * g


def _gelu(x):
    return x * (0.5 * (1.0 + jnp.tanh(math.sqrt(2.0 / math.pi) * (x + 0.044715 * (x * x * x)))))


def _mixed(sb, y_ssm, gate_pre, g_attn, g_ssm):
    ssm = _gelu(y_ssm) * jax.nn.sigmoid(gate_pre)
    return jnp.concatenate([_rms(sb, g_attn), _rms(ssm, g_ssm)], axis=-1)


def _softplus(z):
    return jnp.maximum(z, 0.0) + jnp.log(1.0 + jnp.exp(-jnp.abs(z)))


def _running_sums(x, tri):
    return jnp.dot(x.astype(BF16), tri, preferred_element_type=F32)


def _dot_nt(a, b, **kw):
    return lax.dot_general(a, b, _DIMS["nt"], preferred_element_type=F32, **kw)


def _dot_tn(a, b, **kw):
    return lax.dot_general(a, b, _DIMS["tn"], preferred_element_type=F32, **kw)


ATTN_BQ, ATTN_BK = 2048, 256
HEAD_LANES = tuple(slice(h * HEAD_DIM, (h + 1) * HEAD_DIM) for h in range(LANES // HEAD_DIM))


def _attn_fwd(qs, kn, v, *, batch, seq, bq, bk):
    width = qs.shape[1]
    bq = _tile(seq, bq)
    bk = _tile(bq, bk)
    nq, kpq = seq // bq, bq // bk

    def body(q_ref, k_ref, v_ref, o_ref, c_ref):
        row = lax.broadcasted_iota(jnp.int32, (bq, bk), 0)
        col = lax.broadcasted_iota(jnp.int32, (bq, bk), 1)
        tri = (lax.broadcasted_iota(jnp.int32, (bk, bk), 0) >= lax.broadcasted_iota(jnp.int32, (bk, bk), 1)).astype(BF16)

        def q_block(qi, carry):
            r0 = pl.multiple_of(qi * bq, bq)
            qh = [q_ref[pl.ds(r0, bq), ln] for ln in HEAD_LANES]

            def tile(k0, state, top=0):
                diag = top is not None
                top = top or 0
                msk = (col < row)[:bq - top] if diag else None
                new = []
                for h, ln in enumerate(HEAD_LANES):
                    o, c = state[2 * h], state[2 * h + 1]
                    z = _dot_nt(qh[h][top:], k_ref[pl.ds(k0, bk), ln])
                    sp = _softplus(z)
                    if diag:
                        sp = jnp.where(msk, sp, 0.0)
                    r = _running_sums(sp, tri)
                    a = jnp.exp(z - r - c[top:])
                    if diag:
                        a = jnp.where(msk, a, 0.0)
                    o_new = o[top:] + jnp.dot(a.astype(BF16), v_ref[pl.ds(k0, bk), ln], preferred_element_type=F32)
                    c_new = c[top:] + r[:, 0:1]
                    if top:
                        o_new, c_new = jnp.concatenate([o[:top], o_new]), jnp.concatenate([c[:top], c_new])
                    new += [o_new, c_new]
                return tuple(new)

            state = (jnp.zeros((bq, HEAD_DIM), F32), jnp.zeros((bq, 1), F32)) * len(HEAD_LANES)
            for d in reversed(range(kpq)):
                state = tile(pl.multiple_of(r0 + d * bk, bk), state, top=d * bk)
            state = lax.fori_loop(0, qi * kpq, lambda it, st: tile(pl.multiple_of(r0 - (it + 1) * bk, bk), st, None),
                                  state)
            for h, ln in enumerate(HEAD_LANES):
                o_ref[pl.ds(r0, bq), ln] = state[2 * h]
                c_ref[pl.ds(r0, bq), ln] = jnp.broadcast_to(state[2 * h + 1], (bq, HEAD_DIM))
            return carry

        lax.fori_loop(0, nq, q_block, 0)

    spec = pl.BlockSpec((seq, LANES), lambda b, h: (b, h))
    shape = jax.ShapeDtypeStruct((batch * seq, width), F32)
    return pl.pallas_call(
        body, name="attn_fwd", grid=(batch, width // LANES), in_specs=[spec, spec, spec], out_specs=[spec, spec],
        out_shape=[shape, shape], compiler_params=_params("parallel", "parallel"),
    )(qs, kn, v)


def _attn_bwd(qs, kn, v, c_tot, do, *, batch, seq, bq, bk, after):
    width = qs.shape[1]
    bq = _tile(seq, bq)
    bk = _tile(bq, bk)
    nq, kpq = seq // bq, bq // bk

    def body(q_ref, k_ref, v_ref, c_ref, do_ref, after_ref, dq_ref, dk_ref, dv_ref):
        row = lax.broadcasted_iota(jnp.int32, (bq, bk), 0)
        col = lax.broadcasted_iota(jnp.int32, (bq, bk), 1)
        sq_row = lax.broadcasted_iota(jnp.int32, (bk, bk), 0)
        sq_col = lax.broadcasted_iota(jnp.int32, (bk, bk), 1)
        tri = (sq_row >= sq_col).astype(BF16)
        tri_t = (sq_row <= sq_col).astype(BF16)
        dk_ref[...] = jnp.zeros_like(dk_ref)
        dv_ref[...] = jnp.zeros_like(dv_ref)

        def q_block(qi, carry):
            r0 = pl.multiple_of(qi * bq, bq)
            qh = [q_ref[pl.ds(r0, bq), ln] for ln in HEAD_LANES]
            d_out = [do_ref[pl.ds(r0, bq), ln].astype(BF16) for ln in HEAD_LANES]
            c_all = [c_ref[pl.ds(r0, bq), ln][:, 0:1] for ln in HEAD_LANES]

            def tile(k0, state, top=0):
                diag = top is not None
                top = top or 0
                last = diag and top == bq - bk
                msk = (col < row)[:bq - top] if diag else None
                new = []
                for h, ln in enumerate(HEAD_LANES):
                    c_left, g_left, dq = state[3 * h:3 * h + 3]
                    q, d_o = qh[h][top:], d_out[h][top:]
                    k = k_ref[pl.ds(k0, bk), ln]
                    z = _dot_nt(q, k)
                    e = jnp.exp(-jnp.abs(z))
                    sp = jnp.maximum(z, 0.0) + jnp.log(1.0 + e)
                    sig = jnp.exp(z - sp)
                    if diag:
                        sp = jnp.where(msk, sp, 0.0)
                    r = _running_sums(sp, tri)
                    c_new = c_left[top:] + r[:, 0:1]
                    a = jnp.exp(z - r - (0.0 if last else c_all[h][top:] - c_new))
                    if diag:
                        a = jnp.where(msk, a, 0.0)
                    g = a * _dot_nt(d_o, v_ref[pl.ds(k0, bk), ln])
                    pg = _running_sums(g, tri_t)
                    dz = g - sig * (g_left[top:] + pg)
                    if diag:
                        dz = jnp.where(msk, dz, 0.0)
                    dz = dz.astype(BF16)
                    dk_ref[pl.ds(k0, bk), ln] += _dot_tn(dz, q)
                    dv_ref[pl.ds(k0, bk), ln] += _dot_tn(a.astype(BF16), d_o)
                    g_new = g_left[top:] + pg[:, bk - 1:bk]
                    dq_new = dq[top:] + jnp.dot(dz, k, preferred_element_type=F32)
                    if top:
                        c_new = jnp.concatenate([c_left[:top], c_new])
                        g_new = jnp.concatenate([g_left[:top], g_new])
                        dq_new = jnp.concatenate([dq[:top], dq_new])
                    new += [c_new, g_new, dq_new]
                return tuple(new)

            zero = jnp.zeros((bq, 1), F32)
            init = (zero, zero, jnp.zeros((bq, HEAD_DIM), F32)) * len(HEAD_LANES)
            state = lax.fori_loop(0, qi * kpq, lambda it, st: tile(pl.multiple_of(it * bk, bk), st, None), init)
            for d in range(kpq):
                state = tile(pl.multiple_of(r0 + d * bk, bk), state, top=d * bk)
            for h, ln in enumerate(HEAD_LANES):
                dq_ref[pl.ds(r0, bq), ln] = state[3 * h + 2]
            return carry

        lax.fori_loop(0, nq, q_block, 0)

    spec = pl.BlockSpec((seq, LANES), lambda b, h: (b, h))
    shape = jax.ShapeDtypeStruct((batch * seq, width), F32)
    return pl.pallas_call(
        body, name="attn_bwd", grid=(batch, width // LANES),
        in_specs=[spec] * 5 + [pl.BlockSpec(memory_space=pl.ANY)], out_specs=[spec] * 3,
        out_shape=[shape] * 3, compiler_params=_params("parallel", "parallel"),
    )(qs, kn, v, c_tot, do, after)


def _s5_operators(lam_re, lam_im, log_dt, b_re, b_im, c_re, c_im, d_skip):
    groups, n_state, n_ch = b_re.shape
    cs = SSM_CHUNK
    dt = jnp.exp(log_dt)[:, None]
    steps = jnp.arange(cs + 1, dtype=F32)[None, :, None]
    mag = jnp.exp(steps * (lam_re * dt)[:, None, :])
    ang = steps * (lam_im * dt)[:, None, :]
    pw_re, pw_im = mag * jnp.cos(ang), mag * jnp.sin(ang)
    num_re, num_im = pw_re[:, 1] - 1.0, pw_im[:, 1]
    den = lam_re * lam_re + lam_im * lam_im
    cf_re = (num_re * lam_re + num_im * lam_im) / den
    cf_im = (num_im * lam_re - num_re * lam_im) / den
    bb_re = cf_re[:, :, None] * b_re - cf_im[:, :, None] * b_im
    bb_im = cf_re[:, :, None] * b_im + cf_im[:, :, None] * b_re
    width = cs * n_ch
    ct_re, ct_im = c_re.transpose(0, 2, 1), c_im.transpose(0, 2, 1)

    def c_times_powers(first):
        pr = pw_re[:, first:first + cs].transpose(0, 2, 1)[:, :, :, None]
        pi = pw_im[:, first:first + cs].transpose(0, 2, 1)[:, :, :, None]
        re = pr * ct_re[:, :, None, :] - pi * ct_im[:, :, None, :]
        im = pr * ct_im[:, :, None, :] + pi * ct_re[:, :, None, :]
        return re.reshape(groups, n_state, width), im.reshape(groups, n_state, width)

    w_re, w_im = c_times_powers(0)
    kt_row = (jnp.einsum("gpi,gpw->giw", bb_re, w_re, precision=F32_DOT)
              - jnp.einsum("gpi,gpw->giw", bb_im, w_im, precision=F32_DOT))
    kt_row = kt_row + jnp.pad(jnp.eye(n_ch, dtype=F32)[None] * d_skip[:, None, :], ((0, 0), (0, 0), (0, width - n_ch)))
    rp_re, rp_im = pw_re[:, cs - 1::-1][:, :cs], pw_im[:, cs - 1::-1][:, :cs]
    bm_re = rp_re[:, :, None, :] * bb_re.transpose(0, 2, 1)[:, None] - rp_im[:, :, None, :] * bb_im.transpose(0, 2, 1)[:, None]
    bm_im = rp_re[:, :, None, :] * bb_im.transpose(0, 2, 1)[:, None] + rp_im[:, :, None, :] * bb_re.transpose(0, 2, 1)[:, None]
    b_mat = jnp.concatenate([bm_re, bm_im], axis=-1).reshape(groups, width, 2 * n_state)
    w1_re, w1_im = c_times_powers(1)
    c_mat = jnp.concatenate([w1_re, -w1_im], axis=1)
    la = jnp.concatenate([pw_re[:, cs], pw_re[:, cs]], axis=-1)[:, None, :]
    lb = jnp.concatenate([-pw_im[:, cs], pw_im[:, cs]], axis=-1)[:, None, :]
    return kt_row, b_mat, c_mat, la, lb


GROUPS_PER_BLOCK = LANES // SSM_GROUP


def _tokens_to_groups(name, u, col_block, width):
    t = u.shape[0]
    n = t // SSM_CHUNK
    ch = SSM_CHUNK * SSM_GROUP
    blocks = width // LANES

    def body(u_ref, o_ref):
        for s in range(SSM_CHUNK):
            rows = u_ref[pl.ds(s, n, stride=SSM_CHUNK), :]
            for g in range(GROUPS_PER_BLOCK):
                o_ref[g, :, s * SSM_GROUP:(s + 1) * SSM_GROUP] = rows[:, g * SSM_GROUP:(g + 1) * SSM_GROUP]

    return pl.pallas_call(
        body, name=name, grid=(blocks,),
        in_specs=[pl.BlockSpec((t, LANES), lambda j: (0, col_block * blocks + j))],
        out_specs=pl.BlockSpec((GROUPS_PER_BLOCK, n, ch), lambda j: (j, 0, 0)),
        out_shape=jax.ShapeDtypeStruct((width // SSM_GROUP, n, ch), F32), compiler_params=_params("parallel"),
    )(u)


def _groups_to_tokens(name, ug):
    groups, n, ch = ug.shape

    def body(g_ref, o_ref, rows_ref):
        for s in range(SSM_CHUNK):
            for g in range(GROUPS_PER_BLOCK):
                rows_ref[s % 2, :, g * SSM_GROUP:(g + 1) * SSM_GROUP] = g_ref[g, :, s * SSM_GROUP:(s + 1) * SSM_GROUP]
            o_ref[pl.ds(s, n, stride=SSM_CHUNK), :] = rows_ref[s % 2]

    return pl.pallas_call(
        body, name=name, grid=(groups // GROUPS_PER_BLOCK,),
        in_specs=[pl.BlockSpec((GROUPS_PER_BLOCK, n, ch), lambda j: (j, 0, 0))],
        out_specs=pl.BlockSpec((n * SSM_CHUNK, LANES), lambda j: (0, j)),
        out_shape=jax.ShapeDtypeStruct((n * SSM_CHUNK, groups * SSM_GROUP), F32),
        scratch_shapes=[pltpu.VMEM((2, n, LANES), F32)], compiler_params=_params("parallel"),
    )(ug)


SCAN_ROWS = 8


def _toeplitz_to(tm_ref, g, kt_row):
    width = kt_row.shape[1]
    tm_ref[g] = jnp.zeros((width, width), F32)
    for s in range(SSM_CHUNK):
        tm_ref[g, s * SSM_GROUP:(s + 1) * SSM_GROUP, s * SSM_GROUP:] = kt_row[:, :width - s * SSM_GROUP]


def _lam_powers(la, lb, reverse):
    if reverse:
        lb = -lb

    def mul(p, q):
        return p[0] * q[0] - p[1] * q[1], p[0] * q[1] + p[1] * q[0]

    p1 = (la, lb)
    p2 = mul(p1, p1)
    p3 = mul(p2, p1)
    p4 = mul(p2, p2)
    rows = [p1, p2, p3, p4, mul(p4, p1), mul(p4, p2), mul(p4, p3), mul(p4, p4)]
    if reverse:
        rows = rows[::-1]
    idx = lax.broadcasted_iota(jnp.int32, (SCAN_ROWS, la.shape[1]), 0)
    tab_a = sum(jnp.where(idx == j, r[0], 0.0) for j, r in enumerate(rows))
    tab_b = sum(jnp.where(idx == j, r[1], 0.0) for j, r in enumerate(rows))
    return (p1, p2, p4), (tab_a, tab_b), idx


def _scan_block(e, carry, steps, table, idx, half, reverse):
    n = SCAN_ROWS
    for d, (pa, pb) in zip((1, 2, 4), steps):
        sh = pltpu.roll(e, n - d if reverse else d, 0)
        sh = jnp.where(idx < n - d if reverse else idx >= d, sh, 0.0)
        e = e + pa * sh + pb * pltpu.roll(sh, half, 1)
    tab_a, tab_b = table
    e = e + tab_a * carry + tab_b * pltpu.roll(carry, half, 1)
    shifted = jnp.where(idx == (n - 1 if reverse else 0), carry, pltpu.roll(e, n - 1 if reverse else 1, 0))
    edge = e[0:1] if reverse else e[n - 1:n]
    return shifted, jnp.broadcast_to(edge, e.shape)


def _s5_fwd(ug, kt_row, b_mat, c_mat, la, lb, *, batch, gb=8):
    groups, n, ch = ug.shape
    p2 = b_mat.shape[2]
    gb = _tile(groups, gb)
    nch = n // batch
    nblk = nch // SCAN_ROWS

    def body(u_ref, k_ref, b_ref, c_ref, la_ref, lb_ref, y_ref, x_ref, s_ref, tm_ref):
        for g in range(gb):
            _toeplitz_to(tm_ref, g, k_ref[g])
            s_ref[g] = jnp.dot(u_ref[g], b_ref[g], precision=F32_DOT, preferred_element_type=F32)
        powers = [_lam_powers(la_ref[g], lb_ref[g], False) for g in range(gb)]

        def step(blk, carries):
            new = []
            for g in range(gb):
                steps, table, idx = powers[g]
                for b in range(batch):
                    rows = pl.ds(pl.multiple_of(b * nch + blk * SCAN_ROWS, SCAN_ROWS), SCAN_ROWS)
                    x_in, carry = _scan_block(s_ref[g, rows, :], carries[g * batch + b], steps, table, idx, p2 // 2, False)
                    x_ref[g, rows, :] = x_in
                    new.append(carry)
            return tuple(new)

        lax.fori_loop(0, nblk, step, tuple(jnp.zeros((SCAN_ROWS, p2), F32) for _ in range(gb * batch)))
        for g in range(gb):
            y_ref[g] = (jnp.dot(u_ref[g], tm_ref[g], precision=F32_DOT, preferred_element_type=F32)
                        + jnp.dot(x_ref[g], c_ref[g], precision=F32_DOT, preferred_element_type=F32))

    def spec(a, b):
        return pl.BlockSpec((gb, a, b), lambda i: (i, 0, 0))

    return pl.pallas_call(
        body, name="s5_fwd", grid=(groups // gb,),
        in_specs=[spec(n, ch), spec(SSM_GROUP, ch), spec(ch, p2), spec(p2, ch), spec(1, p2), spec(1, p2)],
        out_specs=[spec(n, ch), spec(n, p2)],
        out_shape=[jax.ShapeDtypeStruct((groups, n, ch), F32), jax.ShapeDtypeStruct((groups, n, p2), F32)],
        scratch_shapes=[pltpu.VMEM((gb, n, p2), F32), pltpu.VMEM((gb, ch, ch), F32)],
        compiler_params=_params("parallel"),
    )(ug, kt_row, b_mat, c_mat, la, lb)


def _s5_bwd(ug, dyg, xin, kt_row, b_mat, c_mat, la, lb, *, batch, gb=8):
    groups, n, ch = ug.shape
    p2 = b_mat.shape[2]
    gb = _tile(groups, gb)
    nch = n // batch
    nblk = nch // SCAN_ROWS

    def body(u_ref, dy_ref, x_ref, k_ref, b_ref, c_ref, la_ref, lb_ref,
             du_ref, dk_ref, db_ref, dc_ref, dla_ref, dlb_ref, dx_ref, ds_ref, tm_ref):
        for g in range(gb):
            _toeplitz_to(tm_ref, g, k_ref[g])
            dx_ref[g] = _dot_nt(dy_ref[g], c_ref[g], precision=F32_DOT)
        powers = [_lam_powers(la_ref[g], lb_ref[g], True) for g in range(gb)]

        def step(it, carries):
            new = []
            for g in range(gb):
                steps, table, idx = powers[g]
                for b in range(batch):
                    rows = pl.ds(pl.multiple_of(b * nch + (nblk - 1 - it) * SCAN_ROWS, SCAN_ROWS), SCAN_ROWS)
                    d_s, carry = _scan_block(dx_ref[g, rows, :], carries[g * batch + b], steps, table, idx, p2 // 2, True)
                    ds_ref[g, rows, :] = d_s
                    new.append(carry)
            return tuple(new)

        lax.fori_loop(0, nblk, step, tuple(jnp.zeros((SCAN_ROWS, p2), F32) for _ in range(gb * batch)))
        for g in range(gb):
            u, dy, ds, x = u_ref[g], dy_ref[g], ds_ref[g], x_ref[g]
            du_ref[g] = _dot_nt(dy, tm_ref[g], precision=F32_DOT) + _dot_nt(ds, b_ref[g], precision=F32_DOT)
            tm_ref[g] = _dot_tn(u, dy, precision=F32_DOT)
            dk_ref[g] = tm_ref[g, 0:SSM_GROUP, :]
            for s in range(1, SSM_CHUNK):
                dk_ref[g, :, :ch - s * SSM_GROUP] += tm_ref[g, s * SSM_GROUP:(s + 1) * SSM_GROUP, s * SSM_GROUP:]
            db_ref[g] = _dot_tn(u, ds, precision=F32_DOT)
            dc_ref[g] = _dot_tn(x, dy, precision=F32_DOT)
            dla_ref[g] = jnp.sum(ds * x, axis=0, keepdims=True)
            dlb_ref[g] = jnp.sum(ds * pltpu.roll(x, p2 // 2, 1), axis=0, keepdims=True)

    def spec(a, b):
        return pl.BlockSpec((gb, a, b), lambda i: (i, 0, 0))

    def shape(a, b):
        return jax.ShapeDtypeStruct((groups, a, b), F32)

    return pl.pallas_call(
        body, name="s5_bwd", grid=(groups // gb,),
        in_specs=[spec(n, ch), spec(n, ch), spec(n, p2), spec(SSM_GROUP, ch), spec(ch, p2), spec(p2, ch), spec(1, p2),
                  spec(1, p2)],
        out_specs=[spec(n, ch), spec(SSM_GROUP, ch), spec(ch, p2), spec(p2, ch), spec(1, p2), spec(1, p2)],
        out_shape=[shape(n, ch), shape(SSM_GROUP, ch), shape(ch, p2), shape(p2, ch), shape(1, p2), shape(1, p2)],
        scratch_shapes=[pltpu.VMEM((gb, n, p2), F32), pltpu.VMEM((gb, n, p2), F32), pltpu.VMEM((gb, ch, ch), F32)],
        compiler_params=_params("parallel"),
    )(ug, dyg, xin, kt_row, b_mat, c_mat, la, lb)


def _block(ref, axis, j, size):
    start = j * size if isinstance(j, int) else pl.multiple_of(j * size, size)
    return ref.at[pl.ds(start, size), :] if axis == 0 else ref.at[:, pl.ds(start, size)]


def _chip_exchange_copies(mode, axes, srcs, lands, send_sems, recv_sems, local_sems):
    x, y, c = lax.axis_index("x"), lax.axis_index("y"), lax.axis_index("c")
    everyone = mode == "all"
    me = 4 * x + 2 * y + c if everyone else 2 * x + y
    n_peers = _exchange_peers(mode)
    local, sends, arrivals = [], [], []
    for w, axis in enumerate(axes):
        if mode == "gather":
            size = srcs[w].shape[axis]
            local.append(pltpu.make_async_copy(srcs[w], _block(lands[w], axis, me, size), local_sems.at[w]))
        elif mode == "scatter":
            size = srcs[w].shape[axis] // N_CHIPS
            local.append(pltpu.make_async_copy(_block(srcs[w], axis, me, size), lands[w].at[me], local_sems.at[w]))
        else:
            local.append(pltpu.make_async_copy(srcs[w], lands[w].at[me], local_sems.at[w]))
        for k in range(1, n_peers + 1):
            bits = k if everyone else 2 * k
            px = 1 - x if bits & 4 else x
            py = 1 - y if bits & 2 else y
            pc = 1 - c if bits & 1 else c
            peer = 4 * px + 2 * py + pc if everyone else 2 * px + py
            if mode == "gather":
                src, dst, arrive = srcs[w], _block(lands[w], axis, me, size), _block(lands[w], axis, peer, size)
            elif mode == "scatter":
                src, dst, arrive = _block(srcs[w], axis, peer, size), lands[w].at[me], lands[w].at[peer]
            else:
                src, dst, arrive = srcs[w], lands[w].at[me], lands[w].at[peer]
            sem = w * n_peers + k - 1
            for target, out in ((dst, sends), (arrive, arrivals)):
                out.append(pltpu.make_async_remote_copy(
                    src_ref=src, dst_ref=target, send_sem=send_sems.at[sem], recv_sem=recv_sems.at[sem],
                    device_id=(px, py, pc), device_id_type=MESH))
    return local, sends, arrivals


def _exchange_peers(mode):
    return N_DEV - 1 if mode == "all" else N_CHIPS - 1


def _chip_exchange_start(name, mode, items, after=None):
    n = len(items)
    n_after = 0 if after is None else 1
    axes = [axis for _, axis in items]
    hbm = pl.BlockSpec(memory_space=pltpu.HBM)
    sem = pl.BlockSpec(memory_space=pltpu.SEMAPHORE)
    lands = []
    for a, axis in items:
        shape = list(a.shape)
        if mode == "gather":
            shape[axis] *= N_CHIPS
        elif mode == "scatter":
            shape[axis] //= N_CHIPS
            shape = [N_CHIPS] + shape
        else:
            shape = [N_DEV] + shape
        lands.append(pltpu.with_memory_space_constraint(lax.empty(tuple(shape), a.dtype), pltpu.HBM))

    def body(*refs):
        srcs, land_refs = refs[:n], refs[n:2 * n]
        send_sems, recv_sems, local_sems = refs[2 * n + n_after:2 * n + n_after + 3]
        token = refs[-1]
        local, sends, _ = _chip_exchange_copies(mode, axes, srcs, land_refs, send_sems, recv_sems, local_sems)
        for cp in local + sends:
            cp.start()
        token[...] = jnp.zeros_like(token)

    n_sem = n * _exchange_peers(mode)
    outs = pl.pallas_call(
        body, name=name,
        out_shape=(pltpu.SemaphoreType.DMA((n_sem,)), pltpu.SemaphoreType.DMA((n_sem,)), pltpu.SemaphoreType.DMA((n,)),
                   *[pltpu.HBM(a.shape, a.dtype) for a, _ in items], *[pltpu.HBM(l.shape, l.dtype) for l in lands],
                   jax.ShapeDtypeStruct((8, LANES), F32)),
        in_specs=[hbm] * (2 * n) + [pl.BlockSpec(memory_space=pl.ANY)] * n_after,
        out_specs=(sem, sem, sem, *[hbm] * (2 * n), pl.BlockSpec(memory_space=pltpu.VMEM)),
        input_output_aliases={i: 3 + i for i in range(2 * n)},
        compiler_params=pltpu.CompilerParams(has_side_effects=pltpu.SideEffectType.DATAFLOW_SIDE_EFFECTING),
    )(*[pltpu.with_memory_space_constraint(a, pltpu.HBM) for a, _ in items], *lands, *([after] if n_after else []))
    return (mode, axes, outs[:3], outs[3:3 + n], outs[3 + n:3 + 2 * n]), outs[-1][0:1, 0:1]


def _chip_exchange_wait(name, handle, after):
    mode, axes, sems, srcs, lands = handle
    n = len(axes)
    after = list(after) if isinstance(after, (tuple, list)) else [after]
    hbm = pl.BlockSpec(memory_space=pltpu.HBM)
    sem = pl.BlockSpec(memory_space=pltpu.SEMAPHORE)

    def body(*refs):
        src_refs, land_refs = refs[:n], refs[n:2 * n]
        send_sems, recv_sems, local_sems = refs[2 * n:2 * n + 3]
        local, sends, arrivals = _chip_exchange_copies(mode, axes, src_refs, land_refs, send_sems, recv_sems, local_sems)
        for cp in sends:
            cp.wait_send()
        for cp in arrivals:
            cp.wait_recv()
        for cp in local:
            cp.wait()

    outs = pl.pallas_call(
        body, name=name,
        out_shape=(*[pltpu.HBM(a.shape, a.dtype) for a in srcs], *[pltpu.HBM(l.shape, l.dtype) for l in lands]),
        in_specs=[hbm] * (2 * n) + [sem] * 3 + [pl.BlockSpec(memory_space=pl.ANY)] * len(after), out_specs=[hbm] * (2 * n),
        input_output_aliases={i: i for i in range(2 * n)},
        compiler_params=pltpu.CompilerParams(has_side_effects=pltpu.SideEffectType.DATAFLOW_SIDE_EFFECTING),
    )(*srcs, *lands, *sems, *after)
    return outs[n:]


def _sum_slots(name, slots, tm=256):
    n_slots, r, c = slots.shape
    tm = _tile(r, tm)

    def body(*refs):
        acc = refs[0][...]
        for s_ref in refs[1:n_slots]:
            acc = acc + s_ref[...]
        refs[n_slots][...] = acc

    specs = [pl.BlockSpec((None, tm, c), functools.partial(lambda i, s: (s, i, 0), s=s)) for s in range(n_slots)]
    return pl.pallas_call(
        body, name=name, grid=(r // tm,), in_specs=specs, out_specs=pl.BlockSpec((tm, c), lambda i: (i, 0)),
        out_shape=jax.ShapeDtypeStruct((r, c), F32), compiler_params=_params("parallel"),
    )(*[slots] * n_slots)


def _swap_with_sibling(name, arrays):
    n = len(arrays)
    hbm = pl.BlockSpec(memory_space=pl.ANY)

    def body(*refs):
        ins, outs = refs[:n], refs[n:2 * n]
        send_sems, recv_sems = refs[2 * n:]
        sibling = (lax.axis_index("x"), lax.axis_index("y"), 1 - lax.axis_index("c"))
        copies = [pltpu.make_async_remote_copy(src_ref=ins[w], dst_ref=outs[w], send_sem=send_sems.at[w],
                                               recv_sem=recv_sems.at[w], device_id=sibling, device_id_type=MESH)
                  for w in range(n)]
        for cp in copies:
            cp.start()
        for cp in copies:
            cp.wait()

    return pl.pallas_call(
        body, name=name, in_specs=[hbm] * n, out_specs=[hbm] * n,
        out_shape=[jax.ShapeDtypeStruct(a.shape, a.dtype) for a in arrays],
        scratch_shapes=[pltpu.SemaphoreType.DMA((n,)), pltpu.SemaphoreType.DMA((n,))],
    )(*arrays)


def _adamw(g, w, m, v):
    m = ADAM_B1 * m + (1.0 - ADAM_B1) * g
    v = ADAM_B2 * v + (1.0 - ADAM_B2) * jnp.square(g)
    m_hat = m / (1.0 - ADAM_B1 ** ADAM_STEP)
    v_hat = v / (1.0 - ADAM_B2 ** ADAM_STEP)
    delta = -ADAM_LR * (m_hat / (jnp.sqrt(v_hat) + ADAM_EPS) + ADAM_WD * w)
    return delta, m, v


def _adamw_small(grads, ws, ms, vs):
    n = len(ws)
    vmem = pl.BlockSpec(memory_space=pltpu.VMEM)

    def body(*refs):
        for i in range(n):
            g, w, m, v = (refs[k * n + i][...] for k in range(4))
            for k, val in enumerate(_adamw(g, w, m, v)):
                refs[(4 + k) * n + i][...] = val

    outs = pl.pallas_call(
        body, name="adamw_small", in_specs=[vmem] * (4 * n), out_specs=[vmem] * (3 * n),
        out_shape=[jax.ShapeDtypeStruct(w.shape, F32) for _ in range(3) for w in ws],
        compiler_params=pltpu.CompilerParams(vmem_limit_bytes=VMEM_LIMIT),
    )(*grads, *ws, *ms, *vs)
    return outs[:n], outs[n:2 * n], outs[2 * n:]


def kernel(x, norm1_g, w_in, q_norm_g, k_norm_g, ssm_lambda_re, ssm_lambda_im, ssm_log_dt, ssm_b_re, ssm_b_im, ssm_c_re, ssm_c_im, ssm_d, w_glu, b_glu, attn_out_g, ssm_out_g, w_out, norm2_g, w_mlp_in, w_mlp_out, loss_target, m_norm1_g, m_w_in, m_q_norm_g, m_k_norm_g, m_ssm_lambda_re, m_ssm_lambda_im, m_ssm_log_dt, m_ssm_b_re, m_ssm_b_im, m_ssm_c_re, m_ssm_c_im, m_ssm_d, m_w_glu, m_b_glu, m_attn_out_g, m_ssm_out_g, m_w_out, m_norm2_g, m_w_mlp_in, m_w_mlp_out, v_norm1_g, v_w_in, v_q_norm_g, v_k_norm_g, v_ssm_lambda_re, v_ssm_lambda_im, v_ssm_log_dt, v_ssm_b_re, v_ssm_b_im, v_ssm_c_re, v_ssm_c_im, v_ssm_d, v_w_glu, v_b_glu, v_attn_out_g, v_ssm_out_g, v_w_out, v_norm2_g, v_w_mlp_in, v_w_mlp_out):
    batch, seq, d_model = x.shape
    tokens = batch * seq
    sb_width = w_in.shape[1]
    n_features = d_model

    big = [("w_in", w_in, m_w_in, v_w_in, 1), ("w_glu", w_glu, m_w_glu, v_w_glu, 0),
           ("w_out", w_out, m_w_out, v_w_out, 0), ("w_mlp_in", w_mlp_in, m_w_mlp_in, v_w_mlp_in, 1),
           ("w_mlp_out", w_mlp_out, m_w_mlp_out, v_w_mlp_out, 0)]
    small = [("norm1_g", norm1_g, m_norm1_g, v_norm1_g), ("q_norm_g", q_norm_g, m_q_norm_g, v_q_norm_g),
             ("k_norm_g", k_norm_g, m_k_norm_g, v_k_norm_g),
             ("ssm_lambda_re", ssm_lambda_re, m_ssm_lambda_re, v_ssm_lambda_re),
             ("ssm_lambda_im", ssm_lambda_im, m_ssm_lambda_im, v_ssm_lambda_im),
             ("ssm_log_dt", ssm_log_dt, m_ssm_log_dt, v_ssm_log_dt),
             ("ssm_b_re", ssm_b_re, m_ssm_b_re, v_ssm_b_re), ("ssm_b_im", ssm_b_im, m_ssm_b_im, v_ssm_b_im),
             ("ssm_c_re", ssm_c_re, m_ssm_c_re, v_ssm_c_re), ("ssm_c_im", ssm_c_im, m_ssm_c_im, v_ssm_c_im),
             ("ssm_d", ssm_d, m_ssm_d, v_ssm_d), ("b_glu", b_glu, m_b_glu, v_b_glu),
             ("attn_out_g", attn_out_g, m_attn_out_g, v_attn_out_g), ("ssm_out_g", ssm_out_g, m_ssm_out_g, v_ssm_out_g),
             ("norm2_g", norm2_g, m_norm2_g, v_norm2_g)]

    gather_in, tok_in = _chip_exchange_start("gather_w_in_start", "gather", [(w_in.astype(BF16), 1)])
    gather_rest, tok_rest = _chip_exchange_start(
        "gather_rest_start", "gather", [(w.astype(BF16), axis) for _, w, _, _, axis in big[1:]], after=tok_in)

    x2 = x.reshape(tokens, d_model)
    tgt2 = loss_target.reshape(tokens, d_model)
    g1, g2 = norm1_g[None, :], norm2_g[None, :]
    g_attn, g_ssm, bias_glu = attn_out_g[None, :], ssm_out_g[None, :], b_glu[None, :]
    heads = sb_width // HEAD_DIM
    qk_scale = 1.0 / math.sqrt(HEAD_DIM)
    gq, gk = (jnp.tile(q_norm_g, heads) * qk_scale)[None, :], jnp.tile(k_norm_g, heads)[None, :]
    lane_head = jnp.arange(LANES) // HEAD_DIM
    ones_blocks = (lane_head[:, None] == lane_head[None, :]).astype(F32)

    (xn,) = _rowwise("norm1", _rms, [x2], [g1 + tok_rest], [(d_model, BF16)])
    s5_params = (ssm_lambda_re, ssm_lambda_im, ssm_log_dt, ssm_b_re, ssm_b_im, ssm_c_re, ssm_c_im, ssm_d)
    (kt_row, b_mat, c_mat, la, lb), s5_vjp = jax.vjp(_s5_operators, *s5_params)
    (wf_in,) = _chip_exchange_wait("gather_w_in_wait", gather_in, [xn, b_mat, c_mat])
    (proj,) = _mm("proj_in", xn, wf_in, "nn")

    def qkv_fn(q, k, v, gq_, gk_, ones):
        return _head_rms(q, gq_, ones), _head_rms(k, gk_, ones), v

    qn, kn, vb = _rowwise("qk_norm", qkv_fn, [(proj, sb_width, 0), (proj, sb_width, 1), (proj, sb_width, 2)],
                          [gq, gk, ones_blocks], [(sb_width, BF16)] * 3)
    sb, c_tot = _attn_fwd(qn, kn, vb, batch=batch, seq=seq, bq=ATTN_BQ, bk=ATTN_BK)
    ug = _tokens_to_groups("u_to_groups", proj, 3, sb_width)
    yg, xin = _s5_fwd(ug, kt_row, b_mat, c_mat, la, lb, batch=batch)
    y_ssm = _groups_to_tokens("y_to_tokens", yg)

    wf_glu, wf_out, wf_mlp_in, wf_mlp_out = _chip_exchange_wait("gather_rest_wait", gather_rest, [y_ssm, sb])
    (gate_pre,) = _mm("glu_gate", y_ssm, wf_glu, "nn", a_fn=_gelu, extras=[(bias_glu, "row")],
                      epilogue=lambda acc, b: acc + b)
    (mixed,) = _rowwise("mix_norm", _mixed, [sb, y_ssm, gate_pre], [g_attn, g_ssm], [(2 * sb_width, BF16)])
    def out_head(acc, r, g):
        h = acc + r
        return h, _rms(h, g)

    h1, hn = _mm("proj_out", mixed, wf_out, "nn", extras=[(x2, "tile"), (g2, "row")], epilogue=out_head,
                 out_dtypes=(F32, BF16), full_rows=True)
    def mlp_act(acc):
        r = jnp.maximum(acc, 0.0)
        return r * r, r

    act, act_root = _mm("mlp_in", hn, wf_mlp_in, "nn", epilogue=mlp_act, out_dtypes=(BF16, BF16))
    inv_n = 1.0 / n_features

    def loss_head(acc, r, t):
        d = ((acc + r) - t) * inv_n
        return d, d, jnp.sum(d * d, keepdims=True) * (0.5 * n_features)

    dy, dy_b, loss_tiles = _mm("mlp_out_loss", act, wf_mlp_out, "nn", extras=[(h1, "tile"), (tgt2, "tile")],
                               epilogue=loss_head, out_dtypes=(F32, BF16), tile_sums=("scalar",))
    loss_part = jnp.sum(loss_tiles)

    (dw_mlp_out,) = _mm("dw_mlp_out", act, dy_b, "tn")
    (dpre,) = _mm("d_mlp_act", dy_b, wf_mlp_out, "nt", extras=[(act_root, "tile")],
                  epilogue=lambda acc, r: acc * (2.0 * r.astype(F32)), out_dtypes=(BF16,))
    (dw_mlp_in,) = _mm("dw_mlp_in", hn, dpre, "tn")
    scatter_mlp, tok_mlp = _chip_exchange_start("scatter_mlp_start", "scatter", [(dw_mlp_in, 1), (dw_mlp_out, 0)])
    def norm_bwd(dn, res, hx, g):
        _, vjp = jax.vjp(_rms, hx, g)
        dh, dg = vjp(dn)
        return res + dh, dg

    dh1, dg_tiles = _mm("d_norm2_in", dpre, wf_mlp_in, "nt", extras=[(dy, "tile"), (h1, "tile"), (g2 + tok_mlp, "row")],
                        epilogue=norm_bwd, tile_sums=("row",), full_rows=True)
    dg_norm2 = jnp.sum(dg_tiles, axis=0, keepdims=True)
    (dmixed,) = _mm("d_mixed", dh1, wf_out, "nt")
    (dw_out,) = _mm("dw_out", mixed, dh1, "tn")

    def mixed_bwd(dm, sb_, ys, gp, ga, gs):
        _, vjp = jax.vjp(lambda a, act, b, c, d: jnp.concatenate(
            [_rms(a, c), _rms(act * jax.nn.sigmoid(b), d)], axis=-1), sb_, _gelu(ys), gp, ga, gs)
        dsb_, dact, dgp_, dga, dgs = vjp(dm)
        return dsb_, dgp_, dact, dga, dgs, jnp.sum(dgp_, axis=0, keepdims=True)

    dsb, dgate_pre, dact_part, dg_attn, dg_ssm, db_glu = _rowwise(
        "mix_norm_bwd", mixed_bwd, [dmixed, sb, y_ssm, gate_pre], [g_attn, g_ssm],
        [(sb_width, F32), (sb_width, BF16), (sb_width, F32)], [(1, sb_width)] * 3)

    def gelu_bwd(acc, part, ys):
        _, vjp = jax.vjp(_gelu, ys)
        return vjp(acc + part)[0]

    (dy_ssm,) = _mm("d_glu_in", dgate_pre, wf_glu, "nt", extras=[(dact_part, "tile"), (y_ssm, "tile")], epilogue=gelu_bwd)
    (dw_glu,) = _mm("dw_glu", y_ssm, dgate_pre, "tn", a_fn=_gelu)
    scatter_mix, tok_mix = _chip_exchange_start("scatter_mix_start", "scatter", [(dw_glu, 0), (dw_out, 0)])

    dug, dkt_row, db_mat, dc_mat, dla, dlb = _s5_bwd(ug, _tokens_to_groups("dy_to_groups", dy_ssm, 0, sb_width), xin,
                                                     kt_row, b_mat, c_mat, la, lb + tok_mix, batch=batch)
    du = _groups_to_tokens("du_to_tokens", dug)
    ds5 = s5_vjp((dkt_row, db_mat, dc_mat, dla, dlb))

    def pack(parts):
        flat = jnp.concatenate([p.reshape(-1) for p in parts])
        rows = -(-flat.shape[0] // (8 * LANES)) * 8
        return jnp.pad(flat, (0, rows * LANES - flat.shape[0])).reshape(rows, LANES)

    def unpack(packed, names):
        flat, out, off = packed.reshape(-1), {}, 0
        for name in names:
            shape = small_shapes[name]
            size = math.prod(shape)
            out[name] = flat[off:off + size].reshape(shape)
            off += size
        return out, flat[off]

    small_shapes = {name: w.shape for name, w, _, _ in small}
    early_names = ["ssm_lambda_re", "ssm_lambda_im", "ssm_log_dt", "ssm_b_re", "ssm_b_im", "ssm_c_re", "ssm_c_im", "ssm_d",
                   "b_glu", "attn_out_g", "ssm_out_g", "norm2_g"]
    late_names = ["norm1_g", "q_norm_g", "k_norm_g"]
    early = pack([*ds5, db_glu[0], dg_attn[0], dg_ssm[0], dg_norm2[0], loss_part])
    early_exchange, _ = _chip_exchange_start("small_early_start", "all", [(early, 0)])

    dqn, dkn, dv = _attn_bwd(qn, kn, vb, c_tot, dsb, batch=batch, seq=seq, bq=ATTN_BQ, bk=ATTN_BK,
                             after=early_exchange[3][0])
    (early_slots,) = _chip_exchange_wait("small_early_wait", early_exchange, dqn)
    small_g, loss = unpack(_sum_slots("sum_small_early", early_slots), early_names)

    def qk_bwd(q, k, dq_, dk_, gq_, gk_, ones):
        _, vjp_q = jax.vjp(lambda a, g: _head_rms(a, g, ones), q, gq_)
        _, vjp_k = jax.vjp(lambda a, g: _head_rms(a, g, ones), k, gk_)
        dq, dgq = vjp_q(dq_)
        dk, dgk = vjp_k(dk_)
        return dq, dk, dgq, dgk

    dq, dk, dgq, dgk = _rowwise("qk_norm_bwd", qk_bwd, [(proj, sb_width, 0), (proj, sb_width, 1), dqn, dkn],
                                [gq, gk, ones_blocks], [(sb_width, BF16)] * 2, [(1, sb_width)] * 2)
    dproj = jnp.concatenate([dq, dk, dv.astype(BF16), du.astype(BF16)], axis=1)
    (dw_in,) = _mm("dw_in", xn, dproj, "tn")
    scatter_in, tok_w_in = _chip_exchange_start("scatter_in_start", "scatter", [(dw_in, 1)])
    dx, dg_tiles = _mm("d_norm1_in", dproj, wf_in, "nt", extras=[(dh1, "tile"), (x2, "tile"), (g1 + tok_w_in, "row")],
                       epilogue=norm_bwd, tile_sums=("row",), full_rows=True)
    dg_norm1 = jnp.sum(dg_tiles, axis=0, keepdims=True)

    late = pack([dg_norm1[0], dgq.reshape(heads, HEAD_DIM).sum(0) * qk_scale, dgk.reshape(heads, HEAD_DIM).sum(0),
                 jnp.zeros((1,), F32)])
    late_exchange, _ = _chip_exchange_start("small_late_start", "all", [(late, 0)])

    def adam_big(sa, sb_, w, m, v):
        g = sa + sb_
        delta, m, v = _adamw(g, w, m, v)
        return g, delta, m, v

    def reduce_and_update(tag, params, slots):
        mine = [_sum_slots("sum_" + name, s) for s, (name, *_rest) in zip(slots, params)]
        theirs = _swap_with_sibling("swap_" + tag, mine)
        return {name: _rowwise("adamw_" + name, adam_big, [sa, sb_, w, m, v], [], [(w.shape[1], F32)] * 4)
                for (name, w, m, v, _), sa, sb_ in zip(params, mine, theirs)}

    started = late_exchange[3][0]
    slots_mlp_in, slots_mlp_out = _chip_exchange_wait("scatter_mlp_wait", scatter_mlp, started)
    slots_glu, slots_out = _chip_exchange_wait("scatter_mix_wait", scatter_mix, started)
    big_out = reduce_and_update("rest", big[1:], [slots_glu, slots_out, slots_mlp_in, slots_mlp_out])

    (late_slots,) = _chip_exchange_wait("small_late_wait", late_exchange, big_out["w_mlp_out"][3])
    reduced = _sum_slots("sum_small_late", late_slots)
    small_g.update(unpack(reduced, late_names)[0])
    small_upd = _adamw_small([small_g[name] for name, *_ in small], [w for _, w, _, _ in small],
                             [m for _, _, m, _ in small], [v for _, _, _, v in small])
    small_out = [small_g] + [{name: small_upd[kind][i] for i, (name, *_) in enumerate(small)} for kind in range(3)]

    (slots_in,) = _chip_exchange_wait("scatter_in_wait", scatter_in, reduced)
    big_out.update(reduce_and_update("w_in", big[:1], [slots_in]))
    names = ["norm1_g", "w_in", "q_norm_g", "k_norm_g", "ssm_lambda_re", "ssm_lambda_im", "ssm_log_dt", "ssm_b_re",
             "ssm_b_im", "ssm_c_re", "ssm_c_im", "ssm_d", "w_glu", "b_glu", "attn_out_g", "ssm_out_g", "w_out",
             "norm2_g", "w_mlp_in", "w_mlp_out"]
    outs = [loss, dx.reshape(batch, seq, d_model)]
    for kind in range(4):
        for name in names:
            outs.append(big_out[name][kind] if name in big_out else small_out[kind][name])
    return tuple(outs)
```

```python
import functools
import math

import jax
import jax.numpy as jnp
from jax import lax
from jax.experimental import pallas as pl
from jax.experimental.pallas import tpu as pltpu

F32 = jnp.float32
BF16 = jnp.bfloat16
F32_DOT = lax.Precision.HIGH
MESH = pl.DeviceIdType.MESH

RMS_EPS = 1e-6
HEAD_DIM = 64
SSM_GROUP = 16
SSM_CHUNK = 16
LANES = 128
N_CHIPS = 4
N_DEV = 8
VMEM_LIMIT = 48 * 1024 * 1024

ADAM_LR = 0.001
ADAM_B1 = 0.9
ADAM_B2 = 0.999
ADAM_EPS = 1e-08
ADAM_WD = 0.01
ADAM_STEP = 10


def _tile(n, pref):
    t = min(n, pref)
    while n % t:
        t //= 2
    return t


def _params(*sem):
    return pltpu.CompilerParams(dimension_semantics=sem, vmem_limit_bytes=VMEM_LIMIT)


_DIMS = {"nn": (((1,), (0,)), ((), ())), "nt": (((1,), (1,)), ((), ())), "tn": (((0,), (0,)), ((), ()))}


MM_VMEM_BUDGET = 40 * 1024 * 1024


def _mm_tiles(m, n, k, a_bytes, b_bytes, tile_bytes, full_rows=False):
    best = None
    for tk in [t for t in (k, k // 2, k // 4, k // 8) if t >= 256 or t == k]:
        for tm in [t for t in (1024, 512, 256, 128) if t <= m and m % t == 0]:
            for tn in [n] if full_rows else [t for t in (1024, 512, 256, 128) if t <= n and n % t == 0]:
                need = 2 * (tm * tk * a_bytes + tk * tn * b_bytes) + 2 * tm * tn * tile_bytes + (tm * tn * 4 if tk < k else 0)
                if need > MM_VMEM_BUDGET:
                    continue
                traffic = m * k * a_bytes * (n // tn) + k * n * b_bytes * (m // tm)
                key = (tk < k, traffic, -tm * tn)
                if best is None or key < best[0]:
                    best = (key, (tm, tn, tk))
    return best[1]


def _mm(name, a, b, mode, *, a_fn=None, extras=(), epilogue=None, out_dtypes=(F32,), tile_sums=(), full_rows=False):
    if mode == "nn":
        (m, k), n = a.shape, b.shape[1]
    elif mode == "nt":
        (m, k), n = a.shape, b.shape[0]
    else:
        (k, m), n = a.shape, b.shape[1]
    tile_bytes = sum(e.dtype.itemsize for e, kind in extras if kind == "tile") + sum(jnp.dtype(d).itemsize for d in out_dtypes)
    tm, tn, tk = _mm_tiles(m, n, k, a.dtype.itemsize, b.dtype.itemsize, tile_bytes, full_rows)
    nk = k // tk
    ne, nout = len(extras), len(out_dtypes)
    dims = _DIMS[mode]

    def body(a_ref, b_ref, *rest):
        ex, outs, sums = rest[:ne], rest[ne:ne + nout], rest[ne + nout:ne + nout + len(tile_sums)]
        at = a_ref[...]
        if a_fn is not None:
            at = a_fn(at)
        p = lax.dot_general(at.astype(BF16), b_ref[...].astype(BF16), dims, preferred_element_type=F32)

        def finish(r):
            if epilogue is not None:
                r = epilogue(r, *[e[...] for e in ex])
            if not isinstance(r, (tuple, list)):
                r = (r,)
            for o, v in zip(outs, r[:nout]):
                o[...] = v.astype(o.dtype)
            for o, v, kind in zip(sums, r[nout:], tile_sums):
                first = lax.broadcasted_iota(jnp.int32, o.shape, 0) == 0
                if kind == "scalar":
                    first &= lax.broadcasted_iota(jnp.int32, o.shape, 1) == 0
                o[...] = jnp.where(first, v, 0.0)

        if nk == 1:
            finish(p)
        else:
            acc = rest[ne + nout + len(tile_sums)]
            kk = pl.program_id(2)

            @pl.when(kk == 0)
            def _():
                acc[...] = p

            @pl.when(kk > 0)
            def _():
                acc[...] += p

            @pl.when(kk == nk - 1)
            def _():
                finish(acc[...])

    if mode == "tn":
        a_spec = pl.BlockSpec((tk, tm), lambda i, j, kk: (kk, i))
    else:
        a_spec = pl.BlockSpec((tm, tk), lambda i, j, kk: (i, kk))
    if mode == "nt":
        b_spec = pl.BlockSpec((tn, tk), lambda i, j, kk: (j, kk))
    else:
        b_spec = pl.BlockSpec((tk, tn), lambda i, j, kk: (kk, j))
    ex_specs = []
    for _, kind in extras:
        if kind == "tile":
            ex_specs.append(pl.BlockSpec((tm, tn), lambda i, j, kk: (i, j)))
        else:
            ex_specs.append(pl.BlockSpec((1, tn), lambda i, j, kk: (0, j)))
    return pl.pallas_call(
        body, name=name, grid=(m // tm, n // tn, nk),
        in_specs=[a_spec, b_spec] + ex_specs,
        out_specs=([pl.BlockSpec((tm, tn), lambda i, j, kk: (i, j)) for _ in out_dtypes]
                   + [pl.BlockSpec((8, LANES if kind == "scalar" else tn), lambda i, j, kk: (i, j)) for kind in tile_sums]),
        out_shape=([jax.ShapeDtypeStruct((m, n), dt) for dt in out_dtypes]
                   + [jax.ShapeDtypeStruct((m // tm * 8, n // tn * LANES if kind == "scalar" else n), F32)
                      for kind in tile_sums]),
        scratch_shapes=[pltpu.VMEM((tm, tn), F32)] if nk > 1 else [],
        compiler_params=_params("parallel", "parallel", "arbitrary"),
    )(a, b, *[e for e, _ in extras])


def _rowwise(name, fn, rows, consts, row_outs, acc_outs=(), tm=256):
    norm = [r if isinstance(r, tuple) else (r, r.shape[1], 0) for r in rows]
    t = norm[0][0].shape[0]
    tm = _tile(t, tm)
    nr, nc, no = len(norm), len(consts), len(row_outs)

    def body(*refs):
        outs = fn(*[r[...] for r in refs[:nr + nc]])
        if not isinstance(outs, (tuple, list)):
            outs = (outs,)
        o_refs, a_refs = refs[nr + nc:nr + nc + no], refs[nr + nc + no:]
        for r, v in zip(o_refs, outs[:no]):
            r[...] = v.astype(r.dtype)
        if a_refs:
            i = pl.program_id(0)

            @pl.when(i == 0)
            def _():
                for r, v in zip(a_refs, outs[no:]):
                    r[...] = v

            @pl.when(i > 0)
            def _():
                for r, v in zip(a_refs, outs[no:]):
                    r[...] += v

    in_specs = [pl.BlockSpec((tm, w), functools.partial(lambda i, cb: (i, cb), cb=cb)) for _, w, cb in norm]
    in_specs += [pl.BlockSpec(c.shape, functools.partial(lambda i, nd: (0,) * nd, nd=c.ndim)) for c in consts]
    out_specs = [pl.BlockSpec((tm, w), lambda i: (i, 0)) for w, _ in row_outs]
    out_specs += [pl.BlockSpec(s, functools.partial(lambda i, nd: (0,) * nd, nd=len(s))) for s in acc_outs]
    out_shape = [jax.ShapeDtypeStruct((t, w), dt) for w, dt in row_outs]
    out_shape += [jax.ShapeDtypeStruct(s, F32) for s in acc_outs]
    return pl.pallas_call(
        body, name=name, grid=(t // tm,), in_specs=in_specs, out_specs=out_specs, out_shape=out_shape,
        compiler_params=_params("arbitrary"),
    )(*[r[0] for r in norm], *consts)


def _rms(x, g):
    return x * lax.rsqrt(jnp.mean(x * x, axis=-1, keepdims=True) + RMS_EPS) * g


@jax.custom_vjp
def _head_sums(x, ones_blocks):
    parts = [jnp.dot(x[:, j:j + LANES], ones_blocks, precision=F32_DOT, preferred_element_type=F32)
             for j in range(0, x.shape[1], LANES)]
    return jnp.concatenate(parts, axis=1)


_head_sums.defvjp(lambda x, ones_blocks: (_head_sums(x, ones_blocks), ones_blocks),
                  lambda ones_blocks, ct: (_head_sums(ct, ones_blocks), None))


def _head_rms(x, g, ones_blocks):
    return x * lax.rsqrt(_head_sums(x * x, ones_blocks) * (1.0 / HEAD_DIM) + RMS_EPS) * g


def _gelu(x):
    return x * (0.5 * (1.0 + jnp.tanh(math.sqrt(2.0 / math.pi) * (x + 0.044715 * (x * x * x)))))


def _mixed(sb, y_ssm, gate_pre, g_attn, g_ssm):
    ssm = _gelu(y_ssm) * jax.nn.sigmoid(gate_pre)
    return jnp.concatenate([_rms(sb, g_attn), _rms(ssm, g_ssm)], axis=-1)


def _softplus(z):
    return jnp.maximum(z, 0.0) + jnp.log(1.0 + jnp.exp(-jnp.abs(z)))


def _running_sums(x, tri):
    return jnp.dot(x.astype(BF16), tri, preferred_element_type=F32)


def _dot_nt(a, b, **kw):
    return lax.dot_general(a, b, _DIMS["nt"], preferred_element_type=F32, **kw)


def _dot_tn(a, b, **kw):
    return lax.dot_general(a, b, _DIMS["tn"], preferred_element_type=F32, **kw)


ATTN_BQ, ATTN_BK = 2048, 256
HEAD_LANES = tuple(slice(h * HEAD_DIM, (h + 1) * HEAD_DIM) for h in range(LANES // HEAD_DIM))


def _attn_fwd(qs, kn, v, *, batch, seq, bq, bk):
    width = qs.shape[1]
    bq = _tile(seq, bq)
    bk = _tile(bq, bk)
    nq, kpq = seq // bq, bq // bk

    def body(q_ref, k_ref, v_ref, o_ref, c_ref):
        row = lax.broadcasted_iota(jnp.int32, (bq, bk), 0)
        col = lax.broadcasted_iota(jnp.int32, (bq, bk), 1)
        tri = (lax.broadcasted_iota(jnp.int32, (bk, bk), 0) >= lax.broadcasted_iota(jnp.int32, (bk, bk), 1)).astype(BF16)

        def q_block(qi, carry):
            r0 = pl.multiple_of(qi * bq, bq)
            qh = [q_ref[pl.ds(r0, bq), ln] for ln in HEAD_LANES]

            def tile(k0, state, top=0):
                diag = top is not None
                top = top or 0
                msk = (col < row)[:bq - top] if diag else None
                new = []
                for h, ln in enumerate(HEAD_LANES):
                    o, c = state[2 * h], state[2 * h + 1]
                    z = _dot_nt(qh[h][top:], k_ref[pl.ds(k0, bk), ln])
                    sp = _softplus(z)
                    if diag:
                        sp = jnp.where(msk, sp, 0.0)
                    r = _running_sums(sp, tri)
                    a = jnp.exp(z - r - c[top:])
                    if diag:
                        a = jnp.where(msk, a, 0.0)
                    o_new = o[top:] + jnp.dot(a.astype(BF16), v_ref[pl.ds(k0, bk), ln], preferred_element_type=F32)
                    c_new = c[top:] + r[:, 0:1]
                    if top:
                        o_new, c_new = jnp.concatenate([o[:top], o_new]), jnp.concatenate([c[:top], c_new])
                    new += [o_new, c_new]
                return tuple(new)

            state = (jnp.zeros((bq, HEAD_DIM), F32), jnp.zeros((bq, 1), F32)) * len(HEAD_LANES)
            for d in reversed(range(kpq)):
                state = tile(pl.multiple_of(r0 + d * bk, bk), state, top=d * bk)
            state = lax.fori_loop(0, qi * kpq, lambda it, st: tile(pl.multiple_of(r0 - (it + 1) * bk, bk), st, None),
                                  state)
            for h, ln in enumerate(HEAD_LANES):
                o_ref[pl.ds(r0, bq), ln] = state[2 * h]
                c_ref[pl.ds(r0, bq), ln] = jnp.broadcast_to(state[2 * h + 1], (bq, HEAD_DIM))
            return carry

        lax.fori_loop(0, nq, q_block, 0)

    spec = pl.BlockSpec((seq, LANES), lambda b, h: (b, h))
    shape = jax.ShapeDtypeStruct((batch * seq, width), F32)
    return pl.pallas_call(
        body, name="attn_fwd", grid=(batch, width // LANES), in_specs=[spec, spec, spec], out_specs=[spec, spec],
        out_shape=[shape, shape], compiler_params=_params("parallel", "parallel"),
    )(qs, kn, v)


def _attn_bwd(qs, kn, v, c_tot, do, *, batch, seq, bq, bk, after):
    width = qs.shape[1]
    bq = _tile(seq, bq)
    bk = _tile(bq, bk)
    nq, kpq = seq // bq, bq // bk

    def body(q_ref, k_ref, v_ref, c_ref, do_ref, after_ref, dq_ref, dk_ref, dv_ref):
        row = lax.broadcasted_iota(jnp.int32, (bq, bk), 0)
        col = lax.broadcasted_iota(jnp.int32, (bq, bk), 1)
        sq_row = lax.broadcasted_iota(jnp.int32, (bk, bk), 0)
        sq_col = lax.broadcasted_iota(jnp.int32, (bk, bk), 1)
        tri = (sq_row >= sq_col).astype(BF16)
        tri_t = (sq_row <= sq_col).astype(BF16)
        dk_ref[...] = jnp.zeros_like(dk_ref)
        dv_ref[...] = jnp.zeros_like(dv_ref)

        def q_block(qi, carry):
            r0 = pl.multiple_of(qi * bq, bq)
            qh = [q_ref[pl.ds(r0, bq), ln] for ln in HEAD_LANES]
            d_out = [do_ref[pl.ds(r0, bq), ln].astype(BF16) for ln in HEAD_LANES]
            c_all = [c_ref[pl.ds(r0, bq), ln][:, 0:1] for ln in HEAD_LANES]

            def tile(k0, state, top=0):
                diag = top is not None
                top = top or 0
                last = diag and top == bq - bk
                msk = (col < row)[:bq - top] if diag else None
                new = []
                for h, ln in enumerate(HEAD_LANES):
                    c_left, g_left, dq = state[3 * h:3 * h + 3]
                    q, d_o = qh[h][top:], d_out[h][top:]
                    k = k_ref[pl.ds(k0, bk), ln]
                    z = _dot_nt(q, k)
                    e = jnp.exp(-jnp.abs(z))
                    sp = jnp.maximum(z, 0.0) + jnp.log(1.0 + e)
                    sig = jnp.exp(z - sp)
                    if diag:
                        sp = jnp.where(msk, sp, 0.0)
                    r = _running_sums(sp, tri)
                    c_new = c_left[top:] + r[:, 0:1]
                    a = jnp.exp(z - r - (0.0 if last else c_all[h][top:] - c_new))
                    if diag:
                        a = jnp.where(msk, a, 0.0)
                    g = a * _dot_nt(d_o, v_ref[pl.ds(k0, bk), ln])
                    pg = _running_sums(g, tri_t)
                    dz = g - sig * (g_left[top:] + pg)
                    if diag:
                        dz = jnp.where(msk, dz, 0.0)
                    dz = dz.astype(BF16)
                    dk_ref[pl.ds(k0, bk), ln] += _dot_tn(dz, q)
                    dv_ref[pl.ds(k0, bk), ln] += _dot_tn(a.astype(BF16), d_o)
                    g_new = g_left[top:] + pg[:, bk - 1:bk]
                    dq_new = dq[top:] + jnp.dot(dz, k, preferred_element_type=F32)
                    if top:
                        c_new = jnp.concatenate([c_left[:top], c_new])
                        g_new = jnp.concatenate([g_left[:top], g_new])
                        dq_new = jnp.concatenate([dq[:top], dq_new])
                    new += [c_new, g_new, dq_new]
                return tuple(new)

            zero = jnp.zeros((bq, 1), F32)
            init = (zero, zero, jnp.zeros((bq, HEAD_DIM), F32)) * len(HEAD_LANES)
            state = lax.fori_loop(0, qi * kpq, lambda it, st: tile(pl.multiple_of(it * bk, bk), st, None), init)
            for d in range(kpq):
                state = tile(pl.multiple_of(r0 + d * bk, bk), state, top=d * bk)
            for h, ln in enumerate(HEAD_LANES):
                dq_ref[pl.ds(r0, bq), ln] = state[3 * h + 2]
            return carry

        lax.fori_loop(0, nq, q_block, 0)

    spec = pl.BlockSpec((seq, LANES), lambda b, h: (b, h))
    shape = jax.ShapeDtypeStruct((batch * seq, width), F32)
    return pl.pallas_call(
        body, name="attn_bwd", grid=(batch, width // LANES),
        in_specs=[spec] * 5 + [pl.BlockSpec(memory_space=pl.ANY)], out_specs=[spec] * 3,
        out_shape=[shape] * 3, compiler_params=_params("parallel", "parallel"),
    )(qs, kn, v, c_tot, do, after)


def _s5_operators(lam_re, lam_im, log_dt, b_re, b_im, c_re, c_im, d_skip):
    groups, n_state, n_ch = b_re.shape
    cs = SSM_CHUNK
    dt = jnp.exp(log_dt)[:, None]
    steps = jnp.arange(cs + 1, dtype=F32)[None, :, None]
    mag = jnp.exp(steps * (lam_re * dt)[:, None, :])
    ang = steps * (lam_im * dt)[:, None, :]
    pw_re, pw_im = mag * jnp.cos(ang), mag * jnp.sin(ang)
    num_re, num_im = pw_re[:, 1] - 1.0, pw_im[:, 1]
    den = lam_re * lam_re + lam_im * lam_im
    cf_re = (num_re * lam_re + num_im * lam_im) / den
    cf_im = (num_im * lam_re - num_re * lam_im) / den
    bb_re = cf_re[:, :, None] * b_re - cf_im[:, :, None] * b_im
    bb_im = cf_re[:, :, None] * b_im + cf_im[:, :, None] * b_re
    width = cs * n_ch
    ct_re, ct_im = c_re.transpose(0, 2, 1), c_im.transpose(0, 2, 1)

    def c_times_powers(first):
        pr = pw_re[:, first:first + cs].transpose(0, 2, 1)[:, :, :, None]
        pi = pw_im[:, first:first + cs].transpose(0, 2, 1)[:, :, :, None]
        re = pr * ct_re[:, :, None, :] - pi * ct_im[:, :, None, :]
        im = pr * ct_im[:, :, None, :] + pi * ct_re[:, :, None, :]
        return re.reshape(groups, n_state, width), im.reshape(groups, n_state, width)

    w_re, w_im = c_times_powers(0)
    kt_row = (jnp.einsum("gpi,gpw->giw", bb_re, w_re, precision=F32_DOT)
              - jnp.einsum("gpi,gpw->giw", bb_im, w_im, precision=F32_DOT))
    kt_row = kt_row + jnp.pad(jnp.eye(n_ch, dtype=F32)[None] * d_skip[:, None, :], ((0, 0), (0, 0), (0, width - n_ch)))
    rp_re, rp_im = pw_re[:, cs - 1::-1][:, :cs], pw_im[:, cs - 1::-1][:, :cs]
    bm_re = rp_re[:, :, None, :] * bb_re.transpose(0, 2, 1)[:, None] - rp_im[:, :, None, :] * bb_im.transpose(0, 2, 1)[:, None]
    bm_im = rp_re[:, :, None, :] * bb_im.transpose(0, 2, 1)[:, None] + rp_im[:, :, None, :] * bb_re.transpose(0, 2, 1)[:, None]
    b_mat = jnp.concatenate([bm_re, bm_im], axis=-1).reshape(groups, width, 2 * n_state)
    w1_re, w1_im = c_times_powers(1)
    c_mat = jnp.concatenate([w1_re, -w1_im], axis=1)
    la = jnp.concatenate([pw_re[:, cs], pw_re[:, cs]], axis=-1)[:, None, :]
    lb = jnp.concatenate([-pw_im[:, cs], pw_im[:, cs]], axis=-1)[:, None, :]
    return kt_row, b_mat, c_mat, la, lb


GROUPS_PER_BLOCK = LANES // SSM_GROUP


def _tokens_to_groups(name, u, col_block, width):
    t = u.shape[0]
    n = t // SSM_CHUNK
    ch = SSM_CHUNK * SSM_GROUP
    blocks = width // LANES

    def body(u_ref, o_ref):
        for s in range(SSM_CHUNK):
            rows = u_ref[pl.ds(s, n, stride=SSM_CHUNK), :]
            for g in range(GROUPS_PER_BLOCK):
                o_ref[g, :, s * SSM_GROUP:(s + 1) * SSM_GROUP] = rows[:, g * SSM_GROUP:(g + 1) * SSM_GROUP]

    return pl.pallas_call(
        body, name=name, grid=(blocks,),
        in_specs=[pl.BlockSpec((t, LANES), lambda j: (0, col_block * blocks + j))],
        out_specs=pl.BlockSpec((GROUPS_PER_BLOCK, n, ch), lambda j: (j, 0, 0)),
        out_shape=jax.ShapeDtypeStruct((width // SSM_GROUP, n, ch), F32), compiler_params=_params("parallel"),
    )(u)


def _groups_to_tokens(name, ug):
    groups, n, ch = ug.shape

    def body(g_ref, o_ref, rows_ref):
        for s in range(SSM_CHUNK):
            for g in range(GROUPS_PER_BLOCK):
                rows_ref[s % 2, :, g * SSM_GROUP:(g + 1) * SSM_GROUP] = g_ref[g, :, s * SSM_GROUP:(s + 1) * SSM_GROUP]
            o_ref[pl.ds(s, n, stride=SSM_CHUNK), :] = rows_ref[s % 2]

    return pl.pallas_call(
        body, name=name, grid=(groups // GROUPS_PER_BLOCK,),
        in_specs=[pl.BlockSpec((GROUPS_PER_BLOCK, n, ch), lambda j: (j, 0, 0))],
        out_specs=pl.BlockSpec((n * SSM_CHUNK, LANES), lambda j: (0, j)),
        out_shape=jax.ShapeDtypeStruct((n * SSM_CHUNK, groups * SSM_GROUP), F32),
        scratch_shapes=[pltpu.VMEM((2, n, LANES), F32)], compiler_params=_params("parallel"),
    )(ug)


SCAN_ROWS = 8


def _toeplitz_to(tm_ref, g, kt_row):
    width = kt_row.shape[1]
    tm_ref[g] = jnp.zeros((width, width), F32)
    for s in range(SSM_CHUNK):
        tm_ref[g, s * SSM_GROUP:(s + 1) * SSM_GROUP, s * SSM_GROUP:] = kt_row[:, :width - s * SSM_GROUP]


def _lam_powers(la, lb, reverse):
    if reverse:
        lb = -lb

    def mul(p, q):
        return p[0] * q[0] - p[1] * q[1], p[0] * q[1] + p[1] * q[0]

    p1 = (la, lb)
    p2 = mul(p1, p1)
    p3 = mul(p2, p1)
    p4 = mul(p2, p2)
    rows = [p1, p2, p3, p4, mul(p4, p1), mul(p4, p2), mul(p4, p3), mul(p4, p4)]
    if reverse:
        rows = rows[::-1]
    idx = lax.broadcasted_iota(jnp.int32, (SCAN_ROWS, la.shape[1]), 0)
    tab_a = sum(jnp.where(idx == j, r[0], 0.0) for j, r in enumerate(rows))
    tab_b = sum(jnp.where(idx == j, r[1], 0.0) for j, r in enumerate(rows))
    return (p1, p2, p4), (tab_a, tab_b), idx


def _scan_block(e, carry, steps, table, idx, half, reverse):
    n = SCAN_ROWS
    for d, (pa, pb) in zip((1, 2, 4), steps):
        sh = pltpu.roll(e, n - d if reverse else d, 0)
        sh = jnp.where(idx < n - d if reverse else idx >= d, sh, 0.0)
        e = e + pa * sh + pb * pltpu.roll(sh, half, 1)
    tab_a, tab_b = table
    e = e + tab_a * carry + tab_b * pltpu.roll(carry, half, 1)
    shifted = jnp.where(idx == (n - 1 if reverse else 0), carry, pltpu.roll(e, n - 1 if reverse else 1, 0))
    edge = e[0:1] if reverse else e[n - 1:n]
    return shifted, jnp.broadcast_to(edge, e.shape)


def _s5_fwd(ug, kt_row, b_mat, c_mat, la, lb, *, batch, gb=8):
    groups, n, ch = ug.shape
    p2 = b_mat.shape[2]
    gb = _tile(groups, gb)
    nch = n // batch
    nblk = nch // SCAN_ROWS

    def body(u_ref, k_ref, b_ref, c_ref, la_ref, lb_ref, y_ref, x_ref, s_ref, tm_ref):
        for g in range(gb):
            _toeplitz_to(tm_ref, g, k_ref[g])
            s_ref[g] = jnp.dot(u_ref[g], b_ref[g], precision=F32_DOT, preferred_element_type=F32)
        powers = [_lam_powers(la_ref[g], lb_ref[g], False) for g in range(gb)]

        def step(blk, carries):
            new = []
            for g in range(gb):
                steps, table, idx = powers[g]
                for b in range(batch):
                    rows = pl.ds(pl.multiple_of(b * nch + blk * SCAN_ROWS, SCAN_ROWS), SCAN_ROWS)
                    x_in, carry = _scan_block(s_ref[g, rows, :], carries[g * batch + b], steps, table, idx, p2 // 2, False)
                    x_ref[g, rows, :] = x_in
                    new.append(carry)
            return tuple(new)

        lax.fori_loop(0, nblk, step, tuple(jnp.zeros((SCAN_ROWS, p2), F32) for _ in range(gb * batch)))
        for g in range(gb):
            y_ref[g] = (jnp.dot(u_ref[g], tm_ref[g], precision=F32_DOT, preferred_element_type=F32)
                        + jnp.dot(x_ref[g], c_ref[g], precision=F32_DOT, preferred_element_type=F32))

    def spec(a, b):
        return pl.BlockSpec((gb, a, b), lambda i: (i, 0, 0))

    return pl.pallas_call(
        body, name="s5_fwd", grid=(groups // gb,),
        in_specs=[spec(n, ch), spec(SSM_GROUP, ch), spec(ch, p2), spec(p2, ch), spec(1, p2), spec(1, p2)],
        out_specs=[spec(n, ch), spec(n, p2)],
        out_shape=[jax.ShapeDtypeStruct((groups, n, ch), F32), jax.ShapeDtypeStruct((groups, n, p2), F32)],
        scratch_shapes=[pltpu.VMEM((gb, n, p2), F32), pltpu.VMEM((gb, ch, ch), F32)],
        compiler_params=_params("parallel"),
    )(ug, kt_row, b_mat, c_mat, la, lb)


def _s5_bwd(ug, dyg, xin, kt_row, b_mat, c_mat, la, lb, *, batch, gb=8):
    groups, n, ch = ug.shape
    p2 = b_mat.shape[2]
    gb = _tile(groups, gb)
    nch = n // batch
    nblk = nch // SCAN_ROWS

    def body(u_ref, dy_ref, x_ref, k_ref, b_ref, c_ref, la_ref, lb_ref,
             du_ref, dk_ref, db_ref, dc_ref, dla_ref, dlb_ref, dx_ref, ds_ref, tm_ref):
        for g in range(gb):
            _toeplitz_to(tm_ref, g, k_ref[g])
            dx_ref[g] = _dot_nt(dy_ref[g], c_ref[g], precision=F32_DOT)
        powers = [_lam_powers(la_ref[g], lb_ref[g], True) for g in range(gb)]

        def step(it, carries):
            new = []
            for g in range(gb):
                steps, table, idx = powers[g]
                for b in range(batch):
                    rows = pl.ds(pl.multiple_of(b * nch + (nblk - 1 - it) * SCAN_ROWS, SCAN_ROWS), SCAN_ROWS)
                    d_s, carry = _scan_block(dx_ref[g, rows, :], carries[g * batch + b], steps, table, idx, p2 // 2, True)
                    ds_ref[g, rows, :] = d_s
                    new.append(carry)
            return tuple(new)

        lax.fori_loop(0, nblk, step, tuple(jnp.zeros((SCAN_ROWS, p2), F32) for _ in range(gb * batch)))
        for g in range(gb):
            u, dy, ds, x = u_ref[g], dy_ref[g], ds_ref[g], x_ref[g]
            du_ref[g] = _dot_nt(dy, tm_ref[g], precision=F32_DOT) + _dot_nt(ds, b_ref[g], precision=F32_DOT)
            tm_ref[g] = _dot_tn(u, dy, precision=F32_DOT)
            dk_ref[g] = tm_ref[g, 0:SSM_GROUP, :]
            for s in range(1, SSM_CHUNK):
                dk_ref[g, :, :ch - s * SSM_GROUP] += tm_ref[g, s * SSM_GROUP:(s + 1) * SSM_GROUP, s * SSM_GROUP:]
            db_ref[g] = _dot_tn(u, ds, precision=F32_DOT)
            dc_ref[g] = _dot_tn(x, dy, precision=F32_DOT)
            dla_ref[g] = jnp.sum(ds * x, axis=0, keepdims=True)
            dlb_ref[g] = jnp.sum(ds * pltpu.roll(x, p2 // 2, 1), axis=0, keepdims=True)

    def spec(a, b):
        return pl.BlockSpec((gb, a, b), lambda i: (i, 0, 0))

    def shape(a, b):
        return jax.ShapeDtypeStruct((groups, a, b), F32)

    return pl.pallas_call(
        body, name="s5_bwd", grid=(groups // gb,),
        in_specs=[spec(n, ch), spec(n, ch), spec(n, p2), spec(SSM_GROUP, ch), spec(ch, p2), spec(p2, ch), spec(1, p2),
                  spec(1, p2)],
        out_specs=[spec(n, ch), spec(SSM_GROUP, ch), spec(ch, p2), spec(p2, ch), spec(1, p2), spec(1, p2)],
        out_shape=[shape(n, ch), shape(SSM_GROUP, ch), shape(ch, p2), shape(p2, ch), shape(1, p2), shape(1, p2)],
        scratch_shapes=[pltpu.VMEM((gb, n, p2), F32), pltpu.VMEM((gb, n, p2), F32), pltpu.VMEM((gb, ch, ch), F32)],
        compiler_params=_params("parallel"),
    )(ug, dyg, xin, kt_row, b_mat, c_mat, la, lb)


def _block(ref, axis, j, size):
    start = j * size if isinstance(j, int) else pl.multiple_of(j * size, size)
    return ref.at[pl.ds(start, size), :] if axis == 0 else ref.at[:, pl.ds(start, size)]


def _chip_exchange_copies(mode, axes, srcs, lands, send_sems, recv_sems, local_sems):
    x, y, c = lax.axis_index("x"), lax.axis_index("y"), lax.axis_index("c")
    everyone = mode == "all"
    me = 4 * x + 2 * y + c if everyone else 2 * x + y
    n_peers = _exchange_peers(mode)
    local, sends, arrivals = [], [], []
    for w, axis in enumerate(axes):
        if mode == "gather":
            size = srcs[w].shape[axis]
            local.append(pltpu.make_async_copy(srcs[w], _block(lands[w], axis, me, size), local_sems.at[w]))
        elif mode == "scatter":
            size = srcs[w].shape[axis] // N_CHIPS
            local.append(pltpu.make_async_copy(_block(srcs[w], axis, me, size), lands[w].at[me], local_sems.at[w]))
        else:
            local.append(pltpu.make_async_copy(srcs[w], lands[w].at[me], local_sems.at[w]))
        for k in range(1, n_peers + 1):
            bits = k if everyone else 2 * k
            px = 1 - x if bits & 4 else x
            py = 1 - y if bits & 2 else y
            pc = 1 - c if bits & 1 else c
            peer = 4 * px + 2 * py + pc if everyone else 2 * px + py
            if mode == "gather":
                src, dst, arrive = srcs[w], _block(lands[w], axis, me, size), _block(lands[w], axis, peer, size)
            elif mode == "scatter":
                src, dst, arrive = _block(srcs[w], axis, peer, size), lands[w].at[me], lands[w].at[peer]
            else:
                src, dst, arrive = srcs[w], lands[w].at[me], lands[w].at[peer]
            sem = w * n_peers + k - 1
            for target, out in ((dst, sends), (arrive, arrivals)):
                out.append(pltpu.make_async_remote_copy(
                    src_ref=src, dst_ref=target, send_sem=send_sems.at[sem], recv_sem=recv_sems.at[sem],
                    device_id=(px, py, pc), device_id_type=MESH))
    return local, sends, arrivals


def _exchange_peers(mode):
    return N_DEV - 1 if mode == "all" else N_CHIPS - 1


def _chip_exchange_start(name, mode, items, after=None):
    n = len(items)
    n_after = 0 if after is None else 1
    axes = [axis for _, axis in items]
    hbm = pl.BlockSpec(memory_space=pltpu.HBM)
    sem = pl.BlockSpec(memory_space=pltpu.SEMAPHORE)
    lands = []
    for a, axis in items:
        shape = list(a.shape)
        if mode == "gather":
            shape[axis] *= N_CHIPS
        elif mode == "scatter":
            shape[axis] //= N_CHIPS
            shape = [N_CHIPS] + shape
        else:
            shape = [N_DEV] + shape
        lands.append(pltpu.with_memory_space_constraint(lax.empty(tuple(shape), a.dtype), pltpu.HBM))

    def body(*refs):
        srcs, land_refs = refs[:n], refs[n:2 * n]
        send_sems, recv_sems, local_sems = refs[2 * n + n_after:2 * n + n_after + 3]
        token = refs[-1]
        local, sends, _ = _chip_exchange_copies(mode, axes, srcs, land_refs, send_sems, recv_sems, local_sems)
        for cp in local + sends:
            cp.start()
        token[...] = jnp.zeros_like(token)

    n_sem = n * _exchange_peers(mode)
    outs = pl.pallas_call(
        body, name=name,
        out_shape=(pltpu.SemaphoreType.DMA((n_sem,)), pltpu.SemaphoreType.DMA((n_sem,)), pltpu.SemaphoreType.DMA((n,)),
                   *[pltpu.HBM(a.shape, a.dtype) for a, _ in items], *[pltpu.HBM(l.shape, l.dtype) for l in lands],
                   jax.ShapeDtypeStruct((8, LANES), F32)),
        in_specs=[hbm] * (2 * n) + [pl.BlockSpec(memory_space=pl.ANY)] * n_after,
        out_specs=(sem, sem, sem, *[hbm] * (2 * n), pl.BlockSpec(memory_space=pltpu.VMEM)),
        input_output_aliases={i: 3 + i for i in range(2 * n)},
        compiler_params=pltpu.CompilerParams(has_side_effects=pltpu.SideEffectType.DATAFLOW_SIDE_EFFECTING),
    )(*[pltpu.with_memory_space_constraint(a, pltpu.HBM) for a, _ in items], *lands, *([after] if n_after else []))
    return (mode, axes, outs[:3], outs[3:3 + n], outs[3 + n:3 + 2 * n]), outs[-1][0:1, 0:1]


def _chip_exchange_wait(name, handle, after):
    mode, axes, sems, srcs, lands = handle
    n = len(axes)
    after = list(after) if isinstance(after, (tuple, list)) else [after]
    hbm = pl.BlockSpec(memory_space=pltpu.HBM)
    sem = pl.BlockSpec(memory_space=pltpu.SEMAPHORE)

    def body(*refs):
        src_refs, land_refs = refs[:n], refs[n:2 * n]
        send_sems, recv_sems, local_sems = refs[2 * n:2 * n + 3]
        local, sends, arrivals = _chip_exchange_copies(mode, axes, src_refs, land_refs, send_sems, recv_sems, local_sems)
        for cp in sends:
            cp.wait_send()
        for cp in arrivals:
            cp.wait_recv()
        for cp in local:
            cp.wait()

    outs = pl.pallas_call(
        body, name=name,
        out_shape=(*[pltpu.HBM(a.shape, a.dtype) for a in srcs], *[pltpu.HBM(l.shape, l.dtype) for l in lands]),
        in_specs=[hbm] * (2 * n) + [sem] * 3 + [pl.BlockSpec(memory_space=pl.ANY)] * len(after), out_specs=[hbm] * (2 * n),
        input_output_aliases={i: i for i in range(2 * n)},
        compiler_params=pltpu.CompilerParams(has_side_effects=pltpu.SideEffectType.DATAFLOW_SIDE_EFFECTING),
    )(*srcs, *lands, *sems, *after)
    return outs[n:]


def _sum_slots(name, slots, tm=256):
    n_slots, r, c = slots.shape
    tm = _tile(r, tm)

    def body(*refs):
        acc = refs[0][...]
        for s_ref in refs[1:n_slots]:
            acc = acc + s_ref[...]
        refs[n_slots][...] = acc

    specs = [pl.BlockSpec((None, tm, c), functools.partial(lambda i, s: (s, i, 0), s=s)) for s in range(n_slots)]
    return pl.pallas_call(
        body, name=name, grid=(r // tm,), in_specs=specs, out_specs=pl.BlockSpec((tm, c), lambda i: (i, 0)),
        out_shape=jax.ShapeDtypeStruct((r, c), F32), compiler_params=_params("parallel"),
    )(*[slots] * n_slots)


def _swap_with_sibling(name, arrays):
    n = len(arrays)
    hbm = pl.BlockSpec(memory_space=pl.ANY)

    def body(*refs):
        ins, outs = refs[:n], refs[n:2 * n]
        send_sems, recv_sems = refs[2 * n:]
        sibling = (lax.axis_index("x"), lax.axis_index("y"), 1 - lax.axis_index("c"))
        copies = [pltpu.make_async_remote_copy(src_ref=ins[w], dst_ref=outs[w], send_sem=send_sems.at[w],
                                               recv_sem=recv_sems.at[w], device_id=sibling, device_id_type=MESH)
                  for w in range(n)]
        for cp in copies:
            cp.start()
        for cp in copies:
            cp.wait()

    return pl.pallas_call(
        body, name=name, in_specs=[hbm] * n, out_specs=[hbm] * n,
        out_shape=[jax.ShapeDtypeStruct(a.shape, a.dtype) for a in arrays],
        scratch_shapes=[pltpu.SemaphoreType.DMA((n,)), pltpu.SemaphoreType.DMA((n,))],
    )(*arrays)


def _adamw(g, w, m, v):
    m = ADAM_B1 * m + (1.0 - ADAM_B1) * g
    v = ADAM_B2 * v + (1.0 - ADAM_B2) * jnp.square(g)
    m_hat = m / (1.0 - ADAM_B1 ** ADAM_STEP)
    v_hat = v / (1.0 - ADAM_B2 ** ADAM_STEP)
    delta = -ADAM_LR * (m_hat / (jnp.sqrt(v_hat) + ADAM_EPS) + ADAM_WD * w)
    return delta, m, v


def _adamw_small(grads, ws, ms, vs):
    n = len(ws)

    def whole(a):
        return pl.BlockSpec(a.shape, functools.partial(lambda i, nd: (0,) * nd, nd=a.ndim))

    def body(*refs):
        for i in range(n):
            g, w, m, v = (refs[k * n + i][...] for k in range(4))
            for k, val in enumerate(_adamw(g, w, m, v)):
                refs[(4 + k) * n + i][...] = val

    outs = pl.pallas_call(
        body, name="adamw_small", grid=(1,), in_specs=[whole(a) for a in (*grads, *ws, *ms, *vs)],
        out_specs=[whole(w) for _ in range(3) for w in ws],
        out_shape=[jax.ShapeDtypeStruct(w.shape, F32) for _ in range(3) for w in ws],
        compiler_params=pltpu.CompilerParams(vmem_limit_bytes=VMEM_LIMIT),
    )(*grads, *ws, *ms, *vs)
    return outs[:n], outs[n:2 * n], outs[2 * n:]


def kernel(x, norm1_g, w_in, q_norm_g, k_norm_g, ssm_lambda_re, ssm_lambda_im, ssm_log_dt, ssm_b_re, ssm_b_im, ssm_c_re, ssm_c_im, ssm_d, w_glu, b_glu, attn_out_g, ssm_out_g, w_out, norm2_g, w_mlp_in, w_mlp_out, loss_target, m_norm1_g, m_w_in, m_q_norm_g, m_k_norm_g, m_ssm_lambda_re, m_ssm_lambda_im, m_ssm_log_dt, m_ssm_b_re, m_ssm_b_im, m_ssm_c_re, m_ssm_c_im, m_ssm_d, m_w_glu, m_b_glu, m_attn_out_g, m_ssm_out_g, m_w_out, m_norm2_g, m_w_mlp_in, m_w_mlp_out, v_norm1_g, v_w_in, v_q_norm_g, v_k_norm_g, v_ssm_lambda_re, v_ssm_lambda_im, v_ssm_log_dt, v_ssm_b_re, v_ssm_b_im, v_ssm_c_re, v_ssm_c_im, v_ssm_d, v_w_glu, v_b_glu, v_attn_out_g, v_ssm_out_g, v_w_out, v_norm2_g, v_w_mlp_in, v_w_mlp_out):
    batch, seq, d_model = x.shape
    tokens = batch * seq
    sb_width = w_in.shape[1]
    n_features = d_model

    big = [("w_in", w_in, m_w_in, v_w_in, 1), ("w_glu", w_glu, m_w_glu, v_w_glu, 0),
           ("w_out", w_out, m_w_out, v_w_out, 0), ("w_mlp_in", w_mlp_in, m_w_mlp_in, v_w_mlp_in, 1),
           ("w_mlp_out", w_mlp_out, m_w_mlp_out, v_w_mlp_out, 0)]
    small = [("norm1_g", norm1_g, m_norm1_g, v_norm1_g), ("q_norm_g", q_norm_g, m_q_norm_g, v_q_norm_g),
             ("k_norm_g", k_norm_g, m_k_norm_g, v_k_norm_g),
             ("ssm_lambda_re", ssm_lambda_re, m_ssm_lambda_re, v_ssm_lambda_re),
             ("ssm_lambda_im", ssm_lambda_im, m_ssm_lambda_im, v_ssm_lambda_im),
             ("ssm_log_dt", ssm_log_dt, m_ssm_log_dt, v_ssm_log_dt),
             ("ssm_b_re", ssm_b_re, m_ssm_b_re, v_ssm_b_re), ("ssm_b_im", ssm_b_im, m_ssm_b_im, v_ssm_b_im),
             ("ssm_c_re", ssm_c_re, m_ssm_c_re, v_ssm_c_re), ("ssm_c_im", ssm_c_im, m_ssm_c_im, v_ssm_c_im),
             ("ssm_d", ssm_d, m_ssm_d, v_ssm_d), ("b_glu", b_glu, m_b_glu, v_b_glu),
             ("attn_out_g", attn_out_g, m_attn_out_g, v_attn_out_g), ("ssm_out_g", ssm_out_g, m_ssm_out_g, v_ssm_out_g),
             ("norm2_g", norm2_g, m_norm2_g, v_norm2_g)]

    gather_in, tok_in = _chip_exchange_start("gather_w_in_start", "gather", [(w_in.astype(BF16), 1)])
    gather_rest, tok_rest = _chip_exchange_start(
        "gather_rest_start", "gather", [(w.astype(BF16), axis) for _, w, _, _, axis in big[1:]], after=tok_in)

    x2 = x.reshape(tokens, d_model)
    tgt2 = loss_target.reshape(tokens, d_model)
    g1, g2 = norm1_g[None, :], norm2_g[None, :]
    g_attn, g_ssm, bias_glu = attn_out_g[None, :], ssm_out_g[None, :], b_glu[None, :]
    heads = sb_width // HEAD_DIM
    qk_scale = 1.0 / math.sqrt(HEAD_DIM)
    gq, gk = (jnp.tile(q_norm_g, heads) * qk_scale)[None, :], jnp.tile(k_norm_g, heads)[None, :]
    lane_head = jnp.arange(LANES) // HEAD_DIM
    ones_blocks = (lane_head[:, None] == lane_head[None, :]).astype(F32)

    (xn,) = _rowwise("norm1", _rms, [x2], [g1 + tok_rest], [(d_model, BF16)])
    s5_params = (ssm_lambda_re, ssm_lambda_im, ssm_log_dt, ssm_b_re, ssm_b_im, ssm_c_re, ssm_c_im, ssm_d)
    (kt_row, b_mat, c_mat, la, lb), s5_vjp = jax.vjp(_s5_operators, *s5_params)
    (wf_in,) = _chip_exchange_wait("gather_w_in_wait", gather_in, [xn, b_mat, c_mat])
    (proj,) = _mm("proj_in", xn, wf_in, "nn")

    def qkv_fn(q, k, v, gq_, gk_, ones):
        return _head_rms(q, gq_, ones), _head_rms(k, gk_, ones), v

    qn, kn, vb = _rowwise("qk_norm", qkv_fn, [(proj, sb_width, 0), (proj, sb_width, 1), (proj, sb_width, 2)],
                          [gq, gk, ones_blocks], [(sb_width, BF16)] * 3)
    sb, c_tot = _attn_fwd(qn, kn, vb, batch=batch, seq=seq, bq=ATTN_BQ, bk=ATTN_BK)
    ug = _tokens_to_groups("u_to_groups", proj, 3, sb_width)
    yg, xin = _s5_fwd(ug, kt_row, b_mat, c_mat, la, lb, batch=batch)
    y_ssm = _groups_to_tokens("y_to_tokens", yg)

    wf_glu, wf_out, wf_mlp_in, wf_mlp_out = _chip_exchange_wait("gather_rest_wait", gather_rest, [y_ssm, sb])
    (gate_pre,) = _mm("glu_gate", y_ssm, wf_glu, "nn", a_fn=_gelu, extras=[(bias_glu, "row")],
                      epilogue=lambda acc, b: acc + b)
    (mixed,) = _rowwise("mix_norm", _mixed, [sb, y_ssm, gate_pre], [g_attn, g_ssm], [(2 * sb_width, BF16)])
    def out_head(acc, r, g):
        h = acc + r
        return h, _rms(h, g)

    h1, hn = _mm("proj_out", mixed, wf_out, "nn", extras=[(x2, "tile"), (g2, "row")], epilogue=out_head,
                 out_dtypes=(F32, BF16), full_rows=True)
    def mlp_act(acc):
        r = jnp.maximum(acc, 0.0)
        return r * r, r

    act, act_root = _mm("mlp_in", hn, wf_mlp_in, "nn", epilogue=mlp_act, out_dtypes=(BF16, BF16))
    inv_n = 1.0 / n_features

    def loss_head(acc, r, t):
        d = ((acc + r) - t) * inv_n
        return d, d, jnp.sum(d * d, keepdims=True) * (0.5 * n_features)

    dy, dy_b, loss_tiles = _mm("mlp_out_loss", act, wf_mlp_out, "nn", extras=[(h1, "tile"), (tgt2, "tile")],
                               epilogue=loss_head, out_dtypes=(F32, BF16), tile_sums=("scalar",))
    loss_part = jnp.sum(loss_tiles)

    (dw_mlp_out,) = _mm("dw_mlp_out", act, dy_b, "tn")
    (dpre,) = _mm("d_mlp_act", dy_b, wf_mlp_out, "nt", extras=[(act_root, "tile")],
                  epilogue=lambda acc, r: acc * (2.0 * r.astype(F32)), out_dtypes=(BF16,))
    (dw_mlp_in,) = _mm("dw_mlp_in", hn, dpre, "tn")
    scatter_mlp, tok_mlp = _chip_exchange_start("scatter_mlp_start", "scatter", [(dw_mlp_in, 1), (dw_mlp_out, 0)])
    def norm_bwd(dn, res, hx, g):
        _, vjp = jax.vjp(_rms, hx, g)
        dh, dg = vjp(dn)
        return res + dh, dg

    dh1, dg_tiles = _mm("d_norm2_in", dpre, wf_mlp_in, "nt", extras=[(dy, "tile"), (h1, "tile"), (g2 + tok_mlp, "row")],
                        epilogue=norm_bwd, tile_sums=("row",), full_rows=True)
    dg_norm2 = jnp.sum(dg_tiles, axis=0, keepdims=True)
    (dmixed,) = _mm("d_mixed", dh1, wf_out, "nt")
    (dw_out,) = _mm("dw_out", mixed, dh1, "tn")

    def mixed_bwd(dm, sb_, ys, gp, ga, gs):
        _, vjp = jax.vjp(lambda a, act, b, c, d: jnp.concatenate(
            [_rms(a, c), _rms(act * jax.nn.sigmoid(b), d)], axis=-1), sb_, _gelu(ys), gp, ga, gs)
        dsb_, dact, dgp_, dga, dgs = vjp(dm)
        return dsb_, dgp_, dact, dga, dgs, jnp.sum(dgp_, axis=0, keepdims=True)

    dsb, dgate_pre, dact_part, dg_attn, dg_ssm, db_glu = _rowwise(
        "mix_norm_bwd", mixed_bwd, [dmixed, sb, y_ssm, gate_pre], [g_attn, g_ssm],
        [(sb_width, F32), (sb_width, BF16), (sb_width, F32)], [(1, sb_width)] * 3)

    def gelu_bwd(acc, part, ys):
        _, vjp = jax.vjp(_gelu, ys)
        return vjp(acc + part)[0]

    (dy_ssm,) = _mm("d_glu_in", dgate_pre, wf_glu, "nt", extras=[(dact_part, "tile"), (y_ssm, "tile")], epilogue=gelu_bwd)
    (dw_glu,) = _mm("dw_glu", y_ssm, dgate_pre, "tn", a_fn=_gelu)
    scatter_mix, tok_mix = _chip_exchange_start("scatter_mix_start", "scatter", [(dw_glu, 0), (dw_out, 0)])

    dug, dkt_row, db_mat, dc_mat, dla, dlb = _s5_bwd(ug, _tokens_to_groups("dy_to_groups", dy_ssm, 0, sb_width), xin,
                                                     kt_row, b_mat, c_mat, la, lb + tok_mix, batch=batch)
    du = _groups_to_tokens("du_to_tokens", dug)
    ds5 = s5_vjp((dkt_row, db_mat, dc_mat, dla, dlb))

    def pack(parts):
        flat = jnp.concatenate([p.reshape(-1) for p in parts])
        rows = -(-flat.shape[0] // (8 * LANES)) * 8
        return jnp.pad(flat, (0, rows * LANES - flat.shape[0])).reshape(rows, LANES)

    def unpack(packed, names):
        flat, out, off = packed.reshape(-1), {}, 0
        for name in names:
            shape = small_shapes[name]
            size = math.prod(shape)
            out[name] = flat[off:off + size].reshape(shape)
            off += size
        return out, flat[off]

    small_shapes = {name: w.shape for name, w, _, _ in small}
    early_names = ["ssm_lambda_re", "ssm_lambda_im", "ssm_log_dt", "ssm_b_re", "ssm_b_im", "ssm_c_re", "ssm_c_im", "ssm_d",
                   "b_glu", "attn_out_g", "ssm_out_g", "norm2_g"]
    late_names = ["norm1_g", "q_norm_g", "k_norm_g"]
    early = pack([*ds5, db_glu[0], dg_attn[0], dg_ssm[0], dg_norm2[0], loss_part])
    early_exchange, _ = _chip_exchange_start("small_early_start", "all", [(early, 0)])

    dqn, dkn, dv = _attn_bwd(qn, kn, vb, c_tot, dsb, batch=batch, seq=seq, bq=ATTN_BQ, bk=ATTN_BK,
                             after=early_exchange[3][0])
    (early_slots,) = _chip_exchange_wait("small_early_wait", early_exchange, dqn)
    small_g, loss = unpack(_sum_slots("sum_small_early", early_slots), early_names)

    def qk_bwd(q, k, dq_, dk_, gq_, gk_, ones):
        _, vjp_q = jax.vjp(lambda a, g: _head_rms(a, g, ones), q, gq_)
        _, vjp_k = jax.vjp(lambda a, g: _head_rms(a, g, ones), k, gk_)
        dq, dgq = vjp_q(dq_)
        dk, dgk = vjp_k(dk_)
        return dq, dk, dgq, dgk

    dq, dk, dgq, dgk = _rowwise("qk_norm_bwd", qk_bwd, [(proj, sb_width, 0), (proj, sb_width, 1), dqn, dkn],
                                [gq, gk, ones_blocks], [(sb_width, BF16)] * 2, [(1, sb_width)] * 2)
    dproj = jnp.concatenate([dq, dk, dv.astype(BF16), du.astype(BF16)], axis=1)
    (dw_in,) = _mm("dw_in", xn, dproj, "tn")
    scatter_in, tok_w_in = _chip_exchange_start("scatter_in_start", "scatter", [(dw_in, 1)])
    dx, dg_tiles = _mm("d_norm1_in", dproj, wf_in, "nt", extras=[(dh1, "tile"), (x2, "tile"), (g1 + tok_w_in, "row")],
                       epilogue=norm_bwd, tile_sums=("row",), full_rows=True)
    dg_norm1 = jnp.sum(dg_tiles, axis=0, keepdims=True)

    late = pack([dg_norm1[0], dgq.reshape(heads, HEAD_DIM).sum(0) * qk_scale, dgk.reshape(heads, HEAD_DIM).sum(0),
                 jnp.zeros((1,), F32)])
    late_exchange, _ = _chip_exchange_start("small_late_start", "all", [(late, 0)])

    def adam_big(sa, sb_, w, m, v):
        g = sa + sb_
        delta, m, v = _adamw(g, w, m, v)
        return g, delta, m, v

    def reduce_and_update(tag, params, slots):
        mine = [_sum_slots("sum_" + name, s) for s, (name, *_rest) in zip(slots, params)]
        theirs = _swap_with_sibling("swap_" + tag, mine)
        return {name: _rowwise("adamw_" + name, adam_big, [sa, sb_, w, m, v], [], [(w.shape[1], F32)] * 4)
                for (name, w, m, v, _), sa, sb_ in zip(params, mine, theirs)}

    started = late_exchange[3][0]
    slots_mlp_in, slots_mlp_out = _chip_exchange_wait("scatter_mlp_wait", scatter_mlp, started)
    slots_glu, slots_out = _chip_exchange_wait("scatter_mix_wait", scatter_mix, started)
    big_out = reduce_and_update("rest", big[1:], [slots_glu, slots_out, slots_mlp_in, slots_mlp_out])

    (late_slots,) = _chip_exchange_wait("small_late_wait", late_exchange, big_out["w_mlp_out"][3])
    reduced = _sum_slots("sum_small_late", late_slots)
    small_g.update(unpack(reduced, late_names)[0])
    small_upd = _adamw_small([small_g[name] for name, *_ in small], [w for _, w, _, _ in small],
                             [m for _, _, m, _ in small], [v for _, _, _, v in small])
    small_out = [small_g] + [{name: small_upd[kind][i] for i, (name, *_) in enumerate(small)} for kind in range(3)]

    (slots_in,) = _chip_exchange_wait("scatter_in_wait", scatter_in, reduced)
    big_out.update(reduce_and_update("w_in", big[:1], [slots_in]))
    names = ["norm1_g", "w_in", "q_norm_g", "k_norm_g", "ssm_lambda_re", "ssm_lambda_im", "ssm_log_dt", "ssm_b_re",
             "ssm_b_im", "ssm_c_re", "ssm_c_im", "ssm_d", "w_glu", "b_glu", "attn_out_g", "ssm_out_g", "w_out",
             "norm2_g", "w_mlp_in", "w_mlp_out"]
    outs = [loss, dx.reshape(batch, seq, d_model)]
    for kind in range(4):
        for name in names:
            outs.append(big_out[name][kind] if name in big_out else small_out[kind][name])
    return tuple(outs)
```

```python
import functools
import math

import jax
import jax.numpy as jnp
from jax import lax
from jax.experimental import pallas as pl
from jax.experimental.pallas import tpu as pltpu

F32 = jnp.float32
BF16 = jnp.bfloat16
F32_DOT = lax.Precision.HIGH
MESH = pl.DeviceIdType.MESH

RMS_EPS = 1e-6
HEAD_DIM = 64
SSM_GROUP = 16
SSM_CHUNK = 16
LANES = 128
N_CHIPS = 4
N_DEV = 8
VMEM_LIMIT = 48 * 1024 * 1024

ADAM_LR = 0.001
ADAM_B1 = 0.9
ADAM_B2 = 0.999
ADAM_EPS = 1e-08
ADAM_WD = 0.01
ADAM_STEP = 10


def _tile(n, pref):
    t = min(n, pref)
    while n % t:
        t //= 2
    return t


def _params(*sem):
    return pltpu.CompilerParams(dimension_semantics=sem, vmem_limit_bytes=VMEM_LIMIT)


_DIMS = {"nn": (((1,), (0,)), ((), ())), "nt": (((1,), (1,)), ((), ())), "tn": (((0,), (0,)), ((), ()))}


MM_VMEM_BUDGET = 40 * 1024 * 1024


def _mm_tiles(m, n, k, a_bytes, b_bytes, tile_bytes, full_rows=False):
    best = None
    for tk in [t for t in (k, k // 2, k // 4, k // 8) if t >= 256 or t == k]:
        for tm in [t for t in (1024, 512, 256, 128) if t <= m and m % t == 0]:
            for tn in [n] if full_rows else [t for t in (1024, 512, 256, 128) if t <= n and n % t == 0]:
                need = 2 * (tm * tk * a_bytes + tk * tn * b_bytes) + 2 * tm * tn * tile_bytes + (tm * tn * 4 if tk < k else 0)
                if need > MM_VMEM_BUDGET:
                    continue
                traffic = m * k * a_bytes * (1 if tk == k else n // tn) + k * n * b_bytes * (1 if n == tn and tk == k else m // tm)
                key = (tk < k, traffic, -tm * tn)
                if best is None or key < best[0]:
                    best = (key, (tm, tn, tk))
    return best[1]


def _mm(name, a, b, mode, *, a_fn=None, extras=(), epilogue=None, out_dtypes=(F32,), tile_sums=(), full_rows=False):
    if mode == "nn":
        (m, k), n = a.shape, b.shape[1]
    elif mode == "nt":
        (m, k), n = a.shape, b.shape[0]
    else:
        (k, m), n = a.shape, b.shape[1]
    tile_bytes = sum(e.dtype.itemsize for e, kind in extras if kind == "tile") + sum(jnp.dtype(d).itemsize for d in out_dtypes)
    tm, tn, tk = _mm_tiles(m, n, k, a.dtype.itemsize, b.dtype.itemsize, tile_bytes, full_rows)
    nk = k // tk
    ne, nout = len(extras), len(out_dtypes)
    dims = _DIMS[mode]

    def body(a_ref, b_ref, *rest):
        ex, outs, sums = rest[:ne], rest[ne:ne + nout], rest[ne + nout:ne + nout + len(tile_sums)]
        at = a_ref[...]
        if a_fn is not None:
            at = a_fn(at)
        p = lax.dot_general(at.astype(BF16), b_ref[...].astype(BF16), dims, preferred_element_type=F32)

        def finish(r):
            if epilogue is not None:
                r = epilogue(r, *[e[...] for e in ex])
            if not isinstance(r, (tuple, list)):
                r = (r,)
            for o, v in zip(outs, r[:nout]):
                o[...] = v.astype(o.dtype)
            for o, v, kind in zip(sums, r[nout:], tile_sums):
                first = lax.broadcasted_iota(jnp.int32, o.shape, 0) == 0
                if kind == "scalar":
                    first &= lax.broadcasted_iota(jnp.int32, o.shape, 1) == 0
                o[...] = jnp.where(first, v, 0.0)

        if nk == 1:
            finish(p)
        else:
            acc = rest[ne + nout + len(tile_sums)]
            kk = pl.program_id(2)

            @pl.when(kk == 0)
            def _():
                acc[...] = p

            @pl.when(kk > 0)
            def _():
                acc[...] += p

            @pl.when(kk == nk - 1)
            def _():
                finish(acc[...])

    if mode == "tn":
        a_spec = pl.BlockSpec((tk, tm), lambda i, j, kk: (kk, i))
    else:
        a_spec = pl.BlockSpec((tm, tk), lambda i, j, kk: (i, kk))
    if mode == "nt":
        b_spec = pl.BlockSpec((tn, tk), lambda i, j, kk: (j, kk))
    else:
        b_spec = pl.BlockSpec((tk, tn), lambda i, j, kk: (kk, j))
    ex_specs = []
    for _, kind in extras:
        if kind == "tile":
            ex_specs.append(pl.BlockSpec((tm, tn), lambda i, j, kk: (i, j)))
        else:
            ex_specs.append(pl.BlockSpec((1, tn), lambda i, j, kk: (0, j)))
    return pl.pallas_call(
        body, name=name, grid=(m // tm, n // tn, nk),
        in_specs=[a_spec, b_spec] + ex_specs,
        out_specs=([pl.BlockSpec((tm, tn), lambda i, j, kk: (i, j)) for _ in out_dtypes]
                   + [pl.BlockSpec((8, LANES if kind == "scalar" else tn), lambda i, j, kk: (i, j)) for kind in tile_sums]),
        out_shape=([jax.ShapeDtypeStruct((m, n), dt) for dt in out_dtypes]
                   + [jax.ShapeDtypeStruct((m // tm * 8, n // tn * LANES if kind == "scalar" else n), F32)
                      for kind in tile_sums]),
        scratch_shapes=[pltpu.VMEM((tm, tn), F32)] if nk > 1 else [],
        compiler_params=_params("parallel", "parallel", "arbitrary"),
    )(a, b, *[e for e, _ in extras])


def _rowwise(name, fn, rows, consts, row_outs, acc_outs=(), tm=256):
    norm = [r if isinstance(r, tuple) else (r, r.shape[1], 0) for r in rows]
    t = norm[0][0].shape[0]
    tm = _tile(t, tm)
    nr, nc, no = len(norm), len(consts), len(row_outs)

    def body(*refs):
        outs = fn(*[r[...] for r in refs[:nr + nc]])
        if not isinstance(outs, (tuple, list)):
            outs = (outs,)
        o_refs, a_refs = refs[nr + nc:nr + nc + no], refs[nr + nc + no:]
        for r, v in zip(o_refs, outs[:no]):
            r[...] = v.astype(r.dtype)
        if a_refs:
            i = pl.program_id(0)

            @pl.when(i == 0)
            def _():
                for r, v in zip(a_refs, outs[no:]):
                    r[...] = v

            @pl.when(i > 0)
            def _():
                for r, v in zip(a_refs, outs[no:]):
                    r[...] += v

    in_specs = [pl.BlockSpec((tm, w), functools.partial(lambda i, cb: (i, cb), cb=cb)) for _, w, cb in norm]
    in_specs += [pl.BlockSpec(c.shape, functools.partial(lambda i, nd: (0,) * nd, nd=c.ndim)) for c in consts]
    out_specs = [pl.BlockSpec((tm, w), lambda i: (i, 0)) for w, _ in row_outs]
    out_specs += [pl.BlockSpec(s, functools.partial(lambda i, nd: (0,) * nd, nd=len(s))) for s in acc_outs]
    out_shape = [jax.ShapeDtypeStruct((t, w), dt) for w, dt in row_outs]
    out_shape += [jax.ShapeDtypeStruct(s, F32) for s in acc_outs]
    return pl.pallas_call(
        body, name=name, grid=(t // tm,), in_specs=in_specs, out_specs=out_specs, out_shape=out_shape,
        compiler_params=_params("arbitrary"),
    )(*[r[0] for r in norm], *consts)


def _rms(x, g):
    return x * lax.rsqrt(jnp.mean(x * x, axis=-1, keepdims=True) + RMS_EPS) * g


@jax.custom_vjp
def _head_sums(x, ones_blocks):
    parts = [jnp.dot(x[:, j:j + LANES], ones_blocks, precision=F32_DOT, preferred_element_type=F32)
             for j in range(0, x.shape[1], LANES)]
    return jnp.concatenate(parts, axis=1)


_head_sums.defvjp(lambda x, ones_blocks: (_head_sums(x, ones_blocks), ones_blocks),
                  lambda ones_blocks, ct: (_head_sums(ct, ones_blocks), None))


def _head_rms(x, g, ones_blocks):
    return x * lax.rsqrt(_head_sums(x * x, ones_blocks) * (1.0 / HEAD_DIM) + RMS_EPS) * g


def _gelu(x):
    return x * (0.5 * (1.0 + jnp.tanh(math.sqrt(2.0 / math.pi) * (x + 0.044715 * (x * x * x)))))


def _mixed(sb, y_ssm, gate_pre, g_attn, g_ssm):
    ssm = _gelu(y_ssm) * jax.nn.sigmoid(gate_pre)
    return jnp.concatenate([_rms(sb, g_attn), _rms(ssm, g_ssm)], axis=-1)


def _softplus(z):
    return jnp.maximum(z, 0.0) + jnp.log(1.0 + jnp.exp(-jnp.abs(z)))


def _running_sums(x, tri):
    return jnp.dot(x.astype(BF16), tri, preferred_element_type=F32)


def _dot_nt(a, b, **kw):
    return lax.dot_general(a, b, _DIMS["nt"], preferred_element_type=F32, **kw)


def _dot_tn(a, b, **kw):
    return lax.dot_general(a, b, _DIMS["tn"], preferred_element_type=F32, **kw)


ATTN_BQ, ATTN_BK = 2048, 256
HEAD_LANES = tuple(slice(h * HEAD_DIM, (h + 1) * HEAD_DIM) for h in range(LANES // HEAD_DIM))


def _attn_fwd(qs, kn, v, *, batch, seq, bq, bk):
    width = qs.shape[1]
    bq = _tile(seq, bq)
    bk = _tile(bq, bk)
    nq, kpq = seq // bq, bq // bk

    def body(q_ref, k_ref, v_ref, o_ref, c_ref):
        row = lax.broadcasted_iota(jnp.int32, (bq, bk), 0)
        col = lax.broadcasted_iota(jnp.int32, (bq, bk), 1)
        tri = (lax.broadcasted_iota(jnp.int32, (bk, bk), 0) >= lax.broadcasted_iota(jnp.int32, (bk, bk), 1)).astype(BF16)

        def q_block(qi, carry):
            r0 = pl.multiple_of(qi * bq, bq)
            qh = [q_ref[pl.ds(r0, bq), ln] for ln in HEAD_LANES]

            def tile(k0, state, top=0):
                diag = top is not None
                top = top or 0
                msk = (col < row)[:bq - top] if diag else None
                new = []
                for h, ln in enumerate(HEAD_LANES):
                    o, c = state[2 * h], state[2 * h + 1]
                    z = _dot_nt(qh[h][top:], k_ref[pl.ds(k0, bk), ln])
                    sp = _softplus(z)
                    if diag:
                        sp = jnp.where(msk, sp, 0.0)
                    r = _running_sums(sp, tri)
                    a = jnp.exp(z - r - c[top:])
                    if diag:
                        a = jnp.where(msk, a, 0.0)
                    o_new = o[top:] + jnp.dot(a.astype(BF16), v_ref[pl.ds(k0, bk), ln], preferred_element_type=F32)
                    c_new = c[top:] + r[:, 0:1]
                    if top:
                        o_new, c_new = jnp.concatenate([o[:top], o_new]), jnp.concatenate([c[:top], c_new])
                    new += [o_new, c_new]
                return tuple(new)

            state = (jnp.zeros((bq, HEAD_DIM), F32), jnp.zeros((bq, 1), F32)) * len(HEAD_LANES)
            for d in reversed(range(kpq)):
                state = tile(pl.multiple_of(r0 + d * bk, bk), state, top=d * bk)
            state = lax.fori_loop(0, qi * kpq, lambda it, st: tile(pl.multiple_of(r0 - (it + 1) * bk, bk), st, None),
                                  state)
            for h, ln in enumerate(HEAD_LANES):
                o_ref[pl.ds(r0, bq), ln] = state[2 * h]
                c_ref[pl.ds(r0, bq), ln] = jnp.broadcast_to(state[2 * h + 1], (bq, HEAD_DIM))
            return carry

        lax.fori_loop(0, nq, q_block, 0)

    spec = pl.BlockSpec((seq, LANES), lambda b, h: (b, h))
    shape = jax.ShapeDtypeStruct((batch * seq, width), F32)
    return pl.pallas_call(
        body, name="attn_fwd", grid=(batch, width // LANES), in_specs=[spec, spec, spec], out_specs=[spec, spec],
        out_shape=[shape, shape], compiler_params=_params("parallel", "parallel"),
    )(qs, kn, v)


def _attn_bwd(qs, kn, v, c_tot, do, *, batch, seq, bq, bk, after):
    width = qs.shape[1]
    bq = _tile(seq, bq)
    bk = _tile(bq, bk)
    nq, kpq = seq // bq, bq // bk

    def body(q_ref, k_ref, v_ref, c_ref, do_ref, after_ref, dq_ref, dk_ref, dv_ref):
        row = lax.broadcasted_iota(jnp.int32, (bq, bk), 0)
        col = lax.broadcasted_iota(jnp.int32, (bq, bk), 1)
        sq_row = lax.broadcasted_iota(jnp.int32, (bk, bk), 0)
        sq_col = lax.broadcasted_iota(jnp.int32, (bk, bk), 1)
        tri = (sq_row >= sq_col).astype(BF16)
        tri_t = (sq_row <= sq_col).astype(BF16)
        dk_ref[...] = jnp.zeros_like(dk_ref)
        dv_ref[...] = jnp.zeros_like(dv_ref)

        def q_block(qi, carry):
            r0 = pl.multiple_of(qi * bq, bq)
            qh = [q_ref[pl.ds(r0, bq), ln] for ln in HEAD_LANES]
            d_out = [do_ref[pl.ds(r0, bq), ln].astype(BF16) for ln in HEAD_LANES]
            c_all = [c_ref[pl.ds(r0, bq), ln][:, 0:1] for ln in HEAD_LANES]

            def tile(k0, state, top=0):
                diag = top is not None
                top = top or 0
                last = diag and top == bq - bk
                msk = (col < row)[:bq - top] if diag else None
                new = []
                for h, ln in enumerate(HEAD_LANES):
                    c_left, g_left, dq = state[3 * h:3 * h + 3]
                    q, d_o = qh[h][top:], d_out[h][top:]
                    k = k_ref[pl.ds(k0, bk), ln]
                    z = _dot_nt(q, k)
                    e = jnp.exp(-jnp.abs(z))
                    sp = jnp.maximum(z, 0.0) + jnp.log(1.0 + e)
                    sig = jnp.exp(z - sp)
                    if diag:
                        sp = jnp.where(msk, sp, 0.0)
                    r = _running_sums(sp, tri)
                    c_new = c_left[top:] + r[:, 0:1]
                    a = jnp.exp(z - r - (0.0 if last else c_all[h][top:] - c_new))
                    if diag:
                        a = jnp.where(msk, a, 0.0)
                    g = a * _dot_nt(d_o, v_ref[pl.ds(k0, bk), ln])
                    pg = _running_sums(g, tri_t)
                    dz = g - sig * (g_left[top:] + pg)
                    if diag:
                        dz = jnp.where(msk, dz, 0.0)
                    dz = dz.astype(BF16)
                    dk_ref[pl.ds(k0, bk), ln] += _dot_tn(dz, q)
                    dv_ref[pl.ds(k0, bk), ln] += _dot_tn(a.astype(BF16), d_o)
                    g_new = g_left[top:] + pg[:, bk - 1:bk]
                    dq_new = dq[top:] + jnp.dot(dz, k, preferred_element_type=F32)
                    if top:
                        c_new = jnp.concatenate([c_left[:top], c_new])
                        g_new = jnp.concatenate([g_left[:top], g_new])
                        dq_new = jnp.concatenate([dq[:top], dq_new])
                    new += [c_new, g_new, dq_new]
                return tuple(new)

            zero = jnp.zeros((bq, 1), F32)
            init = (zero, zero, jnp.zeros((bq, HEAD_DIM), F32)) * len(HEAD_LANES)
            state = lax.fori_loop(0, qi * kpq, lambda it, st: tile(pl.multiple_of(it * bk, bk), st, None), init)
            for d in range(kpq):
                state = tile(pl.multiple_of(r0 + d * bk, bk), state, top=d * bk)
            for h, ln in enumerate(HEAD_LANES):
                dq_ref[pl.ds(r0, bq), ln] = state[3 * h + 2]
            return carry

        lax.fori_loop(0, nq, q_block, 0)

    spec = pl.BlockSpec((seq, LANES), lambda b, h: (b, h))
    shape = jax.ShapeDtypeStruct((batch * seq, width), F32)
    return pl.pallas_call(
        body, name="attn_bwd", grid=(batch, width // LANES),
        in_specs=[spec] * 5 + [pl.BlockSpec(memory_space=pl.ANY)], out_specs=[spec] * 3,
        out_shape=[shape] * 3, compiler_params=_params("parallel", "parallel"),
    )(qs, kn, v, c_tot, do, after)


def _s5_operators(lam_re, lam_im, log_dt, b_re, b_im, c_re, c_im, d_skip):
    groups, n_state, n_ch = b_re.shape
    cs = SSM_CHUNK
    dt = jnp.exp(log_dt)[:, None]
    steps = jnp.arange(cs + 1, dtype=F32)[None, :, None]
    mag = jnp.exp(steps * (lam_re * dt)[:, None, :])
    ang = steps * (lam_im * dt)[:, None, :]
    pw_re, pw_im = mag * jnp.cos(ang), mag * jnp.sin(ang)
    num_re, num_im = pw_re[:, 1] - 1.0, pw_im[:, 1]
    den = lam_re * lam_re + lam_im * lam_im
    cf_re = (num_re * lam_re + num_im * lam_im) / den
    cf_im = (num_im * lam_re - num_re * lam_im) / den
    bb_re = cf_re[:, :, None] * b_re - cf_im[:, :, None] * b_im
    bb_im = cf_re[:, :, None] * b_im + cf_im[:, :, None] * b_re
    width = cs * n_ch
    ct_re, ct_im = c_re.transpose(0, 2, 1), c_im.transpose(0, 2, 1)

    def c_times_powers(first):
        pr = pw_re[:, first:first + cs].transpose(0, 2, 1)[:, :, :, None]
        pi = pw_im[:, first:first + cs].transpose(0, 2, 1)[:, :, :, None]
        re = pr * ct_re[:, :, None, :] - pi * ct_im[:, :, None, :]
        im = pr * ct_im[:, :, None, :] + pi * ct_re[:, :, None, :]
        return re.reshape(groups, n_state, width), im.reshape(groups, n_state, width)

    w_re, w_im = c_times_powers(0)
    kt_row = (jnp.einsum("gpi,gpw->giw", bb_re, w_re, precision=F32_DOT)
              - jnp.einsum("gpi,gpw->giw", bb_im, w_im, precision=F32_DOT))
    kt_row = kt_row + jnp.pad(jnp.eye(n_ch, dtype=F32)[None] * d_skip[:, None, :], ((0, 0), (0, 0), (0, width - n_ch)))
    rp_re, rp_im = pw_re[:, cs - 1::-1][:, :cs], pw_im[:, cs - 1::-1][:, :cs]
    bm_re = rp_re[:, :, None, :] * bb_re.transpose(0, 2, 1)[:, None] - rp_im[:, :, None, :] * bb_im.transpose(0, 2, 1)[:, None]
    bm_im = rp_re[:, :, None, :] * bb_im.transpose(0, 2, 1)[:, None] + rp_im[:, :, None, :] * bb_re.transpose(0, 2, 1)[:, None]
    b_mat = jnp.concatenate([bm_re, bm_im], axis=-1).reshape(groups, width, 2 * n_state)
    w1_re, w1_im = c_times_powers(1)
    c_mat = jnp.concatenate([w1_re, -w1_im], axis=1)
    la = jnp.concatenate([pw_re[:, cs], pw_re[:, cs]], axis=-1)[:, None, :]
    lb = jnp.concatenate([-pw_im[:, cs], pw_im[:, cs]], axis=-1)[:, None, :]
    return kt_row, b_mat, c_mat, la, lb


GROUPS_PER_BLOCK = LANES // SSM_GROUP


def _tokens_to_groups(name, u, col_block, width):
    t = u.shape[0]
    n = t // SSM_CHUNK
    ch = SSM_CHUNK * SSM_GROUP
    blocks = width // LANES

    def body(u_ref, o_ref):
        for s in range(SSM_CHUNK):
            rows = u_ref[pl.ds(s, n, stride=SSM_CHUNK), :]
            for g in range(GROUPS_PER_BLOCK):
                o_ref[g, :, s * SSM_GROUP:(s + 1) * SSM_GROUP] = rows[:, g * SSM_GROUP:(g + 1) * SSM_GROUP]

    return pl.pallas_call(
        body, name=name, grid=(blocks,),
        in_specs=[pl.BlockSpec((t, LANES), lambda j: (0, col_block * blocks + j))],
        out_specs=pl.BlockSpec((GROUPS_PER_BLOCK, n, ch), lambda j: (j, 0, 0)),
        out_shape=jax.ShapeDtypeStruct((width // SSM_GROUP, n, ch), F32), compiler_params=_params("parallel"),
    )(u)


def _groups_to_tokens(name, ug):
    groups, n, ch = ug.shape

    def body(g_ref, o_ref, rows_ref):
        for s in range(SSM_CHUNK):
            for g in range(GROUPS_PER_BLOCK):
                rows_ref[s % 2, :, g * SSM_GROUP:(g + 1) * SSM_GROUP] = g_ref[g, :, s * SSM_GROUP:(s + 1) * SSM_GROUP]
            o_ref[pl.ds(s, n, stride=SSM_CHUNK), :] = rows_ref[s % 2]

    return pl.pallas_call(
        body, name=name, grid=(groups // GROUPS_PER_BLOCK,),
        in_specs=[pl.BlockSpec((GROUPS_PER_BLOCK, n, ch), lambda j: (j, 0, 0))],
        out_specs=pl.BlockSpec((n * SSM_CHUNK, LANES), lambda j: (0, j)),
        out_shape=jax.ShapeDtypeStruct((n * SSM_CHUNK, groups * SSM_GROUP), F32),
        scratch_shapes=[pltpu.VMEM((2, n, LANES), F32)], compiler_params=_params("parallel"),
    )(ug)


SCAN_ROWS = 8


def _toeplitz_to(tm_ref, g, kt_row):
    width = kt_row.shape[1]
    tm_ref[g] = jnp.zeros((width, width), F32)
    for s in range(SSM_CHUNK):
        tm_ref[g, s * SSM_GROUP:(s + 1) * SSM_GROUP, s * SSM_GROUP:] = kt_row[:, :width - s * SSM_GROUP]


def _lam_powers(la, lb, reverse):
    if reverse:
        lb = -lb

    def mul(p, q):
        return p[0] * q[0] - p[1] * q[1], p[0] * q[1] + p[1] * q[0]

    p1 = (la, lb)
    p2 = mul(p1, p1)
    p3 = mul(p2, p1)
    p4 = mul(p2, p2)
    rows = [p1, p2, p3, p4, mul(p4, p1), mul(p4, p2), mul(p4, p3), mul(p4, p4)]
    if reverse:
        rows = rows[::-1]
    idx = lax.broadcasted_iota(jnp.int32, (SCAN_ROWS, la.shape[1]), 0)
    tab_a = sum(jnp.where(idx == j, r[0], 0.0) for j, r in enumerate(rows))
    tab_b = sum(jnp.where(idx == j, r[1], 0.0) for j, r in enumerate(rows))
    return (p1, p2, p4), (tab_a, tab_b), idx


def _scan_block(e, carry, steps, table, idx, half, reverse):
    n = SCAN_ROWS
    for d, (pa, pb) in zip((1, 2, 4), steps):
        sh = pltpu.roll(e, n - d if reverse else d, 0)
        sh = jnp.where(idx < n - d if reverse else idx >= d, sh, 0.0)
        e = e + pa * sh + pb * pltpu.roll(sh, half, 1)
    tab_a, tab_b = table
    e = e + tab_a * carry + tab_b * pltpu.roll(carry, half, 1)
    shifted = jnp.where(idx == (n - 1 if reverse else 0), carry, pltpu.roll(e, n - 1 if reverse else 1, 0))
    edge = e[0:1] if reverse else e[n - 1:n]
    return shifted, jnp.broadcast_to(edge, e.shape)


def _s5_fwd(ug, kt_row, b_mat, c_mat, la, lb, *, batch, gb=8):
    groups, n, ch = ug.shape
    p2 = b_mat.shape[2]
    gb = _tile(groups, gb)
    nch = n // batch
    nblk = nch // SCAN_ROWS

    def body(u_ref, k_ref, b_ref, c_ref, la_ref, lb_ref, y_ref, x_ref, s_ref, tm_ref):
        for g in range(gb):
            _toeplitz_to(tm_ref, g, k_ref[g])
            s_ref[g] = jnp.dot(u_ref[g], b_ref[g], precision=F32_DOT, preferred_element_type=F32)
        powers = [_lam_powers(la_ref[g], lb_ref[g], False) for g in range(gb)]

        def step(blk, carries):
            new = []
            for g in range(gb):
                steps, table, idx = powers[g]
                for b in range(batch):
                    rows = pl.ds(pl.multiple_of(b * nch + blk * SCAN_ROWS, SCAN_ROWS), SCAN_ROWS)
                    x_in, carry = _scan_block(s_ref[g, rows, :], carries[g * batch + b], steps, table, idx, p2 // 2, False)
                    x_ref[g, rows, :] = x_in
                    new.append(carry)
            return tuple(new)

        lax.fori_loop(0, nblk, step, tuple(jnp.zeros((SCAN_ROWS, p2), F32) for _ in range(gb * batch)))
        for g in range(gb):
            y_ref[g] = (jnp.dot(u_ref[g], tm_ref[g], precision=F32_DOT, preferred_element_type=F32)
                        + jnp.dot(x_ref[g], c_ref[g], precision=F32_DOT, preferred_element_type=F32))

    def spec(a, b):
        return pl.BlockSpec((gb, a, b), lambda i: (i, 0, 0))

    return pl.pallas_call(
        body, name="s5_fwd", grid=(groups // gb,),
        in_specs=[spec(n, ch), spec(SSM_GROUP, ch), spec(ch, p2), spec(p2, ch), spec(1, p2), spec(1, p2)],
        out_specs=[spec(n, ch), spec(n, p2)],
        out_shape=[jax.ShapeDtypeStruct((groups, n, ch), F32), jax.ShapeDtypeStruct((groups, n, p2), F32)],
        scratch_shapes=[pltpu.VMEM((gb, n, p2), F32), pltpu.VMEM((gb, ch, ch), F32)],
        compiler_params=_params("parallel"),
    )(ug, kt_row, b_mat, c_mat, la, lb)


def _s5_bwd(ug, dyg, xin, kt_row, b_mat, c_mat, la, lb, *, batch, gb=8):
    groups, n, ch = ug.shape
    p2 = b_mat.shape[2]
    gb = _tile(groups, gb)
    nch = n // batch
    nblk = nch // SCAN_ROWS

    def body(u_ref, dy_ref, x_ref, k_ref, b_ref, c_ref, la_ref, lb_ref,
             du_ref, dk_ref, db_ref, dc_ref, dla_ref, dlb_ref, dx_ref, ds_ref, tm_ref):
        for g in range(gb):
            _toeplitz_to(tm_ref, g, k_ref[g])
            dx_ref[g] = _dot_nt(dy_ref[g], c_ref[g], precision=F32_DOT)
        powers = [_lam_powers(la_ref[g], lb_ref[g], True) for g in range(gb)]

        def step(it, carries):
            new = []
            for g in range(gb):
                steps, table, idx = powers[g]
                for b in range(batch):
                    rows = pl.ds(pl.multiple_of(b * nch + (nblk - 1 - it) * SCAN_ROWS, SCAN_ROWS), SCAN_ROWS)
                    d_s, carry = _scan_block(dx_ref[g, rows, :], carries[g * batch + b], steps, table, idx, p2 // 2, True)
                    ds_ref[g, rows, :] = d_s
                    new.append(carry)
            return tuple(new)

        lax.fori_loop(0, nblk, step, tuple(jnp.zeros((SCAN_ROWS, p2), F32) for _ in range(gb * batch)))
        for g in range(gb):
            u, dy, ds, x = u_ref[g], dy_ref[g], ds_ref[g], x_ref[g]
            du_ref[g] = _dot_nt(dy, tm_ref[g], precision=F32_DOT) + _dot_nt(ds, b_ref[g], precision=F32_DOT)
            tm_ref[g] = _dot_tn(u, dy, precision=F32_DOT)
            dk_ref[g] = tm_ref[g, 0:SSM_GROUP, :]
            for s in range(1, SSM_CHUNK):
                dk_ref[g, :, :ch - s * SSM_GROUP] += tm_ref[g, s * SSM_GROUP:(s + 1) * SSM_GROUP, s * SSM_GROUP:]
            db_ref[g] = _dot_tn(u, ds, precision=F32_DOT)
            dc_ref[g] = _dot_tn(x, dy, precision=F32_DOT)
            dla_ref[g] = jnp.sum(ds * x, axis=0, keepdims=True)
            dlb_ref[g] = jnp.sum(ds * pltpu.roll(x, p2 // 2, 1), axis=0, keepdims=True)

    def spec(a, b):
        return pl.BlockSpec((gb, a, b), lambda i: (i, 0, 0))

    def shape(a, b):
        return jax.ShapeDtypeStruct((groups, a, b), F32)

    return pl.pallas_call(
        body, name="s5_bwd", grid=(groups // gb,),
        in_specs=[spec(n, ch), spec(n, ch), spec(n, p2), spec(SSM_GROUP, ch), spec(ch, p2), spec(p2, ch), spec(1, p2),
                  spec(1, p2)],
        out_specs=[spec(n, ch), spec(SSM_GROUP, ch), spec(ch, p2), spec(p2, ch), spec(1, p2), spec(1, p2)],
        out_shape=[shape(n, ch), shape(SSM_GROUP, ch), shape(ch, p2), shape(p2, ch), shape(1, p2), shape(1, p2)],
        scratch_shapes=[pltpu.VMEM((gb, n, p2), F32), pltpu.VMEM((gb, n, p2), F32), pltpu.VMEM((gb, ch, ch), F32)],
        compiler_params=_params("parallel"),
    )(ug, dyg, xin, kt_row, b_mat, c_mat, la, lb)


def _block(ref, axis, j, size):
    start = j * size if isinstance(j, int) else pl.multiple_of(j * size, size)
    return ref.at[pl.ds(start, size), :] if axis == 0 else ref.at[:, pl.ds(start, size)]


def _chip_exchange_copies(mode, axes, srcs, lands, send_sems, recv_sems, local_sems):
    x, y, c = lax.axis_index("x"), lax.axis_index("y"), lax.axis_index("c")
    everyone = mode == "all"
    me = 4 * x + 2 * y + c if everyone else 2 * x + y
    n_peers = _exchange_peers(mode)
    local, sends, arrivals = [], [], []
    for w, axis in enumerate(axes):
        if mode == "gather":
            size = srcs[w].shape[axis]
            local.append(pltpu.make_async_copy(srcs[w], _block(lands[w], axis, me, size), local_sems.at[w]))
        elif mode == "scatter":
            size = srcs[w].shape[axis] // N_CHIPS
            local.append(pltpu.make_async_copy(_block(srcs[w], axis, me, size), lands[w].at[me], local_sems.at[w]))
        else:
            local.append(pltpu.make_async_copy(srcs[w], lands[w].at[me], local_sems.at[w]))
        for k in range(1, n_peers + 1):
            bits = k if everyone else 2 * k
            px = 1 - x if bits & 4 else x
            py = 1 - y if bits & 2 else y
            pc = 1 - c if bits & 1 else c
            peer = 4 * px + 2 * py + pc if everyone else 2 * px + py
            if mode == "gather":
                src, dst, arrive = srcs[w], _block(lands[w], axis, me, size), _block(lands[w], axis, peer, size)
            elif mode == "scatter":
                src, dst, arrive = _block(srcs[w], axis, peer, size), lands[w].at[me], lands[w].at[peer]
            else:
                src, dst, arrive = srcs[w], lands[w].at[me], lands[w].at[peer]
            sem = w * n_peers + k - 1
            for target, out in ((dst, sends), (arrive, arrivals)):
                out.append(pltpu.make_async_remote_copy(
                    src_ref=src, dst_ref=target, send_sem=send_sems.at[sem], recv_sem=recv_sems.at[sem],
                    device_id=(px, py, pc), device_id_type=MESH))
    return local, sends, arrivals


def _exchange_peers(mode):
    return N_DEV - 1 if mode == "all" else N_CHIPS - 1


def _chip_exchange_start(name, mode, items, after=None):
    n = len(items)
    n_after = 0 if after is None else 1
    axes = [axis for _, axis in items]
    hbm = pl.BlockSpec(memory_space=pltpu.HBM)
    sem = pl.BlockSpec(memory_space=pltpu.SEMAPHORE)
    lands = []
    for a, axis in items:
        shape = list(a.shape)
        if mode == "gather":
            shape[axis] *= N_CHIPS
        elif mode == "scatter":
            shape[axis] //= N_CHIPS
            shape = [N_CHIPS] + shape
        else:
            shape = [N_DEV] + shape
        lands.append(pltpu.with_memory_space_constraint(lax.empty(tuple(shape), a.dtype), pltpu.HBM))

    def body(*refs):
        srcs, land_refs = refs[:n], refs[n:2 * n]
        send_sems, recv_sems, local_sems = refs[2 * n + n_after:2 * n + n_after + 3]
        token = refs[-1]
        local, sends, _ = _chip_exchange_copies(mode, axes, srcs, land_refs, send_sems, recv_sems, local_sems)
        for cp in local + sends:
            cp.start()
        token[...] = jnp.zeros_like(token)

    n_sem = n * _exchange_peers(mode)
    outs = pl.pallas_call(
        body, name=name,
        out_shape=(pltpu.SemaphoreType.DMA((n_sem,)), pltpu.SemaphoreType.DMA((n_sem,)), pltpu.SemaphoreType.DMA((n,)),
                   *[pltpu.HBM(a.shape, a.dtype) for a, _ in items], *[pltpu.HBM(l.shape, l.dtype) for l in lands],
                   jax.ShapeDtypeStruct((8, LANES), F32)),
        in_specs=[hbm] * (2 * n) + [pl.BlockSpec(memory_space=pl.ANY)] * n_after,
        out_specs=(sem, sem, sem, *[hbm] * (2 * n), pl.BlockSpec(memory_space=pltpu.VMEM)),
        input_output_aliases={i: 3 + i for i in range(2 * n)},
        compiler_params=pltpu.CompilerParams(has_side_effects=pltpu.SideEffectType.DATAFLOW_SIDE_EFFECTING),
    )(*[pltpu.with_memory_space_constraint(a, pltpu.HBM) for a, _ in items], *lands, *([after] if n_after else []))
    return (mode, axes, outs[:3], outs[3:3 + n], outs[3 + n:3 + 2 * n]), outs[-1][0:1, 0:1]


def _chip_exchange_wait(name, handle, after):
    mode, axes, sems, srcs, lands = handle
    n = len(axes)
    after = list(after) if isinstance(after, (tuple, list)) else [after]
    hbm = pl.BlockSpec(memory_space=pltpu.HBM)
    sem = pl.BlockSpec(memory_space=pltpu.SEMAPHORE)

    def body(*refs):
        src_refs, land_refs = refs[:n], refs[n:2 * n]
        send_sems, recv_sems, local_sems = refs[2 * n:2 * n + 3]
        local, sends, arrivals = _chip_exchange_copies(mode, axes, src_refs, land_refs, send_sems, recv_sems, local_sems)
        for cp in sends:
            cp.wait_send()
        for cp in arrivals:
            cp.wait_recv()
        for cp in local:
            cp.wait()

    outs = pl.pallas_call(
        body, name=name,
        out_shape=(*[pltpu.HBM(a.shape, a.dtype) for a in srcs], *[pltpu.HBM(l.shape, l.dtype) for l in lands]),
        in_specs=[hbm] * (2 * n) + [sem] * 3 + [pl.BlockSpec(memory_space=pl.ANY)] * len(after), out_specs=[hbm] * (2 * n),
        input_output_aliases={i: i for i in range(2 * n)},
        compiler_params=pltpu.CompilerParams(has_side_effects=pltpu.SideEffectType.DATAFLOW_SIDE_EFFECTING),
    )(*srcs, *lands, *sems, *after)
    return outs[n:]


def _sum_slots(name, slots, tm=256):
    n_slots, r, c = slots.shape
    tm = _tile(r, tm)

    def body(*refs):
        acc = refs[0][...]
        for s_ref in refs[1:n_slots]:
            acc = acc + s_ref[...]
        refs[n_slots][...] = acc

    specs = [pl.BlockSpec((None, tm, c), functools.partial(lambda i, s: (s, i, 0), s=s)) for s in range(n_slots)]
    return pl.pallas_call(
        body, name=name, grid=(r // tm,), in_specs=specs, out_specs=pl.BlockSpec((tm, c), lambda i: (i, 0)),
        out_shape=jax.ShapeDtypeStruct((r, c), F32), compiler_params=_params("parallel"),
    )(*[slots] * n_slots)


def _swap_with_sibling(name, arrays):
    n = len(arrays)
    hbm = pl.BlockSpec(memory_space=pl.ANY)

    def body(*refs):
        ins, outs = refs[:n], refs[n:2 * n]
        send_sems, recv_sems = refs[2 * n:]
        sibling = (lax.axis_index("x"), lax.axis_index("y"), 1 - lax.axis_index("c"))
        copies = [pltpu.make_async_remote_copy(src_ref=ins[w], dst_ref=outs[w], send_sem=send_sems.at[w],
                                               recv_sem=recv_sems.at[w], device_id=sibling, device_id_type=MESH)
                  for w in range(n)]
        for cp in copies:
            cp.start()
        for cp in copies:
            cp.wait()

    return pl.pallas_call(
        body, name=name, in_specs=[hbm] * n, out_specs=[hbm] * n,
        out_shape=[jax.ShapeDtypeStruct(a.shape, a.dtype) for a in arrays],
        scratch_shapes=[pltpu.SemaphoreType.DMA((n,)), pltpu.SemaphoreType.DMA((n,))],
    )(*arrays)


def _adamw(g, w, m, v):
    m = ADAM_B1 * m + (1.0 - ADAM_B1) * g
    v = ADAM_B2 * v + (1.0 - ADAM_B2) * jnp.square(g)
    m_hat = m / (1.0 - ADAM_B1 ** ADAM_STEP)
    v_hat = v / (1.0 - ADAM_B2 ** ADAM_STEP)
    delta = -ADAM_LR * (m_hat / (jnp.sqrt(v_hat) + ADAM_EPS) + ADAM_WD * w)
    return delta, m, v


def _adamw_small(grads, ws, ms, vs):
    n = len(ws)

    def whole(a):
        return pl.BlockSpec(a.shape, functools.partial(lambda i, nd: (0,) * nd, nd=a.ndim))

    def body(*refs):
        for i in range(n):
            g, w, m, v = (refs[k * n + i][...] for k in range(4))
            for k, val in enumerate(_adamw(g, w, m, v)):
                refs[(4 + k) * n + i][...] = val

    outs = pl.pallas_call(
        body, name="adamw_small", grid=(1,), in_specs=[whole(a) for a in (*grads, *ws, *ms, *vs)],
        out_specs=[whole(w) for _ in range(3) for w in ws],
        out_shape=[jax.ShapeDtypeStruct(w.shape, F32) for _ in range(3) for w in ws],
        compiler_params=pltpu.CompilerParams(vmem_limit_bytes=VMEM_LIMIT),
    )(*grads, *ws, *ms, *vs)
    return outs[:n], outs[n:2 * n], outs[2 * n:]


def kernel(x, norm1_g, w_in, q_norm_g, k_norm_g, ssm_lambda_re, ssm_lambda_im, ssm_log_dt, ssm_b_re, ssm_b_im, ssm_c_re, ssm_c_im, ssm_d, w_glu, b_glu, attn_out_g, ssm_out_g, w_out, norm2_g, w_mlp_in, w_mlp_out, loss_target, m_norm1_g, m_w_in, m_q_norm_g, m_k_norm_g, m_ssm_lambda_re, m_ssm_lambda_im, m_ssm_log_dt, m_ssm_b_re, m_ssm_b_im, m_ssm_c_re, m_ssm_c_im, m_ssm_d, m_w_glu, m_b_glu, m_attn_out_g, m_ssm_out_g, m_w_out, m_norm2_g, m_w_mlp_in, m_w_mlp_out, v_norm1_g, v_w_in, v_q_norm_g, v_k_norm_g, v_ssm_lambda_re, v_ssm_lambda_im, v_ssm_log_dt, v_ssm_b_re, v_ssm_b_im, v_ssm_c_re, v_ssm_c_im, v_ssm_d, v_w_glu, v_b_glu, v_attn_out_g, v_ssm_out_g, v_w_out, v_norm2_g, v_w_mlp_in, v_w_mlp_out):
    batch, seq, d_model = x.shape
    tokens = batch * seq
    sb_width = w_in.shape[1]
    n_features = d_model

    big = [("w_in", w_in, m_w_in, v_w_in, 1), ("w_glu", w_glu, m_w_glu, v_w_glu, 0),
           ("w_out", w_out, m_w_out, v_w_out, 0), ("w_mlp_in", w_mlp_in, m_w_mlp_in, v_w_mlp_in, 1),
           ("w_mlp_out", w_mlp_out, m_w_mlp_out, v_w_mlp_out, 0)]
    small = [("norm1_g", norm1_g, m_norm1_g, v_norm1_g), ("q_norm_g", q_norm_g, m_q_norm_g, v_q_norm_g),
             ("k_norm_g", k_norm_g, m_k_norm_g, v_k_norm_g),
             ("ssm_lambda_re", ssm_lambda_re, m_ssm_lambda_re, v_ssm_lambda_re),
             ("ssm_lambda_im", ssm_lambda_im, m_ssm_lambda_im, v_ssm_lambda_im),
             ("ssm_log_dt", ssm_log_dt, m_ssm_log_dt, v_ssm_log_dt),
             ("ssm_b_re", ssm_b_re, m_ssm_b_re, v_ssm_b_re), ("ssm_b_im", ssm_b_im, m_ssm_b_im, v_ssm_b_im),
             ("ssm_c_re", ssm_c_re, m_ssm_c_re, v_ssm_c_re), ("ssm_c_im", ssm_c_im, m_ssm_c_im, v_ssm_c_im),
             ("ssm_d", ssm_d, m_ssm_d, v_ssm_d), ("b_glu", b_glu, m_b_glu, v_b_glu),
             ("attn_out_g", attn_out_g, m_attn_out_g, v_attn_out_g), ("ssm_out_g", ssm_out_g, m_ssm_out_g, v_ssm_out_g),
             ("norm2_g", norm2_g, m_norm2_g, v_norm2_g)]

    gather_in, tok_in = _chip_exchange_start("gather_w_in_start", "gather", [(w_in.astype(BF16), 1)])
    gather_rest, tok_rest = _chip_exchange_start(
        "gather_rest_start", "gather", [(w.astype(BF16), axis) for _, w, _, _, axis in big[1:]], after=tok_in)

    x2 = x.reshape(tokens, d_model)
    tgt2 = loss_target.reshape(tokens, d_model)
    g1, g2 = norm1_g[None, :], norm2_g[None, :]
    g_attn, g_ssm, bias_glu = attn_out_g[None, :], ssm_out_g[None, :], b_glu[None, :]
    heads = sb_width // HEAD_DIM
    qk_scale = 1.0 / math.sqrt(HEAD_DIM)
    gq, gk = (jnp.tile(q_norm_g, heads) * qk_scale)[None, :], jnp.tile(k_norm_g, heads)[None, :]
    lane_head = jnp.arange(LANES) // HEAD_DIM
    ones_blocks = (lane_head[:, None] == lane_head[None, :]).astype(F32)

    (xn,) = _rowwise("norm1", _rms, [x2], [g1 + tok_rest], [(d_model, BF16)])
    s5_params = (ssm_lambda_re, ssm_lambda_im, ssm_log_dt, ssm_b_re, ssm_b_im, ssm_c_re, ssm_c_im, ssm_d)
    (kt_row, b_mat, c_mat, la, lb), s5_vjp = jax.vjp(_s5_operators, *s5_params)
    (wf_in,) = _chip_exchange_wait("gather_w_in_wait", gather_in, [xn, b_mat, c_mat])
    (proj,) = _mm("proj_in", xn, wf_in, "nn")

    def qkv_fn(q, k, v, gq_, gk_, ones):
        return _head_rms(q, gq_, ones), _head_rms(k, gk_, ones), v

    qn, kn, vb = _rowwise("qk_norm", qkv_fn, [(proj, sb_width, 0), (proj, sb_width, 1), (proj, sb_width, 2)],
                          [gq, gk, ones_blocks], [(sb_width, BF16)] * 3, tm=512)
    sb, c_tot = _attn_fwd(qn, kn, vb, batch=batch, seq=seq, bq=ATTN_BQ, bk=ATTN_BK)
    ug = _tokens_to_groups("u_to_groups", proj, 3, sb_width)
    yg, xin = _s5_fwd(ug, kt_row, b_mat, c_mat, la, lb, batch=batch)
    y_ssm = _groups_to_tokens("y_to_tokens", yg)

    wf_glu, wf_out, wf_mlp_in, wf_mlp_out = _chip_exchange_wait("gather_rest_wait", gather_rest, [y_ssm, sb])
    (gate_pre,) = _mm("glu_gate", y_ssm, wf_glu, "nn", a_fn=_gelu, extras=[(bias_glu, "row")],
                      epilogue=lambda acc, b: acc + b)
    (mixed,) = _rowwise("mix_norm", _mixed, [sb, y_ssm, gate_pre], [g_attn, g_ssm], [(2 * sb_width, BF16)])
    def out_head(acc, r, g):
        h = acc + r
        return h, _rms(h, g)

    h1, hn = _mm("proj_out", mixed, wf_out, "nn", extras=[(x2, "tile"), (g2, "row")], epilogue=out_head,
                 out_dtypes=(F32, BF16), full_rows=True)
    def mlp_act(acc):
        r = jnp.maximum(acc, 0.0)
        return r * r, r

    act, act_root = _mm("mlp_in", hn, wf_mlp_in, "nn", epilogue=mlp_act, out_dtypes=(BF16, BF16))
    inv_n = 1.0 / n_features

    def loss_head(acc, r, t):
        d = ((acc + r) - t) * inv_n
        return d, d, jnp.sum(d * d, keepdims=True) * (0.5 * n_features)

    dy, dy_b, loss_tiles = _mm("mlp_out_loss", act, wf_mlp_out, "nn", extras=[(h1, "tile"), (tgt2, "tile")],
                               epilogue=loss_head, out_dtypes=(F32, BF16), tile_sums=("scalar",))
    loss_part = jnp.sum(loss_tiles)

    (dw_mlp_out,) = _mm("dw_mlp_out", act, dy_b, "tn")
    (dpre,) = _mm("d_mlp_act", dy_b, wf_mlp_out, "nt", extras=[(act_root, "tile")],
                  epilogue=lambda acc, r: acc * (2.0 * r.astype(F32)), out_dtypes=(BF16,))
    (dw_mlp_in,) = _mm("dw_mlp_in", hn, dpre, "tn")
    scatter_mlp, tok_mlp = _chip_exchange_start("scatter_mlp_start", "scatter", [(dw_mlp_in, 1), (dw_mlp_out, 0)])
    def norm_bwd(dn, res, hx, g):
        _, vjp = jax.vjp(_rms, hx, g)
        dh, dg = vjp(dn)
        return res + dh, dg

    dh1, dg_tiles = _mm("d_norm2_in", dpre, wf_mlp_in, "nt", extras=[(dy, "tile"), (h1, "tile"), (g2 + tok_mlp, "row")],
                        epilogue=norm_bwd, tile_sums=("row",), full_rows=True)
    dg_norm2 = jnp.sum(dg_tiles, axis=0, keepdims=True)
    (dmixed,) = _mm("d_mixed", dh1, wf_out, "nt")
    (dw_out,) = _mm("dw_out", mixed, dh1, "tn")

    def mixed_bwd(dm, sb_, ys, gp, ga, gs):
        _, vjp = jax.vjp(lambda a, act, b, c, d: jnp.concatenate(
            [_rms(a, c), _rms(act * jax.nn.sigmoid(b), d)], axis=-1), sb_, _gelu(ys), gp, ga, gs)
        dsb_, dact, dgp_, dga, dgs = vjp(dm)
        return dsb_, dgp_, dact, dga, dgs, jnp.sum(dgp_, axis=0, keepdims=True)

    dsb, dgate_pre, dact_part, dg_attn, dg_ssm, db_glu = _rowwise(
        "mix_norm_bwd", mixed_bwd, [dmixed, sb, y_ssm, gate_pre], [g_attn, g_ssm],
        [(sb_width, F32), (sb_width, BF16), (sb_width, F32)], [(1, sb_width)] * 3)

    def gelu_bwd(acc, part, ys):
        _, vjp = jax.vjp(_gelu, ys)
        return vjp(acc + part)[0]

    (dy_ssm,) = _mm("d_glu_in", dgate_pre, wf_glu, "nt", extras=[(dact_part, "tile"), (y_ssm, "tile")], epilogue=gelu_bwd)
    (dw_glu,) = _mm("dw_glu", y_ssm, dgate_pre, "tn", a_fn=_gelu)
    scatter_mix, tok_mix = _chip_exchange_start("scatter_mix_start", "scatter", [(dw_glu, 0), (dw_out, 0)])

    dug, dkt_row, db_mat, dc_mat, dla, dlb = _s5_bwd(ug, _tokens_to_groups("dy_to_groups", dy_ssm, 0, sb_width), xin,
                                                     kt_row, b_mat, c_mat, la, lb + tok_mix, batch=batch)
    du = _groups_to_tokens("du_to_tokens", dug)
    ds5 = s5_vjp((dkt_row, db_mat, dc_mat, dla, dlb))

    def pack(parts):
        flat = jnp.concatenate([p.reshape(-1) for p in parts])
        rows = -(-flat.shape[0] // (8 * LANES)) * 8
        return jnp.pad(flat, (0, rows * LANES - flat.shape[0])).reshape(rows, LANES)

    def unpack(packed, names):
        flat, out, off = packed.reshape(-1), {}, 0
        for name in names:
            shape = small_shapes[name]
            size = math.prod(shape)
            out[name] = flat[off:off + size].reshape(shape)
            off += size
        return out, flat[off]

    small_shapes = {name: w.shape for name, w, _, _ in small}
    early_names = ["ssm_lambda_re", "ssm_lambda_im", "ssm_log_dt", "ssm_b_re", "ssm_b_im", "ssm_c_re", "ssm_c_im", "ssm_d",
                   "b_glu", "attn_out_g", "ssm_out_g", "norm2_g"]
    late_names = ["norm1_g", "q_norm_g", "k_norm_g"]
    early = pack([*ds5, db_glu[0], dg_attn[0], dg_ssm[0], dg_norm2[0], loss_part])
    early_exchange, _ = _chip_exchange_start("small_early_start", "all", [(early, 0)])

    dqn, dkn, dv = _attn_bwd(qn, kn, vb, c_tot, dsb, batch=batch, seq=seq, bq=ATTN_BQ, bk=ATTN_BK,
                             after=early_exchange[3][0])
    (early_slots,) = _chip_exchange_wait("small_early_wait", early_exchange, dqn)
    small_g, loss = unpack(_sum_slots("sum_small_early", early_slots), early_names)

    def qk_bwd(q, k, dq_, dk_, dv_, du_, gq_, gk_, ones):
        _, vjp_q = jax.vjp(lambda a, g: _head_rms(a, g, ones), q, gq_)
        _, vjp_k = jax.vjp(lambda a, g: _head_rms(a, g, ones), k, gk_)
        dq, dgq = vjp_q(dq_)
        dk, dgk = vjp_k(dk_)
        return jnp.concatenate([dq, dk, dv_, du_], axis=1), dgq, dgk

    dproj, dgq, dgk = _rowwise("qk_norm_bwd", qk_bwd, [(proj, sb_width, 0), (proj, sb_width, 1), dqn, dkn, dv, du],
                               [gq, gk, ones_blocks], [(4 * sb_width, BF16)], [(1, sb_width)] * 2, tm=512)
    (dw_in,) = _mm("dw_in", xn, dproj, "tn")
    scatter_in, tok_w_in = _chip_exchange_start("scatter_in_start", "scatter", [(dw_in, 1)])
    dx, dg_tiles = _mm("d_norm1_in", dproj, wf_in, "nt", extras=[(dh1, "tile"), (x2, "tile"), (g1 + tok_w_in, "row")],
                       epilogue=norm_bwd, tile_sums=("row",), full_rows=True)
    dg_norm1 = jnp.sum(dg_tiles, axis=0, keepdims=True)

    late = pack([dg_norm1[0], dgq.reshape(heads, HEAD_DIM).sum(0) * qk_scale, dgk.reshape(heads, HEAD_DIM).sum(0),
                 jnp.zeros((1,), F32)])
    late_exchange, _ = _chip_exchange_start("small_late_start", "all", [(late, 0)])

    def adam_big(sa, sb_, w, m, v):
        g = sa + sb_
        delta, m, v = _adamw(g, w, m, v)
        return g, delta, m, v

    def reduce_and_update(tag, params, slots):
        mine = [_sum_slots("sum_" + name, s) for s, (name, *_rest) in zip(slots, params)]
        theirs = _swap_with_sibling("swap_" + tag, mine)
        return {name: _rowwise("adamw_" + name, adam_big, [sa, sb_, w, m, v], [], [(w.shape[1], F32)] * 4)
                for (name, w, m, v, _), sa, sb_ in zip(params, mine, theirs)}

    started = late_exchange[3][0]
    slots_mlp_in, slots_mlp_out = _chip_exchange_wait("scatter_mlp_wait", scatter_mlp, started)
    slots_glu, slots_out = _chip_exchange_wait("scatter_mix_wait", scatter_mix, started)
    big_out = reduce_and_update("rest", big[1:], [slots_glu, slots_out, slots_mlp_in, slots_mlp_out])

    (late_slots,) = _chip_exchange_wait("small_late_wait", late_exchange, big_out["w_mlp_out"][3])
    reduced = _sum_slots("sum_small_late", late_slots)
    small_g.update(unpack(reduced, late_names)[0])
    narrow = {name for name, w, _, _ in small if w.ndim == 3 and w.shape[2] < w.shape[1]}

    def flip(a, name):
        return jnp.swapaxes(a, 1, 2) if name in narrow else a

    small_upd = _adamw_small([flip(small_g[name], name) for name, *_ in small], [flip(w, name) for name, w, _, _ in small],
                             [flip(m, name) for name, _, m, _ in small], [flip(v, name) for name, _, _, v in small])
    small_out = [small_g] + [{name: flip(small_upd[kind][i], name) for i, (name, *_) in enumerate(small)}
                             for kind in range(3)]

    (slots_in,) = _chip_exchange_wait("scatter_in_wait", scatter_in, reduced)
    big_out.update(reduce_and_update("w_in", big[:1], [slots_in]))
    names = ["norm1_g", "w_in", "q_norm_g", "k_norm_g", "ssm_lambda_re", "ssm_lambda_im", "ssm_log_dt", "ssm_b_re",
             "ssm_b_im", "ssm_c_re", "ssm_c_im", "ssm_d", "w_glu", "b_glu", "attn_out_g", "ssm_out_g", "w_out",
             "norm2_g", "w_mlp_in", "w_mlp_out"]
    outs = [loss, dx.reshape(batch, seq, d_model)]
    for kind in range(4):
        for name in names:
            outs.append(big_out[name][kind] if name in big_out else small_out[kind][name])
    return tuple(outs)
```

```python
import functools
import math

import jax
import jax.numpy as jnp
from jax import lax
from jax.experimental import pallas as pl
from jax.experimental.pallas import tpu as pltpu

F32 = jnp.float32
BF16 = jnp.bfloat16
F32_DOT = lax.Precision.HIGH
MESH = pl.DeviceIdType.MESH

RMS_EPS = 1e-6
HEAD_DIM = 64
SSM_GROUP = 16
SSM_CHUNK = 16
LANES = 128
N_CHIPS = 4
N_DEV = 8
VMEM_LIMIT = 48 * 1024 * 1024

ADAM_LR = 0.001
ADAM_B1 = 0.9
ADAM_B2 = 0.999
ADAM_EPS = 1e-08
ADAM_WD = 0.01
ADAM_STEP = 10


def _tile(n, pref):
    t = min(n, pref)
    while n % t:
        t //= 2
    return t


def _params(*sem):
    return pltpu.CompilerParams(dimension_semantics=sem, vmem_limit_bytes=VMEM_LIMIT)


_DIMS = {"nn": (((1,), (0,)), ((), ())), "nt": (((1,), (1,)), ((), ())), "tn": (((0,), (0,)), ((), ()))}


MM_VMEM_BUDGET = 40 * 1024 * 1024


def _mm_tiles(m, n, k, a_bytes, b_bytes, tile_bytes, full_rows=False):
    best = None
    for tk in [t for t in (k, k // 2, k // 4, k // 8) if t >= 256 or t == k]:
        for tm in [t for t in (1024, 512, 256, 128) if t <= m and m % t == 0]:
            for tn in [n] if full_rows else [t for t in (1024, 512, 256, 128) if t <= n and n % t == 0]:
                need = 2 * (tm * tk * a_bytes + tk * tn * b_bytes) + 2 * tm * tn * tile_bytes + (tm * tn * 4 if tk < k else 0)
                if need > MM_VMEM_BUDGET:
                    continue
                traffic = m * k * a_bytes * (1 if tk == k else n // tn) + k * n * b_bytes * (1 if n == tn and tk == k else m // tm)
                key = (tk < k, traffic, -tm * tn)
                if best is None or key < best[0]:
                    best = (key, (tm, tn, tk))
    return best[1]


def _mm(name, a, b, mode, *, a_fn=None, extras=(), epilogue=None, out_dtypes=(F32,), tile_sums=(), full_rows=False):
    if mode == "nn":
        (m, k), n = a.shape, b.shape[1]
    elif mode == "nt":
        (m, k), n = a.shape, b.shape[0]
    else:
        (k, m), n = a.shape, b.shape[1]
    tile_bytes = sum(e.dtype.itemsize for e, kind in extras if kind == "tile") + sum(jnp.dtype(d).itemsize for d in out_dtypes)
    tm, tn, tk = _mm_tiles(m, n, k, a.dtype.itemsize, b.dtype.itemsize, tile_bytes, full_rows)
    nk = k // tk
    ne, nout = len(extras), len(out_dtypes)
    dims = _DIMS[mode]

    def body(a_ref, b_ref, *rest):
        ex, outs, sums = rest[:ne], rest[ne:ne + nout], rest[ne + nout:ne + nout + len(tile_sums)]
        at = a_ref[...]
        if a_fn is not None:
            at = a_fn(at)
        p = lax.dot_general(at.astype(BF16), b_ref[...].astype(BF16), dims, preferred_element_type=F32)

        def finish(r):
            if epilogue is not None:
                r = epilogue(r, *[e[...] for e in ex])
            if not isinstance(r, (tuple, list)):
                r = (r,)
            for o, v in zip(outs, r[:nout]):
                o[...] = v.astype(o.dtype)
            for o, v, kind in zip(sums, r[nout:], tile_sums):
                first = lax.broadcasted_iota(jnp.int32, o.shape, 0) == 0
                if kind == "scalar":
                    first &= lax.broadcasted_iota(jnp.int32, o.shape, 1) == 0
                o[...] = jnp.where(first, v, 0.0)

        if nk == 1:
            finish(p)
        else:
            acc = rest[ne + nout + len(tile_sums)]
            kk = pl.program_id(2)

            @pl.when(kk == 0)
            def _():
                acc[...] = p

            @pl.when(kk > 0)
            def _():
                acc[...] += p

            @pl.when(kk == nk - 1)
            def _():
                finish(acc[...])

    if mode == "tn":
        a_spec = pl.BlockSpec((tk, tm), lambda i, j, kk: (kk, i))
    else:
        a_spec = pl.BlockSpec((tm, tk), lambda i, j, kk: (i, kk))
    if mode == "nt":
        b_spec = pl.BlockSpec((tn, tk), lambda i, j, kk: (j, kk))
    else:
        b_spec = pl.BlockSpec((tk, tn), lambda i, j, kk: (kk, j))
    ex_specs = []
    for _, kind in extras:
        if kind == "tile":
            ex_specs.append(pl.BlockSpec((tm, tn), lambda i, j, kk: (i, j)))
        else:
            ex_specs.append(pl.BlockSpec((1, tn), lambda i, j, kk: (0, j)))
    return pl.pallas_call(
        body, name=name, grid=(m // tm, n // tn, nk),
        in_specs=[a_spec, b_spec] + ex_specs,
        out_specs=([pl.BlockSpec((tm, tn), lambda i, j, kk: (i, j)) for _ in out_dtypes]
                   + [pl.BlockSpec((8, LANES if kind == "scalar" else tn), lambda i, j, kk: (i, j)) for kind in tile_sums]),
        out_shape=([jax.ShapeDtypeStruct((m, n), dt) for dt in out_dtypes]
                   + [jax.ShapeDtypeStruct((m // tm * 8, n // tn * LANES if kind == "scalar" else n), F32)
                      for kind in tile_sums]),
        scratch_shapes=[pltpu.VMEM((tm, tn), F32)] if nk > 1 else [],
        compiler_params=_params("parallel", "parallel", "arbitrary"),
    )(a, b, *[e for e, _ in extras])


def _rowwise(name, fn, rows, consts, row_outs, acc_outs=(), tm=256):
    norm = [r if isinstance(r, tuple) else (r, r.shape[1], 0) for r in rows]
    t = norm[0][0].shape[0]
    tm = _tile(t, tm)
    nr, nc, no = len(norm), len(consts), len(row_outs)

    def body(*refs):
        outs = fn(*[r[...] for r in refs[:nr + nc]])
        if not isinstance(outs, (tuple, list)):
            outs = (outs,)
        o_refs, a_refs = refs[nr + nc:nr + nc + no], refs[nr + nc + no:]
        for r, v in zip(o_refs, outs[:no]):
            r[...] = v.astype(r.dtype)
        if a_refs:
            i = pl.program_id(0)

            @pl.when(i == 0)
            def _():
                for r, v in zip(a_refs, outs[no:]):
                    r[...] = v

            @pl.when(i > 0)
            def _():
                for r, v in zip(a_refs, outs[no:]):
                    r[...] += v

    in_specs = [pl.BlockSpec((tm, w), functools.partial(lambda i, cb: (i, cb), cb=cb)) for _, w, cb in norm]
    in_specs += [pl.BlockSpec(c.shape, functools.partial(lambda i, nd: (0,) * nd, nd=c.ndim)) for c in consts]
    out_specs = [pl.BlockSpec((tm, w), lambda i: (i, 0)) for w, _ in row_outs]
    out_specs += [pl.BlockSpec(s, functools.partial(lambda i, nd: (0,) * nd, nd=len(s))) for s in acc_outs]
    out_shape = [jax.ShapeDtypeStruct((t, w), dt) for w, dt in row_outs]
    out_shape += [jax.ShapeDtypeStruct(s, F32) for s in acc_outs]
    return pl.pallas_call(
        body, name=name, grid=(t // tm,), in_specs=in_specs, out_specs=out_specs, out_shape=out_shape,
        compiler_params=_params("arbitrary"),
    )(*[r[0] for r in norm], *consts)


def _rms(x, g):
    return x * lax.rsqrt(jnp.mean(x * x, axis=-1, keepdims=True) + RMS_EPS) * g


@jax.custom_vjp
def _head_sums(x, ones_blocks):
    parts = [jnp.dot(x[:, j:j + LANES], ones_blocks, precision=F32_DOT, preferred_element_type=F32)
             for j in range(0, x.shape[1], LANES)]
    return jnp.concatenate(parts, axis=1)


_head_sums.defvjp(lambda x, ones_blocks: (_head_sums(x, ones_blocks), ones_blocks),
                  lambda ones_blocks, ct: (_head_sums(ct, ones_blocks), None))


def _head_rms(x, g, ones_blocks):
    return x * lax.rsqrt(_head_sums(x * x, ones_blocks) * (1.0 / HEAD_DIM) + RMS_EPS) * g


def _gelu(x):
    return x * (0.5 * (1.0 + jnp.tanh(math.sqrt(2.0 / math.pi) * (x + 0.044715 * (x * x * x)))))


def _mixed(sb, y_ssm, gate_pre, g_attn, g_ssm):
    ssm = _gelu(y_ssm) * jax.nn.sigmoid(gate_pre)
    return jnp.concatenate([_rms(sb, g_attn), _rms(ssm, g_ssm)], axis=-1)


def _softplus(z):
    return jnp.maximum(z, 0.0) + jnp.log(1.0 + jnp.exp(-jnp.abs(z)))


def _running_sums(x, tri):
    return jnp.dot(x.astype(BF16), tri, preferred_element_type=F32)


def _dot_nt(a, b, **kw):
    return lax.dot_general(a, b, _DIMS["nt"], preferred_element_type=F32, **kw)


def _dot_tn(a, b, **kw):
    return lax.dot_general(a, b, _DIMS["tn"], preferred_element_type=F32, **kw)


ATTN_BQ, ATTN_BK = 2048, 256
HEAD_LANES = tuple(slice(h * HEAD_DIM, (h + 1) * HEAD_DIM) for h in range(LANES // HEAD_DIM))


def _attn_fwd(qs, kn, v, *, batch, seq, bq, bk):
    width = qs.shape[1]
    bq = _tile(seq, bq)
    bk = _tile(bq, bk)
    nq, kpq = seq // bq, bq // bk

    def body(q_ref, k_ref, v_ref, o_ref, c_ref):
        row = lax.broadcasted_iota(jnp.int32, (bq, bk), 0)
        col = lax.broadcasted_iota(jnp.int32, (bq, bk), 1)
        tri = (lax.broadcasted_iota(jnp.int32, (bk, bk), 0) >= lax.broadcasted_iota(jnp.int32, (bk, bk), 1)).astype(BF16)

        def q_block(qi, carry):
            r0 = pl.multiple_of(qi * bq, bq)
            qh = [q_ref[pl.ds(r0, bq), ln] for ln in HEAD_LANES]

            def tile(k0, state, top=0):
                diag = top is not None
                top = top or 0
                msk = (col < row)[:bq - top] if diag else None
                new = []
                for h, ln in enumerate(HEAD_LANES):
                    o, c = state[2 * h], state[2 * h + 1]
                    z = _dot_nt(qh[h][top:], k_ref[pl.ds(k0, bk), ln])
                    sp = _softplus(z)
                    if diag:
                        sp = jnp.where(msk, sp, 0.0)
                    r = _running_sums(sp, tri)
                    a = jnp.exp(z - r - c[top:])
                    if diag:
                        a = jnp.where(msk, a, 0.0)
                    o_new = o[top:] + jnp.dot(a.astype(BF16), v_ref[pl.ds(k0, bk), ln], preferred_element_type=F32)
                    c_new = c[top:] + r[:, 0:1]
                    if top:
                        o_new, c_new = jnp.concatenate([o[:top], o_new]), jnp.concatenate([c[:top], c_new])
                    new += [o_new, c_new]
                return tuple(new)

            state = (jnp.zeros((bq, HEAD_DIM), F32), jnp.zeros((bq, 1), F32)) * len(HEAD_LANES)
            for d in reversed(range(kpq)):
                state = tile(pl.multiple_of(r0 + d * bk, bk), state, top=d * bk)
            state = lax.fori_loop(0, qi * kpq, lambda it, st: tile(pl.multiple_of(r0 - (it + 1) * bk, bk), st, None),
                                  state)
            for h, ln in enumerate(HEAD_LANES):
                o_ref[pl.ds(r0, bq), ln] = state[2 * h]
                c_ref[pl.ds(r0, bq), ln] = jnp.broadcast_to(state[2 * h + 1], (bq, HEAD_DIM))
            return carry

        lax.fori_loop(0, nq, q_block, 0)

    spec = pl.BlockSpec((seq, LANES), lambda b, h: (b, h))
    shape = jax.ShapeDtypeStruct((batch * seq, width), F32)
    return pl.pallas_call(
        body, name="attn_fwd", grid=(batch, width // LANES), in_specs=[spec, spec, spec], out_specs=[spec, spec],
        out_shape=[shape, shape], compiler_params=_params("parallel", "parallel"),
    )(qs, kn, v)


def _attn_bwd(qs, kn, v, c_tot, do, *, batch, seq, bq, bk, after):
    width = qs.shape[1]
    bq = _tile(seq, bq)
    bk = _tile(bq, bk)
    nq, kpq = seq // bq, bq // bk

    def body(q_ref, k_ref, v_ref, c_ref, do_ref, after_ref, dq_ref, dk_ref, dv_ref):
        row = lax.broadcasted_iota(jnp.int32, (bq, bk), 0)
        col = lax.broadcasted_iota(jnp.int32, (bq, bk), 1)
        sq_row = lax.broadcasted_iota(jnp.int32, (bk, bk), 0)
        sq_col = lax.broadcasted_iota(jnp.int32, (bk, bk), 1)
        tri = (sq_row >= sq_col).astype(BF16)
        tri_t = (sq_row <= sq_col).astype(BF16)
        dk_ref[...] = jnp.zeros_like(dk_ref)
        dv_ref[...] = jnp.zeros_like(dv_ref)

        def q_block(qi, carry):
            r0 = pl.multiple_of(qi * bq, bq)
            qh = [q_ref[pl.ds(r0, bq), ln] for ln in HEAD_LANES]
            d_out = [do_ref[pl.ds(r0, bq), ln].astype(BF16) for ln in HEAD_LANES]
            c_all = [c_ref[pl.ds(r0, bq), ln][:, 0:1] for ln in HEAD_LANES]

            def tile(k0, state, top=0):
                diag = top is not None
                top = top or 0
                last = diag and top == bq - bk
                msk = (col < row)[:bq - top] if diag else None
                new = []
                for h, ln in enumerate(HEAD_LANES):
                    c_left, g_left, dq = state[3 * h:3 * h + 3]
                    q, d_o = qh[h][top:], d_out[h][top:]
                    k = k_ref[pl.ds(k0, bk), ln]
                    z = _dot_nt(q, k)
                    e = jnp.exp(-jnp.abs(z))
                    sp = jnp.maximum(z, 0.0) + jnp.log(1.0 + e)
                    sig = jnp.exp(z - sp)
                    if diag:
                        sp = jnp.where(msk, sp, 0.0)
                    r = _running_sums(sp, tri)
                    c_new = c_left[top:] + r[:, 0:1]
                    a = jnp.exp(z - r - (0.0 if last else c_all[h][top:] - c_new))
                    if diag:
                        a = jnp.where(msk, a, 0.0)
                    g = a * _dot_nt(d_o, v_ref[pl.ds(k0, bk), ln])
                    pg = _running_sums(g, tri_t)
                    dz = g - sig * (g_left[top:] + pg)
                    if diag:
                        dz = jnp.where(msk, dz, 0.0)
                    dz = dz.astype(BF16)
                    dk_ref[pl.ds(k0, bk), ln] += _dot_tn(dz, q)
                    dv_ref[pl.ds(k0, bk), ln] += _dot_tn(a.astype(BF16), d_o)
                    g_new = g_left[top:] + pg[:, bk - 1:bk]
                    dq_new = dq[top:] + jnp.dot(dz, k, preferred_element_type=F32)
                    if top:
                        c_new = jnp.concatenate([c_left[:top], c_new])
                        g_new = jnp.concatenate([g_left[:top], g_new])
                        dq_new = jnp.concatenate([dq[:top], dq_new])
                    new += [c_new, g_new, dq_new]
                return tuple(new)

            zero = jnp.zeros((bq, 1), F32)
            init = (zero, zero, jnp.zeros((bq, HEAD_DIM), F32)) * len(HEAD_LANES)
            state = lax.fori_loop(0, qi * kpq, lambda it, st: tile(pl.multiple_of(it * bk, bk), st, None), init)
            for d in range(kpq):
                state = tile(pl.multiple_of(r0 + d * bk, bk), state, top=d * bk)
            for h, ln in enumerate(HEAD_LANES):
                dq_ref[pl.ds(r0, bq), ln] = state[3 * h + 2]
            return carry

        lax.fori_loop(0, nq, q_block, 0)

    spec = pl.BlockSpec((seq, LANES), lambda b, h: (b, h))
    shape = jax.ShapeDtypeStruct((batch * seq, width), F32)
    return pl.pallas_call(
        body, name="attn_bwd", grid=(batch, width // LANES),
        in_specs=[spec] * 5 + [pl.BlockSpec(memory_space=pl.ANY)], out_specs=[spec] * 3,
        out_shape=[shape] * 3, compiler_params=_params("parallel", "parallel"),
    )(qs, kn, v, c_tot, do, after)


def _s5_group_operators(lr_r, li_r, lr_c, li_c, log_dt, bt_re, bt_im, ct_re, ct_im, d_row):
    cs = SSM_CHUNK
    n_ch, n_state = bt_re.shape
    width = cs * n_ch
    dt = jnp.exp(log_dt)

    def spread(x, pattern):
        return jnp.dot(x, pattern, precision=F32_DOT, preferred_element_type=F32)

    twice = (lax.broadcasted_iota(jnp.int32, (n_state, 2 * n_state), 0)
             == lax.broadcasted_iota(jnp.int32, (n_state, 2 * n_state), 1) % n_state).astype(F32)
    steps = lax.broadcasted_iota(jnp.int32, (cs + 1, 1), 0).astype(F32)
    mag = jnp.exp(steps * (lr_r * dt))
    ang = steps * (li_r * dt)
    pw_re, pw_im = mag * jnp.cos(ang), mag * jnp.sin(ang)
    num_re, num_im = pw_re[1:2] - 1.0, pw_im[1:2]
    den = lr_r * lr_r + li_r * li_r
    cf_re = (num_re * lr_r + num_im * li_r) / den
    cf_im = (num_im * lr_r - num_re * li_r) / den
    bb_re = spread(cf_re * bt_re - cf_im * bt_im, twice)
    bb_im = spread(cf_re * bt_im + cf_im * bt_re, twice)
    pw2_re, pw2_im = spread(pw_re, twice), spread(pw_im, twice)
    real_half = lax.broadcasted_iota(jnp.int32, (1, 2 * n_state), 1) < n_state
    blocks = []
    for s in range(cs):
        pr, pi = pw2_re[cs - 1 - s:cs - s], pw2_im[cs - 1 - s:cs - s]
        blocks.append(jnp.where(real_half, bb_re * pr - bb_im * pi, bb_re * pi + bb_im * pr))
    b_mat = jnp.concatenate(blocks, axis=0)
    la = pw2_re[cs:cs + 1]
    lb = jnp.where(real_half, -pw2_im[cs:cs + 1], pw2_im[cs:cs + 1])

    lane = lax.broadcasted_iota(jnp.int32, (1, width), 1)
    tile_out = (lax.broadcasted_iota(jnp.int32, (n_ch, width), 0)
                == lax.broadcasted_iota(jnp.int32, (n_ch, width), 1) % n_ch).astype(F32)
    c_re, c_im = spread(ct_re, tile_out), spread(ct_im, tile_out)

    def c_times_powers(first):
        k = (lane // n_ch + first).astype(F32)
        m = jnp.exp(k * (lr_c * dt))
        a = k * (li_c * dt)
        p_re, p_im = m * jnp.cos(a), m * jnp.sin(a)
        return p_re * c_re - p_im * c_im, p_re * c_im + p_im * c_re

    w_re, w_im = c_times_powers(0)
    skip = jnp.where((lane < n_ch) & (lane == lax.broadcasted_iota(jnp.int32, (n_ch, width), 0)),
                     spread(d_row, tile_out), 0.0)
    kt_row = (jnp.dot(bb_re[:, :n_state], w_re, precision=F32_DOT, preferred_element_type=F32)
              - jnp.dot(bb_im[:, :n_state], w_im, precision=F32_DOT, preferred_element_type=F32) + skip)
    w1_re, w1_im = c_times_powers(1)
    c_mat = jnp.concatenate([w1_re, -w1_im], axis=0)
    return kt_row, b_mat, c_mat, la, lb


def _s5_operator_inputs(lam_re, lam_im, log_dt, b_re, b_im, c_re, c_im, d_skip):
    return (lam_re[:, None, :], lam_im[:, None, :], lam_re[:, :, None], lam_im[:, :, None], log_dt[:, None, None],
            b_re.transpose(0, 2, 1), b_im.transpose(0, 2, 1), c_re.transpose(0, 2, 1), c_im.transpose(0, 2, 1),
            d_skip[:, None, :])


def _s5_operators_call(name, args, cotangents=None, gb=8):
    groups = args[0].shape[0]
    gb = _tile(groups, gb)
    n_in = len(args)

    def body(*refs):
        n_ct = 0 if cotangents is None else len(cotangents)
        ins, cts, outs = refs[:n_in], refs[n_in:n_in + n_ct], refs[n_in + n_ct:]
        for g in range(gb):
            vals = [r[g] for r in ins]
            if cotangents is None:
                res = _s5_group_operators(*vals)
            else:
                res = jax.vjp(_s5_group_operators, *vals)[1](tuple(c[g] for c in cts))
            for o, v in zip(outs, res):
                o[g] = v

    def spec(a):
        return pl.BlockSpec((gb, *a.shape[1:]), lambda i: (i, 0, 0))

    if cotangents is None:
        n_ch, n_state = args[5].shape[1:]
        width = SSM_CHUNK * n_ch
        out_shape = [jax.ShapeDtypeStruct((groups, *s), F32) for s in
                     ((n_ch, width), (width, 2 * n_state), (2 * n_state, width), (1, 2 * n_state), (1, 2 * n_state))]
    else:
        out_shape = [jax.ShapeDtypeStruct(a.shape, F32) for a in args]
    operands = [*args, *(cotangents or ())]
    return pl.pallas_call(
        body, name=name, grid=(groups // gb,), in_specs=[spec(a) for a in operands], out_specs=[spec(s) for s in out_shape],
        out_shape=out_shape, compiler_params=_params("parallel"),
    )(*operands)


GROUPS_PER_BLOCK = LANES // SSM_GROUP


def _tokens_to_groups(name, u, col_block, width):
    t = u.shape[0]
    n = t // SSM_CHUNK
    ch = SSM_CHUNK * SSM_GROUP
    blocks = width // LANES

    def body(u_ref, o_ref):
        for s in range(SSM_CHUNK):
            rows = u_ref[pl.ds(s, n, stride=SSM_CHUNK), :]
            for g in range(GROUPS_PER_BLOCK):
                o_ref[g, :, s * SSM_GROUP:(s + 1) * SSM_GROUP] = rows[:, g * SSM_GROUP:(g + 1) * SSM_GROUP]

    return pl.pallas_call(
        body, name=name, grid=(blocks,),
        in_specs=[pl.BlockSpec((t, LANES), lambda j: (0, col_block * blocks + j))],
        out_specs=pl.BlockSpec((GROUPS_PER_BLOCK, n, ch), lambda j: (j, 0, 0)),
        out_shape=jax.ShapeDtypeStruct((width // SSM_GROUP, n, ch), F32), compiler_params=_params("parallel"),
    )(u)


def _groups_to_tokens(name, ug):
    groups, n, ch = ug.shape

    def body(g_ref, o_ref, rows_ref):
        for s in range(SSM_CHUNK):
            for g in range(GROUPS_PER_BLOCK):
                rows_ref[s % 2, :, g * SSM_GROUP:(g + 1) * SSM_GROUP] = g_ref[g, :, s * SSM_GROUP:(s + 1) * SSM_GROUP]
            o_ref[pl.ds(s, n, stride=SSM_CHUNK), :] = rows_ref[s % 2]

    return pl.pallas_call(
        body, name=name, grid=(groups // GROUPS_PER_BLOCK,),
        in_specs=[pl.BlockSpec((GROUPS_PER_BLOCK, n, ch), lambda j: (j, 0, 0))],
        out_specs=pl.BlockSpec((n * SSM_CHUNK, LANES), lambda j: (0, j)),
        out_shape=jax.ShapeDtypeStruct((n * SSM_CHUNK, groups * SSM_GROUP), F32),
        scratch_shapes=[pltpu.VMEM((2, n, LANES), F32)], compiler_params=_params("parallel"),
    )(ug)


SCAN_ROWS = 8


def _toeplitz_to(tm_ref, g, kt_row):
    width = kt_row.shape[1]
    tm_ref[g] = jnp.zeros((width, width), F32)
    for s in range(SSM_CHUNK):
        tm_ref[g, s * SSM_GROUP:(s + 1) * SSM_GROUP, s * SSM_GROUP:] = kt_row[:, :width - s * SSM_GROUP]


def _lam_powers(la, lb, reverse):
    if reverse:
        lb = -lb

    def mul(p, q):
        return p[0] * q[0] - p[1] * q[1], p[0] * q[1] + p[1] * q[0]

    p1 = (la, lb)
    p2 = mul(p1, p1)
    p3 = mul(p2, p1)
    p4 = mul(p2, p2)
    rows = [p1, p2, p3, p4, mul(p4, p1), mul(p4, p2), mul(p4, p3), mul(p4, p4)]
    if reverse:
        rows = rows[::-1]
    idx = lax.broadcasted_iota(jnp.int32, (SCAN_ROWS, la.shape[1]), 0)
    tab_a = sum(jnp.where(idx == j, r[0], 0.0) for j, r in enumerate(rows))
    tab_b = sum(jnp.where(idx == j, r[1], 0.0) for j, r in enumerate(rows))
    return (p1, p2, p4), (tab_a, tab_b), idx


def _scan_block(e, carry, steps, table, idx, half, reverse):
    n = SCAN_ROWS
    for d, (pa, pb) in zip((1, 2, 4), steps):
        sh = pltpu.roll(e, n - d if reverse else d, 0)
        sh = jnp.where(idx < n - d if reverse else idx >= d, sh, 0.0)
        e = e + pa * sh + pb * pltpu.roll(sh, half, 1)
    tab_a, tab_b = table
    e = e + tab_a * carry + tab_b * pltpu.roll(carry, half, 1)
    shifted = jnp.where(idx == (n - 1 if reverse else 0), carry, pltpu.roll(e, n - 1 if reverse else 1, 0))
    edge = e[0:1] if reverse else e[n - 1:n]
    return shifted, jnp.broadcast_to(edge, e.shape)


def _s5_fwd(ug, kt_row, b_mat, c_mat, la, lb, *, batch, gb=8):
    groups, n, ch = ug.shape
    p2 = b_mat.shape[2]
    gb = _tile(groups, gb)
    nch = n // batch
    nblk = nch // SCAN_ROWS

    def body(u_ref, k_ref, b_ref, c_ref, la_ref, lb_ref, y_ref, x_ref, s_ref, tm_ref):
        for g in range(gb):
            _toeplitz_to(tm_ref, g, k_ref[g])
            s_ref[g] = jnp.dot(u_ref[g], b_ref[g], precision=F32_DOT, preferred_element_type=F32)
        powers = [_lam_powers(la_ref[g], lb_ref[g], False) for g in range(gb)]

        def step(blk, carries):
            new = []
            for g in range(gb):
                steps, table, idx = powers[g]
                for b in range(batch):
                    rows = pl.ds(pl.multiple_of(b * nch + blk * SCAN_ROWS, SCAN_ROWS), SCAN_ROWS)
                    x_in, carry = _scan_block(s_ref[g, rows, :], carries[g * batch + b], steps, table, idx, p2 // 2, False)
                    x_ref[g, rows, :] = x_in
                    new.append(carry)
            return tuple(new)

        lax.fori_loop(0, nblk, step, tuple(jnp.zeros((SCAN_ROWS, p2), F32) for _ in range(gb * batch)))
        for g in range(gb):
            y_ref[g] = (jnp.dot(u_ref[g], tm_ref[g], precision=F32_DOT, preferred_element_type=F32)
                        + jnp.dot(x_ref[g], c_ref[g], precision=F32_DOT, preferred_element_type=F32))

    def spec(a, b):
        return pl.BlockSpec((gb, a, b), lambda i: (i, 0, 0))

    return pl.pallas_call(
        body, name="s5_fwd", grid=(groups // gb,),
        in_specs=[spec(n, ch), spec(SSM_GROUP, ch), spec(ch, p2), spec(p2, ch), spec(1, p2), spec(1, p2)],
        out_specs=[spec(n, ch), spec(n, p2)],
        out_shape=[jax.ShapeDtypeStruct((groups, n, ch), F32), jax.ShapeDtypeStruct((groups, n, p2), F32)],
        scratch_shapes=[pltpu.VMEM((gb, n, p2), F32), pltpu.VMEM((gb, ch, ch), F32)],
        compiler_params=_params("parallel"),
    )(ug, kt_row, b_mat, c_mat, la, lb)


def _s5_bwd(ug, dyg, xin, kt_row, b_mat, c_mat, la, lb, *, batch, gb=8):
    groups, n, ch = ug.shape
    p2 = b_mat.shape[2]
    gb = _tile(groups, gb)
    nch = n // batch
    nblk = nch // SCAN_ROWS

    def body(u_ref, dy_ref, x_ref, k_ref, b_ref, c_ref, la_ref, lb_ref,
             du_ref, dk_ref, db_ref, dc_ref, dla_ref, dlb_ref, dx_ref, ds_ref, tm_ref):
        for g in range(gb):
            _toeplitz_to(tm_ref, g, k_ref[g])
            dx_ref[g] = _dot_nt(dy_ref[g], c_ref[g], precision=F32_DOT)
        powers = [_lam_powers(la_ref[g], lb_ref[g], True) for g in range(gb)]

        def step(it, carries):
            new = []
            for g in range(gb):
                steps, table, idx = powers[g]
                for b in range(batch):
                    rows = pl.ds(pl.multiple_of(b * nch + (nblk - 1 - it) * SCAN_ROWS, SCAN_ROWS), SCAN_ROWS)
                    d_s, carry = _scan_block(dx_ref[g, rows, :], carries[g * batch + b], steps, table, idx, p2 // 2, True)
                    ds_ref[g, rows, :] = d_s
                    new.append(carry)
            return tuple(new)

        lax.fori_loop(0, nblk, step, tuple(jnp.zeros((SCAN_ROWS, p2), F32) for _ in range(gb * batch)))
        for g in range(gb):
            u, dy, ds, x = u_ref[g], dy_ref[g], ds_ref[g], x_ref[g]
            du_ref[g] = _dot_nt(dy, tm_ref[g], precision=F32_DOT) + _dot_nt(ds, b_ref[g], precision=F32_DOT)
            tm_ref[g] = _dot_tn(u, dy, precision=F32_DOT)
            dk_ref[g] = tm_ref[g, 0:SSM_GROUP, :]
            for s in range(1, SSM_CHUNK):
                dk_ref[g, :, :ch - s * SSM_GROUP] += tm_ref[g, s * SSM_GROUP:(s + 1) * SSM_GROUP, s * SSM_GROUP:]
            db_ref[g] = _dot_tn(u, ds, precision=F32_DOT)
            dc_ref[g] = _dot_tn(x, dy, precision=F32_DOT)
            dla_ref[g] = jnp.sum(ds * x, axis=0, keepdims=True)
            dlb_ref[g] = jnp.sum(ds * pltpu.roll(x, p2 // 2, 1), axis=0, keepdims=True)

    def spec(a, b):
        return pl.BlockSpec((gb, a, b), lambda i: (i, 0, 0))

    def shape(a, b):
        return jax.ShapeDtypeStruct((groups, a, b), F32)

    return pl.pallas_call(
        body, name="s5_bwd", grid=(groups // gb,),
        in_specs=[spec(n, ch), spec(n, ch), spec(n, p2), spec(SSM_GROUP, ch), spec(ch, p2), spec(p2, ch), spec(1, p2),
                  spec(1, p2)],
        out_specs=[spec(n, ch), spec(SSM_GROUP, ch), spec(ch, p2), spec(p2, ch), spec(1, p2), spec(1, p2)],
        out_shape=[shape(n, ch), shape(SSM_GROUP, ch), shape(ch, p2), shape(p2, ch), shape(1, p2), shape(1, p2)],
        scratch_shapes=[pltpu.VMEM((gb, n, p2), F32), pltpu.VMEM((gb, n, p2), F32), pltpu.VMEM((gb, ch, ch), F32)],
        compiler_params=_params("parallel"),
    )(ug, dyg, xin, kt_row, b_mat, c_mat, la, lb)


def _block(ref, axis, j, size):
    start = j * size if isinstance(j, int) else pl.multiple_of(j * size, size)
    return ref.at[pl.ds(start, size), :] if axis == 0 else ref.at[:, pl.ds(start, size)]


def _chip_exchange_copies(mode, axes, srcs, lands, send_sems, recv_sems, local_sems):
    x, y, c = lax.axis_index("x"), lax.axis_index("y"), lax.axis_index("c")
    everyone = mode == "all"
    me = 4 * x + 2 * y + c if everyone else 2 * x + y
    n_peers = _exchange_peers(mode)
    local, sends, arrivals = [], [], []
    for w, axis in enumerate(axes):
        if mode == "gather":
            size = srcs[w].shape[axis]
            local.append(pltpu.make_async_copy(srcs[w], _block(lands[w], axis, me, size), local_sems.at[w]))
        elif mode == "scatter":
            size = srcs[w].shape[axis] // N_CHIPS
            local.append(pltpu.make_async_copy(_block(srcs[w], axis, me, size), lands[w].at[me], local_sems.at[w]))
        else:
            local.append(pltpu.make_async_copy(srcs[w], lands[w].at[me], local_sems.at[w]))
        for k in range(1, n_peers + 1):
            bits = k if everyone else 2 * k
            px = 1 - x if bits & 4 else x
            py = 1 - y if bits & 2 else y
            pc = 1 - c if bits & 1 else c
            peer = 4 * px + 2 * py + pc if everyone else 2 * px + py
            if mode == "gather":
                src, dst, arrive = srcs[w], _block(lands[w], axis, me, size), _block(lands[w], axis, peer, size)
            elif mode == "scatter":
                src, dst, arrive = _block(srcs[w], axis, peer, size), lands[w].at[me], lands[w].at[peer]
            else:
                src, dst, arrive = srcs[w], lands[w].at[me], lands[w].at[peer]
            sem = w * n_peers + k - 1
            for target, out in ((dst, sends), (arrive, arrivals)):
                out.append(pltpu.make_async_remote_copy(
                    src_ref=src, dst_ref=target, send_sem=send_sems.at[sem], recv_sem=recv_sems.at[sem],
                    device_id=(px, py, pc), device_id_type=MESH))
    return local, sends, arrivals


def _exchange_peers(mode):
    return N_DEV - 1 if mode == "all" else N_CHIPS - 1


def _chip_exchange_start(name, mode, items, after=None):
    n = len(items)
    n_after = 0 if after is None else 1
    axes = [axis for _, axis in items]
    hbm = pl.BlockSpec(memory_space=pltpu.HBM)
    sem = pl.BlockSpec(memory_space=pltpu.SEMAPHORE)
    lands = []
    for a, axis in items:
        shape = list(a.shape)
        if mode == "gather":
            shape[axis] *= N_CHIPS
        elif mode == "scatter":
            shape[axis] //= N_CHIPS
            shape = [N_CHIPS] + shape
        else:
            shape = [N_DEV] + shape
        lands.append(pltpu.with_memory_space_constraint(lax.empty(tuple(shape), a.dtype), pltpu.HBM))

    def body(*refs):
        srcs, land_refs = refs[:n], refs[n:2 * n]
        send_sems, recv_sems, local_sems = refs[2 * n + n_after:2 * n + n_after + 3]
        token = refs[-1]
        local, sends, _ = _chip_exchange_copies(mode, axes, srcs, land_refs, send_sems, recv_sems, local_sems)
        for cp in local + sends:
            cp.start()
        token[...] = jnp.zeros_like(token)

    n_sem = n * _exchange_peers(mode)
    outs = pl.pallas_call(
        body, name=name,
        out_shape=(pltpu.SemaphoreType.DMA((n_sem,)), pltpu.SemaphoreType.DMA((n_sem,)), pltpu.SemaphoreType.DMA((n,)),
                   *[pltpu.HBM(a.shape, a.dtype) for a, _ in items], *[pltpu.HBM(l.shape, l.dtype) for l in lands],
                   jax.ShapeDtypeStruct((8, LANES), F32)),
        in_specs=[hbm] * (2 * n) + [pl.BlockSpec(memory_space=pl.ANY)] * n_after,
        out_specs=(sem, sem, sem, *[hbm] * (2 * n), pl.BlockSpec(memory_space=pltpu.VMEM)),
        input_output_aliases={i: 3 + i for i in range(2 * n)},
        compiler_params=pltpu.CompilerParams(has_side_effects=pltpu.SideEffectType.DATAFLOW_SIDE_EFFECTING),
    )(*[pltpu.with_memory_space_constraint(a, pltpu.HBM) for a, _ in items], *lands, *([after] if n_after else []))
    return (mode, axes, outs[:3], outs[3:3 + n], outs[3 + n:3 + 2 * n]), outs[-1][0:1, 0:1]


def _chip_exchange_wait(name, handle, after):
    mode, axes, sems, srcs, lands = handle
    n = len(axes)
    after = list(after) if isinstance(after, (tuple, list)) else [after]
    hbm = pl.BlockSpec(memory_space=pltpu.HBM)
    sem = pl.BlockSpec(memory_space=pltpu.SEMAPHORE)

    def body(*refs):
        src_refs, land_refs = refs[:n], refs[n:2 * n]
        send_sems, recv_sems, local_sems = refs[2 * n:2 * n + 3]
        local, sends, arrivals = _chip_exchange_copies(mode, axes, src_refs, land_refs, send_sems, recv_sems, local_sems)
        for cp in sends:
            cp.wait_send()
        for cp in arrivals:
            cp.wait_recv()
        for cp in local:
            cp.wait()

    outs = pl.pallas_call(
        body, name=name,
        out_shape=(*[pltpu.HBM(a.shape, a.dtype) for a in srcs], *[pltpu.HBM(l.shape, l.dtype) for l in lands]),
        in_specs=[hbm] * (2 * n) + [sem] * 3 + [pl.BlockSpec(memory_space=pl.ANY)] * len(after), out_specs=[hbm] * (2 * n),
        input_output_aliases={i: i for i in range(2 * n)},
        compiler_params=pltpu.CompilerParams(has_side_effects=pltpu.SideEffectType.DATAFLOW_SIDE_EFFECTING),
    )(*srcs, *lands, *sems, *after)
    return outs[n:]


def _sum_slots(name, slots, tm=256):
    n_slots, r, c = slots.shape
    tm = _tile(r, tm)

    def body(*refs):
        acc = refs[0][...]
        for s_ref in refs[1:n_slots]:
            acc = acc + s_ref[...]
        refs[n_slots][...] = acc

    specs = [pl.BlockSpec((None, tm, c), functools.partial(lambda i, s: (s, i, 0), s=s)) for s in range(n_slots)]
    return pl.pallas_call(
        body, name=name, grid=(r // tm,), in_specs=specs, out_specs=pl.BlockSpec((tm, c), lambda i: (i, 0)),
        out_shape=jax.ShapeDtypeStruct((r, c), F32), compiler_params=_params("parallel"),
    )(*[slots] * n_slots)


def _swap_with_sibling(name, arrays):
    n = len(arrays)
    hbm = pl.BlockSpec(memory_space=pl.ANY)

    def body(*refs):
        ins, outs = refs[:n], refs[n:2 * n]
        send_sems, recv_sems = refs[2 * n:]
        sibling = (lax.axis_index("x"), lax.axis_index("y"), 1 - lax.axis_index("c"))
        copies = [pltpu.make_async_remote_copy(src_ref=ins[w], dst_ref=outs[w], send_sem=send_sems.at[w],
                                               recv_sem=recv_sems.at[w], device_id=sibling, device_id_type=MESH)
                  for w in range(n)]
        for cp in copies:
            cp.start()
        for cp in copies:
            cp.wait()

    return pl.pallas_call(
        body, name=name, in_specs=[hbm] * n, out_specs=[hbm] * n,
        out_shape=[jax.ShapeDtypeStruct(a.shape, a.dtype) for a in arrays],
        scratch_shapes=[pltpu.SemaphoreType.DMA((n,)), pltpu.SemaphoreType.DMA((n,))],
    )(*arrays)


def _adamw(g, w, m, v):
    m = ADAM_B1 * m + (1.0 - ADAM_B1) * g
    v = ADAM_B2 * v + (1.0 - ADAM_B2) * jnp.square(g)
    m_hat = m / (1.0 - ADAM_B1 ** ADAM_STEP)
    v_hat = v / (1.0 - ADAM_B2 ** ADAM_STEP)
    delta = -ADAM_LR * (m_hat / (jnp.sqrt(v_hat) + ADAM_EPS) + ADAM_WD * w)
    return delta, m, v


def _adamw_small(grads, ws, ms, vs):
    n = len(ws)

    def whole(a):
        return pl.BlockSpec(a.shape, functools.partial(lambda i, nd: (0,) * nd, nd=a.ndim))

    def body(*refs):
        for i in range(n):
            g, w, m, v = (refs[k * n + i][...] for k in range(4))
            for k, val in enumerate(_adamw(g, w, m, v)):
                refs[(4 + k) * n + i][...] = val

    outs = pl.pallas_call(
        body, name="adamw_small", grid=(1,), in_specs=[whole(a) for a in (*grads, *ws, *ms, *vs)],
        out_specs=[whole(w) for _ in range(3) for w in ws],
        out_shape=[jax.ShapeDtypeStruct(w.shape, F32) for _ in range(3) for w in ws],
        compiler_params=pltpu.CompilerParams(vmem_limit_bytes=VMEM_LIMIT),
    )(*grads, *ws, *ms, *vs)
    return outs[:n], outs[n:2 * n], outs[2 * n:]


def kernel(x, norm1_g, w_in, q_norm_g, k_norm_g, ssm_lambda_re, ssm_lambda_im, ssm_log_dt, ssm_b_re, ssm_b_im, ssm_c_re, ssm_c_im, ssm_d, w_glu, b_glu, attn_out_g, ssm_out_g, w_out, norm2_g, w_mlp_in, w_mlp_out, loss_target, m_norm1_g, m_w_in, m_q_norm_g, m_k_norm_g, m_ssm_lambda_re, m_ssm_lambda_im, m_ssm_log_dt, m_ssm_b_re, m_ssm_b_im, m_ssm_c_re, m_ssm_c_im, m_ssm_d, m_w_glu, m_b_glu, m_attn_out_g, m_ssm_out_g, m_w_out, m_norm2_g, m_w_mlp_in, m_w_mlp_out, v_norm1_g, v_w_in, v_q_norm_g, v_k_norm_g, v_ssm_lambda_re, v_ssm_lambda_im, v_ssm_log_dt, v_ssm_b_re, v_ssm_b_im, v_ssm_c_re, v_ssm_c_im, v_ssm_d, v_w_glu, v_b_glu, v_attn_out_g, v_ssm_out_g, v_w_out, v_norm2_g, v_w_mlp_in, v_w_mlp_out):
    batch, seq, d_model = x.shape
    tokens = batch * seq
    sb_width = w_in.shape[1]
    n_features = d_model

    big = [("w_in", w_in, m_w_in, v_w_in, 1), ("w_glu", w_glu, m_w_glu, v_w_glu, 0),
           ("w_out", w_out, m_w_out, v_w_out, 0), ("w_mlp_in", w_mlp_in, m_w_mlp_in, v_w_mlp_in, 1),
           ("w_mlp_out", w_mlp_out, m_w_mlp_out, v_w_mlp_out, 0)]
    small = [("norm1_g", norm1_g, m_norm1_g, v_norm1_g), ("q_norm_g", q_norm_g, m_q_norm_g, v_q_norm_g),
             ("k_norm_g", k_norm_g, m_k_norm_g, v_k_norm_g),
             ("ssm_lambda_re", ssm_lambda_re, m_ssm_lambda_re, v_ssm_lambda_re),
             ("ssm_lambda_im", ssm_lambda_im, m_ssm_lambda_im, v_ssm_lambda_im),
             ("ssm_log_dt", ssm_log_dt, m_ssm_log_dt, v_ssm_log_dt),
             ("ssm_b_re", ssm_b_re, m_ssm_b_re, v_ssm_b_re), ("ssm_b_im", ssm_b_im, m_ssm_b_im, v_ssm_b_im),
             ("ssm_c_re", ssm_c_re, m_ssm_c_re, v_ssm_c_re), ("ssm_c_im", ssm_c_im, m_ssm_c_im, v_ssm_c_im),
             ("ssm_d", ssm_d, m_ssm_d, v_ssm_d), ("b_glu", b_glu, m_b_glu, v_b_glu),
             ("attn_out_g", attn_out_g, m_attn_out_g, v_attn_out_g), ("ssm_out_g", ssm_out_g, m_ssm_out_g, v_ssm_out_g),
             ("norm2_g", norm2_g, m_norm2_g, v_norm2_g)]

    gather_in, tok_in = _chip_exchange_start("gather_w_in_start", "gather", [(w_in.astype(BF16), 1)])
    gather_rest, tok_rest = _chip_exchange_start(
        "gather_rest_start", "gather", [(w.astype(BF16), axis) for _, w, _, _, axis in big[1:]], after=tok_in)

    x2 = x.reshape(tokens, d_model)
    tgt2 = loss_target.reshape(tokens, d_model)
    g1, g2 = norm1_g[None, :], norm2_g[None, :]
    g_attn, g_ssm, bias_glu = attn_out_g[None, :], ssm_out_g[None, :], b_glu[None, :]
    heads = sb_width // HEAD_DIM
    qk_scale = 1.0 / math.sqrt(HEAD_DIM)
    gq, gk = (jnp.tile(q_norm_g, heads) * qk_scale)[None, :], jnp.tile(k_norm_g, heads)[None, :]
    lane_head = jnp.arange(LANES) // HEAD_DIM
    ones_blocks = (lane_head[:, None] == lane_head[None, :]).astype(F32)

    (xn,) = _rowwise("norm1", _rms, [x2], [g1 + tok_rest], [(d_model, BF16)])
    s5_in = _s5_operator_inputs(ssm_lambda_re, ssm_lambda_im, ssm_log_dt, ssm_b_re, ssm_b_im, ssm_c_re, ssm_c_im, ssm_d)
    kt_row, b_mat, c_mat, la, lb = _s5_operators_call("s5_operators", s5_in)
    (wf_in,) = _chip_exchange_wait("gather_w_in_wait", gather_in, [xn, b_mat, c_mat])
    (proj,) = _mm("proj_in", xn, wf_in, "nn")

    def qkv_fn(q, k, v, gq_, gk_, ones):
        return _head_rms(q, gq_, ones), _head_rms(k, gk_, ones), v

    qn, kn, vb = _rowwise("qk_norm", qkv_fn, [(proj, sb_width, 0), (proj, sb_width, 1), (proj, sb_width, 2)],
                          [gq, gk, ones_blocks], [(sb_width, BF16)] * 3, tm=512)
    sb, c_tot = _attn_fwd(qn, kn, vb, batch=batch, seq=seq, bq=ATTN_BQ, bk=ATTN_BK)
    ug = _tokens_to_groups("u_to_groups", proj, 3, sb_width)
    yg, xin = _s5_fwd(ug, kt_row, b_mat, c_mat, la, lb, batch=batch)
    y_ssm = _groups_to_tokens("y_to_tokens", yg)

    wf_glu, wf_out, wf_mlp_in, wf_mlp_out = _chip_exchange_wait("gather_rest_wait", gather_rest, [y_ssm, sb])
    (gate_pre,) = _mm("glu_gate", y_ssm, wf_glu, "nn", a_fn=_gelu, extras=[(bias_glu, "row")],
                      epilogue=lambda acc, b: acc + b)
    (mixed,) = _rowwise("mix_norm", _mixed, [sb, y_ssm, gate_pre], [g_attn, g_ssm], [(2 * sb_width, BF16)])
    def out_head(acc, r, g):
        h = acc + r
        return h, _rms(h, g)

    h1, hn = _mm("proj_out", mixed, wf_out, "nn", extras=[(x2, "tile"), (g2, "row")], epilogue=out_head,
                 out_dtypes=(F32, BF16), full_rows=True)
    def mlp_act(acc):
        r = jnp.maximum(acc, 0.0)
        return r * r, r

    act, act_root = _mm("mlp_in", hn, wf_mlp_in, "nn", epilogue=mlp_act, out_dtypes=(BF16, BF16))
    inv_n = 1.0 / n_features

    def loss_head(acc, r, t):
        d = ((acc + r) - t) * inv_n
        return d, d, jnp.sum(d * d, keepdims=True) * (0.5 * n_features)

    dy, dy_b, loss_tiles = _mm("mlp_out_loss", act, wf_mlp_out, "nn", extras=[(h1, "tile"), (tgt2, "tile")],
                               epilogue=loss_head, out_dtypes=(F32, BF16), tile_sums=("scalar",))
    loss_part = jnp.sum(loss_tiles)

    (dw_mlp_out,) = _mm("dw_mlp_out", act, dy_b, "tn")
    (dpre,) = _mm("d_mlp_act", dy_b, wf_mlp_out, "nt", extras=[(act_root, "tile")],
                  epilogue=lambda acc, r: acc * (2.0 * r.astype(F32)), out_dtypes=(BF16,))
    (dw_mlp_in,) = _mm("dw_mlp_in", hn, dpre, "tn")
    scatter_mlp, tok_mlp = _chip_exchange_start("scatter_mlp_start", "scatter", [(dw_mlp_in, 1), (dw_mlp_out, 0)])
    def norm_bwd(dn, res, hx, g):
        _, vjp = jax.vjp(_rms, hx, g)
        dh, dg = vjp(dn)
        return res + dh, dg

    dh1, dg_tiles = _mm("d_norm2_in", dpre, wf_mlp_in, "nt", extras=[(dy, "tile"), (h1, "tile"), (g2 + tok_mlp, "row")],
                        epilogue=norm_bwd, tile_sums=("row",), full_rows=True)
    dg_norm2 = jnp.sum(dg_tiles, axis=0, keepdims=True)
    (dmixed,) = _mm("d_mixed", dh1, wf_out, "nt")
    (dw_out,) = _mm("dw_out", mixed, dh1, "tn")

    def mixed_bwd(dm, sb_, ys, gp, ga, gs):
        _, vjp = jax.vjp(lambda a, act, b, c, d: jnp.concatenate(
            [_rms(a, c), _rms(act * jax.nn.sigmoid(b), d)], axis=-1), sb_, _gelu(ys), gp, ga, gs)
        dsb_, dact, dgp_, dga, dgs = vjp(dm)
        return dsb_, dgp_, dact, dga, dgs, jnp.sum(dgp_, axis=0, keepdims=True)

    dsb, dgate_pre, dact_part, dg_attn, dg_ssm, db_glu = _rowwise(
        "mix_norm_bwd", mixed_bwd, [dmixed, sb, y_ssm, gate_pre], [g_attn, g_ssm],
        [(sb_width, F32), (sb_width, BF16), (sb_width, F32)], [(1, sb_width)] * 3)

    def gelu_bwd(acc, part, ys):
        _, vjp = jax.vjp(_gelu, ys)
        return vjp(acc + part)[0]

    (dy_ssm,) = _mm("d_glu_in", dgate_pre, wf_glu, "nt", extras=[(dact_part, "tile"), (y_ssm, "tile")], epilogue=gelu_bwd)
    (dw_glu,) = _mm("dw_glu", y_ssm, dgate_pre, "tn", a_fn=_gelu)
    scatter_mix, tok_mix = _chip_exchange_start("scatter_mix_start", "scatter", [(dw_glu, 0), (dw_out, 0)])

    dug, dkt_row, db_mat, dc_mat, dla, dlb = _s5_bwd(ug, _tokens_to_groups("dy_to_groups", dy_ssm, 0, sb_width), xin,
                                                     kt_row, b_mat, c_mat, la, lb + tok_mix, batch=batch)
    du = _groups_to_tokens("du_to_tokens", dug)
    d_in = _s5_operators_call("s5_operators_bwd", s5_in, (dkt_row, db_mat, dc_mat, dla, dlb))
    ds5 = [d_in[0][:, 0, :] + d_in[2][:, :, 0], d_in[1][:, 0, :] + d_in[3][:, :, 0], d_in[4][:, 0, 0],
           d_in[5].transpose(0, 2, 1), d_in[6].transpose(0, 2, 1), d_in[7].transpose(0, 2, 1), d_in[8].transpose(0, 2, 1),
           d_in[9][:, 0, :]]

    def pack(parts):
        flat = jnp.concatenate([p.reshape(-1) for p in parts])
        rows = -(-flat.shape[0] // (8 * LANES)) * 8
        return jnp.pad(flat, (0, rows * LANES - flat.shape[0])).reshape(rows, LANES)

    def unpack(packed, names):
        flat, out, off = packed.reshape(-1), {}, 0
        for name in names:
            shape = small_shapes[name]
            size = math.prod(shape)
            out[name] = flat[off:off + size].reshape(shape)
            off += size
        return out, flat[off]

    small_shapes = {name: w.shape for name, w, _, _ in small}
    early_names = ["ssm_lambda_re", "ssm_lambda_im", "ssm_log_dt", "ssm_b_re", "ssm_b_im", "ssm_c_re", "ssm_c_im", "ssm_d",
                   "b_glu", "attn_out_g", "ssm_out_g", "norm2_g"]
    late_names = ["norm1_g", "q_norm_g", "k_norm_g"]
    early = pack([*ds5, db_glu[0], dg_attn[0], dg_ssm[0], dg_norm2[0], loss_part])
    early_exchange, _ = _chip_exchange_start("small_early_start", "all", [(early, 0)])

    dqn, dkn, dv = _attn_bwd(qn, kn, vb, c_tot, dsb, batch=batch, seq=seq, bq=ATTN_BQ, bk=ATTN_BK,
                             after=early_exchange[3][0])
    (early_slots,) = _chip_exchange_wait("small_early_wait", early_exchange, dqn)
    small_g, loss = unpack(_sum_slots("sum_small_early", early_slots), early_names)

    def qk_bwd(q, k, dq_, dk_, dv_, du_, gq_, gk_, ones):
        _, vjp_q = jax.vjp(lambda a, g: _head_rms(a, g, ones), q, gq_)
        _, vjp_k = jax.vjp(lambda a, g: _head_rms(a, g, ones), k, gk_)
        dq, dgq = vjp_q(dq_)
        dk, dgk = vjp_k(dk_)
        return jnp.concatenate([dq, dk, dv_, du_], axis=1), dgq, dgk

    dproj, dgq, dgk = _rowwise("qk_norm_bwd", qk_bwd, [(proj, sb_width, 0), (proj, sb_width, 1), dqn, dkn, dv, du],
                               [gq, gk, ones_blocks], [(4 * sb_width, BF16)], [(1, sb_width)] * 2, tm=512)
    (dw_in,) = _mm("dw_in", xn, dproj, "tn")
    scatter_in, tok_w_in = _chip_exchange_start("scatter_in_start", "scatter", [(dw_in, 1)])
    dx, dg_tiles = _mm("d_norm1_in", dproj, wf_in, "nt", extras=[(dh1, "tile"), (x2, "tile"), (g1 + tok_w_in, "row")],
                       epilogue=norm_bwd, tile_sums=("row",), full_rows=True)
    dg_norm1 = jnp.sum(dg_tiles, axis=0, keepdims=True)

    late = pack([dg_norm1[0], dgq.reshape(heads, HEAD_DIM).sum(0) * qk_scale, dgk.reshape(heads, HEAD_DIM).sum(0),
                 jnp.zeros((1,), F32)])
    late_exchange, _ = _chip_exchange_start("small_late_start", "all", [(late, 0)])

    def adam_big(sa, sb_, w, m, v):
        g = sa + sb_
        delta, m, v = _adamw(g, w, m, v)
        return g, delta, m, v

    def reduce_and_update(tag, params, slots):
        mine = [_sum_slots("sum_" + name, s) for s, (name, *_rest) in zip(slots, params)]
        theirs = _swap_with_sibling("swap_" + tag, mine)
        return {name: _rowwise("adamw_" + name, adam_big, [sa, sb_, w, m, v], [], [(w.shape[1], F32)] * 4)
                for (name, w, m, v, _), sa, sb_ in zip(params, mine, theirs)}

    started = late_exchange[3][0]
    slots_mlp_in, slots_mlp_out = _chip_exchange_wait("scatter_mlp_wait", scatter_mlp, started)
    slots_glu, slots_out = _chip_exchange_wait("scatter_mix_wait", scatter_mix, started)
    big_out = reduce_and_update("rest", big[1:], [slots_glu, slots_out, slots_mlp_in, slots_mlp_out])

    (late_slots,) = _chip_exchange_wait("small_late_wait", late_exchange, big_out["w_mlp_out"][3])
    reduced = _sum_slots("sum_small_late", late_slots)
    small_g.update(unpack(reduced, late_names)[0])
    narrow = {name for name, w, _, _ in small if w.ndim == 3 and w.shape[2] < w.shape[1]}

    def flip(a, name):
        return jnp.swapaxes(a, 1, 2) if name in narrow else a

    small_upd = _adamw_small([flip(small_g[name], name) for name, *_ in small], [flip(w, name) for name, w, _, _ in small],
                             [flip(m, name) for name, _, m, _ in small], [flip(v, name) for name, _, _, v in small])
    small_out = [small_g] + [{name: flip(small_upd[kind][i], name) for i, (name, *_) in enumerate(small)}
                             for kind in range(3)]

    (slots_in,) = _chip_exchange_wait("scatter_in_wait", scatter_in, reduced)
    big_out.update(reduce_and_update("w_in", big[:1], [slots_in]))
    names = ["norm1_g", "w_in", "q_norm_g", "k_norm_g", "ssm_lambda_re", "ssm_lambda_im", "ssm_log_dt", "ssm_b_re",
             "ssm_b_im", "ssm_c_re", "ssm_c_im", "ssm_d", "w_glu", "b_glu", "attn_out_g", "ssm_out_g", "w_out",
             "norm2_g", "w_mlp_in", "w_mlp_out"]
    outs = [loss, dx.reshape(batch, seq, d_model)]
    for kind in range(4):
        for name in names:
            outs.append(big_out[name][kind] if name in big_out else small_out[kind][name])
    return tuple(outs)
```

```python
import functools
import math

import jax
import jax.numpy as jnp
from jax import lax
from jax.experimental import pallas as pl
from jax.experimental.pallas import tpu as pltpu

F32 = jnp.float32
BF16 = jnp.bfloat16
F32_DOT = lax.Precision.HIGH
MESH = pl.DeviceIdType.MESH

RMS_EPS = 1e-6
HEAD_DIM = 64
SSM_GROUP = 16
SSM_CHUNK = 16
LANES = 128
N_CHIPS = 4
N_DEV = 8
VMEM_LIMIT = 48 * 1024 * 1024

ADAM_LR = 0.001
ADAM_B1 = 0.9
ADAM_B2 = 0.999
ADAM_EPS = 1e-08
ADAM_WD = 0.01
ADAM_STEP = 10


def _tile(n, pref):
    t = min(n, pref)
    while n % t:
        t //= 2
    return t


def _params(*sem):
    return pltpu.CompilerParams(dimension_semantics=sem, vmem_limit_bytes=VMEM_LIMIT)


_DIMS = {"nn": (((1,), (0,)), ((), ())), "nt": (((1,), (1,)), ((), ())), "tn": (((0,), (0,)), ((), ()))}


MM_VMEM_BUDGET = 40 * 1024 * 1024


def _mm_tiles(m, n, k, a_bytes, b_bytes, tile_bytes, full_rows=False):
    best = None
    for tk in [t for t in (k, k // 2, k // 4, k // 8) if t >= 256 or t == k]:
        for tm in [t for t in (1024, 512, 256, 128) if t <= m and m % t == 0]:
            for tn in [n] if full_rows else [t for t in (1024, 512, 256, 128) if t <= n and n % t == 0]:
                need = 2 * (tm * tk * a_bytes + tk * tn * b_bytes) + 2 * tm * tn * tile_bytes + (tm * tn * 4 if tk < k else 0)
                if need > MM_VMEM_BUDGET:
                    continue
                traffic = m * k * a_bytes * (1 if tk == k else n // tn) + k * n * b_bytes * (1 if n == tn and tk == k else m // tm)
                key = (tk < k, traffic, -tm * tn)
                if best is None or key < best[0]:
                    best = (key, (tm, tn, tk))
    return best[1]


def _mm(name, a, b, mode, *, a_fn=None, extras=(), epilogue=None, out_dtypes=(F32,), tile_sums=(), full_rows=False):
    if mode == "nn":
        (m, k), n = a.shape, b.shape[1]
    elif mode == "nt":
        (m, k), n = a.shape, b.shape[0]
    else:
        (k, m), n = a.shape, b.shape[1]
    tile_bytes = sum(e.dtype.itemsize for e, kind in extras if kind == "tile") + sum(jnp.dtype(d).itemsize for d in out_dtypes)
    tm, tn, tk = _mm_tiles(m, n, k, a.dtype.itemsize, b.dtype.itemsize, tile_bytes, full_rows)
    nk = k // tk
    ne, nout = len(extras), len(out_dtypes)
    dims = _DIMS[mode]

    def body(a_ref, b_ref, *rest):
        ex, outs, sums = rest[:ne], rest[ne:ne + nout], rest[ne + nout:ne + nout + len(tile_sums)]
        at = a_ref[...]
        if a_fn is not None:
            at = a_fn(at)
        p = lax.dot_general(at.astype(BF16), b_ref[...].astype(BF16), dims, preferred_element_type=F32)

        def finish(r):
            if epilogue is not None:
                r = epilogue(r, *[e[...] for e in ex])
            if not isinstance(r, (tuple, list)):
                r = (r,)
            for o, v in zip(outs, r[:nout]):
                o[...] = v.astype(o.dtype)
            for o, v, kind in zip(sums, r[nout:], tile_sums):
                first = lax.broadcasted_iota(jnp.int32, o.shape, 0) == 0
                if kind == "scalar":
                    first &= lax.broadcasted_iota(jnp.int32, o.shape, 1) == 0
                o[...] = jnp.where(first, v, 0.0)

        if nk == 1:
            finish(p)
        else:
            acc = rest[ne + nout + len(tile_sums)]
            kk = pl.program_id(2)

            @pl.when(kk == 0)
            def _():
                acc[...] = p

            @pl.when(kk > 0)
            def _():
                acc[...] += p

            @pl.when(kk == nk - 1)
            def _():
                finish(acc[...])

    if mode == "tn":
        a_spec = pl.BlockSpec((tk, tm), lambda i, j, kk: (kk, i))
    else:
        a_spec = pl.BlockSpec((tm, tk), lambda i, j, kk: (i, kk))
    if mode == "nt":
        b_spec = pl.BlockSpec((tn, tk), lambda i, j, kk: (j, kk))
    else:
        b_spec = pl.BlockSpec((tk, tn), lambda i, j, kk: (kk, j))
    ex_specs = []
    for _, kind in extras:
        if kind == "tile":
            ex_specs.append(pl.BlockSpec((tm, tn), lambda i, j, kk: (i, j)))
        else:
            ex_specs.append(pl.BlockSpec((1, tn), lambda i, j, kk: (0, j)))
    return pl.pallas_call(
        body, name=name, grid=(m // tm, n // tn, nk),
        in_specs=[a_spec, b_spec] + ex_specs,
        out_specs=([pl.BlockSpec((tm, tn), lambda i, j, kk: (i, j)) for _ in out_dtypes]
                   + [pl.BlockSpec((8, LANES if kind == "scalar" else tn), lambda i, j, kk: (i, j)) for kind in tile_sums]),
        out_shape=([jax.ShapeDtypeStruct((m, n), dt) for dt in out_dtypes]
                   + [jax.ShapeDtypeStruct((m // tm * 8, n // tn * LANES if kind == "scalar" else n), F32)
                      for kind in tile_sums]),
        scratch_shapes=[pltpu.VMEM((tm, tn), F32)] if nk > 1 else [],
        compiler_params=_params("parallel", "parallel", "arbitrary"),
    )(a, b, *[e for e, _ in extras])


def _rowwise(name, fn, rows, consts, row_outs, acc_outs=(), tm=256):
    norm = [r if isinstance(r, tuple) else (r, r.shape[1], 0) for r in rows]
    t = norm[0][0].shape[0]
    tm = _tile(t, tm)
    nr, nc, no = len(norm), len(consts), len(row_outs)

    def body(*refs):
        outs = fn(*[r[...] for r in refs[:nr + nc]])
        if not isinstance(outs, (tuple, list)):
            outs = (outs,)
        o_refs, a_refs = refs[nr + nc:nr + nc + no], refs[nr + nc + no:]
        for r, v in zip(o_refs, outs[:no]):
            r[...] = v.astype(r.dtype)
        if a_refs:
            i = pl.program_id(0)

            @pl.when(i == 0)
            def _():
                for r, v in zip(a_refs, outs[no:]):
                    r[...] = v

            @pl.when(i > 0)
            def _():
                for r, v in zip(a_refs, outs[no:]):
                    r[...] += v

    in_specs = [pl.BlockSpec((tm, w), functools.partial(lambda i, cb: (i, cb), cb=cb)) for _, w, cb in norm]
    in_specs += [pl.BlockSpec(c.shape, functools.partial(lambda i, nd: (0,) * nd, nd=c.ndim)) for c in consts]
    out_specs = [pl.BlockSpec((tm, w), lambda i: (i, 0)) for w, _ in row_outs]
    out_specs += [pl.BlockSpec(s, functools.partial(lambda i, nd: (0,) * nd, nd=len(s))) for s in acc_outs]
    out_shape = [jax.ShapeDtypeStruct((t, w), dt) for w, dt in row_outs]
    out_shape += [jax.ShapeDtypeStruct(s, F32) for s in acc_outs]
    return pl.pallas_call(
        body, name=name, grid=(t // tm,), in_specs=in_specs, out_specs=out_specs, out_shape=out_shape,
        compiler_params=_params("arbitrary"),
    )(*[r[0] for r in norm], *consts)


def _rms(x, g):
    return x * lax.rsqrt(jnp.mean(x * x, axis=-1, keepdims=True) + RMS_EPS) * g


@jax.custom_vjp
def _head_sums(x, ones_blocks):
    parts = [jnp.dot(x[:, j:j + LANES], ones_blocks, precision=F32_DOT, preferred_element_type=F32)
             for j in range(0, x.shape[1], LANES)]
    return jnp.concatenate(parts, axis=1)


_head_sums.defvjp(lambda x, ones_blocks: (_head_sums(x, ones_blocks), ones_blocks),
                  lambda ones_blocks, ct: (_head_sums(ct, ones_blocks), None))


def _head_rms(x, g, ones_blocks):
    return x * lax.rsqrt(_head_sums(x * x, ones_blocks) * (1.0 / HEAD_DIM) + RMS_EPS) * g


def _gelu(x):
    return x * (0.5 * (1.0 + jnp.tanh(math.sqrt(2.0 / math.pi) * (x + 0.044715 * (x * x * x)))))


def _mixed(sb, y_ssm, gate_pre, g_attn, g_ssm):
    ssm = _gelu(y_ssm) * jax.nn.sigmoid(gate_pre)
    return jnp.concatenate([_rms(sb, g_attn), _rms(ssm, g_ssm)], axis=-1)


def _softplus(z):
    return jnp.maximum(z, 0.0) + jnp.log(1.0 + jnp.exp(-jnp.abs(z)))


def _running_sums(x, tri):
    return jnp.dot(x.astype(BF16), tri, preferred_element_type=F32)


def _dot_nt(a, b, **kw):
    return lax.dot_general(a, b, _DIMS["nt"], preferred_element_type=F32, **kw)


def _dot_tn(a, b, **kw):
    return lax.dot_general(a, b, _DIMS["tn"], preferred_element_type=F32, **kw)


ATTN_BQ, ATTN_BK = 2048, 256
HEAD_LANES = tuple(slice(h * HEAD_DIM, (h + 1) * HEAD_DIM) for h in range(LANES // HEAD_DIM))


def _attn_fwd(qs, kn, v, *, batch, seq, bq, bk):
    width = qs.shape[1]
    bq = _tile(seq, bq)
    bk = _tile(bq, bk)
    nq, kpq = seq // bq, bq // bk

    def body(q_ref, k_ref, v_ref, o_ref, c_ref):
        row = lax.broadcasted_iota(jnp.int32, (bq, bk), 0)
        col = lax.broadcasted_iota(jnp.int32, (bq, bk), 1)
        tri = (lax.broadcasted_iota(jnp.int32, (bk, bk), 0) >= lax.broadcasted_iota(jnp.int32, (bk, bk), 1)).astype(BF16)

        def q_block(qi, carry):
            r0 = pl.multiple_of(qi * bq, bq)
            qh = [q_ref[pl.ds(r0, bq), ln] for ln in HEAD_LANES]

            def tile(k0, state, top=0):
                diag = top is not None
                top = top or 0
                msk = (col < row)[:bq - top] if diag else None
                new = []
                for h, ln in enumerate(HEAD_LANES):
                    o, c = state[2 * h], state[2 * h + 1]
                    z = _dot_nt(qh[h][top:], k_ref[pl.ds(k0, bk), ln])
                    sp = _softplus(z)
                    if diag:
                        sp = jnp.where(msk, sp, 0.0)
                    r = _running_sums(sp, tri)
                    a = jnp.exp(z - r - c[top:])
                    if diag:
                        a = jnp.where(msk, a, 0.0)
                    o_new = o[top:] + jnp.dot(a.astype(BF16), v_ref[pl.ds(k0, bk), ln], preferred_element_type=F32)
                    c_new = c[top:] + r[:, 0:1]
                    if top:
                        o_new, c_new = jnp.concatenate([o[:top], o_new]), jnp.concatenate([c[:top], c_new])
                    new += [o_new, c_new]
                return tuple(new)

            state = (jnp.zeros((bq, HEAD_DIM), F32), jnp.zeros((bq, 1), F32)) * len(HEAD_LANES)
            for d in reversed(range(kpq)):
                state = tile(pl.multiple_of(r0 + d * bk, bk), state, top=d * bk)
            state = lax.fori_loop(0, qi * kpq, lambda it, st: tile(pl.multiple_of(r0 - (it + 1) * bk, bk), st, None),
                                  state)
            for h, ln in enumerate(HEAD_LANES):
                o_ref[pl.ds(r0, bq), ln] = state[2 * h]
                c_ref[pl.ds(r0, bq), ln] = jnp.broadcast_to(state[2 * h + 1], (bq, HEAD_DIM))
            return carry

        lax.fori_loop(0, nq, q_block, 0)

    spec = pl.BlockSpec((seq, LANES), lambda b, h: (b, h))
    shape = jax.ShapeDtypeStruct((batch * seq, width), F32)
    return pl.pallas_call(
        body, name="attn_fwd", grid=(batch, width // LANES), in_specs=[spec, spec, spec], out_specs=[spec, spec],
        out_shape=[shape, shape], compiler_params=_params("parallel", "parallel"),
    )(qs, kn, v)


def _attn_bwd(qs, kn, v, c_tot, do, *, batch, seq, bq, bk, after):
    width = qs.shape[1]
    bq = _tile(seq, bq)
    bk = _tile(bq, bk)
    nq, kpq = seq // bq, bq // bk

    def body(q_ref, k_ref, v_ref, c_ref, do_ref, after_ref, dq_ref, dk_ref, dv_ref):
        row = lax.broadcasted_iota(jnp.int32, (bq, bk), 0)
        col = lax.broadcasted_iota(jnp.int32, (bq, bk), 1)
        sq_row = lax.broadcasted_iota(jnp.int32, (bk, bk), 0)
        sq_col = lax.broadcasted_iota(jnp.int32, (bk, bk), 1)
        tri = (sq_row >= sq_col).astype(BF16)
        tri_t = (sq_row <= sq_col).astype(BF16)
        dk_ref[...] = jnp.zeros_like(dk_ref)
        dv_ref[...] = jnp.zeros_like(dv_ref)

        def q_block(qi, carry):
            r0 = pl.multiple_of(qi * bq, bq)
            qh = [q_ref[pl.ds(r0, bq), ln] for ln in HEAD_LANES]
            d_out = [do_ref[pl.ds(r0, bq), ln].astype(BF16) for ln in HEAD_LANES]
            c_all = [c_ref[pl.ds(r0, bq), ln][:, 0:1] for ln in HEAD_LANES]

            def tile(k0, state, top=0):
                diag = top is not None
                top = top or 0
                last = diag and top == bq - bk
                msk = (col < row)[:bq - top] if diag else None
                new = []
                for h, ln in enumerate(HEAD_LANES):
                    c_left, g_left, dq = state[3 * h:3 * h + 3]
                    q, d_o = qh[h][top:], d_out[h][top:]
                    k = k_ref[pl.ds(k0, bk), ln]
                    z = _dot_nt(q, k)
                    e = jnp.exp(-jnp.abs(z))
                    sp = jnp.maximum(z, 0.0) + jnp.log(1.0 + e)
                    sig = jnp.exp(z - sp)
                    if diag:
                        sp = jnp.where(msk, sp, 0.0)
                    r = _running_sums(sp, tri)
                    c_new = c_left[top:] + r[:, 0:1]
                    a = jnp.exp(z - r - (0.0 if last else c_all[h][top:] - c_new))
                    if diag:
                        a = jnp.where(msk, a, 0.0)
                    g = a * _dot_nt(d_o, v_ref[pl.ds(k0, bk), ln])
                    pg = _running_sums(g, tri_t)
                    dz = g - sig * (g_left[top:] + pg)
                    if diag:
                        dz = jnp.where(msk, dz, 0.0)
                    dz = dz.astype(BF16)
                    dk_ref[pl.ds(k0, bk), ln] += _dot_tn(dz, q)
                    dv_ref[pl.ds(k0, bk), ln] += _dot_tn(a.astype(BF16), d_o)
                    g_new = g_left[top:] + pg[:, bk - 1:bk]
                    dq_new = dq[top:] + jnp.dot(dz, k, preferred_element_type=F32)
                    if top:
                        c_new = jnp.concatenate([c_left[:top], c_new])
                        g_new = jnp.concatenate([g_left[:top], g_new])
                        dq_new = jnp.concatenate([dq[:top], dq_new])
                    new += [c_new, g_new, dq_new]
                return tuple(new)

            zero = jnp.zeros((bq, 1), F32)
            init = (zero, zero, jnp.zeros((bq, HEAD_DIM), F32)) * len(HEAD_LANES)
            state = lax.fori_loop(0, qi * kpq, lambda it, st: tile(pl.multiple_of(it * bk, bk), st, None), init)
            for d in range(kpq):
                state = tile(pl.multiple_of(r0 + d * bk, bk), state, top=d * bk)
            for h, ln in enumerate(HEAD_LANES):
                dq_ref[pl.ds(r0, bq), ln] = state[3 * h + 2]
            return carry

        lax.fori_loop(0, nq, q_block, 0)

    spec = pl.BlockSpec((seq, LANES), lambda b, h: (b, h))
    shape = jax.ShapeDtypeStruct((batch * seq, width), F32)
    return pl.pallas_call(
        body, name="attn_bwd", grid=(batch, width // LANES),
        in_specs=[spec] * 5 + [pl.BlockSpec(memory_space=pl.ANY)], out_specs=[spec] * 3,
        out_shape=[shape] * 3, compiler_params=_params("parallel", "parallel"),
    )(qs, kn, v, c_tot, do, after)


def _pattern(rows, cols, hit):
    r, c = lax.broadcasted_iota(jnp.int32, (rows, cols), 0), lax.broadcasted_iota(jnp.int32, (rows, cols), 1)
    return hit(r, c).astype(F32)


def _s5_group_operators(lr_r, li_r, lr_c, li_c, log_dt, bt_re, bt_im, ct_re, ct_im, d_row):
    cs = SSM_CHUNK
    n_ch, n_state = bt_re.shape
    width = cs * n_ch
    dt = jnp.exp(log_dt)

    def spread(x, pattern):
        return jnp.dot(x, pattern, precision=F32_DOT, preferred_element_type=F32)

    twice = _pattern(n_state, 2 * n_state, lambda r, c: r == c % n_state)
    steps = lax.broadcasted_iota(jnp.int32, (cs + 1, 1), 0).astype(F32)
    mag = jnp.exp(steps * (lr_r * dt))
    ang = steps * (li_r * dt)
    pw_re, pw_im = mag * jnp.cos(ang), mag * jnp.sin(ang)
    num_re, num_im = pw_re[1:2] - 1.0, pw_im[1:2]
    den = lr_r * lr_r + li_r * li_r
    cf_re = (num_re * lr_r + num_im * li_r) / den
    cf_im = (num_im * lr_r - num_re * li_r) / den
    bb_re = spread(cf_re * bt_re - cf_im * bt_im, twice)
    bb_im = spread(cf_re * bt_im + cf_im * bt_re, twice)
    pw2_re, pw2_im = spread(pw_re, twice), spread(pw_im, twice)
    real_half = lax.broadcasted_iota(jnp.int32, (1, 2 * n_state), 1) < n_state
    blocks = []
    for s in range(cs):
        pr, pi = pw2_re[cs - 1 - s:cs - s], pw2_im[cs - 1 - s:cs - s]
        blocks.append(jnp.where(real_half, bb_re * pr - bb_im * pi, bb_re * pi + bb_im * pr))
    b_mat = jnp.concatenate(blocks, axis=0)
    la = pw2_re[cs:cs + 1]
    lb = jnp.where(real_half, -pw2_im[cs:cs + 1], pw2_im[cs:cs + 1])

    lane = lax.broadcasted_iota(jnp.int32, (1, width), 1)
    tile_out = _pattern(n_ch, width, lambda r, c: r == c % n_ch)
    c_re, c_im = spread(ct_re, tile_out), spread(ct_im, tile_out)

    k_row = lax.broadcasted_iota(jnp.int32, (1, cs), 1).astype(F32)
    m, a = jnp.exp(k_row * (lr_c * dt)), k_row * (li_c * dt)
    repeat = _pattern(cs, width, lambda r, c: r == c // n_ch)
    p_re, p_im = spread(m * jnp.cos(a), repeat), spread(m * jnp.sin(a), repeat)
    w_re, w_im = p_re * c_re - p_im * c_im, p_re * c_im + p_im * c_re
    skip = jnp.where((lane < n_ch) & (lane == lax.broadcasted_iota(jnp.int32, (n_ch, width), 0)),
                     spread(d_row, tile_out), 0.0)
    kt_row = (jnp.dot(bb_re[:, :n_state], w_re, precision=F32_DOT, preferred_element_type=F32)
              - jnp.dot(bb_im[:, :n_state], w_im, precision=F32_DOT, preferred_element_type=F32) + skip)
    bar_re, bar_im = jnp.exp(lr_c * dt) * jnp.cos(li_c * dt), jnp.exp(lr_c * dt) * jnp.sin(li_c * dt)
    w1_re, w1_im = w_re * bar_re - w_im * bar_im, w_re * bar_im + w_im * bar_re
    c_mat = jnp.concatenate([w1_re, -w1_im], axis=0)
    return kt_row, b_mat, c_mat, la, lb


def _s5_operator_inputs(lam_re, lam_im, log_dt, b_re, b_im, c_re, c_im, d_skip):
    return (lam_re[:, None, :], lam_im[:, None, :], lam_re[:, :, None], lam_im[:, :, None], log_dt[:, None, None],
            b_re.transpose(0, 2, 1), b_im.transpose(0, 2, 1), c_re.transpose(0, 2, 1), c_im.transpose(0, 2, 1),
            d_skip[:, None, :])


def _s5_operators_call(name, args, cotangents=None, gb=8):
    groups = args[0].shape[0]
    gb = _tile(groups, gb)
    n_in = len(args)

    def body(*refs):
        n_ct = 0 if cotangents is None else len(cotangents)
        ins, cts, outs = refs[:n_in], refs[n_in:n_in + n_ct], refs[n_in + n_ct:]
        for g in range(gb):
            vals = [r[g] for r in ins]
            if cotangents is None:
                res = _s5_group_operators(*vals)
            else:
                res = jax.vjp(_s5_group_operators, *vals)[1](tuple(c[g] for c in cts))
            for o, v in zip(outs, res):
                o[g] = v

    def spec(a):
        return pl.BlockSpec((gb, *a.shape[1:]), lambda i: (i, 0, 0))

    if cotangents is None:
        n_ch, n_state = args[5].shape[1:]
        width = SSM_CHUNK * n_ch
        out_shape = [jax.ShapeDtypeStruct((groups, *s), F32) for s in
                     ((n_ch, width), (width, 2 * n_state), (2 * n_state, width), (1, 2 * n_state), (1, 2 * n_state))]
    else:
        out_shape = [jax.ShapeDtypeStruct(a.shape, F32) for a in args]
    operands = [*args, *(cotangents or ())]
    return pl.pallas_call(
        body, name=name, grid=(groups // gb,), in_specs=[spec(a) for a in operands], out_specs=[spec(s) for s in out_shape],
        out_shape=out_shape, compiler_params=_params("parallel"),
    )(*operands)


GROUPS_PER_BLOCK = LANES // SSM_GROUP


def _tokens_to_groups(name, u, col_block, width):
    t = u.shape[0]
    n = t // SSM_CHUNK
    ch = SSM_CHUNK * SSM_GROUP
    blocks = width // LANES

    def body(u_ref, o_ref):
        for s in range(SSM_CHUNK):
            rows = u_ref[pl.ds(s, n, stride=SSM_CHUNK), :]
            for g in range(GROUPS_PER_BLOCK):
                o_ref[g, :, s * SSM_GROUP:(s + 1) * SSM_GROUP] = rows[:, g * SSM_GROUP:(g + 1) * SSM_GROUP]

    return pl.pallas_call(
        body, name=name, grid=(blocks,),
        in_specs=[pl.BlockSpec((t, LANES), lambda j: (0, col_block * blocks + j))],
        out_specs=pl.BlockSpec((GROUPS_PER_BLOCK, n, ch), lambda j: (j, 0, 0)),
        out_shape=jax.ShapeDtypeStruct((width // SSM_GROUP, n, ch), F32), compiler_params=_params("parallel"),
    )(u)


def _groups_to_tokens(name, ug):
    groups, n, ch = ug.shape

    def body(g_ref, o_ref, rows_ref):
        for s in range(SSM_CHUNK):
            for g in range(GROUPS_PER_BLOCK):
                rows_ref[s % 2, :, g * SSM_GROUP:(g + 1) * SSM_GROUP] = g_ref[g, :, s * SSM_GROUP:(s + 1) * SSM_GROUP]
            o_ref[pl.ds(s, n, stride=SSM_CHUNK), :] = rows_ref[s % 2]

    return pl.pallas_call(
        body, name=name, grid=(groups // GROUPS_PER_BLOCK,),
        in_specs=[pl.BlockSpec((GROUPS_PER_BLOCK, n, ch), lambda j: (j, 0, 0))],
        out_specs=pl.BlockSpec((n * SSM_CHUNK, LANES), lambda j: (0, j)),
        out_shape=jax.ShapeDtypeStruct((n * SSM_CHUNK, groups * SSM_GROUP), F32),
        scratch_shapes=[pltpu.VMEM((2, n, LANES), F32)], compiler_params=_params("parallel"),
    )(ug)


SCAN_ROWS = 8


def _toeplitz_to(tm_ref, g, kt_row):
    width = kt_row.shape[1]
    tm_ref[g] = jnp.zeros((width, width), F32)
    for s in range(SSM_CHUNK):
        tm_ref[g, s * SSM_GROUP:(s + 1) * SSM_GROUP, s * SSM_GROUP:] = kt_row[:, :width - s * SSM_GROUP]


def _lam_powers(la, lb, reverse):
    if reverse:
        lb = -lb

    def mul(p, q):
        return p[0] * q[0] - p[1] * q[1], p[0] * q[1] + p[1] * q[0]

    p1 = (la, lb)
    p2 = mul(p1, p1)
    p3 = mul(p2, p1)
    p4 = mul(p2, p2)
    rows = [p1, p2, p3, p4, mul(p4, p1), mul(p4, p2), mul(p4, p3), mul(p4, p4)]
    if reverse:
        rows = rows[::-1]
    idx = lax.broadcasted_iota(jnp.int32, (SCAN_ROWS, la.shape[1]), 0)
    tab_a = sum(jnp.where(idx == j, r[0], 0.0) for j, r in enumerate(rows))
    tab_b = sum(jnp.where(idx == j, r[1], 0.0) for j, r in enumerate(rows))
    return (p1, p2, p4), (tab_a, tab_b), idx


def _scan_block(e, carry, steps, table, idx, half, reverse):
    n = SCAN_ROWS
    for d, (pa, pb) in zip((1, 2, 4), steps):
        sh = pltpu.roll(e, n - d if reverse else d, 0)
        sh = jnp.where(idx < n - d if reverse else idx >= d, sh, 0.0)
        e = e + pa * sh + pb * pltpu.roll(sh, half, 1)
    tab_a, tab_b = table
    e = e + tab_a * carry + tab_b * pltpu.roll(carry, half, 1)
    shifted = jnp.where(idx == (n - 1 if reverse else 0), carry, pltpu.roll(e, n - 1 if reverse else 1, 0))
    edge = e[0:1] if reverse else e[n - 1:n]
    return shifted, jnp.broadcast_to(edge, e.shape)


def _s5_fwd(ug, kt_row, b_mat, c_mat, la, lb, *, batch, gb=8):
    groups, n, ch = ug.shape
    p2 = b_mat.shape[2]
    gb = _tile(groups, gb)
    nch = n // batch
    nblk = nch // SCAN_ROWS

    def body(u_ref, k_ref, b_ref, c_ref, la_ref, lb_ref, y_ref, x_ref, s_ref, tm_ref):
        for g in range(gb):
            _toeplitz_to(tm_ref, g, k_ref[g])
            s_ref[g] = jnp.dot(u_ref[g], b_ref[g], precision=F32_DOT, preferred_element_type=F32)
        powers = [_lam_powers(la_ref[g], lb_ref[g], False) for g in range(gb)]

        def step(blk, carries):
            new = []
            for g in range(gb):
                steps, table, idx = powers[g]
                for b in range(batch):
                    rows = pl.ds(pl.multiple_of(b * nch + blk * SCAN_ROWS, SCAN_ROWS), SCAN_ROWS)
                    x_in, carry = _scan_block(s_ref[g, rows, :], carries[g * batch + b], steps, table, idx, p2 // 2, False)
                    x_ref[g, rows, :] = x_in
                    new.append(carry)
            return tuple(new)

        lax.fori_loop(0, nblk, step, tuple(jnp.zeros((SCAN_ROWS, p2), F32) for _ in range(gb * batch)))
        for g in range(gb):
            y_ref[g] = (jnp.dot(u_ref[g], tm_ref[g], precision=F32_DOT, preferred_element_type=F32)
                        + jnp.dot(x_ref[g], c_ref[g], precision=F32_DOT, preferred_element_type=F32))

    def spec(a, b):
        return pl.BlockSpec((gb, a, b), lambda i: (i, 0, 0))

    return pl.pallas_call(
        body, name="s5_fwd", grid=(groups // gb,),
        in_specs=[spec(n, ch), spec(SSM_GROUP, ch), spec(ch, p2), spec(p2, ch), spec(1, p2), spec(1, p2)],
        out_specs=[spec(n, ch), spec(n, p2)],
        out_shape=[jax.ShapeDtypeStruct((groups, n, ch), F32), jax.ShapeDtypeStruct((groups, n, p2), F32)],
        scratch_shapes=[pltpu.VMEM((gb, n, p2), F32), pltpu.VMEM((gb, ch, ch), F32)],
        compiler_params=_params("parallel"),
    )(ug, kt_row, b_mat, c_mat, la, lb)


def _s5_bwd(ug, dyg, xin, kt_row, b_mat, c_mat, la, lb, *, batch, gb=8):
    groups, n, ch = ug.shape
    p2 = b_mat.shape[2]
    gb = _tile(groups, gb)
    nch = n // batch
    nblk = nch // SCAN_ROWS

    def body(u_ref, dy_ref, x_ref, k_ref, b_ref, c_ref, la_ref, lb_ref,
             du_ref, dk_ref, db_ref, dc_ref, dla_ref, dlb_ref, dx_ref, ds_ref, tm_ref):
        for g in range(gb):
            _toeplitz_to(tm_ref, g, k_ref[g])
            dx_ref[g] = _dot_nt(dy_ref[g], c_ref[g], precision=F32_DOT)
        powers = [_lam_powers(la_ref[g], lb_ref[g], True) for g in range(gb)]

        def step(it, carries):
            new = []
            for g in range(gb):
                steps, table, idx = powers[g]
                for b in range(batch):
                    rows = pl.ds(pl.multiple_of(b * nch + (nblk - 1 - it) * SCAN_ROWS, SCAN_ROWS), SCAN_ROWS)
                    d_s, carry = _scan_block(dx_ref[g, rows, :], carries[g * batch + b], steps, table, idx, p2 // 2, True)
                    ds_ref[g, rows, :] = d_s
                    new.append(carry)
            return tuple(new)

        lax.fori_loop(0, nblk, step, tuple(jnp.zeros((SCAN_ROWS, p2), F32) for _ in range(gb * batch)))
        for g in range(gb):
            u, dy, ds, x = u_ref[g], dy_ref[g], ds_ref[g], x_ref[g]
            du_ref[g] = _dot_nt(dy, tm_ref[g], precision=F32_DOT) + _dot_nt(ds, b_ref[g], precision=F32_DOT)
            tm_ref[g] = _dot_tn(u, dy, precision=F32_DOT)
            dk_ref[g] = tm_ref[g, 0:SSM_GROUP, :]
            for s in range(1, SSM_CHUNK):
                dk_ref[g, :, :ch - s * SSM_GROUP] += tm_ref[g, s * SSM_GROUP:(s + 1) * SSM_GROUP, s * SSM_GROUP:]
            db_ref[g] = _dot_tn(u, ds, precision=F32_DOT)
            dc_ref[g] = _dot_tn(x, dy, precision=F32_DOT)
            dla_ref[g] = jnp.sum(ds * x, axis=0, keepdims=True)
            dlb_ref[g] = jnp.sum(ds * pltpu.roll(x, p2 // 2, 1), axis=0, keepdims=True)

    def spec(a, b):
        return pl.BlockSpec((gb, a, b), lambda i: (i, 0, 0))

    def shape(a, b):
        return jax.ShapeDtypeStruct((groups, a, b), F32)

    return pl.pallas_call(
        body, name="s5_bwd", grid=(groups // gb,),
        in_specs=[spec(n, ch), spec(n, ch), spec(n, p2), spec(SSM_GROUP, ch), spec(ch, p2), spec(p2, ch), spec(1, p2),
                  spec(1, p2)],
        out_specs=[spec(n, ch), spec(SSM_GROUP, ch), spec(ch, p2), spec(p2, ch), spec(1, p2), spec(1, p2)],
        out_shape=[shape(n, ch), shape(SSM_GROUP, ch), shape(ch, p2), shape(p2, ch), shape(1, p2), shape(1, p2)],
        scratch_shapes=[pltpu.VMEM((gb, n, p2), F32), pltpu.VMEM((gb, n, p2), F32), pltpu.VMEM((gb, ch, ch), F32)],
        compiler_params=_params("parallel"),
    )(ug, dyg, xin, kt_row, b_mat, c_mat, la, lb)


def _block(ref, axis, j, size):
    start = j * size if isinstance(j, int) else pl.multiple_of(j * size, size)
    return ref.at[pl.ds(start, size), :] if axis == 0 else ref.at[:, pl.ds(start, size)]


def _chip_exchange_copies(mode, axes, srcs, lands, send_sems, recv_sems, local_sems):
    x, y, c = lax.axis_index("x"), lax.axis_index("y"), lax.axis_index("c")
    everyone = mode == "all"
    me = 4 * x + 2 * y + c if everyone else 2 * x + y
    n_peers = _exchange_peers(mode)
    local, sends, arrivals = [], [], []
    for w, axis in enumerate(axes):
        if mode == "gather":
            size = srcs[w].shape[axis]
            local.append(pltpu.make_async_copy(srcs[w], _block(lands[w], axis, me, size), local_sems.at[w]))
        elif mode == "scatter":
            size = srcs[w].shape[axis] // N_CHIPS
            local.append(pltpu.make_async_copy(_block(srcs[w], axis, me, size), lands[w].at[me], local_sems.at[w]))
        else:
            local.append(pltpu.make_async_copy(srcs[w], lands[w].at[me], local_sems.at[w]))
        for k in range(1, n_peers + 1):
            bits = k if everyone else 2 * k
            px = 1 - x if bits & 4 else x
            py = 1 - y if bits & 2 else y
            pc = 1 - c if bits & 1 else c
            peer = 4 * px + 2 * py + pc if everyone else 2 * px + py
            if mode == "gather":
                src, dst, arrive = srcs[w], _block(lands[w], axis, me, size), _block(lands[w], axis, peer, size)
            elif mode == "scatter":
                src, dst, arrive = _block(srcs[w], axis, peer, size), lands[w].at[me], lands[w].at[peer]
            else:
                src, dst, arrive = srcs[w], lands[w].at[me], lands[w].at[peer]
            sem = w * n_peers + k - 1
            for target, out in ((dst, sends), (arrive, arrivals)):
                out.append(pltpu.make_async_remote_copy(
                    src_ref=src, dst_ref=target, send_sem=send_sems.at[sem], recv_sem=recv_sems.at[sem],
                    device_id=(px, py, pc), device_id_type=MESH))
    return local, sends, arrivals


def _exchange_peers(mode):
    return N_DEV - 1 if mode == "all" else N_CHIPS - 1


def _chip_exchange_start(name, mode, items, after=None):
    n = len(items)
    n_after = 0 if after is None else 1
    axes = [axis for _, axis in items]
    hbm = pl.BlockSpec(memory_space=pltpu.HBM)
    sem = pl.BlockSpec(memory_space=pltpu.SEMAPHORE)
    lands = []
    for a, axis in items:
        shape = list(a.shape)
        if mode == "gather":
            shape[axis] *= N_CHIPS
        elif mode == "scatter":
            shape[axis] //= N_CHIPS
            shape = [N_CHIPS] + shape
        else:
            shape = [N_DEV] + shape
        lands.append(pltpu.with_memory_space_constraint(lax.empty(tuple(shape), a.dtype), pltpu.HBM))

    def body(*refs):
        srcs, land_refs = refs[:n], refs[n:2 * n]
        send_sems, recv_sems, local_sems = refs[2 * n + n_after:2 * n + n_after + 3]
        token = refs[-1]
        local, sends, _ = _chip_exchange_copies(mode, axes, srcs, land_refs, send_sems, recv_sems, local_sems)
        for cp in local + sends:
            cp.start()
        token[...] = jnp.zeros_like(token)

    n_sem = n * _exchange_peers(mode)
    outs = pl.pallas_call(
        body, name=name,
        out_shape=(pltpu.SemaphoreType.DMA((n_sem,)), pltpu.SemaphoreType.DMA((n_sem,)), pltpu.SemaphoreType.DMA((n,)),
                   *[pltpu.HBM(a.shape, a.dtype) for a, _ in items], *[pltpu.HBM(l.shape, l.dtype) for l in lands],
                   jax.ShapeDtypeStruct((8, LANES), F32)),
        in_specs=[hbm] * (2 * n) + [pl.BlockSpec(memory_space=pl.ANY)] * n_after,
        out_specs=(sem, sem, sem, *[hbm] * (2 * n), pl.BlockSpec(memory_space=pltpu.VMEM)),
        input_output_aliases={i: 3 + i for i in range(2 * n)},
        compiler_params=pltpu.CompilerParams(has_side_effects=pltpu.SideEffectType.DATAFLOW_SIDE_EFFECTING),
    )(*[pltpu.with_memory_space_constraint(a, pltpu.HBM) for a, _ in items], *lands, *([after] if n_after else []))
    return (mode, axes, outs[:3], outs[3:3 + n], outs[3 + n:3 + 2 * n]), outs[-1][0:1, 0:1]


def _chip_exchange_wait(name, handle, after):
    mode, axes, sems, srcs, lands = handle
    n = len(axes)
    after = list(after) if isinstance(after, (tuple, list)) else [after]
    hbm = pl.BlockSpec(memory_space=pltpu.HBM)
    sem = pl.BlockSpec(memory_space=pltpu.SEMAPHORE)

    def body(*refs):
        src_refs, land_refs = refs[:n], refs[n:2 * n]
        send_sems, recv_sems, local_sems = refs[2 * n:2 * n + 3]
        local, sends, arrivals = _chip_exchange_copies(mode, axes, src_refs, land_refs, send_sems, recv_sems, local_sems)
        for cp in sends:
            cp.wait_send()
        for cp in arrivals:
            cp.wait_recv()
        for cp in local:
            cp.wait()

    outs = pl.pallas_call(
        body, name=name,
        out_shape=(*[pltpu.HBM(a.shape, a.dtype) for a in srcs], *[pltpu.HBM(l.shape, l.dtype) for l in lands]),
        in_specs=[hbm] * (2 * n) + [sem] * 3 + [pl.BlockSpec(memory_space=pl.ANY)] * len(after), out_specs=[hbm] * (2 * n),
        input_output_aliases={i: i for i in range(2 * n)},
        compiler_params=pltpu.CompilerParams(has_side_effects=pltpu.SideEffectType.DATAFLOW_SIDE_EFFECTING),
    )(*srcs, *lands, *sems, *after)
    return outs[n:]


def _sum_slots(name, slots, tm=256):
    n_slots, r, c = slots.shape
    tm = _tile(r, tm)

    def body(*refs):
        acc = refs[0][...]
        for s_ref in refs[1:n_slots]:
            acc = acc + s_ref[...]
        refs[n_slots][...] = acc

    specs = [pl.BlockSpec((None, tm, c), functools.partial(lambda i, s: (s, i, 0), s=s)) for s in range(n_slots)]
    return pl.pallas_call(
        body, name=name, grid=(r // tm,), in_specs=specs, out_specs=pl.BlockSpec((tm, c), lambda i: (i, 0)),
        out_shape=jax.ShapeDtypeStruct((r, c), F32), compiler_params=_params("parallel"),
    )(*[slots] * n_slots)


def _swap_with_sibling(name, arrays):
    n = len(arrays)
    hbm = pl.BlockSpec(memory_space=pl.ANY)

    def body(*refs):
        ins, outs = refs[:n], refs[n:2 * n]
        send_sems, recv_sems = refs[2 * n:]
        sibling = (lax.axis_index("x"), lax.axis_index("y"), 1 - lax.axis_index("c"))
        copies = [pltpu.make_async_remote_copy(src_ref=ins[w], dst_ref=outs[w], send_sem=send_sems.at[w],
                                               recv_sem=recv_sems.at[w], device_id=sibling, device_id_type=MESH)
                  for w in range(n)]
        for cp in copies:
            cp.start()
        for cp in copies:
            cp.wait()

    return pl.pallas_call(
        body, name=name, in_specs=[hbm] * n, out_specs=[hbm] * n,
        out_shape=[jax.ShapeDtypeStruct(a.shape, a.dtype) for a in arrays],
        scratch_shapes=[pltpu.SemaphoreType.DMA((n,)), pltpu.SemaphoreType.DMA((n,))],
    )(*arrays)


def _adamw(g, w, m, v):
    m = ADAM_B1 * m + (1.0 - ADAM_B1) * g
    v = ADAM_B2 * v + (1.0 - ADAM_B2) * jnp.square(g)
    m_hat = m / (1.0 - ADAM_B1 ** ADAM_STEP)
    v_hat = v / (1.0 - ADAM_B2 ** ADAM_STEP)
    delta = -ADAM_LR * (m_hat / (jnp.sqrt(v_hat) + ADAM_EPS) + ADAM_WD * w)
    return delta, m, v


def _adamw_small(grads, ws, ms, vs):
    n = len(ws)

    def whole(a):
        return pl.BlockSpec(a.shape, functools.partial(lambda i, nd: (0,) * nd, nd=a.ndim))

    def body(*refs):
        for i in range(n):
            g, w, m, v = (refs[k * n + i][...] for k in range(4))
            for k, val in enumerate(_adamw(g, w, m, v)):
                refs[(4 + k) * n + i][...] = val

    outs = pl.pallas_call(
        body, name="adamw_small", grid=(1,), in_specs=[whole(a) for a in (*grads, *ws, *ms, *vs)],
        out_specs=[whole(w) for _ in range(3) for w in ws],
        out_shape=[jax.ShapeDtypeStruct(w.shape, F32) for _ in range(3) for w in ws],
        compiler_params=pltpu.CompilerParams(vmem_limit_bytes=VMEM_LIMIT),
    )(*grads, *ws, *ms, *vs)
    return outs[:n], outs[n:2 * n], outs[2 * n:]


def kernel(x, norm1_g, w_in, q_norm_g, k_norm_g, ssm_lambda_re, ssm_lambda_im, ssm_log_dt, ssm_b_re, ssm_b_im, ssm_c_re, ssm_c_im, ssm_d, w_glu, b_glu, attn_out_g, ssm_out_g, w_out, norm2_g, w_mlp_in, w_mlp_out, loss_target, m_norm1_g, m_w_in, m_q_norm_g, m_k_norm_g, m_ssm_lambda_re, m_ssm_lambda_im, m_ssm_log_dt, m_ssm_b_re, m_ssm_b_im, m_ssm_c_re, m_ssm_c_im, m_ssm_d, m_w_glu, m_b_glu, m_attn_out_g, m_ssm_out_g, m_w_out, m_norm2_g, m_w_mlp_in, m_w_mlp_out, v_norm1_g, v_w_in, v_q_norm_g, v_k_norm_g, v_ssm_lambda_re, v_ssm_lambda_im, v_ssm_log_dt, v_ssm_b_re, v_ssm_b_im, v_ssm_c_re, v_ssm_c_im, v_ssm_d, v_w_glu, v_b_glu, v_attn_out_g, v_ssm_out_g, v_w_out, v_norm2_g, v_w_mlp_in, v_w_mlp_out):
    batch, seq, d_model = x.shape
    tokens = batch * seq
    sb_width = w_in.shape[1]
    n_features = d_model

    big = [("w_in", w_in, m_w_in, v_w_in, 1), ("w_glu", w_glu, m_w_glu, v_w_glu, 0),
           ("w_out", w_out, m_w_out, v_w_out, 0), ("w_mlp_in", w_mlp_in, m_w_mlp_in, v_w_mlp_in, 1),
           ("w_mlp_out", w_mlp_out, m_w_mlp_out, v_w_mlp_out, 0)]
    small = [("norm1_g", norm1_g, m_norm1_g, v_norm1_g), ("q_norm_g", q_norm_g, m_q_norm_g, v_q_norm_g),
             ("k_norm_g", k_norm_g, m_k_norm_g, v_k_norm_g),
             ("ssm_lambda_re", ssm_lambda_re, m_ssm_lambda_re, v_ssm_lambda_re),
             ("ssm_lambda_im", ssm_lambda_im, m_ssm_lambda_im, v_ssm_lambda_im),
             ("ssm_log_dt", ssm_log_dt, m_ssm_log_dt, v_ssm_log_dt),
             ("ssm_b_re", ssm_b_re, m_ssm_b_re, v_ssm_b_re), ("ssm_b_im", ssm_b_im, m_ssm_b_im, v_ssm_b_im),
             ("ssm_c_re", ssm_c_re, m_ssm_c_re, v_ssm_c_re), ("ssm_c_im", ssm_c_im, m_ssm_c_im, v_ssm_c_im),
             ("ssm_d", ssm_d, m_ssm_d, v_ssm_d), ("b_glu", b_glu, m_b_glu, v_b_glu),
             ("attn_out_g", attn_out_g, m_attn_out_g, v_attn_out_g), ("ssm_out_g", ssm_out_g, m_ssm_out_g, v_ssm_out_g),
             ("norm2_g", norm2_g, m_norm2_g, v_norm2_g)]

    gather_in, tok_in = _chip_exchange_start("gather_w_in_start", "gather", [(w_in.astype(BF16), 1)])
    gather_rest, tok_rest = _chip_exchange_start(
        "gather_rest_start", "gather", [(w.astype(BF16), axis) for _, w, _, _, axis in big[1:]], after=tok_in)

    x2 = x.reshape(tokens, d_model)
    tgt2 = loss_target.reshape(tokens, d_model)
    g1, g2 = norm1_g[None, :], norm2_g[None, :]
    g_attn, g_ssm, bias_glu = attn_out_g[None, :], ssm_out_g[None, :], b_glu[None, :]
    heads = sb_width // HEAD_DIM
    qk_scale = 1.0 / math.sqrt(HEAD_DIM)
    gq, gk = (jnp.tile(q_norm_g, heads) * qk_scale)[None, :], jnp.tile(k_norm_g, heads)[None, :]
    lane_head = jnp.arange(LANES) // HEAD_DIM
    ones_blocks = (lane_head[:, None] == lane_head[None, :]).astype(F32)

    (xn,) = _rowwise("norm1", _rms, [x2], [g1 + tok_rest], [(d_model, BF16)], tm=512)
    s5_in = _s5_operator_inputs(ssm_lambda_re, ssm_lambda_im, ssm_log_dt, ssm_b_re, ssm_b_im, ssm_c_re, ssm_c_im, ssm_d)
    kt_row, b_mat, c_mat, la, lb = _s5_operators_call("s5_operators", s5_in)
    (wf_in,) = _chip_exchange_wait("gather_w_in_wait", gather_in, [xn, b_mat, c_mat])
    (proj,) = _mm("proj_in", xn, wf_in, "nn")

    def qkv_fn(q, k, v, gq_, gk_, ones):
        return _head_rms(q, gq_, ones), _head_rms(k, gk_, ones), v

    qn, kn, vb = _rowwise("qk_norm", qkv_fn, [(proj, sb_width, 0), (proj, sb_width, 1), (proj, sb_width, 2)],
                          [gq, gk, ones_blocks], [(sb_width, BF16)] * 3, tm=512)
    sb, c_tot = _attn_fwd(qn, kn, vb, batch=batch, seq=seq, bq=ATTN_BQ, bk=ATTN_BK)
    ug = _tokens_to_groups("u_to_groups", proj, 3, sb_width)
    yg, xin = _s5_fwd(ug, kt_row, b_mat, c_mat, la, lb, batch=batch)
    y_ssm = _groups_to_tokens("y_to_tokens", yg)

    wf_glu, wf_out, wf_mlp_in, wf_mlp_out = _chip_exchange_wait("gather_rest_wait", gather_rest, [y_ssm, sb])
    (gate_pre,) = _mm("glu_gate", y_ssm, wf_glu, "nn", a_fn=_gelu, extras=[(bias_glu, "row")],
                      epilogue=lambda acc, b: acc + b)
    (mixed,) = _rowwise("mix_norm", _mixed, [sb, y_ssm, gate_pre], [g_attn, g_ssm], [(2 * sb_width, BF16)], tm=512)
    def out_head(acc, r, g):
        h = acc + r
        return h, _rms(h, g)

    h1, hn = _mm("proj_out", mixed, wf_out, "nn", extras=[(x2, "tile"), (g2, "row")], epilogue=out_head,
                 out_dtypes=(F32, BF16), full_rows=True)
    def mlp_act(acc):
        r = jnp.maximum(acc, 0.0)
        return r * r, r

    act, act_root = _mm("mlp_in", hn, wf_mlp_in, "nn", epilogue=mlp_act, out_dtypes=(BF16, BF16))
    inv_n = 1.0 / n_features

    def loss_head(acc, r, t):
        d = ((acc + r) - t) * inv_n
        return d, d, jnp.sum(d * d, keepdims=True) * (0.5 * n_features)

    dy, dy_b, loss_tiles = _mm("mlp_out_loss", act, wf_mlp_out, "nn", extras=[(h1, "tile"), (tgt2, "tile")],
                               epilogue=loss_head, out_dtypes=(F32, BF16), tile_sums=("scalar",))
    loss_part = jnp.sum(loss_tiles)

    (dw_mlp_out,) = _mm("dw_mlp_out", act, dy_b, "tn")
    (dpre,) = _mm("d_mlp_act", dy_b, wf_mlp_out, "nt", extras=[(act_root, "tile")],
                  epilogue=lambda acc, r: acc * (2.0 * r.astype(F32)), out_dtypes=(BF16,))
    (dw_mlp_in,) = _mm("dw_mlp_in", hn, dpre, "tn")
    scatter_mlp, tok_mlp = _chip_exchange_start("scatter_mlp_start", "scatter", [(dw_mlp_in, 1), (dw_mlp_out, 0)])
    def norm_bwd(dn, res, hx, g):
        _, vjp = jax.vjp(_rms, hx, g)
        dh, dg = vjp(dn)
        return res + dh, dg

    dh1, dg_tiles = _mm("d_norm2_in", dpre, wf_mlp_in, "nt", extras=[(dy, "tile"), (h1, "tile"), (g2 + tok_mlp, "row")],
                        epilogue=norm_bwd, tile_sums=("row",), full_rows=True)
    dg_norm2 = jnp.sum(dg_tiles, axis=0, keepdims=True)
    (dmixed,) = _mm("d_mixed", dh1, wf_out, "nt")
    (dw_out,) = _mm("dw_out", mixed, dh1, "tn")

    def mixed_bwd(dm, sb_, ys, gp, ga, gs):
        _, vjp = jax.vjp(lambda a, act, b, c, d: jnp.concatenate(
            [_rms(a, c), _rms(act * jax.nn.sigmoid(b), d)], axis=-1), sb_, _gelu(ys), gp, ga, gs)
        dsb_, dact, dgp_, dga, dgs = vjp(dm)
        return dsb_, dgp_, dact, dga, dgs, jnp.sum(dgp_, axis=0, keepdims=True)

    dsb, dgate_pre, dact_part, dg_attn, dg_ssm, db_glu = _rowwise(
        "mix_norm_bwd", mixed_bwd, [dmixed, sb, y_ssm, gate_pre], [g_attn, g_ssm],
        [(sb_width, F32), (sb_width, BF16), (sb_width, F32)], [(1, sb_width)] * 3)

    def gelu_bwd(acc, part, ys):
        _, vjp = jax.vjp(_gelu, ys)
        return vjp(acc + part)[0]

    (dy_ssm,) = _mm("d_glu_in", dgate_pre, wf_glu, "nt", extras=[(dact_part, "tile"), (y_ssm, "tile")], epilogue=gelu_bwd)
    (dw_glu,) = _mm("dw_glu", y_ssm, dgate_pre, "tn", a_fn=_gelu)
    scatter_mix, tok_mix = _chip_exchange_start("scatter_mix_start", "scatter", [(dw_glu, 0), (dw_out, 0)])

    dug, dkt_row, db_mat, dc_mat, dla, dlb = _s5_bwd(ug, _tokens_to_groups("dy_to_groups", dy_ssm, 0, sb_width), xin,
                                                     kt_row, b_mat, c_mat, la, lb + tok_mix, batch=batch)
    du = _groups_to_tokens("du_to_tokens", dug)
    d_in = _s5_operators_call("s5_operators_bwd", s5_in, (dkt_row, db_mat, dc_mat, dla, dlb))
    ds5 = [d_in[0][:, 0, :] + d_in[2][:, :, 0], d_in[1][:, 0, :] + d_in[3][:, :, 0], d_in[4][:, 0, 0],
           d_in[5].transpose(0, 2, 1), d_in[6].transpose(0, 2, 1), d_in[7].transpose(0, 2, 1), d_in[8].transpose(0, 2, 1),
           d_in[9][:, 0, :]]

    def pack(parts):
        flat = jnp.concatenate([p.reshape(-1) for p in parts])
        rows = -(-flat.shape[0] // (8 * LANES)) * 8
        return jnp.pad(flat, (0, rows * LANES - flat.shape[0])).reshape(rows, LANES)

    def unpack(packed, names):
        flat, out, off = packed.reshape(-1), {}, 0
        for name in names:
            shape = small_shapes[name]
            size = math.prod(shape)
            out[name] = flat[off:off + size].reshape(shape)
            off += size
        return out, flat[off]

    small_shapes = {name: w.shape for name, w, _, _ in small}
    early_names = ["ssm_lambda_re", "ssm_lambda_im", "ssm_log_dt", "ssm_b_re", "ssm_b_im", "ssm_c_re", "ssm_c_im", "ssm_d",
                   "b_glu", "attn_out_g", "ssm_out_g", "norm2_g"]
    late_names = ["norm1_g", "q_norm_g", "k_norm_g"]
    early = pack([*ds5, db_glu[0], dg_attn[0], dg_ssm[0], dg_norm2[0], loss_part])
    early_exchange, _ = _chip_exchange_start("small_early_start", "all", [(early, 0)])

    dqn, dkn, dv = _attn_bwd(qn, kn, vb, c_tot, dsb, batch=batch, seq=seq, bq=ATTN_BQ, bk=ATTN_BK,
                             after=early_exchange[3][0])
    (early_slots,) = _chip_exchange_wait("small_early_wait", early_exchange, dqn)
    small_g, loss = unpack(_sum_slots("sum_small_early", early_slots), early_names)

    def qk_bwd(q, k, dq_, dk_, dv_, du_, gq_, gk_, ones):
        _, vjp_q = jax.vjp(lambda a, g: _head_rms(a, g, ones), q, gq_)
        _, vjp_k = jax.vjp(lambda a, g: _head_rms(a, g, ones), k, gk_)
        dq, dgq = vjp_q(dq_)
        dk, dgk = vjp_k(dk_)
        return jnp.concatenate([dq, dk, dv_, du_], axis=1), dgq, dgk

    dproj, dgq, dgk = _rowwise("qk_norm_bwd", qk_bwd, [(proj, sb_width, 0), (proj, sb_width, 1), dqn, dkn, dv, du],
                               [gq, gk, ones_blocks], [(4 * sb_width, BF16)], [(1, sb_width)] * 2, tm=512)
    (dw_in,) = _mm("dw_in", xn, dproj, "tn")
    scatter_in, tok_w_in = _chip_exchange_start("scatter_in_start", "scatter", [(dw_in, 1)])
    dx, dg_tiles = _mm("d_norm1_in", dproj, wf_in, "nt", extras=[(dh1, "tile"), (x2, "tile"), (g1 + tok_w_in, "row")],
                       epilogue=norm_bwd, tile_sums=("row",), full_rows=True)
    dg_norm1 = jnp.sum(dg_tiles, axis=0, keepdims=True)

    late = pack([dg_norm1[0], dgq.reshape(heads, HEAD_DIM).sum(0) * qk_scale, dgk.reshape(heads, HEAD_DIM).sum(0),
                 jnp.zeros((1,), F32)])
    late_exchange, _ = _chip_exchange_start("small_late_start", "all", [(late, 0)])

    def adam_big(sa, sb_, w, m, v):
        g = sa + sb_
        delta, m, v = _adamw(g, w, m, v)
        return g, delta, m, v

    def reduce_and_update(tag, params, slots):
        mine = [_sum_slots("sum_" + name, s) for s, (name, *_rest) in zip(slots, params)]
        theirs = _swap_with_sibling("swap_" + tag, mine)
        return {name: _rowwise("adamw_" + name, adam_big, [sa, sb_, w, m, v], [], [(w.shape[1], F32)] * 4)
                for (name, w, m, v, _), sa, sb_ in zip(params, mine, theirs)}

    started = late_exchange[3][0]
    slots_mlp_in, slots_mlp_out = _chip_exchange_wait("scatter_mlp_wait", scatter_mlp, started)
    slots_glu, slots_out = _chip_exchange_wait("scatter_mix_wait", scatter_mix, started)
    big_out = reduce_and_update("rest", big[1:], [slots_glu, slots_out, slots_mlp_in, slots_mlp_out])

    (late_slots,) = _chip_exchange_wait("small_late_wait", late_exchange, big_out["w_mlp_out"][3])
    reduced = _sum_slots("sum_small_late", late_slots)
    small_g.update(unpack(reduced, late_names)[0])
    narrow = {name for name, w, _, _ in small if w.ndim == 3 and w.shape[2] < w.shape[1]}

    def flip(a, name):
        return jnp.swapaxes(a, 1, 2) if name in narrow else a

    small_upd = _adamw_small([flip(small_g[name], name) for name, *_ in small], [flip(w, name) for name, w, _, _ in small],
                             [flip(m, name) for name, _, m, _ in small], [flip(v, name) for name, _, _, v in small])
    small_out = [small_g] + [{name: flip(small_upd[kind][i], name) for i, (name, *_) in enumerate(small)}
                             for kind in range(3)]

    (slots_in,) = _chip_exchange_wait("scatter_in_wait", scatter_in, reduced)
    big_out.update(reduce_and_update("w_in", big[:1], [slots_in]))
    names = ["norm1_g", "w_in", "q_norm_g", "k_norm_g", "ssm_lambda_re", "ssm_lambda_im", "ssm_log_dt", "ssm_b_re",
             "ssm_b_im", "ssm_c_re", "ssm_c_im", "ssm_d", "w_glu", "b_glu", "attn_out_g", "ssm_out_g", "w_out",
             "norm2_g", "w_mlp_in", "w_mlp_out"]
    outs = [loss, dx.reshape(batch, seq, d_model)]
    for kind in range(4):
        for name in names:
            outs.append(big_out[name][kind] if name in big_out else small_out[kind][name])
    return tuple(outs)
```

```python
import functools
import math

import jax
import jax.numpy as jnp
from jax import lax
from jax.experimental import pallas as pl
from jax.experimental.pallas import tpu as pltpu

F32 = jnp.float32
BF16 = jnp.bfloat16
F32_DOT = lax.Precision.HIGH
MESH = pl.DeviceIdType.MESH

RMS_EPS = 1e-6
HEAD_DIM = 64
SSM_GROUP = 16
SSM_CHUNK = 16
LANES = 128
N_CHIPS = 4
N_DEV = 8
VMEM_LIMIT = 48 * 1024 * 1024

ADAM_LR = 0.001
ADAM_B1 = 0.9
ADAM_B2 = 0.999
ADAM_EPS = 1e-08
ADAM_WD = 0.01
ADAM_STEP = 10


def _tile(n, pref):
    t = min(n, pref)
    while n % t:
        t //= 2
    return t


def _params(*sem):
    return pltpu.CompilerParams(dimension_semantics=sem, vmem_limit_bytes=VMEM_LIMIT)


_DIMS = {"nn": (((1,), (0,)), ((), ())), "nt": (((1,), (1,)), ((), ())), "tn": (((0,), (0,)), ((), ()))}


MM_VMEM_BUDGET = 40 * 1024 * 1024


def _mm_tiles(m, n, k, a_bytes, b_bytes, tile_bytes, full_rows=False):
    best = None
    for tk in [t for t in (k, k // 2, k // 4, k // 8) if t >= 256 or t == k]:
        for tm in [t for t in (1024, 512, 256, 128) if t <= m and m % t == 0]:
            for tn in [n] if full_rows else [t for t in (1024, 512, 256, 128) if t <= n and n % t == 0]:
                need = 2 * (tm * tk * a_bytes + tk * tn * b_bytes) + 2 * tm * tn * tile_bytes + (tm * tn * 4 if tk < k else 0)
                if need > MM_VMEM_BUDGET:
                    continue
                traffic = m * k * a_bytes * (1 if tk == k else n // tn) + k * n * b_bytes * (1 if n == tn and tk == k else m // tm)
                key = (tk < k, traffic, -tm * tn)
                if best is None or key < best[0]:
                    best = (key, (tm, tn, tk))
    return best[1]


def _mm(name, a, b, mode, *, a_fn=None, extras=(), epilogue=None, out_dtypes=(F32,), tile_sums=(), full_rows=False):
    if mode == "nn":
        (m, k), n = a.shape, b.shape[1]
    elif mode == "nt":
        (m, k), n = a.shape, b.shape[0]
    else:
        (k, m), n = a.shape, b.shape[1]
    tile_bytes = sum(e.dtype.itemsize for e, kind in extras if kind == "tile") + sum(jnp.dtype(d).itemsize for d in out_dtypes)
    tm, tn, tk = _mm_tiles(m, n, k, a.dtype.itemsize, b.dtype.itemsize, tile_bytes, full_rows)
    nk = k // tk
    ne, nout = len(extras), len(out_dtypes)
    dims = _DIMS[mode]

    def body(a_ref, b_ref, *rest):
        ex, outs, sums = rest[:ne], rest[ne:ne + nout], rest[ne + nout:ne + nout + len(tile_sums)]
        at = a_ref[...]
        if a_fn is not None:
            at = a_fn(at)
        p = lax.dot_general(at.astype(BF16), b_ref[...].astype(BF16), dims, preferred_element_type=F32)

        def finish(r):
            if epilogue is not None:
                r = epilogue(r, *[e[...] for e in ex])
            if not isinstance(r, (tuple, list)):
                r = (r,)
            for o, v in zip(outs, r[:nout]):
                o[...] = v.astype(o.dtype)
            for o, v, kind in zip(sums, r[nout:], tile_sums):
                first = lax.broadcasted_iota(jnp.int32, o.shape, 0) == 0
                if kind == "scalar":
                    first &= lax.broadcasted_iota(jnp.int32, o.shape, 1) == 0
                o[...] = jnp.where(first, v, 0.0)

        if nk == 1:
            finish(p)
        else:
            acc = rest[ne + nout + len(tile_sums)]
            kk = pl.program_id(2)

            @pl.when(kk == 0)
            def _():
                acc[...] = p

            @pl.when(kk > 0)
            def _():
                acc[...] += p

            @pl.when(kk == nk - 1)
            def _():
                finish(acc[...])

    if mode == "tn":
        a_spec = pl.BlockSpec((tk, tm), lambda i, j, kk: (kk, i))
    else:
        a_spec = pl.BlockSpec((tm, tk), lambda i, j, kk: (i, kk))
    if mode == "nt":
        b_spec = pl.BlockSpec((tn, tk), lambda i, j, kk: (j, kk))
    else:
        b_spec = pl.BlockSpec((tk, tn), lambda i, j, kk: (kk, j))
    ex_specs = []
    for _, kind in extras:
        if kind == "tile":
            ex_specs.append(pl.BlockSpec((tm, tn), lambda i, j, kk: (i, j)))
        else:
            ex_specs.append(pl.BlockSpec((1, tn), lambda i, j, kk: (0, j)))
    return pl.pallas_call(
        body, name=name, grid=(m // tm, n // tn, nk),
        in_specs=[a_spec, b_spec] + ex_specs,
        out_specs=([pl.BlockSpec((tm, tn), lambda i, j, kk: (i, j)) for _ in out_dtypes]
                   + [pl.BlockSpec((8, LANES if kind == "scalar" else tn), lambda i, j, kk: (i, j)) for kind in tile_sums]),
        out_shape=([jax.ShapeDtypeStruct((m, n), dt) for dt in out_dtypes]
                   + [jax.ShapeDtypeStruct((m // tm * 8, n // tn * LANES if kind == "scalar" else n), F32)
                      for kind in tile_sums]),
        scratch_shapes=[pltpu.VMEM((tm, tn), F32)] if nk > 1 else [],
        compiler_params=_params("parallel", "parallel", "arbitrary"),
    )(a, b, *[e for e, _ in extras])


def _rowwise(name, fn, rows, consts, row_outs, acc_outs=(), tm=256):
    norm = [r if isinstance(r, tuple) else (r, r.shape[1], 0) for r in rows]
    t = norm[0][0].shape[0]
    tm = _tile(t, tm)
    nr, nc, no = len(norm), len(consts), len(row_outs)

    def body(*refs):
        outs = fn(*[r[...] for r in refs[:nr + nc]])
        if not isinstance(outs, (tuple, list)):
            outs = (outs,)
        o_refs, a_refs = refs[nr + nc:nr + nc + no], refs[nr + nc + no:]
        for r, v in zip(o_refs, outs[:no]):
            r[...] = v.astype(r.dtype)
        if a_refs:
            i = pl.program_id(0)

            @pl.when(i == 0)
            def _():
                for r, v in zip(a_refs, outs[no:]):
                    r[...] = v

            @pl.when(i > 0)
            def _():
                for r, v in zip(a_refs, outs[no:]):
                    r[...] += v

    in_specs = [pl.BlockSpec((tm, w), functools.partial(lambda i, cb: (i, cb), cb=cb)) for _, w, cb in norm]
    in_specs += [pl.BlockSpec(c.shape, functools.partial(lambda i, nd: (0,) * nd, nd=c.ndim)) for c in consts]
    out_specs = [pl.BlockSpec((tm, w), lambda i: (i, 0)) for w, _ in row_outs]
    out_specs += [pl.BlockSpec(s, functools.partial(lambda i, nd: (0,) * nd, nd=len(s))) for s in acc_outs]
    out_shape = [jax.ShapeDtypeStruct((t, w), dt) for w, dt in row_outs]
    out_shape += [jax.ShapeDtypeStruct(s, F32) for s in acc_outs]
    return pl.pallas_call(
        body, name=name, grid=(t // tm,), in_specs=in_specs, out_specs=out_specs, out_shape=out_shape,
        compiler_params=_params("arbitrary"),
    )(*[r[0] for r in norm], *consts)


def _rms(x, g):
    return x * lax.rsqrt(jnp.mean(x * x, axis=-1, keepdims=True) + RMS_EPS) * g


@jax.custom_vjp
def _head_sums(x, ones_blocks):
    parts = [jnp.dot(x[:, j:j + LANES], ones_blocks, precision=F32_DOT, preferred_element_type=F32)
             for j in range(0, x.shape[1], LANES)]
    return jnp.concatenate(parts, axis=1)


_head_sums.defvjp(lambda x, ones_blocks: (_head_sums(x, ones_blocks), ones_blocks),
                  lambda ones_blocks, ct: (_head_sums(ct, ones_blocks), None))


def _head_rms(x, g, ones_blocks):
    return x * lax.rsqrt(_head_sums(x * x, ones_blocks) * (1.0 / HEAD_DIM) + RMS_EPS) * g


def _gelu(x):
    return x * (0.5 * (1.0 + jnp.tanh(math.sqrt(2.0 / math.pi) * (x + 0.044715 * (x * x * x)))))


def _mixed(sb, y_ssm, gate_pre, g_attn, g_ssm):
    ssm = _gelu(y_ssm) * jax.nn.sigmoid(gate_pre)
    return jnp.concatenate([_rms(sb, g_attn), _rms(ssm, g_ssm)], axis=-1)


def _softplus(z):
    return jnp.maximum(z, 0.0) + jnp.log(1.0 + jnp.exp(-jnp.abs(z)))


def _running_sums(x, tri):
    return jnp.dot(x.astype(BF16), tri, preferred_element_type=F32)


def _dot_nt(a, b, **kw):
    return lax.dot_general(a, b, _DIMS["nt"], preferred_element_type=F32, **kw)


def _dot_tn(a, b, **kw):
    return lax.dot_general(a, b, _DIMS["tn"], preferred_element_type=F32, **kw)


ATTN_BQ, ATTN_BK = 2048, 256
HEAD_LANES = tuple(slice(h * HEAD_DIM, (h + 1) * HEAD_DIM) for h in range(LANES // HEAD_DIM))


def _attn_fwd(qs, kn, v, *, batch, seq, bq, bk):
    width = qs.shape[1]
    bq = _tile(seq, bq)
    bk = _tile(bq, bk)
    nq, kpq = seq // bq, bq // bk

    def body(q_ref, k_ref, v_ref, o_ref, c_ref):
        row = lax.broadcasted_iota(jnp.int32, (bq, bk), 0)
        col = lax.broadcasted_iota(jnp.int32, (bq, bk), 1)
        tri = (lax.broadcasted_iota(jnp.int32, (bk, bk), 0) >= lax.broadcasted_iota(jnp.int32, (bk, bk), 1)).astype(BF16)

        def q_block(qi, carry):
            r0 = pl.multiple_of(qi * bq, bq)
            qh = [q_ref[pl.ds(r0, bq), ln] for ln in HEAD_LANES]

            def tile(k0, state, top=0):
                diag = top is not None
                top = top or 0
                msk = (col < row)[:bq - top] if diag else None
                new = []
                for h, ln in enumerate(HEAD_LANES):
                    o, c = state[2 * h], state[2 * h + 1]
                    z = _dot_nt(qh[h][top:], k_ref[pl.ds(k0, bk), ln])
                    sp = _softplus(z)
                    if diag:
                        sp = jnp.where(msk, sp, 0.0)
                    r = _running_sums(sp, tri)
                    a = jnp.exp(z - r - c[top:])
                    if diag:
                        a = jnp.where(msk, a, 0.0)
                    o_new = o[top:] + jnp.dot(a.astype(BF16), v_ref[pl.ds(k0, bk), ln], preferred_element_type=F32)
                    c_new = c[top:] + r[:, 0:1]
                    if top:
                        o_new, c_new = jnp.concatenate([o[:top], o_new]), jnp.concatenate([c[:top], c_new])
                    new += [o_new, c_new]
                return tuple(new)

            state = (jnp.zeros((bq, HEAD_DIM), F32), jnp.zeros((bq, 1), F32)) * len(HEAD_LANES)
            for d in reversed(range(kpq)):
                state = tile(pl.multiple_of(r0 + d * bk, bk), state, top=d * bk)
            state = lax.fori_loop(0, qi * kpq, lambda it, st: tile(pl.multiple_of(r0 - (it + 1) * bk, bk), st, None),
                                  state)
            for h, ln in enumerate(HEAD_LANES):
                o_ref[pl.ds(r0, bq), ln] = state[2 * h]
                c_ref[pl.ds(r0, bq), ln] = jnp.broadcast_to(state[2 * h + 1], (bq, HEAD_DIM))
            return carry

        lax.fori_loop(0, nq, q_block, 0)

    spec = pl.BlockSpec((seq, LANES), lambda b, h: (b, h))
    shape = jax.ShapeDtypeStruct((batch * seq, width), F32)
    return pl.pallas_call(
        body, name="attn_fwd", grid=(batch, width // LANES), in_specs=[spec, spec, spec], out_specs=[spec, spec],
        out_shape=[shape, shape], compiler_params=_params("parallel", "parallel"),
    )(qs, kn, v)


def _attn_bwd(qs, kn, v, c_tot, do, *, batch, seq, bq, bk, after):
    width = qs.shape[1]
    bq = _tile(seq, bq)
    bk = _tile(bq, bk)
    nq, kpq = seq // bq, bq // bk

    def body(q_ref, k_ref, v_ref, c_ref, do_ref, after_ref, dq_ref, dk_ref, dv_ref):
        row = lax.broadcasted_iota(jnp.int32, (bq, bk), 0)
        col = lax.broadcasted_iota(jnp.int32, (bq, bk), 1)
        sq_row = lax.broadcasted_iota(jnp.int32, (bk, bk), 0)
        sq_col = lax.broadcasted_iota(jnp.int32, (bk, bk), 1)
        tri = (sq_row >= sq_col).astype(BF16)
        tri_t = (sq_row <= sq_col).astype(BF16)
        dk_ref[...] = jnp.zeros_like(dk_ref)
        dv_ref[...] = jnp.zeros_like(dv_ref)

        def q_block(qi, carry):
            r0 = pl.multiple_of(qi * bq, bq)
            qh = [q_ref[pl.ds(r0, bq), ln] for ln in HEAD_LANES]
            d_out = [do_ref[pl.ds(r0, bq), ln].astype(BF16) for ln in HEAD_LANES]
            c_all = [c_ref[pl.ds(r0, bq), ln][:, 0:1] for ln in HEAD_LANES]

            def tile(k0, state, top=0):
                diag = top is not None
                top = top or 0
                last = diag and top == bq - bk
                msk = (col < row)[:bq - top] if diag else None
                new = []
                for h, ln in enumerate(HEAD_LANES):
                    c_left, g_left, dq = state[3 * h:3 * h + 3]
                    q, d_o = qh[h][top:], d_out[h][top:]
                    k = k_ref[pl.ds(k0, bk), ln]
                    z = _dot_nt(q, k)
                    e = jnp.exp(-jnp.abs(z))
                    sp = jnp.maximum(z, 0.0) + jnp.log(1.0 + e)
                    sig = jnp.exp(z - sp)
                    if diag:
                        sp = jnp.where(msk, sp, 0.0)
                    r = _running_sums(sp, tri)
                    c_new = c_left[top:] + r[:, 0:1]
                    a = jnp.exp(z - r - (0.0 if last else c_all[h][top:] - c_new))
                    if diag:
                        a = jnp.where(msk, a, 0.0)
                    g = a * _dot_nt(d_o, v_ref[pl.ds(k0, bk), ln])
                    pg = _running_sums(g, tri_t)
                    dz = g - sig * (g_left[top:] + pg)
                    if diag:
                        dz = jnp.where(msk, dz, 0.0)
                    dz = dz.astype(BF16)
                    dk_ref[pl.ds(k0, bk), ln] += _dot_tn(dz, q)
                    dv_ref[pl.ds(k0, bk), ln] += _dot_tn(a.astype(BF16), d_o)
                    g_new = g_left[top:] + pg[:, bk - 1:bk]
                    dq_new = dq[top:] + jnp.dot(dz, k, preferred_element_type=F32)
                    if top:
                        c_new = jnp.concatenate([c_left[:top], c_new])
                        g_new = jnp.concatenate([g_left[:top], g_new])
                        dq_new = jnp.concatenate([dq[:top], dq_new])
                    new += [c_new, g_new, dq_new]
                return tuple(new)

            zero = jnp.zeros((bq, 1), F32)
            init = (zero, zero, jnp.zeros((bq, HEAD_DIM), F32)) * len(HEAD_LANES)
            state = lax.fori_loop(0, qi * kpq, lambda it, st: tile(pl.multiple_of(it * bk, bk), st, None), init)
            for d in range(kpq):
                state = tile(pl.multiple_of(r0 + d * bk, bk), state, top=d * bk)
            for h, ln in enumerate(HEAD_LANES):
                dq_ref[pl.ds(r0, bq), ln] = state[3 * h + 2]
            return carry

        lax.fori_loop(0, nq, q_block, 0)

    spec = pl.BlockSpec((seq, LANES), lambda b, h: (b, h))
    shape = jax.ShapeDtypeStruct((batch * seq, width), F32)
    return pl.pallas_call(
        body, name="attn_bwd", grid=(batch, width // LANES),
        in_specs=[spec] * 5 + [pl.BlockSpec(memory_space=pl.ANY)], out_specs=[spec] * 3,
        out_shape=[shape] * 3, compiler_params=_params("parallel", "parallel"),
    )(qs, kn, v, c_tot, do, after)


def _pattern(rows, cols, hit):
    r, c = lax.broadcasted_iota(jnp.int32, (rows, cols), 0), lax.broadcasted_iota(jnp.int32, (rows, cols), 1)
    return hit(r, c).astype(F32)


def _s5_group_operators(lr_r, li_r, lr_c, li_c, log_dt, bt_re, bt_im, ct_re, ct_im, d_row):
    cs = SSM_CHUNK
    n_ch, n_state = bt_re.shape
    width = cs * n_ch
    dt = jnp.exp(log_dt)

    def spread(x, pattern):
        return jnp.dot(x, pattern, precision=F32_DOT, preferred_element_type=F32)

    twice = _pattern(n_state, 2 * n_state, lambda r, c: r == c % n_state)
    steps = lax.broadcasted_iota(jnp.int32, (cs + 1, 1), 0).astype(F32)
    mag = jnp.exp(steps * (lr_r * dt))
    ang = steps * (li_r * dt)
    pw_re, pw_im = mag * jnp.cos(ang), mag * jnp.sin(ang)
    num_re, num_im = pw_re[1:2] - 1.0, pw_im[1:2]
    den = lr_r * lr_r + li_r * li_r
    cf_re = (num_re * lr_r + num_im * li_r) / den
    cf_im = (num_im * lr_r - num_re * li_r) / den
    bb_re = spread(cf_re * bt_re - cf_im * bt_im, twice)
    bb_im = spread(cf_re * bt_im + cf_im * bt_re, twice)
    pw2_re, pw2_im = spread(pw_re, twice), spread(pw_im, twice)
    real_half = lax.broadcasted_iota(jnp.int32, (1, 2 * n_state), 1) < n_state
    blocks = []
    for s in range(cs):
        pr, pi = pw2_re[cs - 1 - s:cs - s], pw2_im[cs - 1 - s:cs - s]
        blocks.append(jnp.where(real_half, bb_re * pr - bb_im * pi, bb_re * pi + bb_im * pr))
    b_mat = jnp.concatenate(blocks, axis=0)
    la = pw2_re[cs:cs + 1]
    lb = jnp.where(real_half, -pw2_im[cs:cs + 1], pw2_im[cs:cs + 1])

    lane = lax.broadcasted_iota(jnp.int32, (1, width), 1)
    tile_out = _pattern(n_ch, width, lambda r, c: r == c % n_ch)
    c_re, c_im = spread(ct_re, tile_out), spread(ct_im, tile_out)

    k_row = lax.broadcasted_iota(jnp.int32, (1, cs), 1).astype(F32)
    m, a = jnp.exp(k_row * (lr_c * dt)), k_row * (li_c * dt)
    repeat = _pattern(cs, width, lambda r, c: r == c // n_ch)
    p_re, p_im = spread(m * jnp.cos(a), repeat), spread(m * jnp.sin(a), repeat)
    w_re, w_im = p_re * c_re - p_im * c_im, p_re * c_im + p_im * c_re
    skip = jnp.where((lane < n_ch) & (lane == lax.broadcasted_iota(jnp.int32, (n_ch, width), 0)),
                     spread(d_row, tile_out), 0.0)
    kt_row = (jnp.dot(bb_re[:, :n_state], w_re, precision=F32_DOT, preferred_element_type=F32)
              - jnp.dot(bb_im[:, :n_state], w_im, precision=F32_DOT, preferred_element_type=F32) + skip)
    bar_re, bar_im = jnp.exp(lr_c * dt) * jnp.cos(li_c * dt), jnp.exp(lr_c * dt) * jnp.sin(li_c * dt)
    w1_re, w1_im = w_re * bar_re - w_im * bar_im, w_re * bar_im + w_im * bar_re
    c_mat = jnp.concatenate([w1_re, -w1_im], axis=0)
    return kt_row, b_mat, c_mat, la, lb


def _s5_operator_inputs(lam_re, lam_im, log_dt, b_re, b_im, c_re, c_im, d_skip):
    return (lam_re[:, None, :], lam_im[:, None, :], lam_re[:, :, None], lam_im[:, :, None], log_dt[:, None, None],
            b_re.transpose(0, 2, 1), b_im.transpose(0, 2, 1), c_re.transpose(0, 2, 1), c_im.transpose(0, 2, 1),
            d_skip[:, None, :])


def _s5_operators_call(name, args, cotangents=None, gb=8):
    groups = args[0].shape[0]
    gb = _tile(groups, gb)
    n_in = len(args)

    def body(*refs):
        n_ct = 0 if cotangents is None else len(cotangents)
        ins, cts, outs = refs[:n_in], refs[n_in:n_in + n_ct], refs[n_in + n_ct:]
        for g in range(gb):
            vals = [r[g] for r in ins]
            if cotangents is None:
                res = _s5_group_operators(*vals)
            else:
                res = jax.vjp(_s5_group_operators, *vals)[1](tuple(c[g] for c in cts))
            for o, v in zip(outs, res):
                o[g] = v

    def spec(a):
        return pl.BlockSpec((gb, *a.shape[1:]), lambda i: (i, 0, 0))

    if cotangents is None:
        n_ch, n_state = args[5].shape[1:]
        width = SSM_CHUNK * n_ch
        out_shape = [jax.ShapeDtypeStruct((groups, *s), F32) for s in
                     ((n_ch, width), (width, 2 * n_state), (2 * n_state, width), (1, 2 * n_state), (1, 2 * n_state))]
    else:
        out_shape = [jax.ShapeDtypeStruct(a.shape, F32) for a in args]
    operands = [*args, *(cotangents or ())]
    return pl.pallas_call(
        body, name=name, grid=(groups // gb,), in_specs=[spec(a) for a in operands], out_specs=[spec(s) for s in out_shape],
        out_shape=out_shape, compiler_params=_params("parallel"),
    )(*operands)


GROUPS_PER_BLOCK = LANES // SSM_GROUP


def _tokens_to_groups(name, u, col_block, width):
    t = u.shape[0]
    n = t // SSM_CHUNK
    ch = SSM_CHUNK * SSM_GROUP
    blocks = width // LANES
    nb = GROUPS_PER_BLOCK

    def body(u_ref, o_ref):
        block = lax.broadcasted_iota(jnp.int32, (n, LANES), 1) // SSM_GROUP
        for half in range(SSM_CHUNK // nb):
            rows = [u_ref[pl.ds(half * nb + s, n, stride=SSM_CHUNK), :] for s in range(nb)]
            for shift in range(nb):
                merged = rows[shift]
                for b in range(1, nb):
                    merged = jnp.where(block == b, rows[(b + shift) % nb], merged)
                moved = pltpu.roll(merged, shift * SSM_GROUP, 1) if shift else merged
                for b in range(nb):
                    s = (b + shift) % nb
                    o_ref[b, :, half * LANES + s * SSM_GROUP:half * LANES + (s + 1) * SSM_GROUP] = (
                        moved[:, s * SSM_GROUP:(s + 1) * SSM_GROUP])

    return pl.pallas_call(
        body, name=name, grid=(blocks,),
        in_specs=[pl.BlockSpec((t, LANES), lambda j: (0, col_block * blocks + j))],
        out_specs=pl.BlockSpec((GROUPS_PER_BLOCK, n, ch), lambda j: (j, 0, 0)),
        out_shape=jax.ShapeDtypeStruct((width // SSM_GROUP, n, ch), F32), compiler_params=_params("parallel"),
    )(u)


def _groups_to_tokens(name, ug):
    groups, n, ch = ug.shape
    nb = GROUPS_PER_BLOCK

    def body(g_ref, o_ref, rows_ref):
        block = lax.broadcasted_iota(jnp.int32, (n, LANES), 1) // SSM_GROUP
        for half in range(SSM_CHUNK // nb):
            src = [g_ref[b, :, half * LANES:(half + 1) * LANES] for b in range(nb)]
            for shift in range(nb):
                merged = src[-shift % nb]
                for s in range(1, nb):
                    merged = jnp.where(block == s, src[(s - shift) % nb], merged)
                moved = pltpu.roll(merged, (nb - shift) * SSM_GROUP, 1) if shift else merged
                for b in range(nb):
                    rows_ref[(b + shift) % nb, :, b * SSM_GROUP:(b + 1) * SSM_GROUP] = moved[:, b * SSM_GROUP:(b + 1) * SSM_GROUP]
            for s in range(nb):
                o_ref[pl.ds(half * nb + s, n, stride=SSM_CHUNK), :] = rows_ref[s]

    return pl.pallas_call(
        body, name=name, grid=(groups // GROUPS_PER_BLOCK,),
        in_specs=[pl.BlockSpec((GROUPS_PER_BLOCK, n, ch), lambda j: (j, 0, 0))],
        out_specs=pl.BlockSpec((n * SSM_CHUNK, LANES), lambda j: (0, j)),
        out_shape=jax.ShapeDtypeStruct((n * SSM_CHUNK, groups * SSM_GROUP), F32),
        scratch_shapes=[pltpu.VMEM((nb, n, LANES), F32)], compiler_params=_params("parallel"),
    )(ug)


SCAN_ROWS = 8


def _toeplitz_to(tm_ref, g, kt_row):
    width = kt_row.shape[1]
    tm_ref[g] = jnp.zeros((width, width), F32)
    for s in range(SSM_CHUNK):
        tm_ref[g, s * SSM_GROUP:(s + 1) * SSM_GROUP, s * SSM_GROUP:] = kt_row[:, :width - s * SSM_GROUP]


def _lam_powers(la, lb, reverse):
    if reverse:
        lb = -lb

    def mul(p, q):
        return p[0] * q[0] - p[1] * q[1], p[0] * q[1] + p[1] * q[0]

    p1 = (la, lb)
    p2 = mul(p1, p1)
    p3 = mul(p2, p1)
    p4 = mul(p2, p2)
    rows = [p1, p2, p3, p4, mul(p4, p1), mul(p4, p2), mul(p4, p3), mul(p4, p4)]
    if reverse:
        rows = rows[::-1]
    idx = lax.broadcasted_iota(jnp.int32, (SCAN_ROWS, la.shape[1]), 0)
    tab_a = sum(jnp.where(idx == j, r[0], 0.0) for j, r in enumerate(rows))
    tab_b = sum(jnp.where(idx == j, r[1], 0.0) for j, r in enumerate(rows))
    return (p1, p2, p4), (tab_a, tab_b), idx


def _scan_block(e, carry, steps, table, idx, half, reverse):
    n = SCAN_ROWS
    for d, (pa, pb) in zip((1, 2, 4), steps):
        sh = pltpu.roll(e, n - d if reverse else d, 0)
        sh = jnp.where(idx < n - d if reverse else idx >= d, sh, 0.0)
        e = e + pa * sh + pb * pltpu.roll(sh, half, 1)
    tab_a, tab_b = table
    e = e + tab_a * carry + tab_b * pltpu.roll(carry, half, 1)
    shifted = jnp.where(idx == (n - 1 if reverse else 0), carry, pltpu.roll(e, n - 1 if reverse else 1, 0))
    edge = e[0:1] if reverse else e[n - 1:n]
    return shifted, jnp.broadcast_to(edge, e.shape)


def _s5_fwd(ug, kt_row, b_mat, c_mat, la, lb, *, batch, gb=8):
    groups, n, ch = ug.shape
    p2 = b_mat.shape[2]
    gb = _tile(groups, gb)
    nch = n // batch
    nblk = nch // SCAN_ROWS

    def body(u_ref, k_ref, b_ref, c_ref, la_ref, lb_ref, y_ref, x_ref, s_ref, tm_ref):
        for g in range(gb):
            _toeplitz_to(tm_ref, g, k_ref[g])
            s_ref[g] = jnp.dot(u_ref[g], b_ref[g], precision=F32_DOT, preferred_element_type=F32)
        powers = [_lam_powers(la_ref[g], lb_ref[g], False) for g in range(gb)]

        def step(blk, carries):
            new = []
            for g in range(gb):
                steps, table, idx = powers[g]
                for b in range(batch):
                    rows = pl.ds(pl.multiple_of(b * nch + blk * SCAN_ROWS, SCAN_ROWS), SCAN_ROWS)
                    x_in, carry = _scan_block(s_ref[g, rows, :], carries[g * batch + b], steps, table, idx, p2 // 2, False)
                    x_ref[g, rows, :] = x_in
                    new.append(carry)
            return tuple(new)

        lax.fori_loop(0, nblk, step, tuple(jnp.zeros((SCAN_ROWS, p2), F32) for _ in range(gb * batch)))
        for g in range(gb):
            y_ref[g] = (jnp.dot(u_ref[g], tm_ref[g], precision=F32_DOT, preferred_element_type=F32)
                        + jnp.dot(x_ref[g], c_ref[g], precision=F32_DOT, preferred_element_type=F32))

    def spec(a, b):
        return pl.BlockSpec((gb, a, b), lambda i: (i, 0, 0))

    return pl.pallas_call(
        body, name="s5_fwd", grid=(groups // gb,),
        in_specs=[spec(n, ch), spec(SSM_GROUP, ch), spec(ch, p2), spec(p2, ch), spec(1, p2), spec(1, p2)],
        out_specs=[spec(n, ch), spec(n, p2)],
        out_shape=[jax.ShapeDtypeStruct((groups, n, ch), F32), jax.ShapeDtypeStruct((groups, n, p2), F32)],
        scratch_shapes=[pltpu.VMEM((gb, n, p2), F32), pltpu.VMEM((gb, ch, ch), F32)],
        compiler_params=_params("parallel"),
    )(ug, kt_row, b_mat, c_mat, la, lb)


def _s5_bwd(ug, dyg, xin, kt_row, b_mat, c_mat, la, lb, *, batch, gb=8):
    groups, n, ch = ug.shape
    p2 = b_mat.shape[2]
    gb = _tile(groups, gb)
    nch = n // batch
    nblk = nch // SCAN_ROWS

    def body(u_ref, dy_ref, x_ref, k_ref, b_ref, c_ref, la_ref, lb_ref,
             du_ref, dk_ref, db_ref, dc_ref, dla_ref, dlb_ref, dx_ref, ds_ref, tm_ref):
        for g in range(gb):
            _toeplitz_to(tm_ref, g, k_ref[g])
            dx_ref[g] = _dot_nt(dy_ref[g], c_ref[g], precision=F32_DOT)
        powers = [_lam_powers(la_ref[g], lb_ref[g], True) for g in range(gb)]

        def step(it, carries):
            new = []
            for g in range(gb):
                steps, table, idx = powers[g]
                for b in range(batch):
                    rows = pl.ds(pl.multiple_of(b * nch + (nblk - 1 - it) * SCAN_ROWS, SCAN_ROWS), SCAN_ROWS)
                    d_s, carry = _scan_block(dx_ref[g, rows, :], carries[g * batch + b], steps, table, idx, p2 // 2, True)
                    ds_ref[g, rows, :] = d_s
                    new.append(carry)
            return tuple(new)

        lax.fori_loop(0, nblk, step, tuple(jnp.zeros((SCAN_ROWS, p2), F32) for _ in range(gb * batch)))
        for g in range(gb):
            u, dy, ds, x = u_ref[g], dy_ref[g], ds_ref[g], x_ref[g]
            du_ref[g] = _dot_nt(dy, tm_ref[g], precision=F32_DOT) + _dot_nt(ds, b_ref[g], precision=F32_DOT)
            tm_ref[g] = _dot_tn(u, dy, precision=F32_DOT)
            dk_ref[g] = tm_ref[g, 0:SSM_GROUP, :]
            for s in range(1, SSM_CHUNK):
                dk_ref[g, :, :ch - s * SSM_GROUP] += tm_ref[g, s * SSM_GROUP:(s + 1) * SSM_GROUP, s * SSM_GROUP:]
            db_ref[g] = _dot_tn(u, ds, precision=F32_DOT)
            dc_ref[g] = _dot_tn(x, dy, precision=F32_DOT)
            dla_ref[g] = jnp.sum(ds * x, axis=0, keepdims=True)
            dlb_ref[g] = jnp.sum(ds * pltpu.roll(x, p2 // 2, 1), axis=0, keepdims=True)

    def spec(a, b):
        return pl.BlockSpec((gb, a, b), lambda i: (i, 0, 0))

    def shape(a, b):
        return jax.ShapeDtypeStruct((groups, a, b), F32)

    return pl.pallas_call(
        body, name="s5_bwd", grid=(groups // gb,),
        in_specs=[spec(n, ch), spec(n, ch), spec(n, p2), spec(SSM_GROUP, ch), spec(ch, p2), spec(p2, ch), spec(1, p2),
                  spec(1, p2)],
        out_specs=[spec(n, ch), spec(SSM_GROUP, ch), spec(ch, p2), spec(p2, ch), spec(1, p2), spec(1, p2)],
        out_shape=[shape(n, ch), shape(SSM_GROUP, ch), shape(ch, p2), shape(p2, ch), shape(1, p2), shape(1, p2)],
        scratch_shapes=[pltpu.VMEM((gb, n, p2), F32), pltpu.VMEM((gb, n, p2), F32), pltpu.VMEM((gb, ch, ch), F32)],
        compiler_params=_params("parallel"),
    )(ug, dyg, xin, kt_row, b_mat, c_mat, la, lb)


def _block(ref, axis, j, size):
    start = j * size if isinstance(j, int) else pl.multiple_of(j * size, size)
    return ref.at[pl.ds(start, size), :] if axis == 0 else ref.at[:, pl.ds(start, size)]


def _chip_exchange_copies(mode, axes, srcs, lands, send_sems, recv_sems, local_sems):
    x, y, c = lax.axis_index("x"), lax.axis_index("y"), lax.axis_index("c")
    everyone = mode == "all"
    me = 4 * x + 2 * y + c if everyone else 2 * x + y
    n_peers = _exchange_peers(mode)
    local, sends, arrivals = [], [], []
    for w, axis in enumerate(axes):
        if mode == "gather":
            size = srcs[w].shape[axis]
            local.append(pltpu.make_async_copy(srcs[w], _block(lands[w], axis, me, size), local_sems.at[w]))
        elif mode == "scatter":
            size = srcs[w].shape[axis] // N_CHIPS
            local.append(pltpu.make_async_copy(_block(srcs[w], axis, me, size), lands[w].at[me], local_sems.at[w]))
        else:
            local.append(pltpu.make_async_copy(srcs[w], lands[w].at[me], local_sems.at[w]))
        for k in range(1, n_peers + 1):
            bits = k if everyone else 2 * k
            px = 1 - x if bits & 4 else x
            py = 1 - y if bits & 2 else y
            pc = 1 - c if bits & 1 else c
            peer = 4 * px + 2 * py + pc if everyone else 2 * px + py
            if mode == "gather":
                src, dst, arrive = srcs[w], _block(lands[w], axis, me, size), _block(lands[w], axis, peer, size)
            elif mode == "scatter":
                src, dst, arrive = _block(srcs[w], axis, peer, size), lands[w].at[me], lands[w].at[peer]
            else:
                src, dst, arrive = srcs[w], lands[w].at[me], lands[w].at[peer]
            sem = w * n_peers + k - 1
            for target, out in ((dst, sends), (arrive, arrivals)):
                out.append(pltpu.make_async_remote_copy(
                    src_ref=src, dst_ref=target, send_sem=send_sems.at[sem], recv_sem=recv_sems.at[sem],
                    device_id=(px, py, pc), device_id_type=MESH))
    return local, sends, arrivals


def _exchange_peers(mode):
    return N_DEV - 1 if mode == "all" else N_CHIPS - 1


def _chip_exchange_start(name, mode, items, after=None):
    n = len(items)
    n_after = 0 if after is None else 1
    axes = [axis for _, axis in items]
    hbm = pl.BlockSpec(memory_space=pltpu.HBM)
    sem = pl.BlockSpec(memory_space=pltpu.SEMAPHORE)
    lands = []
    for a, axis in items:
        shape = list(a.shape)
        if mode == "gather":
            shape[axis] *= N_CHIPS
        elif mode == "scatter":
            shape[axis] //= N_CHIPS
            shape = [N_CHIPS] + shape
        else:
            shape = [N_DEV] + shape
        lands.append(pltpu.with_memory_space_constraint(lax.empty(tuple(shape), a.dtype), pltpu.HBM))

    def body(*refs):
        srcs, land_refs = refs[:n], refs[n:2 * n]
        send_sems, recv_sems, local_sems = refs[2 * n + n_after:2 * n + n_after + 3]
        token = refs[-1]
        local, sends, _ = _chip_exchange_copies(mode, axes, srcs, land_refs, send_sems, recv_sems, local_sems)
        for cp in local + sends:
            cp.start()
        token[...] = jnp.zeros_like(token)

    n_sem = n * _exchange_peers(mode)
    outs = pl.pallas_call(
        body, name=name,
        out_shape=(pltpu.SemaphoreType.DMA((n_sem,)), pltpu.SemaphoreType.DMA((n_sem,)), pltpu.SemaphoreType.DMA((n,)),
                   *[pltpu.HBM(a.shape, a.dtype) for a, _ in items], *[pltpu.HBM(l.shape, l.dtype) for l in lands],
                   jax.ShapeDtypeStruct((8, LANES), F32)),
        in_specs=[hbm] * (2 * n) + [pl.BlockSpec(memory_space=pl.ANY)] * n_after,
        out_specs=(sem, sem, sem, *[hbm] * (2 * n), pl.BlockSpec(memory_space=pltpu.VMEM)),
        input_output_aliases={i: 3 + i for i in range(2 * n)},
        compiler_params=pltpu.CompilerParams(has_side_effects=pltpu.SideEffectType.DATAFLOW_SIDE_EFFECTING),
    )(*[pltpu.with_memory_space_constraint(a, pltpu.HBM) for a, _ in items], *lands, *([after] if n_after else []))
    return (mode, axes, outs[:3], outs[3:3 + n], outs[3 + n:3 + 2 * n]), outs[-1][0:1, 0:1]


def _chip_exchange_wait(name, handle, after):
    mode, axes, sems, srcs, lands = handle
    n = len(axes)
    after = list(after) if isinstance(after, (tuple, list)) else [after]
    hbm = pl.BlockSpec(memory_space=pltpu.HBM)
    sem = pl.BlockSpec(memory_space=pltpu.SEMAPHORE)

    def body(*refs):
        src_refs, land_refs = refs[:n], refs[n:2 * n]
        send_sems, recv_sems, local_sems = refs[2 * n:2 * n + 3]
        local, sends, arrivals = _chip_exchange_copies(mode, axes, src_refs, land_refs, send_sems, recv_sems, local_sems)
        for cp in sends:
            cp.wait_send()
        for cp in arrivals:
            cp.wait_recv()
        for cp in local:
            cp.wait()

    outs = pl.pallas_call(
        body, name=name,
        out_shape=(*[pltpu.HBM(a.shape, a.dtype) for a in srcs], *[pltpu.HBM(l.shape, l.dtype) for l in lands]),
        in_specs=[hbm] * (2 * n) + [sem] * 3 + [pl.BlockSpec(memory_space=pl.ANY)] * len(after), out_specs=[hbm] * (2 * n),
        input_output_aliases={i: i for i in range(2 * n)},
        compiler_params=pltpu.CompilerParams(has_side_effects=pltpu.SideEffectType.DATAFLOW_SIDE_EFFECTING),
    )(*srcs, *lands, *sems, *after)
    return outs[n:]


def _sum_slots(name, slots, tm=256):
    n_slots, r, c = slots.shape
    tm = _tile(r, tm)

    def body(*refs):
        acc = refs[0][...]
        for s_ref in refs[1:n_slots]:
            acc = acc + s_ref[...]
        refs[n_slots][...] = acc

    specs = [pl.BlockSpec((None, tm, c), functools.partial(lambda i, s: (s, i, 0), s=s)) for s in range(n_slots)]
    return pl.pallas_call(
        body, name=name, grid=(r // tm,), in_specs=specs, out_specs=pl.BlockSpec((tm, c), lambda i: (i, 0)),
        out_shape=jax.ShapeDtypeStruct((r, c), F32), compiler_params=_params("parallel"),
    )(*[slots] * n_slots)


def _swap_with_sibling(name, arrays):
    n = len(arrays)
    hbm = pl.BlockSpec(memory_space=pl.ANY)

    def body(*refs):
        ins, outs = refs[:n], refs[n:2 * n]
        send_sems, recv_sems = refs[2 * n:]
        sibling = (lax.axis_index("x"), lax.axis_index("y"), 1 - lax.axis_index("c"))
        copies = [pltpu.make_async_remote_copy(src_ref=ins[w], dst_ref=outs[w], send_sem=send_sems.at[w],
                                               recv_sem=recv_sems.at[w], device_id=sibling, device_id_type=MESH)
                  for w in range(n)]
        for cp in copies:
            cp.start()
        for cp in copies:
            cp.wait()

    return pl.pallas_call(
        body, name=name, in_specs=[hbm] * n, out_specs=[hbm] * n,
        out_shape=[jax.ShapeDtypeStruct(a.shape, a.dtype) for a in arrays],
        scratch_shapes=[pltpu.SemaphoreType.DMA((n,)), pltpu.SemaphoreType.DMA((n,))],
    )(*arrays)


def _adamw(g, w, m, v):
    m = ADAM_B1 * m + (1.0 - ADAM_B1) * g
    v = ADAM_B2 * v + (1.0 - ADAM_B2) * jnp.square(g)
    m_hat = m / (1.0 - ADAM_B1 ** ADAM_STEP)
    v_hat = v / (1.0 - ADAM_B2 ** ADAM_STEP)
    delta = -ADAM_LR * (m_hat / (jnp.sqrt(v_hat) + ADAM_EPS) + ADAM_WD * w)
    return delta, m, v


def _adamw_small(grads, ws, ms, vs):
    n = len(ws)

    def whole(a):
        return pl.BlockSpec(a.shape, functools.partial(lambda i, nd: (0,) * nd, nd=a.ndim))

    def body(*refs):
        for i in range(n):
            g, w, m, v = (refs[k * n + i][...] for k in range(4))
            for k, val in enumerate(_adamw(g, w, m, v)):
                refs[(4 + k) * n + i][...] = val

    outs = pl.pallas_call(
        body, name="adamw_small", grid=(1,), in_specs=[whole(a) for a in (*grads, *ws, *ms, *vs)],
        out_specs=[whole(w) for _ in range(3) for w in ws],
        out_shape=[jax.ShapeDtypeStruct(w.shape, F32) for _ in range(3) for w in ws],
        compiler_params=pltpu.CompilerParams(vmem_limit_bytes=VMEM_LIMIT),
    )(*grads, *ws, *ms, *vs)
    return outs[:n], outs[n:2 * n], outs[2 * n:]


def kernel(x, norm1_g, w_in, q_norm_g, k_norm_g, ssm_lambda_re, ssm_lambda_im, ssm_log_dt, ssm_b_re, ssm_b_im, ssm_c_re, ssm_c_im, ssm_d, w_glu, b_glu, attn_out_g, ssm_out_g, w_out, norm2_g, w_mlp_in, w_mlp_out, loss_target, m_norm1_g, m_w_in, m_q_norm_g, m_k_norm_g, m_ssm_lambda_re, m_ssm_lambda_im, m_ssm_log_dt, m_ssm_b_re, m_ssm_b_im, m_ssm_c_re, m_ssm_c_im, m_ssm_d, m_w_glu, m_b_glu, m_attn_out_g, m_ssm_out_g, m_w_out, m_norm2_g, m_w_mlp_in, m_w_mlp_out, v_norm1_g, v_w_in, v_q_norm_g, v_k_norm_g, v_ssm_lambda_re, v_ssm_lambda_im, v_ssm_log_dt, v_ssm_b_re, v_ssm_b_im, v_ssm_c_re, v_ssm_c_im, v_ssm_d, v_w_glu, v_b_glu, v_attn_out_g, v_ssm_out_g, v_w_out, v_norm2_g, v_w_mlp_in, v_w_mlp_out):
    batch, seq, d_model = x.shape
    tokens = batch * seq
    sb_width = w_in.shape[1]
    n_features = d_model

    big = [("w_in", w_in, m_w_in, v_w_in, 1), ("w_glu", w_glu, m_w_glu, v_w_glu, 0),
           ("w_out", w_out, m_w_out, v_w_out, 0), ("w_mlp_in", w_mlp_in, m_w_mlp_in, v_w_mlp_in, 1),
           ("w_mlp_out", w_mlp_out, m_w_mlp_out, v_w_mlp_out, 0)]
    small = [("norm1_g", norm1_g, m_norm1_g, v_norm1_g), ("q_norm_g", q_norm_g, m_q_norm_g, v_q_norm_g),
             ("k_norm_g", k_norm_g, m_k_norm_g, v_k_norm_g),
             ("ssm_lambda_re", ssm_lambda_re, m_ssm_lambda_re, v_ssm_lambda_re),
             ("ssm_lambda_im", ssm_lambda_im, m_ssm_lambda_im, v_ssm_lambda_im),
             ("ssm_log_dt", ssm_log_dt, m_ssm_log_dt, v_ssm_log_dt),
             ("ssm_b_re", ssm_b_re, m_ssm_b_re, v_ssm_b_re), ("ssm_b_im", ssm_b_im, m_ssm_b_im, v_ssm_b_im),
             ("ssm_c_re", ssm_c_re, m_ssm_c_re, v_ssm_c_re), ("ssm_c_im", ssm_c_im, m_ssm_c_im, v_ssm_c_im),
             ("ssm_d", ssm_d, m_ssm_d, v_ssm_d), ("b_glu", b_glu, m_b_glu, v_b_glu),
             ("attn_out_g", attn_out_g, m_attn_out_g, v_attn_out_g), ("ssm_out_g", ssm_out_g, m_ssm_out_g, v_ssm_out_g),
             ("norm2_g", norm2_g, m_norm2_g, v_norm2_g)]

    gather_in, tok_in = _chip_exchange_start("gather_w_in_start", "gather", [(w_in.astype(BF16), 1)])
    gather_rest, tok_rest = _chip_exchange_start(
        "gather_rest_start", "gather", [(w.astype(BF16), axis) for _, w, _, _, axis in big[1:]], after=tok_in)

    x2 = x.reshape(tokens, d_model)
    tgt2 = loss_target.reshape(tokens, d_model)
    g1, g2 = norm1_g[None, :], norm2_g[None, :]
    g_attn, g_ssm, bias_glu = attn_out_g[None, :], ssm_out_g[None, :], b_glu[None, :]
    heads = sb_width // HEAD_DIM
    qk_scale = 1.0 / math.sqrt(HEAD_DIM)
    gq, gk = (jnp.tile(q_norm_g, heads) * qk_scale)[None, :], jnp.tile(k_norm_g, heads)[None, :]
    lane_head = jnp.arange(LANES) // HEAD_DIM
    ones_blocks = (lane_head[:, None] == lane_head[None, :]).astype(F32)

    (xn,) = _rowwise("norm1", _rms, [x2], [g1 + tok_rest], [(d_model, BF16)], tm=512)
    s5_in = _s5_operator_inputs(ssm_lambda_re, ssm_lambda_im, ssm_log_dt, ssm_b_re, ssm_b_im, ssm_c_re, ssm_c_im, ssm_d)
    kt_row, b_mat, c_mat, la, lb = _s5_operators_call("s5_operators", s5_in)
    (wf_in,) = _chip_exchange_wait("gather_w_in_wait", gather_in, [xn, b_mat, c_mat])
    (proj,) = _mm("proj_in", xn, wf_in, "nn")

    def qkv_fn(q, k, v, gq_, gk_, ones):
        return _head_rms(q, gq_, ones), _head_rms(k, gk_, ones), v

    qn, kn, vb = _rowwise("qk_norm", qkv_fn, [(proj, sb_width, 0), (proj, sb_width, 1), (proj, sb_width, 2)],
                          [gq, gk, ones_blocks], [(sb_width, BF16)] * 3, tm=512)
    sb, c_tot = _attn_fwd(qn, kn, vb, batch=batch, seq=seq, bq=ATTN_BQ, bk=ATTN_BK)
    ug = _tokens_to_groups("u_to_groups", proj, 3, sb_width)
    yg, xin = _s5_fwd(ug, kt_row, b_mat, c_mat, la, lb, batch=batch)
    y_ssm = _groups_to_tokens("y_to_tokens", yg)

    wf_glu, wf_out, wf_mlp_in, wf_mlp_out = _chip_exchange_wait("gather_rest_wait", gather_rest, [y_ssm, sb])
    (gate_pre,) = _mm("glu_gate", y_ssm, wf_glu, "nn", a_fn=_gelu, extras=[(bias_glu, "row")],
                      epilogue=lambda acc, b: acc + b)
    (mixed,) = _rowwise("mix_norm", _mixed, [sb, y_ssm, gate_pre], [g_attn, g_ssm], [(2 * sb_width, BF16)], tm=512)
    def out_head(acc, r, g):
        h = acc + r
        return h, _rms(h, g)

    h1, hn = _mm("proj_out", mixed, wf_out, "nn", extras=[(x2, "tile"), (g2, "row")], epilogue=out_head,
                 out_dtypes=(F32, BF16), full_rows=True)
    def mlp_act(acc):
        r = jnp.maximum(acc, 0.0)
        return r * r, r

    act, act_root = _mm("mlp_in", hn, wf_mlp_in, "nn", epilogue=mlp_act, out_dtypes=(BF16, BF16))
    inv_n = 1.0 / n_features

    def loss_head(acc, r, t):
        d = ((acc + r) - t) * inv_n
        return d, d, jnp.sum(d * d, keepdims=True) * (0.5 * n_features)

    dy, dy_b, loss_tiles = _mm("mlp_out_loss", act, wf_mlp_out, "nn", extras=[(h1, "tile"), (tgt2, "tile")],
                               epilogue=loss_head, out_dtypes=(F32, BF16), tile_sums=("scalar",))
    loss_part = jnp.sum(loss_tiles)

    (dw_mlp_out,) = _mm("dw_mlp_out", act, dy_b, "tn")
    (dpre,) = _mm("d_mlp_act", dy_b, wf_mlp_out, "nt", extras=[(act_root, "tile")],
                  epilogue=lambda acc, r: acc * (2.0 * r.astype(F32)), out_dtypes=(BF16,))
    (dw_mlp_in,) = _mm("dw_mlp_in", hn, dpre, "tn")
    scatter_mlp, tok_mlp = _chip_exchange_start("scatter_mlp_start", "scatter", [(dw_mlp_in, 1), (dw_mlp_out, 0)])
    def norm_bwd(dn, res, hx, g):
        _, vjp = jax.vjp(_rms, hx, g)
        dh, dg = vjp(dn)
        return res + dh, dg

    dh1, dg_tiles = _mm("d_norm2_in", dpre, wf_mlp_in, "nt", extras=[(dy, "tile"), (h1, "tile"), (g2 + tok_mlp, "row")],
                        epilogue=norm_bwd, tile_sums=("row",), full_rows=True)
    dg_norm2 = jnp.sum(dg_tiles, axis=0, keepdims=True)
    (dmixed,) = _mm("d_mixed", dh1, wf_out, "nt")
    (dw_out,) = _mm("dw_out", mixed, dh1, "tn")

    def mixed_bwd(dm, sb_, ys, gp, ga, gs):
        _, vjp = jax.vjp(lambda a, act, b, c, d: jnp.concatenate(
            [_rms(a, c), _rms(act * jax.nn.sigmoid(b), d)], axis=-1), sb_, _gelu(ys), gp, ga, gs)
        dsb_, dact, dgp_, dga, dgs = vjp(dm)
        return dsb_, dgp_, dact, dga, dgs, jnp.sum(dgp_, axis=0, keepdims=True)

    dsb, dgate_pre, dact_part, dg_attn, dg_ssm, db_glu = _rowwise(
        "mix_norm_bwd", mixed_bwd, [dmixed, sb, y_ssm, gate_pre], [g_attn, g_ssm],
        [(sb_width, F32), (sb_width, BF16), (sb_width, F32)], [(1, sb_width)] * 3)

    def gelu_bwd(acc, part, ys):
        _, vjp = jax.vjp(_gelu, ys)
        return vjp(acc + part)[0]

    (dy_ssm,) = _mm("d_glu_in", dgate_pre, wf_glu, "nt", extras=[(dact_part, "tile"), (y_ssm, "tile")], epilogue=gelu_bwd)
    (dw_glu,) = _mm("dw_glu", y_ssm, dgate_pre, "tn", a_fn=_gelu)
    scatter_mix, tok_mix = _chip_exchange_start("scatter_mix_start", "scatter", [(dw_glu, 0), (dw_out, 0)])

    dug, dkt_row, db_mat, dc_mat, dla, dlb = _s5_bwd(ug, _tokens_to_groups("dy_to_groups", dy_ssm, 0, sb_width), xin,
                                                     kt_row, b_mat, c_mat, la, lb + tok_mix, batch=batch)
    du = _groups_to_tokens("du_to_tokens", dug)
    d_in = _s5_operators_call("s5_operators_bwd", s5_in, (dkt_row, db_mat, dc_mat, dla, dlb))
    ds5 = [d_in[0][:, 0, :] + d_in[2][:, :, 0], d_in[1][:, 0, :] + d_in[3][:, :, 0], d_in[4][:, 0, 0],
           d_in[5].transpose(0, 2, 1), d_in[6].transpose(0, 2, 1), d_in[7].transpose(0, 2, 1), d_in[8].transpose(0, 2, 1),
           d_in[9][:, 0, :]]

    def pack(parts):
        flat = jnp.concatenate([p.reshape(-1) for p in parts])
        rows = -(-flat.shape[0] // (8 * LANES)) * 8
        return jnp.pad(flat, (0, rows * LANES - flat.shape[0])).reshape(rows, LANES)

    def unpack(packed, names):
        flat, out, off = packed.reshape(-1), {}, 0
        for name in names:
            shape = small_shapes[name]
            size = math.prod(shape)
            out[name] = flat[off:off + size].reshape(shape)
            off += size
        return out, flat[off]

    small_shapes = {name: w.shape for name, w, _, _ in small}
    early_names = ["ssm_lambda_re", "ssm_lambda_im", "ssm_log_dt", "ssm_b_re", "ssm_b_im", "ssm_c_re", "ssm_c_im", "ssm_d",
                   "b_glu", "attn_out_g", "ssm_out_g", "norm2_g"]
    late_names = ["norm1_g", "q_norm_g", "k_norm_g"]
    early = pack([*ds5, db_glu[0], dg_attn[0], dg_ssm[0], dg_norm2[0], loss_part])
    early_exchange, _ = _chip_exchange_start("small_early_start", "all", [(early, 0)])

    dqn, dkn, dv = _attn_bwd(qn, kn, vb, c_tot, dsb, batch=batch, seq=seq, bq=ATTN_BQ, bk=ATTN_BK,
                             after=early_exchange[3][0])
    (early_slots,) = _chip_exchange_wait("small_early_wait", early_exchange, dqn)
    small_g, loss = unpack(_sum_slots("sum_small_early", early_slots), early_names)

    def qk_bwd(q, k, dq_, dk_, dv_, du_, gq_, gk_, ones):
        _, vjp_q = jax.vjp(lambda a, g: _head_rms(a, g, ones), q, gq_)
        _, vjp_k = jax.vjp(lambda a, g: _head_rms(a, g, ones), k, gk_)
        dq, dgq = vjp_q(dq_)
        dk, dgk = vjp_k(dk_)
        return jnp.concatenate([dq, dk, dv_, du_], axis=1), dgq, dgk

    dproj, dgq, dgk = _rowwise("qk_norm_bwd", qk_bwd, [(proj, sb_width, 0), (proj, sb_width, 1), dqn, dkn, dv, du],
                               [gq, gk, ones_blocks], [(4 * sb_width, BF16)], [(1, sb_width)] * 2, tm=512)
    (dw_in,) = _mm("dw_in", xn, dproj, "tn")
    scatter_in, tok_w_in = _chip_exchange_start("scatter_in_start", "scatter", [(dw_in, 1)])
    dx, dg_tiles = _mm("d_norm1_in", dproj, wf_in, "nt", extras=[(dh1, "tile"), (x2, "tile"), (g1 + tok_w_in, "row")],
                       epilogue=norm_bwd, tile_sums=("row",), full_rows=True)
    dg_norm1 = jnp.sum(dg_tiles, axis=0, keepdims=True)

    late = pack([dg_norm1[0], dgq.reshape(heads, HEAD_DIM).sum(0) * qk_scale, dgk.reshape(heads, HEAD_DIM).sum(0),
                 jnp.zeros((1,), F32)])
    late_exchange, _ = _chip_exchange_start("small_late_start", "all", [(late, 0)])

    def adam_big(sa, sb_, w, m, v):
        g = sa + sb_
        delta, m, v = _adamw(g, w, m, v)
        return g, delta, m, v

    def reduce_and_update(tag, params, slots):
        mine = [_sum_slots("sum_" + name, s) for s, (name, *_rest) in zip(slots, params)]
        theirs = _swap_with_sibling("swap_" + tag, mine)
        return {name: _rowwise("adamw_" + name, adam_big, [sa, sb_, w, m, v], [], [(w.shape[1], F32)] * 4)
                for (name, w, m, v, _), sa, sb_ in zip(params, mine, theirs)}

    started = late_exchange[3][0]
    slots_mlp_in, slots_mlp_out = _chip_exchange_wait("scatter_mlp_wait", scatter_mlp, started)
    slots_glu, slots_out = _chip_exchange_wait("scatter_mix_wait", scatter_mix, started)
    big_out = reduce_and_update("rest", big[1:], [slots_glu, slots_out, slots_mlp_in, slots_mlp_out])

    (late_slots,) = _chip_exchange_wait("small_late_wait", late_exchange, big_out["w_mlp_out"][3])
    reduced = _sum_slots("sum_small_late", late_slots)
    small_g.update(unpack(reduced, late_names)[0])
    narrow = {name for name, w, _, _ in small if w.ndim == 3 and w.shape[2] < w.shape[1]}

    def flip(a, name):
        return jnp.swapaxes(a, 1, 2) if name in narrow else a

    small_upd = _adamw_small([flip(small_g[name], name) for name, *_ in small], [flip(w, name) for name, w, _, _ in small],
                             [flip(m, name) for name, _, m, _ in small], [flip(v, name) for name, _, _, v in small])
    small_out = [small_g] + [{name: flip(small_upd[kind][i], name) for i, (name, *_) in enumerate(small)}
                             for kind in range(3)]

    (slots_in,) = _chip_exchange_wait("scatter_in_wait", scatter_in, reduced)
    big_out.update(reduce_and_update("w_in", big[:1], [slots_in]))
    names = ["norm1_g", "w_in", "q_norm_g", "k_norm_g", "ssm_lambda_re", "ssm_lambda_im", "ssm_log_dt", "ssm_b_re",
             "ssm_b_im", "ssm_c_re", "ssm_c_im", "ssm_d", "w_glu", "b_glu", "attn_out_g", "ssm_out_g", "w_out",
             "norm2_g", "w_mlp_in", "w_mlp_out"]
    outs = [loss, dx.reshape(batch, seq, d_model)]
    for kind in range(4):
        for name in names:
            outs.append(big_out[name][kind] if name in big_out else small_out[kind][name])
    return tuple(outs)
```

```python
import functools
import math

import jax
import jax.numpy as jnp
from jax import lax
from jax.experimental import pallas as pl
from jax.experimental.pallas import tpu as pltpu

F32 = jnp.float32
BF16 = jnp.bfloat16
F32_DOT = lax.Precision.HIGH
MESH = pl.DeviceIdType.MESH

RMS_EPS = 1e-6
HEAD_DIM = 64
SSM_GROUP = 16
SSM_CHUNK = 16
LANES = 128
N_CHIPS = 4
N_DEV = 8
VMEM_LIMIT = 48 * 1024 * 1024

ADAM_LR = 0.001
ADAM_B1 = 0.9
ADAM_B2 = 0.999
ADAM_EPS = 1e-08
ADAM_WD = 0.01
ADAM_STEP = 10


def _tile(n, pref):
    t = min(n, pref)
    while n % t:
        t //= 2
    return t


def _params(*sem):
    return pltpu.CompilerParams(dimension_semantics=sem, vmem_limit_bytes=VMEM_LIMIT)


_DIMS = {"nn": (((1,), (0,)), ((), ())), "nt": (((1,), (1,)), ((), ())), "tn": (((0,), (0,)), ((), ()))}


MM_VMEM_BUDGET = 40 * 1024 * 1024


def _mm_tiles(m, n, k, a_bytes, b_bytes, tile_bytes, full_rows=False, max_tm=1024):
    best = None
    for tk in [t for t in (k, k // 2, k // 4, k // 8) if t >= 256 or t == k]:
        for tm in [t for t in (1024, 512, 256, 128) if t <= min(m, max_tm) and m % t == 0]:
            for tn in [n] if full_rows else [t for t in (1024, 512, 256, 128) if t <= n and n % t == 0]:
                need = 2 * (tm * tk * a_bytes + tk * tn * b_bytes) + 2 * tm * tn * tile_bytes + (tm * tn * 4 if tk < k else 0)
                if need > MM_VMEM_BUDGET:
                    continue
                traffic = m * k * a_bytes * (1 if tk == k else n // tn) + k * n * b_bytes * (1 if n == tn and tk == k else m // tm)
                key = (tk < k, traffic, -tm * tn)
                if best is None or key < best[0]:
                    best = (key, (tm, tn, tk))
    return best[1]


def _mm(name, a, b, mode, *, a_fn=None, extras=(), epilogue=None, out_dtypes=(F32,), tile_sums=(), full_rows=False,
        max_tm=1024):
    if mode == "nn":
        (m, k), n = a.shape, b.shape[1]
    elif mode == "nt":
        (m, k), n = a.shape, b.shape[0]
    else:
        (k, m), n = a.shape, b.shape[1]
    outs_spec = [(d, n) if not isinstance(d, tuple) else d for d in out_dtypes]
    sums_spec = [(s, n) if not isinstance(s, tuple) else s for s in tile_sums]
    assert full_rows or all(w == n for _, w in outs_spec + sums_spec) and all(e.shape[1] == n for e, _ in extras)
    tile_bytes = (sum(e.dtype.itemsize * e.shape[1] for e, kind in extras if kind == "tile")
                  + sum(jnp.dtype(d).itemsize * w for d, w in outs_spec)) // n + 1
    tm, tn, tk = _mm_tiles(m, n, k, a.dtype.itemsize, b.dtype.itemsize, tile_bytes, full_rows, max_tm)
    nk = k // tk
    ne, nout = len(extras), len(out_dtypes)
    dims = _DIMS[mode]

    def width_spec(rows, w):
        if w == n:
            return pl.BlockSpec((rows, tn), (lambda i, j, kk: (i, j)) if rows != 1 else (lambda i, j, kk: (0, j)))
        return pl.BlockSpec((rows, w), (lambda i, j, kk: (i, 0)) if rows != 1 else (lambda i, j, kk: (0, 0)))

    def body(a_ref, b_ref, *rest):
        ex, outs, sums = rest[:ne], rest[ne:ne + nout], rest[ne + nout:ne + nout + len(tile_sums)]
        at = a_ref[...]
        if a_fn is not None:
            at = a_fn(at)
        p = lax.dot_general(at.astype(BF16), b_ref[...].astype(BF16), dims, preferred_element_type=F32)

        def finish(r):
            if epilogue is not None:
                r = epilogue(r, *[e[...] for e in ex])
            if not isinstance(r, (tuple, list)):
                r = (r,)
            for o, v in zip(outs, r[:nout]):
                o[...] = v.astype(o.dtype)
            for o, v, (kind, _) in zip(sums, r[nout:], sums_spec):
                first = lax.broadcasted_iota(jnp.int32, o.shape, 0) == 0
                if kind == "scalar":
                    first &= lax.broadcasted_iota(jnp.int32, o.shape, 1) == 0
                o[...] = jnp.where(first, v, 0.0)

        if nk == 1:
            finish(p)
        else:
            acc = rest[ne + nout + len(tile_sums)]
            kk = pl.program_id(2)

            @pl.when(kk == 0)
            def _():
                acc[...] = p

            @pl.when(kk > 0)
            def _():
                acc[...] += p

            @pl.when(kk == nk - 1)
            def _():
                finish(acc[...])

    if mode == "tn":
        a_spec = pl.BlockSpec((tk, tm), lambda i, j, kk: (kk, i))
    else:
        a_spec = pl.BlockSpec((tm, tk), lambda i, j, kk: (i, kk))
    if mode == "nt":
        b_spec = pl.BlockSpec((tn, tk), lambda i, j, kk: (j, kk))
    else:
        b_spec = pl.BlockSpec((tk, tn), lambda i, j, kk: (kk, j))
    ex_specs = [width_spec(tm if kind == "tile" else 1, e.shape[1]) for e, kind in extras]
    return pl.pallas_call(
        body, name=name, grid=(m // tm, n // tn, nk),
        in_specs=[a_spec, b_spec] + ex_specs,
        out_specs=([width_spec(tm, w) for _, w in outs_spec]
                   + [pl.BlockSpec((8, LANES), lambda i, j, kk: (i, j)) if kind == "scalar" else width_spec(8, w)
                      for kind, w in sums_spec]),
        out_shape=([jax.ShapeDtypeStruct((m, w), dt) for dt, w in outs_spec]
                   + [jax.ShapeDtypeStruct((m // tm * 8, n // tn * LANES if kind == "scalar" else w), F32)
                      for kind, w in sums_spec]),
        scratch_shapes=[pltpu.VMEM((tm, tn), F32)] if nk > 1 else [],
        compiler_params=_params("parallel", "parallel", "arbitrary"),
    )(a, b, *[e for e, _ in extras])


def _rowwise(name, fn, rows, consts, row_outs, acc_outs=(), tm=256):
    norm = [r if isinstance(r, tuple) else (r, r.shape[1], 0) for r in rows]
    t = norm[0][0].shape[0]
    tm = _tile(t, tm)
    nr, nc, no = len(norm), len(consts), len(row_outs)

    def body(*refs):
        outs = fn(*[r[...] for r in refs[:nr + nc]])
        if not isinstance(outs, (tuple, list)):
            outs = (outs,)
        o_refs, a_refs = refs[nr + nc:nr + nc + no], refs[nr + nc + no:]
        for r, v in zip(o_refs, outs[:no]):
            r[...] = v.astype(r.dtype)
        if a_refs:
            i = pl.program_id(0)

            @pl.when(i == 0)
            def _():
                for r, v in zip(a_refs, outs[no:]):
                    r[...] = v

            @pl.when(i > 0)
            def _():
                for r, v in zip(a_refs, outs[no:]):
                    r[...] += v

    in_specs = [pl.BlockSpec((tm, w), functools.partial(lambda i, cb: (i, cb), cb=cb)) for _, w, cb in norm]
    in_specs += [pl.BlockSpec(c.shape, functools.partial(lambda i, nd: (0,) * nd, nd=c.ndim)) for c in consts]
    out_specs = [pl.BlockSpec((tm, w), lambda i: (i, 0)) for w, _ in row_outs]
    out_specs += [pl.BlockSpec(s, functools.partial(lambda i, nd: (0,) * nd, nd=len(s))) for s in acc_outs]
    out_shape = [jax.ShapeDtypeStruct((t, w), dt) for w, dt in row_outs]
    out_shape += [jax.ShapeDtypeStruct(s, F32) for s in acc_outs]
    return pl.pallas_call(
        body, name=name, grid=(t // tm,), in_specs=in_specs, out_specs=out_specs, out_shape=out_shape,
        compiler_params=_params("arbitrary"),
    )(*[r[0] for r in norm], *consts)


def _rms(x, g):
    return x * lax.rsqrt(jnp.mean(x * x, axis=-1, keepdims=True) + RMS_EPS) * g


@jax.custom_vjp
def _head_sums(x, ones_blocks):
    parts = [jnp.dot(x[:, j:j + LANES], ones_blocks, precision=F32_DOT, preferred_element_type=F32)
             for j in range(0, x.shape[1], LANES)]
    return jnp.concatenate(parts, axis=1)


_head_sums.defvjp(lambda x, ones_blocks: (_head_sums(x, ones_blocks), ones_blocks),
                  lambda ones_blocks, ct: (_head_sums(ct, ones_blocks), None))


def _head_rms(x, g, ones_blocks):
    return x * lax.rsqrt(_head_sums(x * x, ones_blocks) * (1.0 / HEAD_DIM) + RMS_EPS) * g


def _gelu(x):
    return x * (0.5 * (1.0 + jnp.tanh(math.sqrt(2.0 / math.pi) * (x + 0.044715 * (x * x * x)))))


def _mixed(sb, y_ssm, gate_pre, g_attn, g_ssm):
    ssm = _gelu(y_ssm) * jax.nn.sigmoid(gate_pre)
    return jnp.concatenate([_rms(sb, g_attn), _rms(ssm, g_ssm)], axis=-1)


def _softplus(z):
    return jnp.maximum(z, 0.0) + jnp.log(1.0 + jnp.exp(-jnp.abs(z)))


def _running_sums(x, tri):
    return jnp.dot(x.astype(BF16), tri, preferred_element_type=F32)


def _dot_nt(a, b, **kw):
    return lax.dot_general(a, b, _DIMS["nt"], preferred_element_type=F32, **kw)


def _dot_tn(a, b, **kw):
    return lax.dot_general(a, b, _DIMS["tn"], preferred_element_type=F32, **kw)


ATTN_BQ, ATTN_BK = 2048, 256
HEAD_LANES = tuple(slice(h * HEAD_DIM, (h + 1) * HEAD_DIM) for h in range(LANES // HEAD_DIM))


def _attn_fwd(qs, kn, v, *, batch, seq, bq, bk):
    width = qs.shape[1]
    bq = _tile(seq, bq)
    bk = _tile(bq, bk)
    nq, kpq = seq // bq, bq // bk

    def body(q_ref, k_ref, v_ref, o_ref, c_ref):
        row = lax.broadcasted_iota(jnp.int32, (bq, bk), 0)
        col = lax.broadcasted_iota(jnp.int32, (bq, bk), 1)
        tri = (lax.broadcasted_iota(jnp.int32, (bk, bk), 0) >= lax.broadcasted_iota(jnp.int32, (bk, bk), 1)).astype(BF16)

        def q_block(qi, carry):
            r0 = pl.multiple_of(qi * bq, bq)
            qh = [q_ref[pl.ds(r0, bq), ln] for ln in HEAD_LANES]

            def tile(k0, state, top=0):
                diag = top is not None
                top = top or 0
                msk = (col < row)[:bq - top] if diag else None
                new = []
                for h, ln in enumerate(HEAD_LANES):
                    o, c = state[2 * h], state[2 * h + 1]
                    z = _dot_nt(qh[h][top:], k_ref[pl.ds(k0, bk), ln])
                    sp = _softplus(z)
                    if diag:
                        sp = jnp.where(msk, sp, 0.0)
                    r = _running_sums(sp, tri)
                    a = jnp.exp(z - r - c[top:])
                    if diag:
                        a = jnp.where(msk, a, 0.0)
                    o_new = o[top:] + jnp.dot(a.astype(BF16), v_ref[pl.ds(k0, bk), ln], preferred_element_type=F32)
                    c_new = c[top:] + r[:, 0:1]
                    if top:
                        o_new, c_new = jnp.concatenate([o[:top], o_new]), jnp.concatenate([c[:top], c_new])
                    new += [o_new, c_new]
                return tuple(new)

            state = (jnp.zeros((bq, HEAD_DIM), F32), jnp.zeros((bq, 1), F32)) * len(HEAD_LANES)
            for d in reversed(range(kpq)):
                state = tile(pl.multiple_of(r0 + d * bk, bk), state, top=d * bk)
            state = lax.fori_loop(0, qi * kpq, lambda it, st: tile(pl.multiple_of(r0 - (it + 1) * bk, bk), st, None),
                                  state)
            for h, ln in enumerate(HEAD_LANES):
                o_ref[pl.ds(r0, bq), ln] = state[2 * h]
                c_ref[pl.ds(r0, bq), ln] = jnp.broadcast_to(state[2 * h + 1], (bq, HEAD_DIM))
            return carry

        lax.fori_loop(0, nq, q_block, 0)

    spec = pl.BlockSpec((seq, LANES), lambda b, h: (b, h))
    shape = jax.ShapeDtypeStruct((batch * seq, width), F32)
    return pl.pallas_call(
        body, name="attn_fwd", grid=(batch, width // LANES), in_specs=[spec, spec, spec], out_specs=[spec, spec],
        out_shape=[shape, shape], compiler_params=_params("parallel", "parallel"),
    )(qs, kn, v)


def _attn_bwd(qs, kn, v, c_tot, do, *, batch, seq, bq, bk, after):
    width = qs.shape[1]
    bq = _tile(seq, bq)
    bk = _tile(bq, bk)
    nq, kpq = seq // bq, bq // bk

    def body(q_ref, k_ref, v_ref, c_ref, do_ref, after_ref, dq_ref, dk_ref, dv_ref):
        row = lax.broadcasted_iota(jnp.int32, (bq, bk), 0)
        col = lax.broadcasted_iota(jnp.int32, (bq, bk), 1)
        sq_row = lax.broadcasted_iota(jnp.int32, (bk, bk), 0)
        sq_col = lax.broadcasted_iota(jnp.int32, (bk, bk), 1)
        tri = (sq_row >= sq_col).astype(BF16)
        tri_t = (sq_row <= sq_col).astype(BF16)
        dk_ref[...] = jnp.zeros_like(dk_ref)
        dv_ref[...] = jnp.zeros_like(dv_ref)

        def q_block(qi, carry):
            r0 = pl.multiple_of(qi * bq, bq)
            qh = [q_ref[pl.ds(r0, bq), ln] for ln in HEAD_LANES]
            d_out = [do_ref[pl.ds(r0, bq), ln].astype(BF16) for ln in HEAD_LANES]
            c_all = [c_ref[pl.ds(r0, bq), ln][:, 0:1] for ln in HEAD_LANES]

            def tile(k0, state, top=0):
                diag = top is not None
                top = top or 0
                last = diag and top == bq - bk
                msk = (col < row)[:bq - top] if diag else None
                new = []
                for h, ln in enumerate(HEAD_LANES):
                    c_left, g_left, dq = state[3 * h:3 * h + 3]
                    q, d_o = qh[h][top:], d_out[h][top:]
                    k = k_ref[pl.ds(k0, bk), ln]
                    z = _dot_nt(q, k)
                    e = jnp.exp(-jnp.abs(z))
                    sp = jnp.maximum(z, 0.0) + jnp.log(1.0 + e)
                    sig = jnp.exp(z - sp)
                    if diag:
                        sp = jnp.where(msk, sp, 0.0)
                    r = _running_sums(sp, tri)
                    c_new = c_left[top:] + r[:, 0:1]
                    a = jnp.exp(z - r - (0.0 if last else c_all[h][top:] - c_new))
                    if diag:
                        a = jnp.where(msk, a, 0.0)
                    g = a * _dot_nt(d_o, v_ref[pl.ds(k0, bk), ln])
                    pg = _running_sums(g, tri_t)
                    dz = g - sig * (g_left[top:] + pg)
                    if diag:
                        dz = jnp.where(msk, dz, 0.0)
                    dz = dz.astype(BF16)
                    dk_ref[pl.ds(k0, bk), ln] += _dot_tn(dz, q)
                    dv_ref[pl.ds(k0, bk), ln] += _dot_tn(a.astype(BF16), d_o)
                    g_new = g_left[top:] + pg[:, bk - 1:bk]
                    dq_new = dq[top:] + jnp.dot(dz, k, preferred_element_type=F32)
                    if top:
                        c_new = jnp.concatenate([c_left[:top], c_new])
                        g_new = jnp.concatenate([g_left[:top], g_new])
                        dq_new = jnp.concatenate([dq[:top], dq_new])
                    new += [c_new, g_new, dq_new]
                return tuple(new)

            zero = jnp.zeros((bq, 1), F32)
            init = (zero, zero, jnp.zeros((bq, HEAD_DIM), F32)) * len(HEAD_LANES)
            state = lax.fori_loop(0, qi * kpq, lambda it, st: tile(pl.multiple_of(it * bk, bk), st, None), init)
            for d in range(kpq):
                state = tile(pl.multiple_of(r0 + d * bk, bk), state, top=d * bk)
            for h, ln in enumerate(HEAD_LANES):
                dq_ref[pl.ds(r0, bq), ln] = state[3 * h + 2]
            return carry

        lax.fori_loop(0, nq, q_block, 0)

    spec = pl.BlockSpec((seq, LANES), lambda b, h: (b, h))
    shape = jax.ShapeDtypeStruct((batch * seq, width), F32)
    return pl.pallas_call(
        body, name="attn_bwd", grid=(batch, width // LANES),
        in_specs=[spec] * 5 + [pl.BlockSpec(memory_space=pl.ANY)], out_specs=[spec] * 3,
        out_shape=[shape] * 3, compiler_params=_params("parallel", "parallel"),
    )(qs, kn, v, c_tot, do, after)


def _pattern(rows, cols, hit):
    r, c = lax.broadcasted_iota(jnp.int32, (rows, cols), 0), lax.broadcasted_iota(jnp.int32, (rows, cols), 1)
    return hit(r, c).astype(F32)


def _s5_group_operators(lr_r, li_r, lr_c, li_c, log_dt, bt_re, bt_im, ct_re, ct_im, d_row):
    cs = SSM_CHUNK
    n_ch, n_state = bt_re.shape
    width = cs * n_ch
    dt = jnp.exp(log_dt)

    def spread(x, pattern):
        return jnp.dot(x, pattern, precision=F32_DOT, preferred_element_type=F32)

    twice = _pattern(n_state, 2 * n_state, lambda r, c: r == c % n_state)
    steps = lax.broadcasted_iota(jnp.int32, (cs + 1, 1), 0).astype(F32)
    mag = jnp.exp(steps * (lr_r * dt))
    ang = steps * (li_r * dt)
    pw_re, pw_im = mag * jnp.cos(ang), mag * jnp.sin(ang)
    num_re, num_im = pw_re[1:2] - 1.0, pw_im[1:2]
    den = lr_r * lr_r + li_r * li_r
    cf_re = (num_re * lr_r + num_im * li_r) / den
    cf_im = (num_im * lr_r - num_re * li_r) / den
    bb_re = spread(cf_re * bt_re - cf_im * bt_im, twice)
    bb_im = spread(cf_re * bt_im + cf_im * bt_re, twice)
    pw2_re, pw2_im = spread(pw_re, twice), spread(pw_im, twice)
    real_half = lax.broadcasted_iota(jnp.int32, (1, 2 * n_state), 1) < n_state
    blocks = []
    for s in range(cs):
        pr, pi = pw2_re[cs - 1 - s:cs - s], pw2_im[cs - 1 - s:cs - s]
        blocks.append(jnp.where(real_half, bb_re * pr - bb_im * pi, bb_re * pi + bb_im * pr))
    b_mat = jnp.concatenate(blocks, axis=0)
    la = pw2_re[cs:cs + 1]
    lb = jnp.where(real_half, -pw2_im[cs:cs + 1], pw2_im[cs:cs + 1])

    lane = lax.broadcasted_iota(jnp.int32, (1, width), 1)
    tile_out = _pattern(n_ch, width, lambda r, c: r == c % n_ch)
    c_re, c_im = spread(ct_re, tile_out), spread(ct_im, tile_out)

    k_row = lax.broadcasted_iota(jnp.int32, (1, cs), 1).astype(F32)
    m, a = jnp.exp(k_row * (lr_c * dt)), k_row * (li_c * dt)
    repeat = _pattern(cs, width, lambda r, c: r == c // n_ch)
    p_re, p_im = spread(m * jnp.cos(a), repeat), spread(m * jnp.sin(a), repeat)
    w_re, w_im = p_re * c_re - p_im * c_im, p_re * c_im + p_im * c_re
    skip = jnp.where((lane < n_ch) & (lane == lax.broadcasted_iota(jnp.int32, (n_ch, width), 0)),
                     spread(d_row, tile_out), 0.0)
    kt_row = (jnp.dot(bb_re[:, :n_state], w_re, precision=F32_DOT, preferred_element_type=F32)
              - jnp.dot(bb_im[:, :n_state], w_im, precision=F32_DOT, preferred_element_type=F32) + skip)
    bar_re, bar_im = jnp.exp(lr_c * dt) * jnp.cos(li_c * dt), jnp.exp(lr_c * dt) * jnp.sin(li_c * dt)
    w1_re, w1_im = w_re * bar_re - w_im * bar_im, w_re * bar_im + w_im * bar_re
    c_mat = jnp.concatenate([w1_re, -w1_im], axis=0)
    return kt_row, b_mat, c_mat, la, lb


def _s5_operator_inputs(lam_re, lam_im, log_dt, b_re, b_im, c_re, c_im, d_skip):
    return (lam_re[:, None, :], lam_im[:, None, :], lam_re[:, :, None], lam_im[:, :, None], log_dt[:, None, None],
            b_re.transpose(0, 2, 1), b_im.transpose(0, 2, 1), c_re.transpose(0, 2, 1), c_im.transpose(0, 2, 1),
            d_skip[:, None, :])


def _s5_operators_call(name, args, cotangents=None, gb=8):
    groups = args[0].shape[0]
    gb = _tile(groups, gb)
    n_in = len(args)

    def body(*refs):
        n_ct = 0 if cotangents is None else len(cotangents)
        ins, cts, outs = refs[:n_in], refs[n_in:n_in + n_ct], refs[n_in + n_ct:]
        for g in range(gb):
            vals = [r[g] for r in ins]
            if cotangents is None:
                res = _s5_group_operators(*vals)
            else:
                res = jax.vjp(_s5_group_operators, *vals)[1](tuple(c[g] for c in cts))
            for o, v in zip(outs, res):
                o[g] = v

    def spec(a):
        return pl.BlockSpec((gb, *a.shape[1:]), lambda i: (i, 0, 0))

    if cotangents is None:
        n_ch, n_state = args[5].shape[1:]
        width = SSM_CHUNK * n_ch
        out_shape = [jax.ShapeDtypeStruct((groups, *s), F32) for s in
                     ((n_ch, width), (width, 2 * n_state), (2 * n_state, width), (1, 2 * n_state), (1, 2 * n_state))]
    else:
        out_shape = [jax.ShapeDtypeStruct(a.shape, F32) for a in args]
    operands = [*args, *(cotangents or ())]
    return pl.pallas_call(
        body, name=name, grid=(groups // gb,), in_specs=[spec(a) for a in operands], out_specs=[spec(s) for s in out_shape],
        out_shape=out_shape, compiler_params=_params("parallel"),
    )(*operands)


GROUPS_PER_BLOCK = LANES // SSM_GROUP


def _tokens_to_groups(name, u, col_block, width):
    t = u.shape[0]
    n = t // SSM_CHUNK
    ch = SSM_CHUNK * SSM_GROUP
    blocks = width // LANES
    nb = GROUPS_PER_BLOCK

    def body(u_ref, o_ref):
        block = lax.broadcasted_iota(jnp.int32, (n, LANES), 1) // SSM_GROUP
        for half in range(SSM_CHUNK // nb):
            rows = [u_ref[pl.ds(half * nb + s, n, stride=SSM_CHUNK), :] for s in range(nb)]
            for shift in range(nb):
                merged = rows[shift]
                for b in range(1, nb):
                    merged = jnp.where(block == b, rows[(b + shift) % nb], merged)
                moved = pltpu.roll(merged, shift * SSM_GROUP, 1) if shift else merged
                for b in range(nb):
                    s = (b + shift) % nb
                    o_ref[b, :, half * LANES + s * SSM_GROUP:half * LANES + (s + 1) * SSM_GROUP] = (
                        moved[:, s * SSM_GROUP:(s + 1) * SSM_GROUP])

    return pl.pallas_call(
        body, name=name, grid=(blocks,),
        in_specs=[pl.BlockSpec((t, LANES), lambda j: (0, col_block * blocks + j))],
        out_specs=pl.BlockSpec((GROUPS_PER_BLOCK, n, ch), lambda j: (j, 0, 0)),
        out_shape=jax.ShapeDtypeStruct((width // SSM_GROUP, n, ch), F32), compiler_params=_params("parallel"),
    )(u)


def _groups_to_tokens(name, ug):
    groups, n, ch = ug.shape
    nb = GROUPS_PER_BLOCK

    def body(g_ref, o_ref, rows_ref):
        block = lax.broadcasted_iota(jnp.int32, (n, LANES), 1) // SSM_GROUP
        for half in range(SSM_CHUNK // nb):
            src = [g_ref[b, :, half * LANES:(half + 1) * LANES] for b in range(nb)]
            for shift in range(nb):
                merged = src[-shift % nb]
                for s in range(1, nb):
                    merged = jnp.where(block == s, src[(s - shift) % nb], merged)
                moved = pltpu.roll(merged, (nb - shift) * SSM_GROUP, 1) if shift else merged
                for b in range(nb):
                    rows_ref[(b + shift) % nb, :, b * SSM_GROUP:(b + 1) * SSM_GROUP] = moved[:, b * SSM_GROUP:(b + 1) * SSM_GROUP]
            for s in range(nb):
                o_ref[pl.ds(half * nb + s, n, stride=SSM_CHUNK), :] = rows_ref[s]

    return pl.pallas_call(
        body, name=name, grid=(groups // GROUPS_PER_BLOCK,),
        in_specs=[pl.BlockSpec((GROUPS_PER_BLOCK, n, ch), lambda j: (j, 0, 0))],
        out_specs=pl.BlockSpec((n * SSM_CHUNK, LANES), lambda j: (0, j)),
        out_shape=jax.ShapeDtypeStruct((n * SSM_CHUNK, groups * SSM_GROUP), F32),
        scratch_shapes=[pltpu.VMEM((nb, n, LANES), F32)], compiler_params=_params("parallel"),
    )(ug)


SCAN_ROWS = 8


def _toeplitz_to(tm_ref, g, kt_row):
    width = kt_row.shape[1]
    tm_ref[g] = jnp.zeros((width, width), F32)
    for s in range(SSM_CHUNK):
        tm_ref[g, s * SSM_GROUP:(s + 1) * SSM_GROUP, s * SSM_GROUP:] = kt_row[:, :width - s * SSM_GROUP]


def _lam_powers(la, lb, reverse):
    if reverse:
        lb = -lb

    def mul(p, q):
        return p[0] * q[0] - p[1] * q[1], p[0] * q[1] + p[1] * q[0]

    p1 = (la, lb)
    p2 = mul(p1, p1)
    p3 = mul(p2, p1)
    p4 = mul(p2, p2)
    rows = [p1, p2, p3, p4, mul(p4, p1), mul(p4, p2), mul(p4, p3), mul(p4, p4)]
    if reverse:
        rows = rows[::-1]
    idx = lax.broadcasted_iota(jnp.int32, (SCAN_ROWS, la.shape[1]), 0)
    tab_a = sum(jnp.where(idx == j, r[0], 0.0) for j, r in enumerate(rows))
    tab_b = sum(jnp.where(idx == j, r[1], 0.0) for j, r in enumerate(rows))
    return (p1, p2, p4), (tab_a, tab_b), idx


def _scan_block(e, carry, steps, table, idx, half, reverse):
    n = SCAN_ROWS
    for d, (pa, pb) in zip((1, 2, 4), steps):
        sh = pltpu.roll(e, n - d if reverse else d, 0)
        sh = jnp.where(idx < n - d if reverse else idx >= d, sh, 0.0)
        e = e + pa * sh + pb * pltpu.roll(sh, half, 1)
    tab_a, tab_b = table
    e = e + tab_a * carry + tab_b * pltpu.roll(carry, half, 1)
    shifted = jnp.where(idx == (n - 1 if reverse else 0), carry, pltpu.roll(e, n - 1 if reverse else 1, 0))
    edge = e[0:1] if reverse else e[n - 1:n]
    return shifted, jnp.broadcast_to(edge, e.shape)


def _s5_fwd(ug, kt_row, b_mat, c_mat, la, lb, *, batch, gb=8):
    groups, n, ch = ug.shape
    p2 = b_mat.shape[2]
    gb = _tile(groups, gb)
    nch = n // batch
    nblk = nch // SCAN_ROWS

    def body(u_ref, k_ref, b_ref, c_ref, la_ref, lb_ref, y_ref, x_ref, s_ref, tm_ref):
        for g in range(gb):
            _toeplitz_to(tm_ref, g, k_ref[g])
            s_ref[g] = jnp.dot(u_ref[g], b_ref[g], precision=F32_DOT, preferred_element_type=F32)
        powers = [_lam_powers(la_ref[g], lb_ref[g], False) for g in range(gb)]

        def step(blk, carries):
            new = []
            for g in range(gb):
                steps, table, idx = powers[g]
                for b in range(batch):
                    rows = pl.ds(pl.multiple_of(b * nch + blk * SCAN_ROWS, SCAN_ROWS), SCAN_ROWS)
                    x_in, carry = _scan_block(s_ref[g, rows, :], carries[g * batch + b], steps, table, idx, p2 // 2, False)
                    x_ref[g, rows, :] = x_in
                    new.append(carry)
            return tuple(new)

        lax.fori_loop(0, nblk, step, tuple(jnp.zeros((SCAN_ROWS, p2), F32) for _ in range(gb * batch)))
        for g in range(gb):
            y_ref[g] = (jnp.dot(u_ref[g], tm_ref[g], precision=F32_DOT, preferred_element_type=F32)
                        + jnp.dot(x_ref[g], c_ref[g], precision=F32_DOT, preferred_element_type=F32))

    def spec(a, b):
        return pl.BlockSpec((gb, a, b), lambda i: (i, 0, 0))

    return pl.pallas_call(
        body, name="s5_fwd", grid=(groups // gb,),
        in_specs=[spec(n, ch), spec(SSM_GROUP, ch), spec(ch, p2), spec(p2, ch), spec(1, p2), spec(1, p2)],
        out_specs=[spec(n, ch), spec(n, p2)],
        out_shape=[jax.ShapeDtypeStruct((groups, n, ch), F32), jax.ShapeDtypeStruct((groups, n, p2), F32)],
        scratch_shapes=[pltpu.VMEM((gb, n, p2), F32), pltpu.VMEM((gb, ch, ch), F32)],
        compiler_params=_params("parallel"),
    )(ug, kt_row, b_mat, c_mat, la, lb)


def _s5_bwd(ug, dyg, xin, kt_row, b_mat, c_mat, la, lb, *, batch, gb=8):
    groups, n, ch = ug.shape
    p2 = b_mat.shape[2]
    gb = _tile(groups, gb)
    nch = n // batch
    nblk = nch // SCAN_ROWS

    def body(u_ref, dy_ref, x_ref, k_ref, b_ref, c_ref, la_ref, lb_ref,
             du_ref, dk_ref, db_ref, dc_ref, dla_ref, dlb_ref, dx_ref, ds_ref, tm_ref):
        for g in range(gb):
            _toeplitz_to(tm_ref, g, k_ref[g])
            dx_ref[g] = _dot_nt(dy_ref[g], c_ref[g], precision=F32_DOT)
        powers = [_lam_powers(la_ref[g], lb_ref[g], True) for g in range(gb)]

        def step(it, carries):
            new = []
            for g in range(gb):
                steps, table, idx = powers[g]
                for b in range(batch):
                    rows = pl.ds(pl.multiple_of(b * nch + (nblk - 1 - it) * SCAN_ROWS, SCAN_ROWS), SCAN_ROWS)
                    d_s, carry = _scan_block(dx_ref[g, rows, :], carries[g * batch + b], steps, table, idx, p2 // 2, True)
                    ds_ref[g, rows, :] = d_s
                    new.append(carry)
            return tuple(new)

        lax.fori_loop(0, nblk, step, tuple(jnp.zeros((SCAN_ROWS, p2), F32) for _ in range(gb * batch)))
        for g in range(gb):
            u, dy, ds, x = u_ref[g], dy_ref[g], ds_ref[g], x_ref[g]
            du_ref[g] = _dot_nt(dy, tm_ref[g], precision=F32_DOT) + _dot_nt(ds, b_ref[g], precision=F32_DOT)
            tm_ref[g] = _dot_tn(u, dy, precision=F32_DOT)
            dk_ref[g] = tm_ref[g, 0:SSM_GROUP, :]
            for s in range(1, SSM_CHUNK):
                dk_ref[g, :, :ch - s * SSM_GROUP] += tm_ref[g, s * SSM_GROUP:(s + 1) * SSM_GROUP, s * SSM_GROUP:]
            db_ref[g] = _dot_tn(u, ds, precision=F32_DOT)
            dc_ref[g] = _dot_tn(x, dy, precision=F32_DOT)
            dla_ref[g] = jnp.sum(ds * x, axis=0, keepdims=True)
            dlb_ref[g] = jnp.sum(ds * pltpu.roll(x, p2 // 2, 1), axis=0, keepdims=True)

    def spec(a, b):
        return pl.BlockSpec((gb, a, b), lambda i: (i, 0, 0))

    def shape(a, b):
        return jax.ShapeDtypeStruct((groups, a, b), F32)

    return pl.pallas_call(
        body, name="s5_bwd", grid=(groups // gb,),
        in_specs=[spec(n, ch), spec(n, ch), spec(n, p2), spec(SSM_GROUP, ch), spec(ch, p2), spec(p2, ch), spec(1, p2),
                  spec(1, p2)],
        out_specs=[spec(n, ch), spec(SSM_GROUP, ch), spec(ch, p2), spec(p2, ch), spec(1, p2), spec(1, p2)],
        out_shape=[shape(n, ch), shape(SSM_GROUP, ch), shape(ch, p2), shape(p2, ch), shape(1, p2), shape(1, p2)],
        scratch_shapes=[pltpu.VMEM((gb, n, p2), F32), pltpu.VMEM((gb, n, p2), F32), pltpu.VMEM((gb, ch, ch), F32)],
        compiler_params=_params("parallel"),
    )(ug, dyg, xin, kt_row, b_mat, c_mat, la, lb)


def _block(ref, axis, j, size):
    start = j * size if isinstance(j, int) else pl.multiple_of(j * size, size)
    return ref.at[pl.ds(start, size), :] if axis == 0 else ref.at[:, pl.ds(start, size)]


def _chip_exchange_copies(mode, axes, srcs, lands, send_sems, recv_sems, local_sems):
    x, y, c = lax.axis_index("x"), lax.axis_index("y"), lax.axis_index("c")
    everyone = mode == "all"
    me = 4 * x + 2 * y + c if everyone else 2 * x + y
    n_peers = _exchange_peers(mode)
    local, sends, arrivals = [], [], []
    for w, axis in enumerate(axes):
        if mode == "gather":
            size = srcs[w].shape[axis]
            local.append(pltpu.make_async_copy(srcs[w], _block(lands[w], axis, me, size), local_sems.at[w]))
        elif mode == "scatter":
            size = srcs[w].shape[axis] // N_CHIPS
            local.append(pltpu.make_async_copy(_block(srcs[w], axis, me, size), lands[w].at[me], local_sems.at[w]))
        else:
            local.append(pltpu.make_async_copy(srcs[w], lands[w].at[me], local_sems.at[w]))
        for k in range(1, n_peers + 1):
            bits = k if everyone else 2 * k
            px = 1 - x if bits & 4 else x
            py = 1 - y if bits & 2 else y
            pc = 1 - c if bits & 1 else c
            peer = 4 * px + 2 * py + pc if everyone else 2 * px + py
            if mode == "gather":
                src, dst, arrive = srcs[w], _block(lands[w], axis, me, size), _block(lands[w], axis, peer, size)
            elif mode == "scatter":
                src, dst, arrive = _block(srcs[w], axis, peer, size), lands[w].at[me], lands[w].at[peer]
            else:
                src, dst, arrive = srcs[w], lands[w].at[me], lands[w].at[peer]
            sem = w * n_peers + k - 1
            for target, out in ((dst, sends), (arrive, arrivals)):
                out.append(pltpu.make_async_remote_copy(
                    src_ref=src, dst_ref=target, send_sem=send_sems.at[sem], recv_sem=recv_sems.at[sem],
                    device_id=(px, py, pc), device_id_type=MESH))
    return local, sends, arrivals


def _exchange_peers(mode):
    return N_DEV - 1 if mode == "all" else N_CHIPS - 1


def _chip_exchange_start(name, mode, items, after=None):
    n = len(items)
    n_after = 0 if after is None else 1
    axes = [axis for _, axis in items]
    hbm = pl.BlockSpec(memory_space=pltpu.HBM)
    sem = pl.BlockSpec(memory_space=pltpu.SEMAPHORE)
    lands = []
    for a, axis in items:
        shape = list(a.shape)
        if mode == "gather":
            shape[axis] *= N_CHIPS
        elif mode == "scatter":
            shape[axis] //= N_CHIPS
            shape = [N_CHIPS] + shape
        else:
            shape = [N_DEV] + shape
        lands.append(pltpu.with_memory_space_constraint(lax.empty(tuple(shape), a.dtype), pltpu.HBM))

    def body(*refs):
        srcs, land_refs = refs[:n], refs[n:2 * n]
        send_sems, recv_sems, local_sems = refs[2 * n + n_after:2 * n + n_after + 3]
        token = refs[-1]
        local, sends, _ = _chip_exchange_copies(mode, axes, srcs, land_refs, send_sems, recv_sems, local_sems)
        for cp in local + sends:
            cp.start()
        token[...] = jnp.zeros_like(token)

    n_sem = n * _exchange_peers(mode)
    outs = pl.pallas_call(
        body, name=name,
        out_shape=(pltpu.SemaphoreType.DMA((n_sem,)), pltpu.SemaphoreType.DMA((n_sem,)), pltpu.SemaphoreType.DMA((n,)),
                   *[pltpu.HBM(a.shape, a.dtype) for a, _ in items], *[pltpu.HBM(l.shape, l.dtype) for l in lands],
                   jax.ShapeDtypeStruct((8, LANES), F32)),
        in_specs=[hbm] * (2 * n) + [pl.BlockSpec(memory_space=pl.ANY)] * n_after,
        out_specs=(sem, sem, sem, *[hbm] * (2 * n), pl.BlockSpec(memory_space=pltpu.VMEM)),
        input_output_aliases={i: 3 + i for i in range(2 * n)},
        compiler_params=pltpu.CompilerParams(has_side_effects=pltpu.SideEffectType.DATAFLOW_SIDE_EFFECTING),
    )(*[pltpu.with_memory_space_constraint(a, pltpu.HBM) for a, _ in items], *lands, *([after] if n_after else []))
    return (mode, axes, outs[:3], outs[3:3 + n], outs[3 + n:3 + 2 * n]), outs[-1][0:1, 0:1]


def _chip_exchange_wait(name, handle, after):
    mode, axes, sems, srcs, lands = handle
    n = len(axes)
    after = list(after) if isinstance(after, (tuple, list)) else [after]
    hbm = pl.BlockSpec(memory_space=pltpu.HBM)
    sem = pl.BlockSpec(memory_space=pltpu.SEMAPHORE)

    def body(*refs):
        src_refs, land_refs = refs[:n], refs[n:2 * n]
        send_sems, recv_sems, local_sems = refs[2 * n:2 * n + 3]
        local, sends, arrivals = _chip_exchange_copies(mode, axes, src_refs, land_refs, send_sems, recv_sems, local_sems)
        for cp in sends:
            cp.wait_send()
        for cp in arrivals:
            cp.wait_recv()
        for cp in local:
            cp.wait()

    outs = pl.pallas_call(
        body, name=name,
        out_shape=(*[pltpu.HBM(a.shape, a.dtype) for a in srcs], *[pltpu.HBM(l.shape, l.dtype) for l in lands]),
        in_specs=[hbm] * (2 * n) + [sem] * 3 + [pl.BlockSpec(memory_space=pl.ANY)] * len(after), out_specs=[hbm] * (2 * n),
        input_output_aliases={i: i for i in range(2 * n)},
        compiler_params=pltpu.CompilerParams(has_side_effects=pltpu.SideEffectType.DATAFLOW_SIDE_EFFECTING),
    )(*srcs, *lands, *sems, *after)
    return outs[n:]


def _sum_slots(name, slots, tm=256):
    n_slots, r, c = slots.shape
    tm = _tile(r, tm)

    def body(*refs):
        acc = refs[0][...]
        for s_ref in refs[1:n_slots]:
            acc = acc + s_ref[...]
        refs[n_slots][...] = acc

    specs = [pl.BlockSpec((None, tm, c), functools.partial(lambda i, s: (s, i, 0), s=s)) for s in range(n_slots)]
    return pl.pallas_call(
        body, name=name, grid=(r // tm,), in_specs=specs, out_specs=pl.BlockSpec((tm, c), lambda i: (i, 0)),
        out_shape=jax.ShapeDtypeStruct((r, c), F32), compiler_params=_params("parallel"),
    )(*[slots] * n_slots)


def _swap_with_sibling(name, arrays):
    n = len(arrays)
    hbm = pl.BlockSpec(memory_space=pl.ANY)

    def body(*refs):
        ins, outs = refs[:n], refs[n:2 * n]
        send_sems, recv_sems = refs[2 * n:]
        sibling = (lax.axis_index("x"), lax.axis_index("y"), 1 - lax.axis_index("c"))
        copies = [pltpu.make_async_remote_copy(src_ref=ins[w], dst_ref=outs[w], send_sem=send_sems.at[w],
                                               recv_sem=recv_sems.at[w], device_id=sibling, device_id_type=MESH)
                  for w in range(n)]
        for cp in copies:
            cp.start()
        for cp in copies:
            cp.wait()

    return pl.pallas_call(
        body, name=name, in_specs=[hbm] * n, out_specs=[hbm] * n,
        out_shape=[jax.ShapeDtypeStruct(a.shape, a.dtype) for a in arrays],
        scratch_shapes=[pltpu.SemaphoreType.DMA((n,)), pltpu.SemaphoreType.DMA((n,))],
    )(*arrays)


def _adamw(g, w, m, v):
    m = ADAM_B1 * m + (1.0 - ADAM_B1) * g
    v = ADAM_B2 * v + (1.0 - ADAM_B2) * jnp.square(g)
    m_hat = m / (1.0 - ADAM_B1 ** ADAM_STEP)
    v_hat = v / (1.0 - ADAM_B2 ** ADAM_STEP)
    delta = -ADAM_LR * (m_hat / (jnp.sqrt(v_hat) + ADAM_EPS) + ADAM_WD * w)
    return delta, m, v


def _adamw_small(grads, ws, ms, vs):
    n = len(ws)

    def whole(a):
        return pl.BlockSpec(a.shape, functools.partial(lambda i, nd: (0,) * nd, nd=a.ndim))

    def body(*refs):
        for i in range(n):
            g, w, m, v = (refs[k * n + i][...] for k in range(4))
            for k, val in enumerate(_adamw(g, w, m, v)):
                refs[(4 + k) * n + i][...] = val

    outs = pl.pallas_call(
        body, name="adamw_small", grid=(1,), in_specs=[whole(a) for a in (*grads, *ws, *ms, *vs)],
        out_specs=[whole(w) for _ in range(3) for w in ws],
        out_shape=[jax.ShapeDtypeStruct(w.shape, F32) for _ in range(3) for w in ws],
        compiler_params=pltpu.CompilerParams(vmem_limit_bytes=VMEM_LIMIT),
    )(*grads, *ws, *ms, *vs)
    return outs[:n], outs[n:2 * n], outs[2 * n:]


def kernel(x, norm1_g, w_in, q_norm_g, k_norm_g, ssm_lambda_re, ssm_lambda_im, ssm_log_dt, ssm_b_re, ssm_b_im, ssm_c_re, ssm_c_im, ssm_d, w_glu, b_glu, attn_out_g, ssm_out_g, w_out, norm2_g, w_mlp_in, w_mlp_out, loss_target, m_norm1_g, m_w_in, m_q_norm_g, m_k_norm_g, m_ssm_lambda_re, m_ssm_lambda_im, m_ssm_log_dt, m_ssm_b_re, m_ssm_b_im, m_ssm_c_re, m_ssm_c_im, m_ssm_d, m_w_glu, m_b_glu, m_attn_out_g, m_ssm_out_g, m_w_out, m_norm2_g, m_w_mlp_in, m_w_mlp_out, v_norm1_g, v_w_in, v_q_norm_g, v_k_norm_g, v_ssm_lambda_re, v_ssm_lambda_im, v_ssm_log_dt, v_ssm_b_re, v_ssm_b_im, v_ssm_c_re, v_ssm_c_im, v_ssm_d, v_w_glu, v_b_glu, v_attn_out_g, v_ssm_out_g, v_w_out, v_norm2_g, v_w_mlp_in, v_w_mlp_out):
    batch, seq, d_model = x.shape
    tokens = batch * seq
    sb_width = w_in.shape[1]
    n_features = d_model

    big = [("w_in", w_in, m_w_in, v_w_in, 1), ("w_glu", w_glu, m_w_glu, v_w_glu, 0),
           ("w_out", w_out, m_w_out, v_w_out, 0), ("w_mlp_in", w_mlp_in, m_w_mlp_in, v_w_mlp_in, 1),
           ("w_mlp_out", w_mlp_out, m_w_mlp_out, v_w_mlp_out, 0)]
    small = [("norm1_g", norm1_g, m_norm1_g, v_norm1_g), ("q_norm_g", q_norm_g, m_q_norm_g, v_q_norm_g),
             ("k_norm_g", k_norm_g, m_k_norm_g, v_k_norm_g),
             ("ssm_lambda_re", ssm_lambda_re, m_ssm_lambda_re, v_ssm_lambda_re),
             ("ssm_lambda_im", ssm_lambda_im, m_ssm_lambda_im, v_ssm_lambda_im),
             ("ssm_log_dt", ssm_log_dt, m_ssm_log_dt, v_ssm_log_dt),
             ("ssm_b_re", ssm_b_re, m_ssm_b_re, v_ssm_b_re), ("ssm_b_im", ssm_b_im, m_ssm_b_im, v_ssm_b_im),
             ("ssm_c_re", ssm_c_re, m_ssm_c_re, v_ssm_c_re), ("ssm_c_im", ssm_c_im, m_ssm_c_im, v_ssm_c_im),
             ("ssm_d", ssm_d, m_ssm_d, v_ssm_d), ("b_glu", b_glu, m_b_glu, v_b_glu),
             ("attn_out_g", attn_out_g, m_attn_out_g, v_attn_out_g), ("ssm_out_g", ssm_out_g, m_ssm_out_g, v_ssm_out_g),
             ("norm2_g", norm2_g, m_norm2_g, v_norm2_g)]

    gather_in, tok_in = _chip_exchange_start("gather_w_in_start", "gather", [(w_in.astype(BF16), 1)])
    gather_mix, tok_mix_w = _chip_exchange_start(
        "gather_mix_start", "gather", [(w.astype(BF16), axis) for _, w, _, _, axis in big[1:3]], after=tok_in)
    gather_mlp, tok_rest = _chip_exchange_start(
        "gather_mlp_start", "gather", [(w.astype(BF16), axis) for _, w, _, _, axis in big[3:]], after=tok_mix_w)

    x2 = x.reshape(tokens, d_model)
    tgt2 = loss_target.reshape(tokens, d_model)
    g1, g2 = norm1_g[None, :], norm2_g[None, :]
    g_attn, g_ssm, bias_glu = attn_out_g[None, :], ssm_out_g[None, :], b_glu[None, :]
    heads = sb_width // HEAD_DIM
    qk_scale = 1.0 / math.sqrt(HEAD_DIM)
    gq, gk = (jnp.tile(q_norm_g, heads) * qk_scale)[None, :], jnp.tile(k_norm_g, heads)[None, :]
    lane_head = jnp.arange(LANES) // HEAD_DIM
    ones_blocks = (lane_head[:, None] == lane_head[None, :]).astype(F32)

    (xn,) = _rowwise("norm1", _rms, [x2], [g1 + tok_rest], [(d_model, BF16)], tm=512)
    s5_in = _s5_operator_inputs(ssm_lambda_re, ssm_lambda_im, ssm_log_dt, ssm_b_re, ssm_b_im, ssm_c_re, ssm_c_im, ssm_d)
    kt_row, b_mat, c_mat, la, lb = _s5_operators_call("s5_operators", s5_in)
    (wf_in,) = _chip_exchange_wait("gather_w_in_wait", gather_in, [xn, b_mat, c_mat])
    (proj,) = _mm("proj_in", xn, wf_in, "nn")

    def qkv_fn(q, k, v, gq_, gk_, ones):
        return _head_rms(q, gq_, ones), _head_rms(k, gk_, ones), v

    qn, kn, vb = _rowwise("qk_norm", qkv_fn, [(proj, sb_width, 0), (proj, sb_width, 1), (proj, sb_width, 2)],
                          [gq, gk, ones_blocks], [(sb_width, BF16)] * 3, tm=512)
    sb, c_tot = _attn_fwd(qn, kn, vb, batch=batch, seq=seq, bq=ATTN_BQ, bk=ATTN_BK)
    ug = _tokens_to_groups("u_to_groups", proj, 3, sb_width)
    yg, xin = _s5_fwd(ug, kt_row, b_mat, c_mat, la, lb, batch=batch)
    y_ssm = _groups_to_tokens("y_to_tokens", yg)

    wf_glu, wf_out = _chip_exchange_wait("gather_mix_wait", gather_mix, [y_ssm, sb])
    (gate_pre,) = _mm("glu_gate", y_ssm, wf_glu, "nn", a_fn=_gelu, extras=[(bias_glu, "row")],
                      epilogue=lambda acc, b: acc + b)
    (mixed,) = _rowwise("mix_norm", _mixed, [sb, y_ssm, gate_pre], [g_attn, g_ssm], [(2 * sb_width, BF16)], tm=512)
    def out_head(acc, r, g):
        h = acc + r
        return h, _rms(h, g)

    h1, hn = _mm("proj_out", mixed, wf_out, "nn", extras=[(x2, "tile"), (g2, "row")], epilogue=out_head,
                 out_dtypes=(F32, BF16), full_rows=True)
    def mlp_act(acc):
        r = jnp.maximum(acc, 0.0)
        return r * r, r

    wf_mlp_in, wf_mlp_out = _chip_exchange_wait("gather_mlp_wait", gather_mlp, [h1, hn])
    act, act_root = _mm("mlp_in", hn, wf_mlp_in, "nn", epilogue=mlp_act, out_dtypes=(BF16, BF16))
    inv_n = 1.0 / n_features

    def loss_head(acc, r, t):
        d = ((acc + r) - t) * inv_n
        return d, d, jnp.sum(d * d, keepdims=True) * (0.5 * n_features)

    dy, dy_b, loss_tiles = _mm("mlp_out_loss", act, wf_mlp_out, "nn", extras=[(h1, "tile"), (tgt2, "tile")],
                               epilogue=loss_head, out_dtypes=(F32, BF16), tile_sums=("scalar",))
    loss_part = jnp.sum(loss_tiles)

    (dw_mlp_out,) = _mm("dw_mlp_out", act, dy_b, "tn")
    (dpre,) = _mm("d_mlp_act", dy_b, wf_mlp_out, "nt", extras=[(act_root, "tile")],
                  epilogue=lambda acc, r: acc * (2.0 * r.astype(F32)), out_dtypes=(BF16,))
    (dw_mlp_in,) = _mm("dw_mlp_in", hn, dpre, "tn")
    scatter_mlp, tok_mlp = _chip_exchange_start("scatter_mlp_start", "scatter", [(dw_mlp_in, 1), (dw_mlp_out, 0)])
    def norm_bwd(dn, res, hx, g):
        _, vjp = jax.vjp(_rms, hx, g)
        dh, dg = vjp(dn)
        return res + dh, dg

    dh1, dg_tiles = _mm("d_norm2_in", dpre, wf_mlp_in, "nt", extras=[(dy, "tile"), (h1, "tile"), (g2 + tok_mlp, "row")],
                        epilogue=norm_bwd, tile_sums=("row",), full_rows=True)
    dg_norm2 = jnp.sum(dg_tiles, axis=0, keepdims=True)
    (dw_out,) = _mm("dw_out", mixed, dh1, "tn")

    def mixed_bwd(dm, sb_, ys, gp, ga, gs):
        _, vjp = jax.vjp(lambda a, act, b, c, d: jnp.concatenate(
            [_rms(a, c), _rms(act * jax.nn.sigmoid(b), d)], axis=-1), sb_, _gelu(ys), gp, ga, gs)
        dsb_, dact, dgp_, dga, dgs = vjp(dm)
        return dsb_, dgp_, dact, dga, dgs, jnp.sum(dgp_, axis=0, keepdims=True)

    dsb, dgate_pre, dact_part, *gain_tiles = _mm(
        "d_mixed", dh1, wf_out, "nt", epilogue=mixed_bwd, full_rows=True, max_tm=256,
        extras=[(sb, "tile"), (y_ssm, "tile"), (gate_pre, "tile"), (g_attn, "row"), (g_ssm, "row")],
        out_dtypes=((F32, sb_width), (BF16, sb_width), (F32, sb_width)), tile_sums=(("row", sb_width),) * 3)
    dg_attn, dg_ssm, db_glu = (jnp.sum(t, axis=0, keepdims=True) for t in gain_tiles)

    def gelu_bwd(acc, part, ys):
        _, vjp = jax.vjp(_gelu, ys)
        return vjp(acc + part)[0]

    (dy_ssm,) = _mm("d_glu_in", dgate_pre, wf_glu, "nt", extras=[(dact_part, "tile"), (y_ssm, "tile")], epilogue=gelu_bwd)
    (dw_glu,) = _mm("dw_glu", y_ssm, dgate_pre, "tn", a_fn=_gelu)
    scatter_mix, tok_mix = _chip_exchange_start("scatter_mix_start", "scatter", [(dw_glu, 0), (dw_out, 0)])

    dug, dkt_row, db_mat, dc_mat, dla, dlb = _s5_bwd(ug, _tokens_to_groups("dy_to_groups", dy_ssm, 0, sb_width), xin,
                                                     kt_row, b_mat, c_mat, la, lb + tok_mix, batch=batch)
    du = _groups_to_tokens("du_to_tokens", dug)
    d_in = _s5_operators_call("s5_operators_bwd", s5_in, (dkt_row, db_mat, dc_mat, dla, dlb))
    ds5 = [d_in[0][:, 0, :] + d_in[2][:, :, 0], d_in[1][:, 0, :] + d_in[3][:, :, 0], d_in[4][:, 0, 0],
           d_in[5].transpose(0, 2, 1), d_in[6].transpose(0, 2, 1), d_in[7].transpose(0, 2, 1), d_in[8].transpose(0, 2, 1),
           d_in[9][:, 0, :]]

    def pack(parts):
        flat = jnp.concatenate([p.reshape(-1) for p in parts])
        rows = -(-flat.shape[0] // (8 * LANES)) * 8
        return jnp.pad(flat, (0, rows * LANES - flat.shape[0])).reshape(rows, LANES)

    def unpack(packed, names):
        flat, out, off = packed.reshape(-1), {}, 0
        for name in names:
            shape = small_shapes[name]
            size = math.prod(shape)
            out[name] = flat[off:off + size].reshape(shape)
            off += size
        return out, flat[off]

    small_shapes = {name: w.shape for name, w, _, _ in small}
    early_names = ["ssm_lambda_re", "ssm_lambda_im", "ssm_log_dt", "ssm_b_re", "ssm_b_im", "ssm_c_re", "ssm_c_im", "ssm_d",
                   "b_glu", "attn_out_g", "ssm_out_g", "norm2_g"]
    late_names = ["norm1_g", "q_norm_g", "k_norm_g"]
    early = pack([*ds5, db_glu[0], dg_attn[0], dg_ssm[0], dg_norm2[0], loss_part])
    early_exchange, _ = _chip_exchange_start("small_early_start", "all", [(early, 0)])

    dqn, dkn, dv = _attn_bwd(qn, kn, vb, c_tot, dsb, batch=batch, seq=seq, bq=ATTN_BQ, bk=ATTN_BK,
                             after=early_exchange[3][0])
    (early_slots,) = _chip_exchange_wait("small_early_wait", early_exchange, dqn)
    small_g, loss = unpack(_sum_slots("sum_small_early", early_slots), early_names)

    def qk_bwd(q, k, dq_, dk_, dv_, du_, gq_, gk_, ones):
        _, vjp_q = jax.vjp(lambda a, g: _head_rms(a, g, ones), q, gq_)
        _, vjp_k = jax.vjp(lambda a, g: _head_rms(a, g, ones), k, gk_)
        dq, dgq = vjp_q(dq_)
        dk, dgk = vjp_k(dk_)
        return jnp.concatenate([dq, dk, dv_, du_], axis=1), dgq, dgk

    dproj, dgq, dgk = _rowwise("qk_norm_bwd", qk_bwd, [(proj, sb_width, 0), (proj, sb_width, 1), dqn, dkn, dv, du],
                               [gq, gk, ones_blocks], [(4 * sb_width, BF16)], [(1, sb_width)] * 2, tm=512)
    (dw_in,) = _mm("dw_in", xn, dproj, "tn")
    scatter_in, tok_w_in = _chip_exchange_start("scatter_in_start", "scatter", [(dw_in, 1)])
    dx, dg_tiles = _mm("d_norm1_in", dproj, wf_in, "nt", extras=[(dh1, "tile"), (x2, "tile"), (g1 + tok_w_in, "row")],
                       epilogue=norm_bwd, tile_sums=("row",), full_rows=True)
    dg_norm1 = jnp.sum(dg_tiles, axis=0, keepdims=True)

    late = pack([dg_norm1[0], dgq.reshape(heads, HEAD_DIM).sum(0) * qk_scale, dgk.reshape(heads, HEAD_DIM).sum(0),
                 jnp.zeros((1,), F32)])
    late_exchange, _ = _chip_exchange_start("small_late_start", "all", [(late, 0)])

    def adam_big(sa, sb_, w, m, v):
        g = sa + sb_
        delta, m, v = _adamw(g, w, m, v)
        return g, delta, m, v

    def reduce_and_update(tag, params, slots):
        mine = [_sum_slots("sum_" + name, s) for s, (name, *_rest) in zip(slots, params)]
        theirs = _swap_with_sibling("swap_" + tag, mine)
        return {name: _rowwise("adamw_" + name, adam_big, [sa, sb_, w, m, v], [], [(w.shape[1], F32)] * 4)
                for (name, w, m, v, _), sa, sb_ in zip(params, mine, theirs)}

    started = late_exchange[3][0]
    slots_mlp_in, slots_mlp_out = _chip_exchange_wait("scatter_mlp_wait", scatter_mlp, started)
    slots_glu, slots_out = _chip_exchange_wait("scatter_mix_wait", scatter_mix, started)
    big_out = reduce_and_update("rest", big[1:], [slots_glu, slots_out, slots_mlp_in, slots_mlp_out])

    (late_slots,) = _chip_exchange_wait("small_late_wait", late_exchange, big_out["w_mlp_out"][3])
    reduced = _sum_slots("sum_small_late", late_slots)
    small_g.update(unpack(reduced, late_names)[0])
    narrow = {name for name, w, _, _ in small if w.ndim == 3 and w.shape[2] < w.shape[1]}

    def flip(a, name):
        return jnp.swapaxes(a, 1, 2) if name in narrow else a

    small_upd = _adamw_small([flip(small_g[name], name) for name, *_ in small], [flip(w, name) for name, w, _, _ in small],
                             [flip(m, name) for name, _, m, _ in small], [flip(v, name) for name, _, _, v in small])
    small_out = [small_g] + [{name: flip(small_upd[kind][i], name) for i, (name, *_) in enumerate(small)}
                             for kind in range(3)]

    (slots_in,) = _chip_exchange_wait("scatter_in_wait", scatter_in, reduced)
    big_out.update(reduce_and_update("w_in", big[:1], [slots_in]))
    names = ["norm1_g", "w_in", "q_norm_g", "k_norm_g", "ssm_lambda_re", "ssm_lambda_im", "ssm_log_dt", "ssm_b_re",
             "ssm_b_im", "ssm_c_re", "ssm_c_im", "ssm_d", "w_glu", "b_glu", "attn_out_g", "ssm_out_g", "w_out",
             "norm2_g", "w_mlp_in", "w_mlp_out"]
    outs = [loss, dx.reshape(batch, seq, d_model)]
    for kind in range(4):
        for name in names:
            outs.append(big_out[name][kind] if name in big_out else small_out[kind][name])
    return tuple(outs)
```

```python
import functools
import math

import jax
import jax.numpy as jnp
from jax import lax
from jax.experimental import pallas as pl
from jax.experimental.pallas import tpu as pltpu

F32 = jnp.float32
BF16 = jnp.bfloat16
F32_DOT = lax.Precision.HIGH
MESH = pl.DeviceIdType.MESH

RMS_EPS = 1e-6
HEAD_DIM = 64
SSM_GROUP = 16
SSM_CHUNK = 16
LANES = 128
N_CHIPS = 4
N_DEV = 8
VMEM_LIMIT = 48 * 1024 * 1024

ADAM_LR = 0.001
ADAM_B1 = 0.9
ADAM_B2 = 0.999
ADAM_EPS = 1e-08
ADAM_WD = 0.01
ADAM_STEP = 10


def _tile(n, pref):
    t = min(n, pref)
    while n % t:
        t //= 2
    return t


def _params(*sem):
    return pltpu.CompilerParams(dimension_semantics=sem, vmem_limit_bytes=VMEM_LIMIT)


_DIMS = {"nn": (((1,), (0,)), ((), ())), "nt": (((1,), (1,)), ((), ())), "tn": (((0,), (0,)), ((), ()))}


MM_VMEM_BUDGET = 40 * 1024 * 1024


def _mm_tiles(m, n, k, a_bytes, b_bytes, tile_bytes, full_rows=False, max_tm=1024):
    best = None
    for tk in [t for t in (k, k // 2, k // 4, k // 8) if t >= 256 or t == k]:
        for tm in [t for t in (1024, 512, 256, 128) if t <= min(m, max_tm) and m % t == 0]:
            for tn in [n] if full_rows else [t for t in (1024, 512, 256, 128) if t <= n and n % t == 0]:
                need = 2 * (tm * tk * a_bytes + tk * tn * b_bytes) + 2 * tm * tn * tile_bytes + (tm * tn * 4 if tk < k else 0)
                if need > MM_VMEM_BUDGET:
                    continue
                traffic = m * k * a_bytes * (1 if tk == k else n // tn) + k * n * b_bytes * (1 if n == tn and tk == k else m // tm)
                key = (tk < k, traffic, -tm * tn)
                if best is None or key < best[0]:
                    best = (key, (tm, tn, tk))
    return best[1]


def _mm(name, a, b, mode, *, a_fn=None, extras=(), epilogue=None, out_dtypes=(F32,), tile_sums=(), full_rows=False,
        max_tm=1024):
    if mode == "nn":
        (m, k), n = a.shape, b.shape[1]
    elif mode == "nt":
        (m, k), n = a.shape, b.shape[0]
    else:
        (k, m), n = a.shape, b.shape[1]
    outs_spec = [(d, n) if not isinstance(d, tuple) else d for d in out_dtypes]
    sums_spec = [(s, n) if not isinstance(s, tuple) else s for s in tile_sums]
    assert full_rows or all(w == n for _, w in outs_spec + sums_spec) and all(e.shape[1] == n for e, _ in extras)
    tile_bytes = (sum(e.dtype.itemsize * e.shape[1] for e, kind in extras if kind == "tile")
                  + sum(jnp.dtype(d).itemsize * w for d, w in outs_spec)) // n + 1
    tm, tn, tk = _mm_tiles(m, n, k, a.dtype.itemsize, b.dtype.itemsize, tile_bytes, full_rows, max_tm)
    nk = k // tk
    ne, nout = len(extras), len(out_dtypes)
    dims = _DIMS[mode]

    def width_spec(rows, w):
        if w == n:
            return pl.BlockSpec((rows, tn), (lambda i, j, kk: (i, j)) if rows != 1 else (lambda i, j, kk: (0, j)))
        return pl.BlockSpec((rows, w), (lambda i, j, kk: (i, 0)) if rows != 1 else (lambda i, j, kk: (0, 0)))

    def body(a_ref, b_ref, *rest):
        ex, outs, sums = rest[:ne], rest[ne:ne + nout], rest[ne + nout:ne + nout + len(tile_sums)]
        at = a_ref[...]
        if a_fn is not None:
            at = a_fn(at)
        p = lax.dot_general(at.astype(BF16), b_ref[...].astype(BF16), dims, preferred_element_type=F32)

        def finish(r):
            if epilogue is not None:
                r = epilogue(r, *[e[...] for e in ex])
            if not isinstance(r, (tuple, list)):
                r = (r,)
            for o, v in zip(outs, r[:nout]):
                o[...] = v.astype(o.dtype)
            for o, v, (kind, _) in zip(sums, r[nout:], sums_spec):
                first = lax.broadcasted_iota(jnp.int32, o.shape, 0) == 0
                if kind == "scalar":
                    first &= lax.broadcasted_iota(jnp.int32, o.shape, 1) == 0
                o[...] = jnp.where(first, v, 0.0)

        if nk == 1:
            finish(p)
        else:
            acc = rest[ne + nout + len(tile_sums)]
            kk = pl.program_id(2)

            @pl.when(kk == 0)
            def _():
                acc[...] = p

            @pl.when(kk > 0)
            def _():
                acc[...] += p

            @pl.when(kk == nk - 1)
            def _():
                finish(acc[...])

    if mode == "tn":
        a_spec = pl.BlockSpec((tk, tm), lambda i, j, kk: (kk, i))
    else:
        a_spec = pl.BlockSpec((tm, tk), lambda i, j, kk: (i, kk))
    if mode == "nt":
        b_spec = pl.BlockSpec((tn, tk), lambda i, j, kk: (j, kk))
    else:
        b_spec = pl.BlockSpec((tk, tn), lambda i, j, kk: (kk, j))
    ex_specs = [pl.BlockSpec(e.shape, lambda i, j, kk: (0, 0)) if kind == "whole"
                else width_spec(tm if kind == "tile" else 1, e.shape[1]) for e, kind in extras]
    return pl.pallas_call(
        body, name=name, grid=(m // tm, n // tn, nk),
        in_specs=[a_spec, b_spec] + ex_specs,
        out_specs=([width_spec(tm, w) for _, w in outs_spec]
                   + [pl.BlockSpec((8, LANES), lambda i, j, kk: (i, j)) if kind == "scalar" else width_spec(8, w)
                      for kind, w in sums_spec]),
        out_shape=([jax.ShapeDtypeStruct((m, w), dt) for dt, w in outs_spec]
                   + [jax.ShapeDtypeStruct((m // tm * 8, n // tn * LANES if kind == "scalar" else w), F32)
                      for kind, w in sums_spec]),
        scratch_shapes=[pltpu.VMEM((tm, tn), F32)] if nk > 1 else [],
        compiler_params=_params("parallel", "parallel", "arbitrary"),
    )(a, b, *[e for e, _ in extras])


def _rowwise(name, fn, rows, consts, row_outs, acc_outs=(), tm=256):
    norm = [r if isinstance(r, tuple) else (r, r.shape[1], 0) for r in rows]
    t = norm[0][0].shape[0]
    tm = _tile(t, tm)
    nr, nc, no = len(norm), len(consts), len(row_outs)

    def body(*refs):
        outs = fn(*[r[...] for r in refs[:nr + nc]])
        if not isinstance(outs, (tuple, list)):
            outs = (outs,)
        o_refs, a_refs = refs[nr + nc:nr + nc + no], refs[nr + nc + no:]
        for r, v in zip(o_refs, outs[:no]):
            r[...] = v.astype(r.dtype)
        if a_refs:
            i = pl.program_id(0)

            @pl.when(i == 0)
            def _():
                for r, v in zip(a_refs, outs[no:]):
                    r[...] = v

            @pl.when(i > 0)
            def _():
                for r, v in zip(a_refs, outs[no:]):
                    r[...] += v

    in_specs = [pl.BlockSpec((tm, w), functools.partial(lambda i, cb: (i, cb), cb=cb)) for _, w, cb in norm]
    in_specs += [pl.BlockSpec(c.shape, functools.partial(lambda i, nd: (0,) * nd, nd=c.ndim)) for c in consts]
    out_specs = [pl.BlockSpec((tm, w), lambda i: (i, 0)) for w, _ in row_outs]
    out_specs += [pl.BlockSpec(s, functools.partial(lambda i, nd: (0,) * nd, nd=len(s))) for s in acc_outs]
    out_shape = [jax.ShapeDtypeStruct((t, w), dt) for w, dt in row_outs]
    out_shape += [jax.ShapeDtypeStruct(s, F32) for s in acc_outs]
    return pl.pallas_call(
        body, name=name, grid=(t // tm,), in_specs=in_specs, out_specs=out_specs, out_shape=out_shape,
        compiler_params=_params("arbitrary"),
    )(*[r[0] for r in norm], *consts)


def _rms(x, g):
    return x * lax.rsqrt(jnp.mean(x * x, axis=-1, keepdims=True) + RMS_EPS) * g


@jax.custom_vjp
def _head_sums(x, ones_blocks):
    parts = [jnp.dot(x[:, j:j + LANES], ones_blocks, precision=F32_DOT, preferred_element_type=F32)
             for j in range(0, x.shape[1], LANES)]
    return jnp.concatenate(parts, axis=1)


_head_sums.defvjp(lambda x, ones_blocks: (_head_sums(x, ones_blocks), ones_blocks),
                  lambda ones_blocks, ct: (_head_sums(ct, ones_blocks), None))


def _head_rms(x, g, ones_blocks):
    return x * lax.rsqrt(_head_sums(x * x, ones_blocks) * (1.0 / HEAD_DIM) + RMS_EPS) * g


def _gelu(x):
    return x * (0.5 * (1.0 + jnp.tanh(math.sqrt(2.0 / math.pi) * (x + 0.044715 * (x * x * x)))))


def _mixed(sb, y_ssm, gate_pre, g_attn, g_ssm):
    ssm = _gelu(y_ssm) * jax.nn.sigmoid(gate_pre)
    return jnp.concatenate([_rms(sb, g_attn), _rms(ssm, g_ssm)], axis=-1)


def _softplus(z):
    return jnp.maximum(z, 0.0) + jnp.log(1.0 + jnp.exp(-jnp.abs(z)))


def _running_sums(x, tri):
    return jnp.dot(x.astype(BF16), tri, preferred_element_type=F32)


def _dot_nt(a, b, **kw):
    return lax.dot_general(a, b, _DIMS["nt"], preferred_element_type=F32, **kw)


def _dot_tn(a, b, **kw):
    return lax.dot_general(a, b, _DIMS["tn"], preferred_element_type=F32, **kw)


ATTN_BQ, ATTN_BK = 2048, 256
HEAD_LANES = tuple(slice(h * HEAD_DIM, (h + 1) * HEAD_DIM) for h in range(LANES // HEAD_DIM))


def _attn_fwd(qs, kn, v, *, batch, seq, bq, bk):
    width = qs.shape[1]
    bq = _tile(seq, bq)
    bk = _tile(bq, bk)
    nq, kpq = seq // bq, bq // bk

    def body(q_ref, k_ref, v_ref, o_ref, c_ref):
        row = lax.broadcasted_iota(jnp.int32, (bq, bk), 0)
        col = lax.broadcasted_iota(jnp.int32, (bq, bk), 1)
        tri = (lax.broadcasted_iota(jnp.int32, (bk, bk), 0) >= lax.broadcasted_iota(jnp.int32, (bk, bk), 1)).astype(BF16)

        def q_block(qi, carry):
            r0 = pl.multiple_of(qi * bq, bq)
            qh = [q_ref[pl.ds(r0, bq), ln] for ln in HEAD_LANES]

            def tile(k0, state, top=0):
                diag = top is not None
                top = top or 0
                msk = (col < row)[:bq - top] if diag else None
                new = []
                for h, ln in enumerate(HEAD_LANES):
                    o, c = state[2 * h], state[2 * h + 1]
                    z = _dot_nt(qh[h][top:], k_ref[pl.ds(k0, bk), ln])
                    sp = _softplus(z)
                    if diag:
                        sp = jnp.where(msk, sp, 0.0)
                    r = _running_sums(sp, tri)
                    a = jnp.exp(z - r - c[top:])
                    if diag:
                        a = jnp.where(msk, a, 0.0)
                    o_new = o[top:] + jnp.dot(a.astype(BF16), v_ref[pl.ds(k0, bk), ln], preferred_element_type=F32)
                    c_new = c[top:] + r[:, 0:1]
                    if top:
                        o_new, c_new = jnp.concatenate([o[:top], o_new]), jnp.concatenate([c[:top], c_new])
                    new += [o_new, c_new]
                return tuple(new)

            state = (jnp.zeros((bq, HEAD_DIM), F32), jnp.zeros((bq, 1), F32)) * len(HEAD_LANES)
            for d in reversed(range(kpq)):
                state = tile(pl.multiple_of(r0 + d * bk, bk), state, top=d * bk)
            state = lax.fori_loop(0, qi * kpq, lambda it, st: tile(pl.multiple_of(r0 - (it + 1) * bk, bk), st, None),
                                  state)
            for h, ln in enumerate(HEAD_LANES):
                o_ref[pl.ds(r0, bq), ln] = state[2 * h]
                c_ref[pl.ds(r0, bq), ln] = jnp.broadcast_to(state[2 * h + 1], (bq, HEAD_DIM))
            return carry

        lax.fori_loop(0, nq, q_block, 0)

    spec = pl.BlockSpec((seq, LANES), lambda b, h: (b, h))
    shape = jax.ShapeDtypeStruct((batch * seq, width), F32)
    return pl.pallas_call(
        body, name="attn_fwd", grid=(batch, width // LANES), in_specs=[spec, spec, spec], out_specs=[spec, spec],
        out_shape=[shape, shape], compiler_params=_params("parallel", "parallel"),
    )(qs, kn, v)


def _attn_bwd(qs, kn, v, c_tot, do, *, batch, seq, bq, bk, after):
    width = qs.shape[1]
    bq = _tile(seq, bq)
    bk = _tile(bq, bk)
    nq, kpq = seq // bq, bq // bk

    def body(q_ref, k_ref, v_ref, c_ref, do_ref, after_ref, dq_ref, dk_ref, dv_ref):
        row = lax.broadcasted_iota(jnp.int32, (bq, bk), 0)
        col = lax.broadcasted_iota(jnp.int32, (bq, bk), 1)
        sq_row = lax.broadcasted_iota(jnp.int32, (bk, bk), 0)
        sq_col = lax.broadcasted_iota(jnp.int32, (bk, bk), 1)
        tri = (sq_row >= sq_col).astype(BF16)
        tri_t = (sq_row <= sq_col).astype(BF16)
        dk_ref[...] = jnp.zeros_like(dk_ref)
        dv_ref[...] = jnp.zeros_like(dv_ref)

        def q_block(qi, carry):
            r0 = pl.multiple_of(qi * bq, bq)
            qh = [q_ref[pl.ds(r0, bq), ln] for ln in HEAD_LANES]
            d_out = [do_ref[pl.ds(r0, bq), ln].astype(BF16) for ln in HEAD_LANES]
            c_all = [c_ref[pl.ds(r0, bq), ln][:, 0:1] for ln in HEAD_LANES]

            def tile(k0, state, top=0):
                diag = top is not None
                top = top or 0
                last = diag and top == bq - bk
                msk = (col < row)[:bq - top] if diag else None
                new = []
                for h, ln in enumerate(HEAD_LANES):
                    c_left, g_left, dq = state[3 * h:3 * h + 3]
                    q, d_o = qh[h][top:], d_out[h][top:]
                    k = k_ref[pl.ds(k0, bk), ln]
                    z = _dot_nt(q, k)
                    e = jnp.exp(-jnp.abs(z))
                    sp = jnp.maximum(z, 0.0) + jnp.log(1.0 + e)
                    sig = jnp.exp(z - sp)
                    if diag:
                        sp = jnp.where(msk, sp, 0.0)
                    r = _running_sums(sp, tri)
                    c_new = c_left[top:] + r[:, 0:1]
                    a = jnp.exp(z - r - (0.0 if last else c_all[h][top:] - c_new))
                    if diag:
                        a = jnp.where(msk, a, 0.0)
                    g = a * _dot_nt(d_o, v_ref[pl.ds(k0, bk), ln])
                    pg = _running_sums(g, tri_t)
                    dz = g - sig * (g_left[top:] + pg)
                    if diag:
                        dz = jnp.where(msk, dz, 0.0)
                    dz = dz.astype(BF16)
                    dk_ref[pl.ds(k0, bk), ln] += _dot_tn(dz, q)
                    dv_ref[pl.ds(k0, bk), ln] += _dot_tn(a.astype(BF16), d_o)
                    g_new = g_left[top:] + pg[:, bk - 1:bk]
                    dq_new = dq[top:] + jnp.dot(dz, k, preferred_element_type=F32)
                    if top:
                        c_new = jnp.concatenate([c_left[:top], c_new])
                        g_new = jnp.concatenate([g_left[:top], g_new])
                        dq_new = jnp.concatenate([dq[:top], dq_new])
                    new += [c_new, g_new, dq_new]
                return tuple(new)

            zero = jnp.zeros((bq, 1), F32)
            init = (zero, zero, jnp.zeros((bq, HEAD_DIM), F32)) * len(HEAD_LANES)
            state = lax.fori_loop(0, qi * kpq, lambda it, st: tile(pl.multiple_of(it * bk, bk), st, None), init)
            for d in range(kpq):
                state = tile(pl.multiple_of(r0 + d * bk, bk), state, top=d * bk)
            for h, ln in enumerate(HEAD_LANES):
                dq_ref[pl.ds(r0, bq), ln] = state[3 * h + 2]
            return carry

        lax.fori_loop(0, nq, q_block, 0)

    spec = pl.BlockSpec((seq, LANES), lambda b, h: (b, h))
    shape = jax.ShapeDtypeStruct((batch * seq, width), F32)
    return pl.pallas_call(
        body, name="attn_bwd", grid=(batch, width // LANES),
        in_specs=[spec] * 5 + [pl.BlockSpec(memory_space=pl.ANY)], out_specs=[spec] * 3,
        out_shape=[shape] * 3, compiler_params=_params("parallel", "parallel"),
    )(qs, kn, v, c_tot, do, after)


def _pattern(rows, cols, hit):
    r, c = lax.broadcasted_iota(jnp.int32, (rows, cols), 0), lax.broadcasted_iota(jnp.int32, (rows, cols), 1)
    return hit(r, c).astype(F32)


def _s5_group_operators(lr_r, li_r, lr_c, li_c, log_dt, bt_re, bt_im, ct_re, ct_im, d_row):
    cs = SSM_CHUNK
    n_ch, n_state = bt_re.shape
    width = cs * n_ch
    dt = jnp.exp(log_dt)

    def spread(x, pattern):
        return jnp.dot(x, pattern, precision=F32_DOT, preferred_element_type=F32)

    twice = _pattern(n_state, 2 * n_state, lambda r, c: r == c % n_state)
    steps = lax.broadcasted_iota(jnp.int32, (cs + 1, 1), 0).astype(F32)
    mag = jnp.exp(steps * (lr_r * dt))
    ang = steps * (li_r * dt)
    pw_re, pw_im = mag * jnp.cos(ang), mag * jnp.sin(ang)
    num_re, num_im = pw_re[1:2] - 1.0, pw_im[1:2]
    den = lr_r * lr_r + li_r * li_r
    cf_re = (num_re * lr_r + num_im * li_r) / den
    cf_im = (num_im * lr_r - num_re * li_r) / den
    bb_re = spread(cf_re * bt_re - cf_im * bt_im, twice)
    bb_im = spread(cf_re * bt_im + cf_im * bt_re, twice)
    pw2_re, pw2_im = spread(pw_re, twice), spread(pw_im, twice)
    real_half = lax.broadcasted_iota(jnp.int32, (1, 2 * n_state), 1) < n_state
    blocks = []
    for s in range(cs):
        pr, pi = pw2_re[cs - 1 - s:cs - s], pw2_im[cs - 1 - s:cs - s]
        blocks.append(jnp.where(real_half, bb_re * pr - bb_im * pi, bb_re * pi + bb_im * pr))
    b_mat = jnp.concatenate(blocks, axis=0)
    la = pw2_re[cs:cs + 1]
    lb = jnp.where(real_half, -pw2_im[cs:cs + 1], pw2_im[cs:cs + 1])

    lane = lax.broadcasted_iota(jnp.int32, (1, width), 1)
    tile_out = _pattern(n_ch, width, lambda r, c: r == c % n_ch)
    c_re, c_im = spread(ct_re, tile_out), spread(ct_im, tile_out)

    k_row = lax.broadcasted_iota(jnp.int32, (1, cs), 1).astype(F32)
    m, a = jnp.exp(k_row * (lr_c * dt)), k_row * (li_c * dt)
    repeat = _pattern(cs, width, lambda r, c: r == c // n_ch)
    p_re, p_im = spread(m * jnp.cos(a), repeat), spread(m * jnp.sin(a), repeat)
    w_re, w_im = p_re * c_re - p_im * c_im, p_re * c_im + p_im * c_re
    skip = jnp.where((lane < n_ch) & (lane == lax.broadcasted_iota(jnp.int32, (n_ch, width), 0)),
                     spread(d_row, tile_out), 0.0)
    kt_row = (jnp.dot(bb_re[:, :n_state], w_re, precision=F32_DOT, preferred_element_type=F32)
              - jnp.dot(bb_im[:, :n_state], w_im, precision=F32_DOT, preferred_element_type=F32) + skip)
    bar_re, bar_im = jnp.exp(lr_c * dt) * jnp.cos(li_c * dt), jnp.exp(lr_c * dt) * jnp.sin(li_c * dt)
    w1_re, w1_im = w_re * bar_re - w_im * bar_im, w_re * bar_im + w_im * bar_re
    c_mat = jnp.concatenate([w1_re, -w1_im], axis=0)
    return kt_row, b_mat, c_mat, la, lb


def _s5_operator_inputs(lam_re, lam_im, log_dt, b_re, b_im, c_re, c_im, d_skip):
    return (lam_re[:, None, :], lam_im[:, None, :], lam_re[:, :, None], lam_im[:, :, None], log_dt[:, None, None],
            b_re.transpose(0, 2, 1), b_im.transpose(0, 2, 1), c_re.transpose(0, 2, 1), c_im.transpose(0, 2, 1),
            d_skip[:, None, :])


def _s5_operators_call(name, args, cotangents=None, gb=8):
    groups = args[0].shape[0]
    gb = _tile(groups, gb)
    n_in = len(args)

    def body(*refs):
        n_ct = 0 if cotangents is None else len(cotangents)
        ins, cts, outs = refs[:n_in], refs[n_in:n_in + n_ct], refs[n_in + n_ct:]
        for g in range(gb):
            vals = [r[g] for r in ins]
            if cotangents is None:
                res = _s5_group_operators(*vals)
            else:
                res = jax.vjp(_s5_group_operators, *vals)[1](tuple(c[g] for c in cts))
            for o, v in zip(outs, res):
                o[g] = v

    def spec(a):
        return pl.BlockSpec((gb, *a.shape[1:]), lambda i: (i, 0, 0))

    if cotangents is None:
        n_ch, n_state = args[5].shape[1:]
        width = SSM_CHUNK * n_ch
        out_shape = [jax.ShapeDtypeStruct((groups, *s), F32) for s in
                     ((n_ch, width), (width, 2 * n_state), (2 * n_state, width), (1, 2 * n_state), (1, 2 * n_state))]
    else:
        out_shape = [jax.ShapeDtypeStruct(a.shape, F32) for a in args]
    operands = [*args, *(cotangents or ())]
    return pl.pallas_call(
        body, name=name, grid=(groups // gb,), in_specs=[spec(a) for a in operands], out_specs=[spec(s) for s in out_shape],
        out_shape=out_shape, compiler_params=_params("parallel"),
    )(*operands)


GROUPS_PER_BLOCK = LANES // SSM_GROUP


def _tokens_to_groups(name, u, col_block, width):
    t = u.shape[0]
    n = t // SSM_CHUNK
    ch = SSM_CHUNK * SSM_GROUP
    blocks = width // LANES
    nb = GROUPS_PER_BLOCK

    def body(u_ref, o_ref):
        block = lax.broadcasted_iota(jnp.int32, (n, LANES), 1) // SSM_GROUP
        for half in range(SSM_CHUNK // nb):
            rows = [u_ref[pl.ds(half * nb + s, n, stride=SSM_CHUNK), :] for s in range(nb)]
            for shift in range(nb):
                merged = rows[shift]
                for b in range(1, nb):
                    merged = jnp.where(block == b, rows[(b + shift) % nb], merged)
                moved = pltpu.roll(merged, shift * SSM_GROUP, 1) if shift else merged
                for b in range(nb):
                    s = (b + shift) % nb
                    o_ref[b, :, half * LANES + s * SSM_GROUP:half * LANES + (s + 1) * SSM_GROUP] = (
                        moved[:, s * SSM_GROUP:(s + 1) * SSM_GROUP])

    return pl.pallas_call(
        body, name=name, grid=(blocks,),
        in_specs=[pl.BlockSpec((t, LANES), lambda j: (0, col_block * blocks + j))],
        out_specs=pl.BlockSpec((GROUPS_PER_BLOCK, n, ch), lambda j: (j, 0, 0)),
        out_shape=jax.ShapeDtypeStruct((width // SSM_GROUP, n, ch), F32), compiler_params=_params("parallel"),
    )(u)


def _groups_to_tokens(name, ug):
    groups, n, ch = ug.shape
    nb = GROUPS_PER_BLOCK

    def body(g_ref, o_ref, rows_ref):
        block = lax.broadcasted_iota(jnp.int32, (n, LANES), 1) // SSM_GROUP
        for half in range(SSM_CHUNK // nb):
            src = [g_ref[b, :, half * LANES:(half + 1) * LANES] for b in range(nb)]
            for shift in range(nb):
                merged = src[-shift % nb]
                for s in range(1, nb):
                    merged = jnp.where(block == s, src[(s - shift) % nb], merged)
                moved = pltpu.roll(merged, (nb - shift) * SSM_GROUP, 1) if shift else merged
                for b in range(nb):
                    rows_ref[(b + shift) % nb, :, b * SSM_GROUP:(b + 1) * SSM_GROUP] = moved[:, b * SSM_GROUP:(b + 1) * SSM_GROUP]
            for s in range(nb):
                o_ref[pl.ds(half * nb + s, n, stride=SSM_CHUNK), :] = rows_ref[s]

    return pl.pallas_call(
        body, name=name, grid=(groups // GROUPS_PER_BLOCK,),
        in_specs=[pl.BlockSpec((GROUPS_PER_BLOCK, n, ch), lambda j: (j, 0, 0))],
        out_specs=pl.BlockSpec((n * SSM_CHUNK, LANES), lambda j: (0, j)),
        out_shape=jax.ShapeDtypeStruct((n * SSM_CHUNK, groups * SSM_GROUP), F32),
        scratch_shapes=[pltpu.VMEM((nb, n, LANES), F32)], compiler_params=_params("parallel"),
    )(ug)


SCAN_ROWS = 8


def _toeplitz_to(tm_ref, g, kt_row):
    width = kt_row.shape[1]
    tm_ref[g] = jnp.zeros((width, width), F32)
    for s in range(SSM_CHUNK):
        tm_ref[g, s * SSM_GROUP:(s + 1) * SSM_GROUP, s * SSM_GROUP:] = kt_row[:, :width - s * SSM_GROUP]


def _lam_powers(la, lb, reverse):
    if reverse:
        lb = -lb

    def mul(p, q):
        return p[0] * q[0] - p[1] * q[1], p[0] * q[1] + p[1] * q[0]

    p1 = (la, lb)
    p2 = mul(p1, p1)
    p3 = mul(p2, p1)
    p4 = mul(p2, p2)
    rows = [p1, p2, p3, p4, mul(p4, p1), mul(p4, p2), mul(p4, p3), mul(p4, p4)]
    if reverse:
        rows = rows[::-1]
    idx = lax.broadcasted_iota(jnp.int32, (SCAN_ROWS, la.shape[1]), 0)
    tab_a = sum(jnp.where(idx == j, r[0], 0.0) for j, r in enumerate(rows))
    tab_b = sum(jnp.where(idx == j, r[1], 0.0) for j, r in enumerate(rows))
    return (p1, p2, p4), (tab_a, tab_b), idx


def _scan_block(e, carry, steps, table, idx, half, reverse):
    n = SCAN_ROWS
    for d, (pa, pb) in zip((1, 2, 4), steps):
        sh = pltpu.roll(e, n - d if reverse else d, 0)
        sh = jnp.where(idx < n - d if reverse else idx >= d, sh, 0.0)
        e = e + pa * sh + pb * pltpu.roll(sh, half, 1)
    tab_a, tab_b = table
    e = e + tab_a * carry + tab_b * pltpu.roll(carry, half, 1)
    shifted = jnp.where(idx == (n - 1 if reverse else 0), carry, pltpu.roll(e, n - 1 if reverse else 1, 0))
    edge = e[0:1] if reverse else e[n - 1:n]
    return shifted, jnp.broadcast_to(edge, e.shape)


def _s5_fwd(ug, kt_row, b_mat, c_mat, la, lb, *, batch, gb=8):
    groups, n, ch = ug.shape
    p2 = b_mat.shape[2]
    gb = _tile(groups, gb)
    nch = n // batch
    nblk = nch // SCAN_ROWS

    def body(u_ref, k_ref, b_ref, c_ref, la_ref, lb_ref, y_ref, x_ref, s_ref, tm_ref):
        for g in range(gb):
            _toeplitz_to(tm_ref, g, k_ref[g])
            s_ref[g] = jnp.dot(u_ref[g], b_ref[g], precision=F32_DOT, preferred_element_type=F32)
        powers = [_lam_powers(la_ref[g], lb_ref[g], False) for g in range(gb)]

        def step(blk, carries):
            new = []
            for g in range(gb):
                steps, table, idx = powers[g]
                for b in range(batch):
                    rows = pl.ds(pl.multiple_of(b * nch + blk * SCAN_ROWS, SCAN_ROWS), SCAN_ROWS)
                    x_in, carry = _scan_block(s_ref[g, rows, :], carries[g * batch + b], steps, table, idx, p2 // 2, False)
                    x_ref[g, rows, :] = x_in
                    new.append(carry)
            return tuple(new)

        lax.fori_loop(0, nblk, step, tuple(jnp.zeros((SCAN_ROWS, p2), F32) for _ in range(gb * batch)))
        for g in range(gb):
            y_ref[g] = (jnp.dot(u_ref[g], tm_ref[g], precision=F32_DOT, preferred_element_type=F32)
                        + jnp.dot(x_ref[g], c_ref[g], precision=F32_DOT, preferred_element_type=F32))

    def spec(a, b):
        return pl.BlockSpec((gb, a, b), lambda i: (i, 0, 0))

    return pl.pallas_call(
        body, name="s5_fwd", grid=(groups // gb,),
        in_specs=[spec(n, ch), spec(SSM_GROUP, ch), spec(ch, p2), spec(p2, ch), spec(1, p2), spec(1, p2)],
        out_specs=[spec(n, ch), spec(n, p2)],
        out_shape=[jax.ShapeDtypeStruct((groups, n, ch), F32), jax.ShapeDtypeStruct((groups, n, p2), F32)],
        scratch_shapes=[pltpu.VMEM((gb, n, p2), F32), pltpu.VMEM((gb, ch, ch), F32)],
        compiler_params=_params("parallel"),
    )(ug, kt_row, b_mat, c_mat, la, lb)


def _s5_bwd(ug, dyg, xin, kt_row, b_mat, c_mat, la, lb, *, batch, gb=8):
    groups, n, ch = ug.shape
    p2 = b_mat.shape[2]
    gb = _tile(groups, gb)
    nch = n // batch
    nblk = nch // SCAN_ROWS

    def body(u_ref, dy_ref, x_ref, k_ref, b_ref, c_ref, la_ref, lb_ref,
             du_ref, dk_ref, db_ref, dc_ref, dla_ref, dlb_ref, dx_ref, ds_ref, tm_ref):
        for g in range(gb):
            _toeplitz_to(tm_ref, g, k_ref[g])
            dx_ref[g] = _dot_nt(dy_ref[g], c_ref[g], precision=F32_DOT)
        powers = [_lam_powers(la_ref[g], lb_ref[g], True) for g in range(gb)]

        def step(it, carries):
            new = []
            for g in range(gb):
                steps, table, idx = powers[g]
                for b in range(batch):
                    rows = pl.ds(pl.multiple_of(b * nch + (nblk - 1 - it) * SCAN_ROWS, SCAN_ROWS), SCAN_ROWS)
                    d_s, carry = _scan_block(dx_ref[g, rows, :], carries[g * batch + b], steps, table, idx, p2 // 2, True)
                    ds_ref[g, rows, :] = d_s
                    new.append(carry)
            return tuple(new)

        lax.fori_loop(0, nblk, step, tuple(jnp.zeros((SCAN_ROWS, p2), F32) for _ in range(gb * batch)))
        for g in range(gb):
            u, dy, ds, x = u_ref[g], dy_ref[g], ds_ref[g], x_ref[g]
            du_ref[g] = _dot_nt(dy, tm_ref[g], precision=F32_DOT) + _dot_nt(ds, b_ref[g], precision=F32_DOT)
            tm_ref[g] = _dot_tn(u, dy, precision=F32_DOT)
            dk_ref[g] = tm_ref[g, 0:SSM_GROUP, :]
            for s in range(1, SSM_CHUNK):
                dk_ref[g, :, :ch - s * SSM_GROUP] += tm_ref[g, s * SSM_GROUP:(s + 1) * SSM_GROUP, s * SSM_GROUP:]
            db_ref[g] = _dot_tn(u, ds, precision=F32_DOT)
            dc_ref[g] = _dot_tn(x, dy, precision=F32_DOT)
            dla_ref[g] = jnp.sum(ds * x, axis=0, keepdims=True)
            dlb_ref[g] = jnp.sum(ds * pltpu.roll(x, p2 // 2, 1), axis=0, keepdims=True)

    def spec(a, b):
        return pl.BlockSpec((gb, a, b), lambda i: (i, 0, 0))

    def shape(a, b):
        return jax.ShapeDtypeStruct((groups, a, b), F32)

    return pl.pallas_call(
        body, name="s5_bwd", grid=(groups // gb,),
        in_specs=[spec(n, ch), spec(n, ch), spec(n, p2), spec(SSM_GROUP, ch), spec(ch, p2), spec(p2, ch), spec(1, p2),
                  spec(1, p2)],
        out_specs=[spec(n, ch), spec(SSM_GROUP, ch), spec(ch, p2), spec(p2, ch), spec(1, p2), spec(1, p2)],
        out_shape=[shape(n, ch), shape(SSM_GROUP, ch), shape(ch, p2), shape(p2, ch), shape(1, p2), shape(1, p2)],
        scratch_shapes=[pltpu.VMEM((gb, n, p2), F32), pltpu.VMEM((gb, n, p2), F32), pltpu.VMEM((gb, ch, ch), F32)],
        compiler_params=_params("parallel"),
    )(ug, dyg, xin, kt_row, b_mat, c_mat, la, lb)


def _block(ref, axis, j, size):
    start = j * size if isinstance(j, int) else pl.multiple_of(j * size, size)
    return ref.at[pl.ds(start, size), :] if axis == 0 else ref.at[:, pl.ds(start, size)]


def _chip_exchange_copies(mode, axes, srcs, lands, send_sems, recv_sems, local_sems):
    x, y, c = lax.axis_index("x"), lax.axis_index("y"), lax.axis_index("c")
    everyone = mode == "all"
    me = 4 * x + 2 * y + c if everyone else 2 * x + y
    n_peers = _exchange_peers(mode)
    local, sends, arrivals = [], [], []
    for w, axis in enumerate(axes):
        if mode == "gather":
            size = srcs[w].shape[axis]
            local.append(pltpu.make_async_copy(srcs[w], _block(lands[w], axis, me, size), local_sems.at[w]))
        elif mode == "scatter":
            size = srcs[w].shape[axis] // N_CHIPS
            local.append(pltpu.make_async_copy(_block(srcs[w], axis, me, size), lands[w].at[me], local_sems.at[w]))
        else:
            local.append(pltpu.make_async_copy(srcs[w], lands[w].at[me], local_sems.at[w]))
        for k in range(1, n_peers + 1):
            bits = k if everyone else 2 * k
            px = 1 - x if bits & 4 else x
            py = 1 - y if bits & 2 else y
            pc = 1 - c if bits & 1 else c
            peer = 4 * px + 2 * py + pc if everyone else 2 * px + py
            if mode == "gather":
                src, dst, arrive = srcs[w], _block(lands[w], axis, me, size), _block(lands[w], axis, peer, size)
            elif mode == "scatter":
                src, dst, arrive = _block(srcs[w], axis, peer, size), lands[w].at[me], lands[w].at[peer]
            else:
                src, dst, arrive = srcs[w], lands[w].at[me], lands[w].at[peer]
            sem = w * n_peers + k - 1
            for target, out in ((dst, sends), (arrive, arrivals)):
                out.append(pltpu.make_async_remote_copy(
                    src_ref=src, dst_ref=target, send_sem=send_sems.at[sem], recv_sem=recv_sems.at[sem],
                    device_id=(px, py, pc), device_id_type=MESH))
    return local, sends, arrivals


def _exchange_peers(mode):
    return N_DEV - 1 if mode == "all" else N_CHIPS - 1


def _chip_exchange_start(name, mode, items, after=None):
    n = len(items)
    n_after = 0 if after is None else 1
    axes = [axis for _, axis in items]
    hbm = pl.BlockSpec(memory_space=pltpu.HBM)
    sem = pl.BlockSpec(memory_space=pltpu.SEMAPHORE)
    lands = []
    for a, axis in items:
        shape = list(a.shape)
        if mode == "gather":
            shape[axis] *= N_CHIPS
        elif mode == "scatter":
            shape[axis] //= N_CHIPS
            shape = [N_CHIPS] + shape
        else:
            shape = [N_DEV] + shape
        lands.append(pltpu.with_memory_space_constraint(lax.empty(tuple(shape), a.dtype), pltpu.HBM))

    def body(*refs):
        srcs, land_refs = refs[:n], refs[n:2 * n]
        send_sems, recv_sems, local_sems = refs[2 * n + n_after:2 * n + n_after + 3]
        token = refs[-1]
        local, sends, _ = _chip_exchange_copies(mode, axes, srcs, land_refs, send_sems, recv_sems, local_sems)
        for cp in local + sends:
            cp.start()
        token[...] = jnp.zeros_like(token)

    n_sem = n * _exchange_peers(mode)
    outs = pl.pallas_call(
        body, name=name,
        out_shape=(pltpu.SemaphoreType.DMA((n_sem,)), pltpu.SemaphoreType.DMA((n_sem,)), pltpu.SemaphoreType.DMA((n,)),
                   *[pltpu.HBM(a.shape, a.dtype) for a, _ in items], *[pltpu.HBM(l.shape, l.dtype) for l in lands],
                   jax.ShapeDtypeStruct((8, LANES), F32)),
        in_specs=[hbm] * (2 * n) + [pl.BlockSpec(memory_space=pl.ANY)] * n_after,
        out_specs=(sem, sem, sem, *[hbm] * (2 * n), pl.BlockSpec(memory_space=pltpu.VMEM)),
        input_output_aliases={i: 3 + i for i in range(2 * n)},
        compiler_params=pltpu.CompilerParams(has_side_effects=pltpu.SideEffectType.DATAFLOW_SIDE_EFFECTING),
    )(*[pltpu.with_memory_space_constraint(a, pltpu.HBM) for a, _ in items], *lands, *([after] if n_after else []))
    return (mode, axes, outs[:3], outs[3:3 + n], outs[3 + n:3 + 2 * n]), outs[-1][0:1, 0:1]


def _chip_exchange_wait(name, handle, after):
    mode, axes, sems, srcs, lands = handle
    n = len(axes)
    after = list(after) if isinstance(after, (tuple, list)) else [after]
    hbm = pl.BlockSpec(memory_space=pltpu.HBM)
    sem = pl.BlockSpec(memory_space=pltpu.SEMAPHORE)

    def body(*refs):
        src_refs, land_refs = refs[:n], refs[n:2 * n]
        send_sems, recv_sems, local_sems = refs[2 * n:2 * n + 3]
        local, sends, arrivals = _chip_exchange_copies(mode, axes, src_refs, land_refs, send_sems, recv_sems, local_sems)
        for cp in sends:
            cp.wait_send()
        for cp in arrivals:
            cp.wait_recv()
        for cp in local:
            cp.wait()

    outs = pl.pallas_call(
        body, name=name,
        out_shape=(*[pltpu.HBM(a.shape, a.dtype) for a in srcs], *[pltpu.HBM(l.shape, l.dtype) for l in lands]),
        in_specs=[hbm] * (2 * n) + [sem] * 3 + [pl.BlockSpec(memory_space=pl.ANY)] * len(after), out_specs=[hbm] * (2 * n),
        input_output_aliases={i: i for i in range(2 * n)},
        compiler_params=pltpu.CompilerParams(has_side_effects=pltpu.SideEffectType.DATAFLOW_SIDE_EFFECTING),
    )(*srcs, *lands, *sems, *after)
    return outs[n:]


def _sum_slots(name, slots, tm=256):
    n_slots, r, c = slots.shape
    tm = _tile(r, tm)

    def body(*refs):
        acc = refs[0][...]
        for s_ref in refs[1:n_slots]:
            acc = acc + s_ref[...]
        refs[n_slots][...] = acc

    specs = [pl.BlockSpec((None, tm, c), functools.partial(lambda i, s: (s, i, 0), s=s)) for s in range(n_slots)]
    return pl.pallas_call(
        body, name=name, grid=(r // tm,), in_specs=specs, out_specs=pl.BlockSpec((tm, c), lambda i: (i, 0)),
        out_shape=jax.ShapeDtypeStruct((r, c), F32), compiler_params=_params("parallel"),
    )(*[slots] * n_slots)


def _swap_with_sibling(name, arrays):
    n = len(arrays)
    hbm = pl.BlockSpec(memory_space=pl.ANY)

    def body(*refs):
        ins, outs = refs[:n], refs[n:2 * n]
        send_sems, recv_sems = refs[2 * n:]
        sibling = (lax.axis_index("x"), lax.axis_index("y"), 1 - lax.axis_index("c"))
        copies = [pltpu.make_async_remote_copy(src_ref=ins[w], dst_ref=outs[w], send_sem=send_sems.at[w],
                                               recv_sem=recv_sems.at[w], device_id=sibling, device_id_type=MESH)
                  for w in range(n)]
        for cp in copies:
            cp.start()
        for cp in copies:
            cp.wait()

    return pl.pallas_call(
        body, name=name, in_specs=[hbm] * n, out_specs=[hbm] * n,
        out_shape=[jax.ShapeDtypeStruct(a.shape, a.dtype) for a in arrays],
        scratch_shapes=[pltpu.SemaphoreType.DMA((n,)), pltpu.SemaphoreType.DMA((n,))],
    )(*arrays)


def _adamw(g, w, m, v):
    m = ADAM_B1 * m + (1.0 - ADAM_B1) * g
    v = ADAM_B2 * v + (1.0 - ADAM_B2) * jnp.square(g)
    m_hat = m / (1.0 - ADAM_B1 ** ADAM_STEP)
    v_hat = v / (1.0 - ADAM_B2 ** ADAM_STEP)
    delta = -ADAM_LR * (m_hat / (jnp.sqrt(v_hat) + ADAM_EPS) + ADAM_WD * w)
    return delta, m, v


def _adamw_small(grads, ws, ms, vs):
    n = len(ws)

    def whole(a):
        return pl.BlockSpec(a.shape, functools.partial(lambda i, nd: (0,) * nd, nd=a.ndim))

    def body(*refs):
        for i in range(n):
            g, w, m, v = (refs[k * n + i][...] for k in range(4))
            for k, val in enumerate(_adamw(g, w, m, v)):
                refs[(4 + k) * n + i][...] = val

    outs = pl.pallas_call(
        body, name="adamw_small", grid=(1,), in_specs=[whole(a) for a in (*grads, *ws, *ms, *vs)],
        out_specs=[whole(w) for _ in range(3) for w in ws],
        out_shape=[jax.ShapeDtypeStruct(w.shape, F32) for _ in range(3) for w in ws],
        compiler_params=pltpu.CompilerParams(vmem_limit_bytes=VMEM_LIMIT),
    )(*grads, *ws, *ms, *vs)
    return outs[:n], outs[n:2 * n], outs[2 * n:]


def kernel(x, norm1_g, w_in, q_norm_g, k_norm_g, ssm_lambda_re, ssm_lambda_im, ssm_log_dt, ssm_b_re, ssm_b_im, ssm_c_re, ssm_c_im, ssm_d, w_glu, b_glu, attn_out_g, ssm_out_g, w_out, norm2_g, w_mlp_in, w_mlp_out, loss_target, m_norm1_g, m_w_in, m_q_norm_g, m_k_norm_g, m_ssm_lambda_re, m_ssm_lambda_im, m_ssm_log_dt, m_ssm_b_re, m_ssm_b_im, m_ssm_c_re, m_ssm_c_im, m_ssm_d, m_w_glu, m_b_glu, m_attn_out_g, m_ssm_out_g, m_w_out, m_norm2_g, m_w_mlp_in, m_w_mlp_out, v_norm1_g, v_w_in, v_q_norm_g, v_k_norm_g, v_ssm_lambda_re, v_ssm_lambda_im, v_ssm_log_dt, v_ssm_b_re, v_ssm_b_im, v_ssm_c_re, v_ssm_c_im, v_ssm_d, v_w_glu, v_b_glu, v_attn_out_g, v_ssm_out_g, v_w_out, v_norm2_g, v_w_mlp_in, v_w_mlp_out):
    batch, seq, d_model = x.shape
    tokens = batch * seq
    sb_width = w_in.shape[1]
    n_features = d_model

    big = [("w_in", w_in, m_w_in, v_w_in, 1), ("w_glu", w_glu, m_w_glu, v_w_glu, 0),
           ("w_out", w_out, m_w_out, v_w_out, 0), ("w_mlp_in", w_mlp_in, m_w_mlp_in, v_w_mlp_in, 1),
           ("w_mlp_out", w_mlp_out, m_w_mlp_out, v_w_mlp_out, 0)]
    small = [("norm1_g", norm1_g, m_norm1_g, v_norm1_g), ("q_norm_g", q_norm_g, m_q_norm_g, v_q_norm_g),
             ("k_norm_g", k_norm_g, m_k_norm_g, v_k_norm_g),
             ("ssm_lambda_re", ssm_lambda_re, m_ssm_lambda_re, v_ssm_lambda_re),
             ("ssm_lambda_im", ssm_lambda_im, m_ssm_lambda_im, v_ssm_lambda_im),
             ("ssm_log_dt", ssm_log_dt, m_ssm_log_dt, v_ssm_log_dt),
             ("ssm_b_re", ssm_b_re, m_ssm_b_re, v_ssm_b_re), ("ssm_b_im", ssm_b_im, m_ssm_b_im, v_ssm_b_im),
             ("ssm_c_re", ssm_c_re, m_ssm_c_re, v_ssm_c_re), ("ssm_c_im", ssm_c_im, m_ssm_c_im, v_ssm_c_im),
             ("ssm_d", ssm_d, m_ssm_d, v_ssm_d), ("b_glu", b_glu, m_b_glu, v_b_glu),
             ("attn_out_g", attn_out_g, m_attn_out_g, v_attn_out_g), ("ssm_out_g", ssm_out_g, m_ssm_out_g, v_ssm_out_g),
             ("norm2_g", norm2_g, m_norm2_g, v_norm2_g)]

    gather_in, tok_in = _chip_exchange_start("gather_w_in_start", "gather", [(w_in.astype(BF16), 1)])
    gather_mix, tok_mix_w = _chip_exchange_start(
        "gather_mix_start", "gather", [(w.astype(BF16), axis) for _, w, _, _, axis in big[1:3]], after=tok_in)
    gather_mlp, tok_rest = _chip_exchange_start(
        "gather_mlp_start", "gather", [(w.astype(BF16), axis) for _, w, _, _, axis in big[3:]], after=tok_mix_w)

    x2 = x.reshape(tokens, d_model)
    tgt2 = loss_target.reshape(tokens, d_model)
    g1, g2 = norm1_g[None, :], norm2_g[None, :]
    g_attn, g_ssm, bias_glu = attn_out_g[None, :], ssm_out_g[None, :], b_glu[None, :]
    heads = sb_width // HEAD_DIM
    qk_scale = 1.0 / math.sqrt(HEAD_DIM)
    gq, gk = (jnp.tile(q_norm_g, heads) * qk_scale)[None, :], jnp.tile(k_norm_g, heads)[None, :]
    lane_head = jnp.arange(LANES) // HEAD_DIM
    ones_blocks = (lane_head[:, None] == lane_head[None, :]).astype(F32)

    (xn,) = _rowwise("norm1", _rms, [x2], [g1 + tok_rest], [(d_model, BF16)], tm=512)
    s5_in = _s5_operator_inputs(ssm_lambda_re, ssm_lambda_im, ssm_log_dt, ssm_b_re, ssm_b_im, ssm_c_re, ssm_c_im, ssm_d)
    kt_row, b_mat, c_mat, la, lb = _s5_operators_call("s5_operators", s5_in)
    (wf_in,) = _chip_exchange_wait("gather_w_in_wait", gather_in, [xn, b_mat, c_mat])
    def proj_head(acc, gq_, gk_, ones):
        q, k, v = (acc[:, i * sb_width:(i + 1) * sb_width] for i in range(3))
        return acc, _head_rms(q, gq_, ones), _head_rms(k, gk_, ones), v

    proj, qn, kn, vb = _mm("proj_in", xn, wf_in, "nn", epilogue=proj_head, full_rows=True, max_tm=512,
                           extras=[(gq, "row"), (gk, "row"), (ones_blocks, "whole")],
                           out_dtypes=(F32, (BF16, sb_width), (BF16, sb_width), (BF16, sb_width)))
    sb, c_tot = _attn_fwd(qn, kn, vb, batch=batch, seq=seq, bq=ATTN_BQ, bk=ATTN_BK)
    ug = _tokens_to_groups("u_to_groups", proj, 3, sb_width)
    yg, xin = _s5_fwd(ug, kt_row, b_mat, c_mat, la, lb, batch=batch)
    y_ssm = _groups_to_tokens("y_to_tokens", yg)

    wf_glu, wf_out = _chip_exchange_wait("gather_mix_wait", gather_mix, [y_ssm, sb])
    (gate_pre,) = _mm("glu_gate", y_ssm, wf_glu, "nn", a_fn=_gelu, extras=[(bias_glu, "row")],
                      epilogue=lambda acc, b: acc + b)
    (mixed,) = _rowwise("mix_norm", _mixed, [sb, y_ssm, gate_pre], [g_attn, g_ssm], [(2 * sb_width, BF16)], tm=512)
    def out_head(acc, r, g):
        h = acc + r
        return h, _rms(h, g)

    h1, hn = _mm("proj_out", mixed, wf_out, "nn", extras=[(x2, "tile"), (g2, "row")], epilogue=out_head,
                 out_dtypes=(F32, BF16), full_rows=True)
    def mlp_act(acc):
        r = jnp.maximum(acc, 0.0)
        return r * r, r

    wf_mlp_in, wf_mlp_out = _chip_exchange_wait("gather_mlp_wait", gather_mlp, [h1, hn])
    act, act_root = _mm("mlp_in", hn, wf_mlp_in, "nn", epilogue=mlp_act, out_dtypes=(BF16, BF16))
    inv_n = 1.0 / n_features

    def loss_head(acc, r, t):
        d = ((acc + r) - t) * inv_n
        return d, d, jnp.sum(d * d, keepdims=True) * (0.5 * n_features)

    dy, dy_b, loss_tiles = _mm("mlp_out_loss", act, wf_mlp_out, "nn", extras=[(h1, "tile"), (tgt2, "tile")],
                               epilogue=loss_head, out_dtypes=(F32, BF16), tile_sums=("scalar",))
    loss_part = jnp.sum(loss_tiles)

    (dw_mlp_out,) = _mm("dw_mlp_out", act, dy_b, "tn")
    (dpre,) = _mm("d_mlp_act", dy_b, wf_mlp_out, "nt", extras=[(act_root, "tile")],
                  epilogue=lambda acc, r: acc * (2.0 * r.astype(F32)), out_dtypes=(BF16,))
    (dw_mlp_in,) = _mm("dw_mlp_in", hn, dpre, "tn")
    scatter_mlp, tok_mlp = _chip_exchange_start("scatter_mlp_start", "scatter", [(dw_mlp_in, 1), (dw_mlp_out, 0)])
    def norm_bwd(dn, res, hx, g):
        _, vjp = jax.vjp(_rms, hx, g)
        dh, dg = vjp(dn)
        return res + dh, dg

    dh1, dg_tiles = _mm("d_norm2_in", dpre, wf_mlp_in, "nt", extras=[(dy, "tile"), (h1, "tile"), (g2 + tok_mlp, "row")],
                        epilogue=norm_bwd, tile_sums=("row",), full_rows=True)
    dg_norm2 = jnp.sum(dg_tiles, axis=0, keepdims=True)
    (dw_out,) = _mm("dw_out", mixed, dh1, "tn")

    def mixed_bwd(dm, sb_, ys, gp, ga, gs):
        _, vjp = jax.vjp(lambda a, act, b, c, d: jnp.concatenate(
            [_rms(a, c), _rms(act * jax.nn.sigmoid(b), d)], axis=-1), sb_, _gelu(ys), gp, ga, gs)
        dsb_, dact, dgp_, dga, dgs = vjp(dm)
        return dsb_, dgp_, dact, dga, dgs, jnp.sum(dgp_, axis=0, keepdims=True)

    dsb, dgate_pre, dact_part, *gain_tiles = _mm(
        "d_mixed", dh1, wf_out, "nt", epilogue=mixed_bwd, full_rows=True, max_tm=256,
        extras=[(sb, "tile"), (y_ssm, "tile"), (gate_pre, "tile"), (g_attn, "row"), (g_ssm, "row")],
        out_dtypes=((F32, sb_width), (BF16, sb_width), (F32, sb_width)), tile_sums=(("row", sb_width),) * 3)
    dg_attn, dg_ssm, db_glu = (jnp.sum(t, axis=0, keepdims=True) for t in gain_tiles)

    def gelu_bwd(acc, part, ys):
        _, vjp = jax.vjp(_gelu, ys)
        return vjp(acc + part)[0]

    (dy_ssm,) = _mm("d_glu_in", dgate_pre, wf_glu, "nt", extras=[(dact_part, "tile"), (y_ssm, "tile")], epilogue=gelu_bwd)
    (dw_glu,) = _mm("dw_glu", y_ssm, dgate_pre, "tn", a_fn=_gelu)
    scatter_mix, tok_mix = _chip_exchange_start("scatter_mix_start", "scatter", [(dw_glu, 0), (dw_out, 0)])

    dug, dkt_row, db_mat, dc_mat, dla, dlb = _s5_bwd(ug, _tokens_to_groups("dy_to_groups", dy_ssm, 0, sb_width), xin,
                                                     kt_row, b_mat, c_mat, la, lb + tok_mix, batch=batch)
    du = _groups_to_tokens("du_to_tokens", dug)
    d_in = _s5_operators_call("s5_operators_bwd", s5_in, (dkt_row, db_mat, dc_mat, dla, dlb))
    ds5 = [d_in[0][:, 0, :] + d_in[2][:, :, 0], d_in[1][:, 0, :] + d_in[3][:, :, 0], d_in[4][:, 0, 0],
           d_in[5].transpose(0, 2, 1), d_in[6].transpose(0, 2, 1), d_in[7].transpose(0, 2, 1), d_in[8].transpose(0, 2, 1),
           d_in[9][:, 0, :]]

    def pack(parts):
        flat = jnp.concatenate([p.reshape(-1) for p in parts])
        rows = -(-flat.shape[0] // (8 * LANES)) * 8
        return jnp.pad(flat, (0, rows * LANES - flat.shape[0])).reshape(rows, LANES)

    def unpack(packed, names):
        flat, out, off = packed.reshape(-1), {}, 0
        for name in names:
            shape = small_shapes[name]
            size = math.prod(shape)
            out[name] = flat[off:off + size].reshape(shape)
            off += size
        return out, flat[off]

    small_shapes = {name: w.shape for name, w, _, _ in small}
    early_names = ["ssm_lambda_re", "ssm_lambda_im", "ssm_log_dt", "ssm_b_re", "ssm_b_im", "ssm_c_re", "ssm_c_im", "ssm_d",
                   "b_glu", "attn_out_g", "ssm_out_g", "norm2_g"]
    late_names = ["norm1_g", "q_norm_g", "k_norm_g"]
    early = pack([*ds5, db_glu[0], dg_attn[0], dg_ssm[0], dg_norm2[0], loss_part])
    early_exchange, _ = _chip_exchange_start("small_early_start", "all", [(early, 0)])

    dqn, dkn, dv = _attn_bwd(qn, kn, vb, c_tot, dsb, batch=batch, seq=seq, bq=ATTN_BQ, bk=ATTN_BK,
                             after=early_exchange[3][0])
    (early_slots,) = _chip_exchange_wait("small_early_wait", early_exchange, dqn)
    small_g, loss = unpack(_sum_slots("sum_small_early", early_slots), early_names)

    def qk_bwd(q, k, dq_, dk_, dv_, du_, gq_, gk_, ones):
        _, vjp_q = jax.vjp(lambda a, g: _head_rms(a, g, ones), q, gq_)
        _, vjp_k = jax.vjp(lambda a, g: _head_rms(a, g, ones), k, gk_)
        dq, dgq = vjp_q(dq_)
        dk, dgk = vjp_k(dk_)
        return jnp.concatenate([dq, dk, dv_, du_], axis=1), dgq, dgk

    dproj, dgq, dgk = _rowwise("qk_norm_bwd", qk_bwd, [(proj, sb_width, 0), (proj, sb_width, 1), dqn, dkn, dv, du],
                               [gq, gk, ones_blocks], [(4 * sb_width, BF16)], [(1, sb_width)] * 2, tm=512)
    (dw_in,) = _mm("dw_in", xn, dproj, "tn")
    scatter_in, tok_w_in = _chip_exchange_start("scatter_in_start", "scatter", [(dw_in, 1)])
    dx, dg_tiles = _mm("d_norm1_in", dproj, wf_in, "nt", extras=[(dh1, "tile"), (x2, "tile"), (g1 + tok_w_in, "row")],
                       epilogue=norm_bwd, tile_sums=("row",), full_rows=True)
    dg_norm1 = jnp.sum(dg_tiles, axis=0, keepdims=True)

    late = pack([dg_norm1[0], dgq.reshape(heads, HEAD_DIM).sum(0) * qk_scale, dgk.reshape(heads, HEAD_DIM).sum(0),
                 jnp.zeros((1,), F32)])
    late_exchange, _ = _chip_exchange_start("small_late_start", "all", [(late, 0)])

    def adam_big(sa, sb_, w, m, v):
        g = sa + sb_
        delta, m, v = _adamw(g, w, m, v)
        return g, delta, m, v

    def reduce_and_update(tag, params, slots):
        mine = [_sum_slots("sum_" + name, s) for s, (name, *_rest) in zip(slots, params)]
        theirs = _swap_with_sibling("swap_" + tag, mine)
        return {name: _rowwise("adamw_" + name, adam_big, [sa, sb_, w, m, v], [], [(w.shape[1], F32)] * 4)
                for (name, w, m, v, _), sa, sb_ in zip(params, mine, theirs)}

    started = late_exchange[3][0]
    slots_mlp_in, slots_mlp_out = _chip_exchange_wait("scatter_mlp_wait", scatter_mlp, started)
    slots_glu, slots_out = _chip_exchange_wait("scatter_mix_wait", scatter_mix, started)
    big_out = reduce_and_update("rest", big[1:], [slots_glu, slots_out, slots_mlp_in, slots_mlp_out])

    (late_slots,) = _chip_exchange_wait("small_late_wait", late_exchange, big_out["w_mlp_out"][3])
    reduced = _sum_slots("sum_small_late", late_slots)
    small_g.update(unpack(reduced, late_names)[0])
    narrow = {name for name, w, _, _ in small if w.ndim == 3 and w.shape[2] < w.shape[1]}

    def flip(a, name):
        return jnp.swapaxes(a, 1, 2) if name in narrow else a

    small_upd = _adamw_small([flip(small_g[name], name) for name, *_ in small], [flip(w, name) for name, w, _, _ in small],
                             [flip(m, name) for name, _, m, _ in small], [flip(v, name) for name, _, _, v in small])
    small_out = [small_g] + [{name: flip(small_upd[kind][i], name) for i, (name, *_) in enumerate(small)}
                             for kind in range(3)]

    (slots_in,) = _chip_exchange_wait("scatter_in_wait", scatter_in, reduced)
    big_out.update(reduce_and_update("w_in", big[:1], [slots_in]))
    names = ["norm1_g", "w_in", "q_norm_g", "k_norm_g", "ssm_lambda_re", "ssm_lambda_im", "ssm_log_dt", "ssm_b_re",
             "ssm_b_im", "ssm_c_re", "ssm_c_im", "ssm_d", "w_glu", "b_glu", "attn_out_g", "ssm_out_g", "w_out",
             "norm2_g", "w_mlp_in", "w_mlp_out"]
    outs = [loss, dx.reshape(batch, seq, d_model)]
    for kind in range(4):
        for name in names:
            outs.append(big_out[name][kind] if name in big_out else small_out[kind][name])
    return tuple(outs)
```

```python
import functools
import math

import jax
import jax.numpy as jnp
from jax import lax
from jax.experimental import pallas as pl
from jax.experimental.pallas import tpu as pltpu

F32 = jnp.float32
BF16 = jnp.bfloat16
F32_DOT = lax.Precision.HIGH
MESH = pl.DeviceIdType.MESH

RMS_EPS = 1e-6
HEAD_DIM = 64
SSM_GROUP = 16
SSM_CHUNK = 16
LANES = 128
N_CHIPS = 4
N_DEV = 8
VMEM_LIMIT = 48 * 1024 * 1024

ADAM_LR = 0.001
ADAM_B1 = 0.9
ADAM_B2 = 0.999
ADAM_EPS = 1e-08
ADAM_WD = 0.01
ADAM_STEP = 10


def _tile(n, pref):
    t = min(n, pref)
    while n % t:
        t //= 2
    return t


def _params(*sem):
    return pltpu.CompilerParams(dimension_semantics=sem, vmem_limit_bytes=VMEM_LIMIT)


_DIMS = {"nn": (((1,), (0,)), ((), ())), "nt": (((1,), (1,)), ((), ())), "tn": (((0,), (0,)), ((), ()))}


MM_VMEM_BUDGET = 40 * 1024 * 1024


def _mm_tiles(m, n, k, a_bytes, b_bytes, tile_bytes, full_rows=False, max_tm=1024):
    best = None
    for tk in [t for t in (k, k // 2, k // 4, k // 8) if t >= 256 or t == k]:
        for tm in [t for t in (1024, 512, 256, 128) if t <= min(m, max_tm) and m % t == 0]:
            for tn in [n] if full_rows else [t for t in (1024, 512, 256, 128) if t <= n and n % t == 0]:
                need = 2 * (tm * tk * a_bytes + tk * tn * b_bytes) + 2 * tm * tn * tile_bytes + (tm * tn * 4 if tk < k else 0)
                if need > MM_VMEM_BUDGET:
                    continue
                traffic = m * k * a_bytes * (1 if tk == k else n // tn) + k * n * b_bytes * (1 if n == tn and tk == k else m // tm)
                key = (tk < k, traffic, -tm * tn)
                if best is None or key < best[0]:
                    best = (key, (tm, tn, tk))
    return best[1]


def _mm(name, a, b, mode, *, a_fn=None, extras=(), epilogue=None, out_dtypes=(F32,), tile_sums=(), full_rows=False,
        max_tm=1024):
    if mode == "nn":
        (m, k), n = a.shape, b.shape[1]
    elif mode == "nt":
        (m, k), n = a.shape, b.shape[0]
    else:
        (k, m), n = a.shape, b.shape[1]
    outs_spec = [(d, n) if not isinstance(d, tuple) else d for d in out_dtypes]
    sums_spec = [(s, n) if not isinstance(s, tuple) else s for s in tile_sums]
    assert full_rows or all(w == n for _, w in outs_spec + sums_spec) and all(e.shape[1] == n for e, _ in extras)
    tile_bytes = (sum(e.dtype.itemsize * e.shape[1] for e, kind in extras if kind == "tile")
                  + sum(jnp.dtype(d).itemsize * w for d, w in outs_spec)) // n + 1
    tm, tn, tk = _mm_tiles(m, n, k, a.dtype.itemsize, b.dtype.itemsize, tile_bytes, full_rows, max_tm)
    nk = k // tk
    ne, nout = len(extras), len(out_dtypes)
    dims = _DIMS[mode]

    def width_spec(rows, w):
        if w == n:
            return pl.BlockSpec((rows, tn), (lambda i, j, kk: (i, j)) if rows != 1 else (lambda i, j, kk: (0, j)))
        return pl.BlockSpec((rows, w), (lambda i, j, kk: (i, 0)) if rows != 1 else (lambda i, j, kk: (0, 0)))

    def body(a_ref, b_ref, *rest):
        ex, outs, sums = rest[:ne], rest[ne:ne + nout], rest[ne + nout:ne + nout + len(tile_sums)]
        at = a_ref[...]
        if a_fn is not None:
            at = a_fn(at)
        p = lax.dot_general(at.astype(BF16), b_ref[...].astype(BF16), dims, preferred_element_type=F32)

        def finish(r):
            if epilogue is not None:
                r = epilogue(r, *[e[...] for e in ex])
            if not isinstance(r, (tuple, list)):
                r = (r,)
            for o, v in zip(outs, r[:nout]):
                o[...] = v.astype(o.dtype)
            for o, v, (kind, _) in zip(sums, r[nout:], sums_spec):
                first = lax.broadcasted_iota(jnp.int32, o.shape, 0) == 0
                if kind == "scalar":
                    first &= lax.broadcasted_iota(jnp.int32, o.shape, 1) == 0
                o[...] = jnp.where(first, v, 0.0)

        if nk == 1:
            finish(p)
        else:
            acc = rest[ne + nout + len(tile_sums)]
            kk = pl.program_id(2)

            @pl.when(kk == 0)
            def _():
                acc[...] = p

            @pl.when(kk > 0)
            def _():
                acc[...] += p

            @pl.when(kk == nk - 1)
            def _():
                finish(acc[...])

    if mode == "tn":
        a_spec = pl.BlockSpec((tk, tm), lambda i, j, kk: (kk, i))
    else:
        a_spec = pl.BlockSpec((tm, tk), lambda i, j, kk: (i, kk))
    if mode == "nt":
        b_spec = pl.BlockSpec((tn, tk), lambda i, j, kk: (j, kk))
    else:
        b_spec = pl.BlockSpec((tk, tn), lambda i, j, kk: (kk, j))
    ex_specs = [pl.BlockSpec(e.shape, lambda i, j, kk: (0, 0)) if kind == "whole"
                else width_spec(tm if kind == "tile" else 1, e.shape[1]) for e, kind in extras]
    return pl.pallas_call(
        body, name=name, grid=(m // tm, n // tn, nk),
        in_specs=[a_spec, b_spec] + ex_specs,
        out_specs=([width_spec(tm, w) for _, w in outs_spec]
                   + [pl.BlockSpec((8, LANES), lambda i, j, kk: (i, j)) if kind == "scalar" else width_spec(8, w)
                      for kind, w in sums_spec]),
        out_shape=([jax.ShapeDtypeStruct((m, w), dt) for dt, w in outs_spec]
                   + [jax.ShapeDtypeStruct((m // tm * 8, n // tn * LANES if kind == "scalar" else w), F32)
                      for kind, w in sums_spec]),
        scratch_shapes=[pltpu.VMEM((tm, tn), F32)] if nk > 1 else [],
        compiler_params=_params("parallel", "parallel", "arbitrary"),
    )(a, b, *[e for e, _ in extras])


def _rowwise(name, fn, rows, consts, row_outs, acc_outs=(), tm=256):
    norm = [r if isinstance(r, tuple) else (r, r.shape[1], 0) for r in rows]
    t = norm[0][0].shape[0]
    tm = _tile(t, tm)
    nr, nc, no = len(norm), len(consts), len(row_outs)

    def body(*refs):
        outs = fn(*[r[...] for r in refs[:nr + nc]])
        if not isinstance(outs, (tuple, list)):
            outs = (outs,)
        o_refs, a_refs = refs[nr + nc:nr + nc + no], refs[nr + nc + no:]
        for r, v in zip(o_refs, outs[:no]):
            r[...] = v.astype(r.dtype)
        if a_refs:
            i = pl.program_id(0)

            @pl.when(i == 0)
            def _():
                for r, v in zip(a_refs, outs[no:]):
                    r[...] = v

            @pl.when(i > 0)
            def _():
                for r, v in zip(a_refs, outs[no:]):
                    r[...] += v

    in_specs = [pl.BlockSpec((tm, w), functools.partial(lambda i, cb: (i, cb), cb=cb)) for _, w, cb in norm]
    in_specs += [pl.BlockSpec(c.shape, functools.partial(lambda i, nd: (0,) * nd, nd=c.ndim)) for c in consts]
    out_specs = [pl.BlockSpec((tm, w), lambda i: (i, 0)) for w, _ in row_outs]
    out_specs += [pl.BlockSpec(s, functools.partial(lambda i, nd: (0,) * nd, nd=len(s))) for s in acc_outs]
    out_shape = [jax.ShapeDtypeStruct((t, w), dt) for w, dt in row_outs]
    out_shape += [jax.ShapeDtypeStruct(s, F32) for s in acc_outs]
    return pl.pallas_call(
        body, name=name, grid=(t // tm,), in_specs=in_specs, out_specs=out_specs, out_shape=out_shape,
        compiler_params=_params("arbitrary"),
    )(*[r[0] for r in norm], *consts)


def _rms(x, g):
    return x * lax.rsqrt(jnp.mean(x * x, axis=-1, keepdims=True) + RMS_EPS) * g


@jax.custom_vjp
def _head_sums(x, ones_blocks):
    parts = [jnp.dot(x[:, j:j + LANES], ones_blocks, precision=F32_DOT, preferred_element_type=F32)
             for j in range(0, x.shape[1], LANES)]
    return jnp.concatenate(parts, axis=1)


_head_sums.defvjp(lambda x, ones_blocks: (_head_sums(x, ones_blocks), ones_blocks),
                  lambda ones_blocks, ct: (_head_sums(ct, ones_blocks), None))


def _head_rms(x, g, ones_blocks):
    return x * lax.rsqrt(_head_sums(x * x, ones_blocks) * (1.0 / HEAD_DIM) + RMS_EPS) * g


def _gelu(x):
    return x * (0.5 * (1.0 + jnp.tanh(math.sqrt(2.0 / math.pi) * (x + 0.044715 * (x * x * x)))))


def _mixed(sb, y_ssm, gate_pre, g_attn, g_ssm):
    ssm = _gelu(y_ssm) * jax.nn.sigmoid(gate_pre)
    return jnp.concatenate([_rms(sb, g_attn), _rms(ssm, g_ssm)], axis=-1)


def _softplus(z):
    return jnp.maximum(z, 0.0) + jnp.log(1.0 + jnp.exp(-jnp.abs(z)))


def _running_sums(x, tri):
    return jnp.dot(x.astype(BF16), tri, preferred_element_type=F32)


def _dot_nt(a, b, **kw):
    return lax.dot_general(a, b, _DIMS["nt"], preferred_element_type=F32, **kw)


def _dot_tn(a, b, **kw):
    return lax.dot_general(a, b, _DIMS["tn"], preferred_element_type=F32, **kw)


ATTN_BQ, ATTN_BK = 2048, 256
HEAD_LANES = tuple(slice(h * HEAD_DIM, (h + 1) * HEAD_DIM) for h in range(LANES // HEAD_DIM))


def _attn_fwd(qs, kn, v, *, batch, seq, bq, bk):
    width = qs.shape[1]
    bq = _tile(seq, bq)
    bk = _tile(bq, bk)
    nq, kpq = seq // bq, bq // bk

    def body(q_ref, k_ref, v_ref, o_ref, c_ref):
        row = lax.broadcasted_iota(jnp.int32, (bq, bk), 0)
        col = lax.broadcasted_iota(jnp.int32, (bq, bk), 1)
        tri = (lax.broadcasted_iota(jnp.int32, (bk, bk), 0) >= lax.broadcasted_iota(jnp.int32, (bk, bk), 1)).astype(BF16)

        def q_block(qi, carry):
            r0 = pl.multiple_of(qi * bq, bq)
            qh = [q_ref[pl.ds(r0, bq), ln] for ln in HEAD_LANES]

            def tile(k0, state, top=0):
                diag = top is not None
                top = top or 0
                msk = (col < row)[:bq - top] if diag else None
                new = []
                for h, ln in enumerate(HEAD_LANES):
                    o, c = state[2 * h], state[2 * h + 1]
                    z = _dot_nt(qh[h][top:], k_ref[pl.ds(k0, bk), ln])
                    sp = _softplus(z)
                    if diag:
                        sp = jnp.where(msk, sp, 0.0)
                    r = _running_sums(sp, tri)
                    a = jnp.exp(z - r - c[top:])
                    if diag:
                        a = jnp.where(msk, a, 0.0)
                    o_new = o[top:] + jnp.dot(a.astype(BF16), v_ref[pl.ds(k0, bk), ln], preferred_element_type=F32)
                    c_new = c[top:] + r[:, 0:1]
                    if top:
                        o_new, c_new = jnp.concatenate([o[:top], o_new]), jnp.concatenate([c[:top], c_new])
                    new += [o_new, c_new]
                return tuple(new)

            state = (jnp.zeros((bq, HEAD_DIM), F32), jnp.zeros((bq, 1), F32)) * len(HEAD_LANES)
            for d in reversed(range(kpq)):
                state = tile(pl.multiple_of(r0 + d * bk, bk), state, top=d * bk)
            state = lax.fori_loop(0, qi * kpq, lambda it, st: tile(pl.multiple_of(r0 - (it + 1) * bk, bk), st, None),
                                  state)
            for h, ln in enumerate(HEAD_LANES):
                o_ref[pl.ds(r0, bq), ln] = state[2 * h]
                c_ref[pl.ds(r0, bq), ln] = jnp.broadcast_to(state[2 * h + 1], (bq, HEAD_DIM))
            return carry

        lax.fori_loop(0, nq, q_block, 0)

    spec = pl.BlockSpec((seq, LANES), lambda b, h: (b, h))
    shape = jax.ShapeDtypeStruct((batch * seq, width), F32)
    return pl.pallas_call(
        body, name="attn_fwd", grid=(batch, width // LANES), in_specs=[spec, spec, spec], out_specs=[spec, spec],
        out_shape=[shape, shape], compiler_params=_params("parallel", "parallel"),
    )(qs, kn, v)


def _attn_bwd(qs, kn, v, c_tot, do, *, batch, seq, bq, bk, after):
    width = qs.shape[1]
    bq = _tile(seq, bq)
    bk = _tile(bq, bk)
    nq, kpq = seq // bq, bq // bk

    def body(q_ref, k_ref, v_ref, c_ref, do_ref, after_ref, dq_ref, dk_ref, dv_ref):
        row = lax.broadcasted_iota(jnp.int32, (bq, bk), 0)
        col = lax.broadcasted_iota(jnp.int32, (bq, bk), 1)
        sq_row = lax.broadcasted_iota(jnp.int32, (bk, bk), 0)
        sq_col = lax.broadcasted_iota(jnp.int32, (bk, bk), 1)
        tri = (sq_row >= sq_col).astype(BF16)
        tri_t = (sq_row <= sq_col).astype(BF16)
        dk_ref[...] = jnp.zeros_like(dk_ref)
        dv_ref[...] = jnp.zeros_like(dv_ref)

        def q_block(qi, carry):
            r0 = pl.multiple_of(qi * bq, bq)
            qh = [q_ref[pl.ds(r0, bq), ln] for ln in HEAD_LANES]
            d_out = [do_ref[pl.ds(r0, bq), ln].astype(BF16) for ln in HEAD_LANES]
            c_all = [c_ref[pl.ds(r0, bq), ln][:, 0:1] for ln in HEAD_LANES]

            def tile(k0, state, top=0):
                diag = top is not None
                top = top or 0
                last = diag and top == bq - bk
                msk = (col < row)[:bq - top] if diag else None
                new = []
                for h, ln in enumerate(HEAD_LANES):
                    c_left, g_left, dq = state[3 * h:3 * h + 3]
                    q, d_o = qh[h][top:], d_out[h][top:]
                    k = k_ref[pl.ds(k0, bk), ln]
                    z = _dot_nt(q, k)
                    e = jnp.exp(-jnp.abs(z))
                    sp = jnp.maximum(z, 0.0) + jnp.log(1.0 + e)
                    sig = jnp.exp(z - sp)
                    if diag:
                        sp = jnp.where(msk, sp, 0.0)
                    r = _running_sums(sp, tri)
                    c_new = c_left[top:] + r[:, 0:1]
                    a = jnp.exp(z - r - (0.0 if last else c_all[h][top:] - c_new))
                    if diag:
                        a = jnp.where(msk, a, 0.0)
                    g = a * _dot_nt(d_o, v_ref[pl.ds(k0, bk), ln])
                    pg = _running_sums(g, tri_t)
                    dz = g - sig * (g_left[top:] + pg)
                    if diag:
                        dz = jnp.where(msk, dz, 0.0)
                    dz = dz.astype(BF16)
                    dk_ref[pl.ds(k0, bk), ln] += _dot_tn(dz, q)
                    dv_ref[pl.ds(k0, bk), ln] += _dot_tn(a.astype(BF16), d_o)
                    g_new = g_left[top:] + pg[:, bk - 1:bk]
                    dq_new = dq[top:] + jnp.dot(dz, k, preferred_element_type=F32)
                    if top:
                        c_new = jnp.concatenate([c_left[:top], c_new])
                        g_new = jnp.concatenate([g_left[:top], g_new])
                        dq_new = jnp.concatenate([dq[:top], dq_new])
                    new += [c_new, g_new, dq_new]
                return tuple(new)

            zero = jnp.zeros((bq, 1), F32)
            init = (zero, zero, jnp.zeros((bq, HEAD_DIM), F32)) * len(HEAD_LANES)
            state = lax.fori_loop(0, qi * kpq, lambda it, st: tile(pl.multiple_of(it * bk, bk), st, None), init)
            for d in range(kpq):
                state = tile(pl.multiple_of(r0 + d * bk, bk), state, top=d * bk)
            for h, ln in enumerate(HEAD_LANES):
                dq_ref[pl.ds(r0, bq), ln] = state[3 * h + 2]
            return carry

        lax.fori_loop(0, nq, q_block, 0)

    spec = pl.BlockSpec((seq, LANES), lambda b, h: (b, h))
    shape = jax.ShapeDtypeStruct((batch * seq, width), F32)
    return pl.pallas_call(
        body, name="attn_bwd", grid=(batch, width // LANES),
        in_specs=[spec] * 5 + [pl.BlockSpec(memory_space=pl.ANY)], out_specs=[spec] * 3,
        out_shape=[shape] * 3, compiler_params=_params("parallel", "parallel"),
    )(qs, kn, v, c_tot, do, after)


def _pattern(rows, cols, hit):
    r, c = lax.broadcasted_iota(jnp.int32, (rows, cols), 0), lax.broadcasted_iota(jnp.int32, (rows, cols), 1)
    return hit(r, c).astype(F32)


def _s5_group_operators(lr_r, li_r, lr_c, li_c, log_dt, bt_re, bt_im, ct_re, ct_im, d_row):
    cs = SSM_CHUNK
    n_ch, n_state = bt_re.shape
    width = cs * n_ch
    dt = jnp.exp(log_dt)

    def spread(x, pattern):
        return jnp.dot(x, pattern, precision=F32_DOT, preferred_element_type=F32)

    twice = _pattern(n_state, 2 * n_state, lambda r, c: r == c % n_state)
    steps = lax.broadcasted_iota(jnp.int32, (cs + 1, 1), 0).astype(F32)
    mag = jnp.exp(steps * (lr_r * dt))
    ang = steps * (li_r * dt)
    pw_re, pw_im = mag * jnp.cos(ang), mag * jnp.sin(ang)
    num_re, num_im = pw_re[1:2] - 1.0, pw_im[1:2]
    den = lr_r * lr_r + li_r * li_r
    cf_re = (num_re * lr_r + num_im * li_r) / den
    cf_im = (num_im * lr_r - num_re * li_r) / den
    bb_re = spread(cf_re * bt_re - cf_im * bt_im, twice)
    bb_im = spread(cf_re * bt_im + cf_im * bt_re, twice)
    pw2_re, pw2_im = spread(pw_re, twice), spread(pw_im, twice)
    real_half = lax.broadcasted_iota(jnp.int32, (1, 2 * n_state), 1) < n_state
    blocks = []
    for s in range(cs):
        pr, pi = pw2_re[cs - 1 - s:cs - s], pw2_im[cs - 1 - s:cs - s]
        blocks.append(jnp.where(real_half, bb_re * pr - bb_im * pi, bb_re * pi + bb_im * pr))
    b_mat = jnp.concatenate(blocks, axis=0)
    la = pw2_re[cs:cs + 1]
    lb = jnp.where(real_half, -pw2_im[cs:cs + 1], pw2_im[cs:cs + 1])

    lane = lax.broadcasted_iota(jnp.int32, (1, width), 1)
    tile_out = _pattern(n_ch, width, lambda r, c: r == c % n_ch)
    c_re, c_im = spread(ct_re, tile_out), spread(ct_im, tile_out)

    k_row = lax.broadcasted_iota(jnp.int32, (1, cs), 1).astype(F32)
    m, a = jnp.exp(k_row * (lr_c * dt)), k_row * (li_c * dt)
    repeat = _pattern(cs, width, lambda r, c: r == c // n_ch)
    p_re, p_im = spread(m * jnp.cos(a), repeat), spread(m * jnp.sin(a), repeat)
    w_re, w_im = p_re * c_re - p_im * c_im, p_re * c_im + p_im * c_re
    skip = jnp.where((lane < n_ch) & (lane == lax.broadcasted_iota(jnp.int32, (n_ch, width), 0)),
                     spread(d_row, tile_out), 0.0)
    kt_row = (jnp.dot(bb_re[:, :n_state], w_re, precision=F32_DOT, preferred_element_type=F32)
              - jnp.dot(bb_im[:, :n_state], w_im, precision=F32_DOT, preferred_element_type=F32) + skip)
    bar_re, bar_im = jnp.exp(lr_c * dt) * jnp.cos(li_c * dt), jnp.exp(lr_c * dt) * jnp.sin(li_c * dt)
    w1_re, w1_im = w_re * bar_re - w_im * bar_im, w_re * bar_im + w_im * bar_re
    c_mat = jnp.concatenate([w1_re, -w1_im], axis=0)
    return kt_row, b_mat, c_mat, la, lb


def _s5_operator_inputs(lam_re, lam_im, log_dt, b_re, b_im, c_re, c_im, d_skip):
    return (lam_re[:, None, :], lam_im[:, None, :], lam_re[:, :, None], lam_im[:, :, None], log_dt[:, None, None],
            b_re.transpose(0, 2, 1), b_im.transpose(0, 2, 1), c_re.transpose(0, 2, 1), c_im.transpose(0, 2, 1),
            d_skip[:, None, :])


def _s5_operators_call(name, args, cotangents=None, gb=8):
    groups = args[0].shape[0]
    gb = _tile(groups, gb)
    n_in = len(args)

    def body(*refs):
        n_ct = 0 if cotangents is None else len(cotangents)
        ins, cts, outs = refs[:n_in], refs[n_in:n_in + n_ct], refs[n_in + n_ct:]
        for g in range(gb):
            vals = [r[g] for r in ins]
            if cotangents is None:
                res = _s5_group_operators(*vals)
            else:
                res = jax.vjp(_s5_group_operators, *vals)[1](tuple(c[g] for c in cts))
            for o, v in zip(outs, res):
                o[g] = v

    def spec(a):
        return pl.BlockSpec((gb, *a.shape[1:]), lambda i: (i, 0, 0))

    if cotangents is None:
        n_ch, n_state = args[5].shape[1:]
        width = SSM_CHUNK * n_ch
        out_shape = [jax.ShapeDtypeStruct((groups, *s), F32) for s in
                     ((n_ch, width), (width, 2 * n_state), (2 * n_state, width), (1, 2 * n_state), (1, 2 * n_state))]
    else:
        out_shape = [jax.ShapeDtypeStruct(a.shape, F32) for a in args]
    operands = [*args, *(cotangents or ())]
    return pl.pallas_call(
        body, name=name, grid=(groups // gb,), in_specs=[spec(a) for a in operands], out_specs=[spec(s) for s in out_shape],
        out_shape=out_shape, compiler_params=_params("parallel"),
    )(*operands)


GROUPS_PER_BLOCK = LANES // SSM_GROUP


def _tokens_to_groups(name, u, col_block, width):
    t = u.shape[0]
    n = t // SSM_CHUNK
    ch = SSM_CHUNK * SSM_GROUP
    blocks = width // LANES
    nb = GROUPS_PER_BLOCK

    def body(u_ref, o_ref):
        block = lax.broadcasted_iota(jnp.int32, (n, LANES), 1) // SSM_GROUP
        for half in range(SSM_CHUNK // nb):
            rows = [u_ref[pl.ds(half * nb + s, n, stride=SSM_CHUNK), :] for s in range(nb)]
            for shift in range(nb):
                merged = rows[shift]
                for b in range(1, nb):
                    merged = jnp.where(block == b, rows[(b + shift) % nb], merged)
                moved = pltpu.roll(merged, shift * SSM_GROUP, 1) if shift else merged
                for b in range(nb):
                    s = (b + shift) % nb
                    o_ref[b, :, half * LANES + s * SSM_GROUP:half * LANES + (s + 1) * SSM_GROUP] = (
                        moved[:, s * SSM_GROUP:(s + 1) * SSM_GROUP])

    return pl.pallas_call(
        body, name=name, grid=(blocks,),
        in_specs=[pl.BlockSpec((t, LANES), lambda j: (0, col_block * blocks + j))],
        out_specs=pl.BlockSpec((GROUPS_PER_BLOCK, n, ch), lambda j: (j, 0, 0)),
        out_shape=jax.ShapeDtypeStruct((width // SSM_GROUP, n, ch), F32), compiler_params=_params("parallel"),
    )(u)


def _groups_to_tokens(name, ug):
    groups, n, ch = ug.shape
    nb = GROUPS_PER_BLOCK

    def body(g_ref, o_ref, rows_ref):
        block = lax.broadcasted_iota(jnp.int32, (n, LANES), 1) // SSM_GROUP
        for half in range(SSM_CHUNK // nb):
            src = [g_ref[b, :, half * LANES:(half + 1) * LANES] for b in range(nb)]
            for shift in range(nb):
                merged = src[-shift % nb]
                for s in range(1, nb):
                    merged = jnp.where(block == s, src[(s - shift) % nb], merged)
                moved = pltpu.roll(merged, (nb - shift) * SSM_GROUP, 1) if shift else merged
                for b in range(nb):
                    rows_ref[(b + shift) % nb, :, b * SSM_GROUP:(b + 1) * SSM_GROUP] = moved[:, b * SSM_GROUP:(b + 1) * SSM_GROUP]
            for s in range(nb):
                o_ref[pl.ds(half * nb + s, n, stride=SSM_CHUNK), :] = rows_ref[s]

    return pl.pallas_call(
        body, name=name, grid=(groups // GROUPS_PER_BLOCK,),
        in_specs=[pl.BlockSpec((GROUPS_PER_BLOCK, n, ch), lambda j: (j, 0, 0))],
        out_specs=pl.BlockSpec((n * SSM_CHUNK, LANES), lambda j: (0, j)),
        out_shape=jax.ShapeDtypeStruct((n * SSM_CHUNK, groups * SSM_GROUP), F32),
        scratch_shapes=[pltpu.VMEM((nb, n, LANES), F32)], compiler_params=_params("parallel"),
    )(ug)


SCAN_ROWS = 8


def _toeplitz_to(tm_ref, g, kt_row):
    width = kt_row.shape[1]
    tm_ref[g] = jnp.zeros((width, width), F32)
    for s in range(SSM_CHUNK):
        tm_ref[g, s * SSM_GROUP:(s + 1) * SSM_GROUP, s * SSM_GROUP:] = kt_row[:, :width - s * SSM_GROUP]


def _lam_powers(la, lb, reverse):
    if reverse:
        lb = -lb

    def mul(p, q):
        return p[0] * q[0] - p[1] * q[1], p[0] * q[1] + p[1] * q[0]

    p1 = (la, lb)
    p2 = mul(p1, p1)
    p3 = mul(p2, p1)
    p4 = mul(p2, p2)
    rows = [p1, p2, p3, p4, mul(p4, p1), mul(p4, p2), mul(p4, p3), mul(p4, p4)]
    if reverse:
        rows = rows[::-1]
    idx = lax.broadcasted_iota(jnp.int32, (SCAN_ROWS, la.shape[1]), 0)
    tab_a = sum(jnp.where(idx == j, r[0], 0.0) for j, r in enumerate(rows))
    tab_b = sum(jnp.where(idx == j, r[1], 0.0) for j, r in enumerate(rows))
    return (p1, p2, p4), (tab_a, tab_b), idx


def _scan_block(e, carry, steps, table, idx, half, reverse):
    n = SCAN_ROWS
    for d, (pa, pb) in zip((1, 2, 4), steps):
        sh = pltpu.roll(e, n - d if reverse else d, 0)
        sh = jnp.where(idx < n - d if reverse else idx >= d, sh, 0.0)
        e = e + pa * sh + pb * pltpu.roll(sh, half, 1)
    tab_a, tab_b = table
    e = e + tab_a * carry + tab_b * pltpu.roll(carry, half, 1)
    shifted = jnp.where(idx == (n - 1 if reverse else 0), carry, pltpu.roll(e, n - 1 if reverse else 1, 0))
    edge = e[0:1] if reverse else e[n - 1:n]
    return shifted, jnp.broadcast_to(edge, e.shape)


def _s5_fwd(ug, kt_row, b_mat, c_mat, la, lb, *, batch, gb=8):
    groups, n, ch = ug.shape
    p2 = b_mat.shape[2]
    gb = _tile(groups, gb)
    nch = n // batch
    nblk = nch // SCAN_ROWS

    def body(u_ref, k_ref, b_ref, c_ref, la_ref, lb_ref, y_ref, x_ref, s_ref, tm_ref):
        for g in range(gb):
            _toeplitz_to(tm_ref, g, k_ref[g])
            s_ref[g] = jnp.dot(u_ref[g], b_ref[g], precision=F32_DOT, preferred_element_type=F32)
        powers = [_lam_powers(la_ref[g], lb_ref[g], False) for g in range(gb)]

        def step(blk, carries):
            new = []
            for g in range(gb):
                steps, table, idx = powers[g]
                for b in range(batch):
                    rows = pl.ds(pl.multiple_of(b * nch + blk * SCAN_ROWS, SCAN_ROWS), SCAN_ROWS)
                    x_in, carry = _scan_block(s_ref[g, rows, :], carries[g * batch + b], steps, table, idx, p2 // 2, False)
                    x_ref[g, rows, :] = x_in
                    new.append(carry)
            return tuple(new)

        lax.fori_loop(0, nblk, step, tuple(jnp.zeros((SCAN_ROWS, p2), F32) for _ in range(gb * batch)))
        for g in range(gb):
            y_ref[g] = (jnp.dot(u_ref[g], tm_ref[g], precision=F32_DOT, preferred_element_type=F32)
                        + jnp.dot(x_ref[g], c_ref[g], precision=F32_DOT, preferred_element_type=F32))

    def spec(a, b):
        return pl.BlockSpec((gb, a, b), lambda i: (i, 0, 0))

    return pl.pallas_call(
        body, name="s5_fwd", grid=(groups // gb,),
        in_specs=[spec(n, ch), spec(SSM_GROUP, ch), spec(ch, p2), spec(p2, ch), spec(1, p2), spec(1, p2)],
        out_specs=[spec(n, ch), spec(n, p2)],
        out_shape=[jax.ShapeDtypeStruct((groups, n, ch), F32), jax.ShapeDtypeStruct((groups, n, p2), F32)],
        scratch_shapes=[pltpu.VMEM((gb, n, p2), F32), pltpu.VMEM((gb, ch, ch), F32)],
        compiler_params=_params("parallel"),
    )(ug, kt_row, b_mat, c_mat, la, lb)


def _s5_bwd(ug, dyg, xin, kt_row, b_mat, c_mat, la, lb, *, batch, gb=8):
    groups, n, ch = ug.shape
    p2 = b_mat.shape[2]
    gb = _tile(groups, gb)
    nch = n // batch
    nblk = nch // SCAN_ROWS

    def body(u_ref, dy_ref, x_ref, k_ref, b_ref, c_ref, la_ref, lb_ref,
             du_ref, dk_ref, db_ref, dc_ref, dla_ref, dlb_ref, dx_ref, ds_ref, tm_ref):
        for g in range(gb):
            _toeplitz_to(tm_ref, g, k_ref[g])
            dx_ref[g] = _dot_nt(dy_ref[g], c_ref[g], precision=F32_DOT)
        powers = [_lam_powers(la_ref[g], lb_ref[g], True) for g in range(gb)]

        def step(it, carries):
            new = []
            for g in range(gb):
                steps, table, idx = powers[g]
                for b in range(batch):
                    rows = pl.ds(pl.multiple_of(b * nch + (nblk - 1 - it) * SCAN_ROWS, SCAN_ROWS), SCAN_ROWS)
                    d_s, carry = _scan_block(dx_ref[g, rows, :], carries[g * batch + b], steps, table, idx, p2 // 2, True)
                    ds_ref[g, rows, :] = d_s
                    new.append(carry)
            return tuple(new)

        lax.fori_loop(0, nblk, step, tuple(jnp.zeros((SCAN_ROWS, p2), F32) for _ in range(gb * batch)))
        for g in range(gb):
            u, dy, ds, x = u_ref[g], dy_ref[g], ds_ref[g], x_ref[g]
            du_ref[g] = _dot_nt(dy, tm_ref[g], precision=F32_DOT) + _dot_nt(ds, b_ref[g], precision=F32_DOT)
            tm_ref[g] = _dot_tn(u, dy, precision=F32_DOT)
            dk_ref[g] = tm_ref[g, 0:SSM_GROUP, :]
            for s in range(1, SSM_CHUNK):
                dk_ref[g, :, :ch - s * SSM_GROUP] += tm_ref[g, s * SSM_GROUP:(s + 1) * SSM_GROUP, s * SSM_GROUP:]
            db_ref[g] = _dot_tn(u, ds, precision=F32_DOT)
            dc_ref[g] = _dot_tn(x, dy, precision=F32_DOT)
            dla_ref[g] = jnp.sum(ds * x, axis=0, keepdims=True)
            dlb_ref[g] = jnp.sum(ds * pltpu.roll(x, p2 // 2, 1), axis=0, keepdims=True)

    def spec(a, b):
        return pl.BlockSpec((gb, a, b), lambda i: (i, 0, 0))

    def shape(a, b):
        return jax.ShapeDtypeStruct((groups, a, b), F32)

    return pl.pallas_call(
        body, name="s5_bwd", grid=(groups // gb,),
        in_specs=[spec(n, ch), spec(n, ch), spec(n, p2), spec(SSM_GROUP, ch), spec(ch, p2), spec(p2, ch), spec(1, p2),
                  spec(1, p2)],
        out_specs=[spec(n, ch), spec(SSM_GROUP, ch), spec(ch, p2), spec(p2, ch), spec(1, p2), spec(1, p2)],
        out_shape=[shape(n, ch), shape(SSM_GROUP, ch), shape(ch, p2), shape(p2, ch), shape(1, p2), shape(1, p2)],
        scratch_shapes=[pltpu.VMEM((gb, n, p2), F32), pltpu.VMEM((gb, n, p2), F32), pltpu.VMEM((gb, ch, ch), F32)],
        compiler_params=_params("parallel"),
    )(ug, dyg, xin, kt_row, b_mat, c_mat, la, lb)


def _block(ref, axis, j, size):
    start = j * size if isinstance(j, int) else pl.multiple_of(j * size, size)
    return ref.at[pl.ds(start, size), :] if axis == 0 else ref.at[:, pl.ds(start, size)]


def _chip_exchange_copies(mode, axes, srcs, lands, send_sems, recv_sems, local_sems):
    x, y, c = lax.axis_index("x"), lax.axis_index("y"), lax.axis_index("c")
    everyone = mode == "all"
    me = 4 * x + 2 * y + c if everyone else 2 * x + y
    n_peers = _exchange_peers(mode)
    local, sends, arrivals = [], [], []
    for w, axis in enumerate(axes):
        if mode == "gather":
            size = srcs[w].shape[axis]
            local.append(pltpu.make_async_copy(srcs[w], _block(lands[w], axis, me, size), local_sems.at[w]))
        elif mode == "scatter":
            size = srcs[w].shape[axis] // N_CHIPS
            local.append(pltpu.make_async_copy(_block(srcs[w], axis, me, size), lands[w].at[me], local_sems.at[w]))
        else:
            local.append(pltpu.make_async_copy(srcs[w], lands[w].at[me], local_sems.at[w]))
        for k in range(1, n_peers + 1):
            bits = k if everyone else 2 * k
            px = 1 - x if bits & 4 else x
            py = 1 - y if bits & 2 else y
            pc = 1 - c if bits & 1 else c
            peer = 4 * px + 2 * py + pc if everyone else 2 * px + py
            if mode == "gather":
                src, dst, arrive = srcs[w], _block(lands[w], axis, me, size), _block(lands[w], axis, peer, size)
            elif mode == "scatter":
                src, dst, arrive = _block(srcs[w], axis, peer, size), lands[w].at[me], lands[w].at[peer]
            else:
                src, dst, arrive = srcs[w], lands[w].at[me], lands[w].at[peer]
            sem = w * n_peers + k - 1
            for target, out in ((dst, sends), (arrive, arrivals)):
                out.append(pltpu.make_async_remote_copy(
                    src_ref=src, dst_ref=target, send_sem=send_sems.at[sem], recv_sem=recv_sems.at[sem],
                    device_id=(px, py, pc), device_id_type=MESH))
    return local, sends, arrivals


def _exchange_peers(mode):
    return N_DEV - 1 if mode == "all" else N_CHIPS - 1


def _chip_exchange_start(name, mode, items, after=None):
    n = len(items)
    n_after = 0 if after is None else 1
    axes = [axis for _, axis in items]
    hbm = pl.BlockSpec(memory_space=pltpu.HBM)
    sem = pl.BlockSpec(memory_space=pltpu.SEMAPHORE)
    lands = []
    for a, axis in items:
        shape = list(a.shape)
        if mode == "gather":
            shape[axis] *= N_CHIPS
        elif mode == "scatter":
            shape[axis] //= N_CHIPS
            shape = [N_CHIPS] + shape
        else:
            shape = [N_DEV] + shape
        lands.append(pltpu.with_memory_space_constraint(lax.empty(tuple(shape), a.dtype), pltpu.HBM))

    def body(*refs):
        srcs, land_refs = refs[:n], refs[n:2 * n]
        send_sems, recv_sems, local_sems = refs[2 * n + n_after:2 * n + n_after + 3]
        token = refs[-1]
        local, sends, _ = _chip_exchange_copies(mode, axes, srcs, land_refs, send_sems, recv_sems, local_sems)
        for cp in local + sends:
            cp.start()
        token[...] = jnp.zeros_like(token)

    n_sem = n * _exchange_peers(mode)
    outs = pl.pallas_call(
        body, name=name,
        out_shape=(pltpu.SemaphoreType.DMA((n_sem,)), pltpu.SemaphoreType.DMA((n_sem,)), pltpu.SemaphoreType.DMA((n,)),
                   *[pltpu.HBM(a.shape, a.dtype) for a, _ in items], *[pltpu.HBM(l.shape, l.dtype) for l in lands],
                   jax.ShapeDtypeStruct((8, LANES), F32)),
        in_specs=[hbm] * (2 * n) + [pl.BlockSpec(memory_space=pl.ANY)] * n_after,
        out_specs=(sem, sem, sem, *[hbm] * (2 * n), pl.BlockSpec(memory_space=pltpu.VMEM)),
        input_output_aliases={i: 3 + i for i in range(2 * n)},
        compiler_params=pltpu.CompilerParams(has_side_effects=pltpu.SideEffectType.DATAFLOW_SIDE_EFFECTING),
    )(*[pltpu.with_memory_space_constraint(a, pltpu.HBM) for a, _ in items], *lands, *([after] if n_after else []))
    return (mode, axes, outs[:3], outs[3:3 + n], outs[3 + n:3 + 2 * n]), outs[-1][0:1, 0:1]


def _chip_exchange_wait(name, handle, after):
    mode, axes, sems, srcs, lands = handle
    n = len(axes)
    after = list(after) if isinstance(after, (tuple, list)) else [after]
    hbm = pl.BlockSpec(memory_space=pltpu.HBM)
    sem = pl.BlockSpec(memory_space=pltpu.SEMAPHORE)

    def body(*refs):
        src_refs, land_refs = refs[:n], refs[n:2 * n]
        send_sems, recv_sems, local_sems = refs[2 * n:2 * n + 3]
        local, sends, arrivals = _chip_exchange_copies(mode, axes, src_refs, land_refs, send_sems, recv_sems, local_sems)
        for cp in sends:
            cp.wait_send()
        for cp in arrivals:
            cp.wait_recv()
        for cp in local:
            cp.wait()

    outs = pl.pallas_call(
        body, name=name,
        out_shape=(*[pltpu.HBM(a.shape, a.dtype) for a in srcs], *[pltpu.HBM(l.shape, l.dtype) for l in lands]),
        in_specs=[hbm] * (2 * n) + [sem] * 3 + [pl.BlockSpec(memory_space=pl.ANY)] * len(after), out_specs=[hbm] * (2 * n),
        input_output_aliases={i: i for i in range(2 * n)},
        compiler_params=pltpu.CompilerParams(has_side_effects=pltpu.SideEffectType.DATAFLOW_SIDE_EFFECTING),
    )(*srcs, *lands, *sems, *after)
    return outs[n:]


def _sum_slots(name, slots, tm=256):
    n_slots, r, c = slots.shape
    tm = _tile(r, tm)

    def body(*refs):
        acc = refs[0][...]
        for s_ref in refs[1:n_slots]:
            acc = acc + s_ref[...]
        refs[n_slots][...] = acc

    specs = [pl.BlockSpec((None, tm, c), functools.partial(lambda i, s: (s, i, 0), s=s)) for s in range(n_slots)]
    return pl.pallas_call(
        body, name=name, grid=(r // tm,), in_specs=specs, out_specs=pl.BlockSpec((tm, c), lambda i: (i, 0)),
        out_shape=jax.ShapeDtypeStruct((r, c), F32), compiler_params=_params("parallel"),
    )(*[slots] * n_slots)


def _swap_with_sibling(name, arrays):
    n = len(arrays)
    hbm = pl.BlockSpec(memory_space=pl.ANY)

    def body(*refs):
        ins, outs = refs[:n], refs[n:2 * n]
        send_sems, recv_sems = refs[2 * n:]
        sibling = (lax.axis_index("x"), lax.axis_index("y"), 1 - lax.axis_index("c"))
        copies = [pltpu.make_async_remote_copy(src_ref=ins[w], dst_ref=outs[w], send_sem=send_sems.at[w],
                                               recv_sem=recv_sems.at[w], device_id=sibling, device_id_type=MESH)
                  for w in range(n)]
        for cp in copies:
            cp.start()
        for cp in copies:
            cp.wait()

    return pl.pallas_call(
        body, name=name, in_specs=[hbm] * n, out_specs=[hbm] * n,
        out_shape=[jax.ShapeDtypeStruct(a.shape, a.dtype) for a in arrays],
        scratch_shapes=[pltpu.SemaphoreType.DMA((n,)), pltpu.SemaphoreType.DMA((n,))],
    )(*arrays)


def _adamw(g, w, m, v):
    m = ADAM_B1 * m + (1.0 - ADAM_B1) * g
    v = ADAM_B2 * v + (1.0 - ADAM_B2) * jnp.square(g)
    m_hat = m / (1.0 - ADAM_B1 ** ADAM_STEP)
    v_hat = v / (1.0 - ADAM_B2 ** ADAM_STEP)
    delta = -ADAM_LR * (m_hat / (jnp.sqrt(v_hat) + ADAM_EPS) + ADAM_WD * w)
    return delta, m, v


def _adamw_small(grads, ws, ms, vs):
    n = len(ws)
    ins = [*grads, *ws, *ms, *vs]
    out_shape = [jax.ShapeDtypeStruct(w.shape, F32) for _ in range(3) for w in ws]
    hbm = pl.BlockSpec(memory_space=pltpu.HBM)

    def body(*refs):
        n_in, n_out = len(ins), len(out_shape)
        in_refs, out_refs = refs[:n_in], refs[n_in:n_in + n_out]
        in_bufs, out_bufs = refs[n_in + n_out:2 * n_in + n_out], refs[2 * n_in + n_out:2 * (n_in + n_out)]
        in_sems, out_sems = refs[2 * (n_in + n_out):]
        loads = [pltpu.make_async_copy(in_refs[i], in_bufs[i], in_sems.at[i]) for i in range(n_in)]
        for cp in loads:
            cp.start()
        for cp in loads:
            cp.wait()
        for i in range(n):
            g, w, m, v = (in_bufs[k * n + i][...] for k in range(4))
            for k, val in enumerate(_adamw(g, w, m, v)):
                out_bufs[k * n + i][...] = val
        stores = [pltpu.make_async_copy(out_bufs[i], out_refs[i], out_sems.at[i]) for i in range(n_out)]
        for cp in stores:
            cp.start()
        for cp in stores:
            cp.wait()

    outs = pl.pallas_call(
        body, name="adamw_small", in_specs=[hbm] * len(ins), out_specs=[hbm] * len(out_shape), out_shape=out_shape,
        scratch_shapes=([pltpu.VMEM(a.shape, F32) for a in ins] + [pltpu.VMEM(s.shape, F32) for s in out_shape]
                        + [pltpu.SemaphoreType.DMA((len(ins),)), pltpu.SemaphoreType.DMA((len(out_shape),))]),
        compiler_params=pltpu.CompilerParams(vmem_limit_bytes=VMEM_LIMIT),
    )(*ins)
    return outs[:n], outs[n:2 * n], outs[2 * n:]


def kernel(x, norm1_g, w_in, q_norm_g, k_norm_g, ssm_lambda_re, ssm_lambda_im, ssm_log_dt, ssm_b_re, ssm_b_im, ssm_c_re, ssm_c_im, ssm_d, w_glu, b_glu, attn_out_g, ssm_out_g, w_out, norm2_g, w_mlp_in, w_mlp_out, loss_target, m_norm1_g, m_w_in, m_q_norm_g, m_k_norm_g, m_ssm_lambda_re, m_ssm_lambda_im, m_ssm_log_dt, m_ssm_b_re, m_ssm_b_im, m_ssm_c_re, m_ssm_c_im, m_ssm_d, m_w_glu, m_b_glu, m_attn_out_g, m_ssm_out_g, m_w_out, m_norm2_g, m_w_mlp_in, m_w_mlp_out, v_norm1_g, v_w_in, v_q_norm_g, v_k_norm_g, v_ssm_lambda_re, v_ssm_lambda_im, v_ssm_log_dt, v_ssm_b_re, v_ssm_b_im, v_ssm_c_re, v_ssm_c_im, v_ssm_d, v_w_glu, v_b_glu, v_attn_out_g, v_ssm_out_g, v_w_out, v_norm2_g, v_w_mlp_in, v_w_mlp_out):
    batch, seq, d_model = x.shape
    tokens = batch * seq
    sb_width = w_in.shape[1]
    n_features = d_model

    big = [("w_in", w_in, m_w_in, v_w_in, 1), ("w_glu", w_glu, m_w_glu, v_w_glu, 0),
           ("w_out", w_out, m_w_out, v_w_out, 0), ("w_mlp_in", w_mlp_in, m_w_mlp_in, v_w_mlp_in, 1),
           ("w_mlp_out", w_mlp_out, m_w_mlp_out, v_w_mlp_out, 0)]
    small = [("norm1_g", norm1_g, m_norm1_g, v_norm1_g), ("q_norm_g", q_norm_g, m_q_norm_g, v_q_norm_g),
             ("k_norm_g", k_norm_g, m_k_norm_g, v_k_norm_g),
             ("ssm_lambda_re", ssm_lambda_re, m_ssm_lambda_re, v_ssm_lambda_re),
             ("ssm_lambda_im", ssm_lambda_im, m_ssm_lambda_im, v_ssm_lambda_im),
             ("ssm_log_dt", ssm_log_dt, m_ssm_log_dt, v_ssm_log_dt),
             ("ssm_b_re", ssm_b_re, m_ssm_b_re, v_ssm_b_re), ("ssm_b_im", ssm_b_im, m_ssm_b_im, v_ssm_b_im),
             ("ssm_c_re", ssm_c_re, m_ssm_c_re, v_ssm_c_re), ("ssm_c_im", ssm_c_im, m_ssm_c_im, v_ssm_c_im),
             ("ssm_d", ssm_d, m_ssm_d, v_ssm_d), ("b_glu", b_glu, m_b_glu, v_b_glu),
             ("attn_out_g", attn_out_g, m_attn_out_g, v_attn_out_g), ("ssm_out_g", ssm_out_g, m_ssm_out_g, v_ssm_out_g),
             ("norm2_g", norm2_g, m_norm2_g, v_norm2_g)]

    gather_in, tok_in = _chip_exchange_start("gather_w_in_start", "gather", [(w_in.astype(BF16), 1)])
    gather_mix, tok_mix_w = _chip_exchange_start(
        "gather_mix_start", "gather", [(w.astype(BF16), axis) for _, w, _, _, axis in big[1:3]], after=tok_in)
    gather_mlp, tok_rest = _chip_exchange_start(
        "gather_mlp_start", "gather", [(w.astype(BF16), axis) for _, w, _, _, axis in big[3:]], after=tok_mix_w)

    x2 = x.reshape(tokens, d_model)
    tgt2 = loss_target.reshape(tokens, d_model)
    g1, g2 = norm1_g[None, :], norm2_g[None, :]
    g_attn, g_ssm, bias_glu = attn_out_g[None, :], ssm_out_g[None, :], b_glu[None, :]
    heads = sb_width // HEAD_DIM
    qk_scale = 1.0 / math.sqrt(HEAD_DIM)
    gq, gk = (jnp.tile(q_norm_g, heads) * qk_scale)[None, :], jnp.tile(k_norm_g, heads)[None, :]
    lane_head = jnp.arange(LANES) // HEAD_DIM
    ones_blocks = (lane_head[:, None] == lane_head[None, :]).astype(F32)

    (xn,) = _rowwise("norm1", _rms, [x2], [g1 + tok_rest], [(d_model, BF16)], tm=512)
    s5_in = _s5_operator_inputs(ssm_lambda_re, ssm_lambda_im, ssm_log_dt, ssm_b_re, ssm_b_im, ssm_c_re, ssm_c_im, ssm_d)
    kt_row, b_mat, c_mat, la, lb = _s5_operators_call("s5_operators", s5_in)
    (wf_in,) = _chip_exchange_wait("gather_w_in_wait", gather_in, [xn, b_mat, c_mat])
    def proj_head(acc, gq_, gk_, ones):
        q, k, v = (acc[:, i * sb_width:(i + 1) * sb_width] for i in range(3))
        return acc, _head_rms(q, gq_, ones), _head_rms(k, gk_, ones), v

    proj, qn, kn, vb = _mm("proj_in", xn, wf_in, "nn", epilogue=proj_head, full_rows=True, max_tm=512,
                           extras=[(gq, "row"), (gk, "row"), (ones_blocks, "whole")],
                           out_dtypes=(F32, (BF16, sb_width), (BF16, sb_width), (BF16, sb_width)))
    sb, c_tot = _attn_fwd(qn, kn, vb, batch=batch, seq=seq, bq=ATTN_BQ, bk=ATTN_BK)
    ug = _tokens_to_groups("u_to_groups", proj, 3, sb_width)
    yg, xin = _s5_fwd(ug, kt_row, b_mat, c_mat, la, lb, batch=batch)
    y_ssm = _groups_to_tokens("y_to_tokens", yg)

    wf_glu, wf_out = _chip_exchange_wait("gather_mix_wait", gather_mix, [y_ssm, sb])
    (gate_pre,) = _mm("glu_gate", y_ssm, wf_glu, "nn", a_fn=_gelu, extras=[(bias_glu, "row")],
                      epilogue=lambda acc, b: acc + b)
    (mixed,) = _rowwise("mix_norm", _mixed, [sb, y_ssm, gate_pre], [g_attn, g_ssm], [(2 * sb_width, BF16)], tm=512)
    def out_head(acc, r, g):
        h = acc + r
        return h, _rms(h, g)

    h1, hn = _mm("proj_out", mixed, wf_out, "nn", extras=[(x2, "tile"), (g2, "row")], epilogue=out_head,
                 out_dtypes=(F32, BF16), full_rows=True)
    def mlp_act(acc):
        r = jnp.maximum(acc, 0.0)
        return r * r, r

    wf_mlp_in, wf_mlp_out = _chip_exchange_wait("gather_mlp_wait", gather_mlp, [h1, hn])
    act, act_root = _mm("mlp_in", hn, wf_mlp_in, "nn", epilogue=mlp_act, out_dtypes=(BF16, BF16))
    inv_n = 1.0 / n_features

    def loss_head(acc, r, t):
        d = ((acc + r) - t) * inv_n
        return d, d, jnp.sum(d * d, keepdims=True) * (0.5 * n_features)

    dy, dy_b, loss_tiles = _mm("mlp_out_loss", act, wf_mlp_out, "nn", extras=[(h1, "tile"), (tgt2, "tile")],
                               epilogue=loss_head, out_dtypes=(F32, BF16), tile_sums=("scalar",))
    loss_part = jnp.sum(loss_tiles)

    (dw_mlp_out,) = _mm("dw_mlp_out", act, dy_b, "tn")
    (dpre,) = _mm("d_mlp_act", dy_b, wf_mlp_out, "nt", extras=[(act_root, "tile")],
                  epilogue=lambda acc, r: acc * (2.0 * r.astype(F32)), out_dtypes=(BF16,))
    (dw_mlp_in,) = _mm("dw_mlp_in", hn, dpre, "tn")
    scatter_mlp, tok_mlp = _chip_exchange_start("scatter_mlp_start", "scatter", [(dw_mlp_in, 1), (dw_mlp_out, 0)])
    def norm_bwd(dn, res, hx, g):
        _, vjp = jax.vjp(_rms, hx, g)
        dh, dg = vjp(dn)
        return res + dh, dg

    dh1, dg_tiles = _mm("d_norm2_in", dpre, wf_mlp_in, "nt", extras=[(dy, "tile"), (h1, "tile"), (g2 + tok_mlp, "row")],
                        epilogue=norm_bwd, tile_sums=("row",), full_rows=True)
    dg_norm2 = jnp.sum(dg_tiles, axis=0, keepdims=True)
    (dw_out,) = _mm("dw_out", mixed, dh1, "tn")

    def mixed_bwd(dm, sb_, ys, gp, ga, gs):
        _, vjp = jax.vjp(lambda a, act, b, c, d: jnp.concatenate(
            [_rms(a, c), _rms(act * jax.nn.sigmoid(b), d)], axis=-1), sb_, _gelu(ys), gp, ga, gs)
        dsb_, dact, dgp_, dga, dgs = vjp(dm)
        return dsb_, dgp_, dact, dga, dgs, jnp.sum(dgp_, axis=0, keepdims=True)

    dsb, dgate_pre, dact_part, *gain_tiles = _mm(
        "d_mixed", dh1, wf_out, "nt", epilogue=mixed_bwd, full_rows=True, max_tm=256,
        extras=[(sb, "tile"), (y_ssm, "tile"), (gate_pre, "tile"), (g_attn, "row"), (g_ssm, "row")],
        out_dtypes=((F32, sb_width), (BF16, sb_width), (F32, sb_width)), tile_sums=(("row", sb_width),) * 3)
    dg_attn, dg_ssm, db_glu = (jnp.sum(t, axis=0, keepdims=True) for t in gain_tiles)

    def gelu_bwd(acc, part, ys):
        _, vjp = jax.vjp(_gelu, ys)
        return vjp(acc + part)[0]

    (dy_ssm,) = _mm("d_glu_in", dgate_pre, wf_glu, "nt", extras=[(dact_part, "tile"), (y_ssm, "tile")], epilogue=gelu_bwd)
    (dw_glu,) = _mm("dw_glu", y_ssm, dgate_pre, "tn", a_fn=_gelu)
    scatter_mix, tok_mix = _chip_exchange_start("scatter_mix_start", "scatter", [(dw_glu, 0), (dw_out, 0)])

    dug, dkt_row, db_mat, dc_mat, dla, dlb = _s5_bwd(ug, _tokens_to_groups("dy_to_groups", dy_ssm, 0, sb_width), xin,
                                                     kt_row, b_mat, c_mat, la, lb + tok_mix, batch=batch)
    du = _groups_to_tokens("du_to_tokens", dug)
    d_in = _s5_operators_call("s5_operators_bwd", s5_in, (dkt_row, db_mat, dc_mat, dla, dlb))
    ds5 = [d_in[0][:, 0, :] + d_in[2][:, :, 0], d_in[1][:, 0, :] + d_in[3][:, :, 0], d_in[4][:, 0, 0],
           d_in[5].transpose(0, 2, 1), d_in[6].transpose(0, 2, 1), d_in[7].transpose(0, 2, 1), d_in[8].transpose(0, 2, 1),
           d_in[9][:, 0, :]]

    def pack(parts):
        flat = jnp.concatenate([p.reshape(-1) for p in parts])
        rows = -(-flat.shape[0] // (8 * LANES)) * 8
        return jnp.pad(flat, (0, rows * LANES - flat.shape[0])).reshape(rows, LANES)

    def unpack(packed, names):
        flat, out, off = packed.reshape(-1), {}, 0
        for name in names:
            shape = small_shapes[name]
            size = math.prod(shape)
            out[name] = flat[off:off + size].reshape(shape)
            off += size
        return out, flat[off]

    small_shapes = {name: w.shape for name, w, _, _ in small}
    early_names = ["ssm_lambda_re", "ssm_lambda_im", "ssm_log_dt", "ssm_b_re", "ssm_b_im", "ssm_c_re", "ssm_c_im", "ssm_d",
                   "b_glu", "attn_out_g", "ssm_out_g", "norm2_g"]
    late_names = ["norm1_g", "q_norm_g", "k_norm_g"]
    early = pack([*ds5, db_glu[0], dg_attn[0], dg_ssm[0], dg_norm2[0], loss_part])
    early_exchange, _ = _chip_exchange_start("small_early_start", "all", [(early, 0)])

    dqn, dkn, dv = _attn_bwd(qn, kn, vb, c_tot, dsb, batch=batch, seq=seq, bq=ATTN_BQ, bk=ATTN_BK,
                             after=early_exchange[3][0])
    (early_slots,) = _chip_exchange_wait("small_early_wait", early_exchange, dqn)
    small_g, loss = unpack(_sum_slots("sum_small_early", early_slots), early_names)

    def qk_bwd(q, k, dq_, dk_, dv_, du_, gq_, gk_, ones):
        _, vjp_q = jax.vjp(lambda a, g: _head_rms(a, g, ones), q, gq_)
        _, vjp_k = jax.vjp(lambda a, g: _head_rms(a, g, ones), k, gk_)
        dq, dgq = vjp_q(dq_)
        dk, dgk = vjp_k(dk_)
        return jnp.concatenate([dq, dk, dv_, du_], axis=1), dgq, dgk

    dproj, dgq, dgk = _rowwise("qk_norm_bwd", qk_bwd, [(proj, sb_width, 0), (proj, sb_width, 1), dqn, dkn, dv, du],
                               [gq, gk, ones_blocks], [(4 * sb_width, BF16)], [(1, sb_width)] * 2, tm=512)
    (dw_in,) = _mm("dw_in", xn, dproj, "tn")
    scatter_in, tok_w_in = _chip_exchange_start("scatter_in_start", "scatter", [(dw_in, 1)])
    dx, dg_tiles = _mm("d_norm1_in", dproj, wf_in, "nt", extras=[(dh1, "tile"), (x2, "tile"), (g1 + tok_w_in, "row")],
                       epilogue=norm_bwd, tile_sums=("row",), full_rows=True)
    dg_norm1 = jnp.sum(dg_tiles, axis=0, keepdims=True)

    late = pack([dg_norm1[0], dgq.reshape(heads, HEAD_DIM).sum(0) * qk_scale, dgk.reshape(heads, HEAD_DIM).sum(0),
                 jnp.zeros((1,), F32)])
    late_exchange, _ = _chip_exchange_start("small_late_start", "all", [(late, 0)])

    def adam_big(sa, sb_, w, m, v):
        g = sa + sb_
        delta, m, v = _adamw(g, w, m, v)
        return g, delta, m, v

    def reduce_and_update(tag, params, slots):
        mine = [_sum_slots("sum_" + name, s) for s, (name, *_rest) in zip(slots, params)]
        theirs = _swap_with_sibling("swap_" + tag, mine)
        return {name: _rowwise("adamw_" + name, adam_big, [sa, sb_, w, m, v], [], [(w.shape[1], F32)] * 4)
                for (name, w, m, v, _), sa, sb_ in zip(params, mine, theirs)}

    started = late_exchange[3][0]
    slots_mlp_in, slots_mlp_out = _chip_exchange_wait("scatter_mlp_wait", scatter_mlp, started)
    slots_glu, slots_out = _chip_exchange_wait("scatter_mix_wait", scatter_mix, started)
    big_out = reduce_and_update("rest", big[1:], [slots_glu, slots_out, slots_mlp_in, slots_mlp_out])

    (late_slots,) = _chip_exchange_wait("small_late_wait", late_exchange, big_out["w_mlp_out"][3])
    reduced = _sum_slots("sum_small_late", late_slots)
    small_g.update(unpack(reduced, late_names)[0])
    narrow = {name for name, w, _, _ in small if w.ndim == 3 and w.shape[2] < w.shape[1]}

    def flip(a, name):
        return jnp.swapaxes(a, 1, 2) if name in narrow else a

    small_upd = _adamw_small([flip(small_g[name], name) for name, *_ in small], [flip(w, name) for name, w, _, _ in small],
                             [flip(m, name) for name, _, m, _ in small], [flip(v, name) for name, _, _, v in small])
    small_out = [small_g] + [{name: flip(small_upd[kind][i], name) for i, (name, *_) in enumerate(small)}
                             for kind in range(3)]

    (slots_in,) = _chip_exchange_wait("scatter_in_wait", scatter_in, reduced)
    big_out.update(reduce_and_update("w_in", big[:1], [slots_in]))
    names = ["norm1_g", "w_in", "q_norm_g", "k_norm_g", "ssm_lambda_re", "ssm_lambda_im", "ssm_log_dt", "ssm_b_re",
             "ssm_b_im", "ssm_c_re", "ssm_c_im", "ssm_d", "w_glu", "b_glu", "attn_out_g", "ssm_out_g", "w_out",
             "norm2_g", "w_mlp_in", "w_mlp_out"]
    outs = [loss, dx.reshape(batch, seq, d_model)]
    for kind in range(4):
        for name in names:
            outs.append(big_out[name][kind] if name in big_out else small_out[kind][name])
    return tuple(outs)
```

```python
import functools
import math

import jax
import jax.numpy as jnp
from jax import lax
from jax.experimental import pallas as pl
from jax.experimental.pallas import tpu as pltpu

F32 = jnp.float32
BF16 = jnp.bfloat16
F32_DOT = lax.Precision.HIGH
MESH = pl.DeviceIdType.MESH

RMS_EPS = 1e-6
HEAD_DIM = 64
SSM_GROUP = 16
SSM_CHUNK = 16
LANES = 128
N_CHIPS = 4
N_DEV = 8
VMEM_LIMIT = 48 * 1024 * 1024

ADAM_LR = 0.001
ADAM_B1 = 0.9
ADAM_B2 = 0.999
ADAM_EPS = 1e-08
ADAM_WD = 0.01
ADAM_STEP = 10


def _tile(n, pref):
    t = min(n, pref)
    while n % t:
        t //= 2
    return t


def _params(*sem):
    return pltpu.CompilerParams(dimension_semantics=sem, vmem_limit_bytes=VMEM_LIMIT)


_DIMS = {"nn": (((1,), (0,)), ((), ())), "nt": (((1,), (1,)), ((), ())), "tn": (((0,), (0,)), ((), ()))}


MM_VMEM_BUDGET = 40 * 1024 * 1024


def _mm_tiles(m, n, k, a_bytes, b_bytes, tile_bytes, full_rows=False, max_tm=1024):
    best = None
    for tk in [t for t in (k, k // 2, k // 4, k // 8) if t >= 256 or t == k]:
        for tm in [t for t in (1024, 512, 256, 128) if t <= min(m, max_tm) and m % t == 0]:
            for tn in [n] if full_rows else [t for t in (1024, 512, 256, 128) if t <= n and n % t == 0]:
                need = 2 * (tm * tk * a_bytes + tk * tn * b_bytes) + 2 * tm * tn * tile_bytes + (tm * tn * 4 if tk < k else 0)
                if need > MM_VMEM_BUDGET:
                    continue
                traffic = m * k * a_bytes * (1 if tk == k else n // tn) + k * n * b_bytes * (1 if n == tn and tk == k else m // tm)
                key = (tk < k, traffic, -tm * tn)
                if best is None or key < best[0]:
                    best = (key, (tm, tn, tk))
    return best[1]


def _mm(name, a, b, mode, *, a_fn=None, extras=(), epilogue=None, out_dtypes=(F32,), tile_sums=(), full_rows=False,
        max_tm=1024):
    if mode == "nn":
        (m, k), n = a.shape, b.shape[1]
    elif mode == "nt":
        (m, k), n = a.shape, b.shape[0]
    else:
        (k, m), n = a.shape, b.shape[1]
    outs_spec = [(d, n) if not isinstance(d, tuple) else d for d in out_dtypes]
    sums_spec = [(s, n) if not isinstance(s, tuple) else s for s in tile_sums]
    assert full_rows or all(w == n for _, w in outs_spec + sums_spec) and all(e.shape[1] == n for e, _ in extras)
    tile_bytes = (sum(e.dtype.itemsize * e.shape[1] for e, kind in extras if kind == "tile")
                  + sum(jnp.dtype(d).itemsize * w for d, w in outs_spec)) // n + 1
    tm, tn, tk = _mm_tiles(m, n, k, a.dtype.itemsize, b.dtype.itemsize, tile_bytes, full_rows, max_tm)
    nk = k // tk
    ne, nout = len(extras), len(out_dtypes)
    dims = _DIMS[mode]

    def width_spec(rows, w):
        if w == n:
            return pl.BlockSpec((rows, tn), (lambda i, j, kk: (i, j)) if rows != 1 else (lambda i, j, kk: (0, j)))
        return pl.BlockSpec((rows, w), (lambda i, j, kk: (i, 0)) if rows != 1 else (lambda i, j, kk: (0, 0)))

    def body(a_ref, b_ref, *rest):
        ex, outs, sums = rest[:ne], rest[ne:ne + nout], rest[ne + nout:ne + nout + len(tile_sums)]
        at = a_ref[...]
        if a_fn is not None:
            at = a_fn(at)
        p = lax.dot_general(at.astype(BF16), b_ref[...].astype(BF16), dims, preferred_element_type=F32)

        def finish(r):
            if epilogue is not None:
                r = epilogue(r, *[e[...] for e in ex])
            if not isinstance(r, (tuple, list)):
                r = (r,)
            for o, v in zip(outs, r[:nout]):
                o[...] = v.astype(o.dtype)
            for o, v, (kind, _) in zip(sums, r[nout:], sums_spec):
                first = lax.broadcasted_iota(jnp.int32, o.shape, 0) == 0
                if kind == "scalar":
                    first &= lax.broadcasted_iota(jnp.int32, o.shape, 1) == 0
                o[...] = jnp.where(first, v, 0.0)

        if nk == 1:
            finish(p)
        else:
            acc = rest[ne + nout + len(tile_sums)]
            kk = pl.program_id(2)

            @pl.when(kk == 0)
            def _():
                acc[...] = p

            @pl.when(kk > 0)
            def _():
                acc[...] += p

            @pl.when(kk == nk - 1)
            def _():
                finish(acc[...])

    if mode == "tn":
        a_spec = pl.BlockSpec((tk, tm), lambda i, j, kk: (kk, i))
    else:
        a_spec = pl.BlockSpec((tm, tk), lambda i, j, kk: (i, kk))
    if mode == "nt":
        b_spec = pl.BlockSpec((tn, tk), lambda i, j, kk: (j, kk))
    else:
        b_spec = pl.BlockSpec((tk, tn), lambda i, j, kk: (kk, j))
    ex_specs = [pl.BlockSpec(e.shape, lambda i, j, kk: (0, 0)) if kind == "whole"
                else width_spec(tm if kind == "tile" else 1, e.shape[1]) for e, kind in extras]
    return pl.pallas_call(
        body, name=name, grid=(m // tm, n // tn, nk),
        in_specs=[a_spec, b_spec] + ex_specs,
        out_specs=([width_spec(tm, w) for _, w in outs_spec]
                   + [pl.BlockSpec((8, LANES), lambda i, j, kk: (i, j)) if kind == "scalar" else width_spec(8, w)
                      for kind, w in sums_spec]),
        out_shape=([jax.ShapeDtypeStruct((m, w), dt) for dt, w in outs_spec]
                   + [jax.ShapeDtypeStruct((m // tm * 8, n // tn * LANES if kind == "scalar" else w), F32)
                      for kind, w in sums_spec]),
        scratch_shapes=[pltpu.VMEM((tm, tn), F32)] if nk > 1 else [],
        compiler_params=_params("parallel", "parallel", "arbitrary"),
    )(a, b, *[e for e, _ in extras])


def _rowwise(name, fn, rows, consts, row_outs, acc_outs=(), tm=256):
    norm = [r if isinstance(r, tuple) else (r, r.shape[1], 0) for r in rows]
    t = norm[0][0].shape[0]
    tm = _tile(t, tm)
    nr, nc, no = len(norm), len(consts), len(row_outs)

    def body(*refs):
        outs = fn(*[r[...] for r in refs[:nr + nc]])
        if not isinstance(outs, (tuple, list)):
            outs = (outs,)
        o_refs, a_refs = refs[nr + nc:nr + nc + no], refs[nr + nc + no:]
        for r, v in zip(o_refs, outs[:no]):
            r[...] = v.astype(r.dtype)
        if a_refs:
            i = pl.program_id(0)

            @pl.when(i == 0)
            def _():
                for r, v in zip(a_refs, outs[no:]):
                    r[...] = v

            @pl.when(i > 0)
            def _():
                for r, v in zip(a_refs, outs[no:]):
                    r[...] += v

    in_specs = [pl.BlockSpec((tm, w), functools.partial(lambda i, cb: (i, cb), cb=cb)) for _, w, cb in norm]
    in_specs += [pl.BlockSpec(c.shape, functools.partial(lambda i, nd: (0,) * nd, nd=c.ndim)) for c in consts]
    out_specs = [pl.BlockSpec((tm, w), lambda i: (i, 0)) for w, _ in row_outs]
    out_specs += [pl.BlockSpec(s, functools.partial(lambda i, nd: (0,) * nd, nd=len(s))) for s in acc_outs]
    out_shape = [jax.ShapeDtypeStruct((t, w), dt) for w, dt in row_outs]
    out_shape += [jax.ShapeDtypeStruct(s, F32) for s in acc_outs]
    return pl.pallas_call(
        body, name=name, grid=(t // tm,), in_specs=in_specs, out_specs=out_specs, out_shape=out_shape,
        compiler_params=_params("arbitrary"),
    )(*[r[0] for r in norm], *consts)


def _rms(x, g):
    return x * lax.rsqrt(jnp.mean(x * x, axis=-1, keepdims=True) + RMS_EPS) * g


@jax.custom_vjp
def _head_sums(x, ones_blocks):
    parts = [jnp.dot(x[:, j:j + LANES], ones_blocks, precision=F32_DOT, preferred_element_type=F32)
             for j in range(0, x.shape[1], LANES)]
    return jnp.concatenate(parts, axis=1)


_head_sums.defvjp(lambda x, ones_blocks: (_head_sums(x, ones_blocks), ones_blocks),
                  lambda ones_blocks, ct: (_head_sums(ct, ones_blocks), None))


def _head_rms(x, g, ones_blocks):
    return x * lax.rsqrt(_head_sums(x * x, ones_blocks) * (1.0 / HEAD_DIM) + RMS_EPS) * g


def _gelu(x):
    return x * (0.5 * (1.0 + jnp.tanh(math.sqrt(2.0 / math.pi) * (x + 0.044715 * (x * x * x)))))


def _mixed(sb, y_ssm, gate_pre, g_attn, g_ssm):
    ssm = _gelu(y_ssm) * jax.nn.sigmoid(gate_pre)
    return jnp.concatenate([_rms(sb, g_attn), _rms(ssm, g_ssm)], axis=-1)


def _softplus(z):
    return jnp.maximum(z, 0.0) + jnp.log(1.0 + jnp.exp(-jnp.abs(z)))


def _running_sums(x, tri):
    return jnp.dot(x.astype(BF16), tri, preferred_element_type=F32)


def _dot_nt(a, b, **kw):
    return lax.dot_general(a, b, _DIMS["nt"], preferred_element_type=F32, **kw)


def _dot_tn(a, b, **kw):
    return lax.dot_general(a, b, _DIMS["tn"], preferred_element_type=F32, **kw)


ATTN_BQ, ATTN_BK = 2048, 256
HEAD_LANES = tuple(slice(h * HEAD_DIM, (h + 1) * HEAD_DIM) for h in range(LANES // HEAD_DIM))


def _attn_fwd(qs, kn, v, *, batch, seq, bq, bk):
    width = qs.shape[1]
    bq = _tile(seq, bq)
    bk = _tile(bq, bk)
    nq, kpq = seq // bq, bq // bk

    def body(q_ref, k_ref, v_ref, o_ref, c_ref):
        row = lax.broadcasted_iota(jnp.int32, (bq, bk), 0)
        col = lax.broadcasted_iota(jnp.int32, (bq, bk), 1)
        tri = (lax.broadcasted_iota(jnp.int32, (bk, bk), 0) >= lax.broadcasted_iota(jnp.int32, (bk, bk), 1)).astype(BF16)

        def q_block(qi, carry):
            r0 = pl.multiple_of(qi * bq, bq)
            qh = [q_ref[pl.ds(r0, bq), ln] for ln in HEAD_LANES]

            def tile(k0, state, top=0):
                diag = top is not None
                top = top or 0
                msk = (col < row)[:bq - top] if diag else None
                new = []
                for h, ln in enumerate(HEAD_LANES):
                    o, c = state[2 * h], state[2 * h + 1]
                    z = _dot_nt(qh[h][top:], k_ref[pl.ds(k0, bk), ln])
                    sp = _softplus(z)
                    if diag:
                        sp = jnp.where(msk, sp, 0.0)
                    r = _running_sums(sp, tri)
                    a = jnp.exp(z - r - c[top:])
                    if diag:
                        a = jnp.where(msk, a, 0.0)
                    o_new = o[top:] + jnp.dot(a.astype(BF16), v_ref[pl.ds(k0, bk), ln], preferred_element_type=F32)
                    c_new = c[top:] + r[:, 0:1]
                    if top:
                        o_new, c_new = jnp.concatenate([o[:top], o_new]), jnp.concatenate([c[:top], c_new])
                    new += [o_new, c_new]
                return tuple(new)

            state = (jnp.zeros((bq, HEAD_DIM), F32), jnp.zeros((bq, 1), F32)) * len(HEAD_LANES)
            for d in reversed(range(kpq)):
                state = tile(pl.multiple_of(r0 + d * bk, bk), state, top=d * bk)
            state = lax.fori_loop(0, qi * kpq, lambda it, st: tile(pl.multiple_of(r0 - (it + 1) * bk, bk), st, None),
                                  state)
            for h, ln in enumerate(HEAD_LANES):
                o_ref[pl.ds(r0, bq), ln] = state[2 * h]
                c_ref[pl.ds(r0, bq), ln] = jnp.broadcast_to(state[2 * h + 1], (bq, HEAD_DIM))
            return carry

        lax.fori_loop(0, nq, q_block, 0)

    spec = pl.BlockSpec((seq, LANES), lambda b, h: (b, h))
    shape = jax.ShapeDtypeStruct((batch * seq, width), F32)
    return pl.pallas_call(
        body, name="attn_fwd", grid=(batch, width // LANES), in_specs=[spec, spec, spec], out_specs=[spec, spec],
        out_shape=[shape, shape], compiler_params=_params("parallel", "parallel"),
    )(qs, kn, v)


def _attn_bwd(qs, kn, v, c_tot, do, *, batch, seq, bq, bk, after):
    width = qs.shape[1]
    bq = _tile(seq, bq)
    bk = _tile(bq, bk)
    nq, kpq = seq // bq, bq // bk

    def body(q_ref, k_ref, v_ref, c_ref, do_ref, after_ref, dq_ref, dk_ref, dv_ref):
        row = lax.broadcasted_iota(jnp.int32, (bq, bk), 0)
        col = lax.broadcasted_iota(jnp.int32, (bq, bk), 1)
        sq_row = lax.broadcasted_iota(jnp.int32, (bk, bk), 0)
        sq_col = lax.broadcasted_iota(jnp.int32, (bk, bk), 1)
        tri = (sq_row >= sq_col).astype(BF16)
        tri_t = (sq_row <= sq_col).astype(BF16)
        dk_ref[...] = jnp.zeros_like(dk_ref)
        dv_ref[...] = jnp.zeros_like(dv_ref)

        def q_block(qi, carry):
            r0 = pl.multiple_of(qi * bq, bq)
            qh = [q_ref[pl.ds(r0, bq), ln] for ln in HEAD_LANES]
            d_out = [do_ref[pl.ds(r0, bq), ln].astype(BF16) for ln in HEAD_LANES]
            c_all = [c_ref[pl.ds(r0, bq), ln][:, 0:1] for ln in HEAD_LANES]

            def tile(k0, state, top=0):
                diag = top is not None
                top = top or 0
                last = diag and top == bq - bk
                msk = (col < row)[:bq - top] if diag else None
                new = []
                for h, ln in enumerate(HEAD_LANES):
                    c_left, g_left, dq = state[3 * h:3 * h + 3]
                    q, d_o = qh[h][top:], d_out[h][top:]
                    k = k_ref[pl.ds(k0, bk), ln]
                    z = _dot_nt(q, k)
                    e = jnp.exp(-jnp.abs(z))
                    sp = jnp.maximum(z, 0.0) + jnp.log(1.0 + e)
                    sig = jnp.exp(z - sp)
                    if diag:
                        sp = jnp.where(msk, sp, 0.0)
                    r = _running_sums(sp, tri)
                    c_new = c_left[top:] + r[:, 0:1]
                    a = jnp.exp(z - r - (0.0 if last else c_all[h][top:] - c_new))
                    if diag:
                        a = jnp.where(msk, a, 0.0)
                    g = a * _dot_nt(d_o, v_ref[pl.ds(k0, bk), ln])
                    pg = _running_sums(g, tri_t)
                    dz = g - sig * (g_left[top:] + pg)
                    if diag:
                        dz = jnp.where(msk, dz, 0.0)
                    dz = dz.astype(BF16)
                    dk_ref[pl.ds(k0, bk), ln] += _dot_tn(dz, q)
                    dv_ref[pl.ds(k0, bk), ln] += _dot_tn(a.astype(BF16), d_o)
                    g_new = g_left[top:] + pg[:, bk - 1:bk]
                    dq_new = dq[top:] + jnp.dot(dz, k, preferred_element_type=F32)
                    if top:
                        c_new = jnp.concatenate([c_left[:top], c_new])
                        g_new = jnp.concatenate([g_left[:top], g_new])
                        dq_new = jnp.concatenate([dq[:top], dq_new])
                    new += [c_new, g_new, dq_new]
                return tuple(new)

            zero = jnp.zeros((bq, 1), F32)
            init = (zero, zero, jnp.zeros((bq, HEAD_DIM), F32)) * len(HEAD_LANES)
            state = lax.fori_loop(0, qi * kpq, lambda it, st: tile(pl.multiple_of(it * bk, bk), st, None), init)
            for d in range(kpq):
                state = tile(pl.multiple_of(r0 + d * bk, bk), state, top=d * bk)
            for h, ln in enumerate(HEAD_LANES):
                dq_ref[pl.ds(r0, bq), ln] = state[3 * h + 2]
            return carry

        lax.fori_loop(0, nq, q_block, 0)

    spec = pl.BlockSpec((seq, LANES), lambda b, h: (b, h))
    shape = jax.ShapeDtypeStruct((batch * seq, width), F32)
    return pl.pallas_call(
        body, name="attn_bwd", grid=(batch, width // LANES),
        in_specs=[spec] * 5 + [pl.BlockSpec(memory_space=pl.ANY)], out_specs=[spec] * 3,
        out_shape=[shape] * 3, compiler_params=_params("parallel", "parallel"),
    )(qs, kn, v, c_tot, do, after)


def _pattern(rows, cols, hit):
    r, c = lax.broadcasted_iota(jnp.int32, (rows, cols), 0), lax.broadcasted_iota(jnp.int32, (rows, cols), 1)
    return hit(r, c).astype(F32)


def _s5_group_operators(lr_r, li_r, lr_c, li_c, log_dt, bt_re, bt_im, ct_re, ct_im, d_row):
    cs = SSM_CHUNK
    n_ch, n_state = bt_re.shape
    width = cs * n_ch
    dt = jnp.exp(log_dt)

    def spread(x, pattern):
        return jnp.dot(x, pattern, precision=F32_DOT, preferred_element_type=F32)

    twice = _pattern(n_state, 2 * n_state, lambda r, c: r == c % n_state)
    steps = lax.broadcasted_iota(jnp.int32, (cs + 1, 1), 0).astype(F32)
    mag = jnp.exp(steps * (lr_r * dt))
    ang = steps * (li_r * dt)
    pw_re, pw_im = mag * jnp.cos(ang), mag * jnp.sin(ang)
    num_re, num_im = pw_re[1:2] - 1.0, pw_im[1:2]
    den = lr_r * lr_r + li_r * li_r
    cf_re = (num_re * lr_r + num_im * li_r) / den
    cf_im = (num_im * lr_r - num_re * li_r) / den
    bb_re = spread(cf_re * bt_re - cf_im * bt_im, twice)
    bb_im = spread(cf_re * bt_im + cf_im * bt_re, twice)
    pw2_re, pw2_im = spread(pw_re, twice), spread(pw_im, twice)
    real_half = lax.broadcasted_iota(jnp.int32, (1, 2 * n_state), 1) < n_state
    blocks = []
    for s in range(cs):
        pr, pi = pw2_re[cs - 1 - s:cs - s], pw2_im[cs - 1 - s:cs - s]
        blocks.append(jnp.where(real_half, bb_re * pr - bb_im * pi, bb_re * pi + bb_im * pr))
    b_mat = jnp.concatenate(blocks, axis=0)
    la = pw2_re[cs:cs + 1]
    lb = jnp.where(real_half, -pw2_im[cs:cs + 1], pw2_im[cs:cs + 1])

    lane = lax.broadcasted_iota(jnp.int32, (1, width), 1)
    tile_out = _pattern(n_ch, width, lambda r, c: r == c % n_ch)
    c_re, c_im = spread(ct_re, tile_out), spread(ct_im, tile_out)

    k_row = lax.broadcasted_iota(jnp.int32, (1, cs), 1).astype(F32)
    m, a = jnp.exp(k_row * (lr_c * dt)), k_row * (li_c * dt)
    repeat = _pattern(cs, width, lambda r, c: r == c // n_ch)
    p_re, p_im = spread(m * jnp.cos(a), repeat), spread(m * jnp.sin(a), repeat)
    w_re, w_im = p_re * c_re - p_im * c_im, p_re * c_im + p_im * c_re
    skip = jnp.where((lane < n_ch) & (lane == lax.broadcasted_iota(jnp.int32, (n_ch, width), 0)),
                     spread(d_row, tile_out), 0.0)
    kt_row = (jnp.dot(bb_re[:, :n_state], w_re, precision=F32_DOT, preferred_element_type=F32)
              - jnp.dot(bb_im[:, :n_state], w_im, precision=F32_DOT, preferred_element_type=F32) + skip)
    bar_re, bar_im = jnp.exp(lr_c * dt) * jnp.cos(li_c * dt), jnp.exp(lr_c * dt) * jnp.sin(li_c * dt)
    w1_re, w1_im = w_re * bar_re - w_im * bar_im, w_re * bar_im + w_im * bar_re
    c_mat = jnp.concatenate([w1_re, -w1_im], axis=0)
    return kt_row, b_mat, c_mat, la, lb


def _s5_operator_inputs(lam_re, lam_im, log_dt, b_re, b_im, c_re, c_im, d_skip):
    return (lam_re[:, None, :], lam_im[:, None, :], lam_re[:, :, None], lam_im[:, :, None], log_dt[:, None, None],
            b_re.transpose(0, 2, 1), b_im.transpose(0, 2, 1), c_re.transpose(0, 2, 1), c_im.transpose(0, 2, 1),
            d_skip[:, None, :])


def _s5_operators_call(name, args, cotangents=None, gb=8):
    groups = args[0].shape[0]
    gb = _tile(groups, gb)
    n_in = len(args)

    def body(*refs):
        n_ct = 0 if cotangents is None else len(cotangents)
        ins, cts, outs = refs[:n_in], refs[n_in:n_in + n_ct], refs[n_in + n_ct:]
        for g in range(gb):
            vals = [r[g] for r in ins]
            if cotangents is None:
                res = _s5_group_operators(*vals)
            else:
                res = jax.vjp(_s5_group_operators, *vals)[1](tuple(c[g] for c in cts))
            for o, v in zip(outs, res):
                o[g] = v

    def spec(a):
        return pl.BlockSpec((gb, *a.shape[1:]), lambda i: (i, 0, 0))

    if cotangents is None:
        n_ch, n_state = args[5].shape[1:]
        width = SSM_CHUNK * n_ch
        out_shape = [jax.ShapeDtypeStruct((groups, *s), F32) for s in
                     ((n_ch, width), (width, 2 * n_state), (2 * n_state, width), (1, 2 * n_state), (1, 2 * n_state))]
    else:
        out_shape = [jax.ShapeDtypeStruct(a.shape, F32) for a in args]
    operands = [*args, *(cotangents or ())]
    return pl.pallas_call(
        body, name=name, grid=(groups // gb,), in_specs=[spec(a) for a in operands], out_specs=[spec(s) for s in out_shape],
        out_shape=out_shape, compiler_params=_params("parallel"),
    )(*operands)


GROUPS_PER_BLOCK = LANES // SSM_GROUP


def _tokens_to_groups(name, u, col_block, width):
    t = u.shape[0]
    n = t // SSM_CHUNK
    ch = SSM_CHUNK * SSM_GROUP
    blocks = width // LANES
    nb = GROUPS_PER_BLOCK

    def body(u_ref, o_ref):
        block = lax.broadcasted_iota(jnp.int32, (n, LANES), 1) // SSM_GROUP
        for half in range(SSM_CHUNK // nb):
            rows = [u_ref[pl.ds(half * nb + s, n, stride=SSM_CHUNK), :] for s in range(nb)]
            for shift in range(nb):
                merged = rows[shift]
                for b in range(1, nb):
                    merged = jnp.where(block == b, rows[(b + shift) % nb], merged)
                moved = pltpu.roll(merged, shift * SSM_GROUP, 1) if shift else merged
                for b in range(nb):
                    s = (b + shift) % nb
                    o_ref[b, :, half * LANES + s * SSM_GROUP:half * LANES + (s + 1) * SSM_GROUP] = (
                        moved[:, s * SSM_GROUP:(s + 1) * SSM_GROUP])

    return pl.pallas_call(
        body, name=name, grid=(blocks,),
        in_specs=[pl.BlockSpec((t, LANES), lambda j: (0, col_block * blocks + j))],
        out_specs=pl.BlockSpec((GROUPS_PER_BLOCK, n, ch), lambda j: (j, 0, 0)),
        out_shape=jax.ShapeDtypeStruct((width // SSM_GROUP, n, ch), F32), compiler_params=_params("parallel"),
    )(u)


def _groups_to_tokens(name, ug):
    groups, n, ch = ug.shape
    nb = GROUPS_PER_BLOCK

    def body(g_ref, o_ref, rows_ref):
        block = lax.broadcasted_iota(jnp.int32, (n, LANES), 1) // SSM_GROUP
        for half in range(SSM_CHUNK // nb):
            src = [g_ref[b, :, half * LANES:(half + 1) * LANES] for b in range(nb)]
            for shift in range(nb):
                merged = src[-shift % nb]
                for s in range(1, nb):
                    merged = jnp.where(block == s, src[(s - shift) % nb], merged)
                moved = pltpu.roll(merged, (nb - shift) * SSM_GROUP, 1) if shift else merged
                for b in range(nb):
                    rows_ref[(b + shift) % nb, :, b * SSM_GROUP:(b + 1) * SSM_GROUP] = moved[:, b * SSM_GROUP:(b + 1) * SSM_GROUP]
            for s in range(nb):
                o_ref[pl.ds(half * nb + s, n, stride=SSM_CHUNK), :] = rows_ref[s]

    return pl.pallas_call(
        body, name=name, grid=(groups // GROUPS_PER_BLOCK,),
        in_specs=[pl.BlockSpec((GROUPS_PER_BLOCK, n, ch), lambda j: (j, 0, 0))],
        out_specs=pl.BlockSpec((n * SSM_CHUNK, LANES), lambda j: (0, j)),
        out_shape=jax.ShapeDtypeStruct((n * SSM_CHUNK, groups * SSM_GROUP), F32),
        scratch_shapes=[pltpu.VMEM((nb, n, LANES), F32)], compiler_params=_params("parallel"),
    )(ug)


SCAN_ROWS = 8


def _toeplitz_to(tm_ref, g, kt_row):
    width = kt_row.shape[1]
    tm_ref[g] = jnp.zeros((width, width), F32)
    for s in range(SSM_CHUNK):
        tm_ref[g, s * SSM_GROUP:(s + 1) * SSM_GROUP, s * SSM_GROUP:] = kt_row[:, :width - s * SSM_GROUP]


def _lam_powers(la, lb, reverse):
    if reverse:
        lb = -lb

    def mul(p, q):
        return p[0] * q[0] - p[1] * q[1], p[0] * q[1] + p[1] * q[0]

    p1 = (la, lb)
    p2 = mul(p1, p1)
    p3 = mul(p2, p1)
    p4 = mul(p2, p2)
    rows = [p1, p2, p3, p4, mul(p4, p1), mul(p4, p2), mul(p4, p3), mul(p4, p4)]
    if reverse:
        rows = rows[::-1]
    idx = lax.broadcasted_iota(jnp.int32, (SCAN_ROWS, la.shape[1]), 0)
    tab_a = sum(jnp.where(idx == j, r[0], 0.0) for j, r in enumerate(rows))
    tab_b = sum(jnp.where(idx == j, r[1], 0.0) for j, r in enumerate(rows))
    return (p1, p2, p4), (tab_a, tab_b), idx


def _scan_block(e, carry, steps, table, idx, half, reverse):
    n = SCAN_ROWS
    for d, (pa, pb) in zip((1, 2, 4), steps):
        sh = pltpu.roll(e, n - d if reverse else d, 0)
        sh = jnp.where(idx < n - d if reverse else idx >= d, sh, 0.0)
        e = e + pa * sh + pb * pltpu.roll(sh, half, 1)
    tab_a, tab_b = table
    e = e + tab_a * carry + tab_b * pltpu.roll(carry, half, 1)
    shifted = jnp.where(idx == (n - 1 if reverse else 0), carry, pltpu.roll(e, n - 1 if reverse else 1, 0))
    edge = e[0:1] if reverse else e[n - 1:n]
    return shifted, jnp.broadcast_to(edge, e.shape)


def _s5_fwd(ug, kt_row, b_mat, c_mat, la, lb, *, batch, gb=8):
    groups, n, ch = ug.shape
    p2 = b_mat.shape[2]
    gb = _tile(groups, gb)
    nch = n // batch
    nblk = nch // SCAN_ROWS

    def body(u_ref, k_ref, b_ref, c_ref, la_ref, lb_ref, y_ref, x_ref, s_ref, tm_ref):
        for g in range(gb):
            _toeplitz_to(tm_ref, g, k_ref[g])
            s_ref[g] = jnp.dot(u_ref[g], b_ref[g], precision=F32_DOT, preferred_element_type=F32)
        powers = [_lam_powers(la_ref[g], lb_ref[g], False) for g in range(gb)]

        def step(blk, carries):
            new = []
            for g in range(gb):
                steps, table, idx = powers[g]
                for b in range(batch):
                    rows = pl.ds(pl.multiple_of(b * nch + blk * SCAN_ROWS, SCAN_ROWS), SCAN_ROWS)
                    x_in, carry = _scan_block(s_ref[g, rows, :], carries[g * batch + b], steps, table, idx, p2 // 2, False)
                    x_ref[g, rows, :] = x_in
                    new.append(carry)
            return tuple(new)

        lax.fori_loop(0, nblk, step, tuple(jnp.zeros((SCAN_ROWS, p2), F32) for _ in range(gb * batch)))
        for g in range(gb):
            y_ref[g] = (jnp.dot(u_ref[g], tm_ref[g], precision=F32_DOT, preferred_element_type=F32)
                        + jnp.dot(x_ref[g], c_ref[g], precision=F32_DOT, preferred_element_type=F32))

    def spec(a, b):
        return pl.BlockSpec((gb, a, b), lambda i: (i, 0, 0))

    return pl.pallas_call(
        body, name="s5_fwd", grid=(groups // gb,),
        in_specs=[spec(n, ch), spec(SSM_GROUP, ch), spec(ch, p2), spec(p2, ch), spec(1, p2), spec(1, p2)],
        out_specs=[spec(n, ch), spec(n, p2)],
        out_shape=[jax.ShapeDtypeStruct((groups, n, ch), F32), jax.ShapeDtypeStruct((groups, n, p2), F32)],
        scratch_shapes=[pltpu.VMEM((gb, n, p2), F32), pltpu.VMEM((gb, ch, ch), F32)],
        compiler_params=_params("parallel"),
    )(ug, kt_row, b_mat, c_mat, la, lb)


def _s5_bwd(ug, dyg, xin, kt_row, b_mat, c_mat, la, lb, *, batch, gb=8):
    groups, n, ch = ug.shape
    p2 = b_mat.shape[2]
    gb = _tile(groups, gb)
    nch = n // batch
    nblk = nch // SCAN_ROWS

    def body(u_ref, dy_ref, x_ref, k_ref, b_ref, c_ref, la_ref, lb_ref,
             du_ref, dk_ref, db_ref, dc_ref, dla_ref, dlb_ref, dx_ref, ds_ref, tm_ref):
        for g in range(gb):
            _toeplitz_to(tm_ref, g, k_ref[g])
            dx_ref[g] = _dot_nt(dy_ref[g], c_ref[g], precision=F32_DOT)
        powers = [_lam_powers(la_ref[g], lb_ref[g], True) for g in range(gb)]

        def step(it, carries):
            new = []
            for g in range(gb):
                steps, table, idx = powers[g]
                for b in range(batch):
                    rows = pl.ds(pl.multiple_of(b * nch + (nblk - 1 - it) * SCAN_ROWS, SCAN_ROWS), SCAN_ROWS)
                    d_s, carry = _scan_block(dx_ref[g, rows, :], carries[g * batch + b], steps, table, idx, p2 // 2, True)
                    ds_ref[g, rows, :] = d_s
                    new.append(carry)
            return tuple(new)

        lax.fori_loop(0, nblk, step, tuple(jnp.zeros((SCAN_ROWS, p2), F32) for _ in range(gb * batch)))
        for g in range(gb):
            u, dy, ds, x = u_ref[g], dy_ref[g], ds_ref[g], x_ref[g]
            du_ref[g] = _dot_nt(dy, tm_ref[g], precision=F32_DOT) + _dot_nt(ds, b_ref[g], precision=F32_DOT)
            tm_ref[g] = _dot_tn(u, dy, precision=F32_DOT)
            dk_ref[g] = tm_ref[g, 0:SSM_GROUP, :]
            for s in range(1, SSM_CHUNK):
                dk_ref[g, :, :ch - s * SSM_GROUP] += tm_ref[g, s * SSM_GROUP:(s + 1) * SSM_GROUP, s * SSM_GROUP:]
            db_ref[g] = _dot_tn(u, ds, precision=F32_DOT)
            dc_ref[g] = _dot_tn(x, dy, precision=F32_DOT)
            dla_ref[g] = jnp.sum(ds * x, axis=0, keepdims=True)
            dlb_ref[g] = jnp.sum(ds * pltpu.roll(x, p2 // 2, 1), axis=0, keepdims=True)

    def spec(a, b):
        return pl.BlockSpec((gb, a, b), lambda i: (i, 0, 0))

    def shape(a, b):
        return jax.ShapeDtypeStruct((groups, a, b), F32)

    return pl.pallas_call(
        body, name="s5_bwd", grid=(groups // gb,),
        in_specs=[spec(n, ch), spec(n, ch), spec(n, p2), spec(SSM_GROUP, ch), spec(ch, p2), spec(p2, ch), spec(1, p2),
                  spec(1, p2)],
        out_specs=[spec(n, ch), spec(SSM_GROUP, ch), spec(ch, p2), spec(p2, ch), spec(1, p2), spec(1, p2)],
        out_shape=[shape(n, ch), shape(SSM_GROUP, ch), shape(ch, p2), shape(p2, ch), shape(1, p2), shape(1, p2)],
        scratch_shapes=[pltpu.VMEM((gb, n, p2), F32), pltpu.VMEM((gb, n, p2), F32), pltpu.VMEM((gb, ch, ch), F32)],
        compiler_params=_params("parallel"),
    )(ug, dyg, xin, kt_row, b_mat, c_mat, la, lb)


def _block(ref, axis, j, size):
    start = j * size if isinstance(j, int) else pl.multiple_of(j * size, size)
    return ref.at[pl.ds(start, size), :] if axis == 0 else ref.at[:, pl.ds(start, size)]


def _chip_exchange_copies(mode, axes, srcs, lands, send_sems, recv_sems, local_sems):
    x, y, c = lax.axis_index("x"), lax.axis_index("y"), lax.axis_index("c")
    everyone = mode == "all"
    me = 4 * x + 2 * y + c if everyone else 2 * x + y
    n_peers = _exchange_peers(mode)
    local, sends, arrivals = [], [], []
    for w, axis in enumerate(axes):
        if mode == "gather":
            size = srcs[w].shape[axis]
            local.append(pltpu.make_async_copy(srcs[w], _block(lands[w], axis, me, size), local_sems.at[w]))
        elif mode == "scatter":
            size = srcs[w].shape[axis] // N_CHIPS
            local.append(pltpu.make_async_copy(_block(srcs[w], axis, me, size), lands[w].at[me], local_sems.at[w]))
        else:
            local.append(pltpu.make_async_copy(srcs[w], lands[w].at[me], local_sems.at[w]))
        for k in range(1, n_peers + 1):
            bits = k if everyone else 2 * k
            px = 1 - x if bits & 4 else x
            py = 1 - y if bits & 2 else y
            pc = 1 - c if bits & 1 else c
            peer = 4 * px + 2 * py + pc if everyone else 2 * px + py
            if mode == "gather":
                src, dst, arrive = srcs[w], _block(lands[w], axis, me, size), _block(lands[w], axis, peer, size)
            elif mode == "scatter":
                src, dst, arrive = _block(srcs[w], axis, peer, size), lands[w].at[me], lands[w].at[peer]
            else:
                src, dst, arrive = srcs[w], lands[w].at[me], lands[w].at[peer]
            sem = w * n_peers + k - 1
            for target, out in ((dst, sends), (arrive, arrivals)):
                out.append(pltpu.make_async_remote_copy(
                    src_ref=src, dst_ref=target, send_sem=send_sems.at[sem], recv_sem=recv_sems.at[sem],
                    device_id=(px, py, pc), device_id_type=MESH))
    return local, sends, arrivals


def _exchange_peers(mode):
    return N_DEV - 1 if mode == "all" else N_CHIPS - 1


def _chip_exchange_start(name, mode, items, after=None):
    n = len(items)
    n_after = 0 if after is None else 1
    axes = [axis for _, axis in items]
    hbm = pl.BlockSpec(memory_space=pltpu.HBM)
    sem = pl.BlockSpec(memory_space=pltpu.SEMAPHORE)
    lands = []
    for a, axis in items:
        shape = list(a.shape)
        if mode == "gather":
            shape[axis] *= N_CHIPS
        elif mode == "scatter":
            shape[axis] //= N_CHIPS
            shape = [N_CHIPS] + shape
        else:
            shape = [N_DEV] + shape
        lands.append(pltpu.with_memory_space_constraint(lax.empty(tuple(shape), a.dtype), pltpu.HBM))

    def body(*refs):
        srcs, land_refs = refs[:n], refs[n:2 * n]
        send_sems, recv_sems, local_sems = refs[2 * n + n_after:2 * n + n_after + 3]
        token = refs[-1]
        local, sends, _ = _chip_exchange_copies(mode, axes, srcs, land_refs, send_sems, recv_sems, local_sems)
        for cp in local + sends:
            cp.start()
        token[...] = jnp.zeros_like(token)

    n_sem = n * _exchange_peers(mode)
    outs = pl.pallas_call(
        body, name=name,
        out_shape=(pltpu.SemaphoreType.DMA((n_sem,)), pltpu.SemaphoreType.DMA((n_sem,)), pltpu.SemaphoreType.DMA((n,)),
                   *[pltpu.HBM(a.shape, a.dtype) for a, _ in items], *[pltpu.HBM(l.shape, l.dtype) for l in lands],
                   jax.ShapeDtypeStruct((8, LANES), F32)),
        in_specs=[hbm] * (2 * n) + [pl.BlockSpec(memory_space=pl.ANY)] * n_after,
        out_specs=(sem, sem, sem, *[hbm] * (2 * n), pl.BlockSpec(memory_space=pltpu.VMEM)),
        input_output_aliases={i: 3 + i for i in range(2 * n)},
        compiler_params=pltpu.CompilerParams(has_side_effects=pltpu.SideEffectType.DATAFLOW_SIDE_EFFECTING),
    )(*[pltpu.with_memory_space_constraint(a, pltpu.HBM) for a, _ in items], *lands, *([after] if n_after else []))
    return (mode, axes, outs[:3], outs[3:3 + n], outs[3 + n:3 + 2 * n]), outs[-1][0:1, 0:1]


def _chip_exchange_wait(name, handle, after):
    mode, axes, sems, srcs, lands = handle
    n = len(axes)
    after = list(after) if isinstance(after, (tuple, list)) else [after]
    hbm = pl.BlockSpec(memory_space=pltpu.HBM)
    sem = pl.BlockSpec(memory_space=pltpu.SEMAPHORE)

    def body(*refs):
        src_refs, land_refs = refs[:n], refs[n:2 * n]
        send_sems, recv_sems, local_sems = refs[2 * n:2 * n + 3]
        local, sends, arrivals = _chip_exchange_copies(mode, axes, src_refs, land_refs, send_sems, recv_sems, local_sems)
        for cp in sends:
            cp.wait_send()
        for cp in arrivals:
            cp.wait_recv()
        for cp in local:
            cp.wait()

    outs = pl.pallas_call(
        body, name=name,
        out_shape=(*[pltpu.HBM(a.shape, a.dtype) for a in srcs], *[pltpu.HBM(l.shape, l.dtype) for l in lands]),
        in_specs=[hbm] * (2 * n) + [sem] * 3 + [pl.BlockSpec(memory_space=pl.ANY)] * len(after), out_specs=[hbm] * (2 * n),
        input_output_aliases={i: i for i in range(2 * n)},
        compiler_params=pltpu.CompilerParams(has_side_effects=pltpu.SideEffectType.DATAFLOW_SIDE_EFFECTING),
    )(*srcs, *lands, *sems, *after)
    return outs[n:]


def _sum_slots(name, slots, tm=256):
    n_slots, r, c = slots.shape
    tm = _tile(r, tm)

    def body(*refs):
        acc = refs[0][...]
        for s_ref in refs[1:n_slots]:
            acc = acc + s_ref[...]
        refs[n_slots][...] = acc

    specs = [pl.BlockSpec((None, tm, c), functools.partial(lambda i, s: (s, i, 0), s=s)) for s in range(n_slots)]
    return pl.pallas_call(
        body, name=name, grid=(r // tm,), in_specs=specs, out_specs=pl.BlockSpec((tm, c), lambda i: (i, 0)),
        out_shape=jax.ShapeDtypeStruct((r, c), F32), compiler_params=_params("parallel"),
    )(*[slots] * n_slots)


def _swap_with_sibling(name, arrays):
    n = len(arrays)
    hbm = pl.BlockSpec(memory_space=pl.ANY)

    def body(*refs):
        ins, outs = refs[:n], refs[n:2 * n]
        send_sems, recv_sems = refs[2 * n:]
        sibling = (lax.axis_index("x"), lax.axis_index("y"), 1 - lax.axis_index("c"))
        copies = [pltpu.make_async_remote_copy(src_ref=ins[w], dst_ref=outs[w], send_sem=send_sems.at[w],
                                               recv_sem=recv_sems.at[w], device_id=sibling, device_id_type=MESH)
                  for w in range(n)]
        for cp in copies:
            cp.start()
        for cp in copies:
            cp.wait()

    return pl.pallas_call(
        body, name=name, in_specs=[hbm] * n, out_specs=[hbm] * n,
        out_shape=[jax.ShapeDtypeStruct(a.shape, a.dtype) for a in arrays],
        scratch_shapes=[pltpu.SemaphoreType.DMA((n,)), pltpu.SemaphoreType.DMA((n,))],
    )(*arrays)


def _adamw(g, w, m, v):
    m = ADAM_B1 * m + (1.0 - ADAM_B1) * g
    v = ADAM_B2 * v + (1.0 - ADAM_B2) * jnp.square(g)
    m_hat = m / (1.0 - ADAM_B1 ** ADAM_STEP)
    v_hat = v / (1.0 - ADAM_B2 ** ADAM_STEP)
    delta = -ADAM_LR * (m_hat / (jnp.sqrt(v_hat) + ADAM_EPS) + ADAM_WD * w)
    return delta, m, v


def _adamw_small(grads, ws, ms, vs):
    n = len(ws)

    def whole(a):
        return pl.BlockSpec(a.shape, functools.partial(lambda i, nd: (0,) * nd, nd=a.ndim))

    def body(*refs):
        for i in range(n):
            g, w, m, v = (refs[k * n + i][...] for k in range(4))
            for k, val in enumerate(_adamw(g, w, m, v)):
                refs[(4 + k) * n + i][...] = val

    outs = pl.pallas_call(
        body, name="adamw_small", grid=(1,), in_specs=[whole(a) for a in (*grads, *ws, *ms, *vs)],
        out_specs=[whole(w) for _ in range(3) for w in ws],
        out_shape=[jax.ShapeDtypeStruct(w.shape, F32) for _ in range(3) for w in ws],
        compiler_params=pltpu.CompilerParams(vmem_limit_bytes=VMEM_LIMIT),
    )(*grads, *ws, *ms, *vs)
    return outs[:n], outs[n:2 * n], outs[2 * n:]


def kernel(x, norm1_g, w_in, q_norm_g, k_norm_g, ssm_lambda_re, ssm_lambda_im, ssm_log_dt, ssm_b_re, ssm_b_im, ssm_c_re, ssm_c_im, ssm_d, w_glu, b_glu, attn_out_g, ssm_out_g, w_out, norm2_g, w_mlp_in, w_mlp_out, loss_target, m_norm1_g, m_w_in, m_q_norm_g, m_k_norm_g, m_ssm_lambda_re, m_ssm_lambda_im, m_ssm_log_dt, m_ssm_b_re, m_ssm_b_im, m_ssm_c_re, m_ssm_c_im, m_ssm_d, m_w_glu, m_b_glu, m_attn_out_g, m_ssm_out_g, m_w_out, m_norm2_g, m_w_mlp_in, m_w_mlp_out, v_norm1_g, v_w_in, v_q_norm_g, v_k_norm_g, v_ssm_lambda_re, v_ssm_lambda_im, v_ssm_log_dt, v_ssm_b_re, v_ssm_b_im, v_ssm_c_re, v_ssm_c_im, v_ssm_d, v_w_glu, v_b_glu, v_attn_out_g, v_ssm_out_g, v_w_out, v_norm2_g, v_w_mlp_in, v_w_mlp_out):
    batch, seq, d_model = x.shape
    tokens = batch * seq
    sb_width = w_in.shape[1]
    n_features = d_model

    big = [("w_in", w_in, m_w_in, v_w_in, 1), ("w_glu", w_glu, m_w_glu, v_w_glu, 0),
           ("w_out", w_out, m_w_out, v_w_out, 0), ("w_mlp_in", w_mlp_in, m_w_mlp_in, v_w_mlp_in, 1),
           ("w_mlp_out", w_mlp_out, m_w_mlp_out, v_w_mlp_out, 0)]
    small = [("norm1_g", norm1_g, m_norm1_g, v_norm1_g), ("q_norm_g", q_norm_g, m_q_norm_g, v_q_norm_g),
             ("k_norm_g", k_norm_g, m_k_norm_g, v_k_norm_g),
             ("ssm_lambda_re", ssm_lambda_re, m_ssm_lambda_re, v_ssm_lambda_re),
             ("ssm_lambda_im", ssm_lambda_im, m_ssm_lambda_im, v_ssm_lambda_im),
             ("ssm_log_dt", ssm_log_dt, m_ssm_log_dt, v_ssm_log_dt),
             ("ssm_b_re", ssm_b_re, m_ssm_b_re, v_ssm_b_re), ("ssm_b_im", ssm_b_im, m_ssm_b_im, v_ssm_b_im),
             ("ssm_c_re", ssm_c_re, m_ssm_c_re, v_ssm_c_re), ("ssm_c_im", ssm_c_im, m_ssm_c_im, v_ssm_c_im),
             ("ssm_d", ssm_d, m_ssm_d, v_ssm_d), ("b_glu", b_glu, m_b_glu, v_b_glu),
             ("attn_out_g", attn_out_g, m_attn_out_g, v_attn_out_g), ("ssm_out_g", ssm_out_g, m_ssm_out_g, v_ssm_out_g),
             ("norm2_g", norm2_g, m_norm2_g, v_norm2_g)]

    gather_in, tok_in = _chip_exchange_start("gather_w_in_start", "gather", [(w_in.astype(BF16), 1)])
    gather_mix, tok_mix_w = _chip_exchange_start(
        "gather_mix_start", "gather", [(w.astype(BF16), axis) for _, w, _, _, axis in big[1:3]], after=tok_in)
    gather_mlp, tok_rest = _chip_exchange_start(
        "gather_mlp_start", "gather", [(w.astype(BF16), axis) for _, w, _, _, axis in big[3:]], after=tok_mix_w)

    x2 = x.reshape(tokens, d_model)
    tgt2 = loss_target.reshape(tokens, d_model)
    g1, g2 = norm1_g[None, :], norm2_g[None, :]
    g_attn, g_ssm, bias_glu = attn_out_g[None, :], ssm_out_g[None, :], b_glu[None, :]
    heads = sb_width // HEAD_DIM
    qk_scale = 1.0 / math.sqrt(HEAD_DIM)
    gq, gk = (jnp.tile(q_norm_g, heads) * qk_scale)[None, :], jnp.tile(k_norm_g, heads)[None, :]
    lane_head = jnp.arange(LANES) // HEAD_DIM
    ones_blocks = (lane_head[:, None] == lane_head[None, :]).astype(F32)

    (xn,) = _rowwise("norm1", _rms, [x2], [g1 + tok_rest], [(d_model, BF16)], tm=512)
    s5_in = _s5_operator_inputs(ssm_lambda_re, ssm_lambda_im, ssm_log_dt, ssm_b_re, ssm_b_im, ssm_c_re, ssm_c_im, ssm_d)
    kt_row, b_mat, c_mat, la, lb = _s5_operators_call("s5_operators", s5_in)
    (wf_in,) = _chip_exchange_wait("gather_w_in_wait", gather_in, [xn, b_mat, c_mat])
    def proj_head(acc, gq_, gk_, ones):
        q, k, v = (acc[:, i * sb_width:(i + 1) * sb_width] for i in range(3))
        return acc, _head_rms(q, gq_, ones), _head_rms(k, gk_, ones), v

    proj, qn, kn, vb = _mm("proj_in", xn, wf_in, "nn", epilogue=proj_head, full_rows=True, max_tm=512,
                           extras=[(gq, "row"), (gk, "row"), (ones_blocks, "whole")],
                           out_dtypes=(F32, (BF16, sb_width), (BF16, sb_width), (BF16, sb_width)))
    sb, c_tot = _attn_fwd(qn, kn, vb, batch=batch, seq=seq, bq=ATTN_BQ, bk=ATTN_BK)
    ug = _tokens_to_groups("u_to_groups", proj, 3, sb_width)
    yg, xin = _s5_fwd(ug, kt_row, b_mat, c_mat, la, lb, batch=batch)
    y_ssm = _groups_to_tokens("y_to_tokens", yg)

    wf_glu, wf_out = _chip_exchange_wait("gather_mix_wait", gather_mix, [y_ssm, sb])
    (gate_pre,) = _mm("glu_gate", y_ssm, wf_glu, "nn", a_fn=_gelu, extras=[(bias_glu, "row")],
                      epilogue=lambda acc, b: acc + b)
    (mixed,) = _rowwise("mix_norm", _mixed, [sb, y_ssm, gate_pre], [g_attn, g_ssm], [(2 * sb_width, BF16)], tm=512)
    def out_head(acc, r, g):
        h = acc + r
        return h, _rms(h, g)

    h1, hn = _mm("proj_out", mixed, wf_out, "nn", extras=[(x2, "tile"), (g2, "row")], epilogue=out_head,
                 out_dtypes=(F32, BF16), full_rows=True)
    def square(r):
        r = r.astype(F32)
        return r * r

    wf_mlp_in, wf_mlp_out = _chip_exchange_wait("gather_mlp_wait", gather_mlp, [h1, hn])
    (act_root,) = _mm("mlp_in", hn, wf_mlp_in, "nn", epilogue=lambda acc: jnp.maximum(acc, 0.0), out_dtypes=(BF16,))
    inv_n = 1.0 / n_features

    def loss_head(acc, r, t):
        d = ((acc + r) - t) * inv_n
        return d, d, jnp.sum(d * d, keepdims=True) * (0.5 * n_features)

    dy, dy_b, loss_tiles = _mm("mlp_out_loss", act_root, wf_mlp_out, "nn", a_fn=square, extras=[(h1, "tile"), (tgt2, "tile")],
                               epilogue=loss_head, out_dtypes=(F32, BF16), tile_sums=("scalar",))
    loss_part = jnp.sum(loss_tiles)

    (dw_mlp_out,) = _mm("dw_mlp_out", act_root, dy_b, "tn", a_fn=square)
    (dpre,) = _mm("d_mlp_act", dy_b, wf_mlp_out, "nt", extras=[(act_root, "tile")],
                  epilogue=lambda acc, r: acc * (2.0 * r.astype(F32)), out_dtypes=(BF16,))
    (dw_mlp_in,) = _mm("dw_mlp_in", hn, dpre, "tn")
    scatter_mlp, tok_mlp = _chip_exchange_start("scatter_mlp_start", "scatter", [(dw_mlp_in, 1), (dw_mlp_out, 0)])
    def norm_bwd(dn, res, hx, g):
        _, vjp = jax.vjp(_rms, hx, g)
        dh, dg = vjp(dn)
        return res + dh, dg

    dh1, dg_tiles = _mm("d_norm2_in", dpre, wf_mlp_in, "nt", extras=[(dy, "tile"), (h1, "tile"), (g2 + tok_mlp, "row")],
                        epilogue=norm_bwd, tile_sums=("row",), full_rows=True)
    dg_norm2 = jnp.sum(dg_tiles, axis=0, keepdims=True)
    (dw_out,) = _mm("dw_out", mixed, dh1, "tn")

    def mixed_bwd(dm, sb_, ys, gp, ga, gs):
        _, vjp = jax.vjp(lambda a, act, b, c, d: jnp.concatenate(
            [_rms(a, c), _rms(act * jax.nn.sigmoid(b), d)], axis=-1), sb_, _gelu(ys), gp, ga, gs)
        dsb_, dact, dgp_, dga, dgs = vjp(dm)
        return dsb_, dgp_, dact, dga, dgs, jnp.sum(dgp_, axis=0, keepdims=True)

    dsb, dgate_pre, dact_part, *gain_tiles = _mm(
        "d_mixed", dh1, wf_out, "nt", epilogue=mixed_bwd, full_rows=True, max_tm=256,
        extras=[(sb, "tile"), (y_ssm, "tile"), (gate_pre, "tile"), (g_attn, "row"), (g_ssm, "row")],
        out_dtypes=((F32, sb_width), (BF16, sb_width), (F32, sb_width)), tile_sums=(("row", sb_width),) * 3)
    dg_attn, dg_ssm, db_glu = (jnp.sum(t, axis=0, keepdims=True) for t in gain_tiles)

    def gelu_bwd(acc, part, ys):
        _, vjp = jax.vjp(_gelu, ys)
        return vjp(acc + part)[0]

    (dy_ssm,) = _mm("d_glu_in", dgate_pre, wf_glu, "nt", extras=[(dact_part, "tile"), (y_ssm, "tile")], epilogue=gelu_bwd)
    (dw_glu,) = _mm("dw_glu", y_ssm, dgate_pre, "tn", a_fn=_gelu)
    scatter_mix, tok_mix = _chip_exchange_start("scatter_mix_start", "scatter", [(dw_glu, 0), (dw_out, 0)])

    dug, dkt_row, db_mat, dc_mat, dla, dlb = _s5_bwd(ug, _tokens_to_groups("dy_to_groups", dy_ssm, 0, sb_width), xin,
                                                     kt_row, b_mat, c_mat, la, lb + tok_mix, batch=batch)
    du = _groups_to_tokens("du_to_tokens", dug)
    d_in = _s5_operators_call("s5_operators_bwd", s5_in, (dkt_row, db_mat, dc_mat, dla, dlb))
    ds5 = [d_in[0][:, 0, :] + d_in[2][:, :, 0], d_in[1][:, 0, :] + d_in[3][:, :, 0], d_in[4][:, 0, 0],
           d_in[5].transpose(0, 2, 1), d_in[6].transpose(0, 2, 1), d_in[7].transpose(0, 2, 1), d_in[8].transpose(0, 2, 1),
           d_in[9][:, 0, :]]

    def pack(parts):
        flat = jnp.concatenate([p.reshape(-1) for p in parts])
        rows = -(-flat.shape[0] // (8 * LANES)) * 8
        return jnp.pad(flat, (0, rows * LANES - flat.shape[0])).reshape(rows, LANES)

    def unpack(packed, names):
        flat, out, off = packed.reshape(-1), {}, 0
        for name in names:
            shape = small_shapes[name]
            size = math.prod(shape)
            out[name] = flat[off:off + size].reshape(shape)
            off += size
        return out, flat[off]

    small_shapes = {name: w.shape for name, w, _, _ in small}
    early_names = ["ssm_lambda_re", "ssm_lambda_im", "ssm_log_dt", "ssm_b_re", "ssm_b_im", "ssm_c_re", "ssm_c_im", "ssm_d",
                   "b_glu", "attn_out_g", "ssm_out_g", "norm2_g"]
    late_names = ["norm1_g", "q_norm_g", "k_norm_g"]
    early = pack([*ds5, db_glu[0], dg_attn[0], dg_ssm[0], dg_norm2[0], loss_part])
    early_exchange, _ = _chip_exchange_start("small_early_start", "all", [(early, 0)])

    dqn, dkn, dv = _attn_bwd(qn, kn, vb, c_tot, dsb, batch=batch, seq=seq, bq=ATTN_BQ, bk=ATTN_BK,
                             after=early_exchange[3][0])
    (early_slots,) = _chip_exchange_wait("small_early_wait", early_exchange, dqn)
    small_g, loss = unpack(_sum_slots("sum_small_early", early_slots), early_names)

    def qk_bwd(q, k, dq_, dk_, dv_, du_, gq_, gk_, ones):
        _, vjp_q = jax.vjp(lambda a, g: _head_rms(a, g, ones), q, gq_)
        _, vjp_k = jax.vjp(lambda a, g: _head_rms(a, g, ones), k, gk_)
        dq, dgq = vjp_q(dq_)
        dk, dgk = vjp_k(dk_)
        return jnp.concatenate([dq, dk, dv_, du_], axis=1), dgq, dgk

    dproj, dgq, dgk = _rowwise("qk_norm_bwd", qk_bwd, [(proj, sb_width, 0), (proj, sb_width, 1), dqn, dkn, dv, du],
                               [gq, gk, ones_blocks], [(4 * sb_width, BF16)], [(1, sb_width)] * 2, tm=512)
    (dw_in,) = _mm("dw_in", xn, dproj, "tn")
    scatter_in, tok_w_in = _chip_exchange_start("scatter_in_start", "scatter", [(dw_in, 1)])
    dx, dg_tiles = _mm("d_norm1_in", dproj, wf_in, "nt", extras=[(dh1, "tile"), (x2, "tile"), (g1 + tok_w_in, "row")],
                       epilogue=norm_bwd, tile_sums=("row",), full_rows=True)
    dg_norm1 = jnp.sum(dg_tiles, axis=0, keepdims=True)

    late = pack([dg_norm1[0], dgq.reshape(heads, HEAD_DIM).sum(0) * qk_scale, dgk.reshape(heads, HEAD_DIM).sum(0),
                 jnp.zeros((1,), F32)])
    late_exchange, _ = _chip_exchange_start("small_late_start", "all", [(late, 0)])

    def adam_big(sa, sb_, w, m, v):
        g = sa + sb_
        delta, m, v = _adamw(g, w, m, v)
        return g, delta, m, v

    def reduce_and_update(tag, params, slots):
        mine = [_sum_slots("sum_" + name, s) for s, (name, *_rest) in zip(slots, params)]
        theirs = _swap_with_sibling("swap_" + tag, mine)
        return {name: _rowwise("adamw_" + name, adam_big, [sa, sb_, w, m, v], [], [(w.shape[1], F32)] * 4)
                for (name, w, m, v, _), sa, sb_ in zip(params, mine, theirs)}

    started = late_exchange[3][0]
    slots_mlp_in, slots_mlp_out = _chip_exchange_wait("scatter_mlp_wait", scatter_mlp, started)
    slots_glu, slots_out = _chip_exchange_wait("scatter_mix_wait", scatter_mix, started)
    big_out = reduce_and_update("rest", big[1:], [slots_glu, slots_out, slots_mlp_in, slots_mlp_out])

    (late_slots,) = _chip_exchange_wait("small_late_wait", late_exchange, big_out["w_mlp_out"][3])
    reduced = _sum_slots("sum_small_late", late_slots)
    small_g.update(unpack(reduced, late_names)[0])
    narrow = {name for name, w, _, _ in small if w.ndim == 3 and w.shape[2] < w.shape[1]}

    def flip(a, name):
        return jnp.swapaxes(a, 1, 2) if name in narrow else a

    small_upd = _adamw_small([flip(small_g[name], name) for name, *_ in small], [flip(w, name) for name, w, _, _ in small],
                             [flip(m, name) for name, _, m, _ in small], [flip(v, name) for name, _, _, v in small])
    small_out = [small_g] + [{name: flip(small_upd[kind][i], name) for i, (name, *_) in enumerate(small)}
                             for kind in range(3)]

    (slots_in,) = _chip_exchange_wait("scatter_in_wait", scatter_in, reduced)
    big_out.update(reduce_and_update("w_in", big[:1], [slots_in]))
    names = ["norm1_g", "w_in", "q_norm_g", "k_norm_g", "ssm_lambda_re", "ssm_lambda_im", "ssm_log_dt", "ssm_b_re",
             "ssm_b_im", "ssm_c_re", "ssm_c_im", "ssm_d", "w_glu", "b_glu", "attn_out_g", "ssm_out_g", "w_out",
             "norm2_g", "w_mlp_in", "w_mlp_out"]
    outs = [loss, dx.reshape(batch, seq, d_model)]
    for kind in range(4):
        for name in names:
            outs.append(big_out[name][kind] if name in big_out else small_out[kind][name])
    return tuple(outs)
```

```python
import functools
import math

import jax
import jax.numpy as jnp
from jax import lax
from jax.experimental import pallas as pl
from jax.experimental.pallas import tpu as pltpu

F32 = jnp.float32
BF16 = jnp.bfloat16
F32_DOT = lax.Precision.HIGH
MESH = pl.DeviceIdType.MESH

RMS_EPS = 1e-6
HEAD_DIM = 64
SSM_GROUP = 16
SSM_CHUNK = 16
LANES = 128
N_CHIPS = 4
N_DEV = 8
VMEM_LIMIT = 48 * 1024 * 1024

ADAM_LR = 0.001
ADAM_B1 = 0.9
ADAM_B2 = 0.999
ADAM_EPS = 1e-08
ADAM_WD = 0.01
ADAM_STEP = 10


def _tile(n, pref):
    t = min(n, pref)
    while n % t:
        t //= 2
    return t


def _params(*sem):
    return pltpu.CompilerParams(dimension_semantics=sem, vmem_limit_bytes=VMEM_LIMIT)


_DIMS = {"nn": (((1,), (0,)), ((), ())), "nt": (((1,), (1,)), ((), ())), "tn": (((0,), (0,)), ((), ()))}


MM_VMEM_BUDGET = 40 * 1024 * 1024


def _mm_tiles(m, n, k, a_bytes, b_bytes, tile_bytes, full_rows=False, max_tm=1024):
    best = None
    for tk in [t for t in (k, k // 2, k // 4, k // 8) if t >= 256 or t == k]:
        for tm in [t for t in (1024, 512, 256, 128) if t <= min(m, max_tm) and m % t == 0]:
            for tn in [n] if full_rows else [t for t in (1024, 512, 256, 128) if t <= n and n % t == 0]:
                need = 2 * (tm * tk * a_bytes + tk * tn * b_bytes) + 2 * tm * tn * tile_bytes + (tm * tn * 4 if tk < k else 0)
                if need > MM_VMEM_BUDGET:
                    continue
                traffic = m * k * a_bytes * (1 if tk == k else n // tn) + k * n * b_bytes * (1 if n == tn and tk == k else m // tm)
                key = (tk < k, traffic, -tm * tn)
                if best is None or key < best[0]:
                    best = (key, (tm, tn, tk))
    return best[1]


def _mm(name, a, b, mode, *, a_fn=None, extras=(), epilogue=None, out_dtypes=(F32,), tile_sums=(), full_rows=False,
        max_tm=1024):
    if mode == "nn":
        (m, k), n = a.shape, b.shape[1]
    elif mode == "nt":
        (m, k), n = a.shape, b.shape[0]
    else:
        (k, m), n = a.shape, b.shape[1]
    outs_spec = [(d, n) if not isinstance(d, tuple) else d for d in out_dtypes]
    sums_spec = [(s, n) if not isinstance(s, tuple) else s for s in tile_sums]
    assert full_rows or all(w == n for _, w in outs_spec + sums_spec) and all(e.shape[1] == n for e, _ in extras)
    tile_bytes = (sum(e.dtype.itemsize * e.shape[1] for e, kind in extras if kind == "tile")
                  + sum(jnp.dtype(d).itemsize * w for d, w in outs_spec)) // n + 1
    tm, tn, tk = _mm_tiles(m, n, k, a.dtype.itemsize, b.dtype.itemsize, tile_bytes, full_rows, max_tm)
    nk = k // tk
    ne, nout = len(extras), len(out_dtypes)
    dims = _DIMS[mode]

    def width_spec(rows, w):
        if w == n:
            return pl.BlockSpec((rows, tn), (lambda i, j, kk: (i, j)) if rows != 1 else (lambda i, j, kk: (0, j)))
        return pl.BlockSpec((rows, w), (lambda i, j, kk: (i, 0)) if rows != 1 else (lambda i, j, kk: (0, 0)))

    def body(a_ref, b_ref, *rest):
        ex, outs, sums = rest[:ne], rest[ne:ne + nout], rest[ne + nout:ne + nout + len(tile_sums)]
        at = a_ref[...]
        if a_fn is not None:
            at = a_fn(at)
        p = lax.dot_general(at.astype(BF16), b_ref[...].astype(BF16), dims, preferred_element_type=F32)

        def finish(r):
            if epilogue is not None:
                r = epilogue(r, *[e[...] for e in ex])
            if not isinstance(r, (tuple, list)):
                r = (r,)
            for o, v in zip(outs, r[:nout]):
                o[...] = v.astype(o.dtype)
            for o, v, (kind, _) in zip(sums, r[nout:], sums_spec):
                first = lax.broadcasted_iota(jnp.int32, o.shape, 0) == 0
                if kind == "scalar":
                    first &= lax.broadcasted_iota(jnp.int32, o.shape, 1) == 0
                o[...] = jnp.where(first, v, 0.0)

        if nk == 1:
            finish(p)
        else:
            acc = rest[ne + nout + len(tile_sums)]
            kk = pl.program_id(2)

            @pl.when(kk == 0)
            def _():
                acc[...] = p

            @pl.when(kk > 0)
            def _():
                acc[...] += p

            @pl.when(kk == nk - 1)
            def _():
                finish(acc[...])

    if mode == "tn":
        a_spec = pl.BlockSpec((tk, tm), lambda i, j, kk: (kk, i))
    else:
        a_spec = pl.BlockSpec((tm, tk), lambda i, j, kk: (i, kk))
    if mode == "nt":
        b_spec = pl.BlockSpec((tn, tk), lambda i, j, kk: (j, kk))
    else:
        b_spec = pl.BlockSpec((tk, tn), lambda i, j, kk: (kk, j))
    ex_specs = [pl.BlockSpec(e.shape, lambda i, j, kk: (0, 0)) if kind == "whole"
                else width_spec(tm if kind == "tile" else 1, e.shape[1]) for e, kind in extras]
    return pl.pallas_call(
        body, name=name, grid=(m // tm, n // tn, nk),
        in_specs=[a_spec, b_spec] + ex_specs,
        out_specs=([width_spec(tm, w) for _, w in outs_spec]
                   + [pl.BlockSpec((8, LANES), lambda i, j, kk: (i, j)) if kind == "scalar" else width_spec(8, w)
                      for kind, w in sums_spec]),
        out_shape=([jax.ShapeDtypeStruct((m, w), dt) for dt, w in outs_spec]
                   + [jax.ShapeDtypeStruct((m // tm * 8, n // tn * LANES if kind == "scalar" else w), F32)
                      for kind, w in sums_spec]),
        scratch_shapes=[pltpu.VMEM((tm, tn), F32)] if nk > 1 else [],
        compiler_params=_params("parallel", "parallel", "arbitrary"),
    )(a, b, *[e for e, _ in extras])


def _rowwise(name, fn, rows, consts, row_outs, acc_outs=(), tm=256):
    norm = [r if isinstance(r, tuple) else (r, r.shape[1], 0) for r in rows]
    t = norm[0][0].shape[0]
    tm = _tile(t, tm)
    nr, nc, no = len(norm), len(consts), len(row_outs)

    def body(*refs):
        outs = fn(*[r[...] for r in refs[:nr + nc]])
        if not isinstance(outs, (tuple, list)):
            outs = (outs,)
        o_refs, a_refs = refs[nr + nc:nr + nc + no], refs[nr + nc + no:]
        for r, v in zip(o_refs, outs[:no]):
            r[...] = v.astype(r.dtype)
        if a_refs:
            i = pl.program_id(0)

            @pl.when(i == 0)
            def _():
                for r, v in zip(a_refs, outs[no:]):
                    r[...] = v

            @pl.when(i > 0)
            def _():
                for r, v in zip(a_refs, outs[no:]):
                    r[...] += v

    in_specs = [pl.BlockSpec((tm, w), functools.partial(lambda i, cb: (i, cb), cb=cb)) for _, w, cb in norm]
    in_specs += [pl.BlockSpec(c.shape, functools.partial(lambda i, nd: (0,) * nd, nd=c.ndim)) for c in consts]
    out_specs = [pl.BlockSpec((tm, w), lambda i: (i, 0)) for w, _ in row_outs]
    out_specs += [pl.BlockSpec(s, functools.partial(lambda i, nd: (0,) * nd, nd=len(s))) for s in acc_outs]
    out_shape = [jax.ShapeDtypeStruct((t, w), dt) for w, dt in row_outs]
    out_shape += [jax.ShapeDtypeStruct(s, F32) for s in acc_outs]
    return pl.pallas_call(
        body, name=name, grid=(t // tm,), in_specs=in_specs, out_specs=out_specs, out_shape=out_shape,
        compiler_params=_params("arbitrary"),
    )(*[r[0] for r in norm], *consts)


def _rms(x, g):
    return x * lax.rsqrt(jnp.mean(x * x, axis=-1, keepdims=True) + RMS_EPS) * g


@jax.custom_vjp
def _head_sums(x, ones_blocks):
    parts = [jnp.dot(x[:, j:j + LANES], ones_blocks, precision=F32_DOT, preferred_element_type=F32)
             for j in range(0, x.shape[1], LANES)]
    return jnp.concatenate(parts, axis=1)


_head_sums.defvjp(lambda x, ones_blocks: (_head_sums(x, ones_blocks), ones_blocks),
                  lambda ones_blocks, ct: (_head_sums(ct, ones_blocks), None))


def _head_rms(x, g, ones_blocks):
    return x * lax.rsqrt(_head_sums(x * x, ones_blocks) * (1.0 / HEAD_DIM) + RMS_EPS) * g


def _gelu(x):
    return x * (0.5 * (1.0 + jnp.tanh(math.sqrt(2.0 / math.pi) * (x + 0.044715 * (x * x * x)))))


def _mixed(sb, y_ssm, gate_pre, g_attn, g_ssm):
    ssm = _gelu(y_ssm) * jax.nn.sigmoid(gate_pre)
    return jnp.concatenate([_rms(sb, g_attn), _rms(ssm, g_ssm)], axis=-1)


def _softplus(z):
    return jnp.maximum(z, 0.0) + jnp.log(1.0 + jnp.exp(-jnp.abs(z)))


def _running_sums(x, tri):
    return jnp.dot(x.astype(BF16), tri, preferred_element_type=F32)


def _dot_nt(a, b, **kw):
    return lax.dot_general(a, b, _DIMS["nt"], preferred_element_type=F32, **kw)


def _dot_tn(a, b, **kw):
    return lax.dot_general(a, b, _DIMS["tn"], preferred_element_type=F32, **kw)


ATTN_BQ, ATTN_BK = 2048, 256
HEAD_LANES = tuple(slice(h * HEAD_DIM, (h + 1) * HEAD_DIM) for h in range(LANES // HEAD_DIM))


def _attn_fwd(qs, kn, v, *, batch, seq, bq, bk):
    width = qs.shape[1]
    bq = _tile(seq, bq)
    bk = _tile(bq, bk)
    nq, kpq = seq // bq, bq // bk

    def body(q_ref, k_ref, v_ref, o_ref, c_ref):
        row = lax.broadcasted_iota(jnp.int32, (bq, bk), 0)
        col = lax.broadcasted_iota(jnp.int32, (bq, bk), 1)
        tri = (lax.broadcasted_iota(jnp.int32, (bk, bk), 0) >= lax.broadcasted_iota(jnp.int32, (bk, bk), 1)).astype(BF16)

        def q_block(qi, carry):
            r0 = pl.multiple_of(qi * bq, bq)
            qh = [q_ref[pl.ds(r0, bq), ln] for ln in HEAD_LANES]

            def tile(k0, state, top=0):
                diag = top is not None
                top = top or 0
                msk = (col < row)[:bq - top] if diag else None
                new = []
                for h, ln in enumerate(HEAD_LANES):
                    o, c = state[2 * h], state[2 * h + 1]
                    z = _dot_nt(qh[h][top:], k_ref[pl.ds(k0, bk), ln])
                    sp = _softplus(z)
                    if diag:
                        sp = jnp.where(msk, sp, 0.0)
                    r = _running_sums(sp, tri)
                    a = jnp.exp(z - r - c[top:])
                    if diag:
                        a = jnp.where(msk, a, 0.0)
                    o_new = o[top:] + jnp.dot(a.astype(BF16), v_ref[pl.ds(k0, bk), ln], preferred_element_type=F32)
                    c_new = c[top:] + r[:, 0:1]
                    if top:
                        o_new, c_new = jnp.concatenate([o[:top], o_new]), jnp.concatenate([c[:top], c_new])
                    new += [o_new, c_new]
                return tuple(new)

            state = (jnp.zeros((bq, HEAD_DIM), F32), jnp.zeros((bq, 1), F32)) * len(HEAD_LANES)
            for d in reversed(range(kpq)):
                state = tile(pl.multiple_of(r0 + d * bk, bk), state, top=d * bk)
            state = lax.fori_loop(0, qi * kpq, lambda it, st: tile(pl.multiple_of(r0 - (it + 1) * bk, bk), st, None),
                                  state)
            for h, ln in enumerate(HEAD_LANES):
                o_ref[pl.ds(r0, bq), ln] = state[2 * h]
                c_ref[pl.ds(r0, bq), ln] = jnp.broadcast_to(state[2 * h + 1], (bq, HEAD_DIM))
            return carry

        lax.fori_loop(0, nq, q_block, 0)

    spec = pl.BlockSpec((seq, LANES), lambda b, h: (b, h))
    shape = jax.ShapeDtypeStruct((batch * seq, width), F32)
    return pl.pallas_call(
        body, name="attn_fwd", grid=(batch, width // LANES), in_specs=[spec, spec, spec], out_specs=[spec, spec],
        out_shape=[shape, shape], compiler_params=_params("parallel", "parallel"),
    )(qs, kn, v)


def _attn_bwd(qs, kn, v, c_tot, do, *, batch, seq, bq, bk, after):
    width = qs.shape[1]
    bq = _tile(seq, bq)
    bk = _tile(bq, bk)
    nq, kpq = seq // bq, bq // bk

    def body(q_ref, k_ref, v_ref, c_ref, do_ref, after_ref, dq_ref, dk_ref, dv_ref):
        row = lax.broadcasted_iota(jnp.int32, (bq, bk), 0)
        col = lax.broadcasted_iota(jnp.int32, (bq, bk), 1)
        sq_row = lax.broadcasted_iota(jnp.int32, (bk, bk), 0)
        sq_col = lax.broadcasted_iota(jnp.int32, (bk, bk), 1)
        tri = (sq_row >= sq_col).astype(BF16)
        tri_t = (sq_row <= sq_col).astype(BF16)
        dk_ref[...] = jnp.zeros_like(dk_ref)
        dv_ref[...] = jnp.zeros_like(dv_ref)

        def q_block(qi, carry):
            r0 = pl.multiple_of(qi * bq, bq)
            qh = [q_ref[pl.ds(r0, bq), ln] for ln in HEAD_LANES]
            d_out = [do_ref[pl.ds(r0, bq), ln].astype(BF16) for ln in HEAD_LANES]
            c_all = [c_ref[pl.ds(r0, bq), ln][:, 0:1] for ln in HEAD_LANES]

            def tile(k0, state, top=0):
                diag = top is not None
                top = top or 0
                last = diag and top == bq - bk
                msk = (col < row)[:bq - top] if diag else None
                new = []
                for h, ln in enumerate(HEAD_LANES):
                    c_left, g_left, dq = state[3 * h:3 * h + 3]
                    q, d_o = qh[h][top:], d_out[h][top:]
                    k = k_ref[pl.ds(k0, bk), ln]
                    z = _dot_nt(q, k)
                    e = jnp.exp(-jnp.abs(z))
                    sp = jnp.maximum(z, 0.0) + jnp.log(1.0 + e)
                    sig = jnp.exp(z - sp)
                    if diag:
                        sp = jnp.where(msk, sp, 0.0)
                    r = _running_sums(sp, tri)
                    c_new = c_left[top:] + r[:, 0:1]
                    a = jnp.exp(z - r - (0.0 if last else c_all[h][top:] - c_new))
                    if diag:
                        a = jnp.where(msk, a, 0.0)
                    g = a * _dot_nt(d_o, v_ref[pl.ds(k0, bk), ln])
                    pg = _running_sums(g, tri_t)
                    dz = g - sig * (g_left[top:] + pg)
                    if diag:
                        dz = jnp.where(msk, dz, 0.0)
                    dz = dz.astype(BF16)
                    dk_ref[pl.ds(k0, bk), ln] += _dot_tn(dz, q)
                    dv_ref[pl.ds(k0, bk), ln] += _dot_tn(a.astype(BF16), d_o)
                    g_new = g_left[top:] + pg[:, bk - 1:bk]
                    dq_new = dq[top:] + jnp.dot(dz, k, preferred_element_type=F32)
                    if top:
                        c_new = jnp.concatenate([c_left[:top], c_new])
                        g_new = jnp.concatenate([g_left[:top], g_new])
                        dq_new = jnp.concatenate([dq[:top], dq_new])
                    new += [c_new, g_new, dq_new]
                return tuple(new)

            zero = jnp.zeros((bq, 1), F32)
            init = (zero, zero, jnp.zeros((bq, HEAD_DIM), F32)) * len(HEAD_LANES)
            state = lax.fori_loop(0, qi * kpq, lambda it, st: tile(pl.multiple_of(it * bk, bk), st, None), init)
            for d in range(kpq):
                state = tile(pl.multiple_of(r0 + d * bk, bk), state, top=d * bk)
            for h, ln in enumerate(HEAD_LANES):
                dq_ref[pl.ds(r0, bq), ln] = state[3 * h + 2]
            return carry

        lax.fori_loop(0, nq, q_block, 0)

    spec = pl.BlockSpec((seq, LANES), lambda b, h: (b, h))
    shape = jax.ShapeDtypeStruct((batch * seq, width), F32)
    return pl.pallas_call(
        body, name="attn_bwd", grid=(batch, width // LANES),
        in_specs=[spec] * 5 + [pl.BlockSpec(memory_space=pl.ANY)], out_specs=[spec] * 3,
        out_shape=[shape] * 3, compiler_params=_params("parallel", "parallel"),
    )(qs, kn, v, c_tot, do, after)


def _pattern(rows, cols, hit):
    r, c = lax.broadcasted_iota(jnp.int32, (rows, cols), 0), lax.broadcasted_iota(jnp.int32, (rows, cols), 1)
    return hit(r, c).astype(F32)


def _s5_group_operators(lr_r, li_r, lr_c, li_c, log_dt, bt_re, bt_im, ct_re, ct_im, d_row):
    cs = SSM_CHUNK
    n_ch, n_state = bt_re.shape
    width = cs * n_ch
    dt = jnp.exp(log_dt)

    def spread(x, pattern):
        return jnp.dot(x, pattern, precision=F32_DOT, preferred_element_type=F32)

    twice = _pattern(n_state, 2 * n_state, lambda r, c: r == c % n_state)
    steps = lax.broadcasted_iota(jnp.int32, (cs + 1, 1), 0).astype(F32)
    mag = jnp.exp(steps * (lr_r * dt))
    ang = steps * (li_r * dt)
    pw_re, pw_im = mag * jnp.cos(ang), mag * jnp.sin(ang)
    num_re, num_im = pw_re[1:2] - 1.0, pw_im[1:2]
    den = lr_r * lr_r + li_r * li_r
    cf_re = (num_re * lr_r + num_im * li_r) / den
    cf_im = (num_im * lr_r - num_re * li_r) / den
    bb_re = spread(cf_re * bt_re - cf_im * bt_im, twice)
    bb_im = spread(cf_re * bt_im + cf_im * bt_re, twice)
    pw2_re, pw2_im = spread(pw_re, twice), spread(pw_im, twice)
    real_half = lax.broadcasted_iota(jnp.int32, (1, 2 * n_state), 1) < n_state
    blocks = []
    for s in range(cs):
        pr, pi = pw2_re[cs - 1 - s:cs - s], pw2_im[cs - 1 - s:cs - s]
        blocks.append(jnp.where(real_half, bb_re * pr - bb_im * pi, bb_re * pi + bb_im * pr))
    b_mat = jnp.concatenate(blocks, axis=0)
    la = pw2_re[cs:cs + 1]
    lb = jnp.where(real_half, -pw2_im[cs:cs + 1], pw2_im[cs:cs + 1])

    lane = lax.broadcasted_iota(jnp.int32, (1, width), 1)
    tile_out = _pattern(n_ch, width, lambda r, c: r == c % n_ch)
    c_re, c_im = spread(ct_re, tile_out), spread(ct_im, tile_out)

    k_row = lax.broadcasted_iota(jnp.int32, (1, cs), 1).astype(F32)
    m, a = jnp.exp(k_row * (lr_c * dt)), k_row * (li_c * dt)
    repeat = _pattern(cs, width, lambda r, c: r == c // n_ch)
    p_re, p_im = spread(m * jnp.cos(a), repeat), spread(m * jnp.sin(a), repeat)
    w_re, w_im = p_re * c_re - p_im * c_im, p_re * c_im + p_im * c_re
    skip = jnp.where((lane < n_ch) & (lane == lax.broadcasted_iota(jnp.int32, (n_ch, width), 0)),
                     spread(d_row, tile_out), 0.0)
    kt_row = (jnp.dot(bb_re[:, :n_state], w_re, precision=F32_DOT, preferred_element_type=F32)
              - jnp.dot(bb_im[:, :n_state], w_im, precision=F32_DOT, preferred_element_type=F32) + skip)
    bar_re, bar_im = jnp.exp(lr_c * dt) * jnp.cos(li_c * dt), jnp.exp(lr_c * dt) * jnp.sin(li_c * dt)
    w1_re, w1_im = w_re * bar_re - w_im * bar_im, w_re * bar_im + w_im * bar_re
    c_mat = jnp.concatenate([w1_re, -w1_im], axis=0)
    return kt_row, b_mat, c_mat, la, lb


def _s5_operator_inputs(lam_re, lam_im, log_dt, b_re, b_im, c_re, c_im, d_skip):
    return (lam_re[:, None, :], lam_im[:, None, :], lam_re[:, :, None], lam_im[:, :, None], log_dt[:, None, None],
            b_re.transpose(0, 2, 1), b_im.transpose(0, 2, 1), c_re.transpose(0, 2, 1), c_im.transpose(0, 2, 1),
            d_skip[:, None, :])


def _s5_operators_call(name, args, cotangents=None, gb=8):
    groups = args[0].shape[0]
    gb = _tile(groups, gb)
    n_in = len(args)

    def body(*refs):
        n_ct = 0 if cotangents is None else len(cotangents)
        ins, cts, outs = refs[:n_in], refs[n_in:n_in + n_ct], refs[n_in + n_ct:]
        for g in range(gb):
            vals = [r[g] for r in ins]
            if cotangents is None:
                res = _s5_group_operators(*vals)
            else:
                res = jax.vjp(_s5_group_operators, *vals)[1](tuple(c[g] for c in cts))
            for o, v in zip(outs, res):
                o[g] = v

    def spec(a):
        return pl.BlockSpec((gb, *a.shape[1:]), lambda i: (i, 0, 0))

    if cotangents is None:
        n_ch, n_state = args[5].shape[1:]
        width = SSM_CHUNK * n_ch
        out_shape = [jax.ShapeDtypeStruct((groups, *s), F32) for s in
                     ((n_ch, width), (width, 2 * n_state), (2 * n_state, width), (1, 2 * n_state), (1, 2 * n_state))]
    else:
        out_shape = [jax.ShapeDtypeStruct(a.shape, F32) for a in args]
    operands = [*args, *(cotangents or ())]
    return pl.pallas_call(
        body, name=name, grid=(groups // gb,), in_specs=[spec(a) for a in operands], out_specs=[spec(s) for s in out_shape],
        out_shape=out_shape, compiler_params=_params("parallel"),
    )(*operands)


GROUPS_PER_BLOCK = LANES // SSM_GROUP


def _tokens_to_groups(name, u, col_block, width):
    t = u.shape[0]
    n = t // SSM_CHUNK
    ch = SSM_CHUNK * SSM_GROUP
    blocks = width // LANES
    nb = GROUPS_PER_BLOCK

    def body(u_ref, o_ref):
        block = lax.broadcasted_iota(jnp.int32, (n, LANES), 1) // SSM_GROUP
        for half in range(SSM_CHUNK // nb):
            rows = [u_ref[pl.ds(half * nb + s, n, stride=SSM_CHUNK), :] for s in range(nb)]
            for shift in range(nb):
                merged = rows[shift]
                for b in range(1, nb):
                    merged = jnp.where(block == b, rows[(b + shift) % nb], merged)
                moved = pltpu.roll(merged, shift * SSM_GROUP, 1) if shift else merged
                for b in range(nb):
                    s = (b + shift) % nb
                    o_ref[b, :, half * LANES + s * SSM_GROUP:half * LANES + (s + 1) * SSM_GROUP] = (
                        moved[:, s * SSM_GROUP:(s + 1) * SSM_GROUP])

    return pl.pallas_call(
        body, name=name, grid=(blocks,),
        in_specs=[pl.BlockSpec((t, LANES), lambda j: (0, col_block * blocks + j))],
        out_specs=pl.BlockSpec((GROUPS_PER_BLOCK, n, ch), lambda j: (j, 0, 0)),
        out_shape=jax.ShapeDtypeStruct((width // SSM_GROUP, n, ch), F32), compiler_params=_params("parallel"),
    )(u)


def _groups_to_tokens(name, ug):
    groups, n, ch = ug.shape
    nb = GROUPS_PER_BLOCK

    def body(g_ref, o_ref, rows_ref):
        block = lax.broadcasted_iota(jnp.int32, (n, LANES), 1) // SSM_GROUP
        for half in range(SSM_CHUNK // nb):
            src = [g_ref[b, :, half * LANES:(half + 1) * LANES] for b in range(nb)]
            for shift in range(nb):
                merged = src[-shift % nb]
                for s in range(1, nb):
                    merged = jnp.where(block == s, src[(s - shift) % nb], merged)
                moved = pltpu.roll(merged, (nb - shift) * SSM_GROUP, 1) if shift else merged
                for b in range(nb):
                    rows_ref[(b + shift) % nb, :, b * SSM_GROUP:(b + 1) * SSM_GROUP] = moved[:, b * SSM_GROUP:(b + 1) * SSM_GROUP]
            for s in range(nb):
                o_ref[pl.ds(half * nb + s, n, stride=SSM_CHUNK), :] = rows_ref[s]

    return pl.pallas_call(
        body, name=name, grid=(groups // GROUPS_PER_BLOCK,),
        in_specs=[pl.BlockSpec((GROUPS_PER_BLOCK, n, ch), lambda j: (j, 0, 0))],
        out_specs=pl.BlockSpec((n * SSM_CHUNK, LANES), lambda j: (0, j)),
        out_shape=jax.ShapeDtypeStruct((n * SSM_CHUNK, groups * SSM_GROUP), F32),
        scratch_shapes=[pltpu.VMEM((nb, n, LANES), F32)], compiler_params=_params("parallel"),
    )(ug)


SCAN_ROWS = 8


def _toeplitz_to(tm_ref, g, kt_row):
    width = kt_row.shape[1]
    tm_ref[g] = jnp.zeros((width, width), F32)
    for s in range(SSM_CHUNK):
        tm_ref[g, s * SSM_GROUP:(s + 1) * SSM_GROUP, s * SSM_GROUP:] = kt_row[:, :width - s * SSM_GROUP]


def _lam_powers(la, lb, reverse):
    if reverse:
        lb = -lb

    def mul(p, q):
        return p[0] * q[0] - p[1] * q[1], p[0] * q[1] + p[1] * q[0]

    p1 = (la, lb)
    p2 = mul(p1, p1)
    p3 = mul(p2, p1)
    p4 = mul(p2, p2)
    rows = [p1, p2, p3, p4, mul(p4, p1), mul(p4, p2), mul(p4, p3), mul(p4, p4)]
    if reverse:
        rows = rows[::-1]
    idx = lax.broadcasted_iota(jnp.int32, (SCAN_ROWS, la.shape[1]), 0)
    tab_a = sum(jnp.where(idx == j, r[0], 0.0) for j, r in enumerate(rows))
    tab_b = sum(jnp.where(idx == j, r[1], 0.0) for j, r in enumerate(rows))
    return (p1, p2, p4), (tab_a, tab_b), idx


def _scan_block(e, carry, steps, table, idx, half, reverse):
    n = SCAN_ROWS
    for d, (pa, pb) in zip((1, 2, 4), steps):
        sh = pltpu.roll(e, n - d if reverse else d, 0)
        sh = jnp.where(idx < n - d if reverse else idx >= d, sh, 0.0)
        e = e + pa * sh + pb * pltpu.roll(sh, half, 1)
    tab_a, tab_b = table
    e = e + tab_a * carry + tab_b * pltpu.roll(carry, half, 1)
    shifted = jnp.where(idx == (n - 1 if reverse else 0), carry, pltpu.roll(e, n - 1 if reverse else 1, 0))
    edge = e[0:1] if reverse else e[n - 1:n]
    return shifted, jnp.broadcast_to(edge, e.shape)


def _s5_fwd(ug, kt_row, b_mat, c_mat, la, lb, *, batch, gb=8):
    groups, n, ch = ug.shape
    p2 = b_mat.shape[2]
    gb = _tile(groups, gb)
    nch = n // batch
    nblk = nch // SCAN_ROWS

    def body(u_ref, k_ref, b_ref, c_ref, la_ref, lb_ref, y_ref, x_ref, s_ref, tm_ref):
        for g in range(gb):
            _toeplitz_to(tm_ref, g, k_ref[g])
            s_ref[g] = jnp.dot(u_ref[g], b_ref[g], precision=F32_DOT, preferred_element_type=F32)
        powers = [_lam_powers(la_ref[g], lb_ref[g], False) for g in range(gb)]

        def step(blk, carries):
            new = []
            for g in range(gb):
                steps, table, idx = powers[g]
                for b in range(batch):
                    rows = pl.ds(pl.multiple_of(b * nch + blk * SCAN_ROWS, SCAN_ROWS), SCAN_ROWS)
                    x_in, carry = _scan_block(s_ref[g, rows, :], carries[g * batch + b], steps, table, idx, p2 // 2, False)
                    x_ref[g, rows, :] = x_in
                    new.append(carry)
            return tuple(new)

        lax.fori_loop(0, nblk, step, tuple(jnp.zeros((SCAN_ROWS, p2), F32) for _ in range(gb * batch)))
        for g in range(gb):
            y_ref[g] = (jnp.dot(u_ref[g], tm_ref[g], precision=F32_DOT, preferred_element_type=F32)
                        + jnp.dot(x_ref[g], c_ref[g], precision=F32_DOT, preferred_element_type=F32))

    def spec(a, b):
        return pl.BlockSpec((gb, a, b), lambda i: (i, 0, 0))

    return pl.pallas_call(
        body, name="s5_fwd", grid=(groups // gb,),
        in_specs=[spec(n, ch), spec(SSM_GROUP, ch), spec(ch, p2), spec(p2, ch), spec(1, p2), spec(1, p2)],
        out_specs=[spec(n, ch), spec(n, p2)],
        out_shape=[jax.ShapeDtypeStruct((groups, n, ch), F32), jax.ShapeDtypeStruct((groups, n, p2), F32)],
        scratch_shapes=[pltpu.VMEM((gb, n, p2), F32), pltpu.VMEM((gb, ch, ch), F32)],
        compiler_params=_params("parallel"),
    )(ug, kt_row, b_mat, c_mat, la, lb)


def _s5_bwd(ug, dyg, xin, kt_row, b_mat, c_mat, la, lb, *, batch, gb=8):
    groups, n, ch = ug.shape
    p2 = b_mat.shape[2]
    gb = _tile(groups, gb)
    nch = n // batch
    nblk = nch // SCAN_ROWS

    def body(u_ref, dy_ref, x_ref, k_ref, b_ref, c_ref, la_ref, lb_ref,
             du_ref, dk_ref, db_ref, dc_ref, dla_ref, dlb_ref, dx_ref, ds_ref, tm_ref):
        for g in range(gb):
            _toeplitz_to(tm_ref, g, k_ref[g])
            dx_ref[g] = _dot_nt(dy_ref[g], c_ref[g], precision=F32_DOT)
        powers = [_lam_powers(la_ref[g], lb_ref[g], True) for g in range(gb)]

        def step(it, carries):
            new = []
            for g in range(gb):
                steps, table, idx = powers[g]
                for b in range(batch):
                    rows = pl.ds(pl.multiple_of(b * nch + (nblk - 1 - it) * SCAN_ROWS, SCAN_ROWS), SCAN_ROWS)
                    d_s, carry = _scan_block(dx_ref[g, rows, :], carries[g * batch + b], steps, table, idx, p2 // 2, True)
                    ds_ref[g, rows, :] = d_s
                    new.append(carry)
            return tuple(new)

        lax.fori_loop(0, nblk, step, tuple(jnp.zeros((SCAN_ROWS, p2), F32) for _ in range(gb * batch)))
        for g in range(gb):
            u, dy, ds, x = u_ref[g], dy_ref[g], ds_ref[g], x_ref[g]
            du_ref[g] = _dot_nt(dy, tm_ref[g], precision=F32_DOT) + _dot_nt(ds, b_ref[g], precision=F32_DOT)
            tm_ref[g] = _dot_tn(u, dy, precision=F32_DOT)
            dk_ref[g] = tm_ref[g, 0:SSM_GROUP, :]
            for s in range(1, SSM_CHUNK):
                dk_ref[g, :, :ch - s * SSM_GROUP] += tm_ref[g, s * SSM_GROUP:(s + 1) * SSM_GROUP, s * SSM_GROUP:]
            db_ref[g] = _dot_tn(u, ds, precision=F32_DOT)
            dc_ref[g] = _dot_tn(x, dy, precision=F32_DOT)
            dla_ref[g] = jnp.sum(ds * x, axis=0, keepdims=True)
            dlb_ref[g] = jnp.sum(ds * pltpu.roll(x, p2 // 2, 1), axis=0, keepdims=True)

    def spec(a, b):
        return pl.BlockSpec((gb, a, b), lambda i: (i, 0, 0))

    def shape(a, b):
        return jax.ShapeDtypeStruct((groups, a, b), F32)

    return pl.pallas_call(
        body, name="s5_bwd", grid=(groups // gb,),
        in_specs=[spec(n, ch), spec(n, ch), spec(n, p2), spec(SSM_GROUP, ch), spec(ch, p2), spec(p2, ch), spec(1, p2),
                  spec(1, p2)],
        out_specs=[spec(n, ch), spec(SSM_GROUP, ch), spec(ch, p2), spec(p2, ch), spec(1, p2), spec(1, p2)],
        out_shape=[shape(n, ch), shape(SSM_GROUP, ch), shape(ch, p2), shape(p2, ch), shape(1, p2), shape(1, p2)],
        scratch_shapes=[pltpu.VMEM((gb, n, p2), F32), pltpu.VMEM((gb, n, p2), F32), pltpu.VMEM((gb, ch, ch), F32)],
        compiler_params=_params("parallel"),
    )(ug, dyg, xin, kt_row, b_mat, c_mat, la, lb)


def _block(ref, axis, j, size):
    start = j * size if isinstance(j, int) else pl.multiple_of(j * size, size)
    return ref.at[pl.ds(start, size), :] if axis == 0 else ref.at[:, pl.ds(start, size)]


def _chip_exchange_copies(mode, axes, srcs, lands, send_sems, recv_sems, local_sems):
    x, y, c = lax.axis_index("x"), lax.axis_index("y"), lax.axis_index("c")
    everyone = mode == "all"
    me = 4 * x + 2 * y + c if everyone else 2 * x + y
    n_peers = _exchange_peers(mode)
    local, sends, arrivals = [], [], []
    for w, axis in enumerate(axes):
        if mode == "gather":
            size = srcs[w].shape[axis]
            local.append(pltpu.make_async_copy(srcs[w], _block(lands[w], axis, me, size), local_sems.at[w]))
        elif mode == "scatter":
            size = srcs[w].shape[axis] // N_CHIPS
            local.append(pltpu.make_async_copy(_block(srcs[w], axis, me, size), lands[w].at[me], local_sems.at[w]))
        else:
            local.append(pltpu.make_async_copy(srcs[w], lands[w].at[me], local_sems.at[w]))
        for k in range(1, n_peers + 1):
            bits = k if everyone else 2 * k
            px = 1 - x if bits & 4 else x
            py = 1 - y if bits & 2 else y
            pc = 1 - c if bits & 1 else c
            peer = 4 * px + 2 * py + pc if everyone else 2 * px + py
            if mode == "gather":
                src, dst, arrive = srcs[w], _block(lands[w], axis, me, size), _block(lands[w], axis, peer, size)
            elif mode == "scatter":
                src, dst, arrive = _block(srcs[w], axis, peer, size), lands[w].at[me], lands[w].at[peer]
            else:
                src, dst, arrive = srcs[w], lands[w].at[me], lands[w].at[peer]
            sem = w * n_peers + k - 1
            for target, out in ((dst, sends), (arrive, arrivals)):
                out.append(pltpu.make_async_remote_copy(
                    src_ref=src, dst_ref=target, send_sem=send_sems.at[sem], recv_sem=recv_sems.at[sem],
                    device_id=(px, py, pc), device_id_type=MESH))
    return local, sends, arrivals


def _exchange_peers(mode):
    return N_DEV - 1 if mode == "all" else N_CHIPS - 1


def _chip_exchange_start(name, mode, items, after=None, groups=None):
    n = len(items)
    sizes = groups or [n]
    bounds = [(sum(sizes[:i]), sum(sizes[:i + 1])) for i in range(len(sizes))]
    n_after = 0 if after is None else 1
    axes = [axis for _, axis in items]
    hbm = pl.BlockSpec(memory_space=pltpu.HBM)
    sem = pl.BlockSpec(memory_space=pltpu.SEMAPHORE)
    lands = []
    for a, axis in items:
        shape = list(a.shape)
        if mode == "gather":
            shape[axis] *= N_CHIPS
        elif mode == "scatter":
            shape[axis] //= N_CHIPS
            shape = [N_CHIPS] + shape
        else:
            shape = [N_DEV] + shape
        lands.append(pltpu.with_memory_space_constraint(lax.empty(tuple(shape), a.dtype), pltpu.HBM))

    def body(*refs):
        srcs, land_refs = refs[:n], refs[n:2 * n]
        sems = refs[2 * n + n_after:2 * n + n_after + n_sems]
        token = refs[-1]
        for gi, (lo, hi) in enumerate(bounds):
            local, sends, _ = _chip_exchange_copies(mode, axes[lo:hi], srcs[lo:hi], land_refs[lo:hi], *sems[3 * gi:3 * gi + 3])
            for cp in local + sends:
                cp.start()
        token[...] = jnp.zeros_like(token)

    peers = _exchange_peers(mode)
    sem_shapes = [pltpu.SemaphoreType.DMA((count,)) for lo, hi in bounds
                  for count in ((hi - lo) * peers, (hi - lo) * peers, hi - lo)]
    n_sems = len(sem_shapes)
    outs = pl.pallas_call(
        body, name=name,
        out_shape=(*sem_shapes, *[pltpu.HBM(a.shape, a.dtype) for a, _ in items],
                   *[pltpu.HBM(l.shape, l.dtype) for l in lands], jax.ShapeDtypeStruct((8, LANES), F32)),
        in_specs=[hbm] * (2 * n) + [pl.BlockSpec(memory_space=pl.ANY)] * n_after,
        out_specs=(*[sem] * n_sems, *[hbm] * (2 * n), pl.BlockSpec(memory_space=pltpu.VMEM)),
        input_output_aliases={i: n_sems + i for i in range(2 * n)},
        compiler_params=pltpu.CompilerParams(has_side_effects=pltpu.SideEffectType.DATAFLOW_SIDE_EFFECTING),
    )(*[pltpu.with_memory_space_constraint(a, pltpu.HBM) for a, _ in items], *lands, *([after] if n_after else []))
    handles = [(mode, axes[lo:hi], outs[3 * gi:3 * gi + 3], outs[n_sems + lo:n_sems + hi],
                outs[n_sems + n + lo:n_sems + n + hi]) for gi, (lo, hi) in enumerate(bounds)]
    return (handles if groups else handles[0]), outs[-1][0:1, 0:1]


def _chip_exchange_wait(name, handle, after):
    mode, axes, sems, srcs, lands = handle
    n = len(axes)
    after = list(after) if isinstance(after, (tuple, list)) else [after]
    hbm = pl.BlockSpec(memory_space=pltpu.HBM)
    sem = pl.BlockSpec(memory_space=pltpu.SEMAPHORE)

    def body(*refs):
        src_refs, land_refs = refs[:n], refs[n:2 * n]
        send_sems, recv_sems, local_sems = refs[2 * n:2 * n + 3]
        local, sends, arrivals = _chip_exchange_copies(mode, axes, src_refs, land_refs, send_sems, recv_sems, local_sems)
        for cp in sends:
            cp.wait_send()
        for cp in arrivals:
            cp.wait_recv()
        for cp in local:
            cp.wait()

    outs = pl.pallas_call(
        body, name=name,
        out_shape=(*[pltpu.HBM(a.shape, a.dtype) for a in srcs], *[pltpu.HBM(l.shape, l.dtype) for l in lands]),
        in_specs=[hbm] * (2 * n) + [sem] * 3 + [pl.BlockSpec(memory_space=pl.ANY)] * len(after), out_specs=[hbm] * (2 * n),
        input_output_aliases={i: i for i in range(2 * n)},
        compiler_params=pltpu.CompilerParams(has_side_effects=pltpu.SideEffectType.DATAFLOW_SIDE_EFFECTING),
    )(*srcs, *lands, *sems, *after)
    return outs[n:]


def _sum_slots(name, slots, tm=256):
    n_slots, r, c = slots.shape
    tm = _tile(r, tm)

    def body(*refs):
        acc = refs[0][...]
        for s_ref in refs[1:n_slots]:
            acc = acc + s_ref[...]
        refs[n_slots][...] = acc

    specs = [pl.BlockSpec((None, tm, c), functools.partial(lambda i, s: (s, i, 0), s=s)) for s in range(n_slots)]
    return pl.pallas_call(
        body, name=name, grid=(r // tm,), in_specs=specs, out_specs=pl.BlockSpec((tm, c), lambda i: (i, 0)),
        out_shape=jax.ShapeDtypeStruct((r, c), F32), compiler_params=_params("parallel"),
    )(*[slots] * n_slots)


def _swap_with_sibling(name, arrays):
    n = len(arrays)
    hbm = pl.BlockSpec(memory_space=pl.ANY)

    def body(*refs):
        ins, outs = refs[:n], refs[n:2 * n]
        send_sems, recv_sems = refs[2 * n:]
        sibling = (lax.axis_index("x"), lax.axis_index("y"), 1 - lax.axis_index("c"))
        copies = [pltpu.make_async_remote_copy(src_ref=ins[w], dst_ref=outs[w], send_sem=send_sems.at[w],
                                               recv_sem=recv_sems.at[w], device_id=sibling, device_id_type=MESH)
                  for w in range(n)]
        for cp in copies:
            cp.start()
        for cp in copies:
            cp.wait()

    return pl.pallas_call(
        body, name=name, in_specs=[hbm] * n, out_specs=[hbm] * n,
        out_shape=[jax.ShapeDtypeStruct(a.shape, a.dtype) for a in arrays],
        scratch_shapes=[pltpu.SemaphoreType.DMA((n,)), pltpu.SemaphoreType.DMA((n,))],
    )(*arrays)


def _adamw(g, w, m, v):
    m = ADAM_B1 * m + (1.0 - ADAM_B1) * g
    v = ADAM_B2 * v + (1.0 - ADAM_B2) * jnp.square(g)
    m_hat = m / (1.0 - ADAM_B1 ** ADAM_STEP)
    v_hat = v / (1.0 - ADAM_B2 ** ADAM_STEP)
    delta = -ADAM_LR * (m_hat / (jnp.sqrt(v_hat) + ADAM_EPS) + ADAM_WD * w)
    return delta, m, v


def _adamw_small(grads, ws, ms, vs):
    n = len(ws)

    def whole(a):
        return pl.BlockSpec(a.shape, functools.partial(lambda i, nd: (0,) * nd, nd=a.ndim))

    def body(*refs):
        for i in range(n):
            g, w, m, v = (refs[k * n + i][...] for k in range(4))
            for k, val in enumerate(_adamw(g, w, m, v)):
                refs[(4 + k) * n + i][...] = val

    outs = pl.pallas_call(
        body, name="adamw_small", grid=(1,), in_specs=[whole(a) for a in (*grads, *ws, *ms, *vs)],
        out_specs=[whole(w) for _ in range(3) for w in ws],
        out_shape=[jax.ShapeDtypeStruct(w.shape, F32) for _ in range(3) for w in ws],
        compiler_params=pltpu.CompilerParams(vmem_limit_bytes=VMEM_LIMIT),
    )(*grads, *ws, *ms, *vs)
    return outs[:n], outs[n:2 * n], outs[2 * n:]


def kernel(x, norm1_g, w_in, q_norm_g, k_norm_g, ssm_lambda_re, ssm_lambda_im, ssm_log_dt, ssm_b_re, ssm_b_im, ssm_c_re, ssm_c_im, ssm_d, w_glu, b_glu, attn_out_g, ssm_out_g, w_out, norm2_g, w_mlp_in, w_mlp_out, loss_target, m_norm1_g, m_w_in, m_q_norm_g, m_k_norm_g, m_ssm_lambda_re, m_ssm_lambda_im, m_ssm_log_dt, m_ssm_b_re, m_ssm_b_im, m_ssm_c_re, m_ssm_c_im, m_ssm_d, m_w_glu, m_b_glu, m_attn_out_g, m_ssm_out_g, m_w_out, m_norm2_g, m_w_mlp_in, m_w_mlp_out, v_norm1_g, v_w_in, v_q_norm_g, v_k_norm_g, v_ssm_lambda_re, v_ssm_lambda_im, v_ssm_log_dt, v_ssm_b_re, v_ssm_b_im, v_ssm_c_re, v_ssm_c_im, v_ssm_d, v_w_glu, v_b_glu, v_attn_out_g, v_ssm_out_g, v_w_out, v_norm2_g, v_w_mlp_in, v_w_mlp_out):
    batch, seq, d_model = x.shape
    tokens = batch * seq
    sb_width = w_in.shape[1]
    n_features = d_model

    big = [("w_in", w_in, m_w_in, v_w_in, 1), ("w_glu", w_glu, m_w_glu, v_w_glu, 0),
           ("w_out", w_out, m_w_out, v_w_out, 0), ("w_mlp_in", w_mlp_in, m_w_mlp_in, v_w_mlp_in, 1),
           ("w_mlp_out", w_mlp_out, m_w_mlp_out, v_w_mlp_out, 0)]
    small = [("norm1_g", norm1_g, m_norm1_g, v_norm1_g), ("q_norm_g", q_norm_g, m_q_norm_g, v_q_norm_g),
             ("k_norm_g", k_norm_g, m_k_norm_g, v_k_norm_g),
             ("ssm_lambda_re", ssm_lambda_re, m_ssm_lambda_re, v_ssm_lambda_re),
             ("ssm_lambda_im", ssm_lambda_im, m_ssm_lambda_im, v_ssm_lambda_im),
             ("ssm_log_dt", ssm_log_dt, m_ssm_log_dt, v_ssm_log_dt),
             ("ssm_b_re", ssm_b_re, m_ssm_b_re, v_ssm_b_re), ("ssm_b_im", ssm_b_im, m_ssm_b_im, v_ssm_b_im),
             ("ssm_c_re", ssm_c_re, m_ssm_c_re, v_ssm_c_re), ("ssm_c_im", ssm_c_im, m_ssm_c_im, v_ssm_c_im),
             ("ssm_d", ssm_d, m_ssm_d, v_ssm_d), ("b_glu", b_glu, m_b_glu, v_b_glu),
             ("attn_out_g", attn_out_g, m_attn_out_g, v_attn_out_g), ("ssm_out_g", ssm_out_g, m_ssm_out_g, v_ssm_out_g),
             ("norm2_g", norm2_g, m_norm2_g, v_norm2_g)]

    (gather_in, gather_mix, gather_mlp), tok_rest = _chip_exchange_start(
        "gather_weights_start", "gather", [(w.astype(BF16), axis) for _, w, _, _, axis in big], groups=[1, 2, 2])

    x2 = x.reshape(tokens, d_model)
    tgt2 = loss_target.reshape(tokens, d_model)
    g1, g2 = norm1_g[None, :], norm2_g[None, :]
    g_attn, g_ssm, bias_glu = attn_out_g[None, :], ssm_out_g[None, :], b_glu[None, :]
    heads = sb_width // HEAD_DIM
    qk_scale = 1.0 / math.sqrt(HEAD_DIM)
    gq, gk = (jnp.tile(q_norm_g, heads) * qk_scale)[None, :], jnp.tile(k_norm_g, heads)[None, :]
    lane_head = jnp.arange(LANES) // HEAD_DIM
    ones_blocks = (lane_head[:, None] == lane_head[None, :]).astype(F32)

    (xn,) = _rowwise("norm1", _rms, [x2], [g1 + tok_rest], [(d_model, BF16)], tm=512)
    s5_in = _s5_operator_inputs(ssm_lambda_re, ssm_lambda_im, ssm_log_dt, ssm_b_re, ssm_b_im, ssm_c_re, ssm_c_im, ssm_d)
    kt_row, b_mat, c_mat, la, lb = _s5_operators_call("s5_operators", s5_in)
    (wf_in,) = _chip_exchange_wait("gather_w_in_wait", gather_in, [xn, b_mat, c_mat])
    def proj_head(acc, gq_, gk_, ones):
        q, k, v = (acc[:, i * sb_width:(i + 1) * sb_width] for i in range(3))
        return acc, _head_rms(q, gq_, ones), _head_rms(k, gk_, ones), v

    proj, qn, kn, vb = _mm("proj_in", xn, wf_in, "nn", epilogue=proj_head, full_rows=True, max_tm=512,
                           extras=[(gq, "row"), (gk, "row"), (ones_blocks, "whole")],
                           out_dtypes=(F32, (BF16, sb_width), (BF16, sb_width), (BF16, sb_width)))
    sb, c_tot = _attn_fwd(qn, kn, vb, batch=batch, seq=seq, bq=ATTN_BQ, bk=ATTN_BK)
    ug = _tokens_to_groups("u_to_groups", proj, 3, sb_width)
    yg, xin = _s5_fwd(ug, kt_row, b_mat, c_mat, la, lb, batch=batch)
    y_ssm = _groups_to_tokens("y_to_tokens", yg)

    wf_glu, wf_out = _chip_exchange_wait("gather_mix_wait", gather_mix, [y_ssm, sb])
    (gate_pre,) = _mm("glu_gate", y_ssm, wf_glu, "nn", a_fn=_gelu, extras=[(bias_glu, "row")],
                      epilogue=lambda acc, b: acc + b)
    (mixed,) = _rowwise("mix_norm", _mixed, [sb, y_ssm, gate_pre], [g_attn, g_ssm], [(2 * sb_width, BF16)], tm=512)
    def out_head(acc, r, g):
        h = acc + r
        return h, _rms(h, g)

    h1, hn = _mm("proj_out", mixed, wf_out, "nn", extras=[(x2, "tile"), (g2, "row")], epilogue=out_head,
                 out_dtypes=(F32, BF16), full_rows=True)
    def square(r):
        r = r.astype(F32)
        return r * r

    wf_mlp_in, wf_mlp_out = _chip_exchange_wait("gather_mlp_wait", gather_mlp, [h1, hn])
    (act_root,) = _mm("mlp_in", hn, wf_mlp_in, "nn", epilogue=lambda acc: jnp.maximum(acc, 0.0), out_dtypes=(BF16,))
    inv_n = 1.0 / n_features

    def loss_head(acc, r, t):
        d = ((acc + r) - t) * inv_n
        return d, d, jnp.sum(d * d, keepdims=True) * (0.5 * n_features)

    dy, dy_b, loss_tiles = _mm("mlp_out_loss", act_root, wf_mlp_out, "nn", a_fn=square, extras=[(h1, "tile"), (tgt2, "tile")],
                               epilogue=loss_head, out_dtypes=(F32, BF16), tile_sums=("scalar",))
    loss_part = jnp.sum(loss_tiles)

    (dw_mlp_out,) = _mm("dw_mlp_out", act_root, dy_b, "tn", a_fn=square)
    (dpre,) = _mm("d_mlp_act", dy_b, wf_mlp_out, "nt", extras=[(act_root, "tile")],
                  epilogue=lambda acc, r: acc * (2.0 * r.astype(F32)), out_dtypes=(BF16,))
    (dw_mlp_in,) = _mm("dw_mlp_in", hn, dpre, "tn")
    scatter_mlp, tok_mlp = _chip_exchange_start("scatter_mlp_start", "scatter", [(dw_mlp_in, 1), (dw_mlp_out, 0)])
    def norm_bwd(dn, res, hx, g):
        _, vjp = jax.vjp(_rms, hx, g)
        dh, dg = vjp(dn)
        return res + dh, dg

    dh1, dg_tiles = _mm("d_norm2_in", dpre, wf_mlp_in, "nt", extras=[(dy, "tile"), (h1, "tile"), (g2 + tok_mlp, "row")],
                        epilogue=norm_bwd, tile_sums=("row",), full_rows=True)
    dg_norm2 = jnp.sum(dg_tiles, axis=0, keepdims=True)
    (dw_out,) = _mm("dw_out", mixed, dh1, "tn")

    def mixed_bwd(dm, sb_, ys, gp, ga, gs):
        _, vjp = jax.vjp(lambda a, act, b, c, d: jnp.concatenate(
            [_rms(a, c), _rms(act * jax.nn.sigmoid(b), d)], axis=-1), sb_, _gelu(ys), gp, ga, gs)
        dsb_, dact, dgp_, dga, dgs = vjp(dm)
        return dsb_, dgp_, dact, dga, dgs, jnp.sum(dgp_, axis=0, keepdims=True)

    dsb, dgate_pre, dact_part, *gain_tiles = _mm(
        "d_mixed", dh1, wf_out, "nt", epilogue=mixed_bwd, full_rows=True, max_tm=256,
        extras=[(sb, "tile"), (y_ssm, "tile"), (gate_pre, "tile"), (g_attn, "row"), (g_ssm, "row")],
        out_dtypes=((F32, sb_width), (BF16, sb_width), (F32, sb_width)), tile_sums=(("row", sb_width),) * 3)
    dg_attn, dg_ssm, db_glu = (jnp.sum(t, axis=0, keepdims=True) for t in gain_tiles)

    def gelu_bwd(acc, part, ys):
        _, vjp = jax.vjp(_gelu, ys)
        return vjp(acc + part)[0]

    (dy_ssm,) = _mm("d_glu_in", dgate_pre, wf_glu, "nt", extras=[(dact_part, "tile"), (y_ssm, "tile")], epilogue=gelu_bwd)
    (dw_glu,) = _mm("dw_glu", y_ssm, dgate_pre, "tn", a_fn=_gelu)
    scatter_mix, tok_mix = _chip_exchange_start("scatter_mix_start", "scatter", [(dw_glu, 0), (dw_out, 0)])

    dug, dkt_row, db_mat, dc_mat, dla, dlb = _s5_bwd(ug, _tokens_to_groups("dy_to_groups", dy_ssm, 0, sb_width), xin,
                                                     kt_row, b_mat, c_mat, la, lb + tok_mix, batch=batch)
    du = _groups_to_tokens("du_to_tokens", dug)
    d_in = _s5_operators_call("s5_operators_bwd", s5_in, (dkt_row, db_mat, dc_mat, dla, dlb))
    ds5 = [d_in[0][:, 0, :] + d_in[2][:, :, 0], d_in[1][:, 0, :] + d_in[3][:, :, 0], d_in[4][:, 0, 0],
           d_in[5].transpose(0, 2, 1), d_in[6].transpose(0, 2, 1), d_in[7].transpose(0, 2, 1), d_in[8].transpose(0, 2, 1),
           d_in[9][:, 0, :]]

    def pack(parts):
        flat = jnp.concatenate([p.reshape(-1) for p in parts])
        rows = -(-flat.shape[0] // (8 * LANES)) * 8
        return jnp.pad(flat, (0, rows * LANES - flat.shape[0])).reshape(rows, LANES)

    def unpack(packed, names):
        flat, out, off = packed.reshape(-1), {}, 0
        for name in names:
            shape = small_shapes[name]
            size = math.prod(shape)
            out[name] = flat[off:off + size].reshape(shape)
            off += size
        return out, flat[off]

    small_shapes = {name: w.shape for name, w, _, _ in small}
    early_names = ["ssm_lambda_re", "ssm_lambda_im", "ssm_log_dt", "ssm_b_re", "ssm_b_im", "ssm_c_re", "ssm_c_im", "ssm_d",
                   "b_glu", "attn_out_g", "ssm_out_g", "norm2_g"]
    late_names = ["norm1_g", "q_norm_g", "k_norm_g"]
    early = pack([*ds5, db_glu[0], dg_attn[0], dg_ssm[0], dg_norm2[0], loss_part])
    early_exchange, _ = _chip_exchange_start("small_early_start", "all", [(early, 0)])

    dqn, dkn, dv = _attn_bwd(qn, kn, vb, c_tot, dsb, batch=batch, seq=seq, bq=ATTN_BQ, bk=ATTN_BK,
                             after=early_exchange[3][0])
    (early_slots,) = _chip_exchange_wait("small_early_wait", early_exchange, dqn)
    small_g, loss = unpack(_sum_slots("sum_small_early", early_slots), early_names)

    def qk_bwd(q, k, dq_, dk_, dv_, du_, gq_, gk_, ones):
        _, vjp_q = jax.vjp(lambda a, g: _head_rms(a, g, ones), q, gq_)
        _, vjp_k = jax.vjp(lambda a, g: _head_rms(a, g, ones), k, gk_)
        dq, dgq = vjp_q(dq_)
        dk, dgk = vjp_k(dk_)
        return jnp.concatenate([dq, dk, dv_, du_], axis=1), dgq, dgk

    dproj, dgq, dgk = _rowwise("qk_norm_bwd", qk_bwd, [(proj, sb_width, 0), (proj, sb_width, 1), dqn, dkn, dv, du],
                               [gq, gk, ones_blocks], [(4 * sb_width, BF16)], [(1, sb_width)] * 2, tm=512)
    (dw_in,) = _mm("dw_in", xn, dproj, "tn")
    scatter_in, tok_w_in = _chip_exchange_start("scatter_in_start", "scatter", [(dw_in, 1)])
    dx, dg_tiles = _mm("d_norm1_in", dproj, wf_in, "nt", extras=[(dh1, "tile"), (x2, "tile"), (g1 + tok_w_in, "row")],
                       epilogue=norm_bwd, tile_sums=("row",), full_rows=True)
    dg_norm1 = jnp.sum(dg_tiles, axis=0, keepdims=True)

    late = pack([dg_norm1[0], dgq.reshape(heads, HEAD_DIM).sum(0) * qk_scale, dgk.reshape(heads, HEAD_DIM).sum(0),
                 jnp.zeros((1,), F32)])
    late_exchange, _ = _chip_exchange_start("small_late_start", "all", [(late, 0)])

    def adam_big(sa, sb_, w, m, v):
        g = sa + sb_
        delta, m, v = _adamw(g, w, m, v)
        return g, delta, m, v

    def reduce_and_update(tag, params, slots):
        mine = [_sum_slots("sum_" + name, s) for s, (name, *_rest) in zip(slots, params)]
        theirs = _swap_with_sibling("swap_" + tag, mine)
        return {name: _rowwise("adamw_" + name, adam_big, [sa, sb_, w, m, v], [], [(w.shape[1], F32)] * 4)
                for (name, w, m, v, _), sa, sb_ in zip(params, mine, theirs)}

    started = late_exchange[3][0]
    slots_mlp_in, slots_mlp_out = _chip_exchange_wait("scatter_mlp_wait", scatter_mlp, started)
    slots_glu, slots_out = _chip_exchange_wait("scatter_mix_wait", scatter_mix, started)
    big_out = reduce_and_update("rest", big[1:], [slots_glu, slots_out, slots_mlp_in, slots_mlp_out])

    (late_slots,) = _chip_exchange_wait("small_late_wait", late_exchange, big_out["w_mlp_out"][3])
    reduced = _sum_slots("sum_small_late", late_slots)
    small_g.update(unpack(reduced, late_names)[0])
    narrow = {name for name, w, _, _ in small if w.ndim == 3 and w.shape[2] < w.shape[1]}

    def flip(a, name):
        return jnp.swapaxes(a, 1, 2) if name in narrow else a

    small_upd = _adamw_small([flip(small_g[name], name) for name, *_ in small], [flip(w, name) for name, w, _, _ in small],
                             [flip(m, name) for name, _, m, _ in small], [flip(v, name) for name, _, _, v in small])
    small_out = [small_g] + [{name: flip(small_upd[kind][i], name) for i, (name, *_) in enumerate(small)}
                             for kind in range(3)]

    (slots_in,) = _chip_exchange_wait("scatter_in_wait", scatter_in, reduced)
    big_out.update(reduce_and_update("w_in", big[:1], [slots_in]))
    names = ["norm1_g", "w_in", "q_norm_g", "k_norm_g", "ssm_lambda_re", "ssm_lambda_im", "ssm_log_dt", "ssm_b_re",
             "ssm_b_im", "ssm_c_re", "ssm_c_im", "ssm_d", "w_glu", "b_glu", "attn_out_g", "ssm_out_g", "w_out",
             "norm2_g", "w_mlp_in", "w_mlp_out"]
    outs = [loss, dx.reshape(batch, seq, d_model)]
    for kind in range(4):
        for name in names:
            outs.append(big_out[name][kind] if name in big_out else small_out[kind][name])
    return tuple(outs)
```

```python
import functools
import math

import jax
import jax.numpy as jnp
from jax import lax
from jax.experimental import pallas as pl
from jax.experimental.pallas import tpu as pltpu

F32 = jnp.float32
BF16 = jnp.bfloat16
F32_DOT = lax.Precision.HIGH
MESH = pl.DeviceIdType.MESH

RMS_EPS = 1e-6
HEAD_DIM = 64
SSM_GROUP = 16
SSM_CHUNK = 16
LANES = 128
N_CHIPS = 4
N_DEV = 8
VMEM_LIMIT = 48 * 1024 * 1024

ADAM_LR = 0.001
ADAM_B1 = 0.9
ADAM_B2 = 0.999
ADAM_EPS = 1e-08
ADAM_WD = 0.01
ADAM_STEP = 10


def _tile(n, pref):
    t = min(n, pref)
    while n % t:
        t //= 2
    return t


def _params(*sem):
    return pltpu.CompilerParams(dimension_semantics=sem, vmem_limit_bytes=VMEM_LIMIT)


_DIMS = {"nn": (((1,), (0,)), ((), ())), "nt": (((1,), (1,)), ((), ())), "tn": (((0,), (0,)), ((), ()))}


MM_VMEM_BUDGET = 40 * 1024 * 1024


def _mm_tiles(m, n, k, a_bytes, b_bytes, tile_bytes, full_rows=False, max_tm=1024):
    best = None
    for tk in [t for t in (k, k // 2, k // 4, k // 8) if t >= 256 or t == k]:
        for tm in [t for t in (1024, 512, 256, 128) if t <= min(m, max_tm) and m % t == 0]:
            for tn in [n] if full_rows else [t for t in (1024, 512, 256, 128) if t <= n and n % t == 0]:
                need = 2 * (tm * tk * a_bytes + tk * tn * b_bytes) + 2 * tm * tn * tile_bytes + (tm * tn * 4 if tk < k else 0)
                if need > MM_VMEM_BUDGET:
                    continue
                traffic = m * k * a_bytes * (1 if tk == k else n // tn) + k * n * b_bytes * (1 if n == tn and tk == k else m // tm)
                key = (tk < k, traffic, -tm * tn)
                if best is None or key < best[0]:
                    best = (key, (tm, tn, tk))
    return best[1]


def _mm(name, a, b, mode, *, a_fn=None, extras=(), epilogue=None, out_dtypes=(F32,), tile_sums=(), full_rows=False,
        max_tm=1024):
    if mode == "nn":
        (m, k), n = a.shape, b.shape[1]
    elif mode == "nt":
        (m, k), n = a.shape, b.shape[0]
    else:
        (k, m), n = a.shape, b.shape[1]
    outs_spec = [(d, n) if not isinstance(d, tuple) else d for d in out_dtypes]
    sums_spec = [(s, n) if not isinstance(s, tuple) else s for s in tile_sums]
    assert full_rows or all(w == n for _, w in outs_spec + sums_spec) and all(e.shape[1] == n for e, _ in extras)
    tile_bytes = (sum(e.dtype.itemsize * e.shape[1] for e, kind in extras if kind == "tile")
                  + sum(jnp.dtype(d).itemsize * w for d, w in outs_spec)) // n + 1
    tm, tn, tk = _mm_tiles(m, n, k, a.dtype.itemsize, b.dtype.itemsize, tile_bytes, full_rows, max_tm)
    nk = k // tk
    ne, nout = len(extras), len(out_dtypes)
    dims = _DIMS[mode]

    def width_spec(rows, w):
        if w == n:
            return pl.BlockSpec((rows, tn), (lambda i, j, kk: (i, j)) if rows != 1 else (lambda i, j, kk: (0, j)))
        return pl.BlockSpec((rows, w), (lambda i, j, kk: (i, 0)) if rows != 1 else (lambda i, j, kk: (0, 0)))

    def body(a_ref, b_ref, *rest):
        ex, outs, sums = rest[:ne], rest[ne:ne + nout], rest[ne + nout:ne + nout + len(tile_sums)]
        at = a_ref[...]
        if a_fn is not None:
            at = a_fn(at)
        p = lax.dot_general(at.astype(BF16), b_ref[...].astype(BF16), dims, preferred_element_type=F32)

        def finish(r):
            if epilogue is not None:
                r = epilogue(r, *[e[...] for e in ex])
            if not isinstance(r, (tuple, list)):
                r = (r,)
            for o, v in zip(outs, r[:nout]):
                o[...] = v.astype(o.dtype)
            for o, v, (kind, _) in zip(sums, r[nout:], sums_spec):
                first = lax.broadcasted_iota(jnp.int32, o.shape, 0) == 0
                if kind == "scalar":
                    first &= lax.broadcasted_iota(jnp.int32, o.shape, 1) == 0
                o[...] = jnp.where(first, v, 0.0)

        if nk == 1:
            finish(p)
        else:
            acc = rest[ne + nout + len(tile_sums)]
            kk = pl.program_id(2)

            @pl.when(kk == 0)
            def _():
                acc[...] = p

            @pl.when(kk > 0)
            def _():
                acc[...] += p

            @pl.when(kk == nk - 1)
            def _():
                finish(acc[...])

    if mode == "tn":
        a_spec = pl.BlockSpec((tk, tm), lambda i, j, kk: (kk, i))
    else:
        a_spec = pl.BlockSpec((tm, tk), lambda i, j, kk: (i, kk))
    if mode == "nt":
        b_spec = pl.BlockSpec((tn, tk), lambda i, j, kk: (j, kk))
    else:
        b_spec = pl.BlockSpec((tk, tn), lambda i, j, kk: (kk, j))
    ex_specs = [pl.BlockSpec(e.shape, lambda i, j, kk: (0, 0)) if kind == "whole"
                else width_spec(tm if kind == "tile" else 1, e.shape[1]) for e, kind in extras]
    return pl.pallas_call(
        body, name=name, grid=(m // tm, n // tn, nk),
        in_specs=[a_spec, b_spec] + ex_specs,
        out_specs=([width_spec(tm, w) for _, w in outs_spec]
                   + [pl.BlockSpec((8, LANES), lambda i, j, kk: (i, j)) if kind == "scalar" else width_spec(8, w)
                      for kind, w in sums_spec]),
        out_shape=([jax.ShapeDtypeStruct((m, w), dt) for dt, w in outs_spec]
                   + [jax.ShapeDtypeStruct((m // tm * 8, n // tn * LANES if kind == "scalar" else w), F32)
                      for kind, w in sums_spec]),
        scratch_shapes=[pltpu.VMEM((tm, tn), F32)] if nk > 1 else [],
        compiler_params=_params("parallel", "parallel", "arbitrary"),
    )(a, b, *[e for e, _ in extras])


def _rowwise(name, fn, rows, consts, row_outs, acc_outs=(), tm=256):
    norm = [r if isinstance(r, tuple) else (r, r.shape[1], 0) for r in rows]
    t = norm[0][0].shape[0]
    tm = _tile(t, tm)
    nr, nc, no = len(norm), len(consts), len(row_outs)

    def body(*refs):
        outs = fn(*[r[...] for r in refs[:nr + nc]])
        if not isinstance(outs, (tuple, list)):
            outs = (outs,)
        o_refs, a_refs = refs[nr + nc:nr + nc + no], refs[nr + nc + no:]
        for r, v in zip(o_refs, outs[:no]):
            r[...] = v.astype(r.dtype)
        if a_refs:
            i = pl.program_id(0)

            @pl.when(i == 0)
            def _():
                for r, v in zip(a_refs, outs[no:]):
                    r[...] = v

            @pl.when(i > 0)
            def _():
                for r, v in zip(a_refs, outs[no:]):
                    r[...] += v

    in_specs = [pl.BlockSpec((tm, w), functools.partial(lambda i, cb: (i, cb), cb=cb)) for _, w, cb in norm]
    in_specs += [pl.BlockSpec(c.shape, functools.partial(lambda i, nd: (0,) * nd, nd=c.ndim)) for c in consts]
    out_specs = [pl.BlockSpec((tm, w), lambda i: (i, 0)) for w, _ in row_outs]
    out_specs += [pl.BlockSpec(s, functools.partial(lambda i, nd: (0,) * nd, nd=len(s))) for s in acc_outs]
    out_shape = [jax.ShapeDtypeStruct((t, w), dt) for w, dt in row_outs]
    out_shape += [jax.ShapeDtypeStruct(s, F32) for s in acc_outs]
    return pl.pallas_call(
        body, name=name, grid=(t // tm,), in_specs=in_specs, out_specs=out_specs, out_shape=out_shape,
        compiler_params=_params("arbitrary"),
    )(*[r[0] for r in norm], *consts)


def _rms(x, g):
    return x * lax.rsqrt(jnp.mean(x * x, axis=-1, keepdims=True) + RMS_EPS) * g


@jax.custom_vjp
def _head_sums(x, ones_blocks):
    parts = [jnp.dot(x[:, j:j + LANES], ones_blocks, precision=F32_DOT, preferred_element_type=F32)
             for j in range(0, x.shape[1], LANES)]
    return jnp.concatenate(parts, axis=1)


_head_sums.defvjp(lambda x, ones_blocks: (_head_sums(x, ones_blocks), ones_blocks),
                  lambda ones_blocks, ct: (_head_sums(ct, ones_blocks), None))


def _head_rms(x, g, ones_blocks):
    return x * lax.rsqrt(_head_sums(x * x, ones_blocks) * (1.0 / HEAD_DIM) + RMS_EPS) * g


def _gelu(x):
    return x * (0.5 * (1.0 + jnp.tanh(math.sqrt(2.0 / math.pi) * (x + 0.044715 * (x * x * x)))))


def _mixed(sb, y_ssm, gate_pre, g_attn, g_ssm):
    ssm = _gelu(y_ssm) * jax.nn.sigmoid(gate_pre)
    return jnp.concatenate([_rms(sb, g_attn), _rms(ssm, g_ssm)], axis=-1)


def _softplus(z):
    return jnp.maximum(z, 0.0) + jnp.log(1.0 + jnp.exp(-jnp.abs(z)))


def _running_sums(x, tri):
    return jnp.dot(x.astype(BF16), tri, preferred_element_type=F32)


def _dot_nt(a, b, **kw):
    return lax.dot_general(a, b, _DIMS["nt"], preferred_element_type=F32, **kw)


def _dot_tn(a, b, **kw):
    return lax.dot_general(a, b, _DIMS["tn"], preferred_element_type=F32, **kw)


ATTN_BQ, ATTN_BK = 2048, 256
HEAD_LANES = tuple(slice(h * HEAD_DIM, (h + 1) * HEAD_DIM) for h in range(LANES // HEAD_DIM))


def _attn_fwd(qs, kn, v, *, batch, seq, bq, bk):
    width = qs.shape[1]
    bq = _tile(seq, bq)
    bk = _tile(bq, bk)
    nq, kpq = seq // bq, bq // bk

    def body(q_ref, k_ref, v_ref, o_ref, c_ref):
        row = lax.broadcasted_iota(jnp.int32, (bq, bk), 0)
        col = lax.broadcasted_iota(jnp.int32, (bq, bk), 1)
        tri = (lax.broadcasted_iota(jnp.int32, (bk, bk), 0) >= lax.broadcasted_iota(jnp.int32, (bk, bk), 1)).astype(BF16)

        def q_block(qi, carry):
            r0 = pl.multiple_of(qi * bq, bq)
            qh = [q_ref[pl.ds(r0, bq), ln] for ln in HEAD_LANES]

            def tile(k0, state, top=0):
                diag = top is not None
                top = top or 0
                msk = (col < row)[:bq - top] if diag else None
                new = []
                for h, ln in enumerate(HEAD_LANES):
                    o, c = state[2 * h], state[2 * h + 1]
                    z = _dot_nt(qh[h][top:], k_ref[pl.ds(k0, bk), ln])
                    sp = _softplus(z)
                    if diag:
                        sp = jnp.where(msk, sp, 0.0)
                    r = _running_sums(sp, tri)
                    a = jnp.exp(z - r - c[top:])
                    if diag:
                        a = jnp.where(msk, a, 0.0)
                    o_new = o[top:] + jnp.dot(a.astype(BF16), v_ref[pl.ds(k0, bk), ln], preferred_element_type=F32)
                    c_new = c[top:] + r[:, 0:1]
                    if top:
                        o_new, c_new = jnp.concatenate([o[:top], o_new]), jnp.concatenate([c[:top], c_new])
                    new += [o_new, c_new]
                return tuple(new)

            state = (jnp.zeros((bq, HEAD_DIM), F32), jnp.zeros((bq, 1), F32)) * len(HEAD_LANES)
            for d in reversed(range(kpq)):
                state = tile(pl.multiple_of(r0 + d * bk, bk), state, top=d * bk)
            state = lax.fori_loop(0, qi * kpq, lambda it, st: tile(pl.multiple_of(r0 - (it + 1) * bk, bk), st, None),
                                  state)
            for h, ln in enumerate(HEAD_LANES):
                o_ref[pl.ds(r0, bq), ln] = state[2 * h]
                c_ref[pl.ds(r0, bq), ln] = jnp.broadcast_to(state[2 * h + 1], (bq, HEAD_DIM))
            return carry

        lax.fori_loop(0, nq, q_block, 0)

    spec = pl.BlockSpec((seq, LANES), lambda b, h: (b, h))
    shape = jax.ShapeDtypeStruct((batch * seq, width), F32)
    return pl.pallas_call(
        body, name="attn_fwd", grid=(batch, width // LANES), in_specs=[spec, spec, spec], out_specs=[spec, spec],
        out_shape=[shape, shape], compiler_params=_params("parallel", "parallel"),
    )(qs, kn, v)


def _attn_bwd(qs, kn, v, c_tot, do, *, batch, seq, bq, bk, after):
    width = qs.shape[1]
    bq = _tile(seq, bq)
    bk = _tile(bq, bk)
    nq, kpq = seq // bq, bq // bk

    def body(q_ref, k_ref, v_ref, c_ref, do_ref, after_ref, dq_ref, dk_ref, dv_ref):
        row = lax.broadcasted_iota(jnp.int32, (bq, bk), 0)
        col = lax.broadcasted_iota(jnp.int32, (bq, bk), 1)
        sq_row = lax.broadcasted_iota(jnp.int32, (bk, bk), 0)
        sq_col = lax.broadcasted_iota(jnp.int32, (bk, bk), 1)
        tri = (sq_row >= sq_col).astype(BF16)
        tri_t = (sq_row <= sq_col).astype(BF16)
        dk_ref[...] = jnp.zeros_like(dk_ref)
        dv_ref[...] = jnp.zeros_like(dv_ref)

        def q_block(qi, carry):
            r0 = pl.multiple_of(qi * bq, bq)
            qh = [q_ref[pl.ds(r0, bq), ln] for ln in HEAD_LANES]
            d_out = [do_ref[pl.ds(r0, bq), ln].astype(BF16) for ln in HEAD_LANES]
            c_all = [c_ref[pl.ds(r0, bq), ln][:, 0:1] for ln in HEAD_LANES]

            def tile(k0, state, top=0):
                diag = top is not None
                top = top or 0
                last = diag and top == bq - bk
                msk = (col < row)[:bq - top] if diag else None
                new = []
                for h, ln in enumerate(HEAD_LANES):
                    c_left, g_left, dq = state[3 * h:3 * h + 3]
                    q, d_o = qh[h][top:], d_out[h][top:]
                    k = k_ref[pl.ds(k0, bk), ln]
                    z = _dot_nt(q, k)
                    e = jnp.exp(-jnp.abs(z))
                    sp = jnp.maximum(z, 0.0) + jnp.log(1.0 + e)
                    sig = jnp.exp(z - sp)
                    if diag:
                        sp = jnp.where(msk, sp, 0.0)
                    r = _running_sums(sp, tri)
                    c_new = c_left[top:] + r[:, 0:1]
                    a = jnp.exp(z - r - (0.0 if last else c_all[h][top:] - c_new))
                    if diag:
                        a = jnp.where(msk, a, 0.0)
                    g = a * _dot_nt(d_o, v_ref[pl.ds(k0, bk), ln])
                    pg = _running_sums(g, tri_t)
                    dz = g - sig * (g_left[top:] + pg)
                    if diag:
                        dz = jnp.where(msk, dz, 0.0)
                    dz = dz.astype(BF16)
                    dk_ref[pl.ds(k0, bk), ln] += _dot_tn(dz, q)
                    dv_ref[pl.ds(k0, bk), ln] += _dot_tn(a.astype(BF16), d_o)
                    g_new = g_left[top:] + pg[:, bk - 1:bk]
                    dq_new = dq[top:] + jnp.dot(dz, k, preferred_element_type=F32)
                    if top:
                        c_new = jnp.concatenate([c_left[:top], c_new])
                        g_new = jnp.concatenate([g_left[:top], g_new])
                        dq_new = jnp.concatenate([dq[:top], dq_new])
                    new += [c_new, g_new, dq_new]
                return tuple(new)

            zero = jnp.zeros((bq, 1), F32)
            init = (zero, zero, jnp.zeros((bq, HEAD_DIM), F32)) * len(HEAD_LANES)
            state = lax.fori_loop(0, qi * kpq, lambda it, st: tile(pl.multiple_of(it * bk, bk), st, None), init)
            for d in range(kpq):
                state = tile(pl.multiple_of(r0 + d * bk, bk), state, top=d * bk)
            for h, ln in enumerate(HEAD_LANES):
                dq_ref[pl.ds(r0, bq), ln] = state[3 * h + 2]
            return carry

        lax.fori_loop(0, nq, q_block, 0)

    spec = pl.BlockSpec((seq, LANES), lambda b, h: (b, h))
    shape = jax.ShapeDtypeStruct((batch * seq, width), F32)
    return pl.pallas_call(
        body, name="attn_bwd", grid=(batch, width // LANES),
        in_specs=[spec] * 5 + [pl.BlockSpec(memory_space=pl.ANY)], out_specs=[spec] * 3,
        out_shape=[shape] * 3, compiler_params=_params("parallel", "parallel"),
    )(qs, kn, v, c_tot, do, after)


def _pattern(rows, cols, hit):
    r, c = lax.broadcasted_iota(jnp.int32, (rows, cols), 0), lax.broadcasted_iota(jnp.int32, (rows, cols), 1)
    return hit(r, c).astype(F32)


def _s5_group_operators(lr_r, li_r, lr_c, li_c, log_dt, bt_re, bt_im, ct_re, ct_im, d_row):
    cs = SSM_CHUNK
    n_ch, n_state = bt_re.shape
    width = cs * n_ch
    dt = jnp.exp(log_dt)

    def spread(x, pattern):
        return jnp.dot(x, pattern, precision=F32_DOT, preferred_element_type=F32)

    twice = _pattern(n_state, 2 * n_state, lambda r, c: r == c % n_state)
    steps = lax.broadcasted_iota(jnp.int32, (cs + 1, 1), 0).astype(F32)
    mag = jnp.exp(steps * (lr_r * dt))
    ang = steps * (li_r * dt)
    pw_re, pw_im = mag * jnp.cos(ang), mag * jnp.sin(ang)
    num_re, num_im = pw_re[1:2] - 1.0, pw_im[1:2]
    den = lr_r * lr_r + li_r * li_r
    cf_re = (num_re * lr_r + num_im * li_r) / den
    cf_im = (num_im * lr_r - num_re * li_r) / den
    bb_re = spread(cf_re * bt_re - cf_im * bt_im, twice)
    bb_im = spread(cf_re * bt_im + cf_im * bt_re, twice)
    pw2_re, pw2_im = spread(pw_re, twice), spread(pw_im, twice)
    real_half = lax.broadcasted_iota(jnp.int32, (1, 2 * n_state), 1) < n_state
    blocks = []
    for s in range(cs):
        pr, pi = pw2_re[cs - 1 - s:cs - s], pw2_im[cs - 1 - s:cs - s]
        blocks.append(jnp.where(real_half, bb_re * pr - bb_im * pi, bb_re * pi + bb_im * pr))
    b_mat = jnp.concatenate(blocks, axis=0)
    la = pw2_re[cs:cs + 1]
    lb = jnp.where(real_half, -pw2_im[cs:cs + 1], pw2_im[cs:cs + 1])

    lane = lax.broadcasted_iota(jnp.int32, (1, width), 1)
    tile_out = _pattern(n_ch, width, lambda r, c: r == c % n_ch)
    c_re, c_im = spread(ct_re, tile_out), spread(ct_im, tile_out)

    k_row = lax.broadcasted_iota(jnp.int32, (1, cs), 1).astype(F32)
    m, a = jnp.exp(k_row * (lr_c * dt)), k_row * (li_c * dt)
    repeat = _pattern(cs, width, lambda r, c: r == c // n_ch)
    p_re, p_im = spread(m * jnp.cos(a), repeat), spread(m * jnp.sin(a), repeat)
    w_re, w_im = p_re * c_re - p_im * c_im, p_re * c_im + p_im * c_re
    skip = jnp.where((lane < n_ch) & (lane == lax.broadcasted_iota(jnp.int32, (n_ch, width), 0)),
                     spread(d_row, tile_out), 0.0)
    kt_row = (jnp.dot(bb_re[:, :n_state], w_re, precision=F32_DOT, preferred_element_type=F32)
              - jnp.dot(bb_im[:, :n_state], w_im, precision=F32_DOT, preferred_element_type=F32) + skip)
    bar_re, bar_im = jnp.exp(lr_c * dt) * jnp.cos(li_c * dt), jnp.exp(lr_c * dt) * jnp.sin(li_c * dt)
    w1_re, w1_im = w_re * bar_re - w_im * bar_im, w_re * bar_im + w_im * bar_re
    c_mat = jnp.concatenate([w1_re, -w1_im], axis=0)
    return kt_row, b_mat, c_mat, la, lb


def _s5_operator_inputs(lam_re, lam_im, log_dt, b_re, b_im, c_re, c_im, d_skip):
    return (lam_re[:, None, :], lam_im[:, None, :], lam_re[:, :, None], lam_im[:, :, None], log_dt[:, None, None],
            b_re.transpose(0, 2, 1), b_im.transpose(0, 2, 1), c_re.transpose(0, 2, 1), c_im.transpose(0, 2, 1),
            d_skip[:, None, :])


def _s5_operators_call(name, args, cotangents=None, gb=8):
    groups = args[0].shape[0]
    gb = _tile(groups, gb)
    n_in = len(args)

    def body(*refs):
        n_ct = 0 if cotangents is None else len(cotangents)
        ins, cts, outs = refs[:n_in], refs[n_in:n_in + n_ct], refs[n_in + n_ct:]
        for g in range(gb):
            vals = [r[g] for r in ins]
            if cotangents is None:
                res = _s5_group_operators(*vals)
            else:
                res = jax.vjp(_s5_group_operators, *vals)[1](tuple(c[g] for c in cts))
            for o, v in zip(outs, res):
                o[g] = v

    def spec(a):
        return pl.BlockSpec((gb, *a.shape[1:]), lambda i: (i, 0, 0))

    if cotangents is None:
        n_ch, n_state = args[5].shape[1:]
        width = SSM_CHUNK * n_ch
        out_shape = [jax.ShapeDtypeStruct((groups, *s), F32) for s in
                     ((n_ch, width), (width, 2 * n_state), (2 * n_state, width), (1, 2 * n_state), (1, 2 * n_state))]
    else:
        out_shape = [jax.ShapeDtypeStruct(a.shape, F32) for a in args]
    operands = [*args, *(cotangents or ())]
    return pl.pallas_call(
        body, name=name, grid=(groups // gb,), in_specs=[spec(a) for a in operands], out_specs=[spec(s) for s in out_shape],
        out_shape=out_shape, compiler_params=_params("parallel"),
    )(*operands)


GROUPS_PER_BLOCK = LANES // SSM_GROUP


def _tokens_to_groups(name, u, col_block, width):
    t = u.shape[0]
    n = t // SSM_CHUNK
    ch = SSM_CHUNK * SSM_GROUP
    blocks = width // LANES
    nb = GROUPS_PER_BLOCK

    def body(u_ref, o_ref):
        block = lax.broadcasted_iota(jnp.int32, (n, LANES), 1) // SSM_GROUP
        for half in range(SSM_CHUNK // nb):
            rows = [u_ref[pl.ds(half * nb + s, n, stride=SSM_CHUNK), :] for s in range(nb)]
            for shift in range(nb):
                merged = rows[shift]
                for b in range(1, nb):
                    merged = jnp.where(block == b, rows[(b + shift) % nb], merged)
                moved = pltpu.roll(merged, shift * SSM_GROUP, 1) if shift else merged
                for b in range(nb):
                    s = (b + shift) % nb
                    o_ref[b, :, half * LANES + s * SSM_GROUP:half * LANES + (s + 1) * SSM_GROUP] = (
                        moved[:, s * SSM_GROUP:(s + 1) * SSM_GROUP])

    return pl.pallas_call(
        body, name=name, grid=(blocks,),
        in_specs=[pl.BlockSpec((t, LANES), lambda j: (0, col_block * blocks + j))],
        out_specs=pl.BlockSpec((GROUPS_PER_BLOCK, n, ch), lambda j: (j, 0, 0)),
        out_shape=jax.ShapeDtypeStruct((width // SSM_GROUP, n, ch), F32), compiler_params=_params("parallel"),
    )(u)


def _groups_to_tokens(name, ug):
    groups, n, ch = ug.shape
    nb = GROUPS_PER_BLOCK

    def body(g_ref, o_ref, rows_ref):
        block = lax.broadcasted_iota(jnp.int32, (n, LANES), 1) // SSM_GROUP
        for half in range(SSM_CHUNK // nb):
            src = [g_ref[b, :, half * LANES:(half + 1) * LANES] for b in range(nb)]
            for shift in range(nb):
                merged = src[-shift % nb]
                for s in range(1, nb):
                    merged = jnp.where(block == s, src[(s - shift) % nb], merged)
                moved = pltpu.roll(merged, (nb - shift) * SSM_GROUP, 1) if shift else merged
                for b in range(nb):
                    rows_ref[(b + shift) % nb, :, b * SSM_GROUP:(b + 1) * SSM_GROUP] = moved[:, b * SSM_GROUP:(b + 1) * SSM_GROUP]
            for s in range(nb):
                o_ref[pl.ds(half * nb + s, n, stride=SSM_CHUNK), :] = rows_ref[s]

    return pl.pallas_call(
        body, name=name, grid=(groups // GROUPS_PER_BLOCK,),
        in_specs=[pl.BlockSpec((GROUPS_PER_BLOCK, n, ch), lambda j: (j, 0, 0))],
        out_specs=pl.BlockSpec((n * SSM_CHUNK, LANES), lambda j: (0, j)),
        out_shape=jax.ShapeDtypeStruct((n * SSM_CHUNK, groups * SSM_GROUP), F32),
        scratch_shapes=[pltpu.VMEM((nb, n, LANES), F32)], compiler_params=_params("parallel"),
    )(ug)


SCAN_ROWS = 8


def _toeplitz_to(tm_ref, g, kt_row):
    width = kt_row.shape[1]
    tm_ref[g] = jnp.zeros((width, width), F32)
    for s in range(SSM_CHUNK):
        tm_ref[g, s * SSM_GROUP:(s + 1) * SSM_GROUP, s * SSM_GROUP:] = kt_row[:, :width - s * SSM_GROUP]


def _lam_powers(la, lb, reverse):
    if reverse:
        lb = -lb

    def mul(p, q):
        return p[0] * q[0] - p[1] * q[1], p[0] * q[1] + p[1] * q[0]

    p1 = (la, lb)
    p2 = mul(p1, p1)
    p3 = mul(p2, p1)
    p4 = mul(p2, p2)
    rows = [p1, p2, p3, p4, mul(p4, p1), mul(p4, p2), mul(p4, p3), mul(p4, p4)]
    if reverse:
        rows = rows[::-1]
    idx = lax.broadcasted_iota(jnp.int32, (SCAN_ROWS, la.shape[1]), 0)
    tab_a = sum(jnp.where(idx == j, r[0], 0.0) for j, r in enumerate(rows))
    tab_b = sum(jnp.where(idx == j, r[1], 0.0) for j, r in enumerate(rows))
    return (p1, p2, p4), (tab_a, tab_b), idx


def _scan_block(e, carry, steps, table, idx, half, reverse):
    n = SCAN_ROWS
    for d, (pa, pb) in zip((1, 2, 4), steps):
        sh = pltpu.roll(e, n - d if reverse else d, 0)
        sh = jnp.where(idx < n - d if reverse else idx >= d, sh, 0.0)
        e = e + pa * sh + pb * pltpu.roll(sh, half, 1)
    tab_a, tab_b = table
    e = e + tab_a * carry + tab_b * pltpu.roll(carry, half, 1)
    shifted = jnp.where(idx == (n - 1 if reverse else 0), carry, pltpu.roll(e, n - 1 if reverse else 1, 0))
    edge = e[0:1] if reverse else e[n - 1:n]
    return shifted, jnp.broadcast_to(edge, e.shape)


def _s5_fwd(ug, kt_row, b_mat, c_mat, la, lb, *, batch, gb=8):
    groups, n, ch = ug.shape
    p2 = b_mat.shape[2]
    gb = _tile(groups, gb)
    nch = n // batch
    nblk = nch // SCAN_ROWS

    def body(u_ref, k_ref, b_ref, c_ref, la_ref, lb_ref, y_ref, x_ref, s_ref, tm_ref):
        for g in range(gb):
            _toeplitz_to(tm_ref, g, k_ref[g])
            s_ref[g] = jnp.dot(u_ref[g], b_ref[g], precision=F32_DOT, preferred_element_type=F32)
        powers = [_lam_powers(la_ref[g], lb_ref[g], False) for g in range(gb)]

        def step(blk, carries):
            new = []
            for g in range(gb):
                steps, table, idx = powers[g]
                for b in range(batch):
                    rows = pl.ds(pl.multiple_of(b * nch + blk * SCAN_ROWS, SCAN_ROWS), SCAN_ROWS)
                    x_in, carry = _scan_block(s_ref[g, rows, :], carries[g * batch + b], steps, table, idx, p2 // 2, False)
                    x_ref[g, rows, :] = x_in
                    new.append(carry)
            return tuple(new)

        lax.fori_loop(0, nblk, step, tuple(jnp.zeros((SCAN_ROWS, p2), F32) for _ in range(gb * batch)))
        for g in range(gb):
            y_ref[g] = (jnp.dot(u_ref[g], tm_ref[g], precision=F32_DOT, preferred_element_type=F32)
                        + jnp.dot(x_ref[g], c_ref[g], precision=F32_DOT, preferred_element_type=F32))

    def spec(a, b):
        return pl.BlockSpec((gb, a, b), lambda i: (i, 0, 0))

    return pl.pallas_call(
        body, name="s5_fwd", grid=(groups // gb,),
        in_specs=[spec(n, ch), spec(SSM_GROUP, ch), spec(ch, p2), spec(p2, ch), spec(1, p2), spec(1, p2)],
        out_specs=[spec(n, ch), spec(n, p2)],
        out_shape=[jax.ShapeDtypeStruct((groups, n, ch), F32), jax.ShapeDtypeStruct((groups, n, p2), F32)],
        scratch_shapes=[pltpu.VMEM((gb, n, p2), F32), pltpu.VMEM((gb, ch, ch), F32)],
        compiler_params=_params("parallel"),
    )(ug, kt_row, b_mat, c_mat, la, lb)


def _s5_bwd(ug, dyg, xin, kt_row, b_mat, c_mat, la, lb, *, batch, gb=8):
    groups, n, ch = ug.shape
    p2 = b_mat.shape[2]
    gb = _tile(groups, gb)
    nch = n // batch
    nblk = nch // SCAN_ROWS

    def body(u_ref, dy_ref, x_ref, k_ref, b_ref, c_ref, la_ref, lb_ref,
             du_ref, dk_ref, db_ref, dc_ref, dla_ref, dlb_ref, dx_ref, ds_ref, tm_ref):
        for g in range(gb):
            _toeplitz_to(tm_ref, g, k_ref[g])
            dx_ref[g] = _dot_nt(dy_ref[g], c_ref[g], precision=F32_DOT)
        powers = [_lam_powers(la_ref[g], lb_ref[g], True) for g in range(gb)]

        def step(it, carries):
            new = []
            for g in range(gb):
                steps, table, idx = powers[g]
                for b in range(batch):
                    rows = pl.ds(pl.multiple_of(b * nch + (nblk - 1 - it) * SCAN_ROWS, SCAN_ROWS), SCAN_ROWS)
                    d_s, carry = _scan_block(dx_ref[g, rows, :], carries[g * batch + b], steps, table, idx, p2 // 2, True)
                    ds_ref[g, rows, :] = d_s
                    new.append(carry)
            return tuple(new)

        lax.fori_loop(0, nblk, step, tuple(jnp.zeros((SCAN_ROWS, p2), F32) for _ in range(gb * batch)))
        for g in range(gb):
            u, dy, ds, x = u_ref[g], dy_ref[g], ds_ref[g], x_ref[g]
            du_ref[g] = _dot_nt(dy, tm_ref[g], precision=F32_DOT) + _dot_nt(ds, b_ref[g], precision=F32_DOT)
            tm_ref[g] = _dot_tn(u, dy, precision=F32_DOT)
            dk_ref[g] = tm_ref[g, 0:SSM_GROUP, :]
            for s in range(1, SSM_CHUNK):
                dk_ref[g, :, :ch - s * SSM_GROUP] += tm_ref[g, s * SSM_GROUP:(s + 1) * SSM_GROUP, s * SSM_GROUP:]
            db_ref[g] = _dot_tn(u, ds, precision=F32_DOT)
            dc_ref[g] = _dot_tn(x, dy, precision=F32_DOT)
            dla_ref[g] = jnp.sum(ds * x, axis=0, keepdims=True)
            dlb_ref[g] = jnp.sum(ds * pltpu.roll(x, p2 // 2, 1), axis=0, keepdims=True)

    def spec(a, b):
        return pl.BlockSpec((gb, a, b), lambda i: (i, 0, 0))

    def shape(a, b):
        return jax.ShapeDtypeStruct((groups, a, b), F32)

    return pl.pallas_call(
        body, name="s5_bwd", grid=(groups // gb,),
        in_specs=[spec(n, ch), spec(n, ch), spec(n, p2), spec(SSM_GROUP, ch), spec(ch, p2), spec(p2, ch), spec(1, p2),
                  spec(1, p2)],
        out_specs=[spec(n, ch), spec(SSM_GROUP, ch), spec(ch, p2), spec(p2, ch), spec(1, p2), spec(1, p2)],
        out_shape=[shape(n, ch), shape(SSM_GROUP, ch), shape(ch, p2), shape(p2, ch), shape(1, p2), shape(1, p2)],
        scratch_shapes=[pltpu.VMEM((gb, n, p2), F32), pltpu.VMEM((gb, n, p2), F32), pltpu.VMEM((gb, ch, ch), F32)],
        compiler_params=_params("parallel"),
    )(ug, dyg, xin, kt_row, b_mat, c_mat, la, lb)


def _block(ref, axis, j, size):
    start = j * size if isinstance(j, int) else pl.multiple_of(j * size, size)
    return ref.at[pl.ds(start, size), :] if axis == 0 else ref.at[:, pl.ds(start, size)]


def _chip_exchange_copies(mode, axes, srcs, lands, send_sems, recv_sems, local_sems):
    x, y, c = lax.axis_index("x"), lax.axis_index("y"), lax.axis_index("c")
    everyone = mode == "all"
    me = 4 * x + 2 * y + c if everyone else 2 * x + y
    n_peers = _exchange_peers(mode)
    local, sends, arrivals = [], [], []
    for w, axis in enumerate(axes):
        if mode == "gather":
            size = srcs[w].shape[axis]
            local.append(pltpu.make_async_copy(srcs[w], _block(lands[w], axis, me, size), local_sems.at[w]))
        elif mode == "scatter":
            size = srcs[w].shape[axis] // N_CHIPS
            local.append(pltpu.make_async_copy(_block(srcs[w], axis, me, size), lands[w].at[me], local_sems.at[w]))
        else:
            local.append(pltpu.make_async_copy(srcs[w], lands[w].at[me], local_sems.at[w]))
        for k in range(1, n_peers + 1):
            bits = k if everyone else 2 * k
            px = 1 - x if bits & 4 else x
            py = 1 - y if bits & 2 else y
            pc = 1 - c if bits & 1 else c
            peer = 4 * px + 2 * py + pc if everyone else 2 * px + py
            if mode == "gather":
                src, dst, arrive = srcs[w], _block(lands[w], axis, me, size), _block(lands[w], axis, peer, size)
            elif mode == "scatter":
                src, dst, arrive = _block(srcs[w], axis, peer, size), lands[w].at[me], lands[w].at[peer]
            else:
                src, dst, arrive = srcs[w], lands[w].at[me], lands[w].at[peer]
            sem = w * n_peers + k - 1
            for target, out in ((dst, sends), (arrive, arrivals)):
                out.append(pltpu.make_async_remote_copy(
                    src_ref=src, dst_ref=target, send_sem=send_sems.at[sem], recv_sem=recv_sems.at[sem],
                    device_id=(px, py, pc), device_id_type=MESH))
    return local, sends, arrivals


def _exchange_peers(mode):
    return N_DEV - 1 if mode == "all" else N_CHIPS - 1


def _chip_exchange_start(name, mode, items, after=None):
    n = len(items)
    n_after = 0 if after is None else 1
    axes = [axis for _, axis in items]
    hbm = pl.BlockSpec(memory_space=pltpu.HBM)
    sem = pl.BlockSpec(memory_space=pltpu.SEMAPHORE)
    lands = []
    for a, axis in items:
        shape = list(a.shape)
        if mode == "gather":
            shape[axis] *= N_CHIPS
        elif mode == "scatter":
            shape[axis] //= N_CHIPS
            shape = [N_CHIPS] + shape
        else:
            shape = [N_DEV] + shape
        lands.append(pltpu.with_memory_space_constraint(lax.empty(tuple(shape), a.dtype), pltpu.HBM))

    def body(*refs):
        srcs, land_refs = refs[:n], refs[n:2 * n]
        send_sems, recv_sems, local_sems = refs[2 * n + n_after:2 * n + n_after + 3]
        token = refs[-1]
        local, sends, _ = _chip_exchange_copies(mode, axes, srcs, land_refs, send_sems, recv_sems, local_sems)
        for cp in local + sends:
            cp.start()
        token[...] = jnp.zeros_like(token)

    n_sem = n * _exchange_peers(mode)
    outs = pl.pallas_call(
        body, name=name,
        out_shape=(pltpu.SemaphoreType.DMA((n_sem,)), pltpu.SemaphoreType.DMA((n_sem,)), pltpu.SemaphoreType.DMA((n,)),
                   *[pltpu.HBM(a.shape, a.dtype) for a, _ in items], *[pltpu.HBM(l.shape, l.dtype) for l in lands],
                   jax.ShapeDtypeStruct((8, LANES), F32)),
        in_specs=[hbm] * (2 * n) + [pl.BlockSpec(memory_space=pl.ANY)] * n_after,
        out_specs=(sem, sem, sem, *[hbm] * (2 * n), pl.BlockSpec(memory_space=pltpu.VMEM)),
        input_output_aliases={i: 3 + i for i in range(2 * n)},
        compiler_params=pltpu.CompilerParams(has_side_effects=pltpu.SideEffectType.DATAFLOW_SIDE_EFFECTING),
    )(*[pltpu.with_memory_space_constraint(a, pltpu.HBM) for a, _ in items], *lands, *([after] if n_after else []))
    return (mode, axes, outs[:3], outs[3:3 + n], outs[3 + n:3 + 2 * n]), outs[-1][0:1, 0:1]


def _chip_exchange_wait(name, handle, after):
    mode, axes, sems, srcs, lands = handle
    n = len(axes)
    after = list(after) if isinstance(after, (tuple, list)) else [after]
    hbm = pl.BlockSpec(memory_space=pltpu.HBM)
    sem = pl.BlockSpec(memory_space=pltpu.SEMAPHORE)

    def body(*refs):
        src_refs, land_refs = refs[:n], refs[n:2 * n]
        send_sems, recv_sems, local_sems = refs[2 * n:2 * n + 3]
        local, sends, arrivals = _chip_exchange_copies(mode, axes, src_refs, land_refs, send_sems, recv_sems, local_sems)
        for cp in sends:
            cp.wait_send()
        for cp in arrivals:
            cp.wait_recv()
        for cp in local:
            cp.wait()

    outs = pl.pallas_call(
        body, name=name,
        out_shape=(*[pltpu.HBM(a.shape, a.dtype) for a in srcs], *[pltpu.HBM(l.shape, l.dtype) for l in lands]),
        in_specs=[hbm] * (2 * n) + [sem] * 3 + [pl.BlockSpec(memory_space=pl.ANY)] * len(after), out_specs=[hbm] * (2 * n),
        input_output_aliases={i: i for i in range(2 * n)},
        compiler_params=pltpu.CompilerParams(has_side_effects=pltpu.SideEffectType.DATAFLOW_SIDE_EFFECTING),
    )(*srcs, *lands, *sems, *after)
    return outs[n:]


def _sum_slots(name, slots, tm=256):
    n_slots, r, c = slots.shape
    tm = _tile(r, tm)

    def body(*refs):
        acc = refs[0][...].astype(F32)
        for s_ref in refs[1:n_slots]:
            acc = acc + s_ref[...].astype(F32)
        refs[n_slots][...] = acc

    specs = [pl.BlockSpec((None, tm, c), functools.partial(lambda i, s: (s, i, 0), s=s)) for s in range(n_slots)]
    return pl.pallas_call(
        body, name=name, grid=(r // tm,), in_specs=specs, out_specs=pl.BlockSpec((tm, c), lambda i: (i, 0)),
        out_shape=jax.ShapeDtypeStruct((r, c), F32), compiler_params=_params("parallel"),
    )(*[slots] * n_slots)


def _swap_with_sibling(name, arrays):
    n = len(arrays)
    hbm = pl.BlockSpec(memory_space=pl.ANY)

    def body(*refs):
        ins, outs = refs[:n], refs[n:2 * n]
        send_sems, recv_sems = refs[2 * n:]
        sibling = (lax.axis_index("x"), lax.axis_index("y"), 1 - lax.axis_index("c"))
        copies = [pltpu.make_async_remote_copy(src_ref=ins[w], dst_ref=outs[w], send_sem=send_sems.at[w],
                                               recv_sem=recv_sems.at[w], device_id=sibling, device_id_type=MESH)
                  for w in range(n)]
        for cp in copies:
            cp.start()
        for cp in copies:
            cp.wait()

    return pl.pallas_call(
        body, name=name, in_specs=[hbm] * n, out_specs=[hbm] * n,
        out_shape=[jax.ShapeDtypeStruct(a.shape, a.dtype) for a in arrays],
        scratch_shapes=[pltpu.SemaphoreType.DMA((n,)), pltpu.SemaphoreType.DMA((n,))],
    )(*arrays)


def _adamw(g, w, m, v):
    m = ADAM_B1 * m + (1.0 - ADAM_B1) * g
    v = ADAM_B2 * v + (1.0 - ADAM_B2) * jnp.square(g)
    m_hat = m / (1.0 - ADAM_B1 ** ADAM_STEP)
    v_hat = v / (1.0 - ADAM_B2 ** ADAM_STEP)
    delta = -ADAM_LR * (m_hat / (jnp.sqrt(v_hat) + ADAM_EPS) + ADAM_WD * w)
    return delta, m, v


def _adamw_small(grads, ws, ms, vs):
    n = len(ws)

    def whole(a):
        return pl.BlockSpec(a.shape, functools.partial(lambda i, nd: (0,) * nd, nd=a.ndim))

    def body(*refs):
        for i in range(n):
            g, w, m, v = (refs[k * n + i][...] for k in range(4))
            for k, val in enumerate(_adamw(g, w, m, v)):
                refs[(4 + k) * n + i][...] = val

    outs = pl.pallas_call(
        body, name="adamw_small", grid=(1,), in_specs=[whole(a) for a in (*grads, *ws, *ms, *vs)],
        out_specs=[whole(w) for _ in range(3) for w in ws],
        out_shape=[jax.ShapeDtypeStruct(w.shape, F32) for _ in range(3) for w in ws],
        compiler_params=pltpu.CompilerParams(vmem_limit_bytes=VMEM_LIMIT),
    )(*grads, *ws, *ms, *vs)
    return outs[:n], outs[n:2 * n], outs[2 * n:]


def kernel(x, norm1_g, w_in, q_norm_g, k_norm_g, ssm_lambda_re, ssm_lambda_im, ssm_log_dt, ssm_b_re, ssm_b_im, ssm_c_re, ssm_c_im, ssm_d, w_glu, b_glu, attn_out_g, ssm_out_g, w_out, norm2_g, w_mlp_in, w_mlp_out, loss_target, m_norm1_g, m_w_in, m_q_norm_g, m_k_norm_g, m_ssm_lambda_re, m_ssm_lambda_im, m_ssm_log_dt, m_ssm_b_re, m_ssm_b_im, m_ssm_c_re, m_ssm_c_im, m_ssm_d, m_w_glu, m_b_glu, m_attn_out_g, m_ssm_out_g, m_w_out, m_norm2_g, m_w_mlp_in, m_w_mlp_out, v_norm1_g, v_w_in, v_q_norm_g, v_k_norm_g, v_ssm_lambda_re, v_ssm_lambda_im, v_ssm_log_dt, v_ssm_b_re, v_ssm_b_im, v_ssm_c_re, v_ssm_c_im, v_ssm_d, v_w_glu, v_b_glu, v_attn_out_g, v_ssm_out_g, v_w_out, v_norm2_g, v_w_mlp_in, v_w_mlp_out):
    batch, seq, d_model = x.shape
    tokens = batch * seq
    sb_width = w_in.shape[1]
    n_features = d_model

    big = [("w_in", w_in, m_w_in, v_w_in, 1), ("w_glu", w_glu, m_w_glu, v_w_glu, 0),
           ("w_out", w_out, m_w_out, v_w_out, 0), ("w_mlp_in", w_mlp_in, m_w_mlp_in, v_w_mlp_in, 1),
           ("w_mlp_out", w_mlp_out, m_w_mlp_out, v_w_mlp_out, 0)]
    small = [("norm1_g", norm1_g, m_norm1_g, v_norm1_g), ("q_norm_g", q_norm_g, m_q_norm_g, v_q_norm_g),
             ("k_norm_g", k_norm_g, m_k_norm_g, v_k_norm_g),
             ("ssm_lambda_re", ssm_lambda_re, m_ssm_lambda_re, v_ssm_lambda_re),
             ("ssm_lambda_im", ssm_lambda_im, m_ssm_lambda_im, v_ssm_lambda_im),
             ("ssm_log_dt", ssm_log_dt, m_ssm_log_dt, v_ssm_log_dt),
             ("ssm_b_re", ssm_b_re, m_ssm_b_re, v_ssm_b_re), ("ssm_b_im", ssm_b_im, m_ssm_b_im, v_ssm_b_im),
             ("ssm_c_re", ssm_c_re, m_ssm_c_re, v_ssm_c_re), ("ssm_c_im", ssm_c_im, m_ssm_c_im, v_ssm_c_im),
             ("ssm_d", ssm_d, m_ssm_d, v_ssm_d), ("b_glu", b_glu, m_b_glu, v_b_glu),
             ("attn_out_g", attn_out_g, m_attn_out_g, v_attn_out_g), ("ssm_out_g", ssm_out_g, m_ssm_out_g, v_ssm_out_g),
             ("norm2_g", norm2_g, m_norm2_g, v_norm2_g)]

    gather_in, tok_in = _chip_exchange_start("gather_w_in_start", "gather", [(w_in.astype(BF16), 1)])
    gather_mix, tok_mix_w = _chip_exchange_start(
        "gather_mix_start", "gather", [(w.astype(BF16), axis) for _, w, _, _, axis in big[1:3]], after=tok_in)
    gather_mlp, tok_rest = _chip_exchange_start(
        "gather_mlp_start", "gather", [(w.astype(BF16), axis) for _, w, _, _, axis in big[3:]], after=tok_mix_w)

    x2 = x.reshape(tokens, d_model)
    tgt2 = loss_target.reshape(tokens, d_model)
    g1, g2 = norm1_g[None, :], norm2_g[None, :]
    g_attn, g_ssm, bias_glu = attn_out_g[None, :], ssm_out_g[None, :], b_glu[None, :]
    heads = sb_width // HEAD_DIM
    qk_scale = 1.0 / math.sqrt(HEAD_DIM)
    gq, gk = (jnp.tile(q_norm_g, heads) * qk_scale)[None, :], jnp.tile(k_norm_g, heads)[None, :]
    lane_head = jnp.arange(LANES) // HEAD_DIM
    ones_blocks = (lane_head[:, None] == lane_head[None, :]).astype(F32)

    (xn,) = _rowwise("norm1", _rms, [x2], [g1 + tok_rest], [(d_model, BF16)], tm=512)
    s5_in = _s5_operator_inputs(ssm_lambda_re, ssm_lambda_im, ssm_log_dt, ssm_b_re, ssm_b_im, ssm_c_re, ssm_c_im, ssm_d)
    kt_row, b_mat, c_mat, la, lb = _s5_operators_call("s5_operators", s5_in)
    (wf_in,) = _chip_exchange_wait("gather_w_in_wait", gather_in, [xn, b_mat, c_mat])
    def proj_head(acc, gq_, gk_, ones):
        q, k, v = (acc[:, i * sb_width:(i + 1) * sb_width] for i in range(3))
        return acc, _head_rms(q, gq_, ones), _head_rms(k, gk_, ones), v

    proj, qn, kn, vb = _mm("proj_in", xn, wf_in, "nn", epilogue=proj_head, full_rows=True, max_tm=512,
                           extras=[(gq, "row"), (gk, "row"), (ones_blocks, "whole")],
                           out_dtypes=(F32, (BF16, sb_width), (BF16, sb_width), (BF16, sb_width)))
    sb, c_tot = _attn_fwd(qn, kn, vb, batch=batch, seq=seq, bq=ATTN_BQ, bk=ATTN_BK)
    ug = _tokens_to_groups("u_to_groups", proj, 3, sb_width)
    yg, xin = _s5_fwd(ug, kt_row, b_mat, c_mat, la, lb, batch=batch)
    y_ssm = _groups_to_tokens("y_to_tokens", yg)

    wf_glu, wf_out = _chip_exchange_wait("gather_mix_wait", gather_mix, [y_ssm, sb])
    (gate_pre,) = _mm("glu_gate", y_ssm, wf_glu, "nn", a_fn=_gelu, extras=[(bias_glu, "row")],
                      epilogue=lambda acc, b: acc + b)
    (mixed,) = _rowwise("mix_norm", _mixed, [sb, y_ssm, gate_pre], [g_attn, g_ssm], [(2 * sb_width, BF16)], tm=512)
    def out_head(acc, r, g):
        h = acc + r
        return h, _rms(h, g)

    h1, hn = _mm("proj_out", mixed, wf_out, "nn", extras=[(x2, "tile"), (g2, "row")], epilogue=out_head,
                 out_dtypes=(F32, BF16), full_rows=True)
    def square(r):
        r = r.astype(F32)
        return r * r

    wf_mlp_in, wf_mlp_out = _chip_exchange_wait("gather_mlp_wait", gather_mlp, [h1, hn])
    (act_root,) = _mm("mlp_in", hn, wf_mlp_in, "nn", epilogue=lambda acc: jnp.maximum(acc, 0.0), out_dtypes=(BF16,))
    inv_n = 1.0 / n_features

    def loss_head(acc, r, t):
        d = ((acc + r) - t) * inv_n
        return d, d, jnp.sum(d * d, keepdims=True) * (0.5 * n_features)

    dy, dy_b, loss_tiles = _mm("mlp_out_loss", act_root, wf_mlp_out, "nn", a_fn=square, extras=[(h1, "tile"), (tgt2, "tile")],
                               epilogue=loss_head, out_dtypes=(F32, BF16), tile_sums=("scalar",))
    loss_part = jnp.sum(loss_tiles)

    (dw_mlp_out,) = _mm("dw_mlp_out", act_root, dy_b, "tn", a_fn=square, out_dtypes=(BF16,))
    (dpre,) = _mm("d_mlp_act", dy_b, wf_mlp_out, "nt", extras=[(act_root, "tile")],
                  epilogue=lambda acc, r: acc * (2.0 * r.astype(F32)), out_dtypes=(BF16,))
    (dw_mlp_in,) = _mm("dw_mlp_in", hn, dpre, "tn", out_dtypes=(BF16,))
    scatter_mlp, tok_mlp = _chip_exchange_start("scatter_mlp_start", "scatter", [(dw_mlp_in, 1), (dw_mlp_out, 0)])
    def norm_bwd(dn, res, hx, g):
        _, vjp = jax.vjp(_rms, hx, g)
        dh, dg = vjp(dn)
        return res + dh, dg

    dh1, dg_tiles = _mm("d_norm2_in", dpre, wf_mlp_in, "nt", extras=[(dy, "tile"), (h1, "tile"), (g2 + tok_mlp, "row")],
                        epilogue=norm_bwd, tile_sums=("row",), full_rows=True)
    dg_norm2 = jnp.sum(dg_tiles, axis=0, keepdims=True)
    (dw_out,) = _mm("dw_out", mixed, dh1, "tn", out_dtypes=(BF16,))

    def mixed_bwd(dm, sb_, ys, gp, ga, gs):
        _, vjp = jax.vjp(lambda a, act, b, c, d: jnp.concatenate(
            [_rms(a, c), _rms(act * jax.nn.sigmoid(b), d)], axis=-1), sb_, _gelu(ys), gp, ga, gs)
        dsb_, dact, dgp_, dga, dgs = vjp(dm)
        return dsb_, dgp_, dact, dga, dgs, jnp.sum(dgp_, axis=0, keepdims=True)

    dsb, dgate_pre, dact_part, *gain_tiles = _mm(
        "d_mixed", dh1, wf_out, "nt", epilogue=mixed_bwd, full_rows=True, max_tm=256,
        extras=[(sb, "tile"), (y_ssm, "tile"), (gate_pre, "tile"), (g_attn, "row"), (g_ssm, "row")],
        out_dtypes=((F32, sb_width), (BF16, sb_width), (F32, sb_width)), tile_sums=(("row", sb_width),) * 3)
    dg_attn, dg_ssm, db_glu = (jnp.sum(t, axis=0, keepdims=True) for t in gain_tiles)

    def gelu_bwd(acc, part, ys):
        _, vjp = jax.vjp(_gelu, ys)
        return vjp(acc + part)[0]

    (dy_ssm,) = _mm("d_glu_in", dgate_pre, wf_glu, "nt", extras=[(dact_part, "tile"), (y_ssm, "tile")], epilogue=gelu_bwd)
    (dw_glu,) = _mm("dw_glu", y_ssm, dgate_pre, "tn", a_fn=_gelu, out_dtypes=(BF16,))
    scatter_mix, tok_mix = _chip_exchange_start("scatter_mix_start", "scatter", [(dw_glu, 0), (dw_out, 0)])

    dug, dkt_row, db_mat, dc_mat, dla, dlb = _s5_bwd(ug, _tokens_to_groups("dy_to_groups", dy_ssm, 0, sb_width), xin,
                                                     kt_row, b_mat, c_mat, la, lb + tok_mix, batch=batch)
    du = _groups_to_tokens("du_to_tokens", dug)
    d_in = _s5_operators_call("s5_operators_bwd", s5_in, (dkt_row, db_mat, dc_mat, dla, dlb))
    ds5 = [d_in[0][:, 0, :] + d_in[2][:, :, 0], d_in[1][:, 0, :] + d_in[3][:, :, 0], d_in[4][:, 0, 0],
           d_in[5].transpose(0, 2, 1), d_in[6].transpose(0, 2, 1), d_in[7].transpose(0, 2, 1), d_in[8].transpose(0, 2, 1),
           d_in[9][:, 0, :]]

    def pack(parts):
        flat = jnp.concatenate([p.reshape(-1) for p in parts])
        rows = -(-flat.shape[0] // (8 * LANES)) * 8
        return jnp.pad(flat, (0, rows * LANES - flat.shape[0])).reshape(rows, LANES)

    def unpack(packed, names):
        flat, out, off = packed.reshape(-1), {}, 0
        for name in names:
            shape = small_shapes[name]
            size = math.prod(shape)
            out[name] = flat[off:off + size].reshape(shape)
            off += size
        return out, flat[off]

    small_shapes = {name: w.shape for name, w, _, _ in small}
    early_names = ["ssm_lambda_re", "ssm_lambda_im", "ssm_log_dt", "ssm_b_re", "ssm_b_im", "ssm_c_re", "ssm_c_im", "ssm_d",
                   "b_glu", "attn_out_g", "ssm_out_g", "norm2_g"]
    late_names = ["norm1_g", "q_norm_g", "k_norm_g"]
    early = pack([*ds5, db_glu[0], dg_attn[0], dg_ssm[0], dg_norm2[0], loss_part])
    early_exchange, _ = _chip_exchange_start("small_early_start", "all", [(early, 0)])

    dqn, dkn, dv = _attn_bwd(qn, kn, vb, c_tot, dsb, batch=batch, seq=seq, bq=ATTN_BQ, bk=ATTN_BK,
                             after=early_exchange[3][0])
    (early_slots,) = _chip_exchange_wait("small_early_wait", early_exchange, dqn)
    small_g, loss = unpack(_sum_slots("sum_small_early", early_slots), early_names)

    def qk_bwd(q, k, dq_, dk_, dv_, du_, gq_, gk_, ones):
        _, vjp_q = jax.vjp(lambda a, g: _head_rms(a, g, ones), q, gq_)
        _, vjp_k = jax.vjp(lambda a, g: _head_rms(a, g, ones), k, gk_)
        dq, dgq = vjp_q(dq_)
        dk, dgk = vjp_k(dk_)
        return jnp.concatenate([dq, dk, dv_, du_], axis=1), dgq, dgk

    dproj, dgq, dgk = _rowwise("qk_norm_bwd", qk_bwd, [(proj, sb_width, 0), (proj, sb_width, 1), dqn, dkn, dv, du],
                               [gq, gk, ones_blocks], [(4 * sb_width, BF16)], [(1, sb_width)] * 2, tm=512)
    (dw_in,) = _mm("dw_in", xn, dproj, "tn", out_dtypes=(BF16,))
    scatter_in, tok_w_in = _chip_exchange_start("scatter_in_start", "scatter", [(dw_in, 1)])
    dx, dg_tiles = _mm("d_norm1_in", dproj, wf_in, "nt", extras=[(dh1, "tile"), (x2, "tile"), (g1 + tok_w_in, "row")],
                       epilogue=norm_bwd, tile_sums=("row",), full_rows=True)
    dg_norm1 = jnp.sum(dg_tiles, axis=0, keepdims=True)

    late = pack([dg_norm1[0], dgq.reshape(heads, HEAD_DIM).sum(0) * qk_scale, dgk.reshape(heads, HEAD_DIM).sum(0),
                 jnp.zeros((1,), F32)])
    late_exchange, _ = _chip_exchange_start("small_late_start", "all", [(late, 0)])

    def adam_big(sa, sb_, w, m, v):
        g = sa + sb_
        delta, m, v = _adamw(g, w, m, v)
        return g, delta, m, v

    def reduce_and_update(tag, params, slots):
        mine = [_sum_slots("sum_" + name, s) for s, (name, *_rest) in zip(slots, params)]
        theirs = _swap_with_sibling("swap_" + tag, mine)
        return {name: _rowwise("adamw_" + name, adam_big, [sa, sb_, w, m, v], [], [(w.shape[1], F32)] * 4)
                for (name, w, m, v, _), sa, sb_ in zip(params, mine, theirs)}

    started = late_exchange[3][0]
    slots_mlp_in, slots_mlp_out = _chip_exchange_wait("scatter_mlp_wait", scatter_mlp, started)
    slots_glu, slots_out = _chip_exchange_wait("scatter_mix_wait", scatter_mix, started)
    big_out = reduce_and_update("rest", big[1:], [slots_glu, slots_out, slots_mlp_in, slots_mlp_out])

    (late_slots,) = _chip_exchange_wait("small_late_wait", late_exchange, big_out["w_mlp_out"][3])
    reduced = _sum_slots("sum_small_late", late_slots)
    small_g.update(unpack(reduced, late_names)[0])
    narrow = {name for name, w, _, _ in small if w.ndim == 3 and w.shape[2] < w.shape[1]}

    def flip(a, name):
        return jnp.swapaxes(a, 1, 2) if name in narrow else a

    small_upd = _adamw_small([flip(small_g[name], name) for name, *_ in small], [flip(w, name) for name, w, _, _ in small],
                             [flip(m, name) for name, _, m, _ in small], [flip(v, name) for name, _, _, v in small])
    small_out = [small_g] + [{name: flip(small_upd[kind][i], name) for i, (name, *_) in enumerate(small)}
                             for kind in range(3)]

    (slots_in,) = _chip_exchange_wait("scatter_in_wait", scatter_in, reduced)
    big_out.update(reduce_and_update("w_in", big[:1], [slots_in]))
    names = ["norm1_g", "w_in", "q_norm_g", "k_norm_g", "ssm_lambda_re", "ssm_lambda_im", "ssm_log_dt", "ssm_b_re",
             "ssm_b_im", "ssm_c_re", "ssm_c_im", "ssm_d", "w_glu", "b_glu", "attn_out_g", "ssm_out_g", "w_out",
             "norm2_g", "w_mlp_in", "w_mlp_out"]
    outs = [loss, dx.reshape(batch, seq, d_model)]
    for kind in range(4):
        for name in names:
            outs.append(big_out[name][kind] if name in big_out else small_out[kind][name])
    return tuple(outs)
```

```python
import functools
import math

import jax
import jax.numpy as jnp
from jax import lax
from jax.experimental import pallas as pl
from jax.experimental.pallas import tpu as pltpu

F32 = jnp.float32
BF16 = jnp.bfloat16
F32_DOT = lax.Precision.HIGH
MESH = pl.DeviceIdType.MESH

RMS_EPS = 1e-6
HEAD_DIM = 64
SSM_GROUP = 16
SSM_CHUNK = 16
LANES = 128
N_CHIPS = 4
N_DEV = 8
VMEM_LIMIT = 48 * 1024 * 1024

ADAM_LR = 0.001
ADAM_B1 = 0.9
ADAM_B2 = 0.999
ADAM_EPS = 1e-08
ADAM_WD = 0.01
ADAM_STEP = 10


def _tile(n, pref):
    t = min(n, pref)
    while n % t:
        t //= 2
    return t


def _params(*sem):
    return pltpu.CompilerParams(dimension_semantics=sem, vmem_limit_bytes=VMEM_LIMIT)


_DIMS = {"nn": (((1,), (0,)), ((), ())), "nt": (((1,), (1,)), ((), ())), "tn": (((0,), (0,)), ((), ()))}


MM_VMEM_BUDGET = 40 * 1024 * 1024


def _mm_tiles(m, n, k, a_bytes, b_bytes, tile_bytes, full_rows=False, max_tm=1024):
    best = None
    for tk in [t for t in (k, k // 2, k // 4, k // 8) if t >= 256 or t == k]:
        for tm in [t for t in (1024, 512, 256, 128) if t <= min(m, max_tm) and m % t == 0]:
            for tn in [n] if full_rows else [t for t in (1024, 512, 256, 128) if t <= n and n % t == 0]:
                need = 2 * (tm * tk * a_bytes + tk * tn * b_bytes) + 2 * tm * tn * tile_bytes + (tm * tn * 4 if tk < k else 0)
                if need > MM_VMEM_BUDGET:
                    continue
                traffic = m * k * a_bytes * (1 if tk == k else n // tn) + k * n * b_bytes * (1 if n == tn and tk == k else m // tm)
                key = (tk < k, traffic, -tm * tn)
                if best is None or key < best[0]:
                    best = (key, (tm, tn, tk))
    return best[1]


def _mm(name, a, b, mode, *, a_fn=None, extras=(), epilogue=None, out_dtypes=(F32,), tile_sums=(), full_rows=False,
        max_tm=1024):
    if mode == "nn":
        (m, k), n = a.shape, b.shape[1]
    elif mode == "nt":
        (m, k), n = a.shape, b.shape[0]
    else:
        (k, m), n = a.shape, b.shape[1]
    outs_spec = [(d, n) if not isinstance(d, tuple) else d for d in out_dtypes]
    sums_spec = [(s, n) if not isinstance(s, tuple) else s for s in tile_sums]
    assert full_rows or all(w == n for _, w in outs_spec + sums_spec) and all(e.shape[1] == n for e, _ in extras)
    tile_bytes = (sum(e.dtype.itemsize * e.shape[1] for e, kind in extras if kind == "tile")
                  + sum(jnp.dtype(d).itemsize * w for d, w in outs_spec)) // n + 1
    tm, tn, tk = _mm_tiles(m, n, k, a.dtype.itemsize, b.dtype.itemsize, tile_bytes, full_rows, max_tm)
    nk = k // tk
    ne, nout = len(extras), len(out_dtypes)
    dims = _DIMS[mode]

    def width_spec(rows, w):
        if w == n:
            return pl.BlockSpec((rows, tn), (lambda i, j, kk: (i, j)) if rows != 1 else (lambda i, j, kk: (0, j)))
        return pl.BlockSpec((rows, w), (lambda i, j, kk: (i, 0)) if rows != 1 else (lambda i, j, kk: (0, 0)))

    def body(a_ref, b_ref, *rest):
        ex, outs, sums = rest[:ne], rest[ne:ne + nout], rest[ne + nout:ne + nout + len(tile_sums)]
        at = a_ref[...]
        if a_fn is not None:
            at = a_fn(at)
        p = lax.dot_general(at.astype(BF16), b_ref[...].astype(BF16), dims, preferred_element_type=F32)

        def finish(r):
            if epilogue is not None:
                r = epilogue(r, *[e[...] for e in ex])
            if not isinstance(r, (tuple, list)):
                r = (r,)
            for o, v in zip(outs, r[:nout]):
                o[...] = v.astype(o.dtype)
            for o, v, (kind, _) in zip(sums, r[nout:], sums_spec):
                first = lax.broadcasted_iota(jnp.int32, o.shape, 0) == 0
                if kind == "scalar":
                    first &= lax.broadcasted_iota(jnp.int32, o.shape, 1) == 0
                o[...] = jnp.where(first, v, 0.0)

        if nk == 1:
            finish(p)
        else:
            acc = rest[ne + nout + len(tile_sums)]
            kk = pl.program_id(2)

            @pl.when(kk == 0)
            def _():
                acc[...] = p

            @pl.when(kk > 0)
            def _():
                acc[...] += p

            @pl.when(kk == nk - 1)
            def _():
                finish(acc[...])

    if mode == "tn":
        a_spec = pl.BlockSpec((tk, tm), lambda i, j, kk: (kk, i))
    else:
        a_spec = pl.BlockSpec((tm, tk), lambda i, j, kk: (i, kk))
    if mode == "nt":
        b_spec = pl.BlockSpec((tn, tk), lambda i, j, kk: (j, kk))
    else:
        b_spec = pl.BlockSpec((tk, tn), lambda i, j, kk: (kk, j))
    ex_specs = [pl.BlockSpec(e.shape, lambda i, j, kk: (0, 0)) if kind == "whole"
                else width_spec(tm if kind == "tile" else 1, e.shape[1]) for e, kind in extras]
    return pl.pallas_call(
        body, name=name, grid=(m // tm, n // tn, nk),
        in_specs=[a_spec, b_spec] + ex_specs,
        out_specs=([width_spec(tm, w) for _, w in outs_spec]
                   + [pl.BlockSpec((8, LANES), lambda i, j, kk: (i, j)) if kind == "scalar" else width_spec(8, w)
                      for kind, w in sums_spec]),
        out_shape=([jax.ShapeDtypeStruct((m, w), dt) for dt, w in outs_spec]
                   + [jax.ShapeDtypeStruct((m // tm * 8, n // tn * LANES if kind == "scalar" else w), F32)
                      for kind, w in sums_spec]),
        scratch_shapes=[pltpu.VMEM((tm, tn), F32)] if nk > 1 else [],
        compiler_params=_params("parallel", "parallel", "arbitrary"),
    )(a, b, *[e for e, _ in extras])


def _rowwise(name, fn, rows, consts, row_outs, acc_outs=(), tm=256):
    norm = [r if isinstance(r, tuple) else (r, r.shape[1], 0) for r in rows]
    t = norm[0][0].shape[0]
    tm = _tile(t, tm)
    nr, nc, no = len(norm), len(consts), len(row_outs)

    def body(*refs):
        outs = fn(*[r[...] for r in refs[:nr + nc]])
        if not isinstance(outs, (tuple, list)):
            outs = (outs,)
        o_refs, a_refs = refs[nr + nc:nr + nc + no], refs[nr + nc + no:]
        for r, v in zip(o_refs, outs[:no]):
            r[...] = v.astype(r.dtype)
        if a_refs:
            i = pl.program_id(0)

            @pl.when(i == 0)
            def _():
                for r, v in zip(a_refs, outs[no:]):
                    r[...] = v

            @pl.when(i > 0)
            def _():
                for r, v in zip(a_refs, outs[no:]):
                    r[...] += v

    in_specs = [pl.BlockSpec((tm, w), functools.partial(lambda i, cb: (i, cb), cb=cb)) for _, w, cb in norm]
    in_specs += [pl.BlockSpec(c.shape, functools.partial(lambda i, nd: (0,) * nd, nd=c.ndim)) for c in consts]
    out_specs = [pl.BlockSpec((tm, w), lambda i: (i, 0)) for w, _ in row_outs]
    out_specs += [pl.BlockSpec(s, functools.partial(lambda i, nd: (0,) * nd, nd=len(s))) for s in acc_outs]
    out_shape = [jax.ShapeDtypeStruct((t, w), dt) for w, dt in row_outs]
    out_shape += [jax.ShapeDtypeStruct(s, F32) for s in acc_outs]
    return pl.pallas_call(
        body, name=name, grid=(t // tm,), in_specs=in_specs, out_specs=out_specs, out_shape=out_shape,
        compiler_params=_params("arbitrary"),
    )(*[r[0] for r in norm], *consts)


def _rms(x, g):
    return x * lax.rsqrt(jnp.mean(x * x, axis=-1, keepdims=True) + RMS_EPS) * g


@jax.custom_vjp
def _head_sums(x, ones_blocks):
    parts = [jnp.dot(x[:, j:j + LANES], ones_blocks, precision=F32_DOT, preferred_element_type=F32)
             for j in range(0, x.shape[1], LANES)]
    return jnp.concatenate(parts, axis=1)


_head_sums.defvjp(lambda x, ones_blocks: (_head_sums(x, ones_blocks), ones_blocks),
                  lambda ones_blocks, ct: (_head_sums(ct, ones_blocks), None))


def _head_rms(x, g, ones_blocks):
    return x * lax.rsqrt(_head_sums(x * x, ones_blocks) * (1.0 / HEAD_DIM) + RMS_EPS) * g


def _gelu(x):
    return x * (0.5 * (1.0 + jnp.tanh(math.sqrt(2.0 / math.pi) * (x + 0.044715 * (x * x * x)))))


def _mixed(sb, y_ssm, gate_pre, g_attn, g_ssm):
    ssm = _gelu(y_ssm) * jax.nn.sigmoid(gate_pre)
    return jnp.concatenate([_rms(sb, g_attn), _rms(ssm, g_ssm)], axis=-1)


def _softplus(z):
    return jnp.maximum(z, 0.0) + jnp.log(1.0 + jnp.exp(-jnp.abs(z)))


def _running_sums(x, tri):
    return jnp.dot(x.astype(BF16), tri, preferred_element_type=F32)


def _dot_nt(a, b, **kw):
    return lax.dot_general(a, b, _DIMS["nt"], preferred_element_type=F32, **kw)


def _dot_tn(a, b, **kw):
    return lax.dot_general(a, b, _DIMS["tn"], preferred_element_type=F32, **kw)


ATTN_BQ, ATTN_BK = 2048, 256
HEAD_LANES = tuple(slice(h * HEAD_DIM, (h + 1) * HEAD_DIM) for h in range(LANES // HEAD_DIM))


def _attn_fwd(qs, kn, v, *, batch, seq, bq, bk):
    width = qs.shape[1]
    bq = _tile(seq, bq)
    bk = _tile(bq, bk)
    nq, kpq = seq // bq, bq // bk

    def body(q_ref, k_ref, v_ref, o_ref, c_ref):
        row = lax.broadcasted_iota(jnp.int32, (bq, bk), 0)
        col = lax.broadcasted_iota(jnp.int32, (bq, bk), 1)
        tri = (lax.broadcasted_iota(jnp.int32, (bk, bk), 0) >= lax.broadcasted_iota(jnp.int32, (bk, bk), 1)).astype(BF16)

        def q_block(qi, carry):
            r0 = pl.multiple_of(qi * bq, bq)
            qh = [q_ref[pl.ds(r0, bq), ln] for ln in HEAD_LANES]

            def tile(k0, state, top=0):
                diag = top is not None
                top = top or 0
                msk = (col < row)[:bq - top] if diag else None
                new = []
                for h, ln in enumerate(HEAD_LANES):
                    o, c = state[2 * h], state[2 * h + 1]
                    z = _dot_nt(qh[h][top:], k_ref[pl.ds(k0, bk), ln])
                    sp = _softplus(z)
                    if diag:
                        sp = jnp.where(msk, sp, 0.0)
                    r = _running_sums(sp, tri)
                    a = jnp.exp(z - r - c[top:])
                    if diag:
                        a = jnp.where(msk, a, 0.0)
                    o_new = o[top:] + jnp.dot(a.astype(BF16), v_ref[pl.ds(k0, bk), ln], preferred_element_type=F32)
                    c_new = c[top:] + r[:, 0:1]
                    if top:
                        o_new, c_new = jnp.concatenate([o[:top], o_new]), jnp.concatenate([c[:top], c_new])
                    new += [o_new, c_new]
                return tuple(new)

            state = (jnp.zeros((bq, HEAD_DIM), F32), jnp.zeros((bq, 1), F32)) * len(HEAD_LANES)
            for d in reversed(range(kpq)):
                state = tile(pl.multiple_of(r0 + d * bk, bk), state, top=d * bk)
            state = lax.fori_loop(0, qi * kpq, lambda it, st: tile(pl.multiple_of(r0 - (it + 1) * bk, bk), st, None),
                                  state)
            for h, ln in enumerate(HEAD_LANES):
                o_ref[pl.ds(r0, bq), ln] = state[2 * h]
                c_ref[pl.ds(r0, bq), ln] = jnp.broadcast_to(state[2 * h + 1], (bq, HEAD_DIM))
            return carry

        lax.fori_loop(0, nq, q_block, 0)

    spec = pl.BlockSpec((seq, LANES), lambda b, h: (b, h))
    shape = jax.ShapeDtypeStruct((batch * seq, width), F32)
    return pl.pallas_call(
        body, name="attn_fwd", grid=(batch, width // LANES), in_specs=[spec, spec, spec], out_specs=[spec, spec],
        out_shape=[shape, shape], compiler_params=_params("parallel", "parallel"),
    )(qs, kn, v)


def _attn_bwd(qs, kn, v, c_tot, do, *, batch, seq, bq, bk, after):
    width = qs.shape[1]
    bq = _tile(seq, bq)
    bk = _tile(bq, bk)
    nq, kpq = seq // bq, bq // bk

    def body(q_ref, k_ref, v_ref, c_ref, do_ref, after_ref, dq_ref, dk_ref, dv_ref):
        row = lax.broadcasted_iota(jnp.int32, (bq, bk), 0)
        col = lax.broadcasted_iota(jnp.int32, (bq, bk), 1)
        sq_row = lax.broadcasted_iota(jnp.int32, (bk, bk), 0)
        sq_col = lax.broadcasted_iota(jnp.int32, (bk, bk), 1)
        tri = (sq_row >= sq_col).astype(BF16)
        tri_t = (sq_row <= sq_col).astype(BF16)
        dk_ref[...] = jnp.zeros_like(dk_ref)
        dv_ref[...] = jnp.zeros_like(dv_ref)

        def q_block(qi, carry):
            r0 = pl.multiple_of(qi * bq, bq)
            qh = [q_ref[pl.ds(r0, bq), ln] for ln in HEAD_LANES]
            d_out = [do_ref[pl.ds(r0, bq), ln].astype(BF16) for ln in HEAD_LANES]
            c_all = [c_ref[pl.ds(r0, bq), ln][:, 0:1] for ln in HEAD_LANES]

            def tile(k0, state, top=0):
                diag = top is not None
                top = top or 0
                last = diag and top == bq - bk
                msk = (col < row)[:bq - top] if diag else None
                new = []
                for h, ln in enumerate(HEAD_LANES):
                    c_left, g_left, dq = state[3 * h:3 * h + 3]
                    q, d_o = qh[h][top:], d_out[h][top:]
                    k = k_ref[pl.ds(k0, bk), ln]
                    z = _dot_nt(q, k)
                    e = jnp.exp(-jnp.abs(z))
                    sp = jnp.maximum(z, 0.0) + jnp.log(1.0 + e)
                    sig = jnp.exp(z - sp)
                    if diag:
                        sp = jnp.where(msk, sp, 0.0)
                    r = _running_sums(sp, tri)
                    c_new = c_left[top:] + r[:, 0:1]
                    a = jnp.exp(z - r - (0.0 if last else c_all[h][top:] - c_new))
                    if diag:
                        a = jnp.where(msk, a, 0.0)
                    g = a * _dot_nt(d_o, v_ref[pl.ds(k0, bk), ln])
                    pg = _running_sums(g, tri_t)
                    dz = g - sig * (g_left[top:] + pg)
                    if diag:
                        dz = jnp.where(msk, dz, 0.0)
                    dz = dz.astype(BF16)
                    dk_ref[pl.ds(k0, bk), ln] += _dot_tn(dz, q)
                    dv_ref[pl.ds(k0, bk), ln] += _dot_tn(a.astype(BF16), d_o)
                    g_new = g_left[top:] + pg[:, bk - 1:bk]
                    dq_new = dq[top:] + jnp.dot(dz, k, preferred_element_type=F32)
                    if top:
                        c_new = jnp.concatenate([c_left[:top], c_new])
                        g_new = jnp.concatenate([g_left[:top], g_new])
                        dq_new = jnp.concatenate([dq[:top], dq_new])
                    new += [c_new, g_new, dq_new]
                return tuple(new)

            zero = jnp.zeros((bq, 1), F32)
            init = (zero, zero, jnp.zeros((bq, HEAD_DIM), F32)) * len(HEAD_LANES)
            state = lax.fori_loop(0, qi * kpq, lambda it, st: tile(pl.multiple_of(it * bk, bk), st, None), init)
            for d in range(kpq):
                state = tile(pl.multiple_of(r0 + d * bk, bk), state, top=d * bk)
            for h, ln in enumerate(HEAD_LANES):
                dq_ref[pl.ds(r0, bq), ln] = state[3 * h + 2]
            return carry

        lax.fori_loop(0, nq, q_block, 0)

    spec = pl.BlockSpec((seq, LANES), lambda b, h: (b, h))
    shape = jax.ShapeDtypeStruct((batch * seq, width), F32)
    return pl.pallas_call(
        body, name="attn_bwd", grid=(batch, width // LANES),
        in_specs=[spec] * 5 + [pl.BlockSpec(memory_space=pl.ANY)], out_specs=[spec] * 3,
        out_shape=[shape] * 3, compiler_params=_params("parallel", "parallel"),
    )(qs, kn, v, c_tot, do, after)


def _pattern(rows, cols, hit):
    r, c = lax.broadcasted_iota(jnp.int32, (rows, cols), 0), lax.broadcasted_iota(jnp.int32, (rows, cols), 1)
    return hit(r, c).astype(F32)


def _s5_group_operators(lr_r, li_r, lr_c, li_c, log_dt, bt_re, bt_im, ct_re, ct_im, d_row):
    cs = SSM_CHUNK
    n_ch, n_state = bt_re.shape
    width = cs * n_ch
    dt = jnp.exp(log_dt)

    def spread(x, pattern):
        return jnp.dot(x, pattern, precision=F32_DOT, preferred_element_type=F32)

    twice = _pattern(n_state, 2 * n_state, lambda r, c: r == c % n_state)
    steps = lax.broadcasted_iota(jnp.int32, (cs + 1, 1), 0).astype(F32)
    mag = jnp.exp(steps * (lr_r * dt))
    ang = steps * (li_r * dt)
    pw_re, pw_im = mag * jnp.cos(ang), mag * jnp.sin(ang)
    num_re, num_im = pw_re[1:2] - 1.0, pw_im[1:2]
    den = lr_r * lr_r + li_r * li_r
    cf_re = (num_re * lr_r + num_im * li_r) / den
    cf_im = (num_im * lr_r - num_re * li_r) / den
    bb_re = spread(cf_re * bt_re - cf_im * bt_im, twice)
    bb_im = spread(cf_re * bt_im + cf_im * bt_re, twice)
    pw2_re, pw2_im = spread(pw_re, twice), spread(pw_im, twice)
    real_half = lax.broadcasted_iota(jnp.int32, (1, 2 * n_state), 1) < n_state
    blocks = []
    for s in range(cs):
        pr, pi = pw2_re[cs - 1 - s:cs - s], pw2_im[cs - 1 - s:cs - s]
        blocks.append(jnp.where(real_half, bb_re * pr - bb_im * pi, bb_re * pi + bb_im * pr))
    b_mat = jnp.concatenate(blocks, axis=0)
    la = pw2_re[cs:cs + 1]
    lb = jnp.where(real_half, -pw2_im[cs:cs + 1], pw2_im[cs:cs + 1])

    lane = lax.broadcasted_iota(jnp.int32, (1, width), 1)
    tile_out = _pattern(n_ch, width, lambda r, c: r == c % n_ch)
    c_re, c_im = spread(ct_re, tile_out), spread(ct_im, tile_out)

    k_row = lax.broadcasted_iota(jnp.int32, (1, cs), 1).astype(F32)
    m, a = jnp.exp(k_row * (lr_c * dt)), k_row * (li_c * dt)
    repeat = _pattern(cs, width, lambda r, c: r == c // n_ch)
    p_re, p_im = spread(m * jnp.cos(a), repeat), spread(m * jnp.sin(a), repeat)
    w_re, w_im = p_re * c_re - p_im * c_im, p_re * c_im + p_im * c_re
    skip = jnp.where((lane < n_ch) & (lane == lax.broadcasted_iota(jnp.int32, (n_ch, width), 0)),
                     spread(d_row, tile_out), 0.0)
    kt_row = (jnp.dot(bb_re[:, :n_state], w_re, precision=F32_DOT, preferred_element_type=F32)
              - jnp.dot(bb_im[:, :n_state], w_im, precision=F32_DOT, preferred_element_type=F32) + skip)
    bar_re, bar_im = jnp.exp(lr_c * dt) * jnp.cos(li_c * dt), jnp.exp(lr_c * dt) * jnp.sin(li_c * dt)
    w1_re, w1_im = w_re * bar_re - w_im * bar_im, w_re * bar_im + w_im * bar_re
    c_mat = jnp.concatenate([w1_re, -w1_im], axis=0)
    return kt_row, b_mat, c_mat, la, lb


def _s5_operator_inputs(lam_re, lam_im, log_dt, b_re, b_im, c_re, c_im, d_skip):
    return (lam_re[:, None, :], lam_im[:, None, :], lam_re[:, :, None], lam_im[:, :, None], log_dt[:, None, None],
            b_re.transpose(0, 2, 1), b_im.transpose(0, 2, 1), c_re.transpose(0, 2, 1), c_im.transpose(0, 2, 1),
            d_skip[:, None, :])


def _s5_operators_call(name, args, cotangents=None, gb=8):
    groups = args[0].shape[0]
    gb = _tile(groups, gb)
    n_in = len(args)

    def body(*refs):
        n_ct = 0 if cotangents is None else len(cotangents)
        ins, cts, outs = refs[:n_in], refs[n_in:n_in + n_ct], refs[n_in + n_ct:]
        for g in range(gb):
            vals = [r[g] for r in ins]
            if cotangents is None:
                res = _s5_group_operators(*vals)
            else:
                res = jax.vjp(_s5_group_operators, *vals)[1](tuple(c[g] for c in cts))
            for o, v in zip(outs, res):
                o[g] = v

    def spec(a):
        return pl.BlockSpec((gb, *a.shape[1:]), lambda i: (i, 0, 0))

    if cotangents is None:
        n_ch, n_state = args[5].shape[1:]
        width = SSM_CHUNK * n_ch
        out_shape = [jax.ShapeDtypeStruct((groups, *s), F32) for s in
                     ((n_ch, width), (width, 2 * n_state), (2 * n_state, width), (1, 2 * n_state), (1, 2 * n_state))]
    else:
        out_shape = [jax.ShapeDtypeStruct(a.shape, F32) for a in args]
    operands = [*args, *(cotangents or ())]
    return pl.pallas_call(
        body, name=name, grid=(groups // gb,), in_specs=[spec(a) for a in operands], out_specs=[spec(s) for s in out_shape],
        out_shape=out_shape, compiler_params=_params("parallel"),
    )(*operands)


GROUPS_PER_BLOCK = LANES // SSM_GROUP


def _tokens_to_groups(name, u, col_block, width):
    t = u.shape[0]
    n = t // SSM_CHUNK
    ch = SSM_CHUNK * SSM_GROUP
    blocks = width // LANES
    nb = GROUPS_PER_BLOCK

    def body(u_ref, o_ref):
        block = lax.broadcasted_iota(jnp.int32, (n, LANES), 1) // SSM_GROUP
        for half in range(SSM_CHUNK // nb):
            rows = [u_ref[pl.ds(half * nb + s, n, stride=SSM_CHUNK), :] for s in range(nb)]
            for shift in range(nb):
                merged = rows[shift]
                for b in range(1, nb):
                    merged = jnp.where(block == b, rows[(b + shift) % nb], merged)
                moved = pltpu.roll(merged, shift * SSM_GROUP, 1) if shift else merged
                for b in range(nb):
                    s = (b + shift) % nb
                    o_ref[b, :, half * LANES + s * SSM_GROUP:half * LANES + (s + 1) * SSM_GROUP] = (
                        moved[:, s * SSM_GROUP:(s + 1) * SSM_GROUP])

    return pl.pallas_call(
        body, name=name, grid=(blocks,),
        in_specs=[pl.BlockSpec((t, LANES), lambda j: (0, col_block * blocks + j))],
        out_specs=pl.BlockSpec((GROUPS_PER_BLOCK, n, ch), lambda j: (j, 0, 0)),
        out_shape=jax.ShapeDtypeStruct((width // SSM_GROUP, n, ch), F32), compiler_params=_params("parallel"),
    )(u)


def _groups_to_tokens(name, ug):
    groups, n, ch = ug.shape
    nb = GROUPS_PER_BLOCK

    def body(g_ref, o_ref, rows_ref):
        block = lax.broadcasted_iota(jnp.int32, (n, LANES), 1) // SSM_GROUP
        for half in range(SSM_CHUNK // nb):
            src = [g_ref[b, :, half * LANES:(half + 1) * LANES] for b in range(nb)]
            for shift in range(nb):
                merged = src[-shift % nb]
                for s in range(1, nb):
                    merged = jnp.where(block == s, src[(s - shift) % nb], merged)
                moved = pltpu.roll(merged, (nb - shift) * SSM_GROUP, 1) if shift else merged
                for b in range(nb):
                    rows_ref[(b + shift) % nb, :, b * SSM_GROUP:(b + 1) * SSM_GROUP] = moved[:, b * SSM_GROUP:(b + 1) * SSM_GROUP]
            for s in range(nb):
                o_ref[pl.ds(half * nb + s, n, stride=SSM_CHUNK), :] = rows_ref[s]

    return pl.pallas_call(
        body, name=name, grid=(groups // GROUPS_PER_BLOCK,),
        in_specs=[pl.BlockSpec((GROUPS_PER_BLOCK, n, ch), lambda j: (j, 0, 0))],
        out_specs=pl.BlockSpec((n * SSM_CHUNK, LANES), lambda j: (0, j)),
        out_shape=jax.ShapeDtypeStruct((n * SSM_CHUNK, groups * SSM_GROUP), F32),
        scratch_shapes=[pltpu.VMEM((nb, n, LANES), F32)], compiler_params=_params("parallel"),
    )(ug)


SCAN_ROWS = 8


def _toeplitz_to(tm_ref, g, kt_row):
    width = kt_row.shape[1]
    tm_ref[g] = jnp.zeros((width, width), F32)
    for s in range(SSM_CHUNK):
        tm_ref[g, s * SSM_GROUP:(s + 1) * SSM_GROUP, s * SSM_GROUP:] = kt_row[:, :width - s * SSM_GROUP]


def _lam_powers(la, lb, reverse):
    if reverse:
        lb = -lb

    def mul(p, q):
        return p[0] * q[0] - p[1] * q[1], p[0] * q[1] + p[1] * q[0]

    p1 = (la, lb)
    p2 = mul(p1, p1)
    p3 = mul(p2, p1)
    p4 = mul(p2, p2)
    rows = [p1, p2, p3, p4, mul(p4, p1), mul(p4, p2), mul(p4, p3), mul(p4, p4)]
    if reverse:
        rows = rows[::-1]
    idx = lax.broadcasted_iota(jnp.int32, (SCAN_ROWS, la.shape[1]), 0)
    tab_a = sum(jnp.where(idx == j, r[0], 0.0) for j, r in enumerate(rows))
    tab_b = sum(jnp.where(idx == j, r[1], 0.0) for j, r in enumerate(rows))
    return (p1, p2, p4), (tab_a, tab_b), idx


def _scan_block(e, carry, steps, table, idx, half, reverse):
    n = SCAN_ROWS
    for d, (pa, pb) in zip((1, 2, 4), steps):
        sh = pltpu.roll(e, n - d if reverse else d, 0)
        sh = jnp.where(idx < n - d if reverse else idx >= d, sh, 0.0)
        e = e + pa * sh + pb * pltpu.roll(sh, half, 1)
    tab_a, tab_b = table
    e = e + tab_a * carry + tab_b * pltpu.roll(carry, half, 1)
    shifted = jnp.where(idx == (n - 1 if reverse else 0), carry, pltpu.roll(e, n - 1 if reverse else 1, 0))
    edge = e[0:1] if reverse else e[n - 1:n]
    return shifted, jnp.broadcast_to(edge, e.shape)


def _s5_fwd(ug, kt_row, b_mat, c_mat, la, lb, *, batch, gb=8):
    groups, n, ch = ug.shape
    p2 = b_mat.shape[2]
    gb = _tile(groups, gb)
    nch = n // batch
    nblk = nch // SCAN_ROWS

    def body(u_ref, k_ref, b_ref, c_ref, la_ref, lb_ref, y_ref, x_ref, s_ref, tm_ref):
        for g in range(gb):
            _toeplitz_to(tm_ref, g, k_ref[g])
            s_ref[g] = jnp.dot(u_ref[g], b_ref[g], precision=F32_DOT, preferred_element_type=F32)
        powers = [_lam_powers(la_ref[g], lb_ref[g], False) for g in range(gb)]

        def step(blk, carries):
            new = []
            for g in range(gb):
                steps, table, idx = powers[g]
                for b in range(batch):
                    rows = pl.ds(pl.multiple_of(b * nch + blk * SCAN_ROWS, SCAN_ROWS), SCAN_ROWS)
                    x_in, carry = _scan_block(s_ref[g, rows, :], carries[g * batch + b], steps, table, idx, p2 // 2, False)
                    x_ref[g, rows, :] = x_in
                    new.append(carry)
            return tuple(new)

        lax.fori_loop(0, nblk, step, tuple(jnp.zeros((SCAN_ROWS, p2), F32) for _ in range(gb * batch)))
        for g in range(gb):
            y_ref[g] = (jnp.dot(u_ref[g], tm_ref[g], precision=F32_DOT, preferred_element_type=F32)
                        + jnp.dot(x_ref[g], c_ref[g], precision=F32_DOT, preferred_element_type=F32))

    def spec(a, b):
        return pl.BlockSpec((gb, a, b), lambda i: (i, 0, 0))

    return pl.pallas_call(
        body, name="s5_fwd", grid=(groups // gb,),
        in_specs=[spec(n, ch), spec(SSM_GROUP, ch), spec(ch, p2), spec(p2, ch), spec(1, p2), spec(1, p2)],
        out_specs=[spec(n, ch), spec(n, p2)],
        out_shape=[jax.ShapeDtypeStruct((groups, n, ch), F32), jax.ShapeDtypeStruct((groups, n, p2), F32)],
        scratch_shapes=[pltpu.VMEM((gb, n, p2), F32), pltpu.VMEM((gb, ch, ch), F32)],
        compiler_params=_params("parallel"),
    )(ug, kt_row, b_mat, c_mat, la, lb)


def _s5_bwd(ug, dyg, xin, kt_row, b_mat, c_mat, la, lb, *, batch, gb=8):
    groups, n, ch = ug.shape
    p2 = b_mat.shape[2]
    gb = _tile(groups, gb)
    nch = n // batch
    nblk = nch // SCAN_ROWS

    def body(u_ref, dy_ref, x_ref, k_ref, b_ref, c_ref, la_ref, lb_ref,
             du_ref, dk_ref, db_ref, dc_ref, dla_ref, dlb_ref, dx_ref, ds_ref, tm_ref):
        for g in range(gb):
            _toeplitz_to(tm_ref, g, k_ref[g])
            dx_ref[g] = _dot_nt(dy_ref[g], c_ref[g], precision=F32_DOT)
        powers = [_lam_powers(la_ref[g], lb_ref[g], True) for g in range(gb)]

        def step(it, carries):
            new = []
            for g in range(gb):
                steps, table, idx = powers[g]
                for b in range(batch):
                    rows = pl.ds(pl.multiple_of(b * nch + (nblk - 1 - it) * SCAN_ROWS, SCAN_ROWS), SCAN_ROWS)
                    d_s, carry = _scan_block(dx_ref[g, rows, :], carries[g * batch + b], steps, table, idx, p2 // 2, True)
                    ds_ref[g, rows, :] = d_s
                    new.append(carry)
            return tuple(new)

        lax.fori_loop(0, nblk, step, tuple(jnp.zeros((SCAN_ROWS, p2), F32) for _ in range(gb * batch)))
        for g in range(gb):
            u, dy, ds, x = u_ref[g], dy_ref[g], ds_ref[g], x_ref[g]
            du_ref[g] = _dot_nt(dy, tm_ref[g], precision=F32_DOT) + _dot_nt(ds, b_ref[g], precision=F32_DOT)
            tm_ref[g] = _dot_tn(u, dy, precision=F32_DOT)
            dk_ref[g] = tm_ref[g, 0:SSM_GROUP, :]
            for s in range(1, SSM_CHUNK):
                dk_ref[g, :, :ch - s * SSM_GROUP] += tm_ref[g, s * SSM_GROUP:(s + 1) * SSM_GROUP, s * SSM_GROUP:]
            db_ref[g] = _dot_tn(u, ds, precision=F32_DOT)
            dc_ref[g] = _dot_tn(x, dy, precision=F32_DOT)
            dla_ref[g] = jnp.sum(ds * x, axis=0, keepdims=True)
            dlb_ref[g] = jnp.sum(ds * pltpu.roll(x, p2 // 2, 1), axis=0, keepdims=True)

    def spec(a, b):
        return pl.BlockSpec((gb, a, b), lambda i: (i, 0, 0))

    def shape(a, b):
        return jax.ShapeDtypeStruct((groups, a, b), F32)

    return pl.pallas_call(
        body, name="s5_bwd", grid=(groups // gb,),
        in_specs=[spec(n, ch), spec(n, ch), spec(n, p2), spec(SSM_GROUP, ch), spec(ch, p2), spec(p2, ch), spec(1, p2),
                  spec(1, p2)],
        out_specs=[spec(n, ch), spec(SSM_GROUP, ch), spec(ch, p2), spec(p2, ch), spec(1, p2), spec(1, p2)],
        out_shape=[shape(n, ch), shape(SSM_GROUP, ch), shape(ch, p2), shape(p2, ch), shape(1, p2), shape(1, p2)],
        scratch_shapes=[pltpu.VMEM((gb, n, p2), F32), pltpu.VMEM((gb, n, p2), F32), pltpu.VMEM((gb, ch, ch), F32)],
        compiler_params=_params("parallel"),
    )(ug, dyg, xin, kt_row, b_mat, c_mat, la, lb)


def _block(ref, axis, j, size):
    start = j * size if isinstance(j, int) else pl.multiple_of(j * size, size)
    return ref.at[pl.ds(start, size), :] if axis == 0 else ref.at[:, pl.ds(start, size)]


def _chip_exchange_copies(mode, axes, srcs, lands, send_sems, recv_sems, local_sems):
    x, y, c = lax.axis_index("x"), lax.axis_index("y"), lax.axis_index("c")
    everyone = mode == "all"
    me = 4 * x + 2 * y + c if everyone else 2 * x + y
    n_peers = _exchange_peers(mode)
    local, sends, arrivals = [], [], []
    for w, axis in enumerate(axes):
        if mode == "gather":
            size = srcs[w].shape[axis]
            local.append(pltpu.make_async_copy(srcs[w], _block(lands[w], axis, me, size), local_sems.at[w]))
        elif mode == "scatter":
            size = srcs[w].shape[axis] // N_CHIPS
            local.append(pltpu.make_async_copy(_block(srcs[w], axis, me, size), lands[w].at[me], local_sems.at[w]))
        else:
            local.append(pltpu.make_async_copy(srcs[w], lands[w].at[me], local_sems.at[w]))
        for k in range(1, n_peers + 1):
            bits = k if everyone else 2 * k
            px = 1 - x if bits & 4 else x
            py = 1 - y if bits & 2 else y
            pc = 1 - c if bits & 1 else c
            peer = 4 * px + 2 * py + pc if everyone else 2 * px + py
            if mode == "gather":
                src, dst, arrive = srcs[w], _block(lands[w], axis, me, size), _block(lands[w], axis, peer, size)
            elif mode == "scatter":
                src, dst, arrive = _block(srcs[w], axis, peer, size), lands[w].at[me], lands[w].at[peer]
            else:
                src, dst, arrive = srcs[w], lands[w].at[me], lands[w].at[peer]
            sem = w * n_peers + k - 1
            for target, out in ((dst, sends), (arrive, arrivals)):
                out.append(pltpu.make_async_remote_copy(
                    src_ref=src, dst_ref=target, send_sem=send_sems.at[sem], recv_sem=recv_sems.at[sem],
                    device_id=(px, py, pc), device_id_type=MESH))
    return local, sends, arrivals


def _exchange_peers(mode):
    return N_DEV - 1 if mode == "all" else N_CHIPS - 1


def _chip_exchange_start(name, mode, items, after=None):
    n = len(items)
    n_after = 0 if after is None else 1
    axes = [axis for _, axis in items]
    hbm = pl.BlockSpec(memory_space=pltpu.HBM)
    sem = pl.BlockSpec(memory_space=pltpu.SEMAPHORE)
    lands = []
    for a, axis in items:
        shape = list(a.shape)
        if mode == "gather":
            shape[axis] *= N_CHIPS
        elif mode == "scatter":
            shape[axis] //= N_CHIPS
            shape = [N_CHIPS] + shape
        else:
            shape = [N_DEV] + shape
        lands.append(pltpu.with_memory_space_constraint(lax.empty(tuple(shape), a.dtype), pltpu.HBM))

    def body(*refs):
        srcs, land_refs = refs[:n], refs[n:2 * n]
        send_sems, recv_sems, local_sems = refs[2 * n + n_after:2 * n + n_after + 3]
        token = refs[-1]
        local, sends, _ = _chip_exchange_copies(mode, axes, srcs, land_refs, send_sems, recv_sems, local_sems)
        for cp in local + sends:
            cp.start()
        token[...] = jnp.zeros_like(token)

    n_sem = n * _exchange_peers(mode)
    outs = pl.pallas_call(
        body, name=name,
        out_shape=(pltpu.SemaphoreType.DMA((n_sem,)), pltpu.SemaphoreType.DMA((n_sem,)), pltpu.SemaphoreType.DMA((n,)),
                   *[pltpu.HBM(a.shape, a.dtype) for a, _ in items], *[pltpu.HBM(l.shape, l.dtype) for l in lands],
                   jax.ShapeDtypeStruct((8, LANES), F32)),
        in_specs=[hbm] * (2 * n) + [pl.BlockSpec(memory_space=pl.ANY)] * n_after,
        out_specs=(sem, sem, sem, *[hbm] * (2 * n), pl.BlockSpec(memory_space=pltpu.VMEM)),
        input_output_aliases={i: 3 + i for i in range(2 * n)},
        compiler_params=pltpu.CompilerParams(has_side_effects=pltpu.SideEffectType.DATAFLOW_SIDE_EFFECTING),
    )(*[pltpu.with_memory_space_constraint(a, pltpu.HBM) for a, _ in items], *lands, *([after] if n_after else []))
    return (mode, axes, outs[:3], outs[3:3 + n], outs[3 + n:3 + 2 * n]), outs[-1][0:1, 0:1]


def _chip_exchange_wait(name, handle, after):
    mode, axes, sems, srcs, lands = handle
    n = len(axes)
    after = list(after) if isinstance(after, (tuple, list)) else [after]
    hbm = pl.BlockSpec(memory_space=pltpu.HBM)
    sem = pl.BlockSpec(memory_space=pltpu.SEMAPHORE)

    def body(*refs):
        src_refs, land_refs = refs[:n], refs[n:2 * n]
        send_sems, recv_sems, local_sems = refs[2 * n:2 * n + 3]
        local, sends, arrivals = _chip_exchange_copies(mode, axes, src_refs, land_refs, send_sems, recv_sems, local_sems)
        for cp in sends:
            cp.wait_send()
        for cp in arrivals:
            cp.wait_recv()
        for cp in local:
            cp.wait()

    outs = pl.pallas_call(
        body, name=name,
        out_shape=(*[pltpu.HBM(a.shape, a.dtype) for a in srcs], *[pltpu.HBM(l.shape, l.dtype) for l in lands]),
        in_specs=[hbm] * (2 * n) + [sem] * 3 + [pl.BlockSpec(memory_space=pl.ANY)] * len(after), out_specs=[hbm] * (2 * n),
        input_output_aliases={i: i for i in range(2 * n)},
        compiler_params=pltpu.CompilerParams(has_side_effects=pltpu.SideEffectType.DATAFLOW_SIDE_EFFECTING),
    )(*srcs, *lands, *sems, *after)
    return outs[n:]


def _sum_slots(name, slots, tm=256):
    n_slots, r, c = slots.shape
    tm = _tile(r, tm)

    def body(*refs):
        acc = refs[0][...]
        for s_ref in refs[1:n_slots]:
            acc = acc + s_ref[...]
        refs[n_slots][...] = acc

    specs = [pl.BlockSpec((None, tm, c), functools.partial(lambda i, s: (s, i, 0), s=s)) for s in range(n_slots)]
    return pl.pallas_call(
        body, name=name, grid=(r // tm,), in_specs=specs, out_specs=pl.BlockSpec((tm, c), lambda i: (i, 0)),
        out_shape=jax.ShapeDtypeStruct((r, c), F32), compiler_params=_params("parallel"),
    )(*[slots] * n_slots)


def _swap_with_sibling(name, arrays):
    n = len(arrays)
    hbm = pl.BlockSpec(memory_space=pl.ANY)

    def body(*refs):
        ins, outs = refs[:n], refs[n:2 * n]
        send_sems, recv_sems = refs[2 * n:]
        sibling = (lax.axis_index("x"), lax.axis_index("y"), 1 - lax.axis_index("c"))
        copies = [pltpu.make_async_remote_copy(src_ref=ins[w], dst_ref=outs[w], send_sem=send_sems.at[w],
                                               recv_sem=recv_sems.at[w], device_id=sibling, device_id_type=MESH)
                  for w in range(n)]
        for cp in copies:
            cp.start()
        for cp in copies:
            cp.wait()

    return pl.pallas_call(
        body, name=name, in_specs=[hbm] * n, out_specs=[hbm] * n,
        out_shape=[jax.ShapeDtypeStruct(a.shape, a.dtype) for a in arrays],
        scratch_shapes=[pltpu.SemaphoreType.DMA((n,)), pltpu.SemaphoreType.DMA((n,))],
    )(*arrays)


def _adamw(g, w, m, v):
    m = ADAM_B1 * m + (1.0 - ADAM_B1) * g
    v = ADAM_B2 * v + (1.0 - ADAM_B2) * jnp.square(g)
    m_hat = m / (1.0 - ADAM_B1 ** ADAM_STEP)
    v_hat = v / (1.0 - ADAM_B2 ** ADAM_STEP)
    delta = -ADAM_LR * (m_hat / (jnp.sqrt(v_hat) + ADAM_EPS) + ADAM_WD * w)
    return delta, m, v


def _adamw_small(grads, ws, ms, vs):
    n = len(ws)

    def whole(a):
        return pl.BlockSpec(a.shape, functools.partial(lambda i, nd: (0,) * nd, nd=a.ndim))

    def body(*refs):
        for i in range(n):
            g, w, m, v = (refs[k * n + i][...] for k in range(4))
            for k, val in enumerate(_adamw(g, w, m, v)):
                refs[(4 + k) * n + i][...] = val

    outs = pl.pallas_call(
        body, name="adamw_small", grid=(1,), in_specs=[whole(a) for a in (*grads, *ws, *ms, *vs)],
        out_specs=[whole(w) for _ in range(3) for w in ws],
        out_shape=[jax.ShapeDtypeStruct(w.shape, F32) for _ in range(3) for w in ws],
        compiler_params=pltpu.CompilerParams(vmem_limit_bytes=VMEM_LIMIT),
    )(*grads, *ws, *ms, *vs)
    return outs[:n], outs[n:2 * n], outs[2 * n:]


def kernel(x, norm1_g, w_in, q_norm_g, k_norm_g, ssm_lambda_re, ssm_lambda_im, ssm_log_dt, ssm_b_re, ssm_b_im, ssm_c_re, ssm_c_im, ssm_d, w_glu, b_glu, attn_out_g, ssm_out_g, w_out, norm2_g, w_mlp_in, w_mlp_out, loss_target, m_norm1_g, m_w_in, m_q_norm_g, m_k_norm_g, m_ssm_lambda_re, m_ssm_lambda_im, m_ssm_log_dt, m_ssm_b_re, m_ssm_b_im, m_ssm_c_re, m_ssm_c_im, m_ssm_d, m_w_glu, m_b_glu, m_attn_out_g, m_ssm_out_g, m_w_out, m_norm2_g, m_w_mlp_in, m_w_mlp_out, v_norm1_g, v_w_in, v_q_norm_g, v_k_norm_g, v_ssm_lambda_re, v_ssm_lambda_im, v_ssm_log_dt, v_ssm_b_re, v_ssm_b_im, v_ssm_c_re, v_ssm_c_im, v_ssm_d, v_w_glu, v_b_glu, v_attn_out_g, v_ssm_out_g, v_w_out, v_norm2_g, v_w_mlp_in, v_w_mlp_out):
    batch, seq, d_model = x.shape
    tokens = batch * seq
    sb_width = w_in.shape[1]
    n_features = d_model

    big = [("w_in", w_in, m_w_in, v_w_in, 1), ("w_glu", w_glu, m_w_glu, v_w_glu, 0),
           ("w_out", w_out, m_w_out, v_w_out, 0), ("w_mlp_in", w_mlp_in, m_w_mlp_in, v_w_mlp_in, 1),
           ("w_mlp_out", w_mlp_out, m_w_mlp_out, v_w_mlp_out, 0)]
    small = [("norm1_g", norm1_g, m_norm1_g, v_norm1_g), ("q_norm_g", q_norm_g, m_q_norm_g, v_q_norm_g),
             ("k_norm_g", k_norm_g, m_k_norm_g, v_k_norm_g),
             ("ssm_lambda_re", ssm_lambda_re, m_ssm_lambda_re, v_ssm_lambda_re),
             ("ssm_lambda_im", ssm_lambda_im, m_ssm_lambda_im, v_ssm_lambda_im),
             ("ssm_log_dt", ssm_log_dt, m_ssm_log_dt, v_ssm_log_dt),
             ("ssm_b_re", ssm_b_re, m_ssm_b_re, v_ssm_b_re), ("ssm_b_im", ssm_b_im, m_ssm_b_im, v_ssm_b_im),
             ("ssm_c_re", ssm_c_re, m_ssm_c_re, v_ssm_c_re), ("ssm_c_im", ssm_c_im, m_ssm_c_im, v_ssm_c_im),
             ("ssm_d", ssm_d, m_ssm_d, v_ssm_d), ("b_glu", b_glu, m_b_glu, v_b_glu),
             ("attn_out_g", attn_out_g, m_attn_out_g, v_attn_out_g), ("ssm_out_g", ssm_out_g, m_ssm_out_g, v_ssm_out_g),
             ("norm2_g", norm2_g, m_norm2_g, v_norm2_g)]

    gather_in, tok_in = _chip_exchange_start("gather_w_in_start", "gather", [(w_in.astype(BF16), 1)])
    gather_mix, tok_mix_w = _chip_exchange_start(
        "gather_mix_start", "gather", [(w.astype(BF16), axis) for _, w, _, _, axis in big[1:3]], after=tok_in)
    gather_mlp, tok_rest = _chip_exchange_start(
        "gather_mlp_start", "gather", [(w.astype(BF16), axis) for _, w, _, _, axis in big[3:]], after=tok_mix_w)

    x2 = x.reshape(tokens, d_model)
    tgt2 = loss_target.reshape(tokens, d_model)
    g1, g2 = norm1_g[None, :], norm2_g[None, :]
    g_attn, g_ssm, bias_glu = attn_out_g[None, :], ssm_out_g[None, :], b_glu[None, :]
    heads = sb_width // HEAD_DIM
    qk_scale = 1.0 / math.sqrt(HEAD_DIM)
    gq, gk = (jnp.tile(q_norm_g, heads) * qk_scale)[None, :], jnp.tile(k_norm_g, heads)[None, :]
    lane_head = jnp.arange(LANES) // HEAD_DIM
    ones_blocks = (lane_head[:, None] == lane_head[None, :]).astype(F32)

    (xn,) = _rowwise("norm1", _rms, [x2], [g1 + tok_rest], [(d_model, BF16)], tm=512)
    s5_in = _s5_operator_inputs(ssm_lambda_re, ssm_lambda_im, ssm_log_dt, ssm_b_re, ssm_b_im, ssm_c_re, ssm_c_im, ssm_d)
    kt_row, b_mat, c_mat, la, lb = _s5_operators_call("s5_operators", s5_in)
    (wf_in,) = _chip_exchange_wait("gather_w_in_wait", gather_in, [xn, b_mat, c_mat])
    def proj_head(acc, gq_, gk_, ones):
        q, k, v = (acc[:, i * sb_width:(i + 1) * sb_width] for i in range(3))
        return acc, _head_rms(q, gq_, ones), _head_rms(k, gk_, ones), v

    proj, qn, kn, vb = _mm("proj_in", xn, wf_in, "nn", epilogue=proj_head, full_rows=True, max_tm=512,
                           extras=[(gq, "row"), (gk, "row"), (ones_blocks, "whole")],
                           out_dtypes=(F32, (BF16, sb_width), (BF16, sb_width), (BF16, sb_width)))
    sb, c_tot = _attn_fwd(qn, kn, vb, batch=batch, seq=seq, bq=ATTN_BQ, bk=ATTN_BK)
    ug = _tokens_to_groups("u_to_groups", proj, 3, sb_width)
    yg, xin = _s5_fwd(ug, kt_row, b_mat, c_mat, la, lb, batch=batch)
    y_ssm = _groups_to_tokens("y_to_tokens", yg)

    wf_glu, wf_out = _chip_exchange_wait("gather_mix_wait", gather_mix, [y_ssm, sb])
    (gate_pre,) = _mm("glu_gate", y_ssm, wf_glu, "nn", a_fn=_gelu, extras=[(bias_glu, "row")],
                      epilogue=lambda acc, b: acc + b)
    (mixed,) = _rowwise("mix_norm", _mixed, [sb, y_ssm, gate_pre], [g_attn, g_ssm], [(2 * sb_width, BF16)], tm=512)
    def out_head(acc, r, g):
        h = acc + r
        return h, _rms(h, g)

    h1, hn = _mm("proj_out", mixed, wf_out, "nn", extras=[(x2, "tile"), (g2, "row")], epilogue=out_head,
                 out_dtypes=(F32, BF16), full_rows=True)
    def square(r):
        r = r.astype(F32)
        return r * r

    wf_mlp_in, wf_mlp_out = _chip_exchange_wait("gather_mlp_wait", gather_mlp, [h1, hn])
    (act_root,) = _mm("mlp_in", hn, wf_mlp_in, "nn", epilogue=lambda acc: jnp.maximum(acc, 0.0), out_dtypes=(BF16,))
    inv_n = 1.0 / n_features

    def loss_head(acc, r, t):
        d = ((acc + r) - t) * inv_n
        return d, d, jnp.sum(d * d, keepdims=True) * (0.5 * n_features)

    dy, dy_b, loss_tiles = _mm("mlp_out_loss", act_root, wf_mlp_out, "nn", a_fn=square, extras=[(h1, "tile"), (tgt2, "tile")],
                               epilogue=loss_head, out_dtypes=(F32, BF16), tile_sums=("scalar",))
    loss_part = jnp.sum(loss_tiles)

    (dw_mlp_out,) = _mm("dw_mlp_out", act_root, dy_b, "tn", a_fn=square)
    (dpre,) = _mm("d_mlp_act", dy_b, wf_mlp_out, "nt", extras=[(act_root, "tile")],
                  epilogue=lambda acc, r: acc * (2.0 * r.astype(F32)), out_dtypes=(BF16,))
    (dw_mlp_in,) = _mm("dw_mlp_in", hn, dpre, "tn")
    scatter_mlp, tok_mlp = _chip_exchange_start("scatter_mlp_start", "scatter", [(dw_mlp_in, 1), (dw_mlp_out, 0)])
    def norm_bwd(dn, res, hx, g):
        _, vjp = jax.vjp(_rms, hx, g)
        dh, dg = vjp(dn)
        return res + dh, dg

    dh1, dg_tiles = _mm("d_norm2_in", dpre, wf_mlp_in, "nt", extras=[(dy, "tile"), (h1, "tile"), (g2 + tok_mlp, "row")],
                        epilogue=norm_bwd, tile_sums=("row",), full_rows=True)
    dg_norm2 = jnp.sum(dg_tiles, axis=0, keepdims=True)
    (dw_out,) = _mm("dw_out", mixed, dh1, "tn")

    def mixed_bwd(dm, sb_, ys, gp, ga, gs):
        _, vjp = jax.vjp(lambda a, act, b, c, d: jnp.concatenate(
            [_rms(a, c), _rms(act * jax.nn.sigmoid(b), d)], axis=-1), sb_, _gelu(ys), gp, ga, gs)
        dsb_, dact, dgp_, dga, dgs = vjp(dm)
        return dsb_, dgp_, dact, dga, dgs, jnp.sum(dgp_, axis=0, keepdims=True)

    dsb, dgate_pre, dact_part, *gain_tiles = _mm(
        "d_mixed", dh1, wf_out, "nt", epilogue=mixed_bwd, full_rows=True, max_tm=512,
        extras=[(sb, "tile"), (y_ssm, "tile"), (gate_pre, "tile"), (g_attn, "row"), (g_ssm, "row")],
        out_dtypes=((F32, sb_width), (BF16, sb_width), (F32, sb_width)), tile_sums=(("row", sb_width),) * 3)
    dg_attn, dg_ssm, db_glu = (jnp.sum(t, axis=0, keepdims=True) for t in gain_tiles)

    def gelu_bwd(acc, part, ys):
        _, vjp = jax.vjp(_gelu, ys)
        return vjp(acc + part)[0]

    (dy_ssm,) = _mm("d_glu_in", dgate_pre, wf_glu, "nt", extras=[(dact_part, "tile"), (y_ssm, "tile")], epilogue=gelu_bwd)
    (dw_glu,) = _mm("dw_glu", y_ssm, dgate_pre, "tn", a_fn=_gelu)
    scatter_mix, tok_mix = _chip_exchange_start("scatter_mix_start", "scatter", [(dw_glu, 0), (dw_out, 0)])

    dug, dkt_row, db_mat, dc_mat, dla, dlb = _s5_bwd(ug, _tokens_to_groups("dy_to_groups", dy_ssm, 0, sb_width), xin,
                                                     kt_row, b_mat, c_mat, la, lb + tok_mix, batch=batch)
    du = _groups_to_tokens("du_to_tokens", dug)
    d_in = _s5_operators_call("s5_operators_bwd", s5_in, (dkt_row, db_mat, dc_mat, dla, dlb))
    ds5 = [d_in[0][:, 0, :] + d_in[2][:, :, 0], d_in[1][:, 0, :] + d_in[3][:, :, 0], d_in[4][:, 0, 0],
           d_in[5].transpose(0, 2, 1), d_in[6].transpose(0, 2, 1), d_in[7].transpose(0, 2, 1), d_in[8].transpose(0, 2, 1),
           d_in[9][:, 0, :]]

    def pack(parts):
        flat = jnp.concatenate([p.reshape(-1) for p in parts])
        rows = -(-flat.shape[0] // (8 * LANES)) * 8
        return jnp.pad(flat, (0, rows * LANES - flat.shape[0])).reshape(rows, LANES)

    def unpack(packed, names):
        flat, out, off = packed.reshape(-1), {}, 0
        for name in names:
            shape = small_shapes[name]
            size = math.prod(shape)
            out[name] = flat[off:off + size].reshape(shape)
            off += size
        return out, flat[off]

    small_shapes = {name: w.shape for name, w, _, _ in small}
    early_names = ["ssm_lambda_re", "ssm_lambda_im", "ssm_log_dt", "ssm_b_re", "ssm_b_im", "ssm_c_re", "ssm_c_im", "ssm_d",
                   "b_glu", "attn_out_g", "ssm_out_g", "norm2_g"]
    late_names = ["norm1_g", "q_norm_g", "k_norm_g"]
    early = pack([*ds5, db_glu[0], dg_attn[0], dg_ssm[0], dg_norm2[0], loss_part])
    early_exchange, _ = _chip_exchange_start("small_early_start", "all", [(early, 0)])

    dqn, dkn, dv = _attn_bwd(qn, kn, vb, c_tot, dsb, batch=batch, seq=seq, bq=ATTN_BQ, bk=ATTN_BK,
                             after=early_exchange[3][0])
    (early_slots,) = _chip_exchange_wait("small_early_wait", early_exchange, dqn)
    small_g, loss = unpack(_sum_slots("sum_small_early", early_slots), early_names)

    def qk_bwd(q, k, dq_, dk_, dv_, du_, gq_, gk_, ones):
        _, vjp_q = jax.vjp(lambda a, g: _head_rms(a, g, ones), q, gq_)
        _, vjp_k = jax.vjp(lambda a, g: _head_rms(a, g, ones), k, gk_)
        dq, dgq = vjp_q(dq_)
        dk, dgk = vjp_k(dk_)
        return jnp.concatenate([dq, dk, dv_, du_], axis=1), dgq, dgk

    dproj, dgq, dgk = _rowwise("qk_norm_bwd", qk_bwd, [(proj, sb_width, 0), (proj, sb_width, 1), dqn, dkn, dv, du],
                               [gq, gk, ones_blocks], [(4 * sb_width, BF16)], [(1, sb_width)] * 2, tm=512)
    (dw_in,) = _mm("dw_in", xn, dproj, "tn")
    scatter_in, tok_w_in = _chip_exchange_start("scatter_in_start", "scatter", [(dw_in, 1)])
    dx, dg_tiles = _mm("d_norm1_in", dproj, wf_in, "nt", extras=[(dh1, "tile"), (x2, "tile"), (g1 + tok_w_in, "row")],
                       epilogue=norm_bwd, tile_sums=("row",), full_rows=True)
    dg_norm1 = jnp.sum(dg_tiles, axis=0, keepdims=True)

    late = pack([dg_norm1[0], dgq.reshape(heads, HEAD_DIM).sum(0) * qk_scale, dgk.reshape(heads, HEAD_DIM).sum(0),
                 jnp.zeros((1,), F32)])
    late_exchange, _ = _chip_exchange_start("small_late_start", "all", [(late, 0)])

    def adam_big(sa, sb_, w, m, v):
        g = sa + sb_
        delta, m, v = _adamw(g, w, m, v)
        return g, delta, m, v

    def reduce_and_update(tag, params, slots):
        mine = [_sum_slots("sum_" + name, s) for s, (name, *_rest) in zip(slots, params)]
        theirs = _swap_with_sibling("swap_" + tag, mine)
        return {name: _rowwise("adamw_" + name, adam_big, [sa, sb_, w, m, v], [], [(w.shape[1], F32)] * 4)
                for (name, w, m, v, _), sa, sb_ in zip(params, mine, theirs)}

    started = late_exchange[3][0]
    slots_mlp_in, slots_mlp_out = _chip_exchange_wait("scatter_mlp_wait", scatter_mlp, started)
    slots_glu, slots_out = _chip_exchange_wait("scatter_mix_wait", scatter_mix, started)
    big_out = reduce_and_update("rest", big[1:], [slots_glu, slots_out, slots_mlp_in, slots_mlp_out])

    (late_slots,) = _chip_exchange_wait("small_late_wait", late_exchange, big_out["w_mlp_out"][3])
    reduced = _sum_slots("sum_small_late", late_slots)
    small_g.update(unpack(reduced, late_names)[0])
    narrow = {name for name, w, _, _ in small if w.ndim == 3 and w.shape[2] < w.shape[1]}

    def flip(a, name):
        return jnp.swapaxes(a, 1, 2) if name in narrow else a

    small_upd = _adamw_small([flip(small_g[name], name) for name, *_ in small], [flip(w, name) for name, w, _, _ in small],
                             [flip(m, name) for name, _, m, _ in small], [flip(v, name) for name, _, _, v in small])
    small_out = [small_g] + [{name: flip(small_upd[kind][i], name) for i, (name, *_) in enumerate(small)}
                             for kind in range(3)]

    (slots_in,) = _chip_exchange_wait("scatter_in_wait", scatter_in, reduced)
    big_out.update(reduce_and_update("w_in", big[:1], [slots_in]))
    names = ["norm1_g", "w_in", "q_norm_g", "k_norm_g", "ssm_lambda_re", "ssm_lambda_im", "ssm_log_dt", "ssm_b_re",
             "ssm_b_im", "ssm_c_re", "ssm_c_im", "ssm_d", "w_glu", "b_glu", "attn_out_g", "ssm_out_g", "w_out",
             "norm2_g", "w_mlp_in", "w_mlp_out"]
    outs = [loss, dx.reshape(batch, seq, d_model)]
    for kind in range(4):
        for name in names:
            outs.append(big_out[name][kind] if name in big_out else small_out[kind][name])
    return tuple(outs)
```
